```python
import jax, jax.numpy as jnp
from jax import lax
import numpy as np

D_MODEL = 1024
BATCH = 8
SEQ = 8192
DEPTH = 1

POOL_WINDOWS = (2, 4, 8, 16)
POOL_GROUPS = 4
POOL_GROUP_DIM = 128
POOL_WIDTH = POOL_GROUPS * POOL_GROUP_DIM
SB_HEADS = 8
SB_HEAD_DIM = 64
SB_WIDTH = SB_HEADS * SB_HEAD_DIM
Q_BLOCK = 128
IN_COLS = POOL_WIDTH + 3 * SB_WIDTH + 2 * D_MODEL
N_EXPERTS = 32
TOP_K = 4
D_EXPERT = D_MODEL
SWIGLU_LIMIT = 7.0
SWIGLU_ALPHA = 1.702
EXPERT_BLOCK = 256
EPS = 1e-6

kernel_name = "hybrid_pool_stickbreaking_moe_block"


def rmsnorm(x, g):
    xf = x.astype(jnp.float32)
    y = xf * lax.rsqrt(jnp.mean(xf * xf, axis=-1, keepdims=True) + EPS)
    return (y * g.astype(jnp.float32)).astype(x.dtype)


def pool_mixer(u, w_pool_grp, pool_scale):
    B, S, _ = u.shape
    uf = u.astype(jnp.float32)
    cs0 = jnp.concatenate([jnp.zeros((B, 1, POOL_WIDTH), jnp.float32), jnp.cumsum(uf, axis=1)], axis=1)
    t = jnp.arange(S)
    pooled = []
    for g, w in enumerate(POOL_WINDOWS):
        c = cs0[..., g * POOL_GROUP_DIM:(g + 1) * POOL_GROUP_DIM]
        upper = c[:, 1:]
        lower = jnp.pad(c, ((0, 0), (w - 1, 0), (0, 0)))[:, :S]
        count = jnp.minimum(t + 1, w).astype(jnp.float32)[None, :, None]
        pooled.append((upper - lower) / count)
    d = (jnp.concatenate(pooled, axis=-1) - uf).reshape(B, S, POOL_GROUPS, POOL_GROUP_DIM)
    mixed = jnp.einsum('bsgc,gcd->bsgd', d, w_pool_grp.astype(jnp.float32)).reshape(B, S, POOL_WIDTH)
    return (mixed * pool_scale.astype(jnp.float32)).astype(u.dtype)


def stick_breaking_attention(q, k, v):
    B, H, S, Dh = q.shape
    n_qb = S // Q_BLOCK
    scale = Dh ** -0.5
    kf = k.astype(jnp.float32)
    vf = v.astype(jnp.float32)
    key_pos = jnp.arange(S)

    def block(i):
        start = i * Q_BLOCK
        qb = lax.dynamic_slice_in_dim(q, start, Q_BLOCK, axis=2).astype(jnp.float32)
        z = jnp.einsum('bhqd,bhkd->bhqk', qb, kf) * scale
        q_pos = start + jnp.arange(Q_BLOCK)
        mask = key_pos[None, :] < q_pos[:, None]
        log_beta = jax.nn.log_sigmoid(z)
        log_1mb = jnp.where(mask, log_beta - z, 0.0)
        rem = lax.cumsum(log_1mb, axis=3, reverse=True) - log_1mb
        a = jnp.where(mask, jnp.exp(log_beta + rem), 0.0)
        return jnp.einsum('bhqk,bhkd->bhqd', a, vf)

    out = lax.map(block, jnp.arange(n_qb))
    out = out.transpose(1, 0, 3, 2, 4).reshape(B, S, H * Dh)
    return out.astype(v.dtype)


def moe_ffn(h, w_router, b_router, w_gate_up, b_gate_up, w_down, b_down):
    B, S, D = h.shape
    N = B * S
    hf = h.reshape(N, D)
    logits = hf.astype(jnp.float32) @ w_router.astype(jnp.float32) + b_router.astype(jnp.float32)
    top_val, top_idx = lax.top_k(logits, TOP_K)
    gate = jax.nn.softmax(top_val, axis=-1)

    n_assign = N * TOP_K
    n_blocks = -(-(n_assign + N_EXPERTS * (EXPERT_BLOCK - 1)) // EXPERT_BLOCK)
    n_rows = n_blocks * EXPERT_BLOCK
    flat_e = top_idx.reshape(-1).astype(jnp.int32)
    flat_tok = jnp.repeat(jnp.arange(N, dtype=jnp.int32), TOP_K)
    flat_gate = gate.reshape(-1)
    order = jnp.argsort(flat_e)
    e_sorted = flat_e[order]
    tok_sorted = flat_tok[order]
    gate_sorted = flat_gate[order]
    counts = jnp.bincount(flat_e, length=N_EXPERTS).astype(jnp.int32)
    padded = (counts + EXPERT_BLOCK - 1) // EXPERT_BLOCK * EXPERT_BLOCK
    start_unp = jnp.cumsum(counts) - counts
    padded_end = jnp.cumsum(padded)
    start_pad = padded_end - padded
    dest = start_pad[e_sorted] + (jnp.arange(n_assign, dtype=jnp.int32) - start_unp[e_sorted])
    row_tok = jnp.full((n_rows,), N, jnp.int32).at[dest].set(tok_sorted)
    row_gate = jnp.zeros((n_rows,), jnp.float32).at[dest].set(gate_sorted)
    block_start = jnp.arange(n_blocks, dtype=jnp.int32) * EXPERT_BLOCK
    block_expert = jnp.minimum(jnp.searchsorted(padded_end, block_start, side='right'), N_EXPERTS - 1).astype(jnp.int32)
    h_pad = jnp.concatenate([hf, jnp.zeros((1, D), hf.dtype)], axis=0)

    def step(acc, blk):
        tok, g, e = blk
        xb = h_pad[tok]
        gu = xb @ w_gate_up[e] + b_gate_up[e]
        glu = jnp.minimum(gu[:, :D_EXPERT], SWIGLU_LIMIT)
        lin = jnp.clip(gu[:, D_EXPERT:], -SWIGLU_LIMIT, SWIGLU_LIMIT)
        act = glu * jax.nn.sigmoid(SWIGLU_ALPHA * glu) * (lin + 1.0)
        y = act @ w_down[e] + b_down[e]
        return acc.at[tok].add((y * g[:, None].astype(y.dtype)).astype(acc.dtype)), None

    acc0 = jnp.zeros((N + 1, D), h.dtype)
    acc, _ = lax.scan(step, acc0, (row_tok.reshape(n_blocks, EXPERT_BLOCK),
                                   row_gate.reshape(n_blocks, EXPERT_BLOCK), block_expert))
    return acc[:N].reshape(B, S, D)


def setup_inputs(seed: int = 0) -> dict:
    key = jax.random.key(seed)
    ks = jax.random.split(key, 20)
    f32 = jnp.float32
    nrm = lambda k, shape, fan_in: jax.random.normal(k, shape, f32) * (fan_in ** -0.5)
    L = DEPTH
    return {
        "x": jax.random.normal(ks[0], (BATCH, SEQ, D_MODEL), f32),
        "norm1_g": 1.0 + 0.05 * jax.random.normal(ks[1], (L, D_MODEL), f32),
        "w_in": nrm(ks[2], (L, D_MODEL, IN_COLS), D_MODEL),
        "q_norm_g": 1.0 + 0.05 * jax.random.normal(ks[3], (L, SB_HEAD_DIM), f32),
        "k_norm_g": 1.0 + 0.05 * jax.random.normal(ks[4], (L, SB_HEAD_DIM), f32),
        "w_pool_grp": nrm(ks[5], (L, POOL_GROUPS, POOL_GROUP_DIM, POOL_GROUP_DIM), POOL_GROUP_DIM),
        "pool_scale": 1.0 + 0.1 * jax.random.normal(ks[6], (L, POOL_WIDTH), f32),
        "w_pool_up": nrm(ks[7], (L, POOL_WIDTH, D_MODEL), POOL_WIDTH),
        "w_attn_up": nrm(ks[8], (L, SB_WIDTH, D_MODEL), SB_WIDTH),
        "w_out": nrm(ks[9], (L, D_MODEL, D_MODEL), D_MODEL),
        "norm2_g": 1.0 + 0.05 * jax.random.normal(ks[10], (L, D_MODEL), f32),
        "w_router": nrm(ks[11], (L, D_MODEL, N_EXPERTS), D_MODEL),
        "b_router": 0.01 * jax.random.normal(ks[12], (L, N_EXPERTS), f32),
        "w_gate_up": nrm(ks[13], (L, N_EXPERTS, D_MODEL, 2 * D_EXPERT), D_MODEL),
        "b_gate_up": 0.01 * jax.random.normal(ks[14], (L, N_EXPERTS, 2 * D_EXPERT), f32),
        "w_down": nrm(ks[15], (L, N_EXPERTS, D_EXPERT, D_MODEL), D_EXPERT),
        "b_down": 0.01 * jax.random.normal(ks[16], (L, N_EXPERTS, D_MODEL), f32),
    }


def reference(x, norm1_g, w_in, q_norm_g, k_norm_g, w_pool_grp, pool_scale, w_pool_up,
              w_attn_up, w_out, norm2_g, w_router, b_router, w_gate_up, b_gate_up,
              w_down, b_down):
    B, S, D = x.shape
    for layer in range(DEPTH):
        h = rmsnorm(x, norm1_g[layer])
        proj = h @ w_in[layer]
        u, q, k, v, g_pool, g_attn = jnp.split(
            proj, [POOL_WIDTH, POOL_WIDTH + SB_WIDTH, POOL_WIDTH + 2 * SB_WIDTH,
                   POOL_WIDTH + 3 * SB_WIDTH, POOL_WIDTH + 3 * SB_WIDTH + D_MODEL], axis=-1)
        q = rmsnorm(q.reshape(B, S, SB_HEADS, SB_HEAD_DIM), q_norm_g[layer]).transpose(0, 2, 1, 3)
        k = rmsnorm(k.reshape(B, S, SB_HEADS, SB_HEAD_DIM), k_norm_g[layer]).transpose(0, 2, 1, 3)
        v = v.reshape(B, S, SB_HEADS, SB_HEAD_DIM).transpose(0, 2, 1, 3)
        pool_out = pool_mixer(u, w_pool_grp[layer], pool_scale[layer]) @ w_pool_up[layer]
        attn_out = stick_breaking_attention(q, k, v) @ w_attn_up[layer]
        merged = jax.nn.sigmoid(g_pool) * pool_out + jax.nn.sigmoid(g_attn) * attn_out
        x = x + merged @ w_out[layer]
        h2 = rmsnorm(x, norm2_g[layer])
        x = x + moe_ffn(h2, w_router[layer], b_router[layer], w_gate_up[layer],
                        b_gate_up[layer], w_down[layer], b_down[layer])
    return x
```

```python
import functools

import jax
import jax.numpy as jnp
from jax import lax
from jax.experimental import pallas as pl
from jax.experimental.pallas import tpu as pltpu

F32 = jnp.float32
BF16 = jnp.bfloat16
U32 = jnp.uint32
I32 = jnp.int32

EPS = 1e-6
POOL_WINDOWS = (2, 4, 8, 16)
POOL_GROUP_DIM = 128
POOL_HALO = 16
SB_HEAD_DIM = 64
TOP_K = 4
SWIGLU_LIMIT = 7.0
SWIGLU_ALPHA = 1.702
EXPERT_BLOCK = 256
LANES = 128
PACK_ROWS = 4
ATTN_BLOCK = 128
ATTN_EXIT = 88.0
VMEM_LIMIT = 56 * 1024 * 1024


def _dot(a, b):
    return jnp.dot(a, b, preferred_element_type=F32)


def _split_bf16(x):
    hi = x.astype(BF16)
    lo = (x - hi.astype(F32)).astype(BF16)
    return hi, lo


def _pack_rows(v, out_ref):
    t, d = v.shape
    half = d // 2
    lo = lax.bitcast_convert_type(v[:, :half].astype(BF16).astype(F32), U32) >> 16
    hi = lax.bitcast_convert_type(v[:, half:].astype(BF16).astype(F32), U32) & jnp.uint32(0xFFFF0000)
    w = lo | hi
    for c in range(PACK_ROWS):
        out_ref[pl.ds(c, t, stride=PACK_ROWS), :] = w[:, c * LANES:(c + 1) * LANES]


def _unpack_rows(ref, t):
    los, his = [], []
    for c in range(PACK_ROWS):
        w = ref[pl.ds(c, t, stride=PACK_ROWS), :]
        los.append(lax.bitcast_convert_type(w << 16, F32))
        his.append(lax.bitcast_convert_type(w & jnp.uint32(0xFFFF0000), F32))
    return jnp.concatenate(los + his, axis=1)


def _mixer_in_kernel(x_ref, g1_ref, win_ref, gq_ref, gk_ref, hsum_ref, wgrp_ref, pscale_ref, wpu_ref,
                     q_ref, k_ref, v_ref, ga_ref, p_ref, tail_ref, *, tm, pw, sw):
    i = pl.program_id(1)

    @pl.when(i == 0)
    def _():
        tail_ref[...] = jnp.zeros_like(tail_ref)

    x = x_ref[...]
    ms = jnp.mean(x * x, axis=-1, keepdims=True)
    h = (x * lax.rsqrt(ms + EPS) * g1_ref[...]).astype(BF16)

    u = _dot(h, win_ref[:, 0:pw])
    xx = jnp.concatenate([tail_ref[...], u], axis=0)
    tail_ref[...] = u[tm - POOL_HALO:, :]
    pos = i * tm + lax.broadcasted_iota(I32, (tm, POOL_GROUP_DIM), 0)
    mixed = []
    for g, w in enumerate(POOL_WINDOWS):
        s = xx[:, g * POOL_GROUP_DIM:(g + 1) * POOL_GROUP_DIM]
        step = 1
        while step < w:
            s = s + pltpu.roll(s, step, axis=0)
            step *= 2
        count = jnp.minimum(pos + 1, w).astype(F32)
        ug = u[:, g * POOL_GROUP_DIM:(g + 1) * POOL_GROUP_DIM]
        d = s[POOL_HALO:, :] / count - ug
        mixed.append(_dot(d.astype(BF16), wgrp_ref[g]))
    pm = jnp.concatenate(mixed, axis=1) * pscale_ref[...]
    pool_out = _dot(pm.astype(BF16), wpu_ref[...])
    d_model = pool_out.shape[1]
    g_pool = _dot(h, win_ref[:, pw + 3 * sw:pw + 3 * sw + d_model])
    p_ref[...] = (jax.nn.sigmoid(g_pool) * pool_out).astype(BF16)
    g_attn = _dot(h, win_ref[:, pw + 3 * sw + d_model:pw + 3 * sw + 2 * d_model])
    ga_ref[...] = jax.nn.sigmoid(g_attn).astype(BF16)

    def head_norm(t, gain):
        hi, lo = _split_bf16(t * t)
        ss = _dot(hi, hsum_ref[...]) + _dot(lo, hsum_ref[...])
        return t * lax.rsqrt(ss * (1.0 / SB_HEAD_DIM) + EPS) * gain

    q = _dot(h, win_ref[:, pw:pw + sw])
    q_ref[...] = (head_norm(q, gq_ref[...]) * (SB_HEAD_DIM ** -0.5)).astype(BF16)
    k = _dot(h, win_ref[:, pw + sw:pw + 2 * sw])
    k_ref[...] = head_norm(k, gk_ref[...]).astype(BF16)
    v_ref[...] = _dot(h, win_ref[:, pw + 2 * sw:pw + 3 * sw]).astype(BF16)


def _attn_kernel(q_ref, k_ref, v_ref, tri_ref, o_ref, acc_ref, r_ref):
    qi = pl.program_id(2)
    bq = ATTN_BLOCK
    lane = lax.broadcasted_iota(I32, (bq, LANES), 1)
    q2 = q_ref[...]
    zero = jnp.zeros_like(q2)
    q_heads = (jnp.where(lane < SB_HEAD_DIM, q2, zero), jnp.where(lane >= SB_HEAD_DIM, q2, zero))
    acc_ref[...] = jnp.zeros_like(acc_ref)
    r_ref[...] = jnp.zeros_like(r_ref)
    row = lax.broadcasted_iota(I32, (bq, bq), 0)
    col = lax.broadcasted_iota(I32, (bq, bq), 1)
    causal = col < row

    def visit(j, diagonal):
        start = pl.multiple_of(j * bq, bq)
        kb = k_ref[pl.ds(start, bq), :]
        vb = v_ref[pl.ds(start, bq), :]
        for hh in range(2):
            z = lax.dot_general(q_heads[hh], kb, (((1,), (1,)), ((), ())), preferred_element_type=F32)
            sp = jnp.maximum(z, 0.0) + jnp.log(1.0 + jnp.exp(-jnp.abs(z)))
            if diagonal:
                sp = jnp.where(causal, sp, 0.0)
            hi, lo = _split_bf16(sp)
            s = _dot(jnp.concatenate([hi, lo], axis=1), tri_ref[...])
            r = r_ref[hh]
            a = jnp.exp(z - (r + s[:, :bq]))
            if diagonal:
                a = jnp.where(causal, a, 0.0)
            acc_ref[hh] += _dot(a.astype(BF16), vb)
            r_ref[hh] = r + s[:, bq:]

    def r_min():
        return jnp.min(jnp.minimum(r_ref[0], r_ref[1]))

    visit(qi, True)

    def cond(c):
        j, rm = c
        return jnp.logical_and(j >= 0, rm < ATTN_EXIT)

    def body(c):
        j, _ = c
        visit(j, False)
        return j - 1, r_min()

    lax.while_loop(cond, body, (qi - 1, r_min()))
    o_ref[...] = jnp.where(lane < SB_HEAD_DIM, acc_ref[0], acc_ref[1]).astype(BF16)


def _mixer_out_kernel(x_ref, p_ref, ga_ref, sba_ref, wau_ref, wout_ref, g2_ref, wr_hi_ref, wr_lo_ref, br_ref,
                      ltri_ref, x1_ref, hp_ref, ri_ref, gate_ref, cnt_ref, *, tm):
    step = pl.program_id(0)

    @pl.when(step == 0)
    def _():
        cnt_ref[...] = jnp.zeros_like(cnt_ref)

    attn_out = _dot(sba_ref[...], wau_ref[...])
    merged = p_ref[...].astype(F32) + ga_ref[...].astype(F32) * attn_out
    x1 = x_ref[...] + _dot(merged.astype(BF16), wout_ref[...])
    x1_ref[...] = x1
    ms = jnp.mean(x1 * x1, axis=-1, keepdims=True)
    h2 = x1 * lax.rsqrt(ms + EPS) * g2_ref[...]
    _pack_rows(h2, hp_ref)

    h_hi, h_lo = _split_bf16(h2)
    logits = (_dot(h_hi, wr_hi_ref[...]) + _dot(h_hi, wr_lo_ref[...]) + _dot(h_lo, wr_hi_ref[...])
              + br_ref[...])
    lane = lax.broadcasted_iota(I32, logits.shape, 1)
    work = logits
    vals, idxs = [], []
    for _ in range(TOP_K):
        m = jnp.max(work, axis=-1, keepdims=True)
        ik = jnp.min(jnp.where(work == m, lane, LANES), axis=-1, keepdims=True)
        vals.append(m)
        idxs.append(ik)
        work = jnp.where(lane == ik, -jnp.inf, work)
    es = [jnp.exp(v - vals[0]) for v in vals]
    denom = es[0] + es[1] + es[2] + es[3]
    hot = jnp.zeros(logits.shape, F32)
    for ik in idxs:
        hot = hot + (lane == ik).astype(F32)
    before = _dot(ltri_ref[...], hot.astype(BF16)) + cnt_ref[0:1, :]
    ri = jnp.zeros(logits.shape, I32)
    gt = jnp.zeros(logits.shape, F32)
    for kk in range(TOP_K):
        rank = jnp.sum(jnp.where(lane == idxs[kk], before, 0.0), axis=-1, keepdims=True).astype(I32)
        ri = jnp.where(lane == kk, idxs[kk], ri)
        ri = jnp.where(lane == TOP_K + kk, rank, ri)
        gt = jnp.where(lane == kk, es[kk] / denom, gt)
    ri_ref[...] = ri
    gate_ref[...] = gt
    cnt_ref[...] = cnt_ref[...] + jnp.sum(hot, axis=0, keepdims=True)


def _row_copy(src, src_row, dst, dst_row, sem):
    return pltpu.make_async_copy(src.at[pl.ds(pl.multiple_of(src_row * PACK_ROWS, PACK_ROWS), PACK_ROWS)],
                                 dst.at[pl.ds(pl.multiple_of(dst_row * PACK_ROWS, PACK_ROWS), PACK_ROWS)], sem)


def _dispatch_kernel(dest_ref, h_ref, xs_in_ref, xs_ref, sem, *, tm):
    del xs_in_ref

    def start(r, c):
        for kk in range(TOP_K):
            _row_copy(h_ref, r, xs_ref, dest_ref[0, 0, r * TOP_K + kk], sem).start()
        return c

    lax.fori_loop(0, tm, start, 0)

    def wait(r, c):
        for kk in range(TOP_K):
            _row_copy(h_ref, r, xs_ref, dest_ref[0, 0, r * TOP_K + kk], sem).wait()
        return c

    lax.fori_loop(0, tm, wait, 0)


def _experts_kernel(be_ref, nb_ref, xs_ref, wgu_ref, bgu_ref, wd_ref, bd_ref, ys_ref):
    del be_ref
    blk = pl.program_id(0)
    de = wd_ref.shape[1]

    @pl.when(blk < nb_ref[0])
    def _():
        x = _unpack_rows(xs_ref, EXPERT_BLOCK).astype(BF16)
        gu = _dot(x, wgu_ref[0]) + bgu_ref[0]
        glu = jnp.minimum(gu[:, :de], SWIGLU_LIMIT)
        lin = jnp.clip(gu[:, de:], -SWIGLU_LIMIT, SWIGLU_LIMIT)
        act = glu * jax.nn.sigmoid(SWIGLU_ALPHA * glu) * (lin + 1.0)
        y = _dot(act.astype(BF16), wd_ref[0]) + bd_ref[0]
        _pack_rows(y, ys_ref)

    @pl.when(blk >= nb_ref[0])
    def _():
        ys_ref[...] = jnp.zeros_like(ys_ref)


def _combine_kernel(dest_ref, x1_ref, gate_ref, ys_ref, o_ref, buf_ref, sem, *, tm):
    def start(r, c):
        for kk in range(TOP_K):
            _row_copy(ys_ref, dest_ref[0, 0, r * TOP_K + kk], buf_ref.at[kk], r, sem).start()
        return c

    lax.fori_loop(0, tm, start, 0)

    def wait(r, c):
        for kk in range(TOP_K):
            _row_copy(ys_ref, dest_ref[0, 0, r * TOP_K + kk], buf_ref.at[kk], r, sem).wait()
        return c

    lax.fori_loop(0, tm, wait, 0)
    gate = gate_ref[...]
    out = x1_ref[...]
    for kk in range(TOP_K):
        out = out + gate[:, kk:kk + 1] * _unpack_rows(buf_ref.at[kk], tm)
    o_ref[...] = out


def _const_spec(shape):
    nd = len(shape)
    return pl.BlockSpec(shape, lambda *_: (0,) * nd)


def _layer(x, norm1_g, w_in, q_norm_g, k_norm_g, w_pool_grp, pool_scale, w_pool_up, w_attn_up, w_out, norm2_g,
           w_router, b_router, w_gate_up, b_gate_up, w_down, b_down):
    B, S, D = x.shape
    N = B * S
    pw = w_pool_up.shape[0]
    sw = w_attn_up.shape[0]
    n_exp = w_router.shape[1]
    de = w_down.shape[1]
    heads = sw // SB_HEAD_DIM
    assert pw == len(POOL_WINDOWS) * POOL_GROUP_DIM and heads % 2 == 0 and n_exp <= LANES
    assert D == 2 * PACK_ROWS * LANES and w_in.shape[1] == pw + 3 * sw + 2 * D
    tm = 512 if S % 512 == 0 else 256
    assert S % tm == 0 and S % ATTN_BLOCK == 0
    xf = x.reshape(N, D)
    cparams = functools.partial(pltpu.CompilerParams, vmem_limit_bytes=VMEM_LIMIT)

    hsum = (jnp.arange(sw)[:, None] // SB_HEAD_DIM == jnp.arange(sw)[None, :] // SB_HEAD_DIM).astype(BF16)
    nt = S // tm
    tok_spec = lambda w: pl.BlockSpec((tm, w), lambda b, i: (b * nt + i, 0))
    q2, k2, v2, ga, pg = pl.pallas_call(
        functools.partial(_mixer_in_kernel, tm=tm, pw=pw, sw=sw),
        grid=(B, nt),
        in_specs=[tok_spec(D), _const_spec((1, D)), _const_spec((D, w_in.shape[1])), _const_spec((1, sw)),
                  _const_spec((1, sw)), _const_spec((sw, sw)),
                  _const_spec((len(POOL_WINDOWS), POOL_GROUP_DIM, POOL_GROUP_DIM)), _const_spec((1, pw)),
                  _const_spec((pw, D))],
        out_specs=[tok_spec(sw), tok_spec(sw), tok_spec(sw), tok_spec(D), tok_spec(D)],
        out_shape=[jax.ShapeDtypeStruct((N, sw), BF16)] * 3 + [jax.ShapeDtypeStruct((N, D), BF16)] * 2,
        scratch_shapes=[pltpu.VMEM((POOL_HALO, pw), F32)],
        compiler_params=cparams(dimension_semantics=("arbitrary", "arbitrary")),
        name="mixer_in",
    )(xf, norm1_g.reshape(1, D), w_in.astype(BF16), jnp.tile(q_norm_g, heads).reshape(1, sw),
      jnp.tile(k_norm_g, heads).reshape(1, sw), hsum, w_pool_grp.astype(BF16), pool_scale.reshape(1, pw),
      w_pool_up.astype(BF16))

    bq = ATTN_BLOCK
    nq = S // bq
    jj = jnp.arange(bq)
    tri_half = jnp.concatenate([(jj[:, None] >= jj[None, :]).astype(BF16), jnp.ones((bq, bq), BF16)], axis=1)
    tri = jnp.concatenate([tri_half, tri_half], axis=0)
    sba = pl.pallas_call(
        _attn_kernel,
        grid=(B, heads // 2, nq),
        in_specs=[pl.BlockSpec((bq, LANES), lambda b, hp, qi: (b * nq + qi, hp)),
                  pl.BlockSpec((S, LANES), lambda b, hp, qi: (b, hp)),
                  pl.BlockSpec((S, LANES), lambda b, hp, qi: (b, hp)),
                  _const_spec((2 * bq, 2 * bq))],
        out_specs=pl.BlockSpec((bq, LANES), lambda b, hp, qi: (b * nq + qi, hp)),
        out_shape=jax.ShapeDtypeStruct((N, sw), BF16),
        scratch_shapes=[pltpu.VMEM((2, bq, LANES), F32), pltpu.VMEM((2, bq, LANES), F32)],
        compiler_params=cparams(dimension_semantics=("arbitrary", "arbitrary", "arbitrary")),
        name="sb_attn",
    )(q2, k2, v2, tri)

    wr = jnp.zeros((D, LANES), F32).at[:, :n_exp].set(w_router)
    wr_hi = wr.astype(BF16)
    wr_lo = (wr - wr_hi.astype(F32)).astype(BF16)
    br = jnp.full((1, LANES), -jnp.inf, F32).at[0, :n_exp].set(b_router)
    ltri = (jnp.arange(tm)[:, None] > jnp.arange(tm)[None, :]).astype(BF16)
    row_spec = lambda w: pl.BlockSpec((tm, w), lambda i: (i, 0))
    x1, hpk, ri, gate, cnt = pl.pallas_call(
        functools.partial(_mixer_out_kernel, tm=tm),
        grid=(N // tm,),
        in_specs=[row_spec(D), row_spec(D), row_spec(D), row_spec(sw), _const_spec((sw, D)), _const_spec((D, D)),
                  _const_spec((1, D)), _const_spec((D, LANES)), _const_spec((D, LANES)), _const_spec((1, LANES)),
                  _const_spec((tm, tm))],
        out_specs=[row_spec(D), pl.BlockSpec((tm * PACK_ROWS, LANES), lambda i: (i, 0)), row_spec(LANES),
                   row_spec(LANES), _const_spec((8, LANES))],
        out_shape=[jax.ShapeDtypeStruct((N, D), F32), jax.ShapeDtypeStruct((N * PACK_ROWS, LANES), U32),
                   jax.ShapeDtypeStruct((N, LANES), I32), jax.ShapeDtypeStruct((N, LANES), F32),
                   jax.ShapeDtypeStruct((8, LANES), F32)],
        compiler_params=cparams(dimension_semantics=("arbitrary",)),
        name="mixer_out",
    )(xf, pg, ga, sba, w_attn_up.astype(BF16), w_out.astype(BF16), norm2_g.reshape(1, D), wr_hi, wr_lo, br, ltri)

    n_assign = N * TOP_K
    n_blocks = -(-(n_assign + n_exp * (EXPERT_BLOCK - 1)) // EXPERT_BLOCK)
    counts = cnt[0, :n_exp].astype(I32)
    padded = (counts + EXPERT_BLOCK - 1) // EXPERT_BLOCK * EXPERT_BLOCK
    padded_end = jnp.cumsum(padded)
    start_pad = padded_end - padded
    idx = ri[:, :TOP_K]
    rank = ri[:, TOP_K:2 * TOP_K]
    onehot = idx[:, :, None] == jnp.arange(n_exp, dtype=I32)[None, None, :]
    dest = jnp.sum(jnp.where(onehot, start_pad[None, None, :], 0), axis=-1) + rank
    block_start = jnp.arange(n_blocks, dtype=I32) * EXPERT_BLOCK
    block_expert = jnp.minimum(jnp.searchsorted(padded_end, block_start, side="right"), n_exp - 1).astype(I32)
    n_used = (padded_end[-1] // EXPERT_BLOCK).astype(I32).reshape(1)

    td = 256
    dest_blocks = dest.reshape(N // td, 1, td * TOP_K)
    dest_spec = pl.BlockSpec((1, 1, td * TOP_K), lambda i: (i, 0, 0), memory_space=pltpu.SMEM)
    n_rows = n_blocks * EXPERT_BLOCK
    xs = pl.pallas_call(
        functools.partial(_dispatch_kernel, tm=td),
        grid=(N // td,),
        in_specs=[dest_spec, pl.BlockSpec((td * PACK_ROWS, LANES), lambda i: (i, 0)),
                  pl.BlockSpec(memory_space=pl.ANY)],
        out_specs=pl.BlockSpec(memory_space=pl.ANY),
        out_shape=jax.ShapeDtypeStruct((n_rows * PACK_ROWS, LANES), U32),
        scratch_shapes=[pltpu.SemaphoreType.DMA(())],
        input_output_aliases={2: 0},
        compiler_params=cparams(dimension_semantics=("arbitrary",)),
        name="dispatch",
    )(dest_blocks, hpk, jnp.zeros((n_rows * PACK_ROWS, LANES), U32))

    ys = pl.pallas_call(
        _experts_kernel,
        grid_spec=pltpu.PrefetchScalarGridSpec(
            num_scalar_prefetch=2,
            grid=(n_blocks,),
            in_specs=[pl.BlockSpec((EXPERT_BLOCK * PACK_ROWS, LANES), lambda i, be, nb: (i, 0)),
                      pl.BlockSpec((1, D, 2 * de), lambda i, be, nb: (be[i], 0, 0)),
                      pl.BlockSpec((1, 1, 2 * de), lambda i, be, nb: (be[i], 0, 0)),
                      pl.BlockSpec((1, de, D), lambda i, be, nb: (be[i], 0, 0)),
                      pl.BlockSpec((1, 1, D), lambda i, be, nb: (be[i], 0, 0))],
            out_specs=pl.BlockSpec((EXPERT_BLOCK * PACK_ROWS, LANES), lambda i, be, nb: (i, 0)),
        ),
        out_shape=jax.ShapeDtypeStruct((n_rows * PACK_ROWS, LANES), U32),
        compiler_params=cparams(dimension_semantics=("arbitrary",)),
        name="experts",
    )(block_expert, n_used, xs, w_gate_up.astype(BF16), b_gate_up.reshape(n_exp, 1, 2 * de), w_down.astype(BF16),
      b_down.reshape(n_exp, 1, D))

    out = pl.pallas_call(
        functools.partial(_combine_kernel, tm=td),
        grid=(N // td,),
        in_specs=[dest_spec, pl.BlockSpec((td, D), lambda i: (i, 0)), pl.BlockSpec((td, LANES), lambda i: (i, 0)),
                  pl.BlockSpec(memory_space=pl.ANY)],
        out_specs=pl.BlockSpec((td, D), lambda i: (i, 0)),
        out_shape=jax.ShapeDtypeStruct((N, D), F32),
        scratch_shapes=[pltpu.VMEM((TOP_K, td * PACK_ROWS, LANES), U32), pltpu.SemaphoreType.DMA(())],
        compiler_params=cparams(dimension_semantics=("arbitrary",)),
        name="combine",
    )(dest_blocks, x1, gate, ys)
    return out.reshape(B, S, D)


def kernel(x, norm1_g, w_in, q_norm_g, k_norm_g, w_pool_grp, pool_scale, w_pool_up, w_attn_up, w_out, norm2_g,
           w_router, b_router, w_gate_up, b_gate_up, w_down, b_down):
    for layer in range(norm1_g.shape[0]):
        x = _layer(x, norm1_g[layer], w_in[layer], q_norm_g[layer], k_norm_g[layer], w_pool_grp[layer],
                   pool_scale[layer], w_pool_up[layer], w_attn_up[layer], w_out[layer], norm2_g[layer],
                   w_router[layer], b_router[layer], w_gate_up[layer], b_gate_up[layer], w_down[layer],
                   b_down[layer])
    return x
```

```python
import functools

import jax
import jax.numpy as jnp
from jax import lax
from jax.experimental import pallas as pl
from jax.experimental.pallas import tpu as pltpu

F32 = jnp.float32
BF16 = jnp.bfloat16
U32 = jnp.uint32
I32 = jnp.int32

EPS = 1e-6
POOL_WINDOWS = (2, 4, 8, 16)
POOL_GROUP_DIM = 128
POOL_HALO = 16
SB_HEAD_DIM = 64
TOP_K = 4
SWIGLU_LIMIT = 7.0
SWIGLU_ALPHA = 1.702
EXPERT_BLOCK = 256
LANES = 128
PACK_ROWS = 4
ATTN_BLOCK = 128
ATTN_EXIT = 48.0
VMEM_LIMIT = 56 * 1024 * 1024


def _dot(a, b):
    return jnp.dot(a, b, preferred_element_type=F32)


def _split_bf16(x):
    hi = x.astype(BF16)
    lo = (x - hi.astype(F32)).astype(BF16)
    return hi, lo


def _pack_rows(v, out_ref):
    t, d = v.shape
    half = d // 2
    lo = lax.bitcast_convert_type(v[:, :half].astype(BF16).astype(F32), U32) >> 16
    hi = lax.bitcast_convert_type(v[:, half:].astype(BF16).astype(F32), U32) & jnp.uint32(0xFFFF0000)
    w = lo | hi
    for c in range(PACK_ROWS):
        out_ref[pl.ds(c, t, stride=PACK_ROWS), :] = w[:, c * LANES:(c + 1) * LANES]


def _unpack_rows(ref, t):
    los, his = [], []
    for c in range(PACK_ROWS):
        w = ref[pl.ds(c, t, stride=PACK_ROWS), :]
        los.append(lax.bitcast_convert_type(w << 16, F32))
        his.append(lax.bitcast_convert_type(w & jnp.uint32(0xFFFF0000), F32))
    return jnp.concatenate(los + his, axis=1)


def _mixer_in_kernel(x_ref, g1_ref, win_ref, gq_ref, gk_ref, hsum_ref, wgrp_ref, pscale_ref, wpu_ref,
                     q_ref, k_ref, v_ref, ga_ref, p_ref, tail_ref, *, tm, pw, sw):
    i = pl.program_id(1)

    @pl.when(i == 0)
    def _():
        tail_ref[...] = jnp.zeros_like(tail_ref)

    x = x_ref[...]
    ms = jnp.mean(x * x, axis=-1, keepdims=True)
    h = (x * lax.rsqrt(ms + EPS) * g1_ref[...]).astype(BF16)

    u = _dot(h, win_ref[:, 0:pw])
    xx = jnp.concatenate([tail_ref[...], u], axis=0)
    tail_ref[...] = u[tm - POOL_HALO:, :]
    pos = i * tm + lax.broadcasted_iota(I32, (tm, POOL_GROUP_DIM), 0)
    mixed = []
    for g, w in enumerate(POOL_WINDOWS):
        s = xx[:, g * POOL_GROUP_DIM:(g + 1) * POOL_GROUP_DIM]
        step = 1
        while step < w:
            s = s + pltpu.roll(s, step, axis=0)
            step *= 2
        count = jnp.minimum(pos + 1, w).astype(F32)
        ug = u[:, g * POOL_GROUP_DIM:(g + 1) * POOL_GROUP_DIM]
        d = s[POOL_HALO:, :] / count - ug
        mixed.append(_dot(d.astype(BF16), wgrp_ref[g]))
    pm = jnp.concatenate(mixed, axis=1) * pscale_ref[...]
    pool_out = _dot(pm.astype(BF16), wpu_ref[...])
    d_model = pool_out.shape[1]
    g_pool = _dot(h, win_ref[:, pw + 3 * sw:pw + 3 * sw + d_model])
    p_ref[...] = (jax.nn.sigmoid(g_pool) * pool_out).astype(BF16)
    g_attn = _dot(h, win_ref[:, pw + 3 * sw + d_model:pw + 3 * sw + 2 * d_model])
    ga_ref[...] = jax.nn.sigmoid(g_attn).astype(BF16)

    def head_norm(t, gain):
        hi, lo = _split_bf16(t * t)
        ss = _dot(hi, hsum_ref[...]) + _dot(lo, hsum_ref[...])
        return t * lax.rsqrt(ss * (1.0 / SB_HEAD_DIM) + EPS) * gain

    q = _dot(h, win_ref[:, pw:pw + sw])
    q_ref[...] = (head_norm(q, gq_ref[...]) * (SB_HEAD_DIM ** -0.5)).astype(BF16)
    k = _dot(h, win_ref[:, pw + sw:pw + 2 * sw])
    k_ref[...] = head_norm(k, gk_ref[...]).astype(BF16)
    v_ref[...] = _dot(h, win_ref[:, pw + 2 * sw:pw + 3 * sw]).astype(BF16)


def _attn_kernel(q_ref, k_ref, v_ref, tri_ref, o_ref, *scratch, n_pairs):
    qs, acc, rr = scratch[:n_pairs], scratch[n_pairs:2 * n_pairs], scratch[2 * n_pairs:]
    qi = pl.program_id(1)
    bq = ATTN_BLOCK
    first_head = lax.broadcasted_iota(I32, (bq, LANES), 1) < SB_HEAD_DIM
    for p in range(n_pairs):
        q2 = q_ref[:, p * LANES:(p + 1) * LANES]
        qs[p][:bq] = jnp.where(first_head, q2, jnp.zeros_like(q2))
        qs[p][bq:] = jnp.where(first_head, jnp.zeros_like(q2), q2)
    row = lax.broadcasted_iota(I32, (2 * bq, bq), 0)
    col = lax.broadcasted_iota(I32, (2 * bq, bq), 1)
    causal = col < (row & (bq - 1))
    contract_last = (((1,), (1,)), ((), ()))

    def softplus(z):
        return jnp.maximum(z, 0.0) + jnp.log(1.0 + jnp.exp(-jnp.abs(z)))

    def suffix_sums(sp):
        hi, lo = _split_bf16(sp)
        return _dot(jnp.concatenate([hi, lo], axis=1), tri_ref[...])

    def cols(ref, start, p):
        return ref[pl.ds(start, bq), p * LANES:(p + 1) * LANES]

    start_d = pl.multiple_of(qi * bq, bq)
    start_n = pl.multiple_of(jnp.maximum(qi - 1, 0) * bq, bq)
    has_n = qi >= 1
    z_d = [lax.dot_general(qs[p][...], cols(k_ref, start_d, p), contract_last, preferred_element_type=F32)
           for p in range(n_pairs)]
    z_n = [lax.dot_general(qs[p][...], cols(k_ref, start_n, p), contract_last, preferred_element_type=F32)
           for p in range(n_pairs)]
    s_d = [suffix_sums(jnp.where(causal, softplus(z), 0.0)) for z in z_d]
    s_n = [suffix_sums(jnp.where(has_n, softplus(z), 0.0)) for z in z_n]
    for p in range(n_pairs):
        a_d = jnp.where(causal, jnp.exp(z_d[p] - s_d[p][:, :bq]), 0.0)
        r_d = s_d[p][:, bq:]
        a_n = jnp.where(has_n, jnp.exp(z_n[p] - (r_d + s_n[p][:, :bq])), 0.0)
        acc[p][...] = (_dot(a_d.astype(BF16), cols(v_ref, start_d, p)) + _dot(a_n.astype(BF16), cols(v_ref, start_n, p)))
        rr[p][...] = r_d + s_n[p][:, bq:]

    def r_min():
        m = rr[0][...]
        for p in range(1, n_pairs):
            m = jnp.minimum(m, rr[p][...])
        return jnp.min(m)

    def cond(c):
        j, rm = c
        return jnp.logical_and(j >= 0, rm < ATTN_EXIT)

    def body(c):
        j, _ = c
        start = pl.multiple_of(j * bq, bq)
        zs = [lax.dot_general(qs[p][...], cols(k_ref, start, p), contract_last, preferred_element_type=F32)
              for p in range(n_pairs)]
        ss = [suffix_sums(softplus(z)) for z in zs]
        for p in range(n_pairs):
            r = rr[p][...]
            a = jnp.exp(zs[p] - (r + ss[p][:, :bq]))
            acc[p][...] += _dot(a.astype(BF16), cols(v_ref, start, p))
            rr[p][...] = r + ss[p][:, bq:]
        return j - 1, r_min()

    lax.while_loop(cond, body, (qi - 2, r_min()))
    for p in range(n_pairs):
        o_ref[:, p * LANES:(p + 1) * LANES] = jnp.where(first_head, acc[p][:bq], acc[p][bq:]).astype(BF16)


def _mixer_out_kernel(x_ref, p_ref, ga_ref, sba_ref, wau_ref, wout_ref, g2_ref, wr_hi_ref, wr_lo_ref, br_ref,
                      ltri_ref, x1_ref, hp_ref, ri_ref, gate_ref, cnt_ref, *, tm):
    step = pl.program_id(0)

    @pl.when(step == 0)
    def _():
        cnt_ref[...] = jnp.zeros_like(cnt_ref)

    attn_out = _dot(sba_ref[...], wau_ref[...])
    merged = p_ref[...].astype(F32) + ga_ref[...].astype(F32) * attn_out
    x1 = x_ref[...] + _dot(merged.astype(BF16), wout_ref[...])
    x1_ref[...] = x1
    ms = jnp.mean(x1 * x1, axis=-1, keepdims=True)
    h2 = x1 * lax.rsqrt(ms + EPS) * g2_ref[...]
    _pack_rows(h2, hp_ref)

    h_hi, h_lo = _split_bf16(h2)
    logits = (_dot(h_hi, wr_hi_ref[...]) + _dot(h_hi, wr_lo_ref[...]) + _dot(h_lo, wr_hi_ref[...])
              + br_ref[...])
    lane = lax.broadcasted_iota(I32, logits.shape, 1)
    work = logits
    vals, idxs = [], []
    for _ in range(TOP_K):
        m = jnp.max(work, axis=-1, keepdims=True)
        ik = jnp.min(jnp.where(work == m, lane, LANES), axis=-1, keepdims=True)
        vals.append(m)
        idxs.append(ik)
        work = jnp.where(lane == ik, -jnp.inf, work)
    es = [jnp.exp(v - vals[0]) for v in vals]
    denom = es[0] + es[1] + es[2] + es[3]
    hot = jnp.zeros(logits.shape, F32)
    for ik in idxs:
        hot = hot + (lane == ik).astype(F32)
    before = _dot(ltri_ref[...], hot.astype(BF16)) + cnt_ref[0:1, :]
    ri = jnp.zeros(logits.shape, I32)
    gt = jnp.zeros(logits.shape, F32)
    for kk in range(TOP_K):
        rank = jnp.sum(jnp.where(lane == idxs[kk], before, 0.0), axis=-1, keepdims=True).astype(I32)
        ri = jnp.where(lane == kk, idxs[kk], ri)
        ri = jnp.where(lane == TOP_K + kk, rank, ri)
        gt = jnp.where(lane == kk, es[kk] / denom, gt)
    ri_ref[...] = ri
    gate_ref[...] = gt
    cnt_ref[...] = cnt_ref[...] + jnp.sum(hot, axis=0, keepdims=True)


def _row_copy(src, src_row, dst, dst_row, sem):
    return pltpu.make_async_copy(src.at[pl.ds(pl.multiple_of(src_row * PACK_ROWS, PACK_ROWS), PACK_ROWS)],
                                 dst.at[pl.ds(pl.multiple_of(dst_row * PACK_ROWS, PACK_ROWS), PACK_ROWS)], sem)


def _dispatch_kernel(dest_ref, h_ref, xs_in_ref, xs_ref, sem, *, tm):
    del xs_in_ref

    def start(r, c):
        for kk in range(TOP_K):
            _row_copy(h_ref, r, xs_ref, dest_ref[0, 0, r * TOP_K + kk], sem).start(priority=kk % 2)
        return c

    lax.fori_loop(0, tm, start, 0)

    def wait(r, c):
        for kk in range(TOP_K):
            _row_copy(h_ref, r, xs_ref, dest_ref[0, 0, r * TOP_K + kk], sem).wait()
        return c

    lax.fori_loop(0, tm, wait, 0)


def _experts_kernel(be_ref, nb_ref, xs_ref, wgu_ref, bgu_ref, wd_ref, bd_ref, ys_ref, wgu_bf_ref, wd_bf_ref):
    blk = pl.program_id(0)
    de = wd_ref.shape[1]

    @pl.when(jnp.logical_or(blk == 0, be_ref[blk] != be_ref[jnp.maximum(blk - 1, 0)]))
    def _():
        wgu_bf_ref[...] = wgu_ref[0].astype(BF16)
        wd_bf_ref[...] = wd_ref[0].astype(BF16)

    @pl.when(blk < nb_ref[0])
    def _():
        x = _unpack_rows(xs_ref, EXPERT_BLOCK).astype(BF16)
        gu = _dot(x, wgu_bf_ref[...]) + bgu_ref[0]
        glu = jnp.minimum(gu[:, :de], SWIGLU_LIMIT)
        lin = jnp.clip(gu[:, de:], -SWIGLU_LIMIT, SWIGLU_LIMIT)
        act = glu * jax.nn.sigmoid(SWIGLU_ALPHA * glu) * (lin + 1.0)
        y = _dot(act.astype(BF16), wd_bf_ref[...]) + bd_ref[0]
        _pack_rows(y, ys_ref)

    @pl.when(blk >= nb_ref[0])
    def _():
        ys_ref[...] = jnp.zeros_like(ys_ref)


def _combine_kernel(dest_ref, x1_ref, gate_ref, ys_ref, o_ref, buf_ref, sem, *, tm):
    def start(r, c):
        for kk in range(TOP_K):
            _row_copy(ys_ref, dest_ref[0, 0, r * TOP_K + kk], buf_ref.at[kk], r, sem).start(priority=kk % 2)
        return c

    lax.fori_loop(0, tm, start, 0)

    def wait(r, c):
        for kk in range(TOP_K):
            _row_copy(ys_ref, dest_ref[0, 0, r * TOP_K + kk], buf_ref.at[kk], r, sem).wait()
        return c

    lax.fori_loop(0, tm, wait, 0)
    gate = gate_ref[...]
    out = x1_ref[...]
    for kk in range(TOP_K):
        out = out + gate[:, kk:kk + 1] * _unpack_rows(buf_ref.at[kk], tm)
    o_ref[...] = out


def _const_spec(shape):
    nd = len(shape)
    return pl.BlockSpec(shape, lambda *_: (0,) * nd)


def _layer(x, norm1_g, w_in, q_norm_g, k_norm_g, w_pool_grp, pool_scale, w_pool_up, w_attn_up, w_out, norm2_g,
           w_router, b_router, w_gate_up, b_gate_up, w_down, b_down):
    B, S, D = x.shape
    N = B * S
    pw = w_pool_up.shape[0]
    sw = w_attn_up.shape[0]
    n_exp = w_router.shape[1]
    de = w_down.shape[1]
    heads = sw // SB_HEAD_DIM
    assert pw == len(POOL_WINDOWS) * POOL_GROUP_DIM and heads % 2 == 0 and n_exp <= LANES
    assert D == 2 * PACK_ROWS * LANES and w_in.shape[1] == pw + 3 * sw + 2 * D
    tm = 512 if S % 512 == 0 else 256
    assert S % tm == 0 and S % ATTN_BLOCK == 0
    xf = x.reshape(N, D)
    cparams = functools.partial(pltpu.CompilerParams, vmem_limit_bytes=VMEM_LIMIT)

    hsum = (jnp.arange(sw)[:, None] // SB_HEAD_DIM == jnp.arange(sw)[None, :] // SB_HEAD_DIM).astype(BF16)
    nt = S // tm
    tok_spec = lambda w: pl.BlockSpec((tm, w), lambda b, i: (b * nt + i, 0))
    q2, k2, v2, ga, pg = pl.pallas_call(
        functools.partial(_mixer_in_kernel, tm=tm, pw=pw, sw=sw),
        grid=(B, nt),
        in_specs=[tok_spec(D), _const_spec((1, D)), _const_spec((D, w_in.shape[1])), _const_spec((1, sw)),
                  _const_spec((1, sw)), _const_spec((sw, sw)),
                  _const_spec((len(POOL_WINDOWS), POOL_GROUP_DIM, POOL_GROUP_DIM)), _const_spec((1, pw)),
                  _const_spec((pw, D))],
        out_specs=[tok_spec(sw), tok_spec(sw), tok_spec(sw), tok_spec(D), tok_spec(D)],
        out_shape=[jax.ShapeDtypeStruct((N, sw), BF16)] * 3 + [jax.ShapeDtypeStruct((N, D), BF16)] * 2,
        scratch_shapes=[pltpu.VMEM((POOL_HALO, pw), F32)],
        compiler_params=cparams(dimension_semantics=("arbitrary", "arbitrary")),
        name="mixer_in",
    )(xf, norm1_g.reshape(1, D), w_in.astype(BF16), jnp.tile(q_norm_g, heads).reshape(1, sw),
      jnp.tile(k_norm_g, heads).reshape(1, sw), hsum, w_pool_grp.astype(BF16), pool_scale.reshape(1, pw),
      w_pool_up.astype(BF16))

    bq = ATTN_BLOCK
    nq = S // bq
    jj = jnp.arange(bq)
    tri_half = jnp.concatenate([(jj[:, None] >= jj[None, :]).astype(BF16), jnp.ones((bq, bq), BF16)], axis=1)
    tri = jnp.concatenate([tri_half, tri_half], axis=0)
    n_pairs = heads // 2
    kv_spec = pl.BlockSpec((S, sw), lambda b, qi: (b, 0), pipeline_mode=pl.Buffered(1))
    sba = pl.pallas_call(
        functools.partial(_attn_kernel, n_pairs=n_pairs),
        grid=(B, nq),
        in_specs=[pl.BlockSpec((bq, sw), lambda b, qi: (b * nq + qi, 0)), kv_spec, kv_spec,
                  _const_spec((2 * bq, 2 * bq))],
        out_specs=pl.BlockSpec((bq, sw), lambda b, qi: (b * nq + qi, 0)),
        out_shape=jax.ShapeDtypeStruct((N, sw), BF16),
        scratch_shapes=([pltpu.VMEM((2 * bq, LANES), BF16)] * n_pairs + [pltpu.VMEM((2 * bq, LANES), F32)] * (2 * n_pairs)),
        compiler_params=cparams(dimension_semantics=("arbitrary", "arbitrary")),
        name="sb_attn",
    )(q2, k2, v2, tri)

    wr = jnp.zeros((D, LANES), F32).at[:, :n_exp].set(w_router)
    wr_hi = wr.astype(BF16)
    wr_lo = (wr - wr_hi.astype(F32)).astype(BF16)
    br = jnp.full((1, LANES), -jnp.inf, F32).at[0, :n_exp].set(b_router)
    ltri = (jnp.arange(tm)[:, None] > jnp.arange(tm)[None, :]).astype(BF16)
    row_spec = lambda w: pl.BlockSpec((tm, w), lambda i: (i, 0))
    x1, hpk, ri, gate, cnt = pl.pallas_call(
        functools.partial(_mixer_out_kernel, tm=tm),
        grid=(N // tm,),
        in_specs=[row_spec(D), row_spec(D), row_spec(D), row_spec(sw), _const_spec((sw, D)), _const_spec((D, D)),
                  _const_spec((1, D)), _const_spec((D, LANES)), _const_spec((D, LANES)), _const_spec((1, LANES)),
                  _const_spec((tm, tm))],
        out_specs=[row_spec(D), pl.BlockSpec((tm * PACK_ROWS, LANES), lambda i: (i, 0)), row_spec(LANES),
                   row_spec(LANES), _const_spec((8, LANES))],
        out_shape=[jax.ShapeDtypeStruct((N, D), F32), jax.ShapeDtypeStruct((N * PACK_ROWS, LANES), U32),
                   jax.ShapeDtypeStruct((N, LANES), I32), jax.ShapeDtypeStruct((N, LANES), F32),
                   jax.ShapeDtypeStruct((8, LANES), F32)],
        compiler_params=cparams(dimension_semantics=("arbitrary",)),
        name="mixer_out",
    )(xf, pg, ga, sba, w_attn_up.astype(BF16), w_out.astype(BF16), norm2_g.reshape(1, D), wr_hi, wr_lo, br, ltri)

    n_assign = N * TOP_K
    n_blocks = -(-(n_assign + n_exp * (EXPERT_BLOCK - 1)) // EXPERT_BLOCK)
    counts = cnt[0, :n_exp].astype(I32)
    padded = (counts + EXPERT_BLOCK - 1) // EXPERT_BLOCK * EXPERT_BLOCK
    padded_end = jnp.cumsum(padded)
    start_pad = padded_end - padded
    idx = ri[:, :TOP_K]
    rank = ri[:, TOP_K:2 * TOP_K]
    onehot = idx[:, :, None] == jnp.arange(n_exp, dtype=I32)[None, None, :]
    dest = jnp.sum(jnp.where(onehot, start_pad[None, None, :], 0), axis=-1) + rank
    block_start = jnp.arange(n_blocks, dtype=I32) * EXPERT_BLOCK
    block_expert = jnp.minimum(jnp.sum((padded_end[None, :] <= block_start[:, None]).astype(I32), axis=1), n_exp - 1)
    n_used = (padded_end[-1] // EXPERT_BLOCK).astype(I32).reshape(1)

    td = 256
    dest_blocks = dest.reshape(N // td, 1, td * TOP_K)
    dest_spec = pl.BlockSpec((1, 1, td * TOP_K), lambda i: (i, 0, 0), memory_space=pltpu.SMEM)
    n_rows = n_blocks * EXPERT_BLOCK
    xs = pl.pallas_call(
        functools.partial(_dispatch_kernel, tm=td),
        grid=(N // td,),
        in_specs=[dest_spec, pl.BlockSpec((td * PACK_ROWS, LANES), lambda i: (i, 0)),
                  pl.BlockSpec(memory_space=pl.ANY)],
        out_specs=pl.BlockSpec(memory_space=pl.ANY),
        out_shape=jax.ShapeDtypeStruct((n_rows * PACK_ROWS, LANES), U32),
        scratch_shapes=[pltpu.SemaphoreType.DMA(())],
        input_output_aliases={2: 0},
        compiler_params=cparams(dimension_semantics=("arbitrary",)),
        name="dispatch",
    )(dest_blocks, hpk, jnp.zeros((n_rows * PACK_ROWS, LANES), U32))

    ys = pl.pallas_call(
        _experts_kernel,
        grid_spec=pltpu.PrefetchScalarGridSpec(
            num_scalar_prefetch=2,
            grid=(n_blocks,),
            in_specs=[pl.BlockSpec((EXPERT_BLOCK * PACK_ROWS, LANES), lambda i, be, nb: (i, 0)),
                      pl.BlockSpec((1, D, 2 * de), lambda i, be, nb: (be[i], 0, 0)),
                      pl.BlockSpec((1, 1, 2 * de), lambda i, be, nb: (be[i], 0, 0)),
                      pl.BlockSpec((1, de, D), lambda i, be, nb: (be[i], 0, 0)),
                      pl.BlockSpec((1, 1, D), lambda i, be, nb: (be[i], 0, 0))],
            out_specs=pl.BlockSpec((EXPERT_BLOCK * PACK_ROWS, LANES), lambda i, be, nb: (i, 0)),
            scratch_shapes=[pltpu.VMEM((D, 2 * de), BF16), pltpu.VMEM((de, D), BF16)],
        ),
        out_shape=jax.ShapeDtypeStruct((n_rows * PACK_ROWS, LANES), U32),
        compiler_params=cparams(dimension_semantics=("arbitrary",)),
        name="experts",
    )(block_expert, n_used, xs, w_gate_up, b_gate_up.reshape(n_exp, 1, 2 * de), w_down, b_down.reshape(n_exp, 1, D))

    out = pl.pallas_call(
        functools.partial(_combine_kernel, tm=td),
        grid=(N // td,),
        in_specs=[dest_spec, pl.BlockSpec((td, D), lambda i: (i, 0)), pl.BlockSpec((td, LANES), lambda i: (i, 0)),
                  pl.BlockSpec(memory_space=pl.ANY)],
        out_specs=pl.BlockSpec((td, D), lambda i: (i, 0)),
        out_shape=jax.ShapeDtypeStruct((N, D), F32),
        scratch_shapes=[pltpu.VMEM((TOP_K, td * PACK_ROWS, LANES), U32), pltpu.SemaphoreType.DMA(())],
        compiler_params=cparams(dimension_semantics=("arbitrary",)),
        name="combine",
    )(dest_blocks, x1, gate, ys)
    return out.reshape(B, S, D)


def kernel(x, norm1_g, w_in, q_norm_g, k_norm_g, w_pool_grp, pool_scale, w_pool_up, w_attn_up, w_out, norm2_g,
           w_router, b_router, w_gate_up, b_gate_up, w_down, b_down):
    for layer in range(norm1_g.shape[0]):
        x = _layer(x, norm1_g[layer], w_in[layer], q_norm_g[layer], k_norm_g[layer], w_pool_grp[layer],
                   pool_scale[layer], w_pool_up[layer], w_attn_up[layer], w_out[layer], norm2_g[layer],
                   w_router[layer], b_router[layer], w_gate_up[layer], b_gate_up[layer], w_down[layer],
                   b_down[layer])
    return x
```

```python
import functools

import jax
import jax.numpy as jnp
from jax import lax
from jax.experimental import pallas as pl
from jax.experimental.pallas import tpu as pltpu

F32 = jnp.float32
BF16 = jnp.bfloat16
U32 = jnp.uint32
I32 = jnp.int32

EPS = 1e-6
POOL_WINDOWS = (2, 4, 8, 16)
POOL_GROUP_DIM = 128
POOL_HALO = 16
SB_HEAD_DIM = 64
TOP_K = 4
SWIGLU_LIMIT = 7.0
SWIGLU_ALPHA = 1.702
EXPERT_BLOCK = 256
LANES = 128
PACK_ROWS = 4
ATTN_BLOCK = 128
ATTN_EXIT = 48.0
VMEM_LIMIT = 56 * 1024 * 1024


def _dot(a, b):
    return jnp.dot(a, b, preferred_element_type=F32)


def _split_bf16(x):
    hi = x.astype(BF16)
    lo = (x - hi.astype(F32)).astype(BF16)
    return hi, lo


def _pack_rows(v, out_ref):
    t, d = v.shape
    half = d // 2
    lo = lax.bitcast_convert_type(v[:, :half].astype(BF16).astype(F32), U32) >> 16
    hi = lax.bitcast_convert_type(v[:, half:].astype(BF16).astype(F32), U32) & jnp.uint32(0xFFFF0000)
    w = lo | hi
    for c in range(PACK_ROWS):
        out_ref[pl.ds(c, t, stride=PACK_ROWS), :] = w[:, c * LANES:(c + 1) * LANES]


def _unpack_rows(ref, t):
    los, his = [], []
    for c in range(PACK_ROWS):
        w = ref[pl.ds(c, t, stride=PACK_ROWS), :]
        los.append(lax.bitcast_convert_type(w << 16, F32))
        his.append(lax.bitcast_convert_type(w & jnp.uint32(0xFFFF0000), F32))
    return jnp.concatenate(los + his, axis=1)


def _mixer_in_kernel(x_ref, g1_ref, win_ref, gq_ref, gk_ref, hsum_ref, wgrp_ref, pscale_ref, wpu_ref,
                     q_ref, k_ref, v_ref, ga_ref, p_ref, tail_ref, *, tm, pw, sw):
    i = pl.program_id(1)

    @pl.when(i == 0)
    def _():
        tail_ref[...] = jnp.zeros_like(tail_ref)

    x = x_ref[...]
    ms = jnp.mean(x * x, axis=-1, keepdims=True)
    h = (x * lax.rsqrt(ms + EPS) * g1_ref[...]).astype(BF16)

    u = _dot(h, win_ref[:, 0:pw])
    xx = jnp.concatenate([tail_ref[...], u], axis=0)
    tail_ref[...] = u[tm - POOL_HALO:, :]
    pos = i * tm + lax.broadcasted_iota(I32, (tm, POOL_GROUP_DIM), 0)
    mixed = []
    for g, w in enumerate(POOL_WINDOWS):
        s = xx[:, g * POOL_GROUP_DIM:(g + 1) * POOL_GROUP_DIM]
        step = 1
        while step < w:
            s = s + pltpu.roll(s, step, axis=0)
            step *= 2
        count = jnp.minimum(pos + 1, w).astype(F32)
        ug = u[:, g * POOL_GROUP_DIM:(g + 1) * POOL_GROUP_DIM]
        d = s[POOL_HALO:, :] / count - ug
        mixed.append(_dot(d.astype(BF16), wgrp_ref[g]))
    pm = jnp.concatenate(mixed, axis=1) * pscale_ref[...]
    pool_out = _dot(pm.astype(BF16), wpu_ref[...])
    d_model = pool_out.shape[1]
    g_pool = _dot(h, win_ref[:, pw + 3 * sw:pw + 3 * sw + d_model])
    p_ref[...] = (jax.nn.sigmoid(g_pool) * pool_out).astype(BF16)
    g_attn = _dot(h, win_ref[:, pw + 3 * sw + d_model:pw + 3 * sw + 2 * d_model])
    ga_ref[...] = jax.nn.sigmoid(g_attn).astype(BF16)

    def head_norm(t, gain):
        hi, lo = _split_bf16(t * t)
        ss = _dot(hi, hsum_ref[...]) + _dot(lo, hsum_ref[...])
        return t * lax.rsqrt(ss * (1.0 / SB_HEAD_DIM) + EPS) * gain

    q = _dot(h, win_ref[:, pw:pw + sw])
    q_ref[...] = (head_norm(q, gq_ref[...]) * (SB_HEAD_DIM ** -0.5)).astype(BF16)
    k = _dot(h, win_ref[:, pw + sw:pw + 2 * sw])
    k_ref[...] = head_norm(k, gk_ref[...]).astype(BF16)
    v_ref[...] = _dot(h, win_ref[:, pw + 2 * sw:pw + 3 * sw]).astype(BF16)


def _attn_kernel(q_ref, k_ref, v_ref, tri_ref, o_ref, *scratch, n_pairs):
    qs, acc, rr = scratch[:n_pairs], scratch[n_pairs:2 * n_pairs], scratch[2 * n_pairs:]
    qi = pl.program_id(1)
    bq = ATTN_BLOCK
    first_head = lax.broadcasted_iota(I32, (bq, LANES), 1) < SB_HEAD_DIM
    for p in range(n_pairs):
        q2 = q_ref[:, p * LANES:(p + 1) * LANES]
        qs[p][:bq] = jnp.where(first_head, q2, jnp.zeros_like(q2))
        qs[p][bq:] = jnp.where(first_head, jnp.zeros_like(q2), q2)
    row = lax.broadcasted_iota(I32, (2 * bq, bq), 0)
    col = lax.broadcasted_iota(I32, (2 * bq, bq), 1)
    causal = col < (row & (bq - 1))
    contract_last = (((1,), (1,)), ((), ()))

    def softplus(z):
        return jnp.maximum(z, 0.0) + jnp.log(1.0 + jnp.exp(-jnp.abs(z)))

    def suffix_sums(sp):
        hi, lo = _split_bf16(sp)
        return _dot(jnp.concatenate([hi, lo], axis=1), tri_ref[...])

    def cols(ref, start, p):
        return ref[pl.ds(start, bq), p * LANES:(p + 1) * LANES]

    start_d = pl.multiple_of(qi * bq, bq)
    start_n = pl.multiple_of(jnp.maximum(qi - 1, 0) * bq, bq)
    has_n = qi >= 1
    z_d = [lax.dot_general(qs[p][...], cols(k_ref, start_d, p), contract_last, preferred_element_type=F32)
           for p in range(n_pairs)]
    z_n = [lax.dot_general(qs[p][...], cols(k_ref, start_n, p), contract_last, preferred_element_type=F32)
           for p in range(n_pairs)]
    s_d = [suffix_sums(jnp.where(causal, softplus(z), 0.0)) for z in z_d]
    s_n = [suffix_sums(jnp.where(has_n, softplus(z), 0.0)) for z in z_n]
    for p in range(n_pairs):
        a_d = jnp.where(causal, jnp.exp(z_d[p] - s_d[p][:, :bq]), 0.0)
        r_d = s_d[p][:, bq:]
        a_n = jnp.where(has_n, jnp.exp(z_n[p] - (r_d + s_n[p][:, :bq])), 0.0)
        acc[p][...] = (_dot(a_d.astype(BF16), cols(v_ref, start_d, p)) + _dot(a_n.astype(BF16), cols(v_ref, start_n, p)))
        rr[p][...] = r_d + s_n[p][:, bq:]

    def r_min():
        m = rr[0][...]
        for p in range(1, n_pairs):
            m = jnp.minimum(m, rr[p][...])
        return jnp.min(m)

    def cond(c):
        j, rm = c
        return jnp.logical_and(j >= 0, rm < ATTN_EXIT)

    def body(c):
        j, _ = c
        start = pl.multiple_of(j * bq, bq)
        zs = [lax.dot_general(qs[p][...], cols(k_ref, start, p), contract_last, preferred_element_type=F32)
              for p in range(n_pairs)]
        ss = [suffix_sums(softplus(z)) for z in zs]
        for p in range(n_pairs):
            r = rr[p][...]
            a = jnp.exp(zs[p] - (r + ss[p][:, :bq]))
            acc[p][...] += _dot(a.astype(BF16), cols(v_ref, start, p))
            rr[p][...] = r + ss[p][:, bq:]
        return j - 1, r_min()

    lax.while_loop(cond, body, (qi - 2, r_min()))
    for p in range(n_pairs):
        o_ref[:, p * LANES:(p + 1) * LANES] = jnp.where(first_head, acc[p][:bq], acc[p][bq:]).astype(BF16)


def _mixer_out_kernel(x_ref, p_ref, ga_ref, sba_ref, wau_ref, wout_ref, g2_ref, wr_hi_ref, wr_lo_ref, br_ref,
                      ltri_ref, x1_ref, hp_ref, ri_ref, gate_ref, cnt_ref, *, tm):
    step = pl.program_id(0)

    @pl.when(step == 0)
    def _():
        cnt_ref[...] = jnp.zeros_like(cnt_ref)

    attn_out = _dot(sba_ref[...], wau_ref[...])
    merged = p_ref[...].astype(F32) + ga_ref[...].astype(F32) * attn_out
    x1 = x_ref[...] + _dot(merged.astype(BF16), wout_ref[...])
    x1_ref[...] = x1
    ms = jnp.mean(x1 * x1, axis=-1, keepdims=True)
    h2 = x1 * lax.rsqrt(ms + EPS) * g2_ref[...]
    _pack_rows(h2, hp_ref)

    h_hi, h_lo = _split_bf16(h2)
    logits = (_dot(h_hi, wr_hi_ref[...]) + _dot(h_hi, wr_lo_ref[...]) + _dot(h_lo, wr_hi_ref[...])
              + br_ref[...])
    lane = lax.broadcasted_iota(I32, logits.shape, 1)
    work = logits
    vals, idxs = [], []
    for _ in range(TOP_K):
        m = jnp.max(work, axis=-1, keepdims=True)
        ik = jnp.min(jnp.where(work == m, lane, LANES), axis=-1, keepdims=True)
        vals.append(m)
        idxs.append(ik)
        work = jnp.where(lane == ik, -jnp.inf, work)
    es = [jnp.exp(v - vals[0]) for v in vals]
    denom = es[0] + es[1] + es[2] + es[3]
    hot = jnp.zeros(logits.shape, F32)
    for ik in idxs:
        hot = hot + (lane == ik).astype(F32)
    before = _dot(ltri_ref[...], hot.astype(BF16)) + cnt_ref[0:1, :]
    ri = jnp.zeros(logits.shape, I32)
    gt = jnp.zeros(logits.shape, F32)
    for kk in range(TOP_K):
        rank = jnp.sum(jnp.where(lane == idxs[kk], before, 0.0), axis=-1, keepdims=True).astype(I32)
        ri = jnp.where(lane == kk, idxs[kk], ri)
        ri = jnp.where(lane == TOP_K + kk, rank, ri)
        gt = jnp.where(lane == kk, es[kk] / denom, gt)
    ri_ref[...] = ri
    gate_ref[...] = gt
    cnt_ref[...] = cnt_ref[...] + jnp.sum(hot, axis=0, keepdims=True)


def _row_copy(src, src_row, dst, dst_row, sem):
    return pltpu.make_async_copy(src.at[pl.ds(pl.multiple_of(src_row * PACK_ROWS, PACK_ROWS), PACK_ROWS)],
                                 dst.at[pl.ds(pl.multiple_of(dst_row * PACK_ROWS, PACK_ROWS), PACK_ROWS)], sem)


def _experts_kernel(be_ref, nb_ref, src_ref, dst_ref, h_ref, wgu_ref, bgu_ref, wd_ref, bd_ref, y_ref,
                    xbuf, obuf, wgu_bf_ref, wd_bf_ref, gsem, ssem, *, dummy_row):
    s = pl.program_id(0)
    nb = nb_ref[0]
    de = wd_ref.shape[1]
    rows = EXPERT_BLOCK
    even = s % 2
    odd = 1 - even
    n_chunk = 4
    cw = de // n_chunk
    per_chunk = rows // n_chunk

    def gather_copy(r):
        return _row_copy(h_ref, src_ref[0, 0, r], xbuf.at[even], r, gsem.at[even])

    def scatter_copy(r):
        return _row_copy(obuf.at[even], r, y_ref, dst_ref[0, 0, r], ssem.at[even])

    def wait_gather(slot):
        pltpu.make_async_copy(h_ref.at[pl.ds(0, rows * PACK_ROWS)], xbuf.at[slot], gsem.at[slot]).wait()

    def wait_scatter(slot):
        pltpu.make_async_copy(obuf.at[slot], y_ref.at[pl.ds(0, rows * PACK_ROWS)], ssem.at[slot]).wait()

    def mlp(between):
        x = _unpack_rows(xbuf.at[odd], rows).astype(BF16)
        y = None
        for c in range(n_chunk):
            between(c)
            lo, hi = c * cw, (c + 1) * cw
            g = _dot(x, wgu_bf_ref[:, lo:hi]) + bgu_ref[0][:, lo:hi]
            lin = _dot(x, wgu_bf_ref[:, de + lo:de + hi]) + bgu_ref[0][:, de + lo:de + hi]
            glu = jnp.minimum(g, SWIGLU_LIMIT)
            lin = jnp.clip(lin, -SWIGLU_LIMIT, SWIGLU_LIMIT)
            act = glu * jax.nn.sigmoid(SWIGLU_ALPHA * glu) * (lin + 1.0)
            part = _dot(act.astype(BF16), wd_bf_ref[lo:hi, :])
            y = part if y is None else y + part
        _pack_rows(y + bd_ref[0], obuf.at[odd])

    blk = jnp.clip(s - 1, 0, be_ref.shape[0] - 1)
    fresh = jnp.logical_or(s == 1, be_ref[blk] != be_ref[jnp.maximum(blk - 1, 0)])

    @pl.when(jnp.logical_and(jnp.logical_and(s >= 1, s <= nb), fresh))
    def _():
        wgu_bf_ref[...] = wgu_ref[0].astype(BF16)
        wd_bf_ref[...] = wd_ref[0].astype(BF16)

    steady = jnp.logical_and(s >= 3, s < nb)

    @pl.when(steady)
    def _():
        wait_scatter(odd)
        wait_gather(odd)

        def starts(c):
            for r in range(c * per_chunk, (c + 1) * per_chunk):
                gather_copy(r).start(priority=0)
                scatter_copy(r).start(priority=1)

        mlp(starts)

    @pl.when(jnp.logical_not(steady))
    def _():
        @pl.when(s == 0)
        def _():
            obuf[...] = jnp.zeros_like(obuf)
            for b in range(2):
                fill = pltpu.make_async_copy(
                    obuf.at[b], y_ref.at[pl.ds((dummy_row + b * rows) * PACK_ROWS, rows * PACK_ROWS)], ssem.at[b])
                fill.start()
                fill.wait()

        @pl.when(jnp.logical_and(s >= 3, s - 3 < nb - 1))
        def _():
            wait_scatter(odd)

        @pl.when(s < nb)
        def _():
            lax.fori_loop(0, rows, lambda r, c: (gather_copy(r).start(), c)[1], 0)

        @pl.when(jnp.logical_and(s >= 1, s <= nb))
        def _():
            wait_gather(odd)
            mlp(lambda c: None)

        @pl.when(jnp.logical_and(s >= 2, s - 2 < nb))
        def _():
            lax.fori_loop(0, rows, lambda r, c: (scatter_copy(r).start(), c)[1], 0)

        @pl.when(s == nb + 1)
        def _():
            wait_scatter(even)


def _combine_kernel(x1_ref, gate_ref, y0_ref, y1_ref, y2_ref, y3_ref, o_ref, *, tm):
    gate = gate_ref[...]
    out = x1_ref[...]
    for kk, y_ref in enumerate((y0_ref, y1_ref, y2_ref, y3_ref)):
        out = out + gate[:, kk:kk + 1] * _unpack_rows(y_ref, tm)
    o_ref[...] = out


def _const_spec(shape):
    nd = len(shape)
    return pl.BlockSpec(shape, lambda *_: (0,) * nd)


def _layer(x, norm1_g, w_in, q_norm_g, k_norm_g, w_pool_grp, pool_scale, w_pool_up, w_attn_up, w_out, norm2_g,
           w_router, b_router, w_gate_up, b_gate_up, w_down, b_down):
    B, S, D = x.shape
    N = B * S
    pw = w_pool_up.shape[0]
    sw = w_attn_up.shape[0]
    n_exp = w_router.shape[1]
    de = w_down.shape[1]
    heads = sw // SB_HEAD_DIM
    assert pw == len(POOL_WINDOWS) * POOL_GROUP_DIM and heads % 2 == 0 and n_exp <= LANES
    assert D == 2 * PACK_ROWS * LANES and w_in.shape[1] == pw + 3 * sw + 2 * D
    tm = 512 if S % 512 == 0 else 256
    assert S % tm == 0 and S % ATTN_BLOCK == 0
    xf = x.reshape(N, D)
    cparams = functools.partial(pltpu.CompilerParams, vmem_limit_bytes=VMEM_LIMIT)

    hsum = (jnp.arange(sw)[:, None] // SB_HEAD_DIM == jnp.arange(sw)[None, :] // SB_HEAD_DIM).astype(BF16)
    nt = S // tm
    tok_spec = lambda w: pl.BlockSpec((tm, w), lambda b, i: (b * nt + i, 0))
    q2, k2, v2, ga, pg = pl.pallas_call(
        functools.partial(_mixer_in_kernel, tm=tm, pw=pw, sw=sw),
        grid=(B, nt),
        in_specs=[tok_spec(D), _const_spec((1, D)), _const_spec((D, w_in.shape[1])), _const_spec((1, sw)),
                  _const_spec((1, sw)), _const_spec((sw, sw)),
                  _const_spec((len(POOL_WINDOWS), POOL_GROUP_DIM, POOL_GROUP_DIM)), _const_spec((1, pw)),
                  _const_spec((pw, D))],
        out_specs=[tok_spec(sw), tok_spec(sw), tok_spec(sw), tok_spec(D), tok_spec(D)],
        out_shape=[jax.ShapeDtypeStruct((N, sw), BF16)] * 3 + [jax.ShapeDtypeStruct((N, D), BF16)] * 2,
        scratch_shapes=[pltpu.VMEM((POOL_HALO, pw), F32)],
        compiler_params=cparams(dimension_semantics=("arbitrary", "arbitrary")),
        name="mixer_in",
    )(xf, norm1_g.reshape(1, D), w_in.astype(BF16), jnp.tile(q_norm_g, heads).reshape(1, sw),
      jnp.tile(k_norm_g, heads).reshape(1, sw), hsum, w_pool_grp.astype(BF16), pool_scale.reshape(1, pw),
      w_pool_up.astype(BF16))

    bq = ATTN_BLOCK
    nq = S // bq
    jj = jnp.arange(bq)
    tri_half = jnp.concatenate([(jj[:, None] >= jj[None, :]).astype(BF16), jnp.ones((bq, bq), BF16)], axis=1)
    tri = jnp.concatenate([tri_half, tri_half], axis=0)
    n_pairs = heads // 2
    kv_spec = pl.BlockSpec((S, sw), lambda b, qi: (b, 0), pipeline_mode=pl.Buffered(1))
    sba = pl.pallas_call(
        functools.partial(_attn_kernel, n_pairs=n_pairs),
        grid=(B, nq),
        in_specs=[pl.BlockSpec((bq, sw), lambda b, qi: (b * nq + qi, 0)), kv_spec, kv_spec,
                  _const_spec((2 * bq, 2 * bq))],
        out_specs=pl.BlockSpec((bq, sw), lambda b, qi: (b * nq + qi, 0)),
        out_shape=jax.ShapeDtypeStruct((N, sw), BF16),
        scratch_shapes=([pltpu.VMEM((2 * bq, LANES), BF16)] * n_pairs + [pltpu.VMEM((2 * bq, LANES), F32)] * (2 * n_pairs)),
        compiler_params=cparams(dimension_semantics=("arbitrary", "arbitrary")),
        name="sb_attn",
    )(q2, k2, v2, tri)

    wr = jnp.zeros((D, LANES), F32).at[:, :n_exp].set(w_router)
    wr_hi = wr.astype(BF16)
    wr_lo = (wr - wr_hi.astype(F32)).astype(BF16)
    br = jnp.full((1, LANES), -jnp.inf, F32).at[0, :n_exp].set(b_router)
    ltri = (jnp.arange(tm)[:, None] > jnp.arange(tm)[None, :]).astype(BF16)
    row_spec = lambda w: pl.BlockSpec((tm, w), lambda i: (i, 0))
    x1, hpk, ri, gate, cnt = pl.pallas_call(
        functools.partial(_mixer_out_kernel, tm=tm),
        grid=(N // tm,),
        in_specs=[row_spec(D), row_spec(D), row_spec(D), row_spec(sw), _const_spec((sw, D)), _const_spec((D, D)),
                  _const_spec((1, D)), _const_spec((D, LANES)), _const_spec((D, LANES)), _const_spec((1, LANES)),
                  _const_spec((tm, tm))],
        out_specs=[row_spec(D), pl.BlockSpec((tm * PACK_ROWS, LANES), lambda i: (i, 0)), row_spec(LANES),
                   row_spec(LANES), _const_spec((8, LANES))],
        out_shape=[jax.ShapeDtypeStruct((N, D), F32), jax.ShapeDtypeStruct((N * PACK_ROWS, LANES), U32),
                   jax.ShapeDtypeStruct((N, LANES), I32), jax.ShapeDtypeStruct((N, LANES), F32),
                   jax.ShapeDtypeStruct((8, LANES), F32)],
        compiler_params=cparams(dimension_semantics=("arbitrary",)),
        name="mixer_out",
    )(xf, pg, ga, sba, w_attn_up.astype(BF16), w_out.astype(BF16), norm2_g.reshape(1, D), wr_hi, wr_lo, br, ltri)

    n_assign = N * TOP_K
    n_blocks = -(-(n_assign + n_exp * (EXPERT_BLOCK - 1)) // EXPERT_BLOCK)
    counts = cnt[0, :n_exp].astype(I32)
    padded = (counts + EXPERT_BLOCK - 1) // EXPERT_BLOCK * EXPERT_BLOCK
    padded_end = jnp.cumsum(padded)
    start_pad = padded_end - padded
    idx = ri[:, :TOP_K]
    rank = ri[:, TOP_K:2 * TOP_K]
    onehot = idx[:, :, None] == jnp.arange(n_exp, dtype=I32)[None, None, :]
    dest = jnp.sum(jnp.where(onehot, start_pad[None, None, :], 0), axis=-1) + rank
    block_start = jnp.arange(n_blocks, dtype=I32) * EXPERT_BLOCK
    block_expert = jnp.minimum(jnp.sum((padded_end[None, :] <= block_start[:, None]).astype(I32), axis=1), n_exp - 1)
    n_used = (padded_end[-1] // EXPERT_BLOCK).astype(I32).reshape(1)

    n_rows = n_blocks * EXPERT_BLOCK
    flat = jnp.arange(n_assign, dtype=I32)
    out_row = (flat % TOP_K) * N + flat // TOP_K
    row_out = jnp.full((n_rows,), -1, I32).at[dest.reshape(-1)].set(out_row, unique_indices=True)
    is_pad = row_out < 0
    dummy_row = n_assign
    row_dst = jnp.where(is_pad, dummy_row + jnp.arange(n_rows, dtype=I32) % (2 * EXPERT_BLOCK), row_out)
    row_src = jnp.where(is_pad, 0, row_out % N)

    last = n_blocks - 1
    tbl_spec = lambda off: pl.BlockSpec((1, 1, EXPERT_BLOCK), lambda s, be, nb: (jnp.clip(s - off, 0, last), 0, 0),
                                        memory_space=pltpu.SMEM)
    w_spec = lambda shape: pl.BlockSpec(shape, lambda s, be, nb: (be[jnp.clip(s - 1, 0, last)], 0, 0))
    y4 = pl.pallas_call(
        functools.partial(_experts_kernel, dummy_row=dummy_row),
        grid_spec=pltpu.PrefetchScalarGridSpec(
            num_scalar_prefetch=2,
            grid=(n_blocks + 2,),
            in_specs=[tbl_spec(0), tbl_spec(2), pl.BlockSpec(memory_space=pl.ANY),
                      w_spec((1, D, 2 * de)), w_spec((1, 1, 2 * de)), w_spec((1, de, D)), w_spec((1, 1, D))],
            out_specs=pl.BlockSpec(memory_space=pl.ANY),
            scratch_shapes=[pltpu.VMEM((2, EXPERT_BLOCK * PACK_ROWS, LANES), U32),
                            pltpu.VMEM((2, EXPERT_BLOCK * PACK_ROWS, LANES), U32),
                            pltpu.VMEM((D, 2 * de), BF16), pltpu.VMEM((de, D), BF16),
                            pltpu.SemaphoreType.DMA((2,)), pltpu.SemaphoreType.DMA((2,))],
        ),
        out_shape=jax.ShapeDtypeStruct(((n_assign + 2 * EXPERT_BLOCK) * PACK_ROWS, LANES), U32),
        compiler_params=cparams(dimension_semantics=("arbitrary",)),
        name="experts",
    )(block_expert, n_used, row_src.reshape(n_blocks, 1, EXPERT_BLOCK), row_dst.reshape(n_blocks, 1, EXPERT_BLOCK),
      hpk, w_gate_up, b_gate_up.reshape(n_exp, 1, 2 * de), w_down, b_down.reshape(n_exp, 1, D))

    y_spec = lambda kk: pl.BlockSpec((tm * PACK_ROWS, LANES), lambda i: (kk * (N // tm) + i, 0))
    out = pl.pallas_call(
        functools.partial(_combine_kernel, tm=tm),
        grid=(N // tm,),
        in_specs=[row_spec(D), row_spec(LANES)] + [y_spec(kk) for kk in range(TOP_K)],
        out_specs=row_spec(D),
        out_shape=jax.ShapeDtypeStruct((N, D), F32),
        compiler_params=cparams(dimension_semantics=("arbitrary",)),
        name="combine",
    )(x1, gate, y4, y4, y4, y4)
    return out.reshape(B, S, D)


def kernel(x, norm1_g, w_in, q_norm_g, k_norm_g, w_pool_grp, pool_scale, w_pool_up, w_attn_up, w_out, norm2_g,
           w_router, b_router, w_gate_up, b_gate_up, w_down, b_down):
    for layer in range(norm1_g.shape[0]):
        x = _layer(x, norm1_g[layer], w_in[layer], q_norm_g[layer], k_norm_g[layer], w_pool_grp[layer],
                   pool_scale[layer], w_pool_up[layer], w_attn_up[layer], w_out[layer], norm2_g[layer],
                   w_router[layer], b_router[layer], w_gate_up[layer], b_gate_up[layer], w_down[layer],
                   b_down[layer])
    return x
```

```python
import functools

import jax
import jax.numpy as jnp
from jax import lax
from jax.experimental import pallas as pl
from jax.experimental.pallas import tpu as pltpu
from jax.experimental.pallas import tpu_sc as plsc

F32 = jnp.float32
BF16 = jnp.bfloat16
U32 = jnp.uint32
I32 = jnp.int32

EPS = 1e-6
POOL_WINDOWS = (2, 4, 8, 16)
POOL_GROUP_DIM = 128
POOL_HALO = 16
SB_HEAD_DIM = 64
TOP_K = 4
SWIGLU_LIMIT = 7.0
SWIGLU_ALPHA = 1.702
EXPERT_BLOCK = 256
LANES = 128
PACK_ROWS = 4
ATTN_BLOCK = 128
ATTN_EXIT = 48.0
VMEM_LIMIT = 56 * 1024 * 1024
SC_CHUNK = 128
SC_RING = 4


def _dot(a, b):
    return jnp.dot(a, b, preferred_element_type=F32)


def _split_bf16(x):
    hi = x.astype(BF16)
    lo = (x - hi.astype(F32)).astype(BF16)
    return hi, lo


def _pack_rows(v, out_ref):
    t, d = v.shape
    half = d // 2
    lo = lax.bitcast_convert_type(v[:, :half].astype(BF16).astype(F32), U32) >> 16
    hi = lax.bitcast_convert_type(v[:, half:].astype(BF16).astype(F32), U32) & jnp.uint32(0xFFFF0000)
    w = lo | hi
    for c in range(PACK_ROWS):
        out_ref[pl.ds(c, t, stride=PACK_ROWS), :] = w[:, c * LANES:(c + 1) * LANES]


def _unpack_rows(ref, t):
    los, his = [], []
    for c in range(PACK_ROWS):
        w = ref[pl.ds(c, t, stride=PACK_ROWS), :]
        los.append(lax.bitcast_convert_type(w << 16, F32))
        his.append(lax.bitcast_convert_type(w & jnp.uint32(0xFFFF0000), F32))
    return jnp.concatenate(los + his, axis=1)


def _mixer_in_kernel(x_ref, g1_ref, win_ref, gq_ref, gk_ref, hsum_ref, wgrp_ref, pscale_ref, wpu_ref,
                     q_ref, k_ref, v_ref, ga_ref, p_ref, tail_ref, *, tm, pw, sw):
    i = pl.program_id(1)

    @pl.when(i == 0)
    def _():
        tail_ref[...] = jnp.zeros_like(tail_ref)

    x = x_ref[...]
    ms = jnp.mean(x * x, axis=-1, keepdims=True)
    h = (x * lax.rsqrt(ms + EPS) * g1_ref[...]).astype(BF16)

    u = _dot(h, win_ref[:, 0:pw])
    xx = jnp.concatenate([tail_ref[...], u], axis=0)
    tail_ref[...] = u[tm - POOL_HALO:, :]
    pos = i * tm + lax.broadcasted_iota(I32, (tm, POOL_GROUP_DIM), 0)
    mixed = []
    for g, w in enumerate(POOL_WINDOWS):
        s = xx[:, g * POOL_GROUP_DIM:(g + 1) * POOL_GROUP_DIM]
        step = 1
        while step < w:
            s = s + pltpu.roll(s, step, axis=0)
            step *= 2
        count = jnp.minimum(pos + 1, w).astype(F32)
        ug = u[:, g * POOL_GROUP_DIM:(g + 1) * POOL_GROUP_DIM]
        d = s[POOL_HALO:, :] / count - ug
        mixed.append(_dot(d.astype(BF16), wgrp_ref[g]))
    pm = jnp.concatenate(mixed, axis=1) * pscale_ref[...]
    pool_out = _dot(pm.astype(BF16), wpu_ref[...])
    d_model = pool_out.shape[1]
    g_pool = _dot(h, win_ref[:, pw + 3 * sw:pw + 3 * sw + d_model])
    p_ref[...] = (jax.nn.sigmoid(g_pool) * pool_out).astype(BF16)
    g_attn = _dot(h, win_ref[:, pw + 3 * sw + d_model:pw + 3 * sw + 2 * d_model])
    ga_ref[...] = jax.nn.sigmoid(g_attn).astype(BF16)

    def head_norm(t, gain):
        hi, lo = _split_bf16(t * t)
        ss = _dot(hi, hsum_ref[...]) + _dot(lo, hsum_ref[...])
        return t * lax.rsqrt(ss * (1.0 / SB_HEAD_DIM) + EPS) * gain

    q = _dot(h, win_ref[:, pw:pw + sw])
    q_ref[...] = (head_norm(q, gq_ref[...]) * (SB_HEAD_DIM ** -0.5)).astype(BF16)
    k = _dot(h, win_ref[:, pw + sw:pw + 2 * sw])
    k_ref[...] = head_norm(k, gk_ref[...]).astype(BF16)
    v_ref[...] = _dot(h, win_ref[:, pw + 2 * sw:pw + 3 * sw]).astype(BF16)


def _attn_kernel(q_ref, k_ref, v_ref, tri_ref, o_ref, *scratch, n_pairs):
    qs, acc, rr = scratch[:n_pairs], scratch[n_pairs:2 * n_pairs], scratch[2 * n_pairs:]
    qi = pl.program_id(1)
    bq = ATTN_BLOCK
    first_head = lax.broadcasted_iota(I32, (bq, LANES), 1) < SB_HEAD_DIM
    for p in range(n_pairs):
        q2 = q_ref[:, p * LANES:(p + 1) * LANES]
        qs[p][:bq] = jnp.where(first_head, q2, jnp.zeros_like(q2))
        qs[p][bq:] = jnp.where(first_head, jnp.zeros_like(q2), q2)
    row = lax.broadcasted_iota(I32, (2 * bq, bq), 0)
    col = lax.broadcasted_iota(I32, (2 * bq, bq), 1)
    causal = col < (row & (bq - 1))
    contract_last = (((1,), (1,)), ((), ()))

    def softplus(z):
        return jnp.maximum(z, 0.0) + jnp.log(1.0 + jnp.exp(-jnp.abs(z)))

    def suffix_sums(sp):
        hi, lo = _split_bf16(sp)
        return _dot(jnp.concatenate([hi, lo], axis=1), tri_ref[...])

    def cols(ref, start, p):
        return ref[pl.ds(start, bq), p * LANES:(p + 1) * LANES]

    start_d = pl.multiple_of(qi * bq, bq)
    start_n = pl.multiple_of(jnp.maximum(qi - 1, 0) * bq, bq)
    has_n = qi >= 1
    z_d = [lax.dot_general(qs[p][...], cols(k_ref, start_d, p), contract_last, preferred_element_type=F32)
           for p in range(n_pairs)]
    z_n = [lax.dot_general(qs[p][...], cols(k_ref, start_n, p), contract_last, preferred_element_type=F32)
           for p in range(n_pairs)]
    s_d = [suffix_sums(jnp.where(causal, softplus(z), 0.0)) for z in z_d]
    s_n = [suffix_sums(jnp.where(has_n, softplus(z), 0.0)) for z in z_n]
    for p in range(n_pairs):
        a_d = jnp.where(causal, jnp.exp(z_d[p] - s_d[p][:, :bq]), 0.0)
        r_d = s_d[p][:, bq:]
        a_n = jnp.where(has_n, jnp.exp(z_n[p] - (r_d + s_n[p][:, :bq])), 0.0)
        acc[p][...] = (_dot(a_d.astype(BF16), cols(v_ref, start_d, p)) + _dot(a_n.astype(BF16), cols(v_ref, start_n, p)))
        rr[p][...] = r_d + s_n[p][:, bq:]

    def r_min():
        m = rr[0][...]
        for p in range(1, n_pairs):
            m = jnp.minimum(m, rr[p][...])
        return jnp.min(m)

    def cond(c):
        j, rm = c
        return jnp.logical_and(j >= 0, rm < ATTN_EXIT)

    def body(c):
        j, _ = c
        start = pl.multiple_of(j * bq, bq)
        zs = [lax.dot_general(qs[p][...], cols(k_ref, start, p), contract_last, preferred_element_type=F32)
              for p in range(n_pairs)]
        ss = [suffix_sums(softplus(z)) for z in zs]
        for p in range(n_pairs):
            r = rr[p][...]
            a = jnp.exp(zs[p] - (r + ss[p][:, :bq]))
            acc[p][...] += _dot(a.astype(BF16), cols(v_ref, start, p))
            rr[p][...] = r + ss[p][:, bq:]
        return j - 1, r_min()

    lax.while_loop(cond, body, (qi - 2, r_min()))
    for p in range(n_pairs):
        o_ref[:, p * LANES:(p + 1) * LANES] = jnp.where(first_head, acc[p][:bq], acc[p][bq:]).astype(BF16)


def _mixer_out_kernel(x_ref, p_ref, ga_ref, sba_ref, wau_ref, wout_ref, g2_ref, wr_hi_ref, wr_lo_ref, br_ref,
                      ltri_ref, x1_ref, hp_ref, ri_ref, gate_ref, cnt_ref, *, tm):
    step = pl.program_id(0)

    @pl.when(step == 0)
    def _():
        cnt_ref[...] = jnp.zeros_like(cnt_ref)

    attn_out = _dot(sba_ref[...], wau_ref[...])
    merged = p_ref[...].astype(F32) + ga_ref[...].astype(F32) * attn_out
    x1 = x_ref[...] + _dot(merged.astype(BF16), wout_ref[...])
    x1_ref[...] = x1
    ms = jnp.mean(x1 * x1, axis=-1, keepdims=True)
    h2 = x1 * lax.rsqrt(ms + EPS) * g2_ref[...]
    _pack_rows(h2, hp_ref)

    h_hi, h_lo = _split_bf16(h2)
    logits = (_dot(h_hi, wr_hi_ref[...]) + _dot(h_hi, wr_lo_ref[...]) + _dot(h_lo, wr_hi_ref[...])
              + br_ref[...])
    lane = lax.broadcasted_iota(I32, logits.shape, 1)
    work = logits
    vals, idxs = [], []
    for _ in range(TOP_K):
        m = jnp.max(work, axis=-1, keepdims=True)
        ik = jnp.min(jnp.where(work == m, lane, LANES), axis=-1, keepdims=True)
        vals.append(m)
        idxs.append(ik)
        work = jnp.where(lane == ik, -jnp.inf, work)
    es = [jnp.exp(v - vals[0]) for v in vals]
    denom = es[0] + es[1] + es[2] + es[3]
    hot = jnp.zeros(logits.shape, F32)
    for ik in idxs:
        hot = hot + (lane == ik).astype(F32)
    before = _dot(ltri_ref[...], hot.astype(BF16)) + cnt_ref[0:1, :]
    ri = jnp.zeros(logits.shape, I32)
    gt = jnp.zeros(logits.shape, F32)
    for kk in range(TOP_K):
        rank = jnp.sum(jnp.where(lane == idxs[kk], before, 0.0), axis=-1, keepdims=True).astype(I32)
        ri = jnp.where(lane == kk, idxs[kk], ri)
        ri = jnp.where(lane == TOP_K + kk, rank, ri)
        gt = jnp.where(lane == kk, es[kk] / denom, gt)
    ri_ref[...] = ri
    gate_ref[...] = gt
    cnt_ref[...] = cnt_ref[...] + jnp.sum(hot, axis=0, keepdims=True)


def _row_copy(src, src_row, dst, dst_row, sem):
    return pltpu.make_async_copy(src.at[pl.ds(pl.multiple_of(src_row * PACK_ROWS, PACK_ROWS), PACK_ROWS)],
                                 dst.at[pl.ds(pl.multiple_of(dst_row * PACK_ROWS, PACK_ROWS), PACK_ROWS)], sem)


def _dispatch_kernel(dest_ref, h_ref, xs_in_ref, xs_ref, sem, *, tm):
    del xs_in_ref

    def start(r, c):
        for kk in range(TOP_K):
            _row_copy(h_ref, r, xs_ref, dest_ref[0, 0, r * TOP_K + kk], sem).start(priority=kk % 2)
        return c

    lax.fori_loop(0, tm, start, 0)

    def wait(r, c):
        for kk in range(TOP_K):
            _row_copy(h_ref, r, xs_ref, dest_ref[0, 0, r * TOP_K + kk], sem).wait()
        return c

    lax.fori_loop(0, tm, wait, 0)


def _experts_kernel(be_ref, nb_ref, xs_ref, wgu_ref, bgu_ref, wd_ref, bd_ref, ys_ref, wgu_bf_ref, wd_bf_ref):
    blk = pl.program_id(0)
    de = wd_ref.shape[1]

    @pl.when(jnp.logical_or(blk == 0, be_ref[blk] != be_ref[jnp.maximum(blk - 1, 0)]))
    def _():
        wgu_bf_ref[...] = wgu_ref[0].astype(BF16)
        wd_bf_ref[...] = wd_ref[0].astype(BF16)

    @pl.when(blk < nb_ref[0])
    def _():
        x = _unpack_rows(xs_ref, EXPERT_BLOCK).astype(BF16)
        gu = _dot(x, wgu_bf_ref[...]) + bgu_ref[0]
        glu = jnp.minimum(gu[:, :de], SWIGLU_LIMIT)
        lin = jnp.clip(gu[:, de:], -SWIGLU_LIMIT, SWIGLU_LIMIT)
        act = glu * jax.nn.sigmoid(SWIGLU_ALPHA * glu) * (lin + 1.0)
        y = _dot(act.astype(BF16), wd_bf_ref[...]) + bd_ref[0]
        _pack_rows(y, ys_ref)

    @pl.when(blk >= nb_ref[0])
    def _():
        ys_ref[...] = jnp.zeros_like(ys_ref)


def _sc_gather_rows(table, idx):
    info = plsc.get_sparse_core_info()
    nc = info.num_cores
    nw = nc * info.num_subcores
    n_idx_rows, ch = idx.shape
    n_ch = n_idx_rows // nw
    half = SC_RING // 2
    assert ch == SC_CHUNK and n_ch * nw == n_idx_rows and n_ch % SC_RING == 0 and n_ch >= 2 * SC_RING
    width = table.shape[1]
    mesh = plsc.VectorSubcoreMesh(core_axis_name="c", subcore_axis_name="s")

    @functools.partial(
        pl.kernel, mesh=mesh, out_type=jax.ShapeDtypeStruct((n_idx_rows * ch, width), table.dtype),
        scratch_types=([pltpu.VMEM((n_ch, ch), I32)] + [pltpu.VMEM((ch, width), table.dtype)] * SC_RING
                       + [pltpu.SemaphoreType.DMA((SC_RING,)), pltpu.SemaphoreType.DMA((SC_RING,))]))
    def gather_kernel(table_hbm, idx_hbm, out_hbm, idx_v, *rest):
        bufs, gsem, wsem = rest[:SC_RING], rest[SC_RING], rest[SC_RING + 1]
        wid = lax.axis_index("s") * nc + lax.axis_index("c")
        pltpu.sync_copy(idx_hbm.at[pl.ds(wid * n_ch, n_ch)], idx_v)
        base = wid * n_ch * ch

        def gather(j, b):
            return pltpu.make_async_copy(table_hbm.at[idx_v.at[j]], bufs[b], gsem.at[b])

        def write(j, b):
            return pltpu.make_async_copy(bufs[b], out_hbm.at[pl.ds(pl.multiple_of(base + j * ch, ch), ch)], wsem.at[b])

        for b in range(SC_RING):
            gather(b, b).start()
        for b in range(half):
            gather(b, b).wait()
            write(b, b).start()

        @pl.loop(SC_RING, n_ch, step=SC_RING)
        def _(j0):
            for b in range(SC_RING):
                j = j0 + b
                write(j - SC_RING, b).wait()
                gather(j, b).start()
                b2 = (b + half) % SC_RING
                gather(j - half, b2).wait()
                write(j - half, b2).start()

        for b in range(half, SC_RING):
            gather(n_ch - SC_RING + b, b).wait()
            write(n_ch - SC_RING + b, b).start()
        for b in range(SC_RING):
            write(n_ch - SC_RING + b, b).wait()

    return gather_kernel(table, idx)


def _combine_kernel(x1_ref, gate_ref, y0_ref, y1_ref, y2_ref, y3_ref, o_ref, *, tm):
    gate = gate_ref[...]
    out = x1_ref[...]
    for kk, y_ref in enumerate((y0_ref, y1_ref, y2_ref, y3_ref)):
        out = out + gate[:, kk:kk + 1] * _unpack_rows(y_ref, tm)
    o_ref[...] = out


def _const_spec(shape):
    nd = len(shape)
    return pl.BlockSpec(shape, lambda *_: (0,) * nd)


def _layer(x, norm1_g, w_in, q_norm_g, k_norm_g, w_pool_grp, pool_scale, w_pool_up, w_attn_up, w_out, norm2_g,
           w_router, b_router, w_gate_up, b_gate_up, w_down, b_down):
    B, S, D = x.shape
    N = B * S
    pw = w_pool_up.shape[0]
    sw = w_attn_up.shape[0]
    n_exp = w_router.shape[1]
    de = w_down.shape[1]
    heads = sw // SB_HEAD_DIM
    assert pw == len(POOL_WINDOWS) * POOL_GROUP_DIM and heads % 2 == 0 and n_exp <= LANES
    assert D == 2 * PACK_ROWS * LANES and w_in.shape[1] == pw + 3 * sw + 2 * D
    tm = 512 if S % 512 == 0 else 256
    assert S % tm == 0 and S % ATTN_BLOCK == 0
    xf = x.reshape(N, D)
    cparams = functools.partial(pltpu.CompilerParams, vmem_limit_bytes=VMEM_LIMIT)

    hsum = (jnp.arange(sw)[:, None] // SB_HEAD_DIM == jnp.arange(sw)[None, :] // SB_HEAD_DIM).astype(BF16)
    nt = S // tm
    tok_spec = lambda w: pl.BlockSpec((tm, w), lambda b, i: (b * nt + i, 0))
    q2, k2, v2, ga, pg = pl.pallas_call(
        functools.partial(_mixer_in_kernel, tm=tm, pw=pw, sw=sw),
        grid=(B, nt),
        in_specs=[tok_spec(D), _const_spec((1, D)), _const_spec((D, w_in.shape[1])), _const_spec((1, sw)),
                  _const_spec((1, sw)), _const_spec((sw, sw)),
                  _const_spec((len(POOL_WINDOWS), POOL_GROUP_DIM, POOL_GROUP_DIM)), _const_spec((1, pw)),
                  _const_spec((pw, D))],
        out_specs=[tok_spec(sw), tok_spec(sw), tok_spec(sw), tok_spec(D), tok_spec(D)],
        out_shape=[jax.ShapeDtypeStruct((N, sw), BF16)] * 3 + [jax.ShapeDtypeStruct((N, D), BF16)] * 2,
        scratch_shapes=[pltpu.VMEM((POOL_HALO, pw), F32)],
        compiler_params=cparams(dimension_semantics=("arbitrary", "arbitrary")),
        name="mixer_in",
    )(xf, norm1_g.reshape(1, D), w_in.astype(BF16), jnp.tile(q_norm_g, heads).reshape(1, sw),
      jnp.tile(k_norm_g, heads).reshape(1, sw), hsum, w_pool_grp.astype(BF16), pool_scale.reshape(1, pw),
      w_pool_up.astype(BF16))

    bq = ATTN_BLOCK
    nq = S // bq
    jj = jnp.arange(bq)
    tri_half = jnp.concatenate([(jj[:, None] >= jj[None, :]).astype(BF16), jnp.ones((bq, bq), BF16)], axis=1)
    tri = jnp.concatenate([tri_half, tri_half], axis=0)
    n_pairs = heads // 2
    kv_spec = pl.BlockSpec((S, sw), lambda b, qi: (b, 0), pipeline_mode=pl.Buffered(1))
    sba = pl.pallas_call(
        functools.partial(_attn_kernel, n_pairs=n_pairs),
        grid=(B, nq),
        in_specs=[pl.BlockSpec((bq, sw), lambda b, qi: (b * nq + qi, 0)), kv_spec, kv_spec,
                  _const_spec((2 * bq, 2 * bq))],
        out_specs=pl.BlockSpec((bq, sw), lambda b, qi: (b * nq + qi, 0)),
        out_shape=jax.ShapeDtypeStruct((N, sw), BF16),
        scratch_shapes=([pltpu.VMEM((2 * bq, LANES), BF16)] * n_pairs + [pltpu.VMEM((2 * bq, LANES), F32)] * (2 * n_pairs)),
        compiler_params=cparams(dimension_semantics=("arbitrary", "arbitrary")),
        name="sb_attn",
    )(q2, k2, v2, tri)

    wr = jnp.zeros((D, LANES), F32).at[:, :n_exp].set(w_router)
    wr_hi = wr.astype(BF16)
    wr_lo = (wr - wr_hi.astype(F32)).astype(BF16)
    br = jnp.full((1, LANES), -jnp.inf, F32).at[0, :n_exp].set(b_router)
    ltri = (jnp.arange(tm)[:, None] > jnp.arange(tm)[None, :]).astype(BF16)
    row_spec = lambda w: pl.BlockSpec((tm, w), lambda i: (i, 0))
    x1, hpk, ri, gate, cnt = pl.pallas_call(
        functools.partial(_mixer_out_kernel, tm=tm),
        grid=(N // tm,),
        in_specs=[row_spec(D), row_spec(D), row_spec(D), row_spec(sw), _const_spec((sw, D)), _const_spec((D, D)),
                  _const_spec((1, D)), _const_spec((D, LANES)), _const_spec((D, LANES)), _const_spec((1, LANES)),
                  _const_spec((tm, tm))],
        out_specs=[row_spec(D), pl.BlockSpec((tm * PACK_ROWS, LANES), lambda i: (i, 0)), row_spec(LANES),
                   row_spec(LANES), _const_spec((8, LANES))],
        out_shape=[jax.ShapeDtypeStruct((N, D), F32), jax.ShapeDtypeStruct((N * PACK_ROWS, LANES), U32),
                   jax.ShapeDtypeStruct((N, LANES), I32), jax.ShapeDtypeStruct((N, LANES), F32),
                   jax.ShapeDtypeStruct((8, LANES), F32)],
        compiler_params=cparams(dimension_semantics=("arbitrary",)),
        name="mixer_out",
    )(xf, pg, ga, sba, w_attn_up.astype(BF16), w_out.astype(BF16), norm2_g.reshape(1, D), wr_hi, wr_lo, br, ltri)

    n_assign = N * TOP_K
    n_blocks = -(-(n_assign + n_exp * (EXPERT_BLOCK - 1)) // EXPERT_BLOCK)
    counts = cnt[0, :n_exp].astype(I32)
    padded = (counts + EXPERT_BLOCK - 1) // EXPERT_BLOCK * EXPERT_BLOCK
    padded_end = jnp.cumsum(padded)
    start_pad = padded_end - padded
    idx = ri[:, :TOP_K]
    rank = ri[:, TOP_K:2 * TOP_K]
    onehot = idx[:, :, None] == jnp.arange(n_exp, dtype=I32)[None, None, :]
    dest = jnp.sum(jnp.where(onehot, start_pad[None, None, :], 0), axis=-1) + rank
    block_start = jnp.arange(n_blocks, dtype=I32) * EXPERT_BLOCK
    block_expert = jnp.minimum(jnp.sum((padded_end[None, :] <= block_start[:, None]).astype(I32), axis=1), n_exp - 1)
    n_used = (padded_end[-1] // EXPERT_BLOCK).astype(I32).reshape(1)

    td = 256
    dest_blocks = dest.reshape(N // td, 1, td * TOP_K)
    dest_spec = pl.BlockSpec((1, 1, td * TOP_K), lambda i: (i, 0, 0), memory_space=pltpu.SMEM)
    n_rows = n_blocks * EXPERT_BLOCK
    xs = pl.pallas_call(
        functools.partial(_dispatch_kernel, tm=td),
        grid=(N // td,),
        in_specs=[dest_spec, pl.BlockSpec((td * PACK_ROWS, LANES), lambda i: (i, 0)),
                  pl.BlockSpec(memory_space=pl.ANY)],
        out_specs=pl.BlockSpec(memory_space=pl.ANY),
        out_shape=jax.ShapeDtypeStruct((n_rows * PACK_ROWS, LANES), U32),
        scratch_shapes=[pltpu.SemaphoreType.DMA(())],
        input_output_aliases={2: 0},
        compiler_params=cparams(dimension_semantics=("arbitrary",)),
        name="dispatch",
    )(dest_blocks, hpk, jnp.zeros((n_rows * PACK_ROWS, LANES), U32))

    ys = pl.pallas_call(
        _experts_kernel,
        grid_spec=pltpu.PrefetchScalarGridSpec(
            num_scalar_prefetch=2,
            grid=(n_blocks,),
            in_specs=[pl.BlockSpec((EXPERT_BLOCK * PACK_ROWS, LANES), lambda i, be, nb: (i, 0)),
                      pl.BlockSpec((1, D, 2 * de), lambda i, be, nb: (be[i], 0, 0)),
                      pl.BlockSpec((1, 1, 2 * de), lambda i, be, nb: (be[i], 0, 0)),
                      pl.BlockSpec((1, de, D), lambda i, be, nb: (be[i], 0, 0)),
                      pl.BlockSpec((1, 1, D), lambda i, be, nb: (be[i], 0, 0))],
            out_specs=pl.BlockSpec((EXPERT_BLOCK * PACK_ROWS, LANES), lambda i, be, nb: (i, 0)),
            scratch_shapes=[pltpu.VMEM((D, 2 * de), BF16), pltpu.VMEM((de, D), BF16)],
        ),
        out_shape=jax.ShapeDtypeStruct((n_rows * PACK_ROWS, LANES), U32),
        compiler_params=cparams(dimension_semantics=("arbitrary",)),
        name="experts",
    )(block_expert, n_used, xs, w_gate_up, b_gate_up.reshape(n_exp, 1, 2 * de), w_down, b_down.reshape(n_exp, 1, D))

    src_rows = dest.T.reshape(-1, 1) * PACK_ROWS + jnp.arange(PACK_ROWS, dtype=I32)[None, :]
    y4 = _sc_gather_rows(lax.bitcast_convert_type(ys, I32), src_rows.reshape(-1, SC_CHUNK))
    y4 = lax.bitcast_convert_type(y4, U32)

    y_spec = lambda kk: pl.BlockSpec((tm * PACK_ROWS, LANES), lambda i: (kk * (N // tm) + i, 0))
    out = pl.pallas_call(
        functools.partial(_combine_kernel, tm=tm),
        grid=(N // tm,),
        in_specs=[row_spec(D), row_spec(LANES)] + [y_spec(kk) for kk in range(TOP_K)],
        out_specs=row_spec(D),
        out_shape=jax.ShapeDtypeStruct((N, D), F32),
        compiler_params=cparams(dimension_semantics=("arbitrary",)),
        name="combine",
    )(x1, gate, y4, y4, y4, y4)
    return out.reshape(B, S, D)


def kernel(x, norm1_g, w_in, q_norm_g, k_norm_g, w_pool_grp, pool_scale, w_pool_up, w_attn_up, w_out, norm2_g,
           w_router, b_router, w_gate_up, b_gate_up, w_down, b_down):
    for layer in range(norm1_g.shape[0]):
        x = _layer(x, norm1_g[layer], w_in[layer], q_norm_g[layer], k_norm_g[layer], w_pool_grp[layer],
                   pool_scale[layer], w_pool_up[layer], w_attn_up[layer], w_out[layer], norm2_g[layer],
                   w_router[layer], b_router[layer], w_gate_up[layer], b_gate_up[layer], w_down[layer],
                   b_down[layer])
    return x
```

```python
import functools

import jax
import jax.numpy as jnp
from jax import lax
from jax.experimental import pallas as pl
from jax.experimental.pallas import tpu as pltpu
from jax.experimental.pallas import tpu_sc as plsc

F32 = jnp.float32
BF16 = jnp.bfloat16
U32 = jnp.uint32
I32 = jnp.int32

EPS = 1e-6
POOL_WINDOWS = (2, 4, 8, 16)
POOL_GROUP_DIM = 128
POOL_HALO = 16
SB_HEAD_DIM = 64
TOP_K = 4
SWIGLU_LIMIT = 7.0
SWIGLU_ALPHA = 1.702
EXPERT_BLOCK = 256
LANES = 128
PACK_ROWS = 4
ATTN_BLOCK = 128
ATTN_EXIT = 48.0
VMEM_LIMIT = 56 * 1024 * 1024
SC_CHUNK = 128
SC_RING = 4


def _dot(a, b):
    return jnp.dot(a, b, preferred_element_type=F32)


def _split_bf16(x):
    hi = x.astype(BF16)
    lo = (x - hi.astype(F32)).astype(BF16)
    return hi, lo


def _pack_rows(v, out_ref):
    t, d = v.shape
    half = d // 2
    lo = lax.bitcast_convert_type(v[:, :half].astype(BF16).astype(F32), U32) >> 16
    hi = lax.bitcast_convert_type(v[:, half:].astype(BF16).astype(F32), U32) & jnp.uint32(0xFFFF0000)
    w = lo | hi
    for c in range(PACK_ROWS):
        out_ref[pl.ds(c, t, stride=PACK_ROWS), :] = w[:, c * LANES:(c + 1) * LANES]


def _unpack_rows(ref, t):
    los, his = [], []
    for c in range(PACK_ROWS):
        w = ref[pl.ds(c, t, stride=PACK_ROWS), :]
        los.append(lax.bitcast_convert_type(w << 16, F32))
        his.append(lax.bitcast_convert_type(w & jnp.uint32(0xFFFF0000), F32))
    return jnp.concatenate(los + his, axis=1)


def _mixer_in_kernel(x_ref, g1_ref, win_ref, gq_ref, gk_ref, hsum_ref, wgrp_ref, pscale_ref, wpu_ref,
                     q_ref, k_ref, v_ref, ga_ref, p_ref, tail_ref, *, tm, pw, sw):
    i = pl.program_id(1)

    @pl.when(i == 0)
    def _():
        tail_ref[...] = jnp.zeros_like(tail_ref)

    x = x_ref[...]
    ms = jnp.mean(x * x, axis=-1, keepdims=True)
    h = (x * lax.rsqrt(ms + EPS) * g1_ref[...]).astype(BF16)

    u = _dot(h, win_ref[:, 0:pw])
    xx = jnp.concatenate([tail_ref[...], u], axis=0)
    tail_ref[...] = u[tm - POOL_HALO:, :]
    pos = i * tm + lax.broadcasted_iota(I32, (tm, POOL_GROUP_DIM), 0)
    mixed = []
    for g, w in enumerate(POOL_WINDOWS):
        s = xx[:, g * POOL_GROUP_DIM:(g + 1) * POOL_GROUP_DIM]
        step = 1
        while step < w:
            s = s + pltpu.roll(s, step, axis=0)
            step *= 2
        count = jnp.minimum(pos + 1, w).astype(F32)
        ug = u[:, g * POOL_GROUP_DIM:(g + 1) * POOL_GROUP_DIM]
        d = s[POOL_HALO:, :] / count - ug
        mixed.append(_dot(d.astype(BF16), wgrp_ref[g]))
    pm = jnp.concatenate(mixed, axis=1) * pscale_ref[...]
    pool_out = _dot(pm.astype(BF16), wpu_ref[...])
    d_model = pool_out.shape[1]
    g_pool = _dot(h, win_ref[:, pw + 3 * sw:pw + 3 * sw + d_model])
    p_ref[...] = (jax.nn.sigmoid(g_pool) * pool_out).astype(BF16)
    g_attn = _dot(h, win_ref[:, pw + 3 * sw + d_model:pw + 3 * sw + 2 * d_model])
    ga_ref[...] = jax.nn.sigmoid(g_attn).astype(BF16)

    def head_norm(t, gain):
        hi, lo = _split_bf16(t * t)
        ss = _dot(hi, hsum_ref[...]) + _dot(lo, hsum_ref[...])
        return t * lax.rsqrt(ss * (1.0 / SB_HEAD_DIM) + EPS) * gain

    q = _dot(h, win_ref[:, pw:pw + sw])
    q_ref[...] = (head_norm(q, gq_ref[...]) * (SB_HEAD_DIM ** -0.5)).astype(BF16)
    k = _dot(h, win_ref[:, pw + sw:pw + 2 * sw])
    k_ref[...] = head_norm(k, gk_ref[...]).astype(BF16)
    v_ref[...] = _dot(h, win_ref[:, pw + 2 * sw:pw + 3 * sw]).astype(BF16)


def _attn_kernel(q_ref, k_ref, v_ref, tri_ref, o_ref, *scratch, n_pairs):
    qs, acc, rr = scratch[:n_pairs], scratch[n_pairs:2 * n_pairs], scratch[2 * n_pairs:]
    qi = pl.program_id(1)
    bq = ATTN_BLOCK
    first_head = lax.broadcasted_iota(I32, (bq, LANES), 1) < SB_HEAD_DIM
    for p in range(n_pairs):
        q2 = q_ref[:, p * LANES:(p + 1) * LANES]
        qs[p][:bq] = jnp.where(first_head, q2, jnp.zeros_like(q2))
        qs[p][bq:] = jnp.where(first_head, jnp.zeros_like(q2), q2)
    row = lax.broadcasted_iota(I32, (2 * bq, bq), 0)
    col = lax.broadcasted_iota(I32, (2 * bq, bq), 1)
    causal = col < (row & (bq - 1))
    contract_last = (((1,), (1,)), ((), ()))

    def softplus(z):
        return jnp.maximum(z, 0.0) + jnp.log(1.0 + jnp.exp(-jnp.abs(z)))

    def suffix_sums(sp):
        hi, lo = _split_bf16(sp)
        return _dot(jnp.concatenate([hi, lo], axis=1), tri_ref[...])

    def cols(ref, start, p):
        return ref[pl.ds(start, bq), p * LANES:(p + 1) * LANES]

    start_d = pl.multiple_of(qi * bq, bq)
    start_n = pl.multiple_of(jnp.maximum(qi - 1, 0) * bq, bq)
    has_n = qi >= 1
    z_d = [lax.dot_general(qs[p][...], cols(k_ref, start_d, p), contract_last, preferred_element_type=F32)
           for p in range(n_pairs)]
    z_n = [lax.dot_general(qs[p][...], cols(k_ref, start_n, p), contract_last, preferred_element_type=F32)
           for p in range(n_pairs)]
    s_d = [suffix_sums(jnp.where(causal, softplus(z), 0.0)) for z in z_d]
    s_n = [suffix_sums(jnp.where(has_n, softplus(z), 0.0)) for z in z_n]
    for p in range(n_pairs):
        a_d = jnp.where(causal, jnp.exp(z_d[p] - s_d[p][:, :bq]), 0.0)
        r_d = s_d[p][:, bq:]
        a_n = jnp.where(has_n, jnp.exp(z_n[p] - (r_d + s_n[p][:, :bq])), 0.0)
        acc[p][...] = (_dot(a_d.astype(BF16), cols(v_ref, start_d, p)) + _dot(a_n.astype(BF16), cols(v_ref, start_n, p)))
        rr[p][...] = r_d + s_n[p][:, bq:]

    def r_min():
        m = rr[0][...]
        for p in range(1, n_pairs):
            m = jnp.minimum(m, rr[p][...])
        return jnp.min(m)

    def cond(c):
        j, rm = c
        return jnp.logical_and(j >= 0, rm < ATTN_EXIT)

    def body(c):
        j, _ = c
        start = pl.multiple_of(j * bq, bq)
        zs = [lax.dot_general(qs[p][...], cols(k_ref, start, p), contract_last, preferred_element_type=F32)
              for p in range(n_pairs)]
        ss = [suffix_sums(softplus(z)) for z in zs]
        for p in range(n_pairs):
            r = rr[p][...]
            a = jnp.exp(zs[p] - (r + ss[p][:, :bq]))
            acc[p][...] += _dot(a.astype(BF16), cols(v_ref, start, p))
            rr[p][...] = r + ss[p][:, bq:]
        return j - 1, r_min()

    lax.while_loop(cond, body, (qi - 2, r_min()))
    for p in range(n_pairs):
        o_ref[:, p * LANES:(p + 1) * LANES] = jnp.where(first_head, acc[p][:bq], acc[p][bq:]).astype(BF16)


def _mixer_out_kernel(x_ref, p_ref, ga_ref, sba_ref, wau_ref, wout_ref, g2_ref, wr_hi_ref, wr_lo_ref, br_ref,
                      ltri_ref, x1_ref, hp_ref, ri_ref, gate_ref, cnt_ref, *, tm):
    step = pl.program_id(0)

    @pl.when(step == 0)
    def _():
        cnt_ref[...] = jnp.zeros_like(cnt_ref)

    attn_out = _dot(sba_ref[...], wau_ref[...])
    merged = p_ref[...].astype(F32) + ga_ref[...].astype(F32) * attn_out
    x1 = x_ref[...] + _dot(merged.astype(BF16), wout_ref[...])
    x1_ref[...] = x1
    ms = jnp.mean(x1 * x1, axis=-1, keepdims=True)
    h2 = x1 * lax.rsqrt(ms + EPS) * g2_ref[...]
    _pack_rows(h2, hp_ref)

    h_hi, h_lo = _split_bf16(h2)
    logits = (_dot(h_hi, wr_hi_ref[...]) + _dot(h_hi, wr_lo_ref[...]) + _dot(h_lo, wr_hi_ref[...])
              + br_ref[...])
    lane = lax.broadcasted_iota(I32, logits.shape, 1)
    work = logits
    vals, idxs = [], []
    for _ in range(TOP_K):
        m = jnp.max(work, axis=-1, keepdims=True)
        ik = jnp.min(jnp.where(work == m, lane, LANES), axis=-1, keepdims=True)
        vals.append(m)
        idxs.append(ik)
        work = jnp.where(lane == ik, -jnp.inf, work)
    es = [jnp.exp(v - vals[0]) for v in vals]
    denom = es[0] + es[1] + es[2] + es[3]
    hot = jnp.zeros(logits.shape, F32)
    for ik in idxs:
        hot = hot + (lane == ik).astype(F32)
    before = _dot(ltri_ref[...], hot.astype(BF16)) + cnt_ref[0:1, :]
    ri = jnp.zeros(logits.shape, I32)
    gt = jnp.zeros(logits.shape, F32)
    for kk in range(TOP_K):
        rank = jnp.sum(jnp.where(lane == idxs[kk], before, 0.0), axis=-1, keepdims=True).astype(I32)
        ri = jnp.where(lane == kk, idxs[kk], ri)
        ri = jnp.where(lane == TOP_K + kk, rank, ri)
        gt = jnp.where(lane == kk, es[kk] / denom, gt)
    ri_ref[...] = ri
    gate_ref[...] = gt
    cnt_ref[...] = cnt_ref[...] + jnp.sum(hot, axis=0, keepdims=True)


def _sc_mesh():
    info = plsc.get_sparse_core_info()
    return plsc.VectorSubcoreMesh(core_axis_name="c", subcore_axis_name="s"), info.num_cores, info.num_subcores


def _sc_scatter_rows(rows, idx, n_out):
    mesh, nc, ns = _sc_mesh()
    nw = nc * ns
    m, width = rows.shape
    ch = SC_CHUNK
    n_ch = m // ch // nw
    half = SC_RING // 2
    assert n_ch * ch * nw == m and idx.shape == (m // ch * TOP_K, ch) and n_ch % SC_RING == 0 and n_ch >= 2 * SC_RING

    @functools.partial(
        pl.kernel, mesh=mesh, out_type=jax.ShapeDtypeStruct((n_out, width), rows.dtype),
        scratch_types=([pltpu.VMEM((n_ch * TOP_K, ch), I32)] + [pltpu.VMEM((ch, width), rows.dtype)] * SC_RING
                       + [pltpu.SemaphoreType.DMA((SC_RING,)), pltpu.SemaphoreType.DMA((SC_RING,))]))
    def scatter_kernel(rows_hbm, idx_hbm, out_hbm, idx_v, *rest):
        bufs, rsem, ssem = rest[:SC_RING], rest[SC_RING], rest[SC_RING + 1]
        wid = lax.axis_index("s") * nc + lax.axis_index("c")
        pltpu.sync_copy(idx_hbm.at[pl.ds(wid * n_ch * TOP_K, n_ch * TOP_K)], idx_v)
        base = wid * n_ch * ch

        def read(j, b):
            return pltpu.make_async_copy(rows_hbm.at[pl.ds(pl.multiple_of(base + j * ch, ch), ch)], bufs[b], rsem.at[b])

        def scatters(j, b):
            return [pltpu.make_async_copy(bufs[b], out_hbm.at[idx_v.at[j * TOP_K + kk]], ssem.at[b])
                    for kk in range(TOP_K)]

        def start_all(copies):
            for c in copies:
                c.start()

        def wait_all(copies):
            for c in copies:
                c.wait()

        for b in range(SC_RING):
            read(b, b).start()
        for b in range(half):
            read(b, b).wait()
            start_all(scatters(b, b))

        @pl.loop(SC_RING, n_ch, step=SC_RING)
        def _(j0):
            for b in range(SC_RING):
                j = j0 + b
                wait_all(scatters(j - SC_RING, b))
                read(j, b).start()
                b2 = (b + half) % SC_RING
                read(j - half, b2).wait()
                start_all(scatters(j - half, b2))

        for b in range(half, SC_RING):
            read(n_ch - SC_RING + b, b).wait()
            start_all(scatters(n_ch - SC_RING + b, b))
        for b in range(SC_RING):
            wait_all(scatters(n_ch - SC_RING + b, b))

    return scatter_kernel(rows, idx)


def _padfill_kernel(start_ref, len_ref, xs_in_ref, xs_ref, zeros_ref, sem):
    del xs_in_ref
    zeros_ref[...] = jnp.zeros_like(zeros_ref)
    bits = [1 << b for b in reversed(range(EXPERT_BLOCK.bit_length() - 1))]

    def pieces(e):
        n = len_ref[e]
        for bit in bits:
            offset = n & ~(2 * bit - 1)
            row0 = pl.multiple_of((start_ref[e] + offset) * PACK_ROWS, PACK_ROWS)
            copy = pltpu.make_async_copy(zeros_ref.at[pl.ds(0, bit * PACK_ROWS)],
                                         xs_ref.at[pl.ds(row0, bit * PACK_ROWS)], sem)
            yield (n & bit) != 0, copy

    def start(e, c):
        for on, copy in pieces(e):
            pl.when(on)(copy.start)
        return c

    def wait(e, c):
        for on, copy in pieces(e):
            pl.when(on)(copy.wait)
        return c

    lax.fori_loop(0, start_ref.shape[0], start, 0)
    lax.fori_loop(0, start_ref.shape[0], wait, 0)


def _experts_kernel(be_ref, nb_ref, xs_ref, wgu_ref, bgu_ref, wd_ref, bd_ref, ys_ref, wgu_bf_ref, wd_bf_ref):
    blk = pl.program_id(0)
    de = wd_ref.shape[1]

    @pl.when(jnp.logical_or(blk == 0, be_ref[blk] != be_ref[jnp.maximum(blk - 1, 0)]))
    def _():
        wgu_bf_ref[...] = wgu_ref[0].astype(BF16)
        wd_bf_ref[...] = wd_ref[0].astype(BF16)

    @pl.when(blk < nb_ref[0])
    def _():
        x = _unpack_rows(xs_ref, EXPERT_BLOCK).astype(BF16)
        gu = _dot(x, wgu_bf_ref[...]) + bgu_ref[0]
        glu = jnp.minimum(gu[:, :de], SWIGLU_LIMIT)
        lin = jnp.clip(gu[:, de:], -SWIGLU_LIMIT, SWIGLU_LIMIT)
        act = glu * jax.nn.sigmoid(SWIGLU_ALPHA * glu) * (lin + 1.0)
        y = _dot(act.astype(BF16), wd_bf_ref[...]) + bd_ref[0]
        _pack_rows(y, ys_ref)

    @pl.when(blk >= nb_ref[0])
    def _():
        ys_ref[...] = jnp.zeros_like(ys_ref)


def _sc_gather_rows(table, idx):
    mesh, nc, ns = _sc_mesh()
    nw = nc * ns
    n_idx_rows, ch = idx.shape
    n_ch = n_idx_rows // nw
    half = SC_RING // 2
    assert ch == SC_CHUNK and n_ch * nw == n_idx_rows and n_ch % SC_RING == 0 and n_ch >= 2 * SC_RING
    width = table.shape[1]

    @functools.partial(
        pl.kernel, mesh=mesh, out_type=jax.ShapeDtypeStruct((n_idx_rows * ch, width), table.dtype),
        scratch_types=([pltpu.VMEM((n_ch, ch), I32)] + [pltpu.VMEM((ch, width), table.dtype)] * SC_RING
                       + [pltpu.SemaphoreType.DMA((SC_RING,)), pltpu.SemaphoreType.DMA((SC_RING,))]))
    def gather_kernel(table_hbm, idx_hbm, out_hbm, idx_v, *rest):
        bufs, gsem, wsem = rest[:SC_RING], rest[SC_RING], rest[SC_RING + 1]
        wid = lax.axis_index("s") * nc + lax.axis_index("c")
        pltpu.sync_copy(idx_hbm.at[pl.ds(wid * n_ch, n_ch)], idx_v)
        base = wid * n_ch * ch

        def gather(j, b):
            return pltpu.make_async_copy(table_hbm.at[idx_v.at[j]], bufs[b], gsem.at[b])

        def write(j, b):
            return pltpu.make_async_copy(bufs[b], out_hbm.at[pl.ds(pl.multiple_of(base + j * ch, ch), ch)], wsem.at[b])

        for b in range(SC_RING):
            gather(b, b).start()
        for b in range(half):
            gather(b, b).wait()
            write(b, b).start()

        @pl.loop(SC_RING, n_ch, step=SC_RING)
        def _(j0):
            for b in range(SC_RING):
                j = j0 + b
                write(j - SC_RING, b).wait()
                gather(j, b).start()
                b2 = (b + half) % SC_RING
                gather(j - half, b2).wait()
                write(j - half, b2).start()

        for b in range(half, SC_RING):
            gather(n_ch - SC_RING + b, b).wait()
            write(n_ch - SC_RING + b, b).start()
        for b in range(SC_RING):
            write(n_ch - SC_RING + b, b).wait()

    return gather_kernel(table, idx)


def _combine_kernel(x1_ref, gate_ref, y0_ref, y1_ref, y2_ref, y3_ref, o_ref, *, tm):
    gate = gate_ref[...]
    out = x1_ref[...]
    for kk, y_ref in enumerate((y0_ref, y1_ref, y2_ref, y3_ref)):
        out = out + gate[:, kk:kk + 1] * _unpack_rows(y_ref, tm)
    o_ref[...] = out


def _const_spec(shape):
    nd = len(shape)
    return pl.BlockSpec(shape, lambda *_: (0,) * nd)


def _layer(x, norm1_g, w_in, q_norm_g, k_norm_g, w_pool_grp, pool_scale, w_pool_up, w_attn_up, w_out, norm2_g,
           w_router, b_router, w_gate_up, b_gate_up, w_down, b_down):
    B, S, D = x.shape
    N = B * S
    pw = w_pool_up.shape[0]
    sw = w_attn_up.shape[0]
    n_exp = w_router.shape[1]
    de = w_down.shape[1]
    heads = sw // SB_HEAD_DIM
    assert pw == len(POOL_WINDOWS) * POOL_GROUP_DIM and heads % 2 == 0 and n_exp <= LANES
    assert D == 2 * PACK_ROWS * LANES and w_in.shape[1] == pw + 3 * sw + 2 * D
    tm = 512 if S % 512 == 0 else 256
    assert S % tm == 0 and S % ATTN_BLOCK == 0
    xf = x.reshape(N, D)
    cparams = functools.partial(pltpu.CompilerParams, vmem_limit_bytes=VMEM_LIMIT)

    hsum = (jnp.arange(sw)[:, None] // SB_HEAD_DIM == jnp.arange(sw)[None, :] // SB_HEAD_DIM).astype(BF16)
    nt = S // tm
    tok_spec = lambda w: pl.BlockSpec((tm, w), lambda b, i: (b * nt + i, 0))
    q2, k2, v2, ga, pg = pl.pallas_call(
        functools.partial(_mixer_in_kernel, tm=tm, pw=pw, sw=sw),
        grid=(B, nt),
        in_specs=[tok_spec(D), _const_spec((1, D)), _const_spec((D, w_in.shape[1])), _const_spec((1, sw)),
                  _const_spec((1, sw)), _const_spec((sw, sw)),
                  _const_spec((len(POOL_WINDOWS), POOL_GROUP_DIM, POOL_GROUP_DIM)), _const_spec((1, pw)),
                  _const_spec((pw, D))],
        out_specs=[tok_spec(sw), tok_spec(sw), tok_spec(sw), tok_spec(D), tok_spec(D)],
        out_shape=[jax.ShapeDtypeStruct((N, sw), BF16)] * 3 + [jax.ShapeDtypeStruct((N, D), BF16)] * 2,
        scratch_shapes=[pltpu.VMEM((POOL_HALO, pw), F32)],
        compiler_params=cparams(dimension_semantics=("arbitrary", "arbitrary")),
        name="mixer_in",
    )(xf, norm1_g.reshape(1, D), w_in.astype(BF16), jnp.tile(q_norm_g, heads).reshape(1, sw),
      jnp.tile(k_norm_g, heads).reshape(1, sw), hsum, w_pool_grp.astype(BF16), pool_scale.reshape(1, pw),
      w_pool_up.astype(BF16))

    bq = ATTN_BLOCK
    nq = S // bq
    jj = jnp.arange(bq)
    tri_half = jnp.concatenate([(jj[:, None] >= jj[None, :]).astype(BF16), jnp.ones((bq, bq), BF16)], axis=1)
    tri = jnp.concatenate([tri_half, tri_half], axis=0)
    n_pairs = heads // 2
    kv_spec = pl.BlockSpec((S, sw), lambda b, qi: (b, 0), pipeline_mode=pl.Buffered(1))
    sba = pl.pallas_call(
        functools.partial(_attn_kernel, n_pairs=n_pairs),
        grid=(B, nq),
        in_specs=[pl.BlockSpec((bq, sw), lambda b, qi: (b * nq + qi, 0)), kv_spec, kv_spec,
                  _const_spec((2 * bq, 2 * bq))],
        out_specs=pl.BlockSpec((bq, sw), lambda b, qi: (b * nq + qi, 0)),
        out_shape=jax.ShapeDtypeStruct((N, sw), BF16),
        scratch_shapes=([pltpu.VMEM((2 * bq, LANES), BF16)] * n_pairs + [pltpu.VMEM((2 * bq, LANES), F32)] * (2 * n_pairs)),
        compiler_params=cparams(dimension_semantics=("arbitrary", "arbitrary")),
        name="sb_attn",
    )(q2, k2, v2, tri)

    wr = jnp.zeros((D, LANES), F32).at[:, :n_exp].set(w_router)
    wr_hi = wr.astype(BF16)
    wr_lo = (wr - wr_hi.astype(F32)).astype(BF16)
    br = jnp.full((1, LANES), -jnp.inf, F32).at[0, :n_exp].set(b_router)
    ltri = (jnp.arange(tm)[:, None] > jnp.arange(tm)[None, :]).astype(BF16)
    row_spec = lambda w: pl.BlockSpec((tm, w), lambda i: (i, 0))
    x1, hpk, ri, gate, cnt = pl.pallas_call(
        functools.partial(_mixer_out_kernel, tm=tm),
        grid=(N // tm,),
        in_specs=[row_spec(D), row_spec(D), row_spec(D), row_spec(sw), _const_spec((sw, D)), _const_spec((D, D)),
                  _const_spec((1, D)), _const_spec((D, LANES)), _const_spec((D, LANES)), _const_spec((1, LANES)),
                  _const_spec((tm, tm))],
        out_specs=[row_spec(D), pl.BlockSpec((tm * PACK_ROWS, LANES), lambda i: (i, 0)), row_spec(LANES),
                   row_spec(LANES), _const_spec((8, LANES))],
        out_shape=[jax.ShapeDtypeStruct((N, D), F32), jax.ShapeDtypeStruct((N * PACK_ROWS, LANES), U32),
                   jax.ShapeDtypeStruct((N, LANES), I32), jax.ShapeDtypeStruct((N, LANES), F32),
                   jax.ShapeDtypeStruct((8, LANES), F32)],
        compiler_params=cparams(dimension_semantics=("arbitrary",)),
        name="mixer_out",
    )(xf, pg, ga, sba, w_attn_up.astype(BF16), w_out.astype(BF16), norm2_g.reshape(1, D), wr_hi, wr_lo, br, ltri)

    n_assign = N * TOP_K
    n_blocks = -(-(n_assign + n_exp * (EXPERT_BLOCK - 1)) // EXPERT_BLOCK)
    counts = cnt[0, :n_exp].astype(I32)
    padded = (counts + EXPERT_BLOCK - 1) // EXPERT_BLOCK * EXPERT_BLOCK
    padded_end = jnp.cumsum(padded)
    start_pad = padded_end - padded
    idx = ri[:, :TOP_K]
    rank = ri[:, TOP_K:2 * TOP_K]
    onehot = idx[:, :, None] == jnp.arange(n_exp, dtype=I32)[None, None, :]
    dest = jnp.sum(jnp.where(onehot, start_pad[None, None, :], 0), axis=-1) + rank
    block_start = jnp.arange(n_blocks, dtype=I32) * EXPERT_BLOCK
    block_expert = jnp.minimum(jnp.sum((padded_end[None, :] <= block_start[:, None]).astype(I32), axis=1), n_exp - 1)
    n_used = (padded_end[-1] // EXPERT_BLOCK).astype(I32).reshape(1)

    n_rows = n_blocks * EXPERT_BLOCK
    tok_per_chunk = SC_CHUNK // PACK_ROWS
    dst_rows = (dest.reshape(N // tok_per_chunk, tok_per_chunk, TOP_K).transpose(0, 2, 1)[..., None] * PACK_ROWS
                + jnp.arange(PACK_ROWS, dtype=I32))
    xs = _sc_scatter_rows(hpk, dst_rows.reshape(-1, SC_CHUNK), n_rows * PACK_ROWS)
    xs = pl.pallas_call(
        _padfill_kernel,
        grid_spec=pltpu.PrefetchScalarGridSpec(
            num_scalar_prefetch=2,
            grid=(1,),
            in_specs=[pl.BlockSpec(memory_space=pl.ANY)],
            out_specs=pl.BlockSpec(memory_space=pl.ANY),
            scratch_shapes=[pltpu.VMEM((EXPERT_BLOCK // 2 * PACK_ROWS, LANES), U32), pltpu.SemaphoreType.DMA(())],
        ),
        out_shape=jax.ShapeDtypeStruct((n_rows * PACK_ROWS, LANES), U32),
        input_output_aliases={2: 0},
        compiler_params=cparams(dimension_semantics=("arbitrary",)),
        name="padfill",
    )(start_pad + counts, padded - counts, xs)

    ys = pl.pallas_call(
        _experts_kernel,
        grid_spec=pltpu.PrefetchScalarGridSpec(
            num_scalar_prefetch=2,
            grid=(n_blocks,),
            in_specs=[pl.BlockSpec((EXPERT_BLOCK * PACK_ROWS, LANES),
                                   lambda i, be, nb: (jnp.minimum(i, nb[0] - 1), 0)),
                      pl.BlockSpec((1, D, 2 * de), lambda i, be, nb: (be[i], 0, 0)),
                      pl.BlockSpec((1, 1, 2 * de), lambda i, be, nb: (be[i], 0, 0)),
                      pl.BlockSpec((1, de, D), lambda i, be, nb: (be[i], 0, 0)),
                      pl.BlockSpec((1, 1, D), lambda i, be, nb: (be[i], 0, 0))],
            out_specs=pl.BlockSpec((EXPERT_BLOCK * PACK_ROWS, LANES), lambda i, be, nb: (i, 0)),
            scratch_shapes=[pltpu.VMEM((D, 2 * de), BF16), pltpu.VMEM((de, D), BF16)],
        ),
        out_shape=jax.ShapeDtypeStruct((n_rows * PACK_ROWS, LANES), U32),
        compiler_params=cparams(dimension_semantics=("arbitrary",)),
        name="experts",
    )(block_expert, n_used, xs, w_gate_up, b_gate_up.reshape(n_exp, 1, 2 * de), w_down, b_down.reshape(n_exp, 1, D))

    src_rows = dest.T.reshape(-1, 1) * PACK_ROWS + jnp.arange(PACK_ROWS, dtype=I32)[None, :]
    y4 = _sc_gather_rows(ys, src_rows.reshape(-1, SC_CHUNK))

    y_spec = lambda kk: pl.BlockSpec((tm * PACK_ROWS, LANES), lambda i: (kk * (N // tm) + i, 0))
    out = pl.pallas_call(
        functools.partial(_combine_kernel, tm=tm),
        grid=(N // tm,),
        in_specs=[row_spec(D), row_spec(LANES)] + [y_spec(kk) for kk in range(TOP_K)],
        out_specs=row_spec(D),
        out_shape=jax.ShapeDtypeStruct((N, D), F32),
        compiler_params=cparams(dimension_semantics=("arbitrary",)),
        name="combine",
    )(x1, gate, y4, y4, y4, y4)
    return out.reshape(B, S, D)


def kernel(x, norm1_g, w_in, q_norm_g, k_norm_g, w_pool_grp, pool_scale, w_pool_up, w_attn_up, w_out, norm2_g,
           w_router, b_router, w_gate_up, b_gate_up, w_down, b_down):
    for layer in range(norm1_g.shape[0]):
        x = _layer(x, norm1_g[layer], w_in[layer], q_norm_g[layer], k_norm_g[layer], w_pool_grp[layer],
                   pool_scale[layer], w_pool_up[layer], w_attn_up[layer], w_out[layer], norm2_g[layer],
                   w_router[layer], b_router[layer], w_gate_up[layer], b_gate_up[layer], w_down[layer],
                   b_down[layer])
    return x
```

```python
import functools

import jax
import jax.numpy as jnp
from jax import lax
from jax.experimental import pallas as pl
from jax.experimental.pallas import tpu as pltpu
from jax.experimental.pallas import tpu_sc as plsc

F32 = jnp.float32
BF16 = jnp.bfloat16
U32 = jnp.uint32
I32 = jnp.int32

EPS = 1e-6
POOL_WINDOWS = (2, 4, 8, 16)
POOL_GROUP_DIM = 128
POOL_HALO = 16
SB_HEAD_DIM = 64
TOP_K = 4
SWIGLU_LIMIT = 7.0
SWIGLU_ALPHA = 1.702
EXPERT_BLOCK = 256
LANES = 128
PACK_ROWS = 4
ATTN_BLOCK = 128
ATTN_EXIT = 48.0
VMEM_LIMIT = 56 * 1024 * 1024
SC_CHUNK = 128


def _dot(a, b):
    return jnp.dot(a, b, preferred_element_type=F32)


def _split_bf16(x):
    hi = x.astype(BF16)
    lo = (x - hi.astype(F32)).astype(BF16)
    return hi, lo


def _pack_rows(v, out_ref):
    half = v.shape[1] // 2
    lo = lax.bitcast_convert_type(v[:, :half].astype(BF16).astype(F32), U32) >> 16
    hi = lax.bitcast_convert_type(v[:, half:].astype(BF16).astype(F32), U32) & jnp.uint32(0xFFFF0000)
    w = lo | hi
    for c in range(PACK_ROWS):
        out_ref[c] = w[:, c * LANES:(c + 1) * LANES]


def _unpack_rows(ref):
    los, his = [], []
    for c in range(PACK_ROWS):
        w = ref[c]
        los.append(lax.bitcast_convert_type(w << 16, F32))
        his.append(lax.bitcast_convert_type(w & jnp.uint32(0xFFFF0000), F32))
    return jnp.concatenate(los + his, axis=1)


def _mixer_in_kernel(x_ref, g1_ref, win_ref, gq_ref, gk_ref, hsum_ref, wgrp_ref, pscale_ref, wpu_ref,
                     q_ref, k_ref, v_ref, ga_ref, p_ref, tail_ref, *, tm, pw, sw):
    i = pl.program_id(1)

    @pl.when(i == 0)
    def _():
        tail_ref[...] = jnp.zeros_like(tail_ref)

    x = x_ref[...]
    ms = jnp.mean(x * x, axis=-1, keepdims=True)
    h = (x * lax.rsqrt(ms + EPS) * g1_ref[...]).astype(BF16)

    u = _dot(h, win_ref[:, 0:pw])
    xx = jnp.concatenate([tail_ref[...], u], axis=0)
    tail_ref[...] = u[tm - POOL_HALO:, :]
    pos = i * tm + lax.broadcasted_iota(I32, (tm, POOL_GROUP_DIM), 0)
    mixed = []
    for g, w in enumerate(POOL_WINDOWS):
        s = xx[:, g * POOL_GROUP_DIM:(g + 1) * POOL_GROUP_DIM]
        step = 1
        while step < w:
            s = s + pltpu.roll(s, step, axis=0)
            step *= 2
        count = jnp.minimum(pos + 1, w).astype(F32)
        ug = u[:, g * POOL_GROUP_DIM:(g + 1) * POOL_GROUP_DIM]
        d = s[POOL_HALO:, :] / count - ug
        mixed.append(_dot(d.astype(BF16), wgrp_ref[g]))
    pm = jnp.concatenate(mixed, axis=1) * pscale_ref[...]
    pool_out = _dot(pm.astype(BF16), wpu_ref[...])
    d_model = pool_out.shape[1]
    g_pool = _dot(h, win_ref[:, pw + 3 * sw:pw + 3 * sw + d_model])
    p_ref[...] = (jax.nn.sigmoid(g_pool) * pool_out).astype(BF16)
    g_attn = _dot(h, win_ref[:, pw + 3 * sw + d_model:pw + 3 * sw + 2 * d_model])
    ga_ref[...] = jax.nn.sigmoid(g_attn).astype(BF16)

    def head_norm(t, gain):
        ss = _dot((t * t).astype(BF16), hsum_ref[...])
        return t * lax.rsqrt(ss * (1.0 / SB_HEAD_DIM) + EPS) * gain

    q = _dot(h, win_ref[:, pw:pw + sw])
    q_ref[...] = (head_norm(q, gq_ref[...]) * (SB_HEAD_DIM ** -0.5)).astype(BF16)
    k = _dot(h, win_ref[:, pw + sw:pw + 2 * sw])
    k_ref[...] = head_norm(k, gk_ref[...]).astype(BF16)
    v_ref[...] = _dot(h, win_ref[:, pw + 2 * sw:pw + 3 * sw]).astype(BF16)


def _attn_kernel(q_ref, k_ref, v_ref, tri_ref, o_ref, *scratch, n_pairs):
    qs, acc, rr = scratch[:n_pairs], scratch[n_pairs:2 * n_pairs], scratch[2 * n_pairs:]
    qi = pl.program_id(1)
    bq = ATTN_BLOCK
    first_head = lax.broadcasted_iota(I32, (bq, LANES), 1) < SB_HEAD_DIM
    for p in range(n_pairs):
        q2 = q_ref[:, p * LANES:(p + 1) * LANES]
        qs[p][:bq] = jnp.where(first_head, q2, jnp.zeros_like(q2))
        qs[p][bq:] = jnp.where(first_head, jnp.zeros_like(q2), q2)
    row = lax.broadcasted_iota(I32, (2 * bq, bq), 0)
    col = lax.broadcasted_iota(I32, (2 * bq, bq), 1)
    causal = col < (row & (bq - 1))
    contract_last = (((1,), (1,)), ((), ()))

    def softplus(z):
        return jnp.maximum(z, 0.0) + jnp.log(1.0 + jnp.exp(-jnp.abs(z)))

    def suffix_sums(sp):
        hi, lo = _split_bf16(sp)
        return _dot(jnp.concatenate([hi, lo], axis=1), tri_ref[...])

    def cols(ref, start, p):
        return ref[pl.ds(start, bq), p * LANES:(p + 1) * LANES]

    start_d = pl.multiple_of(qi * bq, bq)
    start_n = pl.multiple_of(jnp.maximum(qi - 1, 0) * bq, bq)
    has_n = qi >= 1
    z_d = [lax.dot_general(qs[p][...], cols(k_ref, start_d, p), contract_last, preferred_element_type=F32)
           for p in range(n_pairs)]
    z_n = [lax.dot_general(qs[p][...], cols(k_ref, start_n, p), contract_last, preferred_element_type=F32)
           for p in range(n_pairs)]
    s_d = [suffix_sums(jnp.where(causal, softplus(z), 0.0)) for z in z_d]
    s_n = [suffix_sums(jnp.where(has_n, softplus(z), 0.0)) for z in z_n]
    for p in range(n_pairs):
        a_d = jnp.where(causal, jnp.exp(z_d[p] - s_d[p][:, :bq]), 0.0)
        r_d = s_d[p][:, bq:]
        a_n = jnp.where(has_n, jnp.exp(z_n[p] - (r_d + s_n[p][:, :bq])), 0.0)
        acc[p][...] = (_dot(a_d.astype(BF16), cols(v_ref, start_d, p)) + _dot(a_n.astype(BF16), cols(v_ref, start_n, p)))
        rr[p][...] = r_d + s_n[p][:, bq:]

    def r_min():
        m = rr[0][...]
        for p in range(1, n_pairs):
            m = jnp.minimum(m, rr[p][...])
        return jnp.min(m)

    def cond(c):
        j, rm = c
        return jnp.logical_and(j >= 0, rm < ATTN_EXIT)

    def body(c):
        j, _ = c
        start = pl.multiple_of(j * bq, bq)
        zs = [lax.dot_general(qs[p][...], cols(k_ref, start, p), contract_last, preferred_element_type=F32)
              for p in range(n_pairs)]
        ss = [suffix_sums(softplus(z)) for z in zs]
        for p in range(n_pairs):
            r = rr[p][...]
            a = jnp.exp(zs[p] - (r + ss[p][:, :bq]))
            acc[p][...] += _dot(a.astype(BF16), cols(v_ref, start, p))
            rr[p][...] = r + ss[p][:, bq:]
        return j - 1, r_min()

    lax.while_loop(cond, body, (qi - 2, r_min()))
    for p in range(n_pairs):
        o_ref[:, p * LANES:(p + 1) * LANES] = jnp.where(first_head, acc[p][:bq], acc[p][bq:]).astype(BF16)


def _mixer_out_kernel(x_ref, p_ref, ga_ref, sba_ref, wau_ref, wout_ref, g2_ref, wr_hi_ref, wr_lo_ref, br_ref,
                      ltri_ref, x1_ref, hp_ref, ri_ref, gate_ref, cnt_ref, *, tm):
    step = pl.program_id(0)

    @pl.when(step == 0)
    def _():
        cnt_ref[...] = jnp.zeros_like(cnt_ref)

    attn_out = _dot(sba_ref[...], wau_ref[...])
    merged = p_ref[...].astype(F32) + ga_ref[...].astype(F32) * attn_out
    x1 = x_ref[...] + _dot(merged.astype(BF16), wout_ref[...])
    x1_ref[...] = x1
    ms = jnp.mean(x1 * x1, axis=-1, keepdims=True)
    h2 = x1 * lax.rsqrt(ms + EPS) * g2_ref[...]
    _pack_rows(h2, hp_ref)

    h_hi, h_lo = _split_bf16(h2)
    logits = (_dot(h_hi, wr_hi_ref[...]) + _dot(h_hi, wr_lo_ref[...]) + _dot(h_lo, wr_hi_ref[...])
              + br_ref[...])
    lane = lax.broadcasted_iota(I32, logits.shape, 1)
    work = logits
    vals, idxs = [], []
    for _ in range(TOP_K):
        m = jnp.max(work, axis=-1, keepdims=True)
        ik = jnp.min(jnp.where(work == m, lane, LANES), axis=-1, keepdims=True)
        vals.append(m)
        idxs.append(ik)
        work = jnp.where(lane == ik, -jnp.inf, work)
    es = [jnp.exp(v - vals[0]) for v in vals]
    denom = es[0] + es[1] + es[2] + es[3]
    hot = jnp.zeros(logits.shape, F32)
    for ik in idxs:
        hot = hot + (lane == ik).astype(F32)
    before = _dot(ltri_ref[...], hot.astype(BF16)) + cnt_ref[0:1, :]
    ri = jnp.zeros(logits.shape, I32)
    gt = jnp.zeros(logits.shape, F32)
    for kk in range(TOP_K):
        rank = jnp.sum(jnp.where(lane == idxs[kk], before, 0.0), axis=-1, keepdims=True).astype(I32)
        ri = jnp.where(lane == kk, idxs[kk], ri)
        ri = jnp.where(lane == TOP_K + kk, rank, ri)
        gt = jnp.where(lane == kk, es[kk] / denom, gt)
    ri_ref[...] = ri.T[:2 * TOP_K]
    gate_ref[...] = gt
    cnt_ref[...] = cnt_ref[...] + jnp.sum(hot, axis=0, keepdims=True)


def _sc_mesh():
    info = plsc.get_sparse_core_info()
    mesh = plsc.VectorSubcoreMesh(core_axis_name="c", subcore_axis_name="s")
    return mesh, info.num_cores, info.num_subcores, info.num_lanes


def _sc_scatter_rows(rows, idx, n_out):
    mesh, nc, ns, lanes = _sc_mesh()
    nw = nc * ns
    planes, m, width = rows.shape
    ch = SC_CHUNK
    n_ch = m // ch // nw
    half = planes // 2
    assert planes % 2 == 0 and n_ch * ch * nw == m and idx.shape == (m // ch * TOP_K, ch) and n_ch >= 2

    @functools.partial(
        pl.kernel, mesh=mesh, out_type=jax.ShapeDtypeStruct((planes * n_out, width), rows.dtype),
        scratch_types=([pltpu.VMEM((n_ch * TOP_K, ch), I32)] + [pltpu.VMEM((ch, width), rows.dtype)] * planes
                       + [pltpu.VMEM((TOP_K, ch), I32)] * planes
                       + [pltpu.SemaphoreType.DMA((planes,)), pltpu.SemaphoreType.DMA((planes,))]))
    def scatter_kernel(rows_hbm, idx_hbm, out_hbm, idx_v, *rest):
        bufs, ibufs, rsem, ssem = rest[:planes], rest[planes:2 * planes], rest[2 * planes], rest[2 * planes + 1]
        wid = lax.axis_index("s") * nc + lax.axis_index("c")
        pltpu.sync_copy(idx_hbm.at[pl.ds(wid * n_ch * TOP_K, n_ch * TOP_K)], idx_v)
        base = wid * n_ch * ch

        def read(j, b):
            return pltpu.make_async_copy(rows_hbm.at[b, pl.ds(pl.multiple_of(base + j * ch, ch), ch)], bufs[b], rsem.at[b])

        def scatters(b):
            return [pltpu.make_async_copy(bufs[b], out_hbm.at[ibufs[b].at[kk]], ssem.at[b]) for kk in range(TOP_K)]

        def start_scatters(j, b):
            for kk in range(TOP_K):
                for t in range(0, ch, lanes):
                    ibufs[b][kk, pl.ds(t, lanes)] = idx_v[j * TOP_K + kk, pl.ds(t, lanes)] + b * n_out
            for c in scatters(b):
                c.start()

        def wait_scatters(b):
            for c in scatters(b):
                c.wait()

        def finish(j, b):
            pj, pb = (j, b - half) if b >= half else (j - 1, b + half)
            read(pj, pb).wait()
            start_scatters(pj, pb)

        for b in range(planes):
            read(0, b).start()
        for b in range(half, planes):
            finish(0, b)

        @pl.loop(1, n_ch)
        def _(j):
            for b in range(planes):
                wait_scatters(b)
                read(j, b).start()
                finish(j, b)

        for b in range(half):
            finish(n_ch, b)
        for b in range(planes):
            wait_scatters(b)

    return scatter_kernel(rows, idx).reshape(planes, n_out, width)


def _padfill_kernel(start_ref, len_ref, xs_in_ref, xs_ref, zeros_ref, sem):
    del xs_in_ref
    zeros_ref[...] = jnp.zeros_like(zeros_ref)
    bits = [1 << b for b in reversed(range(EXPERT_BLOCK.bit_length() - 1))]

    def pieces(e):
        n = len_ref[e]
        for bit in bits:
            row0 = start_ref[e] + (n & ~(2 * bit - 1))
            copy = pltpu.make_async_copy(zeros_ref.at[:, pl.ds(0, bit)], xs_ref.at[:, pl.ds(row0, bit)], sem)
            yield (n & bit) != 0, copy

    def start(e, c):
        for on, copy in pieces(e):
            pl.when(on)(copy.start)
        return c

    def wait(e, c):
        for on, copy in pieces(e):
            pl.when(on)(copy.wait)
        return c

    lax.fori_loop(0, start_ref.shape[0], start, 0)
    lax.fori_loop(0, start_ref.shape[0], wait, 0)


def _experts_kernel(be_ref, nb_ref, xs_ref, wgu_ref, bgu_ref, wd_ref, bd_ref, ys_ref, wgu_bf_ref, wd_bf_ref):
    blk = pl.program_id(0)
    de = wd_ref.shape[1]

    @pl.when(jnp.logical_or(blk == 0, be_ref[blk] != be_ref[jnp.maximum(blk - 1, 0)]))
    def _():
        wgu_bf_ref[...] = wgu_ref[0].astype(BF16)
        wd_bf_ref[...] = wd_ref[0].astype(BF16)

    @pl.when(blk < nb_ref[0])
    def _():
        x = _unpack_rows(xs_ref).astype(BF16)
        gu = _dot(x, wgu_bf_ref[...]) + bgu_ref[0]
        glu = jnp.minimum(gu[:, :de], SWIGLU_LIMIT)
        lin = jnp.clip(gu[:, de:], -SWIGLU_LIMIT, SWIGLU_LIMIT)
        act = glu * jax.nn.sigmoid(SWIGLU_ALPHA * glu) * (lin + 1.0)
        y = _dot(act.astype(BF16), wd_bf_ref[...]) + bd_ref[0]
        _pack_rows(y, ys_ref)

    @pl.when(blk >= nb_ref[0])
    def _():
        ys_ref[...] = jnp.zeros_like(ys_ref)


def _sc_gather_rows(table, idx):
    mesh, nc, ns, lanes = _sc_mesh()
    nw = nc * ns
    planes, n_tab, width = table.shape
    n_idx_rows, ch = idx.shape
    n_ch = n_idx_rows // nw
    m = n_idx_rows * ch
    half = planes // 2
    assert planes % 2 == 0 and ch == SC_CHUNK and n_ch * nw == n_idx_rows and n_ch >= 2

    @functools.partial(
        pl.kernel, mesh=mesh, out_type=jax.ShapeDtypeStruct((planes, m, width), table.dtype),
        scratch_types=([pltpu.VMEM((n_ch, ch), I32)] + [pltpu.VMEM((ch, width), table.dtype)] * planes
                       + [pltpu.VMEM((8, ch), I32)] * planes
                       + [pltpu.SemaphoreType.DMA((planes,)), pltpu.SemaphoreType.DMA((planes,))]))
    def gather_kernel(table_hbm, idx_hbm, out_hbm, idx_v, *rest):
        bufs, ibufs, gsem, wsem = rest[:planes], rest[planes:2 * planes], rest[2 * planes], rest[2 * planes + 1]
        wid = lax.axis_index("s") * nc + lax.axis_index("c")
        pltpu.sync_copy(idx_hbm.at[pl.ds(wid * n_ch, n_ch)], idx_v)
        base = wid * n_ch * ch

        def gather(b):
            return pltpu.make_async_copy(table_hbm.at[ibufs[b].at[0]], bufs[b], gsem.at[b])

        def start_gather(j, b):
            for t in range(0, ch, lanes):
                ibufs[b][0, pl.ds(t, lanes)] = idx_v[j, pl.ds(t, lanes)] + b * n_tab
            gather(b).start()

        def write(j, b):
            return pltpu.make_async_copy(bufs[b], out_hbm.at[b, pl.ds(pl.multiple_of(base + j * ch, ch), ch)], wsem.at[b])

        def finish(j, b):
            pj, pb = (j, b - half) if b >= half else (j - 1, b + half)
            gather(pb).wait()
            write(pj, pb).start()

        for b in range(planes):
            start_gather(0, b)
        for b in range(half, planes):
            finish(0, b)

        @pl.loop(1, n_ch)
        def _(j):
            for b in range(planes):
                write(j - 1, b).wait()
                start_gather(j, b)
                finish(j, b)

        for b in range(half):
            finish(n_ch, b)
        for b in range(planes):
            write(n_ch - 1, b).wait()

    return gather_kernel(table.reshape(planes * n_tab, width), idx)


def _combine_kernel(x1_ref, gate_ref, y0_ref, y1_ref, y2_ref, y3_ref, o_ref):
    gate = gate_ref[...]
    out = x1_ref[...]
    for kk, y_ref in enumerate((y0_ref, y1_ref, y2_ref, y3_ref)):
        out = out + gate[:, kk:kk + 1] * _unpack_rows(y_ref)
    o_ref[...] = out


def _const_spec(shape):
    nd = len(shape)
    return pl.BlockSpec(shape, lambda *_: (0,) * nd)


def _layer(x, norm1_g, w_in, q_norm_g, k_norm_g, w_pool_grp, pool_scale, w_pool_up, w_attn_up, w_out, norm2_g,
           w_router, b_router, w_gate_up, b_gate_up, w_down, b_down):
    B, S, D = x.shape
    N = B * S
    pw = w_pool_up.shape[0]
    sw = w_attn_up.shape[0]
    n_exp = w_router.shape[1]
    de = w_down.shape[1]
    heads = sw // SB_HEAD_DIM
    assert pw == len(POOL_WINDOWS) * POOL_GROUP_DIM and heads % 2 == 0 and n_exp <= LANES
    assert D == 2 * PACK_ROWS * LANES and w_in.shape[1] == pw + 3 * sw + 2 * D
    tm = 512 if S % 512 == 0 else 256
    assert S % tm == 0 and S % ATTN_BLOCK == 0
    xf = x.reshape(N, D)
    cparams = functools.partial(pltpu.CompilerParams, vmem_limit_bytes=VMEM_LIMIT)

    hsum = (jnp.arange(sw)[:, None] // SB_HEAD_DIM == jnp.arange(sw)[None, :] // SB_HEAD_DIM).astype(BF16)
    nt = S // tm
    tok_spec = lambda w: pl.BlockSpec((tm, w), lambda b, i: (b * nt + i, 0))
    q2, k2, v2, ga, pg = pl.pallas_call(
        functools.partial(_mixer_in_kernel, tm=tm, pw=pw, sw=sw),
        grid=(B, nt),
        in_specs=[tok_spec(D), _const_spec((1, D)), _const_spec((D, w_in.shape[1])), _const_spec((1, sw)),
                  _const_spec((1, sw)), _const_spec((sw, sw)),
                  _const_spec((len(POOL_WINDOWS), POOL_GROUP_DIM, POOL_GROUP_DIM)), _const_spec((1, pw)),
                  _const_spec((pw, D))],
        out_specs=[tok_spec(sw), tok_spec(sw), tok_spec(sw), tok_spec(D), tok_spec(D)],
        out_shape=[jax.ShapeDtypeStruct((N, sw), BF16)] * 3 + [jax.ShapeDtypeStruct((N, D), BF16)] * 2,
        scratch_shapes=[pltpu.VMEM((POOL_HALO, pw), F32)],
        compiler_params=cparams(dimension_semantics=("arbitrary", "arbitrary")),
        name="mixer_in",
    )(xf, norm1_g.reshape(1, D), w_in.astype(BF16), jnp.tile(q_norm_g, heads).reshape(1, sw),
      jnp.tile(k_norm_g, heads).reshape(1, sw), hsum, w_pool_grp.astype(BF16), pool_scale.reshape(1, pw),
      w_pool_up.astype(BF16))

    bq = ATTN_BLOCK
    nq = S // bq
    jj = jnp.arange(bq)
    tri_half = jnp.concatenate([(jj[:, None] >= jj[None, :]).astype(BF16), jnp.ones((bq, bq), BF16)], axis=1)
    tri = jnp.concatenate([tri_half, tri_half], axis=0)
    n_pairs = heads // 2
    kv_spec = pl.BlockSpec((S, sw), lambda b, qi: (b, 0), pipeline_mode=pl.Buffered(1))
    sba = pl.pallas_call(
        functools.partial(_attn_kernel, n_pairs=n_pairs),
        grid=(B, nq),
        in_specs=[pl.BlockSpec((bq, sw), lambda b, qi: (b * nq + qi, 0)), kv_spec, kv_spec,
                  _const_spec((2 * bq, 2 * bq))],
        out_specs=pl.BlockSpec((bq, sw), lambda b, qi: (b * nq + qi, 0)),
        out_shape=jax.ShapeDtypeStruct((N, sw), BF16),
        scratch_shapes=([pltpu.VMEM((2 * bq, LANES), BF16)] * n_pairs + [pltpu.VMEM((2 * bq, LANES), F32)] * (2 * n_pairs)),
        compiler_params=cparams(dimension_semantics=("arbitrary", "arbitrary")),
        name="sb_attn",
    )(q2, k2, v2, tri)

    wr = jnp.zeros((D, LANES), F32).at[:, :n_exp].set(w_router)
    wr_hi = wr.astype(BF16)
    wr_lo = (wr - wr_hi.astype(F32)).astype(BF16)
    br = jnp.full((1, LANES), -jnp.inf, F32).at[0, :n_exp].set(b_router)
    ltri = (jnp.arange(tm)[:, None] > jnp.arange(tm)[None, :]).astype(BF16)
    row_spec = lambda w: pl.BlockSpec((tm, w), lambda i: (i, 0))
    plane_spec = lambda rows, index: pl.BlockSpec((PACK_ROWS, rows, LANES), lambda i, *_: (0, index(i, *_), 0))
    x1, hpk, ri, gate, cnt = pl.pallas_call(
        functools.partial(_mixer_out_kernel, tm=tm),
        grid=(N // tm,),
        in_specs=[row_spec(D), row_spec(D), row_spec(D), row_spec(sw), _const_spec((sw, D)), _const_spec((D, D)),
                  _const_spec((1, D)), _const_spec((D, LANES)), _const_spec((D, LANES)), _const_spec((1, LANES)),
                  _const_spec((tm, tm))],
        out_specs=[row_spec(D), plane_spec(tm, lambda i: i), pl.BlockSpec((2 * TOP_K, tm), lambda i: (0, i)),
                   row_spec(LANES), _const_spec((8, LANES))],
        out_shape=[jax.ShapeDtypeStruct((N, D), F32), jax.ShapeDtypeStruct((PACK_ROWS, N, LANES), U32),
                   jax.ShapeDtypeStruct((2 * TOP_K, N), I32), jax.ShapeDtypeStruct((N, LANES), F32),
                   jax.ShapeDtypeStruct((8, LANES), F32)],
        compiler_params=cparams(dimension_semantics=("arbitrary",)),
        name="mixer_out",
    )(xf, pg, ga, sba, w_attn_up.astype(BF16), w_out.astype(BF16), norm2_g.reshape(1, D), wr_hi, wr_lo, br, ltri)

    n_assign = N * TOP_K
    n_blocks = -(-(n_assign + n_exp * (EXPERT_BLOCK - 1)) // EXPERT_BLOCK)
    counts = cnt[0, :n_exp].astype(I32)
    padded = (counts + EXPERT_BLOCK - 1) // EXPERT_BLOCK * EXPERT_BLOCK
    padded_end = jnp.cumsum(padded)
    start_pad = padded_end - padded
    idx = ri[:TOP_K]
    dest = ri[TOP_K:]
    for e in range(n_exp):
        dest = dest + jnp.where(idx == e, start_pad[e], 0)
    block_start = jnp.arange(n_blocks, dtype=I32) * EXPERT_BLOCK
    block_expert = jnp.minimum(jnp.sum((padded_end[None, :] <= block_start[:, None]).astype(I32), axis=1), n_exp - 1)
    n_used = (padded_end[-1] // EXPERT_BLOCK).astype(I32).reshape(1)

    n_rows = n_blocks * EXPERT_BLOCK
    dst_idx = dest.reshape(TOP_K, N // SC_CHUNK, SC_CHUNK).transpose(1, 0, 2).reshape(-1, SC_CHUNK)
    xs = _sc_scatter_rows(hpk, dst_idx, n_rows)
    xs = pl.pallas_call(
        _padfill_kernel,
        grid_spec=pltpu.PrefetchScalarGridSpec(
            num_scalar_prefetch=2,
            grid=(1,),
            in_specs=[pl.BlockSpec(memory_space=pl.ANY)],
            out_specs=pl.BlockSpec(memory_space=pl.ANY),
            scratch_shapes=[pltpu.VMEM((PACK_ROWS, EXPERT_BLOCK // 2, LANES), U32), pltpu.SemaphoreType.DMA(())],
        ),
        out_shape=jax.ShapeDtypeStruct((PACK_ROWS, n_rows, LANES), U32),
        input_output_aliases={2: 0},
        compiler_params=cparams(dimension_semantics=("arbitrary",)),
        name="padfill",
    )(start_pad + counts, padded - counts, xs)

    ys = pl.pallas_call(
        _experts_kernel,
        grid_spec=pltpu.PrefetchScalarGridSpec(
            num_scalar_prefetch=2,
            grid=(n_blocks,),
            in_specs=[plane_spec(EXPERT_BLOCK, lambda i, be, nb: jnp.minimum(i, nb[0] - 1)),
                      pl.BlockSpec((1, D, 2 * de), lambda i, be, nb: (be[i], 0, 0)),
                      pl.BlockSpec((1, 1, 2 * de), lambda i, be, nb: (be[i], 0, 0)),
                      pl.BlockSpec((1, de, D), lambda i, be, nb: (be[i], 0, 0)),
                      pl.BlockSpec((1, 1, D), lambda i, be, nb: (be[i], 0, 0))],
            out_specs=plane_spec(EXPERT_BLOCK, lambda i, be, nb: i),
            scratch_shapes=[pltpu.VMEM((D, 2 * de), BF16), pltpu.VMEM((de, D), BF16)],
        ),
        out_shape=jax.ShapeDtypeStruct((PACK_ROWS, n_rows, LANES), U32),
        compiler_params=cparams(dimension_semantics=("arbitrary",)),
        name="experts",
    )(block_expert, n_used, xs, w_gate_up, b_gate_up.reshape(n_exp, 1, 2 * de), w_down, b_down.reshape(n_exp, 1, D))

    y4 = _sc_gather_rows(ys, dest.reshape(-1, SC_CHUNK))

    out = pl.pallas_call(
        _combine_kernel,
        grid=(N // tm,),
        in_specs=[row_spec(D), row_spec(LANES)] + [plane_spec(tm, lambda i, kk=kk: kk * (N // tm) + i)
                                                     for kk in range(TOP_K)],
        out_specs=row_spec(D),
        out_shape=jax.ShapeDtypeStruct((N, D), F32),
        compiler_params=cparams(dimension_semantics=("arbitrary",)),
        name="combine",
    )(x1, gate, y4, y4, y4, y4)
    return out.reshape(B, S, D)


def kernel(x, norm1_g, w_in, q_norm_g, k_norm_g, w_pool_grp, pool_scale, w_pool_up, w_attn_up, w_out, norm2_g,
           w_router, b_router, w_gate_up, b_gate_up, w_down, b_down):
    for layer in range(norm1_g.shape[0]):
        x = _layer(x, norm1_g[layer], w_in[layer], q_norm_g[layer], k_norm_g[layer], w_pool_grp[layer],
                   pool_scale[layer], w_pool_up[layer], w_attn_up[layer], w_out[layer], norm2_g[layer],
                   w_router[layer], b_router[layer], w_gate_up[layer], b_gate_up[layer], w_down[layer],
                   b_down[layer])
    return x
```

```python
import functools

import jax
import jax.numpy as jnp
from jax import lax
from jax.experimental import pallas as pl
from jax.experimental.pallas import tpu as pltpu
from jax.experimental.pallas import tpu_sc as plsc

F32 = jnp.float32
BF16 = jnp.bfloat16
U32 = jnp.uint32
I32 = jnp.int32

EPS = 1e-6
POOL_WINDOWS = (2, 4, 8, 16)
POOL_GROUP_DIM = 128
POOL_HALO = 16
SB_HEAD_DIM = 64
TOP_K = 4
SWIGLU_LIMIT = 7.0
SWIGLU_ALPHA = 1.702
EXPERT_BLOCK = 256
LANES = 128
PACK_ROWS = 4
ATTN_BLOCK = 128
ATTN_EXIT = 48.0
VMEM_LIMIT = 56 * 1024 * 1024
SC_CHUNK = 128
MOE_GROUPS = 2


def _dot(a, b):
    return jnp.dot(a, b, preferred_element_type=F32)


def _split_bf16(x):
    hi = x.astype(BF16)
    lo = (x - hi.astype(F32)).astype(BF16)
    return hi, lo


def _pack_rows(v, out_ref):
    half = v.shape[1] // 2
    lo = lax.bitcast_convert_type(v[:, :half].astype(BF16).astype(F32), U32) >> 16
    hi = lax.bitcast_convert_type(v[:, half:].astype(BF16).astype(F32), U32) & jnp.uint32(0xFFFF0000)
    w = lo | hi
    for c in range(PACK_ROWS):
        out_ref[c] = w[:, c * LANES:(c + 1) * LANES]


def _unpack_rows(ref):
    los, his = [], []
    for c in range(PACK_ROWS):
        w = ref[c]
        los.append(lax.bitcast_convert_type(w << 16, F32))
        his.append(lax.bitcast_convert_type(w & jnp.uint32(0xFFFF0000), F32))
    return jnp.concatenate(los + his, axis=1)


def _mixer_in_kernel(x_ref, g1_ref, win_ref, gq_ref, gk_ref, hsum_ref, wgrp_ref, pscale_ref, wpu_ref,
                     q_ref, k_ref, v_ref, ga_ref, p_ref, tail_ref, *, tm, pw, sw):
    i = pl.program_id(1)

    @pl.when(i == 0)
    def _():
        tail_ref[...] = jnp.zeros_like(tail_ref)

    x = x_ref[...]
    ms = jnp.mean(x * x, axis=-1, keepdims=True)
    h = (x * lax.rsqrt(ms + EPS) * g1_ref[...]).astype(BF16)

    u = _dot(h, win_ref[:, 0:pw])
    xx = jnp.concatenate([tail_ref[...], u], axis=0)
    tail_ref[...] = u[tm - POOL_HALO:, :]
    pos = i * tm + lax.broadcasted_iota(I32, (tm, POOL_GROUP_DIM), 0)
    mixed = []
    for g, w in enumerate(POOL_WINDOWS):
        s = xx[:, g * POOL_GROUP_DIM:(g + 1) * POOL_GROUP_DIM]
        step = 1
        while step < w:
            s = s + pltpu.roll(s, step, axis=0)
            step *= 2
        count = jnp.minimum(pos + 1, w).astype(F32)
        ug = u[:, g * POOL_GROUP_DIM:(g + 1) * POOL_GROUP_DIM]
        d = s[POOL_HALO:, :] / count - ug
        mixed.append(_dot(d.astype(BF16), wgrp_ref[g]))
    pm = jnp.concatenate(mixed, axis=1) * pscale_ref[...]
    pool_out = _dot(pm.astype(BF16), wpu_ref[...])
    d_model = pool_out.shape[1]
    g_pool = _dot(h, win_ref[:, pw + 3 * sw:pw + 3 * sw + d_model])
    p_ref[...] = (jax.nn.sigmoid(g_pool) * pool_out).astype(BF16)
    g_attn = _dot(h, win_ref[:, pw + 3 * sw + d_model:pw + 3 * sw + 2 * d_model])
    ga_ref[...] = jax.nn.sigmoid(g_attn).astype(BF16)

    def head_norm(t, gain):
        ss = _dot((t * t).astype(BF16), hsum_ref[...])
        return t * lax.rsqrt(ss * (1.0 / SB_HEAD_DIM) + EPS) * gain

    q = _dot(h, win_ref[:, pw:pw + sw])
    q_ref[...] = (head_norm(q, gq_ref[...]) * (SB_HEAD_DIM ** -0.5)).astype(BF16)
    k = _dot(h, win_ref[:, pw + sw:pw + 2 * sw])
    k_ref[...] = head_norm(k, gk_ref[...]).astype(BF16)
    v_ref[...] = _dot(h, win_ref[:, pw + 2 * sw:pw + 3 * sw]).astype(BF16)


def _attn_kernel(q_ref, k_ref, v_ref, tri_ref, o_ref, *scratch, n_pairs):
    qs, acc, rr = scratch[:n_pairs], scratch[n_pairs:2 * n_pairs], scratch[2 * n_pairs:]
    qi = pl.program_id(1)
    bq = ATTN_BLOCK
    first_head = lax.broadcasted_iota(I32, (bq, LANES), 1) < SB_HEAD_DIM
    for p in range(n_pairs):
        q2 = q_ref[:, p * LANES:(p + 1) * LANES]
        qs[p][:bq] = jnp.where(first_head, q2, jnp.zeros_like(q2))
        qs[p][bq:] = jnp.where(first_head, jnp.zeros_like(q2), q2)
    row = lax.broadcasted_iota(I32, (2 * bq, bq), 0)
    col = lax.broadcasted_iota(I32, (2 * bq, bq), 1)
    causal = col < (row & (bq - 1))
    contract_last = (((1,), (1,)), ((), ()))

    def softplus(z):
        return jnp.maximum(z, 0.0) + jnp.log(1.0 + jnp.exp(-jnp.abs(z)))

    def suffix_sums(sp):
        hi, lo = _split_bf16(sp)
        return _dot(jnp.concatenate([hi, lo], axis=1), tri_ref[...])

    def cols(ref, start, p):
        return ref[pl.ds(start, bq), p * LANES:(p + 1) * LANES]

    start_d = pl.multiple_of(qi * bq, bq)
    start_n = pl.multiple_of(jnp.maximum(qi - 1, 0) * bq, bq)
    has_n = qi >= 1
    z_d = [lax.dot_general(qs[p][...], cols(k_ref, start_d, p), contract_last, preferred_element_type=F32)
           for p in range(n_pairs)]
    z_n = [lax.dot_general(qs[p][...], cols(k_ref, start_n, p), contract_last, preferred_element_type=F32)
           for p in range(n_pairs)]
    s_d = [suffix_sums(jnp.where(causal, softplus(z), 0.0)) for z in z_d]
    s_n = [suffix_sums(jnp.where(has_n, softplus(z), 0.0)) for z in z_n]
    for p in range(n_pairs):
        a_d = jnp.where(causal, jnp.exp(z_d[p] - s_d[p][:, :bq]), 0.0)
        r_d = s_d[p][:, bq:]
        a_n = jnp.where(has_n, jnp.exp(z_n[p] - (r_d + s_n[p][:, :bq])), 0.0)
        acc[p][...] = (_dot(a_d.astype(BF16), cols(v_ref, start_d, p)) + _dot(a_n.astype(BF16), cols(v_ref, start_n, p)))
        rr[p][...] = r_d + s_n[p][:, bq:]

    def r_min():
        m = rr[0][...]
        for p in range(1, n_pairs):
            m = jnp.minimum(m, rr[p][...])
        return jnp.min(m)

    def cond(c):
        j, rm = c
        return jnp.logical_and(j >= 0, rm < ATTN_EXIT)

    def body(c):
        j, _ = c
        start = pl.multiple_of(j * bq, bq)
        zs = [lax.dot_general(qs[p][...], cols(k_ref, start, p), contract_last, preferred_element_type=F32)
              for p in range(n_pairs)]
        ss = [suffix_sums(softplus(z)) for z in zs]
        for p in range(n_pairs):
            r = rr[p][...]
            a = jnp.exp(zs[p] - (r + ss[p][:, :bq]))
            acc[p][...] += _dot(a.astype(BF16), cols(v_ref, start, p))
            rr[p][...] = r + ss[p][:, bq:]
        return j - 1, r_min()

    lax.while_loop(cond, body, (qi - 2, r_min()))
    for p in range(n_pairs):
        o_ref[:, p * LANES:(p + 1) * LANES] = jnp.where(first_head, acc[p][:bq], acc[p][bq:]).astype(BF16)


def _mixer_out_kernel(x_ref, p_ref, ga_ref, sba_ref, wau_ref, wout_ref, g2_ref, wr_hi_ref, wr_lo_ref, br_ref,
                      ltri_ref, x1_ref, hp_ref, ri_ref, gate_ref, cnt_ref, *, tm):
    step = pl.program_id(0)

    @pl.when(step == 0)
    def _():
        cnt_ref[...] = jnp.zeros_like(cnt_ref)

    attn_out = _dot(sba_ref[...], wau_ref[...])
    merged = p_ref[...].astype(F32) + ga_ref[...].astype(F32) * attn_out
    x1 = x_ref[...] + _dot(merged.astype(BF16), wout_ref[...])
    x1_ref[...] = x1
    ms = jnp.mean(x1 * x1, axis=-1, keepdims=True)
    h2 = x1 * lax.rsqrt(ms + EPS) * g2_ref[...]
    _pack_rows(h2, hp_ref)

    h_hi, h_lo = _split_bf16(h2)
    logits = (_dot(h_hi, wr_hi_ref[...]) + _dot(h_hi, wr_lo_ref[...]) + _dot(h_lo, wr_hi_ref[...])
              + br_ref[...])
    lane = lax.broadcasted_iota(I32, logits.shape, 1)
    work = logits
    vals, idxs = [], []
    for _ in range(TOP_K):
        m = jnp.max(work, axis=-1, keepdims=True)
        ik = jnp.min(jnp.where(work == m, lane, LANES), axis=-1, keepdims=True)
        vals.append(m)
        idxs.append(ik)
        work = jnp.where(lane == ik, -jnp.inf, work)
    es = [jnp.exp(v - vals[0]) for v in vals]
    denom = es[0] + es[1] + es[2] + es[3]
    hot = jnp.zeros(logits.shape, F32)
    for ik in idxs:
        hot = hot + (lane == ik).astype(F32)
    before = _dot(ltri_ref[...], hot.astype(BF16)) + cnt_ref[0:1, :]
    ri = jnp.zeros(logits.shape, I32)
    gt = jnp.zeros(logits.shape, F32)
    for kk in range(TOP_K):
        rank = jnp.sum(jnp.where(lane == idxs[kk], before, 0.0), axis=-1, keepdims=True).astype(I32)
        ri = jnp.where(lane == kk, idxs[kk], ri)
        ri = jnp.where(lane == TOP_K + kk, rank, ri)
        gt = jnp.where(lane == kk, es[kk] / denom, gt)
    ri_ref[...] = ri.T[:2 * TOP_K]
    gate_ref[...] = gt
    cnt_ref[...] = cnt_ref[...] + jnp.sum(hot, axis=0, keepdims=True)


def _sc_mesh():
    info = plsc.get_sparse_core_info()
    mesh = plsc.VectorSubcoreMesh(core_axis_name="c", subcore_axis_name="s")
    return mesh, info.num_cores, info.num_subcores, info.num_lanes


def _sc_scatter_rows(rows, idx, n_out):
    mesh, nc, ns, lanes = _sc_mesh()
    nw = nc * ns
    planes, m, width = rows.shape
    ch = SC_CHUNK
    n_ch = m // ch // nw
    half = planes // 2
    assert planes % 2 == 0 and n_ch * ch * nw == m and idx.shape == (m // ch * TOP_K, ch) and n_ch >= 2

    @functools.partial(
        pl.kernel, mesh=mesh, out_type=jax.ShapeDtypeStruct((planes * n_out, width), rows.dtype),
        scratch_types=([pltpu.VMEM((n_ch * TOP_K, ch), I32)] + [pltpu.VMEM((ch, width), rows.dtype)] * planes
                       + [pltpu.VMEM((TOP_K, ch), I32)] * planes
                       + [pltpu.SemaphoreType.DMA((planes,)), pltpu.SemaphoreType.DMA((planes,))]))
    def scatter_kernel(rows_hbm, idx_hbm, out_hbm, idx_v, *rest):
        bufs, ibufs, rsem, ssem = rest[:planes], rest[planes:2 * planes], rest[2 * planes], rest[2 * planes + 1]
        wid = lax.axis_index("s") * nc + lax.axis_index("c")
        pltpu.sync_copy(idx_hbm.at[pl.ds(wid * n_ch * TOP_K, n_ch * TOP_K)], idx_v)
        base = wid * n_ch * ch

        def read(j, b):
            return pltpu.make_async_copy(rows_hbm.at[b, pl.ds(pl.multiple_of(base + j * ch, ch), ch)], bufs[b], rsem.at[b])

        def scatters(b):
            return [pltpu.make_async_copy(bufs[b], out_hbm.at[ibufs[b].at[kk]], ssem.at[b]) for kk in range(TOP_K)]

        def start_scatters(j, b):
            for kk in range(TOP_K):
                for t in range(0, ch, lanes):
                    ibufs[b][kk, pl.ds(t, lanes)] = idx_v[j * TOP_K + kk, pl.ds(t, lanes)] + b * n_out
            for c in scatters(b):
                c.start()

        def wait_scatters(b):
            for c in scatters(b):
                c.wait()

        def finish(j, b):
            pj, pb = (j, b - half) if b >= half else (j - 1, b + half)
            read(pj, pb).wait()
            start_scatters(pj, pb)

        for b in range(planes):
            read(0, b).start()
        for b in range(half, planes):
            finish(0, b)

        @pl.loop(1, n_ch)
        def _(j):
            for b in range(planes):
                wait_scatters(b)
                read(j, b).start()
                finish(j, b)

        for b in range(half):
            finish(n_ch, b)
        for b in range(planes):
            wait_scatters(b)

    return scatter_kernel(rows, idx).reshape(planes, n_out, width)


def _padfill_kernel(start_ref, len_ref, xs_in_ref, xs_ref, zeros_ref, sem):
    del xs_in_ref
    zeros_ref[...] = jnp.zeros_like(zeros_ref)
    bits = [1 << b for b in reversed(range(EXPERT_BLOCK.bit_length() - 1))]

    def pieces(e):
        n = len_ref[e]
        for bit in bits:
            row0 = start_ref[e] + (n & ~(2 * bit - 1))
            copy = pltpu.make_async_copy(zeros_ref.at[:, pl.ds(0, bit)], xs_ref.at[:, pl.ds(row0, bit)], sem)
            yield (n & bit) != 0, copy

    def start(e, c):
        for on, copy in pieces(e):
            pl.when(on)(copy.start)
        return c

    def wait(e, c):
        for on, copy in pieces(e):
            pl.when(on)(copy.wait)
        return c

    lax.fori_loop(0, start_ref.shape[0], start, 0)
    lax.fori_loop(0, start_ref.shape[0], wait, 0)


def _experts_kernel(be_ref, nb_ref, xs_ref, wgu_ref, bgu_ref, wd_ref, bd_ref, ys_ref, wgu_bf_ref, wd_bf_ref):
    blk = pl.program_id(0)
    de = wd_ref.shape[1]

    @pl.when(jnp.logical_or(blk == 0, be_ref[blk] != be_ref[jnp.maximum(blk - 1, 0)]))
    def _():
        wgu_bf_ref[...] = wgu_ref[0].astype(BF16)
        wd_bf_ref[...] = wd_ref[0].astype(BF16)

    @pl.when(blk < nb_ref[0])
    def _():
        x = _unpack_rows(xs_ref).astype(BF16)
        gu = _dot(x, wgu_bf_ref[...]) + bgu_ref[0]
        glu = jnp.minimum(gu[:, :de], SWIGLU_LIMIT)
        lin = jnp.clip(gu[:, de:], -SWIGLU_LIMIT, SWIGLU_LIMIT)
        act = glu * jax.nn.sigmoid(SWIGLU_ALPHA * glu) * (lin + 1.0)
        y = _dot(act.astype(BF16), wd_bf_ref[...]) + bd_ref[0]
        _pack_rows(y, ys_ref)

    @pl.when(blk >= nb_ref[0])
    def _():
        ys_ref[...] = jnp.zeros_like(ys_ref)


def _sc_gather_rows(table, idx):
    mesh, nc, ns, lanes = _sc_mesh()
    nw = nc * ns
    planes, n_tab, width = table.shape
    n_idx_rows, ch = idx.shape
    n_ch = n_idx_rows // nw
    m = n_idx_rows * ch
    half = planes // 2
    assert planes % 2 == 0 and ch == SC_CHUNK and n_ch * nw == n_idx_rows and n_ch >= 2

    @functools.partial(
        pl.kernel, mesh=mesh, out_type=jax.ShapeDtypeStruct((planes, m, width), table.dtype),
        scratch_types=([pltpu.VMEM((n_ch, ch), I32)] + [pltpu.VMEM((ch, width), table.dtype)] * planes
                       + [pltpu.VMEM((8, ch), I32)] * planes
                       + [pltpu.SemaphoreType.DMA((planes,)), pltpu.SemaphoreType.DMA((planes,))]))
    def gather_kernel(table_hbm, idx_hbm, out_hbm, idx_v, *rest):
        bufs, ibufs, gsem, wsem = rest[:planes], rest[planes:2 * planes], rest[2 * planes], rest[2 * planes + 1]
        wid = lax.axis_index("s") * nc + lax.axis_index("c")
        pltpu.sync_copy(idx_hbm.at[pl.ds(wid * n_ch, n_ch)], idx_v)
        base = wid * n_ch * ch

        def gather(b):
            return pltpu.make_async_copy(table_hbm.at[ibufs[b].at[0]], bufs[b], gsem.at[b])

        def start_gather(j, b):
            for t in range(0, ch, lanes):
                ibufs[b][0, pl.ds(t, lanes)] = idx_v[j, pl.ds(t, lanes)] + b * n_tab
            gather(b).start()

        def write(j, b):
            return pltpu.make_async_copy(bufs[b], out_hbm.at[b, pl.ds(pl.multiple_of(base + j * ch, ch), ch)], wsem.at[b])

        def finish(j, b):
            pj, pb = (j, b - half) if b >= half else (j - 1, b + half)
            gather(pb).wait()
            write(pj, pb).start()

        for b in range(planes):
            start_gather(0, b)
        for b in range(half, planes):
            finish(0, b)

        @pl.loop(1, n_ch)
        def _(j):
            for b in range(planes):
                write(j - 1, b).wait()
                start_gather(j, b)
                finish(j, b)

        for b in range(half):
            finish(n_ch, b)
        for b in range(planes):
            write(n_ch - 1, b).wait()

    return gather_kernel(table.reshape(planes * n_tab, width), idx)


def _combine_kernel(x1_ref, gate_ref, y0_ref, y1_ref, y2_ref, y3_ref, *rest):
    o_ref = rest[-1]
    gate = gate_ref[...]
    out = x1_ref[...]
    for kk, y_ref in enumerate((y0_ref, y1_ref, y2_ref, y3_ref)):
        out = out + gate[:, kk:kk + 1] * _unpack_rows(y_ref)
    o_ref[...] = out


def _const_spec(shape):
    nd = len(shape)
    return pl.BlockSpec(shape, lambda *_: (0,) * nd)


def _layer(x, norm1_g, w_in, q_norm_g, k_norm_g, w_pool_grp, pool_scale, w_pool_up, w_attn_up, w_out, norm2_g,
           w_router, b_router, w_gate_up, b_gate_up, w_down, b_down):
    B, S, D = x.shape
    N = B * S
    pw = w_pool_up.shape[0]
    sw = w_attn_up.shape[0]
    n_exp = w_router.shape[1]
    de = w_down.shape[1]
    heads = sw // SB_HEAD_DIM
    assert pw == len(POOL_WINDOWS) * POOL_GROUP_DIM and heads % 2 == 0 and n_exp <= LANES
    assert D == 2 * PACK_ROWS * LANES and w_in.shape[1] == pw + 3 * sw + 2 * D
    tm = 512 if S % 512 == 0 else 256
    assert S % tm == 0 and S % ATTN_BLOCK == 0
    xf = x.reshape(N, D)
    cparams = functools.partial(pltpu.CompilerParams, vmem_limit_bytes=VMEM_LIMIT)

    hsum = (jnp.arange(sw)[:, None] // SB_HEAD_DIM == jnp.arange(sw)[None, :] // SB_HEAD_DIM).astype(BF16)
    nt = S // tm
    tok_spec = lambda w: pl.BlockSpec((tm, w), lambda b, i: (b * nt + i, 0))
    q2, k2, v2, ga, pg = pl.pallas_call(
        functools.partial(_mixer_in_kernel, tm=tm, pw=pw, sw=sw),
        grid=(B, nt),
        in_specs=[tok_spec(D), _const_spec((1, D)), _const_spec((D, w_in.shape[1])), _const_spec((1, sw)),
                  _const_spec((1, sw)), _const_spec((sw, sw)),
                  _const_spec((len(POOL_WINDOWS), POOL_GROUP_DIM, POOL_GROUP_DIM)), _const_spec((1, pw)),
                  _const_spec((pw, D))],
        out_specs=[tok_spec(sw), tok_spec(sw), tok_spec(sw), tok_spec(D), tok_spec(D)],
        out_shape=[jax.ShapeDtypeStruct((N, sw), BF16)] * 3 + [jax.ShapeDtypeStruct((N, D), BF16)] * 2,
        scratch_shapes=[pltpu.VMEM((POOL_HALO, pw), F32)],
        compiler_params=cparams(dimension_semantics=("arbitrary", "arbitrary")),
        name="mixer_in",
    )(xf, norm1_g.reshape(1, D), w_in.astype(BF16), jnp.tile(q_norm_g, heads).reshape(1, sw),
      jnp.tile(k_norm_g, heads).reshape(1, sw), hsum, w_pool_grp.astype(BF16), pool_scale.reshape(1, pw),
      w_pool_up.astype(BF16))

    bq = ATTN_BLOCK
    nq = S // bq
    jj = jnp.arange(bq)
    tri_half = jnp.concatenate([(jj[:, None] >= jj[None, :]).astype(BF16), jnp.ones((bq, bq), BF16)], axis=1)
    tri = jnp.concatenate([tri_half, tri_half], axis=0)
    n_pairs = heads // 2
    kv_spec = pl.BlockSpec((S, sw), lambda b, qi: (b, 0), pipeline_mode=pl.Buffered(1))
    sba = pl.pallas_call(
        functools.partial(_attn_kernel, n_pairs=n_pairs),
        grid=(B, nq),
        in_specs=[pl.BlockSpec((bq, sw), lambda b, qi: (b * nq + qi, 0)), kv_spec, kv_spec,
                  _const_spec((2 * bq, 2 * bq))],
        out_specs=pl.BlockSpec((bq, sw), lambda b, qi: (b * nq + qi, 0)),
        out_shape=jax.ShapeDtypeStruct((N, sw), BF16),
        scratch_shapes=([pltpu.VMEM((2 * bq, LANES), BF16)] * n_pairs + [pltpu.VMEM((2 * bq, LANES), F32)] * (2 * n_pairs)),
        compiler_params=cparams(dimension_semantics=("arbitrary", "arbitrary")),
        name="sb_attn",
    )(q2, k2, v2, tri)

    wr = jnp.zeros((D, LANES), F32).at[:, :n_exp].set(w_router)
    wr_hi = wr.astype(BF16)
    wr_lo = (wr - wr_hi.astype(F32)).astype(BF16)
    br = jnp.full((1, LANES), -jnp.inf, F32).at[0, :n_exp].set(b_router)
    ltri = (jnp.arange(tm)[:, None] > jnp.arange(tm)[None, :]).astype(BF16)
    assert N % (MOE_GROUPS * tm) == 0
    ng = N // MOE_GROUPS
    steps = ng // tm
    n_assign = ng * TOP_K
    n_blocks = -(-(n_assign + n_exp * (EXPERT_BLOCK - 1)) // EXPERT_BLOCK)
    n_rows = n_blocks * EXPERT_BLOCK
    plane_spec = lambda rows, index: pl.BlockSpec((PACK_ROWS, rows, LANES), lambda i, *_: (0, index(i, *_), 0))
    row_spec = lambda w: pl.BlockSpec((tm, w), lambda i: (i, 0))
    w_attn_up_bf, w_out_bf, g2 = w_attn_up.astype(BF16), w_out.astype(BF16), norm2_g.reshape(1, D)
    bgu, bdn = b_gate_up.reshape(n_exp, 1, 2 * de), b_down.reshape(n_exp, 1, D)
    out = None
    for grp in range(MOE_GROUPS):
        grp_spec = lambda w, first=grp * steps: pl.BlockSpec((tm, w), lambda i: (first + i, 0))

        x1, hpk, ri, gate, cnt = pl.pallas_call(
            functools.partial(_mixer_out_kernel, tm=tm),
            grid=(steps,),
            in_specs=[grp_spec(D), grp_spec(D), grp_spec(D), grp_spec(sw), _const_spec((sw, D)), _const_spec((D, D)),
                      _const_spec((1, D)), _const_spec((D, LANES)), _const_spec((D, LANES)), _const_spec((1, LANES)),
                      _const_spec((tm, tm))],
            out_specs=[row_spec(D), plane_spec(tm, lambda i: i), pl.BlockSpec((2 * TOP_K, tm), lambda i: (0, i)),
                       row_spec(LANES), _const_spec((8, LANES))],
            out_shape=[jax.ShapeDtypeStruct((ng, D), F32), jax.ShapeDtypeStruct((PACK_ROWS, ng, LANES), U32),
                       jax.ShapeDtypeStruct((2 * TOP_K, ng), I32), jax.ShapeDtypeStruct((ng, LANES), F32),
                       jax.ShapeDtypeStruct((8, LANES), F32)],
            compiler_params=cparams(dimension_semantics=("arbitrary",)),
            name="mixer_out",
        )(xf, pg, ga, sba, w_attn_up_bf, w_out_bf, g2, wr_hi, wr_lo, br, ltri)

        counts = cnt[0, :n_exp].astype(I32)
        padded = (counts + EXPERT_BLOCK - 1) // EXPERT_BLOCK * EXPERT_BLOCK
        padded_end = jnp.cumsum(padded)
        start_pad = padded_end - padded
        idx = ri[:TOP_K]
        dest = ri[TOP_K:]
        for e in range(n_exp):
            dest = dest + jnp.where(idx == e, start_pad[e], 0)
        block_start = jnp.arange(n_blocks, dtype=I32) * EXPERT_BLOCK
        block_expert = jnp.minimum(jnp.sum((padded_end[None, :] <= block_start[:, None]).astype(I32), axis=1),
                                   n_exp - 1)
        n_used = (padded_end[-1] // EXPERT_BLOCK).astype(I32).reshape(1)

        dst_idx = dest.reshape(TOP_K, ng // SC_CHUNK, SC_CHUNK).transpose(1, 0, 2).reshape(-1, SC_CHUNK)
        xs = _sc_scatter_rows(hpk, dst_idx, n_rows)
        xs = pl.pallas_call(
            _padfill_kernel,
            grid_spec=pltpu.PrefetchScalarGridSpec(
                num_scalar_prefetch=2,
                grid=(1,),
                in_specs=[pl.BlockSpec(memory_space=pl.ANY)],
                out_specs=pl.BlockSpec(memory_space=pl.ANY),
                scratch_shapes=[pltpu.VMEM((PACK_ROWS, EXPERT_BLOCK // 2, LANES), U32), pltpu.SemaphoreType.DMA(())],
            ),
            out_shape=jax.ShapeDtypeStruct((PACK_ROWS, n_rows, LANES), U32),
            input_output_aliases={2: 0},
            compiler_params=cparams(dimension_semantics=("arbitrary",)),
            name="padfill",
        )(start_pad + counts, padded - counts, xs)

        ys = pl.pallas_call(
            _experts_kernel,
            grid_spec=pltpu.PrefetchScalarGridSpec(
                num_scalar_prefetch=2,
                grid=(n_blocks,),
                in_specs=[plane_spec(EXPERT_BLOCK, lambda i, be, nb: jnp.minimum(i, nb[0] - 1)),
                          pl.BlockSpec((1, D, 2 * de), lambda i, be, nb: (be[i], 0, 0)),
                          pl.BlockSpec((1, 1, 2 * de), lambda i, be, nb: (be[i], 0, 0)),
                          pl.BlockSpec((1, de, D), lambda i, be, nb: (be[i], 0, 0)),
                          pl.BlockSpec((1, 1, D), lambda i, be, nb: (be[i], 0, 0))],
                out_specs=plane_spec(EXPERT_BLOCK, lambda i, be, nb: i),
                scratch_shapes=[pltpu.VMEM((D, 2 * de), BF16), pltpu.VMEM((de, D), BF16)],
            ),
            out_shape=jax.ShapeDtypeStruct((PACK_ROWS, n_rows, LANES), U32),
            compiler_params=cparams(dimension_semantics=("arbitrary",)),
            name="experts",
        )(block_expert, n_used, xs, w_gate_up, bgu, w_down, bdn)

        y4 = _sc_gather_rows(ys, dest.reshape(-1, SC_CHUNK))

        operands = [x1, gate, y4, y4, y4, y4] + ([] if out is None else [out])
        out = pl.pallas_call(
            _combine_kernel,
            grid=(steps,),
            in_specs=([row_spec(D), row_spec(LANES)]
                      + [plane_spec(tm, lambda i, kk=kk: kk * steps + i) for kk in range(TOP_K)]
                      + ([] if out is None else [pl.BlockSpec(memory_space=pl.ANY)])),
            out_specs=grp_spec(D),
            out_shape=jax.ShapeDtypeStruct((N, D), F32),
            input_output_aliases={} if out is None else {len(operands) - 1: 0},
            compiler_params=cparams(dimension_semantics=("arbitrary",)),
            name="combine",
        )(*operands)
    return out.reshape(B, S, D)


def kernel(x, norm1_g, w_in, q_norm_g, k_norm_g, w_pool_grp, pool_scale, w_pool_up, w_attn_up, w_out, norm2_g,
           w_router, b_router, w_gate_up, b_gate_up, w_down, b_down):
    for layer in range(norm1_g.shape[0]):
        x = _layer(x, norm1_g[layer], w_in[layer], q_norm_g[layer], k_norm_g[layer], w_pool_grp[layer],
                   pool_scale[layer], w_pool_up[layer], w_attn_up[layer], w_out[layer], norm2_g[layer],
                   w_router[layer], b_router[layer], w_gate_up[layer], b_gate_up[layer], w_down[layer],
                   b_down[layer])
    return x
```

```python
import functools

import jax
import jax.numpy as jnp
from jax import lax
from jax.experimental import pallas as pl
from jax.experimental.pallas import tpu as pltpu
from jax.experimental.pallas import tpu_sc as plsc

F32 = jnp.float32
BF16 = jnp.bfloat16
U32 = jnp.uint32
I32 = jnp.int32

EPS = 1e-6
POOL_WINDOWS = (2, 4, 8, 16)
POOL_GROUP_DIM = 128
POOL_HALO = 16
SB_HEAD_DIM = 64
TOP_K = 4
SWIGLU_LIMIT = 7.0
SWIGLU_ALPHA = 1.702
EXPERT_BLOCK = 512
LANES = 128
PACK_ROWS = 4
ATTN_BLOCK = 128
ATTN_EXIT = 48.0
VMEM_LIMIT = 56 * 1024 * 1024
SC_CHUNK = 128
MOE_GROUPS = 2


def _dot(a, b):
    return jnp.dot(a, b, preferred_element_type=F32)


def _split_bf16(x):
    hi = x.astype(BF16)
    lo = (x - hi.astype(F32)).astype(BF16)
    return hi, lo


def _pack_rows(v, out_ref):
    half = v.shape[1] // 2
    lo = lax.bitcast_convert_type(v[:, :half].astype(BF16).astype(F32), U32) >> 16
    hi = lax.bitcast_convert_type(v[:, half:].astype(BF16).astype(F32), U32) & jnp.uint32(0xFFFF0000)
    w = lo | hi
    for c in range(PACK_ROWS):
        out_ref[c] = w[:, c * LANES:(c + 1) * LANES]


def _unpack_rows(ref):
    los, his = [], []
    for c in range(PACK_ROWS):
        w = ref[c]
        los.append(lax.bitcast_convert_type(w << 16, F32))
        his.append(lax.bitcast_convert_type(w & jnp.uint32(0xFFFF0000), F32))
    return jnp.concatenate(los + his, axis=1)


def _mixer_in_kernel(x_ref, g1_ref, win_ref, gq_ref, gk_ref, hsum_ref, wgrp_ref, pscale_ref, wpu_ref,
                     q_ref, k_ref, v_ref, ga_ref, p_ref, tail_ref, *, tm, pw, sw):
    i = pl.program_id(1)

    @pl.when(i == 0)
    def _():
        tail_ref[...] = jnp.zeros_like(tail_ref)

    x = x_ref[...]
    ms = jnp.mean(x * x, axis=-1, keepdims=True)
    h = (x * lax.rsqrt(ms + EPS) * g1_ref[...]).astype(BF16)

    u = _dot(h, win_ref[:, 0:pw])
    xx = jnp.concatenate([tail_ref[...], u], axis=0)
    tail_ref[...] = u[tm - POOL_HALO:, :]
    pos = i * tm + lax.broadcasted_iota(I32, (tm, POOL_GROUP_DIM), 0)
    mixed = []
    for g, w in enumerate(POOL_WINDOWS):
        s = xx[:, g * POOL_GROUP_DIM:(g + 1) * POOL_GROUP_DIM]
        step = 1
        while step < w:
            s = s + pltpu.roll(s, step, axis=0)
            step *= 2
        count = jnp.minimum(pos + 1, w).astype(F32)
        ug = u[:, g * POOL_GROUP_DIM:(g + 1) * POOL_GROUP_DIM]
        d = s[POOL_HALO:, :] / count - ug
        mixed.append(_dot(d.astype(BF16), wgrp_ref[g]))
    pm = jnp.concatenate(mixed, axis=1) * pscale_ref[...]
    pool_out = _dot(pm.astype(BF16), wpu_ref[...])
    d_model = pool_out.shape[1]
    g_pool = _dot(h, win_ref[:, pw + 3 * sw:pw + 3 * sw + d_model])
    p_ref[...] = (jax.nn.sigmoid(g_pool) * pool_out).astype(BF16)
    g_attn = _dot(h, win_ref[:, pw + 3 * sw + d_model:pw + 3 * sw + 2 * d_model])
    ga_ref[...] = jax.nn.sigmoid(g_attn).astype(BF16)

    def head_norm(t, gain):
        ss = _dot((t * t).astype(BF16), hsum_ref[...])
        return t * lax.rsqrt(ss * (1.0 / SB_HEAD_DIM) + EPS) * gain

    q = _dot(h, win_ref[:, pw:pw + sw])
    q_ref[...] = (head_norm(q, gq_ref[...]) * (SB_HEAD_DIM ** -0.5)).astype(BF16)
    k = _dot(h, win_ref[:, pw + sw:pw + 2 * sw])
    k_ref[...] = head_norm(k, gk_ref[...]).astype(BF16)
    v_ref[...] = _dot(h, win_ref[:, pw + 2 * sw:pw + 3 * sw]).astype(BF16)


def _attn_kernel(q_ref, k_ref, v_ref, tri_ref, o_ref, *scratch, n_pairs):
    qs, acc, rr = scratch[:n_pairs], scratch[n_pairs:2 * n_pairs], scratch[2 * n_pairs:]
    qi = pl.program_id(1)
    bq = ATTN_BLOCK
    first_head = lax.broadcasted_iota(I32, (bq, LANES), 1) < SB_HEAD_DIM
    for p in range(n_pairs):
        q2 = q_ref[:, p * LANES:(p + 1) * LANES]
        qs[p][:bq] = jnp.where(first_head, q2, jnp.zeros_like(q2))
        qs[p][bq:] = jnp.where(first_head, jnp.zeros_like(q2), q2)
    row = lax.broadcasted_iota(I32, (2 * bq, bq), 0)
    col = lax.broadcasted_iota(I32, (2 * bq, bq), 1)
    causal = col < (row & (bq - 1))
    contract_last = (((1,), (1,)), ((), ()))

    def softplus(z):
        return jnp.maximum(z, 0.0) + jnp.log(1.0 + jnp.exp(-jnp.abs(z)))

    def suffix_sums(sp):
        hi, lo = _split_bf16(sp)
        return _dot(jnp.concatenate([hi, lo], axis=1), tri_ref[...])

    def cols(ref, start, p):
        return ref[pl.ds(start, bq), p * LANES:(p + 1) * LANES]

    start_d = pl.multiple_of(qi * bq, bq)
    start_n = pl.multiple_of(jnp.maximum(qi - 1, 0) * bq, bq)
    has_n = qi >= 1
    z_d = [lax.dot_general(qs[p][...], cols(k_ref, start_d, p), contract_last, preferred_element_type=F32)
           for p in range(n_pairs)]
    z_n = [lax.dot_general(qs[p][...], cols(k_ref, start_n, p), contract_last, preferred_element_type=F32)
           for p in range(n_pairs)]
    s_d = [suffix_sums(jnp.where(causal, softplus(z), 0.0)) for z in z_d]
    s_n = [suffix_sums(jnp.where(has_n, softplus(z), 0.0)) for z in z_n]
    for p in range(n_pairs):
        a_d = jnp.where(causal, jnp.exp(z_d[p] - s_d[p][:, :bq]), 0.0)
        r_d = s_d[p][:, bq:]
        a_n = jnp.where(has_n, jnp.exp(z_n[p] - (r_d + s_n[p][:, :bq])), 0.0)
        acc[p][...] = (_dot(a_d.astype(BF16), cols(v_ref, start_d, p)) + _dot(a_n.astype(BF16), cols(v_ref, start_n, p)))
        rr[p][...] = r_d + s_n[p][:, bq:]

    def r_min():
        m = rr[0][...]
        for p in range(1, n_pairs):
            m = jnp.minimum(m, rr[p][...])
        return jnp.min(m)

    def cond(c):
        j, rm = c
        return jnp.logical_and(j >= 0, rm < ATTN_EXIT)

    def body(c):
        j, _ = c
        start = pl.multiple_of(j * bq, bq)
        zs = [lax.dot_general(qs[p][...], cols(k_ref, start, p), contract_last, preferred_element_type=F32)
              for p in range(n_pairs)]
        ss = [suffix_sums(softplus(z)) for z in zs]
        for p in range(n_pairs):
            r = rr[p][...]
            a = jnp.exp(zs[p] - (r + ss[p][:, :bq]))
            acc[p][...] += _dot(a.astype(BF16), cols(v_ref, start, p))
            rr[p][...] = r + ss[p][:, bq:]
        return j - 1, r_min()

    lax.while_loop(cond, body, (qi - 2, r_min()))
    for p in range(n_pairs):
        o_ref[:, p * LANES:(p + 1) * LANES] = jnp.where(first_head, acc[p][:bq], acc[p][bq:]).astype(BF16)


def _mixer_out_kernel(x_ref, p_ref, ga_ref, sba_ref, wau_ref, wout_ref, g2_ref, wr_hi_ref, wr_lo_ref, br_ref,
                      ltri_ref, x1_ref, hp_ref, ri_ref, gate_ref, cnt_ref, *, tm):
    step = pl.program_id(0)

    @pl.when(step == 0)
    def _():
        cnt_ref[...] = jnp.zeros_like(cnt_ref)

    attn_out = _dot(sba_ref[...], wau_ref[...])
    merged = p_ref[...].astype(F32) + ga_ref[...].astype(F32) * attn_out
    x1 = x_ref[...] + _dot(merged.astype(BF16), wout_ref[...])
    x1_ref[...] = x1
    ms = jnp.mean(x1 * x1, axis=-1, keepdims=True)
    h2 = x1 * lax.rsqrt(ms + EPS) * g2_ref[...]
    _pack_rows(h2, hp_ref)

    h_hi, h_lo = _split_bf16(h2)
    logits = (_dot(h_hi, wr_hi_ref[...]) + _dot(h_hi, wr_lo_ref[...]) + _dot(h_lo, wr_hi_ref[...])
              + br_ref[...])
    lane = lax.broadcasted_iota(I32, logits.shape, 1)
    work = logits
    vals, idxs = [], []
    for _ in range(TOP_K):
        m = jnp.max(work, axis=-1, keepdims=True)
        ik = jnp.min(jnp.where(work == m, lane, LANES), axis=-1, keepdims=True)
        vals.append(m)
        idxs.append(ik)
        work = jnp.where(lane == ik, -jnp.inf, work)
    es = [jnp.exp(v - vals[0]) for v in vals]
    denom = es[0] + es[1] + es[2] + es[3]
    hot = jnp.zeros(logits.shape, F32)
    for ik in idxs:
        hot = hot + (lane == ik).astype(F32)
    before = _dot(ltri_ref[...], hot.astype(BF16)) + cnt_ref[0:1, :]
    ri = jnp.zeros(logits.shape, I32)
    gt = jnp.zeros(logits.shape, F32)
    for kk in range(TOP_K):
        rank = jnp.sum(jnp.where(lane == idxs[kk], before, 0.0), axis=-1, keepdims=True).astype(I32)
        ri = jnp.where(lane == kk, idxs[kk], ri)
        ri = jnp.where(lane == TOP_K + kk, rank, ri)
        gt = jnp.where(lane == kk, es[kk] / denom, gt)
    ri_ref[...] = ri.T[:2 * TOP_K]
    gate_ref[...] = gt
    cnt_ref[...] = cnt_ref[...] + jnp.sum(hot, axis=0, keepdims=True)


def _sc_mesh():
    info = plsc.get_sparse_core_info()
    mesh = plsc.VectorSubcoreMesh(core_axis_name="c", subcore_axis_name="s")
    return mesh, info.num_cores, info.num_subcores, info.num_lanes


def _sc_scatter_rows(rows, idx, n_out):
    mesh, nc, ns, lanes = _sc_mesh()
    nw = nc * ns
    planes, m, width = rows.shape
    ch = SC_CHUNK
    n_ch = m // ch // nw
    half = planes // 2
    assert planes % 2 == 0 and n_ch * ch * nw == m and idx.shape == (m // ch * TOP_K, ch) and n_ch >= 2

    @functools.partial(
        pl.kernel, mesh=mesh, out_type=jax.ShapeDtypeStruct((planes * n_out, width), rows.dtype),
        scratch_types=([pltpu.VMEM((n_ch * TOP_K, ch), I32)] + [pltpu.VMEM((ch, width), rows.dtype)] * planes
                       + [pltpu.VMEM((TOP_K, ch), I32)] * planes
                       + [pltpu.SemaphoreType.DMA((planes,)), pltpu.SemaphoreType.DMA((planes,))]))
    def scatter_kernel(rows_hbm, idx_hbm, out_hbm, idx_v, *rest):
        bufs, ibufs, rsem, ssem = rest[:planes], rest[planes:2 * planes], rest[2 * planes], rest[2 * planes + 1]
        wid = lax.axis_index("s") * nc + lax.axis_index("c")
        pltpu.sync_copy(idx_hbm.at[pl.ds(wid * n_ch * TOP_K, n_ch * TOP_K)], idx_v)
        base = wid * n_ch * ch

        def read(j, b):
            return pltpu.make_async_copy(rows_hbm.at[b, pl.ds(pl.multiple_of(base + j * ch, ch), ch)], bufs[b], rsem.at[b])

        def scatters(b):
            return [pltpu.make_async_copy(bufs[b], out_hbm.at[ibufs[b].at[kk]], ssem.at[b]) for kk in range(TOP_K)]

        def start_scatters(j, b):
            for kk in range(TOP_K):
                for t in range(0, ch, lanes):
                    ibufs[b][kk, pl.ds(t, lanes)] = idx_v[j * TOP_K + kk, pl.ds(t, lanes)] + b * n_out
            for c in scatters(b):
                c.start()

        def wait_scatters(b):
            for c in scatters(b):
                c.wait()

        def finish(j, b):
            pj, pb = (j, b - half) if b >= half else (j - 1, b + half)
            read(pj, pb).wait()
            start_scatters(pj, pb)

        for b in range(planes):
            read(0, b).start()
        for b in range(half, planes):
            finish(0, b)

        @pl.loop(1, n_ch)
        def _(j):
            for b in range(planes):
                wait_scatters(b)
                read(j, b).start()
                finish(j, b)

        for b in range(half):
            finish(n_ch, b)
        for b in range(planes):
            wait_scatters(b)

    return scatter_kernel(rows, idx).reshape(planes, n_out, width)


def _padfill_kernel(start_ref, len_ref, xs_in_ref, xs_ref, zeros_ref, sem):
    del xs_in_ref
    zeros_ref[...] = jnp.zeros_like(zeros_ref)
    bits = [1 << b for b in reversed(range(EXPERT_BLOCK.bit_length() - 1))]

    def pieces(e):
        n = len_ref[e]
        for bit in bits:
            row0 = start_ref[e] + (n & ~(2 * bit - 1))
            copy = pltpu.make_async_copy(zeros_ref.at[:, pl.ds(0, bit)], xs_ref.at[:, pl.ds(row0, bit)], sem)
            yield (n & bit) != 0, copy

    def start(e, c):
        for on, copy in pieces(e):
            pl.when(on)(copy.start)
        return c

    def wait(e, c):
        for on, copy in pieces(e):
            pl.when(on)(copy.wait)
        return c

    lax.fori_loop(0, start_ref.shape[0], start, 0)
    lax.fori_loop(0, start_ref.shape[0], wait, 0)


def _experts_kernel(be_ref, nb_ref, xs_ref, wgu_ref, bgu_ref, wd_ref, bd_ref, ys_ref, wgu_bf_ref, wd_bf_ref):
    blk = pl.program_id(0)
    de = wd_ref.shape[1]

    @pl.when(jnp.logical_or(blk == 0, be_ref[blk] != be_ref[jnp.maximum(blk - 1, 0)]))
    def _():
        wgu_bf_ref[...] = wgu_ref[0].astype(BF16)
        wd_bf_ref[...] = wd_ref[0].astype(BF16)

    @pl.when(blk < nb_ref[0])
    def _():
        x = _unpack_rows(xs_ref).astype(BF16)
        gu = _dot(x, wgu_bf_ref[...]) + bgu_ref[0]
        glu = jnp.minimum(gu[:, :de], SWIGLU_LIMIT)
        lin = jnp.clip(gu[:, de:], -SWIGLU_LIMIT, SWIGLU_LIMIT)
        act = glu * jax.nn.sigmoid(SWIGLU_ALPHA * glu) * (lin + 1.0)
        y = _dot(act.astype(BF16), wd_bf_ref[...]) + bd_ref[0]
        _pack_rows(y, ys_ref)

    @pl.when(blk >= nb_ref[0])
    def _():
        ys_ref[...] = jnp.zeros_like(ys_ref)


def _sc_gather_rows(table, idx):
    mesh, nc, ns, lanes = _sc_mesh()
    nw = nc * ns
    planes, n_tab, width = table.shape
    n_idx_rows, ch = idx.shape
    n_ch = n_idx_rows // nw
    m = n_idx_rows * ch
    half = planes // 2
    assert planes % 2 == 0 and ch == SC_CHUNK and n_ch * nw == n_idx_rows and n_ch >= 2

    @functools.partial(
        pl.kernel, mesh=mesh, out_type=jax.ShapeDtypeStruct((planes, m, width), table.dtype),
        scratch_types=([pltpu.VMEM((n_ch, ch), I32)] + [pltpu.VMEM((ch, width), table.dtype)] * planes
                       + [pltpu.VMEM((8, ch), I32)] * planes
                       + [pltpu.SemaphoreType.DMA((planes,)), pltpu.SemaphoreType.DMA((planes,))]))
    def gather_kernel(table_hbm, idx_hbm, out_hbm, idx_v, *rest):
        bufs, ibufs, gsem, wsem = rest[:planes], rest[planes:2 * planes], rest[2 * planes], rest[2 * planes + 1]
        wid = lax.axis_index("s") * nc + lax.axis_index("c")
        pltpu.sync_copy(idx_hbm.at[pl.ds(wid * n_ch, n_ch)], idx_v)
        base = wid * n_ch * ch

        def gather(b):
            return pltpu.make_async_copy(table_hbm.at[ibufs[b].at[0]], bufs[b], gsem.at[b])

        def start_gather(j, b):
            for t in range(0, ch, lanes):
                ibufs[b][0, pl.ds(t, lanes)] = idx_v[j, pl.ds(t, lanes)] + b * n_tab
            gather(b).start()

        def write(j, b):
            return pltpu.make_async_copy(bufs[b], out_hbm.at[b, pl.ds(pl.multiple_of(base + j * ch, ch), ch)], wsem.at[b])

        def finish(j, b):
            pj, pb = (j, b - half) if b >= half else (j - 1, b + half)
            gather(pb).wait()
            write(pj, pb).start()

        for b in range(planes):
            start_gather(0, b)
        for b in range(half, planes):
            finish(0, b)

        @pl.loop(1, n_ch)
        def _(j):
            for b in range(planes):
                write(j - 1, b).wait()
                start_gather(j, b)
                finish(j, b)

        for b in range(half):
            finish(n_ch, b)
        for b in range(planes):
            write(n_ch - 1, b).wait()

    return gather_kernel(table.reshape(planes * n_tab, width), idx)


def _combine_kernel(x1_ref, gate_ref, y0_ref, y1_ref, y2_ref, y3_ref, *rest):
    o_ref = rest[-1]
    gate = gate_ref[...]
    out = x1_ref[...]
    for kk, y_ref in enumerate((y0_ref, y1_ref, y2_ref, y3_ref)):
        out = out + gate[:, kk:kk + 1] * _unpack_rows(y_ref)
    o_ref[...] = out


def _const_spec(shape):
    nd = len(shape)
    return pl.BlockSpec(shape, lambda *_: (0,) * nd)


def _layer(x, norm1_g, w_in, q_norm_g, k_norm_g, w_pool_grp, pool_scale, w_pool_up, w_attn_up, w_out, norm2_g,
           w_router, b_router, w_gate_up, b_gate_up, w_down, b_down):
    B, S, D = x.shape
    N = B * S
    pw = w_pool_up.shape[0]
    sw = w_attn_up.shape[0]
    n_exp = w_router.shape[1]
    de = w_down.shape[1]
    heads = sw // SB_HEAD_DIM
    assert pw == len(POOL_WINDOWS) * POOL_GROUP_DIM and heads % 2 == 0 and n_exp <= LANES
    assert D == 2 * PACK_ROWS * LANES and w_in.shape[1] == pw + 3 * sw + 2 * D
    tm = 512 if S % 512 == 0 else 256
    assert S % tm == 0 and S % ATTN_BLOCK == 0
    xf = x.reshape(N, D)
    cparams = functools.partial(pltpu.CompilerParams, vmem_limit_bytes=VMEM_LIMIT)

    hsum = (jnp.arange(sw)[:, None] // SB_HEAD_DIM == jnp.arange(sw)[None, :] // SB_HEAD_DIM).astype(BF16)
    nt = S // tm
    tok_spec = lambda w: pl.BlockSpec((tm, w), lambda b, i: (b * nt + i, 0))
    q2, k2, v2, ga, pg = pl.pallas_call(
        functools.partial(_mixer_in_kernel, tm=tm, pw=pw, sw=sw),
        grid=(B, nt),
        in_specs=[tok_spec(D), _const_spec((1, D)), _const_spec((D, w_in.shape[1])), _const_spec((1, sw)),
                  _const_spec((1, sw)), _const_spec((sw, sw)),
                  _const_spec((len(POOL_WINDOWS), POOL_GROUP_DIM, POOL_GROUP_DIM)), _const_spec((1, pw)),
                  _const_spec((pw, D))],
        out_specs=[tok_spec(sw), tok_spec(sw), tok_spec(sw), tok_spec(D), tok_spec(D)],
        out_shape=[jax.ShapeDtypeStruct((N, sw), BF16)] * 3 + [jax.ShapeDtypeStruct((N, D), BF16)] * 2,
        scratch_shapes=[pltpu.VMEM((POOL_HALO, pw), F32)],
        compiler_params=cparams(dimension_semantics=("arbitrary", "arbitrary")),
        name="mixer_in",
    )(xf, norm1_g.reshape(1, D), w_in.astype(BF16), jnp.tile(q_norm_g, heads).reshape(1, sw),
      jnp.tile(k_norm_g, heads).reshape(1, sw), hsum, w_pool_grp.astype(BF16), pool_scale.reshape(1, pw),
      w_pool_up.astype(BF16))

    bq = ATTN_BLOCK
    nq = S // bq
    jj = jnp.arange(bq)
    tri_half = jnp.concatenate([(jj[:, None] >= jj[None, :]).astype(BF16), jnp.ones((bq, bq), BF16)], axis=1)
    tri = jnp.concatenate([tri_half, tri_half], axis=0)
    n_pairs = heads // 2
    kv_spec = pl.BlockSpec((S, sw), lambda b, qi: (b, 0), pipeline_mode=pl.Buffered(1))
    sba = pl.pallas_call(
        functools.partial(_attn_kernel, n_pairs=n_pairs),
        grid=(B, nq),
        in_specs=[pl.BlockSpec((bq, sw), lambda b, qi: (b * nq + qi, 0)), kv_spec, kv_spec,
                  _const_spec((2 * bq, 2 * bq))],
        out_specs=pl.BlockSpec((bq, sw), lambda b, qi: (b * nq + qi, 0)),
        out_shape=jax.ShapeDtypeStruct((N, sw), BF16),
        scratch_shapes=([pltpu.VMEM((2 * bq, LANES), BF16)] * n_pairs + [pltpu.VMEM((2 * bq, LANES), F32)] * (2 * n_pairs)),
        compiler_params=cparams(dimension_semantics=("arbitrary", "arbitrary")),
        name="sb_attn",
    )(q2, k2, v2, tri)

    wr = jnp.zeros((D, LANES), F32).at[:, :n_exp].set(w_router)
    wr_hi = wr.astype(BF16)
    wr_lo = (wr - wr_hi.astype(F32)).astype(BF16)
    br = jnp.full((1, LANES), -jnp.inf, F32).at[0, :n_exp].set(b_router)
    ltri = (jnp.arange(tm)[:, None] > jnp.arange(tm)[None, :]).astype(BF16)
    assert N % (MOE_GROUPS * tm) == 0
    ng = N // MOE_GROUPS
    steps = ng // tm
    n_assign = ng * TOP_K
    n_blocks = -(-(n_assign + n_exp * (EXPERT_BLOCK - 1)) // EXPERT_BLOCK)
    n_rows = n_blocks * EXPERT_BLOCK
    plane_spec = lambda rows, index: pl.BlockSpec((PACK_ROWS, rows, LANES), lambda i, *_: (0, index(i, *_), 0))
    row_spec = lambda w: pl.BlockSpec((tm, w), lambda i: (i, 0))
    w_attn_up_bf, w_out_bf, g2 = w_attn_up.astype(BF16), w_out.astype(BF16), norm2_g.reshape(1, D)
    bgu, bdn = b_gate_up.reshape(n_exp, 1, 2 * de), b_down.reshape(n_exp, 1, D)
    out = None
    for grp in range(MOE_GROUPS):
        grp_spec = lambda w, first=grp * steps: pl.BlockSpec((tm, w), lambda i: (first + i, 0))

        x1, hpk, ri, gate, cnt = pl.pallas_call(
            functools.partial(_mixer_out_kernel, tm=tm),
            grid=(steps,),
            in_specs=[grp_spec(D), grp_spec(D), grp_spec(D), grp_spec(sw), _const_spec((sw, D)), _const_spec((D, D)),
                      _const_spec((1, D)), _const_spec((D, LANES)), _const_spec((D, LANES)), _const_spec((1, LANES)),
                      _const_spec((tm, tm))],
            out_specs=[row_spec(D), plane_spec(tm, lambda i: i), pl.BlockSpec((2 * TOP_K, tm), lambda i: (0, i)),
                       row_spec(LANES), _const_spec((8, LANES))],
            out_shape=[jax.ShapeDtypeStruct((ng, D), F32), jax.ShapeDtypeStruct((PACK_ROWS, ng, LANES), U32),
                       jax.ShapeDtypeStruct((2 * TOP_K, ng), I32), jax.ShapeDtypeStruct((ng, LANES), F32),
                       jax.ShapeDtypeStruct((8, LANES), F32)],
            compiler_params=cparams(dimension_semantics=("arbitrary",)),
            name="mixer_out",
        )(xf, pg, ga, sba, w_attn_up_bf, w_out_bf, g2, wr_hi, wr_lo, br, ltri)

        counts = cnt[0, :n_exp].astype(I32)
        padded = (counts + EXPERT_BLOCK - 1) // EXPERT_BLOCK * EXPERT_BLOCK
        padded_end = jnp.cumsum(padded)
        start_pad = padded_end - padded
        idx = ri[:TOP_K]
        dest = ri[TOP_K:]
        for e in range(n_exp):
            dest = dest + jnp.where(idx == e, start_pad[e], 0)
        block_start = jnp.arange(n_blocks, dtype=I32) * EXPERT_BLOCK
        block_expert = jnp.minimum(jnp.sum((padded_end[None, :] <= block_start[:, None]).astype(I32), axis=1),
                                   n_exp - 1)
        n_used = (padded_end[-1] // EXPERT_BLOCK).astype(I32).reshape(1)

        dst_idx = dest.reshape(TOP_K, ng // SC_CHUNK, SC_CHUNK).transpose(1, 0, 2).reshape(-1, SC_CHUNK)
        xs = _sc_scatter_rows(hpk, dst_idx, n_rows)
        xs = pl.pallas_call(
            _padfill_kernel,
            grid_spec=pltpu.PrefetchScalarGridSpec(
                num_scalar_prefetch=2,
                grid=(1,),
                in_specs=[pl.BlockSpec(memory_space=pl.ANY)],
                out_specs=pl.BlockSpec(memory_space=pl.ANY),
                scratch_shapes=[pltpu.VMEM((PACK_ROWS, EXPERT_BLOCK // 2, LANES), U32), pltpu.SemaphoreType.DMA(())],
            ),
            out_shape=jax.ShapeDtypeStruct((PACK_ROWS, n_rows, LANES), U32),
            input_output_aliases={2: 0},
            compiler_params=cparams(dimension_semantics=("arbitrary",)),
            name="padfill",
        )(start_pad + counts, padded - counts, xs)

        ys = pl.pallas_call(
            _experts_kernel,
            grid_spec=pltpu.PrefetchScalarGridSpec(
                num_scalar_prefetch=2,
                grid=(n_blocks,),
                in_specs=[plane_spec(EXPERT_BLOCK, lambda i, be, nb: jnp.minimum(i, nb[0] - 1)),
                          pl.BlockSpec((1, D, 2 * de), lambda i, be, nb: (be[i], 0, 0)),
                          pl.BlockSpec((1, 1, 2 * de), lambda i, be, nb: (be[i], 0, 0)),
                          pl.BlockSpec((1, de, D), lambda i, be, nb: (be[i], 0, 0)),
                          pl.BlockSpec((1, 1, D), lambda i, be, nb: (be[i], 0, 0))],
                out_specs=plane_spec(EXPERT_BLOCK, lambda i, be, nb: i),
                scratch_shapes=[pltpu.VMEM((D, 2 * de), BF16), pltpu.VMEM((de, D), BF16)],
            ),
            out_shape=jax.ShapeDtypeStruct((PACK_ROWS, n_rows, LANES), U32),
            compiler_params=cparams(dimension_semantics=("arbitrary",)),
            name="experts",
        )(block_expert, n_used, xs, w_gate_up, bgu, w_down, bdn)

        y4 = _sc_gather_rows(ys, dest.reshape(-1, SC_CHUNK))

        operands = [x1, gate, y4, y4, y4, y4] + ([] if out is None else [out])
        out = pl.pallas_call(
            _combine_kernel,
            grid=(steps,),
            in_specs=([row_spec(D), row_spec(LANES)]
                      + [plane_spec(tm, lambda i, kk=kk: kk * steps + i) for kk in range(TOP_K)]
                      + ([] if out is None else [pl.BlockSpec(memory_space=pl.ANY)])),
            out_specs=grp_spec(D),
            out_shape=jax.ShapeDtypeStruct((N, D), F32),
            input_output_aliases={} if out is None else {len(operands) - 1: 0},
            compiler_params=cparams(dimension_semantics=("arbitrary",)),
            name="combine",
        )(*operands)
    return out.reshape(B, S, D)


def kernel(x, norm1_g, w_in, q_norm_g, k_norm_g, w_pool_grp, pool_scale, w_pool_up, w_attn_up, w_out, norm2_g,
           w_router, b_router, w_gate_up, b_gate_up, w_down, b_down):
    for layer in range(norm1_g.shape[0]):
        x = _layer(x, norm1_g[layer], w_in[layer], q_norm_g[layer], k_norm_g[layer], w_pool_grp[layer],
                   pool_scale[layer], w_pool_up[layer], w_attn_up[layer], w_out[layer], norm2_g[layer],
                   w_router[layer], b_router[layer], w_gate_up[layer], b_gate_up[layer], w_down[layer],
                   b_down[layer])
    return x
```

```python
import functools

import jax
import jax.numpy as jnp
from jax import lax
from jax.experimental import pallas as pl
from jax.experimental.pallas import tpu as pltpu
from jax.experimental.pallas import tpu_sc as plsc

F32 = jnp.float32
BF16 = jnp.bfloat16
U32 = jnp.uint32
I32 = jnp.int32

EPS = 1e-6
POOL_WINDOWS = (2, 4, 8, 16)
POOL_GROUP_DIM = 128
POOL_HALO = 16
SB_HEAD_DIM = 64
TOP_K = 4
SWIGLU_LIMIT = 7.0
SWIGLU_ALPHA = 1.702
EXPERT_BLOCK = 512
LANES = 128
PACK_ROWS = 4
ATTN_BLOCK = 128
ATTN_EXIT_BITS = 70.0
LOG2_E = 1.4426950408889634
VMEM_LIMIT = 56 * 1024 * 1024
SC_CHUNK = 128
MOE_GROUPS = 2


def _dot(a, b):
    return jnp.dot(a, b, preferred_element_type=F32)


def _split_bf16(x):
    hi = x.astype(BF16)
    lo = (x - hi.astype(F32)).astype(BF16)
    return hi, lo


def _pack_rows(v, out_ref):
    half = v.shape[1] // 2
    lo = lax.bitcast_convert_type(v[:, :half].astype(BF16).astype(F32), U32) >> 16
    hi = lax.bitcast_convert_type(v[:, half:].astype(BF16).astype(F32), U32) & jnp.uint32(0xFFFF0000)
    w = lo | hi
    for c in range(PACK_ROWS):
        out_ref[c] = w[:, c * LANES:(c + 1) * LANES]


def _unpack_rows(ref):
    los, his = [], []
    for c in range(PACK_ROWS):
        w = ref[c]
        los.append(lax.bitcast_convert_type(w << 16, F32))
        his.append(lax.bitcast_convert_type(w & jnp.uint32(0xFFFF0000), F32))
    return jnp.concatenate(los + his, axis=1)


def _mixer_in_kernel(x_ref, g1_ref, win_ref, gq_ref, gk_ref, hsum_ref, wgrp_ref, pscale_ref, wpu_ref,
                     q_ref, k_ref, v_ref, ga_ref, p_ref, tail_ref, *, tm, pw, sw):
    i = pl.program_id(1)

    @pl.when(i == 0)
    def _():
        tail_ref[...] = jnp.zeros_like(tail_ref)

    x = x_ref[...]
    ms = jnp.mean(x * x, axis=-1, keepdims=True)
    h = (x * lax.rsqrt(ms + EPS) * g1_ref[...]).astype(BF16)

    u = _dot(h, win_ref[:, 0:pw])
    xx = jnp.concatenate([tail_ref[...], u], axis=0)
    tail_ref[...] = u[tm - POOL_HALO:, :]
    pos = i * tm + lax.broadcasted_iota(I32, (tm, POOL_GROUP_DIM), 0)
    mixed = []
    for g, w in enumerate(POOL_WINDOWS):
        s = xx[:, g * POOL_GROUP_DIM:(g + 1) * POOL_GROUP_DIM]
        step = 1
        while step < w:
            s = s + pltpu.roll(s, step, axis=0)
            step *= 2
        count = jnp.minimum(pos + 1, w).astype(F32)
        ug = u[:, g * POOL_GROUP_DIM:(g + 1) * POOL_GROUP_DIM]
        d = s[POOL_HALO:, :] / count - ug
        mixed.append(_dot(d.astype(BF16), wgrp_ref[g]))
    pm = jnp.concatenate(mixed, axis=1) * pscale_ref[...]
    pool_out = _dot(pm.astype(BF16), wpu_ref[...])
    d_model = pool_out.shape[1]
    g_pool = _dot(h, win_ref[:, pw + 3 * sw:pw + 3 * sw + d_model])
    p_ref[...] = (jax.nn.sigmoid(g_pool) * pool_out).astype(BF16)
    g_attn = _dot(h, win_ref[:, pw + 3 * sw + d_model:pw + 3 * sw + 2 * d_model])
    ga_ref[...] = jax.nn.sigmoid(g_attn).astype(BF16)

    def head_norm(t, gain):
        ss = _dot((t * t).astype(BF16), hsum_ref[...])
        return t * lax.rsqrt(ss * (1.0 / SB_HEAD_DIM) + EPS) * gain

    q = _dot(h, win_ref[:, pw:pw + sw])
    q_ref[...] = (head_norm(q, gq_ref[...]) * (SB_HEAD_DIM ** -0.5 * LOG2_E)).astype(BF16)
    k = _dot(h, win_ref[:, pw + sw:pw + 2 * sw])
    k_ref[...] = head_norm(k, gk_ref[...]).astype(BF16)
    v_ref[...] = _dot(h, win_ref[:, pw + 2 * sw:pw + 3 * sw]).astype(BF16)


def _attn_kernel(q_ref, k_ref, v_ref, tri_ref, o_ref, *scratch, n_pairs):
    qs, acc, rr = scratch[:n_pairs], scratch[n_pairs:2 * n_pairs], scratch[2 * n_pairs:]
    qi = pl.program_id(1)
    bq = ATTN_BLOCK
    first_head = lax.broadcasted_iota(I32, (bq, LANES), 1) < SB_HEAD_DIM
    for p in range(n_pairs):
        q2 = q_ref[:, p * LANES:(p + 1) * LANES]
        qs[p][:bq] = jnp.where(first_head, q2, jnp.zeros_like(q2))
        qs[p][bq:] = jnp.where(first_head, jnp.zeros_like(q2), q2)
    row = lax.broadcasted_iota(I32, (2 * bq, bq), 0)
    col = lax.broadcasted_iota(I32, (2 * bq, bq), 1)
    causal = col < (row & (bq - 1))
    contract_last = (((1,), (1,)), ((), ()))

    def softplus(z):
        return jnp.maximum(z, 0.0) + jnp.log2(1.0 + jnp.exp2(-jnp.abs(z)))

    def suffix_sums(sp):
        hi, lo = _split_bf16(sp)
        return _dot(jnp.concatenate([hi, lo], axis=1), tri_ref[...])

    def cols(ref, start, p):
        return ref[pl.ds(start, bq), p * LANES:(p + 1) * LANES]

    start_d = pl.multiple_of(qi * bq, bq)
    start_n = pl.multiple_of(jnp.maximum(qi - 1, 0) * bq, bq)
    has_n = qi >= 1
    z_d = [lax.dot_general(qs[p][...], cols(k_ref, start_d, p), contract_last, preferred_element_type=F32)
           for p in range(n_pairs)]
    z_n = [lax.dot_general(qs[p][...], cols(k_ref, start_n, p), contract_last, preferred_element_type=F32)
           for p in range(n_pairs)]
    s_d = [suffix_sums(jnp.where(causal, softplus(z), 0.0)) for z in z_d]
    s_n = [suffix_sums(jnp.where(has_n, softplus(z), 0.0)) for z in z_n]
    for p in range(n_pairs):
        a_d = jnp.where(causal, jnp.exp2(z_d[p] - s_d[p][:, :bq]), 0.0)
        r_d = s_d[p][:, bq:]
        a_n = jnp.where(has_n, jnp.exp2(z_n[p] - (r_d + s_n[p][:, :bq])), 0.0)
        acc[p][...] = (_dot(a_d.astype(BF16), cols(v_ref, start_d, p)) + _dot(a_n.astype(BF16), cols(v_ref, start_n, p)))
        rr[p][...] = r_d + s_n[p][:, bq:]

    def r_min():
        m = rr[0][...]
        for p in range(1, n_pairs):
            m = jnp.minimum(m, rr[p][...])
        return jnp.min(m)

    def cond(c):
        j, rm = c
        return jnp.logical_and(j >= 0, rm < ATTN_EXIT_BITS)

    def body(c):
        j, _ = c
        start = pl.multiple_of(j * bq, bq)
        zs = [lax.dot_general(qs[p][...], cols(k_ref, start, p), contract_last, preferred_element_type=F32)
              for p in range(n_pairs)]
        ss = [suffix_sums(softplus(z)) for z in zs]
        for p in range(n_pairs):
            r = rr[p][...]
            a = jnp.exp2(zs[p] - (r + ss[p][:, :bq]))
            acc[p][...] += _dot(a.astype(BF16), cols(v_ref, start, p))
            rr[p][...] = r + ss[p][:, bq:]
        return j - 1, r_min()

    lax.while_loop(cond, body, (qi - 2, r_min()))
    for p in range(n_pairs):
        o_ref[:, p * LANES:(p + 1) * LANES] = jnp.where(first_head, acc[p][:bq], acc[p][bq:]).astype(BF16)


def _mixer_out_kernel(x_ref, p_ref, ga_ref, sba_ref, wau_ref, wout_ref, g2_ref, wr_hi_ref, wr_lo_ref, br_ref,
                      ltri_ref, x1_ref, hp_ref, ri_ref, gate_ref, cnt_ref, *, tm):
    step = pl.program_id(0)

    @pl.when(step == 0)
    def _():
        cnt_ref[...] = jnp.zeros_like(cnt_ref)

    attn_out = _dot(sba_ref[...], wau_ref[...])
    merged = p_ref[...].astype(F32) + ga_ref[...].astype(F32) * attn_out
    x1 = x_ref[...] + _dot(merged.astype(BF16), wout_ref[...])
    x1_ref[...] = x1
    ms = jnp.mean(x1 * x1, axis=-1, keepdims=True)
    h2 = x1 * lax.rsqrt(ms + EPS) * g2_ref[...]
    _pack_rows(h2, hp_ref)

    h_hi, h_lo = _split_bf16(h2)
    logits = (_dot(h_hi, wr_hi_ref[...]) + _dot(h_hi, wr_lo_ref[...]) + _dot(h_lo, wr_hi_ref[...])
              + br_ref[...])
    lane = lax.broadcasted_iota(I32, logits.shape, 1).astype(F32)
    work = logits
    vals, idxs = [], []
    for _ in range(TOP_K):
        m = jnp.max(work, axis=-1, keepdims=True)
        ik = jnp.min(jnp.where(work == m, lane, float(LANES)), axis=-1, keepdims=True)
        vals.append(m)
        idxs.append(ik)
        work = jnp.where(lane == ik, -jnp.inf, work)
    es = [jnp.exp(v - vals[0]) for v in vals]
    denom = es[0] + es[1] + es[2] + es[3]
    hot = jnp.zeros(logits.shape, F32)
    for ik in idxs:
        hot = hot + (lane == ik).astype(F32)
    before = _dot(ltri_ref[...], hot.astype(BF16)) + cnt_ref[0:1, :]
    ri = jnp.zeros(logits.shape, F32)
    gt = jnp.zeros(logits.shape, F32)
    for kk in range(TOP_K):
        rank = jnp.sum(jnp.where(lane == idxs[kk], before, 0.0), axis=-1, keepdims=True)
        ri = jnp.where(lane == kk, idxs[kk], ri)
        ri = jnp.where(lane == TOP_K + kk, rank, ri)
        gt = jnp.where(lane == kk, es[kk] / denom, gt)
    ri_ref[...] = ri.T[:2 * TOP_K].astype(I32)
    gate_ref[...] = gt
    cnt_ref[...] = cnt_ref[...] + jnp.sum(hot, axis=0, keepdims=True)


def _sc_mesh():
    info = plsc.get_sparse_core_info()
    mesh = plsc.VectorSubcoreMesh(core_axis_name="c", subcore_axis_name="s")
    return mesh, info.num_cores, info.num_subcores, info.num_lanes


def _sc_scatter_rows(rows, idx, n_out):
    mesh, nc, ns, lanes = _sc_mesh()
    nw = nc * ns
    planes, m, width = rows.shape
    ch = SC_CHUNK
    n_ch = m // ch // nw
    half = planes // 2
    assert planes % 2 == 0 and n_ch * ch * nw == m and idx.shape == (m // ch * TOP_K, ch) and n_ch >= 2

    @functools.partial(
        pl.kernel, mesh=mesh, out_type=jax.ShapeDtypeStruct((planes * n_out, width), rows.dtype),
        scratch_types=([pltpu.VMEM((n_ch * TOP_K, ch), I32)] + [pltpu.VMEM((ch, width), rows.dtype)] * planes
                       + [pltpu.VMEM((TOP_K, ch), I32)] * planes
                       + [pltpu.SemaphoreType.DMA((planes,)), pltpu.SemaphoreType.DMA((planes,))]))
    def scatter_kernel(rows_hbm, idx_hbm, out_hbm, idx_v, *rest):
        bufs, ibufs, rsem, ssem = rest[:planes], rest[planes:2 * planes], rest[2 * planes], rest[2 * planes + 1]
        wid = lax.axis_index("s") * nc + lax.axis_index("c")
        pltpu.sync_copy(idx_hbm.at[pl.ds(wid * n_ch * TOP_K, n_ch * TOP_K)], idx_v)
        base = wid * n_ch * ch

        def read(j, b):
            return pltpu.make_async_copy(rows_hbm.at[b, pl.ds(pl.multiple_of(base + j * ch, ch), ch)], bufs[b], rsem.at[b])

        def scatters(b):
            return [pltpu.make_async_copy(bufs[b], out_hbm.at[ibufs[b].at[kk]], ssem.at[b]) for kk in range(TOP_K)]

        def start_scatters(j, b):
            for kk in range(TOP_K):
                for t in range(0, ch, lanes):
                    ibufs[b][kk, pl.ds(t, lanes)] = idx_v[j * TOP_K + kk, pl.ds(t, lanes)] + b * n_out
            for c in scatters(b):
                c.start()

        def wait_scatters(b):
            for c in scatters(b):
                c.wait()

        def finish(j, b):
            pj, pb = (j, b - half) if b >= half else (j - 1, b + half)
            read(pj, pb).wait()
            start_scatters(pj, pb)

        for b in range(planes):
            read(0, b).start()
        for b in range(half, planes):
            finish(0, b)

        @pl.loop(1, n_ch)
        def _(j):
            for b in range(planes):
                wait_scatters(b)
                read(j, b).start()
                finish(j, b)

        for b in range(half):
            finish(n_ch, b)
        for b in range(planes):
            wait_scatters(b)

    return scatter_kernel(rows, idx).reshape(planes, n_out, width)


def _padfill_kernel(start_ref, len_ref, xs_in_ref, xs_ref, zeros_ref, sem):
    del xs_in_ref
    zeros_ref[...] = jnp.zeros_like(zeros_ref)
    bits = [1 << b for b in reversed(range(EXPERT_BLOCK.bit_length() - 1))]

    def pieces(e):
        n = len_ref[e]
        for bit in bits:
            row0 = start_ref[e] + (n & ~(2 * bit - 1))
            copy = pltpu.make_async_copy(zeros_ref.at[:, pl.ds(0, bit)], xs_ref.at[:, pl.ds(row0, bit)], sem)
            yield (n & bit) != 0, copy

    def start(e, c):
        for on, copy in pieces(e):
            pl.when(on)(copy.start)
        return c

    def wait(e, c):
        for on, copy in pieces(e):
            pl.when(on)(copy.wait)
        return c

    lax.fori_loop(0, start_ref.shape[0], start, 0)
    lax.fori_loop(0, start_ref.shape[0], wait, 0)


def _experts_kernel(be_ref, nb_ref, xs_ref, wgu_ref, bgu_ref, wd_ref, bd_ref, ys_ref, wgu_bf_ref, wd_bf_ref):
    blk = pl.program_id(0)
    de = wd_ref.shape[1]

    @pl.when(jnp.logical_or(blk == 0, be_ref[blk] != be_ref[jnp.maximum(blk - 1, 0)]))
    def _():
        wgu_bf_ref[...] = wgu_ref[0].astype(BF16)
        wd_bf_ref[...] = wd_ref[0].astype(BF16)

    @pl.when(blk < nb_ref[0])
    def _():
        x = _unpack_rows(xs_ref).astype(BF16)
        gu = _dot(x, wgu_bf_ref[...]) + bgu_ref[0]
        glu = jnp.minimum(gu[:, :de], SWIGLU_LIMIT)
        lin = jnp.clip(gu[:, de:], -SWIGLU_LIMIT, SWIGLU_LIMIT)
        act = glu * jax.nn.sigmoid(SWIGLU_ALPHA * glu) * (lin + 1.0)
        y = _dot(act.astype(BF16), wd_bf_ref[...]) + bd_ref[0]
        _pack_rows(y, ys_ref)

    @pl.when(blk >= nb_ref[0])
    def _():
        ys_ref[...] = jnp.zeros_like(ys_ref)


def _sc_gather_rows(table, idx):
    mesh, nc, ns, lanes = _sc_mesh()
    nw = nc * ns
    planes, n_tab, width = table.shape
    n_idx_rows, ch = idx.shape
    n_ch = n_idx_rows // nw
    m = n_idx_rows * ch
    half = planes // 2
    assert planes % 2 == 0 and ch == SC_CHUNK and n_ch * nw == n_idx_rows and n_ch >= 2

    @functools.partial(
        pl.kernel, mesh=mesh, out_type=jax.ShapeDtypeStruct((planes, m, width), table.dtype),
        scratch_types=([pltpu.VMEM((n_ch, ch), I32)] + [pltpu.VMEM((ch, width), table.dtype)] * planes
                       + [pltpu.VMEM((8, ch), I32)] * planes
                       + [pltpu.SemaphoreType.DMA((planes,)), pltpu.SemaphoreType.DMA((planes,))]))
    def gather_kernel(table_hbm, idx_hbm, out_hbm, idx_v, *rest):
        bufs, ibufs, gsem, wsem = rest[:planes], rest[planes:2 * planes], rest[2 * planes], rest[2 * planes + 1]
        wid = lax.axis_index("s") * nc + lax.axis_index("c")
        pltpu.sync_copy(idx_hbm.at[pl.ds(wid * n_ch, n_ch)], idx_v)
        base = wid * n_ch * ch

        def gather(b):
            return pltpu.make_async_copy(table_hbm.at[ibufs[b].at[0]], bufs[b], gsem.at[b])

        def start_gather(j, b):
            for t in range(0, ch, lanes):
                ibufs[b][0, pl.ds(t, lanes)] = idx_v[j, pl.ds(t, lanes)] + b * n_tab
            gather(b).start()

        def write(j, b):
            return pltpu.make_async_copy(bufs[b], out_hbm.at[b, pl.ds(pl.multiple_of(base + j * ch, ch), ch)], wsem.at[b])

        def finish(j, b):
            pj, pb = (j, b - half) if b >= half else (j - 1, b + half)
            gather(pb).wait()
            write(pj, pb).start()

        for b in range(planes):
            start_gather(0, b)
        for b in range(half, planes):
            finish(0, b)

        @pl.loop(1, n_ch)
        def _(j):
            for b in range(planes):
                write(j - 1, b).wait()
                start_gather(j, b)
                finish(j, b)

        for b in range(half):
            finish(n_ch, b)
        for b in range(planes):
            write(n_ch - 1, b).wait()

    return gather_kernel(table.reshape(planes * n_tab, width), idx)


def _combine_kernel(x1_ref, gate_ref, y0_ref, y1_ref, y2_ref, y3_ref, *rest):
    o_ref = rest[-1]
    gate = gate_ref[...]
    out = x1_ref[...]
    for kk, y_ref in enumerate((y0_ref, y1_ref, y2_ref, y3_ref)):
        out = out + gate[:, kk:kk + 1] * _unpack_rows(y_ref)
    o_ref[...] = out


def _const_spec(shape):
    nd = len(shape)
    return pl.BlockSpec(shape, lambda *_: (0,) * nd)


def _layer(x, norm1_g, w_in, q_norm_g, k_norm_g, w_pool_grp, pool_scale, w_pool_up, w_attn_up, w_out, norm2_g,
           w_router, b_router, w_gate_up, b_gate_up, w_down, b_down):
    B, S, D = x.shape
    N = B * S
    pw = w_pool_up.shape[0]
    sw = w_attn_up.shape[0]
    n_exp = w_router.shape[1]
    de = w_down.shape[1]
    heads = sw // SB_HEAD_DIM
    assert pw == len(POOL_WINDOWS) * POOL_GROUP_DIM and heads % 2 == 0 and n_exp <= LANES
    assert D == 2 * PACK_ROWS * LANES and w_in.shape[1] == pw + 3 * sw + 2 * D
    tm = 512 if S % 512 == 0 else 256
    assert S % tm == 0 and S % ATTN_BLOCK == 0
    xf = x.reshape(N, D)
    cparams = functools.partial(pltpu.CompilerParams, vmem_limit_bytes=VMEM_LIMIT)

    hsum = (jnp.arange(sw)[:, None] // SB_HEAD_DIM == jnp.arange(sw)[None, :] // SB_HEAD_DIM).astype(BF16)
    nt = S // tm
    tok_spec = lambda w: pl.BlockSpec((tm, w), lambda b, i: (b * nt + i, 0))
    q2, k2, v2, ga, pg = pl.pallas_call(
        functools.partial(_mixer_in_kernel, tm=tm, pw=pw, sw=sw),
        grid=(B, nt),
        in_specs=[tok_spec(D), _const_spec((1, D)), _const_spec((D, w_in.shape[1])), _const_spec((1, sw)),
                  _const_spec((1, sw)), _const_spec((sw, sw)),
                  _const_spec((len(POOL_WINDOWS), POOL_GROUP_DIM, POOL_GROUP_DIM)), _const_spec((1, pw)),
                  _const_spec((pw, D))],
        out_specs=[tok_spec(sw), tok_spec(sw), tok_spec(sw), tok_spec(D), tok_spec(D)],
        out_shape=[jax.ShapeDtypeStruct((N, sw), BF16)] * 3 + [jax.ShapeDtypeStruct((N, D), BF16)] * 2,
        scratch_shapes=[pltpu.VMEM((POOL_HALO, pw), F32)],
        compiler_params=cparams(dimension_semantics=("arbitrary", "arbitrary")),
        name="mixer_in",
    )(xf, norm1_g.reshape(1, D), w_in.astype(BF16), jnp.tile(q_norm_g, heads).reshape(1, sw),
      jnp.tile(k_norm_g, heads).reshape(1, sw), hsum, w_pool_grp.astype(BF16), pool_scale.reshape(1, pw),
      w_pool_up.astype(BF16))

    bq = ATTN_BLOCK
    nq = S // bq
    jj = jnp.arange(bq)
    tri_half = jnp.concatenate([(jj[:, None] >= jj[None, :]).astype(BF16), jnp.ones((bq, bq), BF16)], axis=1)
    tri = jnp.concatenate([tri_half, tri_half], axis=0)
    n_pairs = heads // 2
    kv_spec = pl.BlockSpec((S, sw), lambda b, qi: (b, 0), pipeline_mode=pl.Buffered(1))
    sba = pl.pallas_call(
        functools.partial(_attn_kernel, n_pairs=n_pairs),
        grid=(B, nq),
        in_specs=[pl.BlockSpec((bq, sw), lambda b, qi: (b * nq + qi, 0)), kv_spec, kv_spec,
                  _const_spec((2 * bq, 2 * bq))],
        out_specs=pl.BlockSpec((bq, sw), lambda b, qi: (b * nq + qi, 0)),
        out_shape=jax.ShapeDtypeStruct((N, sw), BF16),
        scratch_shapes=([pltpu.VMEM((2 * bq, LANES), BF16)] * n_pairs + [pltpu.VMEM((2 * bq, LANES), F32)] * (2 * n_pairs)),
        compiler_params=cparams(dimension_semantics=("arbitrary", "arbitrary")),
        name="sb_attn",
    )(q2, k2, v2, tri)

    wr = jnp.zeros((D, LANES), F32).at[:, :n_exp].set(w_router)
    wr_hi = wr.astype(BF16)
    wr_lo = (wr - wr_hi.astype(F32)).astype(BF16)
    br = jnp.full((1, LANES), -jnp.inf, F32).at[0, :n_exp].set(b_router)
    ltri = (jnp.arange(tm)[:, None] > jnp.arange(tm)[None, :]).astype(BF16)
    assert N % (MOE_GROUPS * tm) == 0
    ng = N // MOE_GROUPS
    steps = ng // tm
    n_assign = ng * TOP_K
    n_blocks = -(-(n_assign + n_exp * (EXPERT_BLOCK - 1)) // EXPERT_BLOCK)
    n_rows = n_blocks * EXPERT_BLOCK
    plane_spec = lambda rows, index: pl.BlockSpec((PACK_ROWS, rows, LANES), lambda i, *_: (0, index(i, *_), 0))
    row_spec = lambda w: pl.BlockSpec((tm, w), lambda i: (i, 0))
    w_attn_up_bf, w_out_bf, g2 = w_attn_up.astype(BF16), w_out.astype(BF16), norm2_g.reshape(1, D)
    bgu, bdn = b_gate_up.reshape(n_exp, 1, 2 * de), b_down.reshape(n_exp, 1, D)
    out = None
    for grp in range(MOE_GROUPS):
        grp_spec = lambda w, first=grp * steps: pl.BlockSpec((tm, w), lambda i: (first + i, 0))

        x1, hpk, ri, gate, cnt = pl.pallas_call(
            functools.partial(_mixer_out_kernel, tm=tm),
            grid=(steps,),
            in_specs=[grp_spec(D), grp_spec(D), grp_spec(D), grp_spec(sw), _const_spec((sw, D)), _const_spec((D, D)),
                      _const_spec((1, D)), _const_spec((D, LANES)), _const_spec((D, LANES)), _const_spec((1, LANES)),
                      _const_spec((tm, tm))],
            out_specs=[row_spec(D), plane_spec(tm, lambda i: i), pl.BlockSpec((2 * TOP_K, tm), lambda i: (0, i)),
                       row_spec(LANES), _const_spec((8, LANES))],
            out_shape=[jax.ShapeDtypeStruct((ng, D), F32), jax.ShapeDtypeStruct((PACK_ROWS, ng, LANES), U32),
                       jax.ShapeDtypeStruct((2 * TOP_K, ng), I32), jax.ShapeDtypeStruct((ng, LANES), F32),
                       jax.ShapeDtypeStruct((8, LANES), F32)],
            compiler_params=cparams(dimension_semantics=("arbitrary",)),
            name="mixer_out",
        )(xf, pg, ga, sba, w_attn_up_bf, w_out_bf, g2, wr_hi, wr_lo, br, ltri)

        counts = cnt[0, :n_exp].astype(I32)
        padded = (counts + EXPERT_BLOCK - 1) // EXPERT_BLOCK * EXPERT_BLOCK
        padded_end = jnp.cumsum(padded)
        start_pad = padded_end - padded
        idx = ri[:TOP_K]
        dest = ri[TOP_K:]
        for e in range(n_exp):
            dest = dest + jnp.where(idx == e, start_pad[e], 0)
        block_start = jnp.arange(n_blocks, dtype=I32) * EXPERT_BLOCK
        block_expert = jnp.minimum(jnp.sum((padded_end[None, :] <= block_start[:, None]).astype(I32), axis=1),
                                   n_exp - 1)
        n_used = (padded_end[-1] // EXPERT_BLOCK).astype(I32).reshape(1)

        dst_idx = dest.reshape(TOP_K, ng // SC_CHUNK, SC_CHUNK).transpose(1, 0, 2).reshape(-1, SC_CHUNK)
        xs = _sc_scatter_rows(hpk, dst_idx, n_rows)
        xs = pl.pallas_call(
            _padfill_kernel,
            grid_spec=pltpu.PrefetchScalarGridSpec(
                num_scalar_prefetch=2,
                grid=(1,),
                in_specs=[pl.BlockSpec(memory_space=pl.ANY)],
                out_specs=pl.BlockSpec(memory_space=pl.ANY),
                scratch_shapes=[pltpu.VMEM((PACK_ROWS, EXPERT_BLOCK // 2, LANES), U32), pltpu.SemaphoreType.DMA(())],
            ),
            out_shape=jax.ShapeDtypeStruct((PACK_ROWS, n_rows, LANES), U32),
            input_output_aliases={2: 0},
            compiler_params=cparams(dimension_semantics=("arbitrary",)),
            name="padfill",
        )(start_pad + counts, padded - counts, xs)

        ys = pl.pallas_call(
            _experts_kernel,
            grid_spec=pltpu.PrefetchScalarGridSpec(
                num_scalar_prefetch=2,
                grid=(n_blocks,),
                in_specs=[plane_spec(EXPERT_BLOCK, lambda i, be, nb: jnp.minimum(i, nb[0] - 1)),
                          pl.BlockSpec((1, D, 2 * de), lambda i, be, nb: (be[i], 0, 0)),
                          pl.BlockSpec((1, 1, 2 * de), lambda i, be, nb: (be[i], 0, 0)),
                          pl.BlockSpec((1, de, D), lambda i, be, nb: (be[i], 0, 0)),
                          pl.BlockSpec((1, 1, D), lambda i, be, nb: (be[i], 0, 0))],
                out_specs=plane_spec(EXPERT_BLOCK, lambda i, be, nb: i),
                scratch_shapes=[pltpu.VMEM((D, 2 * de), BF16), pltpu.VMEM((de, D), BF16)],
            ),
            out_shape=jax.ShapeDtypeStruct((PACK_ROWS, n_rows, LANES), U32),
            compiler_params=cparams(dimension_semantics=("arbitrary",)),
            name="experts",
        )(block_expert, n_used, xs, w_gate_up, bgu, w_down, bdn)

        y4 = _sc_gather_rows(ys, dest.reshape(-1, SC_CHUNK))

        operands = [x1, gate, y4, y4, y4, y4] + ([] if out is None else [out])
        out = pl.pallas_call(
            _combine_kernel,
            grid=(steps,),
            in_specs=([row_spec(D), row_spec(LANES)]
                      + [plane_spec(tm, lambda i, kk=kk: kk * steps + i) for kk in range(TOP_K)]
                      + ([] if out is None else [pl.BlockSpec(memory_space=pl.ANY)])),
            out_specs=grp_spec(D),
            out_shape=jax.ShapeDtypeStruct((N, D), F32),
            input_output_aliases={} if out is None else {len(operands) - 1: 0},
            compiler_params=cparams(dimension_semantics=("arbitrary",)),
            name="combine",
        )(*operands)
    return out.reshape(B, S, D)


def kernel(x, norm1_g, w_in, q_norm_g, k_norm_g, w_pool_grp, pool_scale, w_pool_up, w_attn_up, w_out, norm2_g,
           w_router, b_router, w_gate_up, b_gate_up, w_down, b_down):
    for layer in range(norm1_g.shape[0]):
        x = _layer(x, norm1_g[layer], w_in[layer], q_norm_g[layer], k_norm_g[layer], w_pool_grp[layer],
                   pool_scale[layer], w_pool_up[layer], w_attn_up[layer], w_out[layer], norm2_g[layer],
                   w_router[layer], b_router[layer], w_gate_up[layer], b_gate_up[layer], w_down[layer],
                   b_down[layer])
    return x
```

```python
import functools

import jax
import jax.numpy as jnp
from jax import lax
from jax.experimental import pallas as pl
from jax.experimental.pallas import tpu as pltpu
from jax.experimental.pallas import tpu_sc as plsc

F32 = jnp.float32
BF16 = jnp.bfloat16
U32 = jnp.uint32
I32 = jnp.int32

EPS = 1e-6
POOL_WINDOWS = (2, 4, 8, 16)
POOL_GROUP_DIM = 128
POOL_HALO = 16
SB_HEAD_DIM = 64
TOP_K = 4
SWIGLU_LIMIT = 7.0
SWIGLU_ALPHA = 1.702
EXPERT_BLOCK = 512
LANES = 128
PACK_ROWS = 4
ATTN_BLOCK = 128
ATTN_SUB = 2
ATTN_EXIT_BITS = 70.0
LOG2_E = 1.4426950408889634
VMEM_LIMIT = 56 * 1024 * 1024
SC_CHUNK = 128
MOE_GROUPS = 2


def _dot(a, b):
    return jnp.dot(a, b, preferred_element_type=F32)


def _split_bf16(x):
    hi = x.astype(BF16)
    lo = (x - hi.astype(F32)).astype(BF16)
    return hi, lo


def _pack_rows(v, out_ref):
    half = v.shape[1] // 2
    lo = lax.bitcast_convert_type(v[:, :half].astype(BF16).astype(F32), U32) >> 16
    hi = lax.bitcast_convert_type(v[:, half:].astype(BF16).astype(F32), U32) & jnp.uint32(0xFFFF0000)
    w = lo | hi
    for c in range(PACK_ROWS):
        out_ref[c] = w[:, c * LANES:(c + 1) * LANES]


def _unpack_rows(ref):
    los, his = [], []
    for c in range(PACK_ROWS):
        w = ref[c]
        los.append(lax.bitcast_convert_type(w << 16, F32))
        his.append(lax.bitcast_convert_type(w & jnp.uint32(0xFFFF0000), F32))
    return jnp.concatenate(los + his, axis=1)


def _mixer_in_kernel(x_ref, g1_ref, win_ref, gq_ref, gk_ref, hsum_ref, wgrp_ref, pscale_ref, wpu_ref,
                     q_ref, k_ref, v_ref, ga_ref, p_ref, tail_ref, *, tm, pw, sw):
    i = pl.program_id(1)

    @pl.when(i == 0)
    def _():
        tail_ref[...] = jnp.zeros_like(tail_ref)

    x = x_ref[...]
    ms = jnp.mean(x * x, axis=-1, keepdims=True)
    h = (x * lax.rsqrt(ms + EPS) * g1_ref[...]).astype(BF16)

    u = _dot(h, win_ref[:, 0:pw])
    xx = jnp.concatenate([tail_ref[...], u], axis=0)
    tail_ref[...] = u[tm - POOL_HALO:, :]
    pos = i * tm + lax.broadcasted_iota(I32, (tm, POOL_GROUP_DIM), 0)
    mixed = []
    for g, w in enumerate(POOL_WINDOWS):
        s = xx[:, g * POOL_GROUP_DIM:(g + 1) * POOL_GROUP_DIM]
        step = 1
        while step < w:
            s = s + pltpu.roll(s, step, axis=0)
            step *= 2
        count = jnp.minimum(pos + 1, w).astype(F32)
        ug = u[:, g * POOL_GROUP_DIM:(g + 1) * POOL_GROUP_DIM]
        d = s[POOL_HALO:, :] / count - ug
        mixed.append(_dot(d.astype(BF16), wgrp_ref[g]))
    pm = jnp.concatenate(mixed, axis=1) * pscale_ref[...]
    pool_out = _dot(pm.astype(BF16), wpu_ref[...])
    d_model = pool_out.shape[1]
    g_pool = _dot(h, win_ref[:, pw + 3 * sw:pw + 3 * sw + d_model])
    p_ref[...] = (jax.nn.sigmoid(g_pool) * pool_out).astype(BF16)
    g_attn = _dot(h, win_ref[:, pw + 3 * sw + d_model:pw + 3 * sw + 2 * d_model])
    ga_ref[...] = jax.nn.sigmoid(g_attn).astype(BF16)

    def head_norm(t, gain):
        ss = _dot((t * t).astype(BF16), hsum_ref[...])
        return t * lax.rsqrt(ss * (1.0 / SB_HEAD_DIM) + EPS) * gain

    q = _dot(h, win_ref[:, pw:pw + sw])
    q_ref[...] = (head_norm(q, gq_ref[...]) * (SB_HEAD_DIM ** -0.5 * LOG2_E)).astype(BF16)
    k = _dot(h, win_ref[:, pw + sw:pw + 2 * sw])
    k_ref[...] = head_norm(k, gk_ref[...]).astype(BF16)
    v_ref[...] = _dot(h, win_ref[:, pw + 2 * sw:pw + 3 * sw]).astype(BF16)


def _attn_kernel(q_ref, k_ref, v_ref, tri_ref, o_ref, *scratch, n_pairs):
    units = [(sub, p) for sub in range(ATTN_SUB) for p in range(n_pairs)]
    n_units = len(units)
    qs, acc, rr = scratch[:n_units], scratch[n_units:2 * n_units], scratch[2 * n_units:]
    bq = ATTN_BLOCK
    first_block = pl.program_id(1) * ATTN_SUB
    first_head = lax.broadcasted_iota(I32, (bq, LANES), 1) < SB_HEAD_DIM
    for u, (sub, p) in enumerate(units):
        q2 = q_ref[sub * bq:(sub + 1) * bq, p * LANES:(p + 1) * LANES]
        qs[u][:bq] = jnp.where(first_head, q2, jnp.zeros_like(q2))
        qs[u][bq:] = jnp.where(first_head, jnp.zeros_like(q2), q2)
    row = lax.broadcasted_iota(I32, (2 * bq, bq), 0)
    col = lax.broadcasted_iota(I32, (2 * bq, bq), 1)
    causal = col < (row & (bq - 1))
    contract_last = (((1,), (1,)), ((), ()))

    def softplus(z):
        return jnp.maximum(z, 0.0) + jnp.log2(1.0 + jnp.exp2(-jnp.abs(z)))

    def suffix_sums(sp):
        return _dot(sp.astype(BF16), tri_ref[...])

    def cols(ref, block, p):
        start = pl.multiple_of(jnp.maximum(block, 0) * bq, bq)
        return ref[pl.ds(start, bq), p * LANES:(p + 1) * LANES]

    def scores(u, block):
        return lax.dot_general(qs[u][...], cols(k_ref, block, units[u][1]), contract_last, preferred_element_type=F32)

    diag = [first_block + sub for sub, _ in units]
    z_d = [scores(u, diag[u]) for u in range(n_units)]
    z_n = [scores(u, diag[u] - 1) for u in range(n_units)]
    s_d = [suffix_sums(jnp.where(causal, softplus(z), 0.0)) for z in z_d]
    s_n = [suffix_sums(jnp.where(diag[u] >= 1, softplus(z_n[u]), 0.0)) for u in range(n_units)]
    for u, (_, p) in enumerate(units):
        a_d = jnp.where(causal, jnp.exp2(z_d[u] - s_d[u][:, :bq]), 0.0)
        r_d = s_d[u][:, bq:]
        a_n = jnp.where(diag[u] >= 1, jnp.exp2(z_n[u] - (r_d + s_n[u][:, :bq])), 0.0)
        acc[u][...] = (_dot(a_d.astype(BF16), cols(v_ref, diag[u], p)) + _dot(a_n.astype(BF16), cols(v_ref, diag[u] - 1, p)))
        rr[u][...] = r_d + s_n[u][:, bq:]

    def r_min():
        m = rr[0][...]
        for u in range(1, n_units):
            m = jnp.minimum(m, rr[u][...])
        return jnp.min(m)

    def cond(c):
        back, rm = c
        return jnp.logical_and(diag[-1] - back >= 0, rm < ATTN_EXIT_BITS)

    def body(c):
        back, _ = c
        blocks = [d - back for d in diag]
        zs = [scores(u, blocks[u]) for u in range(n_units)]
        ss = [suffix_sums(jnp.where(blocks[u] >= 0, softplus(zs[u]), 0.0)) for u in range(n_units)]
        for u, (_, p) in enumerate(units):
            r = rr[u][...]
            a = jnp.where(blocks[u] >= 0, jnp.exp2(zs[u] - (r + ss[u][:, :bq])), 0.0)
            acc[u][...] += _dot(a.astype(BF16), cols(v_ref, blocks[u], p))
            rr[u][...] = r + ss[u][:, bq:]
        return back + 1, r_min()

    lax.while_loop(cond, body, (2, r_min()))
    for u, (sub, p) in enumerate(units):
        o_ref[sub * bq:(sub + 1) * bq, p * LANES:(p + 1) * LANES] = (
            jnp.where(first_head, acc[u][:bq], acc[u][bq:]).astype(BF16))


def _mixer_out_kernel(x_ref, p_ref, ga_ref, sba_ref, wau_ref, wout_ref, g2_ref, wr_hi_ref, wr_lo_ref, br_ref,
                      ltri_ref, x1_ref, hp_ref, ri_ref, gate_ref, cnt_ref, *, tm):
    step = pl.program_id(0)

    @pl.when(step == 0)
    def _():
        cnt_ref[...] = jnp.zeros_like(cnt_ref)

    attn_out = _dot(sba_ref[...], wau_ref[...])
    merged = p_ref[...].astype(F32) + ga_ref[...].astype(F32) * attn_out
    x1 = x_ref[...] + _dot(merged.astype(BF16), wout_ref[...])
    x1_ref[...] = x1
    ms = jnp.mean(x1 * x1, axis=-1, keepdims=True)
    h2 = x1 * lax.rsqrt(ms + EPS) * g2_ref[...]
    _pack_rows(h2, hp_ref)

    h_hi, h_lo = _split_bf16(h2)
    logits = (_dot(h_hi, wr_hi_ref[...]) + _dot(h_hi, wr_lo_ref[...]) + _dot(h_lo, wr_hi_ref[...])
              + br_ref[...])
    lane = lax.broadcasted_iota(I32, logits.shape, 1).astype(F32)
    work = logits
    vals, idxs = [], []
    for _ in range(TOP_K):
        m = jnp.max(work, axis=-1, keepdims=True)
        ik = jnp.min(jnp.where(work == m, lane, float(LANES)), axis=-1, keepdims=True)
        vals.append(m)
        idxs.append(ik)
        work = jnp.where(lane == ik, -jnp.inf, work)
    es = [jnp.exp(v - vals[0]) for v in vals]
    denom = es[0] + es[1] + es[2] + es[3]
    hot = jnp.zeros(logits.shape, F32)
    for ik in idxs:
        hot = hot + (lane == ik).astype(F32)
    before = _dot(ltri_ref[...], hot.astype(BF16)) + cnt_ref[0:1, :]
    ri = jnp.zeros(logits.shape, F32)
    gt = jnp.zeros(logits.shape, F32)
    for kk in range(TOP_K):
        rank = jnp.sum(jnp.where(lane == idxs[kk], before, 0.0), axis=-1, keepdims=True)
        ri = jnp.where(lane == kk, idxs[kk], ri)
        ri = jnp.where(lane == TOP_K + kk, rank, ri)
        gt = jnp.where(lane == kk, es[kk] / denom, gt)
    ri_ref[...] = ri.T[:2 * TOP_K].astype(I32)
    gate_ref[...] = gt
    cnt_ref[...] = cnt_ref[...] + jnp.sum(hot, axis=0, keepdims=True)


def _sc_mesh():
    info = plsc.get_sparse_core_info()
    mesh = plsc.VectorSubcoreMesh(core_axis_name="c", subcore_axis_name="s")
    return mesh, info.num_cores, info.num_subcores, info.num_lanes


def _sc_scatter_rows(rows, idx, n_out):
    mesh, nc, ns, lanes = _sc_mesh()
    nw = nc * ns
    planes, m, width = rows.shape
    ch = SC_CHUNK
    n_ch = m // ch // nw
    half = planes // 2
    assert planes % 2 == 0 and n_ch * ch * nw == m and idx.shape == (m // ch * TOP_K, ch) and n_ch >= 2

    @functools.partial(
        pl.kernel, mesh=mesh, out_type=jax.ShapeDtypeStruct((planes * n_out, width), rows.dtype),
        scratch_types=([pltpu.VMEM((n_ch * TOP_K, ch), I32)] + [pltpu.VMEM((ch, width), rows.dtype)] * planes
                       + [pltpu.VMEM((TOP_K, ch), I32)] * planes
                       + [pltpu.SemaphoreType.DMA((planes,)), pltpu.SemaphoreType.DMA((planes,))]))
    def scatter_kernel(rows_hbm, idx_hbm, out_hbm, idx_v, *rest):
        bufs, ibufs, rsem, ssem = rest[:planes], rest[planes:2 * planes], rest[2 * planes], rest[2 * planes + 1]
        wid = lax.axis_index("s") * nc + lax.axis_index("c")
        pltpu.sync_copy(idx_hbm.at[pl.ds(wid * n_ch * TOP_K, n_ch * TOP_K)], idx_v)
        base = wid * n_ch * ch

        def read(j, b):
            return pltpu.make_async_copy(rows_hbm.at[b, pl.ds(pl.multiple_of(base + j * ch, ch), ch)], bufs[b], rsem.at[b])

        def scatters(b):
            return [pltpu.make_async_copy(bufs[b], out_hbm.at[ibufs[b].at[kk]], ssem.at[b]) for kk in range(TOP_K)]

        def start_scatters(j, b):
            for kk in range(TOP_K):
                for t in range(0, ch, lanes):
                    ibufs[b][kk, pl.ds(t, lanes)] = idx_v[j * TOP_K + kk, pl.ds(t, lanes)] + b * n_out
            for c in scatters(b):
                c.start()

        def wait_scatters(b):
            for c in scatters(b):
                c.wait()

        def finish(j, b):
            pj, pb = (j, b - half) if b >= half else (j - 1, b + half)
            read(pj, pb).wait()
            start_scatters(pj, pb)

        for b in range(planes):
            read(0, b).start()
        for b in range(half, planes):
            finish(0, b)

        @pl.loop(1, n_ch)
        def _(j):
            for b in range(planes):
                wait_scatters(b)
                read(j, b).start()
                finish(j, b)

        for b in range(half):
            finish(n_ch, b)
        for b in range(planes):
            wait_scatters(b)

    return scatter_kernel(rows, idx).reshape(planes, n_out, width)


def _padfill_kernel(start_ref, len_ref, xs_in_ref, xs_ref, zeros_ref, sem):
    del xs_in_ref
    zeros_ref[...] = jnp.zeros_like(zeros_ref)
    bits = [1 << b for b in reversed(range(EXPERT_BLOCK.bit_length() - 1))]

    def pieces(e):
        n = len_ref[e]
        for bit in bits:
            row0 = start_ref[e] + (n & ~(2 * bit - 1))
            copy = pltpu.make_async_copy(zeros_ref.at[:, pl.ds(0, bit)], xs_ref.at[:, pl.ds(row0, bit)], sem)
            yield (n & bit) != 0, copy

    def start(e, c):
        for on, copy in pieces(e):
            pl.when(on)(copy.start)
        return c

    def wait(e, c):
        for on, copy in pieces(e):
            pl.when(on)(copy.wait)
        return c

    lax.fori_loop(0, start_ref.shape[0], start, 0)
    lax.fori_loop(0, start_ref.shape[0], wait, 0)


def _experts_kernel(be_ref, nb_ref, xs_ref, wgu_ref, bgu_ref, wd_ref, bd_ref, ys_ref, wgu_bf_ref, wd_bf_ref):
    blk = pl.program_id(0)
    de = wd_ref.shape[1]

    @pl.when(jnp.logical_or(blk == 0, be_ref[blk] != be_ref[jnp.maximum(blk - 1, 0)]))
    def _():
        wgu_bf_ref[...] = wgu_ref[0].astype(BF16)
        wd_bf_ref[...] = wd_ref[0].astype(BF16)

    @pl.when(blk < nb_ref[0])
    def _():
        x = _unpack_rows(xs_ref).astype(BF16)
        gu = _dot(x, wgu_bf_ref[...]) + bgu_ref[0]
        glu = jnp.minimum(gu[:, :de], SWIGLU_LIMIT)
        lin = jnp.clip(gu[:, de:], -SWIGLU_LIMIT, SWIGLU_LIMIT)
        act = glu * jax.nn.sigmoid(SWIGLU_ALPHA * glu) * (lin + 1.0)
        y = _dot(act.astype(BF16), wd_bf_ref[...]) + bd_ref[0]
        _pack_rows(y, ys_ref)

    @pl.when(blk >= nb_ref[0])
    def _():
        ys_ref[...] = jnp.zeros_like(ys_ref)


def _sc_gather_rows(table, idx):
    mesh, nc, ns, lanes = _sc_mesh()
    nw = nc * ns
    planes, n_tab, width = table.shape
    n_idx_rows, ch = idx.shape
    n_ch = n_idx_rows // nw
    m = n_idx_rows * ch
    half = planes // 2
    assert planes % 2 == 0 and ch == SC_CHUNK and n_ch * nw == n_idx_rows and n_ch >= 2

    @functools.partial(
        pl.kernel, mesh=mesh, out_type=jax.ShapeDtypeStruct((planes, m, width), table.dtype),
        scratch_types=([pltpu.VMEM((n_ch, ch), I32)] + [pltpu.VMEM((ch, width), table.dtype)] * planes
                       + [pltpu.VMEM((8, ch), I32)] * planes
                       + [pltpu.SemaphoreType.DMA((planes,)), pltpu.SemaphoreType.DMA((planes,))]))
    def gather_kernel(table_hbm, idx_hbm, out_hbm, idx_v, *rest):
        bufs, ibufs, gsem, wsem = rest[:planes], rest[planes:2 * planes], rest[2 * planes], rest[2 * planes + 1]
        wid = lax.axis_index("s") * nc + lax.axis_index("c")
        pltpu.sync_copy(idx_hbm.at[pl.ds(wid * n_ch, n_ch)], idx_v)
        base = wid * n_ch * ch

        def gather(b):
            return pltpu.make_async_copy(table_hbm.at[ibufs[b].at[0]], bufs[b], gsem.at[b])

        def start_gather(j, b):
            for t in range(0, ch, lanes):
                ibufs[b][0, pl.ds(t, lanes)] = idx_v[j, pl.ds(t, lanes)] + b * n_tab
            gather(b).start()

        def write(j, b):
            return pltpu.make_async_copy(bufs[b], out_hbm.at[b, pl.ds(pl.multiple_of(base + j * ch, ch), ch)], wsem.at[b])

        def finish(j, b):
            pj, pb = (j, b - half) if b >= half else (j - 1, b + half)
            gather(pb).wait()
            write(pj, pb).start()

        for b in range(planes):
            start_gather(0, b)
        for b in range(half, planes):
            finish(0, b)

        @pl.loop(1, n_ch)
        def _(j):
            for b in range(planes):
                write(j - 1, b).wait()
                start_gather(j, b)
                finish(j, b)

        for b in range(half):
            finish(n_ch, b)
        for b in range(planes):
            write(n_ch - 1, b).wait()

    return gather_kernel(table.reshape(planes * n_tab, width), idx)


def _combine_kernel(x1_ref, gate_ref, y0_ref, y1_ref, y2_ref, y3_ref, *rest):
    o_ref = rest[-1]
    gate = gate_ref[...]
    out = x1_ref[...]
    for kk, y_ref in enumerate((y0_ref, y1_ref, y2_ref, y3_ref)):
        out = out + gate[:, kk:kk + 1] * _unpack_rows(y_ref)
    o_ref[...] = out


def _const_spec(shape):
    nd = len(shape)
    return pl.BlockSpec(shape, lambda *_: (0,) * nd)


def _layer(x, norm1_g, w_in, q_norm_g, k_norm_g, w_pool_grp, pool_scale, w_pool_up, w_attn_up, w_out, norm2_g,
           w_router, b_router, w_gate_up, b_gate_up, w_down, b_down):
    B, S, D = x.shape
    N = B * S
    pw = w_pool_up.shape[0]
    sw = w_attn_up.shape[0]
    n_exp = w_router.shape[1]
    de = w_down.shape[1]
    heads = sw // SB_HEAD_DIM
    assert pw == len(POOL_WINDOWS) * POOL_GROUP_DIM and heads % 2 == 0 and n_exp <= LANES
    assert D == 2 * PACK_ROWS * LANES and w_in.shape[1] == pw + 3 * sw + 2 * D
    tm = 512 if S % 512 == 0 else 256
    assert S % tm == 0 and S % (ATTN_SUB * ATTN_BLOCK) == 0
    xf = x.reshape(N, D)
    cparams = functools.partial(pltpu.CompilerParams, vmem_limit_bytes=VMEM_LIMIT)

    hsum = (jnp.arange(sw)[:, None] // SB_HEAD_DIM == jnp.arange(sw)[None, :] // SB_HEAD_DIM).astype(BF16)
    nt = S // tm
    tok_spec = lambda w: pl.BlockSpec((tm, w), lambda b, i: (b * nt + i, 0))
    q2, k2, v2, ga, pg = pl.pallas_call(
        functools.partial(_mixer_in_kernel, tm=tm, pw=pw, sw=sw),
        grid=(B, nt),
        in_specs=[tok_spec(D), _const_spec((1, D)), _const_spec((D, w_in.shape[1])), _const_spec((1, sw)),
                  _const_spec((1, sw)), _const_spec((sw, sw)),
                  _const_spec((len(POOL_WINDOWS), POOL_GROUP_DIM, POOL_GROUP_DIM)), _const_spec((1, pw)),
                  _const_spec((pw, D))],
        out_specs=[tok_spec(sw), tok_spec(sw), tok_spec(sw), tok_spec(D), tok_spec(D)],
        out_shape=[jax.ShapeDtypeStruct((N, sw), BF16)] * 3 + [jax.ShapeDtypeStruct((N, D), BF16)] * 2,
        scratch_shapes=[pltpu.VMEM((POOL_HALO, pw), F32)],
        compiler_params=cparams(dimension_semantics=("arbitrary", "arbitrary")),
        name="mixer_in",
    )(xf, norm1_g.reshape(1, D), w_in.astype(BF16), jnp.tile(q_norm_g, heads).reshape(1, sw),
      jnp.tile(k_norm_g, heads).reshape(1, sw), hsum, w_pool_grp.astype(BF16), pool_scale.reshape(1, pw),
      w_pool_up.astype(BF16))

    bq = ATTN_BLOCK
    rows_q = ATTN_SUB * bq
    nq = S // rows_q
    jj = jnp.arange(bq)
    tri = jnp.concatenate([(jj[:, None] >= jj[None, :]).astype(BF16), jnp.ones((bq, bq), BF16)], axis=1)
    n_pairs = heads // 2
    n_units = ATTN_SUB * n_pairs
    kv_spec = pl.BlockSpec((S, sw), lambda b, qi: (b, 0), pipeline_mode=pl.Buffered(1))
    sba = pl.pallas_call(
        functools.partial(_attn_kernel, n_pairs=n_pairs),
        grid=(B, nq),
        in_specs=[pl.BlockSpec((rows_q, sw), lambda b, qi: (b * nq + qi, 0)), kv_spec, kv_spec,
                  _const_spec((bq, 2 * bq))],
        out_specs=pl.BlockSpec((rows_q, sw), lambda b, qi: (b * nq + qi, 0)),
        out_shape=jax.ShapeDtypeStruct((N, sw), BF16),
        scratch_shapes=([pltpu.VMEM((2 * bq, LANES), BF16)] * n_units + [pltpu.VMEM((2 * bq, LANES), F32)] * (2 * n_units)),
        compiler_params=cparams(dimension_semantics=("arbitrary", "arbitrary")),
        name="sb_attn",
    )(q2, k2, v2, tri)

    wr = jnp.zeros((D, LANES), F32).at[:, :n_exp].set(w_router)
    wr_hi = wr.astype(BF16)
    wr_lo = (wr - wr_hi.astype(F32)).astype(BF16)
    br = jnp.full((1, LANES), -jnp.inf, F32).at[0, :n_exp].set(b_router)
    ltri = (jnp.arange(tm)[:, None] > jnp.arange(tm)[None, :]).astype(BF16)
    assert N % (MOE_GROUPS * tm) == 0
    ng = N // MOE_GROUPS
    steps = ng // tm
    n_assign = ng * TOP_K
    n_blocks = -(-(n_assign + n_exp * (EXPERT_BLOCK - 1)) // EXPERT_BLOCK)
    n_rows = n_blocks * EXPERT_BLOCK
    plane_spec = lambda rows, index: pl.BlockSpec((PACK_ROWS, rows, LANES), lambda i, *_: (0, index(i, *_), 0))
    row_spec = lambda w: pl.BlockSpec((tm, w), lambda i: (i, 0))
    w_attn_up_bf, w_out_bf, g2 = w_attn_up.astype(BF16), w_out.astype(BF16), norm2_g.reshape(1, D)
    bgu, bdn = b_gate_up.reshape(n_exp, 1, 2 * de), b_down.reshape(n_exp, 1, D)
    out = None
    for grp in range(MOE_GROUPS):
        grp_spec = lambda w, first=grp * steps: pl.BlockSpec((tm, w), lambda i: (first + i, 0))

        x1, hpk, ri, gate, cnt = pl.pallas_call(
            functools.partial(_mixer_out_kernel, tm=tm),
            grid=(steps,),
            in_specs=[grp_spec(D), grp_spec(D), grp_spec(D), grp_spec(sw), _const_spec((sw, D)), _const_spec((D, D)),
                      _const_spec((1, D)), _const_spec((D, LANES)), _const_spec((D, LANES)), _const_spec((1, LANES)),
                      _const_spec((tm, tm))],
            out_specs=[row_spec(D), plane_spec(tm, lambda i: i), pl.BlockSpec((2 * TOP_K, tm), lambda i: (0, i)),
                       row_spec(LANES), _const_spec((8, LANES))],
            out_shape=[jax.ShapeDtypeStruct((ng, D), F32), jax.ShapeDtypeStruct((PACK_ROWS, ng, LANES), U32),
                       jax.ShapeDtypeStruct((2 * TOP_K, ng), I32), jax.ShapeDtypeStruct((ng, LANES), F32),
                       jax.ShapeDtypeStruct((8, LANES), F32)],
            compiler_params=cparams(dimension_semantics=("arbitrary",)),
            name="mixer_out",
        )(xf, pg, ga, sba, w_attn_up_bf, w_out_bf, g2, wr_hi, wr_lo, br, ltri)

        counts = cnt[0, :n_exp].astype(I32)
        padded = (counts + EXPERT_BLOCK - 1) // EXPERT_BLOCK * EXPERT_BLOCK
        padded_end = jnp.cumsum(padded)
        start_pad = padded_end - padded
        idx = ri[:TOP_K]
        dest = ri[TOP_K:]
        for e in range(n_exp):
            dest = dest + jnp.where(idx == e, start_pad[e], 0)
        block_start = jnp.arange(n_blocks, dtype=I32) * EXPERT_BLOCK
        block_expert = jnp.minimum(jnp.sum((padded_end[None, :] <= block_start[:, None]).astype(I32), axis=1),
                                   n_exp - 1)
        n_used = (padded_end[-1] // EXPERT_BLOCK).astype(I32).reshape(1)

        dst_idx = dest.reshape(TOP_K, ng // SC_CHUNK, SC_CHUNK).transpose(1, 0, 2).reshape(-1, SC_CHUNK)
        xs = _sc_scatter_rows(hpk, dst_idx, n_rows)
        xs = pl.pallas_call(
            _padfill_kernel,
            grid_spec=pltpu.PrefetchScalarGridSpec(
                num_scalar_prefetch=2,
                grid=(1,),
                in_specs=[pl.BlockSpec(memory_space=pl.ANY)],
                out_specs=pl.BlockSpec(memory_space=pl.ANY),
                scratch_shapes=[pltpu.VMEM((PACK_ROWS, EXPERT_BLOCK // 2, LANES), U32), pltpu.SemaphoreType.DMA(())],
            ),
            out_shape=jax.ShapeDtypeStruct((PACK_ROWS, n_rows, LANES), U32),
            input_output_aliases={2: 0},
            compiler_params=cparams(dimension_semantics=("arbitrary",)),
            name="padfill",
        )(start_pad + counts, padded - counts, xs)

        ys = pl.pallas_call(
            _experts_kernel,
            grid_spec=pltpu.PrefetchScalarGridSpec(
                num_scalar_prefetch=2,
                grid=(n_blocks,),
                in_specs=[plane_spec(EXPERT_BLOCK, lambda i, be, nb: jnp.minimum(i, nb[0] - 1)),
                          pl.BlockSpec((1, D, 2 * de), lambda i, be, nb: (be[i], 0, 0)),
                          pl.BlockSpec((1, 1, 2 * de), lambda i, be, nb: (be[i], 0, 0)),
                          pl.BlockSpec((1, de, D), lambda i, be, nb: (be[i], 0, 0)),
                          pl.BlockSpec((1, 1, D), lambda i, be, nb: (be[i], 0, 0))],
                out_specs=plane_spec(EXPERT_BLOCK, lambda i, be, nb: i),
                scratch_shapes=[pltpu.VMEM((D, 2 * de), BF16), pltpu.VMEM((de, D), BF16)],
            ),
            out_shape=jax.ShapeDtypeStruct((PACK_ROWS, n_rows, LANES), U32),
            compiler_params=cparams(dimension_semantics=("arbitrary",)),
            name="experts",
        )(block_expert, n_used, xs, w_gate_up, bgu, w_down, bdn)

        y4 = _sc_gather_rows(ys, dest.reshape(-1, SC_CHUNK))

        operands = [x1, gate, y4, y4, y4, y4] + ([] if out is None else [out])
        out = pl.pallas_call(
            _combine_kernel,
            grid=(steps,),
            in_specs=([row_spec(D), row_spec(LANES)]
                      + [plane_spec(tm, lambda i, kk=kk: kk * steps + i) for kk in range(TOP_K)]
                      + ([] if out is None else [pl.BlockSpec(memory_space=pl.ANY)])),
            out_specs=grp_spec(D),
            out_shape=jax.ShapeDtypeStruct((N, D), F32),
            input_output_aliases={} if out is None else {len(operands) - 1: 0},
            compiler_params=cparams(dimension_semantics=("arbitrary",)),
            name="combine",
        )(*operands)
    return out.reshape(B, S, D)


def kernel(x, norm1_g, w_in, q_norm_g, k_norm_g, w_pool_grp, pool_scale, w_pool_up, w_attn_up, w_out, norm2_g,
           w_router, b_router, w_gate_up, b_gate_up, w_down, b_down):
    for layer in range(norm1_g.shape[0]):
        x = _layer(x, norm1_g[layer], w_in[layer], q_norm_g[layer], k_norm_g[layer], w_pool_grp[layer],
                   pool_scale[layer], w_pool_up[layer], w_attn_up[layer], w_out[layer], norm2_g[layer],
                   w_router[layer], b_router[layer], w_gate_up[layer], b_gate_up[layer], w_down[layer],
                   b_down[layer])
    return x
```

```python
import functools

import jax
import jax.numpy as jnp
from jax import lax
from jax.experimental import pallas as pl
from jax.experimental.pallas import tpu as pltpu
from jax.experimental.pallas import tpu_sc as plsc

F32 = jnp.float32
BF16 = jnp.bfloat16
U32 = jnp.uint32
I32 = jnp.int32

EPS = 1e-6
POOL_WINDOWS = (2, 4, 8, 16)
POOL_GROUP_DIM = 128
POOL_HALO = 16
SB_HEAD_DIM = 64
TOP_K = 4
SWIGLU_LIMIT = 7.0
SWIGLU_ALPHA = 1.702
EXPERT_BLOCK = 512
LANES = 128
PACK_ROWS = 4
ATTN_BLOCK = 128
ATTN_SUB = 4
ATTN_EXIT_BITS = 70.0
LOG2_E = 1.4426950408889634
VMEM_LIMIT = 56 * 1024 * 1024
SC_CHUNK = 128
MOE_GROUPS = 2


def _dot(a, b):
    return jnp.dot(a, b, preferred_element_type=F32)


def _split_bf16(x):
    hi = x.astype(BF16)
    lo = (x - hi.astype(F32)).astype(BF16)
    return hi, lo


def _pack_rows(v, out_ref):
    half = v.shape[1] // 2
    lo = lax.bitcast_convert_type(v[:, :half].astype(BF16).astype(F32), U32) >> 16
    hi = lax.bitcast_convert_type(v[:, half:].astype(BF16).astype(F32), U32) & jnp.uint32(0xFFFF0000)
    w = lo | hi
    for c in range(PACK_ROWS):
        out_ref[c] = w[:, c * LANES:(c + 1) * LANES]


def _unpack_rows(ref):
    los, his = [], []
    for c in range(PACK_ROWS):
        w = ref[c]
        los.append(lax.bitcast_convert_type(w << 16, F32))
        his.append(lax.bitcast_convert_type(w & jnp.uint32(0xFFFF0000), F32))
    return jnp.concatenate(los + his, axis=1)


def _mixer_in_kernel(x_ref, g1_ref, win_ref, gq_ref, gk_ref, hsum_ref, wgrp_ref, pscale_ref, wpu_ref,
                     q_ref, k_ref, v_ref, ga_ref, p_ref, tail_ref, *, tm, pw, sw):
    i = pl.program_id(1)

    @pl.when(i == 0)
    def _():
        tail_ref[...] = jnp.zeros_like(tail_ref)

    x = x_ref[...]
    ms = jnp.mean(x * x, axis=-1, keepdims=True)
    h = (x * lax.rsqrt(ms + EPS) * g1_ref[...]).astype(BF16)

    u = _dot(h, win_ref[:, 0:pw])
    xx = jnp.concatenate([tail_ref[...], u], axis=0)
    tail_ref[...] = u[tm - POOL_HALO:, :]
    pos = i * tm + lax.broadcasted_iota(I32, (tm, POOL_GROUP_DIM), 0)
    mixed = []
    for g, w in enumerate(POOL_WINDOWS):
        s = xx[:, g * POOL_GROUP_DIM:(g + 1) * POOL_GROUP_DIM]
        step = 1
        while step < w:
            s = s + pltpu.roll(s, step, axis=0)
            step *= 2
        count = jnp.minimum(pos + 1, w).astype(F32)
        ug = u[:, g * POOL_GROUP_DIM:(g + 1) * POOL_GROUP_DIM]
        d = s[POOL_HALO:, :] / count - ug
        mixed.append(_dot(d.astype(BF16), wgrp_ref[g]))
    pm = jnp.concatenate(mixed, axis=1) * pscale_ref[...]
    pool_out = _dot(pm.astype(BF16), wpu_ref[...])
    d_model = pool_out.shape[1]
    g_pool = _dot(h, win_ref[:, pw + 3 * sw:pw + 3 * sw + d_model])
    p_ref[...] = (jax.nn.sigmoid(g_pool) * pool_out).astype(BF16)
    g_attn = _dot(h, win_ref[:, pw + 3 * sw + d_model:pw + 3 * sw + 2 * d_model])
    ga_ref[...] = jax.nn.sigmoid(g_attn).astype(BF16)

    def head_norm(t, gain):
        ss = _dot((t * t).astype(BF16), hsum_ref[...])
        return t * lax.rsqrt(ss * (1.0 / SB_HEAD_DIM) + EPS) * gain

    q = _dot(h, win_ref[:, pw:pw + sw])
    q_ref[...] = (head_norm(q, gq_ref[...]) * (SB_HEAD_DIM ** -0.5 * LOG2_E)).astype(BF16)
    k = _dot(h, win_ref[:, pw + sw:pw + 2 * sw])
    k_ref[...] = head_norm(k, gk_ref[...]).astype(BF16)
    v_ref[...] = _dot(h, win_ref[:, pw + 2 * sw:pw + 3 * sw]).astype(BF16)


def _attn_kernel(q_ref, k_ref, v_ref, tri_ref, o_ref, *scratch, n_pairs):
    units = [(sub, p) for sub in range(ATTN_SUB) for p in range(n_pairs)]
    n_units = len(units)
    qs, acc, rr = scratch[:n_units], scratch[n_units:2 * n_units], scratch[2 * n_units:]
    bq = ATTN_BLOCK
    first_block = pl.program_id(1) * ATTN_SUB
    first_head = lax.broadcasted_iota(I32, (bq, LANES), 1) < SB_HEAD_DIM
    for u, (sub, p) in enumerate(units):
        q2 = q_ref[sub * bq:(sub + 1) * bq, p * LANES:(p + 1) * LANES]
        qs[u][:bq] = jnp.where(first_head, q2, jnp.zeros_like(q2))
        qs[u][bq:] = jnp.where(first_head, jnp.zeros_like(q2), q2)
    row = lax.broadcasted_iota(I32, (2 * bq, bq), 0)
    col = lax.broadcasted_iota(I32, (2 * bq, bq), 1)
    causal = col < (row & (bq - 1))
    contract_last = (((1,), (1,)), ((), ()))

    def softplus(z):
        return jnp.maximum(z, 0.0) + jnp.log2(1.0 + jnp.exp2(-jnp.abs(z)))

    def suffix_sums(sp):
        return _dot(sp.astype(BF16), tri_ref[...])

    def cols(ref, block, p):
        start = pl.multiple_of(jnp.maximum(block, 0) * bq, bq)
        return ref[pl.ds(start, bq), p * LANES:(p + 1) * LANES]

    def scores(u, block):
        return lax.dot_general(qs[u][...], cols(k_ref, block, units[u][1]), contract_last, preferred_element_type=F32)

    diag = [first_block + sub for sub, _ in units]
    z_d = [scores(u, diag[u]) for u in range(n_units)]
    z_n = [scores(u, diag[u] - 1) for u in range(n_units)]
    s_d = [suffix_sums(jnp.where(causal, softplus(z), 0.0)) for z in z_d]
    s_n = [suffix_sums(jnp.where(diag[u] >= 1, softplus(z_n[u]), 0.0)) for u in range(n_units)]
    for u, (_, p) in enumerate(units):
        a_d = jnp.where(causal, jnp.exp2(z_d[u] - s_d[u][:, :bq]), 0.0)
        r_d = s_d[u][:, bq:]
        a_n = jnp.where(diag[u] >= 1, jnp.exp2(z_n[u] - (r_d + s_n[u][:, :bq])), 0.0)
        acc[u][...] = (_dot(a_d.astype(BF16), cols(v_ref, diag[u], p)) + _dot(a_n.astype(BF16), cols(v_ref, diag[u] - 1, p)))
        rr[u][...] = r_d + s_n[u][:, bq:]

    def r_min():
        m = rr[0][...]
        for u in range(1, n_units):
            m = jnp.minimum(m, rr[u][...])
        return jnp.min(m)

    def cond(c):
        back, rm = c
        return jnp.logical_and(diag[-1] - back >= 0, rm < ATTN_EXIT_BITS)

    def body(c):
        back, _ = c
        blocks = [d - back for d in diag]
        zs = [scores(u, blocks[u]) for u in range(n_units)]
        ss = [suffix_sums(jnp.where(blocks[u] >= 0, softplus(zs[u]), 0.0)) for u in range(n_units)]
        for u, (_, p) in enumerate(units):
            r = rr[u][...]
            a = jnp.where(blocks[u] >= 0, jnp.exp2(zs[u] - (r + ss[u][:, :bq])), 0.0)
            acc[u][...] += _dot(a.astype(BF16), cols(v_ref, blocks[u], p))
            rr[u][...] = r + ss[u][:, bq:]
        return back + 1, r_min()

    lax.while_loop(cond, body, (2, r_min()))
    for u, (sub, p) in enumerate(units):
        o_ref[sub * bq:(sub + 1) * bq, p * LANES:(p + 1) * LANES] = (
            jnp.where(first_head, acc[u][:bq], acc[u][bq:]).astype(BF16))


def _mixer_out_kernel(x_ref, p_ref, ga_ref, sba_ref, wau_ref, wout_ref, g2_ref, wr_hi_ref, wr_lo_ref, br_ref,
                      ltri_ref, x1_ref, hp_ref, ri_ref, gate_ref, cnt_ref, *, tm):
    step = pl.program_id(0)

    @pl.when(step == 0)
    def _():
        cnt_ref[...] = jnp.zeros_like(cnt_ref)

    attn_out = _dot(sba_ref[...], wau_ref[...])
    merged = p_ref[...].astype(F32) + ga_ref[...].astype(F32) * attn_out
    x1 = x_ref[...] + _dot(merged.astype(BF16), wout_ref[...])
    x1_ref[...] = x1
    ms = jnp.mean(x1 * x1, axis=-1, keepdims=True)
    h2 = x1 * lax.rsqrt(ms + EPS) * g2_ref[...]
    _pack_rows(h2, hp_ref)

    h_hi, h_lo = _split_bf16(h2)
    logits = (_dot(h_hi, wr_hi_ref[...]) + _dot(h_hi, wr_lo_ref[...]) + _dot(h_lo, wr_hi_ref[...])
              + br_ref[...])
    lane = lax.broadcasted_iota(I32, logits.shape, 1).astype(F32)
    work = logits
    vals, idxs = [], []
    for _ in range(TOP_K):
        m = jnp.max(work, axis=-1, keepdims=True)
        ik = jnp.min(jnp.where(work == m, lane, float(LANES)), axis=-1, keepdims=True)
        vals.append(m)
        idxs.append(ik)
        work = jnp.where(lane == ik, -jnp.inf, work)
    es = [jnp.exp(v - vals[0]) for v in vals]
    denom = es[0] + es[1] + es[2] + es[3]
    hot = jnp.zeros(logits.shape, F32)
    for ik in idxs:
        hot = hot + (lane == ik).astype(F32)
    before = _dot(ltri_ref[...], hot.astype(BF16)) + cnt_ref[0:1, :]
    ri = jnp.zeros(logits.shape, F32)
    gt = jnp.zeros(logits.shape, F32)
    for kk in range(TOP_K):
        rank = jnp.sum(jnp.where(lane == idxs[kk], before, 0.0), axis=-1, keepdims=True)
        ri = jnp.where(lane == kk, idxs[kk], ri)
        ri = jnp.where(lane == TOP_K + kk, rank, ri)
        gt = jnp.where(lane == kk, es[kk] / denom, gt)
    ri_ref[...] = ri.T[:2 * TOP_K].astype(I32)
    gate_ref[...] = gt
    cnt_ref[...] = cnt_ref[...] + jnp.sum(hot, axis=0, keepdims=True)


def _sc_mesh():
    info = plsc.get_sparse_core_info()
    mesh = plsc.VectorSubcoreMesh(core_axis_name="c", subcore_axis_name="s")
    return mesh, info.num_cores, info.num_subcores, info.num_lanes


def _sc_scatter_rows(rows, idx, n_out):
    mesh, nc, ns, lanes = _sc_mesh()
    nw = nc * ns
    planes, m, width = rows.shape
    ch = SC_CHUNK
    n_ch = m // ch // nw
    half = planes // 2
    assert planes % 2 == 0 and n_ch * ch * nw == m and idx.shape == (m // ch * TOP_K, ch) and n_ch >= 2

    @functools.partial(
        pl.kernel, mesh=mesh, out_type=jax.ShapeDtypeStruct((planes * n_out, width), rows.dtype),
        scratch_types=([pltpu.VMEM((n_ch * TOP_K, ch), I32)] + [pltpu.VMEM((ch, width), rows.dtype)] * planes
                       + [pltpu.VMEM((TOP_K, ch), I32)] * planes
                       + [pltpu.SemaphoreType.DMA((planes,)), pltpu.SemaphoreType.DMA((planes,))]))
    def scatter_kernel(rows_hbm, idx_hbm, out_hbm, idx_v, *rest):
        bufs, ibufs, rsem, ssem = rest[:planes], rest[planes:2 * planes], rest[2 * planes], rest[2 * planes + 1]
        wid = lax.axis_index("s") * nc + lax.axis_index("c")
        pltpu.sync_copy(idx_hbm.at[pl.ds(wid * n_ch * TOP_K, n_ch * TOP_K)], idx_v)
        base = wid * n_ch * ch

        def read(j, b):
            return pltpu.make_async_copy(rows_hbm.at[b, pl.ds(pl.multiple_of(base + j * ch, ch), ch)], bufs[b], rsem.at[b])

        def scatters(b):
            return [pltpu.make_async_copy(bufs[b], out_hbm.at[ibufs[b].at[kk]], ssem.at[b]) for kk in range(TOP_K)]

        def start_scatters(j, b):
            for kk in range(TOP_K):
                for t in range(0, ch, lanes):
                    ibufs[b][kk, pl.ds(t, lanes)] = idx_v[j * TOP_K + kk, pl.ds(t, lanes)] + b * n_out
            for c in scatters(b):
                c.start()

        def wait_scatters(b):
            for c in scatters(b):
                c.wait()

        def finish(j, b):
            pj, pb = (j, b - half) if b >= half else (j - 1, b + half)
            read(pj, pb).wait()
            start_scatters(pj, pb)

        for b in range(planes):
            read(0, b).start()
        for b in range(half, planes):
            finish(0, b)

        @pl.loop(1, n_ch)
        def _(j):
            for b in range(planes):
                wait_scatters(b)
                read(j, b).start()
                finish(j, b)

        for b in range(half):
            finish(n_ch, b)
        for b in range(planes):
            wait_scatters(b)

    return scatter_kernel(rows, idx).reshape(planes, n_out, width)


def _padfill_kernel(start_ref, len_ref, xs_in_ref, xs_ref, zeros_ref, sem):
    del xs_in_ref
    zeros_ref[...] = jnp.zeros_like(zeros_ref)
    bits = [1 << b for b in reversed(range(EXPERT_BLOCK.bit_length() - 1))]

    def pieces(e):
        n = len_ref[e]
        for bit in bits:
            row0 = start_ref[e] + (n & ~(2 * bit - 1))
            copy = pltpu.make_async_copy(zeros_ref.at[:, pl.ds(0, bit)], xs_ref.at[:, pl.ds(row0, bit)], sem)
            yield (n & bit) != 0, copy

    def start(e, c):
        for on, copy in pieces(e):
            pl.when(on)(copy.start)
        return c

    def wait(e, c):
        for on, copy in pieces(e):
            pl.when(on)(copy.wait)
        return c

    lax.fori_loop(0, start_ref.shape[0], start, 0)
    lax.fori_loop(0, start_ref.shape[0], wait, 0)


def _experts_kernel(be_ref, nb_ref, xs_ref, wgu_ref, bgu_ref, wd_ref, bd_ref, ys_ref, wgu_bf_ref, wd_bf_ref):
    blk = pl.program_id(0)
    de = wd_ref.shape[1]

    @pl.when(jnp.logical_or(blk == 0, be_ref[blk] != be_ref[jnp.maximum(blk - 1, 0)]))
    def _():
        wgu_bf_ref[...] = wgu_ref[0].astype(BF16)
        wd_bf_ref[...] = wd_ref[0].astype(BF16)

    @pl.when(blk < nb_ref[0])
    def _():
        x = _unpack_rows(xs_ref).astype(BF16)
        gu = _dot(x, wgu_bf_ref[...]) + bgu_ref[0]
        glu = jnp.minimum(gu[:, :de], SWIGLU_LIMIT)
        lin = jnp.clip(gu[:, de:], -SWIGLU_LIMIT, SWIGLU_LIMIT)
        act = glu * jax.nn.sigmoid(SWIGLU_ALPHA * glu) * (lin + 1.0)
        y = _dot(act.astype(BF16), wd_bf_ref[...]) + bd_ref[0]
        _pack_rows(y, ys_ref)

    @pl.when(blk >= nb_ref[0])
    def _():
        ys_ref[...] = jnp.zeros_like(ys_ref)


def _sc_gather_rows(table, idx):
    mesh, nc, ns, lanes = _sc_mesh()
    nw = nc * ns
    planes, n_tab, width = table.shape
    n_idx_rows, ch = idx.shape
    n_ch = n_idx_rows // nw
    m = n_idx_rows * ch
    half = planes // 2
    assert planes % 2 == 0 and ch == SC_CHUNK and n_ch * nw == n_idx_rows and n_ch >= 2

    @functools.partial(
        pl.kernel, mesh=mesh, out_type=jax.ShapeDtypeStruct((planes, m, width), table.dtype),
        scratch_types=([pltpu.VMEM((n_ch, ch), I32)] + [pltpu.VMEM((ch, width), table.dtype)] * planes
                       + [pltpu.VMEM((8, ch), I32)] * planes
                       + [pltpu.SemaphoreType.DMA((planes,)), pltpu.SemaphoreType.DMA((planes,))]))
    def gather_kernel(table_hbm, idx_hbm, out_hbm, idx_v, *rest):
        bufs, ibufs, gsem, wsem = rest[:planes], rest[planes:2 * planes], rest[2 * planes], rest[2 * planes + 1]
        wid = lax.axis_index("s") * nc + lax.axis_index("c")
        pltpu.sync_copy(idx_hbm.at[pl.ds(wid * n_ch, n_ch)], idx_v)
        base = wid * n_ch * ch

        def gather(b):
            return pltpu.make_async_copy(table_hbm.at[ibufs[b].at[0]], bufs[b], gsem.at[b])

        def start_gather(j, b):
            for t in range(0, ch, lanes):
                ibufs[b][0, pl.ds(t, lanes)] = idx_v[j, pl.ds(t, lanes)] + b * n_tab
            gather(b).start()

        def write(j, b):
            return pltpu.make_async_copy(bufs[b], out_hbm.at[b, pl.ds(pl.multiple_of(base + j * ch, ch), ch)], wsem.at[b])

        def finish(j, b):
            pj, pb = (j, b - half) if b >= half else (j - 1, b + half)
            gather(pb).wait()
            write(pj, pb).start()

        for b in range(planes):
            start_gather(0, b)
        for b in range(half, planes):
            finish(0, b)

        @pl.loop(1, n_ch)
        def _(j):
            for b in range(planes):
                write(j - 1, b).wait()
                start_gather(j, b)
                finish(j, b)

        for b in range(half):
            finish(n_ch, b)
        for b in range(planes):
            write(n_ch - 1, b).wait()

    return gather_kernel(table.reshape(planes * n_tab, width), idx)


def _combine_kernel(x1_ref, gate_ref, y0_ref, y1_ref, y2_ref, y3_ref, *rest):
    o_ref = rest[-1]
    gate = gate_ref[...]
    out = x1_ref[...]
    for kk, y_ref in enumerate((y0_ref, y1_ref, y2_ref, y3_ref)):
        out = out + gate[:, kk:kk + 1] * _unpack_rows(y_ref)
    o_ref[...] = out


def _const_spec(shape):
    nd = len(shape)
    return pl.BlockSpec(shape, lambda *_: (0,) * nd)


def _layer(x, norm1_g, w_in, q_norm_g, k_norm_g, w_pool_grp, pool_scale, w_pool_up, w_attn_up, w_out, norm2_g,
           w_router, b_router, w_gate_up, b_gate_up, w_down, b_down):
    B, S, D = x.shape
    N = B * S
    pw = w_pool_up.shape[0]
    sw = w_attn_up.shape[0]
    n_exp = w_router.shape[1]
    de = w_down.shape[1]
    heads = sw // SB_HEAD_DIM
    assert pw == len(POOL_WINDOWS) * POOL_GROUP_DIM and heads % 2 == 0 and n_exp <= LANES
    assert D == 2 * PACK_ROWS * LANES and w_in.shape[1] == pw + 3 * sw + 2 * D
    tm = 512 if S % 512 == 0 else 256
    assert S % tm == 0 and S % (ATTN_SUB * ATTN_BLOCK) == 0
    xf = x.reshape(N, D)
    cparams = functools.partial(pltpu.CompilerParams, vmem_limit_bytes=VMEM_LIMIT)

    hsum = (jnp.arange(sw)[:, None] // SB_HEAD_DIM == jnp.arange(sw)[None, :] // SB_HEAD_DIM).astype(BF16)
    nt = S // tm
    tok_spec = lambda w: pl.BlockSpec((tm, w), lambda b, i: (b * nt + i, 0))
    q2, k2, v2, ga, pg = pl.pallas_call(
        functools.partial(_mixer_in_kernel, tm=tm, pw=pw, sw=sw),
        grid=(B, nt),
        in_specs=[tok_spec(D), _const_spec((1, D)), _const_spec((D, w_in.shape[1])), _const_spec((1, sw)),
                  _const_spec((1, sw)), _const_spec((sw, sw)),
                  _const_spec((len(POOL_WINDOWS), POOL_GROUP_DIM, POOL_GROUP_DIM)), _const_spec((1, pw)),
                  _const_spec((pw, D))],
        out_specs=[tok_spec(sw), tok_spec(sw), tok_spec(sw), tok_spec(D), tok_spec(D)],
        out_shape=[jax.ShapeDtypeStruct((N, sw), BF16)] * 3 + [jax.ShapeDtypeStruct((N, D), BF16)] * 2,
        scratch_shapes=[pltpu.VMEM((POOL_HALO, pw), F32)],
        compiler_params=cparams(dimension_semantics=("arbitrary", "arbitrary")),
        name="mixer_in",
    )(xf, norm1_g.reshape(1, D), w_in.astype(BF16), jnp.tile(q_norm_g, heads).reshape(1, sw),
      jnp.tile(k_norm_g, heads).reshape(1, sw), hsum, w_pool_grp.astype(BF16), pool_scale.reshape(1, pw),
      w_pool_up.astype(BF16))

    bq = ATTN_BLOCK
    rows_q = ATTN_SUB * bq
    nq = S // rows_q
    jj = jnp.arange(bq)
    tri = jnp.concatenate([(jj[:, None] >= jj[None, :]).astype(BF16), jnp.ones((bq, bq), BF16)], axis=1)
    n_pairs = heads // 2
    n_units = ATTN_SUB * n_pairs
    kv_spec = pl.BlockSpec((S, sw), lambda b, qi: (b, 0), pipeline_mode=pl.Buffered(1))
    sba = pl.pallas_call(
        functools.partial(_attn_kernel, n_pairs=n_pairs),
        grid=(B, nq),
        in_specs=[pl.BlockSpec((rows_q, sw), lambda b, qi: (b * nq + qi, 0)), kv_spec, kv_spec,
                  _const_spec((bq, 2 * bq))],
        out_specs=pl.BlockSpec((rows_q, sw), lambda b, qi: (b * nq + qi, 0)),
        out_shape=jax.ShapeDtypeStruct((N, sw), BF16),
        scratch_shapes=([pltpu.VMEM((2 * bq, LANES), BF16)] * n_units + [pltpu.VMEM((2 * bq, LANES), F32)] * (2 * n_units)),
        compiler_params=cparams(dimension_semantics=("arbitrary", "arbitrary")),
        name="sb_attn",
    )(q2, k2, v2, tri)

    wr = jnp.zeros((D, LANES), F32).at[:, :n_exp].set(w_router)
    wr_hi = wr.astype(BF16)
    wr_lo = (wr - wr_hi.astype(F32)).astype(BF16)
    br = jnp.full((1, LANES), -jnp.inf, F32).at[0, :n_exp].set(b_router)
    ltri = (jnp.arange(tm)[:, None] > jnp.arange(tm)[None, :]).astype(BF16)
    assert N % (MOE_GROUPS * tm) == 0
    ng = N // MOE_GROUPS
    steps = ng // tm
    n_assign = ng * TOP_K
    n_blocks = -(-(n_assign + n_exp * (EXPERT_BLOCK - 1)) // EXPERT_BLOCK)
    n_rows = n_blocks * EXPERT_BLOCK
    plane_spec = lambda rows, index: pl.BlockSpec((PACK_ROWS, rows, LANES), lambda i, *_: (0, index(i, *_), 0))
    row_spec = lambda w: pl.BlockSpec((tm, w), lambda i: (i, 0))
    w_attn_up_bf, w_out_bf, g2 = w_attn_up.astype(BF16), w_out.astype(BF16), norm2_g.reshape(1, D)
    bgu, bdn = b_gate_up.reshape(n_exp, 1, 2 * de), b_down.reshape(n_exp, 1, D)
    out = None
    for grp in range(MOE_GROUPS):
        grp_spec = lambda w, first=grp * steps: pl.BlockSpec((tm, w), lambda i: (first + i, 0))

        x1, hpk, ri, gate, cnt = pl.pallas_call(
            functools.partial(_mixer_out_kernel, tm=tm),
            grid=(steps,),
            in_specs=[grp_spec(D), grp_spec(D), grp_spec(D), grp_spec(sw), _const_spec((sw, D)), _const_spec((D, D)),
                      _const_spec((1, D)), _const_spec((D, LANES)), _const_spec((D, LANES)), _const_spec((1, LANES)),
                      _const_spec((tm, tm))],
            out_specs=[row_spec(D), plane_spec(tm, lambda i: i), pl.BlockSpec((2 * TOP_K, tm), lambda i: (0, i)),
                       row_spec(LANES), _const_spec((8, LANES))],
            out_shape=[jax.ShapeDtypeStruct((ng, D), F32), jax.ShapeDtypeStruct((PACK_ROWS, ng, LANES), U32),
                       jax.ShapeDtypeStruct((2 * TOP_K, ng), I32), jax.ShapeDtypeStruct((ng, LANES), F32),
                       jax.ShapeDtypeStruct((8, LANES), F32)],
            compiler_params=cparams(dimension_semantics=("arbitrary",)),
            name="mixer_out",
        )(xf, pg, ga, sba, w_attn_up_bf, w_out_bf, g2, wr_hi, wr_lo, br, ltri)

        counts = cnt[0, :n_exp].astype(I32)
        padded = (counts + EXPERT_BLOCK - 1) // EXPERT_BLOCK * EXPERT_BLOCK
        padded_end = jnp.cumsum(padded)
        start_pad = padded_end - padded
        idx = ri[:TOP_K]
        dest = ri[TOP_K:]
        for e in range(n_exp):
            dest = dest + jnp.where(idx == e, start_pad[e], 0)
        block_start = jnp.arange(n_blocks, dtype=I32) * EXPERT_BLOCK
        block_expert = jnp.minimum(jnp.sum((padded_end[None, :] <= block_start[:, None]).astype(I32), axis=1),
                                   n_exp - 1)
        n_used = (padded_end[-1] // EXPERT_BLOCK).astype(I32).reshape(1)

        dst_idx = dest.reshape(TOP_K, ng // SC_CHUNK, SC_CHUNK).transpose(1, 0, 2).reshape(-1, SC_CHUNK)
        xs = _sc_scatter_rows(hpk, dst_idx, n_rows)
        xs = pl.pallas_call(
            _padfill_kernel,
            grid_spec=pltpu.PrefetchScalarGridSpec(
                num_scalar_prefetch=2,
                grid=(1,),
                in_specs=[pl.BlockSpec(memory_space=pl.ANY)],
                out_specs=pl.BlockSpec(memory_space=pl.ANY),
                scratch_shapes=[pltpu.VMEM((PACK_ROWS, EXPERT_BLOCK // 2, LANES), U32), pltpu.SemaphoreType.DMA(())],
            ),
            out_shape=jax.ShapeDtypeStruct((PACK_ROWS, n_rows, LANES), U32),
            input_output_aliases={2: 0},
            compiler_params=cparams(dimension_semantics=("arbitrary",)),
            name="padfill",
        )(start_pad + counts, padded - counts, xs)

        ys = pl.pallas_call(
            _experts_kernel,
            grid_spec=pltpu.PrefetchScalarGridSpec(
                num_scalar_prefetch=2,
                grid=(n_blocks,),
                in_specs=[plane_spec(EXPERT_BLOCK, lambda i, be, nb: jnp.minimum(i, nb[0] - 1)),
                          pl.BlockSpec((1, D, 2 * de), lambda i, be, nb: (be[i], 0, 0)),
                          pl.BlockSpec((1, 1, 2 * de), lambda i, be, nb: (be[i], 0, 0)),
                          pl.BlockSpec((1, de, D), lambda i, be, nb: (be[i], 0, 0)),
                          pl.BlockSpec((1, 1, D), lambda i, be, nb: (be[i], 0, 0))],
                out_specs=plane_spec(EXPERT_BLOCK, lambda i, be, nb: i),
                scratch_shapes=[pltpu.VMEM((D, 2 * de), BF16), pltpu.VMEM((de, D), BF16)],
            ),
            out_shape=jax.ShapeDtypeStruct((PACK_ROWS, n_rows, LANES), U32),
            compiler_params=cparams(dimension_semantics=("arbitrary",)),
            name="experts",
        )(block_expert, n_used, xs, w_gate_up, bgu, w_down, bdn)

        y4 = _sc_gather_rows(ys, dest.reshape(-1, SC_CHUNK))

        operands = [x1, gate, y4, y4, y4, y4] + ([] if out is None else [out])
        out = pl.pallas_call(
            _combine_kernel,
            grid=(steps,),
            in_specs=([row_spec(D), row_spec(LANES)]
                      + [plane_spec(tm, lambda i, kk=kk: kk * steps + i) for kk in range(TOP_K)]
                      + ([] if out is None else [pl.BlockSpec(memory_space=pl.ANY)])),
            out_specs=grp_spec(D),
            out_shape=jax.ShapeDtypeStruct((N, D), F32),
            input_output_aliases={} if out is None else {len(operands) - 1: 0},
            compiler_params=cparams(dimension_semantics=("arbitrary",)),
            name="combine",
        )(*operands)
    return out.reshape(B, S, D)


def kernel(x, norm1_g, w_in, q_norm_g, k_norm_g, w_pool_grp, pool_scale, w_pool_up, w_attn_up, w_out, norm2_g,
           w_router, b_router, w_gate_up, b_gate_up, w_down, b_down):
    for layer in range(norm1_g.shape[0]):
        x = _layer(x, norm1_g[layer], w_in[layer], q_norm_g[layer], k_norm_g[layer], w_pool_grp[layer],
                   pool_scale[layer], w_pool_up[layer], w_attn_up[layer], w_out[layer], norm2_g[layer],
                   w_router[layer], b_router[layer], w_gate_up[layer], b_gate_up[layer], w_down[layer],
                   b_down[layer])
    return x
```

```python
import functools

import jax
import jax.numpy as jnp
from jax import lax
from jax.experimental import pallas as pl
from jax.experimental.pallas import tpu as pltpu
from jax.experimental.pallas import tpu_sc as plsc

F32 = jnp.float32
BF16 = jnp.bfloat16
U32 = jnp.uint32
I32 = jnp.int32

EPS = 1e-6
POOL_WINDOWS = (2, 4, 8, 16)
POOL_GROUP_DIM = 128
POOL_HALO = 16
SB_HEAD_DIM = 64
TOP_K = 4
SWIGLU_LIMIT = 7.0
SWIGLU_ALPHA = 1.702
EXPERT_BLOCK = 512
EXPERT_SPLIT = 4
LANES = 128
PACK_ROWS = 4
ATTN_BLOCK = 128
ATTN_SUB = 2
ATTN_EXIT_BITS = 70.0
LOG2_E = 1.4426950408889634
VMEM_LIMIT = 56 * 1024 * 1024
SC_CHUNK = 128
MOE_GROUPS = 2


def _dot(a, b):
    return jnp.dot(a, b, preferred_element_type=F32)


def _split_bf16(x):
    hi = x.astype(BF16)
    lo = (x - hi.astype(F32)).astype(BF16)
    return hi, lo


def _pack_rows(v, out_ref):
    half = v.shape[1] // 2
    lo = lax.bitcast_convert_type(v[:, :half].astype(BF16).astype(F32), U32) >> 16
    hi = lax.bitcast_convert_type(v[:, half:].astype(BF16).astype(F32), U32) & jnp.uint32(0xFFFF0000)
    w = lo | hi
    for c in range(PACK_ROWS):
        out_ref[c] = w[:, c * LANES:(c + 1) * LANES]


def _unpack_rows(ref):
    los, his = [], []
    for c in range(PACK_ROWS):
        w = ref[c]
        los.append(lax.bitcast_convert_type(w << 16, F32))
        his.append(lax.bitcast_convert_type(w & jnp.uint32(0xFFFF0000), F32))
    return jnp.concatenate(los + his, axis=1)


def _mixer_in_kernel(x_ref, g1_ref, win_ref, gq_ref, gk_ref, hsum_ref, wgrp_ref, pscale_ref, wpu_ref,
                     q_ref, k_ref, v_ref, ga_ref, p_ref, tail_ref, *, tm, pw, sw):
    i = pl.program_id(1)

    @pl.when(i == 0)
    def _():
        tail_ref[...] = jnp.zeros_like(tail_ref)

    x = x_ref[...]
    ms = jnp.mean(x * x, axis=-1, keepdims=True)
    h = (x * lax.rsqrt(ms + EPS) * g1_ref[...]).astype(BF16)

    u = _dot(h, win_ref[:, 0:pw])
    xx = jnp.concatenate([tail_ref[...], u], axis=0)
    tail_ref[...] = u[tm - POOL_HALO:, :]
    pos = i * tm + lax.broadcasted_iota(I32, (tm, POOL_GROUP_DIM), 0)
    mixed = []
    for g, w in enumerate(POOL_WINDOWS):
        s = xx[:, g * POOL_GROUP_DIM:(g + 1) * POOL_GROUP_DIM]
        step = 1
        while step < w:
            s = s + pltpu.roll(s, step, axis=0)
            step *= 2
        count = jnp.minimum(pos + 1, w).astype(F32)
        ug = u[:, g * POOL_GROUP_DIM:(g + 1) * POOL_GROUP_DIM]
        d = s[POOL_HALO:, :] / count - ug
        mixed.append(_dot(d.astype(BF16), wgrp_ref[g]))
    pm = jnp.concatenate(mixed, axis=1) * pscale_ref[...]
    pool_out = _dot(pm.astype(BF16), wpu_ref[...])
    d_model = pool_out.shape[1]
    g_pool = _dot(h, win_ref[:, pw + 3 * sw:pw + 3 * sw + d_model])
    p_ref[...] = (jax.nn.sigmoid(g_pool) * pool_out).astype(BF16)
    g_attn = _dot(h, win_ref[:, pw + 3 * sw + d_model:pw + 3 * sw + 2 * d_model])
    ga_ref[...] = jax.nn.sigmoid(g_attn).astype(BF16)

    def head_norm(t, gain):
        ss = _dot((t * t).astype(BF16), hsum_ref[...])
        return t * lax.rsqrt(ss * (1.0 / SB_HEAD_DIM) + EPS) * gain

    q = _dot(h, win_ref[:, pw:pw + sw])
    q_ref[...] = (head_norm(q, gq_ref[...]) * (SB_HEAD_DIM ** -0.5 * LOG2_E)).astype(BF16)
    k = _dot(h, win_ref[:, pw + sw:pw + 2 * sw])
    k_ref[...] = head_norm(k, gk_ref[...]).astype(BF16)
    v_ref[...] = _dot(h, win_ref[:, pw + 2 * sw:pw + 3 * sw]).astype(BF16)


def _attn_kernel(q_ref, k_ref, v_ref, tri_ref, o_ref, *scratch, n_pairs):
    units = [(sub, p) for sub in range(ATTN_SUB) for p in range(n_pairs)]
    n_units = len(units)
    qs, acc, rr = scratch[:n_units], scratch[n_units:2 * n_units], scratch[2 * n_units:]
    bq = ATTN_BLOCK
    first_block = pl.program_id(1) * ATTN_SUB
    first_head = lax.broadcasted_iota(I32, (bq, LANES), 1) < SB_HEAD_DIM
    for u, (sub, p) in enumerate(units):
        q2 = q_ref[sub * bq:(sub + 1) * bq, p * LANES:(p + 1) * LANES]
        qs[u][:bq] = jnp.where(first_head, q2, jnp.zeros_like(q2))
        qs[u][bq:] = jnp.where(first_head, jnp.zeros_like(q2), q2)
    row = lax.broadcasted_iota(I32, (2 * bq, bq), 0)
    col = lax.broadcasted_iota(I32, (2 * bq, bq), 1)
    causal = col < (row & (bq - 1))
    contract_last = (((1,), (1,)), ((), ()))

    def softplus(z):
        return jnp.maximum(z, 0.0) + jnp.log2(1.0 + jnp.exp2(-jnp.abs(z)))

    def suffix_sums(sp):
        return _dot(sp.astype(BF16), tri_ref[...])

    def cols(ref, block, p):
        start = pl.multiple_of(jnp.maximum(block, 0) * bq, bq)
        return ref[pl.ds(start, bq), p * LANES:(p + 1) * LANES]

    def scores(u, block):
        return lax.dot_general(qs[u][...], cols(k_ref, block, units[u][1]), contract_last, preferred_element_type=F32)

    diag = [first_block + sub for sub, _ in units]
    z_d = [scores(u, diag[u]) for u in range(n_units)]
    z_n = [scores(u, diag[u] - 1) for u in range(n_units)]
    s_d = [suffix_sums(jnp.where(causal, softplus(z), 0.0)) for z in z_d]
    s_n = [suffix_sums(jnp.where(diag[u] >= 1, softplus(z_n[u]), 0.0)) for u in range(n_units)]
    for u, (_, p) in enumerate(units):
        a_d = jnp.where(causal, jnp.exp2(z_d[u] - s_d[u][:, :bq]), 0.0)
        r_d = s_d[u][:, bq:]
        a_n = jnp.where(diag[u] >= 1, jnp.exp2(z_n[u] - (r_d + s_n[u][:, :bq])), 0.0)
        acc[u][...] = (_dot(a_d.astype(BF16), cols(v_ref, diag[u], p)) + _dot(a_n.astype(BF16), cols(v_ref, diag[u] - 1, p)))
        rr[u][...] = r_d + s_n[u][:, bq:]

    def r_min():
        m = rr[0][...]
        for u in range(1, n_units):
            m = jnp.minimum(m, rr[u][...])
        return jnp.min(m)

    def cond(c):
        back, rm = c
        return jnp.logical_and(diag[-1] - back >= 0, rm < ATTN_EXIT_BITS)

    def body(c):
        back, _ = c
        blocks = [d - back for d in diag]
        zs = [scores(u, blocks[u]) for u in range(n_units)]
        ss = [suffix_sums(jnp.where(blocks[u] >= 0, softplus(zs[u]), 0.0)) for u in range(n_units)]
        for u, (_, p) in enumerate(units):
            r = rr[u][...]
            a = jnp.where(blocks[u] >= 0, jnp.exp2(zs[u] - (r + ss[u][:, :bq])), 0.0)
            acc[u][...] += _dot(a.astype(BF16), cols(v_ref, blocks[u], p))
            rr[u][...] = r + ss[u][:, bq:]
        return back + 1, r_min()

    lax.while_loop(cond, body, (2, r_min()))
    for u, (sub, p) in enumerate(units):
        o_ref[sub * bq:(sub + 1) * bq, p * LANES:(p + 1) * LANES] = (
            jnp.where(first_head, acc[u][:bq], acc[u][bq:]).astype(BF16))


def _mixer_out_kernel(x_ref, p_ref, ga_ref, sba_ref, wau_ref, wout_ref, g2_ref, wr_hi_ref, wr_lo_ref, br_ref,
                      ltri_ref, x1_ref, hp_ref, ri_ref, gate_ref, cnt_ref, *, tm):
    step = pl.program_id(0)

    @pl.when(step == 0)
    def _():
        cnt_ref[...] = jnp.zeros_like(cnt_ref)

    attn_out = _dot(sba_ref[...], wau_ref[...])
    merged = p_ref[...].astype(F32) + ga_ref[...].astype(F32) * attn_out
    x1 = x_ref[...] + _dot(merged.astype(BF16), wout_ref[...])
    x1_ref[...] = x1
    ms = jnp.mean(x1 * x1, axis=-1, keepdims=True)
    h2 = x1 * lax.rsqrt(ms + EPS) * g2_ref[...]
    _pack_rows(h2, hp_ref)

    h_hi, h_lo = _split_bf16(h2)
    logits = (_dot(h_hi, wr_hi_ref[...]) + _dot(h_hi, wr_lo_ref[...]) + _dot(h_lo, wr_hi_ref[...])
              + br_ref[...])
    lane = lax.broadcasted_iota(I32, logits.shape, 1).astype(F32)
    work = logits
    vals, idxs = [], []
    for _ in range(TOP_K):
        m = jnp.max(work, axis=-1, keepdims=True)
        ik = jnp.min(jnp.where(work == m, lane, float(LANES)), axis=-1, keepdims=True)
        vals.append(m)
        idxs.append(ik)
        work = jnp.where(lane == ik, -jnp.inf, work)
    es = [jnp.exp(v - vals[0]) for v in vals]
    denom = es[0] + es[1] + es[2] + es[3]
    hot = jnp.zeros(logits.shape, F32)
    for ik in idxs:
        hot = hot + (lane == ik).astype(F32)
    before = _dot(ltri_ref[...], hot.astype(BF16)) + cnt_ref[0:1, :]
    ri = jnp.zeros(logits.shape, F32)
    gt = jnp.zeros(logits.shape, F32)
    for kk in range(TOP_K):
        rank = jnp.sum(jnp.where(lane == idxs[kk], before, 0.0), axis=-1, keepdims=True)
        ri = jnp.where(lane == kk, idxs[kk], ri)
        ri = jnp.where(lane == TOP_K + kk, rank, ri)
        gt = jnp.where(lane == kk, es[kk] / denom, gt)
    ri_ref[...] = ri.T[:2 * TOP_K].astype(I32)
    gate_ref[...] = gt
    cnt_ref[...] = cnt_ref[...] + jnp.sum(hot, axis=0, keepdims=True)


def _sc_mesh():
    info = plsc.get_sparse_core_info()
    mesh = plsc.VectorSubcoreMesh(core_axis_name="c", subcore_axis_name="s")
    return mesh, info.num_cores, info.num_subcores, info.num_lanes


def _sc_scatter_rows(rows, idx, n_out):
    mesh, nc, ns, lanes = _sc_mesh()
    nw = nc * ns
    planes, m, width = rows.shape
    ch = SC_CHUNK
    n_ch = m // ch // nw
    half = planes // 2
    assert planes % 2 == 0 and n_ch * ch * nw == m and idx.shape == (m // ch * TOP_K, ch) and n_ch >= 2

    @functools.partial(
        pl.kernel, mesh=mesh, out_type=jax.ShapeDtypeStruct((planes * n_out, width), rows.dtype),
        scratch_types=([pltpu.VMEM((n_ch * TOP_K, ch), I32)] + [pltpu.VMEM((ch, width), rows.dtype)] * planes
                       + [pltpu.VMEM((TOP_K, ch), I32)] * planes
                       + [pltpu.SemaphoreType.DMA((planes,)), pltpu.SemaphoreType.DMA((planes,))]))
    def scatter_kernel(rows_hbm, idx_hbm, out_hbm, idx_v, *rest):
        bufs, ibufs, rsem, ssem = rest[:planes], rest[planes:2 * planes], rest[2 * planes], rest[2 * planes + 1]
        wid = lax.axis_index("s") * nc + lax.axis_index("c")
        pltpu.sync_copy(idx_hbm.at[pl.ds(wid * n_ch * TOP_K, n_ch * TOP_K)], idx_v)
        base = wid * n_ch * ch

        def read(j, b):
            return pltpu.make_async_copy(rows_hbm.at[b, pl.ds(pl.multiple_of(base + j * ch, ch), ch)], bufs[b], rsem.at[b])

        def scatters(b):
            return [pltpu.make_async_copy(bufs[b], out_hbm.at[ibufs[b].at[kk]], ssem.at[b]) for kk in range(TOP_K)]

        def start_scatters(j, b):
            for kk in range(TOP_K):
                for t in range(0, ch, lanes):
                    ibufs[b][kk, pl.ds(t, lanes)] = idx_v[j * TOP_K + kk, pl.ds(t, lanes)] + b * n_out
            for c in scatters(b):
                c.start()

        def wait_scatters(b):
            for c in scatters(b):
                c.wait()

        def finish(j, b):
            pj, pb = (j, b - half) if b >= half else (j - 1, b + half)
            read(pj, pb).wait()
            start_scatters(pj, pb)

        for b in range(planes):
            read(0, b).start()
        for b in range(half, planes):
            finish(0, b)

        @pl.loop(1, n_ch)
        def _(j):
            for b in range(planes):
                wait_scatters(b)
                read(j, b).start()
                finish(j, b)

        for b in range(half):
            finish(n_ch, b)
        for b in range(planes):
            wait_scatters(b)

    return scatter_kernel(rows, idx).reshape(planes, n_out, width)


def _padfill_kernel(start_ref, len_ref, xs_in_ref, xs_ref, zeros_ref, sem):
    del xs_in_ref
    zeros_ref[...] = jnp.zeros_like(zeros_ref)
    bits = [1 << b for b in reversed(range(EXPERT_BLOCK.bit_length() - 1))]

    def pieces(e):
        n = len_ref[e]
        for bit in bits:
            row0 = start_ref[e] + (n & ~(2 * bit - 1))
            copy = pltpu.make_async_copy(zeros_ref.at[:, pl.ds(0, bit)], xs_ref.at[:, pl.ds(row0, bit)], sem)
            yield (n & bit) != 0, copy

    def start(e, c):
        for on, copy in pieces(e):
            pl.when(on)(copy.start)
        return c

    def wait(e, c):
        for on, copy in pieces(e):
            pl.when(on)(copy.wait)
        return c

    lax.fori_loop(0, start_ref.shape[0], start, 0)
    lax.fori_loop(0, start_ref.shape[0], wait, 0)


def _experts_kernel(be_ref, nb_ref, valid_ref, xs_ref, wgu_ref, bgu_ref, wd_ref, bd_ref, ys_ref, wgu_bf_ref, wd_bf_ref):
    del nb_ref
    blk = pl.program_id(0)
    de = wd_ref.shape[1]
    valid = valid_ref[blk]

    @pl.when(jnp.logical_and(valid > 0,
                             jnp.logical_or(blk == 0, be_ref[blk] != be_ref[jnp.maximum(blk - 1, 0)])))
    def _():
        wgu_bf_ref[...] = wgu_ref[0].astype(BF16)
        wd_bf_ref[...] = wd_ref[0].astype(BF16)

    def mlp(m):
        x = _unpack_rows(xs_ref.at[:, pl.ds(0, m)]).astype(BF16)
        gu = _dot(x, wgu_bf_ref[...]) + bgu_ref[0]
        glu = jnp.minimum(gu[:, :de], SWIGLU_LIMIT)
        lin = jnp.clip(gu[:, de:], -SWIGLU_LIMIT, SWIGLU_LIMIT)
        act = glu * jax.nn.sigmoid(SWIGLU_ALPHA * glu) * (lin + 1.0)
        y = _dot(act.astype(BF16), wd_bf_ref[...]) + bd_ref[0]
        _pack_rows(y, ys_ref.at[:, pl.ds(0, m)])
        if m < EXPERT_BLOCK:
            ys_ref[:, m:, :] = jnp.zeros((PACK_ROWS, EXPERT_BLOCK - m, LANES), U32)

    step = EXPERT_BLOCK // EXPERT_SPLIT
    for q in range(EXPERT_SPLIT):
        pl.when(jnp.logical_and(valid > q * step, valid <= (q + 1) * step))(functools.partial(mlp, (q + 1) * step))

    @pl.when(valid == 0)
    def _():
        ys_ref[...] = jnp.zeros_like(ys_ref)


def _sc_gather_rows(table, idx):
    mesh, nc, ns, lanes = _sc_mesh()
    nw = nc * ns
    planes, n_tab, width = table.shape
    n_idx_rows, ch = idx.shape
    n_ch = n_idx_rows // nw
    m = n_idx_rows * ch
    half = planes // 2
    assert planes % 2 == 0 and ch == SC_CHUNK and n_ch * nw == n_idx_rows and n_ch >= 2

    @functools.partial(
        pl.kernel, mesh=mesh, out_type=jax.ShapeDtypeStruct((planes, m, width), table.dtype),
        scratch_types=([pltpu.VMEM((n_ch, ch), I32)] + [pltpu.VMEM((ch, width), table.dtype)] * planes
                       + [pltpu.VMEM((8, ch), I32)] * planes
                       + [pltpu.SemaphoreType.DMA((planes,)), pltpu.SemaphoreType.DMA((planes,))]))
    def gather_kernel(table_hbm, idx_hbm, out_hbm, idx_v, *rest):
        bufs, ibufs, gsem, wsem = rest[:planes], rest[planes:2 * planes], rest[2 * planes], rest[2 * planes + 1]
        wid = lax.axis_index("s") * nc + lax.axis_index("c")
        pltpu.sync_copy(idx_hbm.at[pl.ds(wid * n_ch, n_ch)], idx_v)
        base = wid * n_ch * ch

        def gather(b):
            return pltpu.make_async_copy(table_hbm.at[ibufs[b].at[0]], bufs[b], gsem.at[b])

        def start_gather(j, b):
            for t in range(0, ch, lanes):
                ibufs[b][0, pl.ds(t, lanes)] = idx_v[j, pl.ds(t, lanes)] + b * n_tab
            gather(b).start()

        def write(j, b):
            return pltpu.make_async_copy(bufs[b], out_hbm.at[b, pl.ds(pl.multiple_of(base + j * ch, ch), ch)], wsem.at[b])

        def finish(j, b):
            pj, pb = (j, b - half) if b >= half else (j - 1, b + half)
            gather(pb).wait()
            write(pj, pb).start()

        for b in range(planes):
            start_gather(0, b)
        for b in range(half, planes):
            finish(0, b)

        @pl.loop(1, n_ch)
        def _(j):
            for b in range(planes):
                write(j - 1, b).wait()
                start_gather(j, b)
                finish(j, b)

        for b in range(half):
            finish(n_ch, b)
        for b in range(planes):
            write(n_ch - 1, b).wait()

    return gather_kernel(table.reshape(planes * n_tab, width), idx)


def _combine_kernel(x1_ref, gate_ref, y0_ref, y1_ref, y2_ref, y3_ref, *rest):
    o_ref = rest[-1]
    gate = gate_ref[...]
    out = x1_ref[...]
    for kk, y_ref in enumerate((y0_ref, y1_ref, y2_ref, y3_ref)):
        out = out + gate[:, kk:kk + 1] * _unpack_rows(y_ref)
    o_ref[...] = out


def _const_spec(shape):
    nd = len(shape)
    return pl.BlockSpec(shape, lambda *_: (0,) * nd)


def _layer(x, norm1_g, w_in, q_norm_g, k_norm_g, w_pool_grp, pool_scale, w_pool_up, w_attn_up, w_out, norm2_g,
           w_router, b_router, w_gate_up, b_gate_up, w_down, b_down):
    B, S, D = x.shape
    N = B * S
    pw = w_pool_up.shape[0]
    sw = w_attn_up.shape[0]
    n_exp = w_router.shape[1]
    de = w_down.shape[1]
    heads = sw // SB_HEAD_DIM
    assert pw == len(POOL_WINDOWS) * POOL_GROUP_DIM and heads % 2 == 0 and n_exp <= LANES
    assert D == 2 * PACK_ROWS * LANES and w_in.shape[1] == pw + 3 * sw + 2 * D
    tm = 512 if S % 512 == 0 else 256
    assert S % tm == 0 and S % (ATTN_SUB * ATTN_BLOCK) == 0
    xf = x.reshape(N, D)
    cparams = functools.partial(pltpu.CompilerParams, vmem_limit_bytes=VMEM_LIMIT)

    hsum = (jnp.arange(sw)[:, None] // SB_HEAD_DIM == jnp.arange(sw)[None, :] // SB_HEAD_DIM).astype(BF16)
    nt = S // tm
    tok_spec = lambda w: pl.BlockSpec((tm, w), lambda b, i: (b * nt + i, 0))
    q2, k2, v2, ga, pg = pl.pallas_call(
        functools.partial(_mixer_in_kernel, tm=tm, pw=pw, sw=sw),
        grid=(B, nt),
        in_specs=[tok_spec(D), _const_spec((1, D)), _const_spec((D, w_in.shape[1])), _const_spec((1, sw)),
                  _const_spec((1, sw)), _const_spec((sw, sw)),
                  _const_spec((len(POOL_WINDOWS), POOL_GROUP_DIM, POOL_GROUP_DIM)), _const_spec((1, pw)),
                  _const_spec((pw, D))],
        out_specs=[tok_spec(sw), tok_spec(sw), tok_spec(sw), tok_spec(D), tok_spec(D)],
        out_shape=[jax.ShapeDtypeStruct((N, sw), BF16)] * 3 + [jax.ShapeDtypeStruct((N, D), BF16)] * 2,
        scratch_shapes=[pltpu.VMEM((POOL_HALO, pw), F32)],
        compiler_params=cparams(dimension_semantics=("arbitrary", "arbitrary")),
        name="mixer_in",
    )(xf, norm1_g.reshape(1, D), w_in.astype(BF16), jnp.tile(q_norm_g, heads).reshape(1, sw),
      jnp.tile(k_norm_g, heads).reshape(1, sw), hsum, w_pool_grp.astype(BF16), pool_scale.reshape(1, pw),
      w_pool_up.astype(BF16))

    bq = ATTN_BLOCK
    rows_q = ATTN_SUB * bq
    nq = S // rows_q
    jj = jnp.arange(bq)
    tri = jnp.concatenate([(jj[:, None] >= jj[None, :]).astype(BF16), jnp.ones((bq, bq), BF16)], axis=1)
    n_pairs = heads // 2
    n_units = ATTN_SUB * n_pairs
    kv_spec = pl.BlockSpec((S, sw), lambda b, qi: (b, 0), pipeline_mode=pl.Buffered(1))
    sba = pl.pallas_call(
        functools.partial(_attn_kernel, n_pairs=n_pairs),
        grid=(B, nq),
        in_specs=[pl.BlockSpec((rows_q, sw), lambda b, qi: (b * nq + qi, 0)), kv_spec, kv_spec,
                  _const_spec((bq, 2 * bq))],
        out_specs=pl.BlockSpec((rows_q, sw), lambda b, qi: (b * nq + qi, 0)),
        out_shape=jax.ShapeDtypeStruct((N, sw), BF16),
        scratch_shapes=([pltpu.VMEM((2 * bq, LANES), BF16)] * n_units + [pltpu.VMEM((2 * bq, LANES), F32)] * (2 * n_units)),
        compiler_params=cparams(dimension_semantics=("arbitrary", "arbitrary")),
        name="sb_attn",
    )(q2, k2, v2, tri)

    wr = jnp.zeros((D, LANES), F32).at[:, :n_exp].set(w_router)
    wr_hi = wr.astype(BF16)
    wr_lo = (wr - wr_hi.astype(F32)).astype(BF16)
    br = jnp.full((1, LANES), -jnp.inf, F32).at[0, :n_exp].set(b_router)
    ltri = (jnp.arange(tm)[:, None] > jnp.arange(tm)[None, :]).astype(BF16)
    assert N % (MOE_GROUPS * tm) == 0
    ng = N // MOE_GROUPS
    steps = ng // tm
    n_assign = ng * TOP_K
    n_blocks = -(-(n_assign + n_exp * (EXPERT_BLOCK - 1)) // EXPERT_BLOCK)
    n_rows = n_blocks * EXPERT_BLOCK
    plane_spec = lambda rows, index: pl.BlockSpec((PACK_ROWS, rows, LANES), lambda i, *_: (0, index(i, *_), 0))
    row_spec = lambda w: pl.BlockSpec((tm, w), lambda i: (i, 0))
    w_attn_up_bf, w_out_bf, g2 = w_attn_up.astype(BF16), w_out.astype(BF16), norm2_g.reshape(1, D)
    bgu, bdn = b_gate_up.reshape(n_exp, 1, 2 * de), b_down.reshape(n_exp, 1, D)
    out = None
    for grp in range(MOE_GROUPS):
        grp_spec = lambda w, first=grp * steps: pl.BlockSpec((tm, w), lambda i: (first + i, 0))

        x1, hpk, ri, gate, cnt = pl.pallas_call(
            functools.partial(_mixer_out_kernel, tm=tm),
            grid=(steps,),
            in_specs=[grp_spec(D), grp_spec(D), grp_spec(D), grp_spec(sw), _const_spec((sw, D)), _const_spec((D, D)),
                      _const_spec((1, D)), _const_spec((D, LANES)), _const_spec((D, LANES)), _const_spec((1, LANES)),
                      _const_spec((tm, tm))],
            out_specs=[row_spec(D), plane_spec(tm, lambda i: i), pl.BlockSpec((2 * TOP_K, tm), lambda i: (0, i)),
                       row_spec(LANES), _const_spec((8, LANES))],
            out_shape=[jax.ShapeDtypeStruct((ng, D), F32), jax.ShapeDtypeStruct((PACK_ROWS, ng, LANES), U32),
                       jax.ShapeDtypeStruct((2 * TOP_K, ng), I32), jax.ShapeDtypeStruct((ng, LANES), F32),
                       jax.ShapeDtypeStruct((8, LANES), F32)],
            compiler_params=cparams(dimension_semantics=("arbitrary",)),
            name="mixer_out",
        )(xf, pg, ga, sba, w_attn_up_bf, w_out_bf, g2, wr_hi, wr_lo, br, ltri)

        counts = cnt[0, :n_exp].astype(I32)
        padded = (counts + EXPERT_BLOCK - 1) // EXPERT_BLOCK * EXPERT_BLOCK
        padded_end = jnp.cumsum(padded)
        start_pad = padded_end - padded
        idx = ri[:TOP_K]
        dest = ri[TOP_K:]
        for e in range(n_exp):
            dest = dest + jnp.where(idx == e, start_pad[e], 0)
        block_start = jnp.arange(n_blocks, dtype=I32) * EXPERT_BLOCK
        block_expert = jnp.minimum(jnp.sum((padded_end[None, :] <= block_start[:, None]).astype(I32), axis=1),
                                   n_exp - 1)
        n_used = (padded_end[-1] // EXPERT_BLOCK).astype(I32).reshape(1)

        dst_idx = dest.reshape(TOP_K, ng // SC_CHUNK, SC_CHUNK).transpose(1, 0, 2).reshape(-1, SC_CHUNK)
        xs = _sc_scatter_rows(hpk, dst_idx, n_rows)
        xs = pl.pallas_call(
            _padfill_kernel,
            grid_spec=pltpu.PrefetchScalarGridSpec(
                num_scalar_prefetch=2,
                grid=(1,),
                in_specs=[pl.BlockSpec(memory_space=pl.ANY)],
                out_specs=pl.BlockSpec(memory_space=pl.ANY),
                scratch_shapes=[pltpu.VMEM((PACK_ROWS, EXPERT_BLOCK // 2, LANES), U32), pltpu.SemaphoreType.DMA(())],
            ),
            out_shape=jax.ShapeDtypeStruct((PACK_ROWS, n_rows, LANES), U32),
            input_output_aliases={2: 0},
            compiler_params=cparams(dimension_semantics=("arbitrary",)),
            name="padfill",
        )(start_pad + counts, padded - counts, xs)

        block_valid = jnp.clip(jnp.sum(jnp.where(block_expert[:, None] == jnp.arange(n_exp, dtype=I32)[None, :],
                                                 (start_pad + counts)[None, :], 0), axis=1) - block_start,
                               0, EXPERT_BLOCK)
        w_spec = lambda shape: pl.BlockSpec(shape, lambda i, be, nb, bv: (be[i], 0, 0))
        ys = pl.pallas_call(
            _experts_kernel,
            grid_spec=pltpu.PrefetchScalarGridSpec(
                num_scalar_prefetch=3,
                grid=(n_blocks,),
                in_specs=[plane_spec(EXPERT_BLOCK, lambda i, be, nb, bv: jnp.minimum(i, nb[0] - 1)),
                          w_spec((1, D, 2 * de)), w_spec((1, 1, 2 * de)), w_spec((1, de, D)), w_spec((1, 1, D))],
                out_specs=plane_spec(EXPERT_BLOCK, lambda i, be, nb, bv: i),
                scratch_shapes=[pltpu.VMEM((D, 2 * de), BF16), pltpu.VMEM((de, D), BF16)],
            ),
            out_shape=jax.ShapeDtypeStruct((PACK_ROWS, n_rows, LANES), U32),
            compiler_params=cparams(dimension_semantics=("arbitrary",)),
            name="experts",
        )(block_expert, n_used, block_valid, xs, w_gate_up, bgu, w_down, bdn)

        y4 = _sc_gather_rows(ys, dest.reshape(-1, SC_CHUNK))

        operands = [x1, gate, y4, y4, y4, y4] + ([] if out is None else [out])
        out = pl.pallas_call(
            _combine_kernel,
            grid=(steps,),
            in_specs=([row_spec(D), row_spec(LANES)]
                      + [plane_spec(tm, lambda i, kk=kk: kk * steps + i) for kk in range(TOP_K)]
                      + ([] if out is None else [pl.BlockSpec(memory_space=pl.ANY)])),
            out_specs=grp_spec(D),
            out_shape=jax.ShapeDtypeStruct((N, D), F32),
            input_output_aliases={} if out is None else {len(operands) - 1: 0},
            compiler_params=cparams(dimension_semantics=("arbitrary",)),
            name="combine",
        )(*operands)
    return out.reshape(B, S, D)


def kernel(x, norm1_g, w_in, q_norm_g, k_norm_g, w_pool_grp, pool_scale, w_pool_up, w_attn_up, w_out, norm2_g,
           w_router, b_router, w_gate_up, b_gate_up, w_down, b_down):
    for layer in range(norm1_g.shape[0]):
        x = _layer(x, norm1_g[layer], w_in[layer], q_norm_g[layer], k_norm_g[layer], w_pool_grp[layer],
                   pool_scale[layer], w_pool_up[layer], w_attn_up[layer], w_out[layer], norm2_g[layer],
                   w_router[layer], b_router[layer], w_gate_up[layer], b_gate_up[layer], w_down[layer],
                   b_down[layer])
    return x
```

```python
import functools

import jax
import jax.numpy as jnp
from jax import lax
from jax.experimental import pallas as pl
from jax.experimental.pallas import tpu as pltpu
from jax.experimental.pallas import tpu_sc as plsc

F32 = jnp.float32
BF16 = jnp.bfloat16
U32 = jnp.uint32
I32 = jnp.int32

EPS = 1e-6
POOL_WINDOWS = (2, 4, 8, 16)
POOL_GROUP_DIM = 128
POOL_HALO = 16
SB_HEAD_DIM = 64
TOP_K = 4
SWIGLU_LIMIT = 7.0
SWIGLU_ALPHA = 1.702
EXPERT_BLOCK = 512
LANES = 128
PACK_ROWS = 4
ATTN_BLOCK = 128
ATTN_SUB = 2
ATTN_EXIT_BITS = 70.0
LOG2_E = 1.4426950408889634
VMEM_LIMIT = 56 * 1024 * 1024
SC_CHUNK = 128
MOE_GROUPS = 2


def _dot(a, b):
    return jnp.dot(a, b, preferred_element_type=F32)


def _split_bf16(x):
    hi = x.astype(BF16)
    lo = (x - hi.astype(F32)).astype(BF16)
    return hi, lo


def _pack_rows(v, out_ref):
    half = v.shape[1] // 2
    lo = lax.bitcast_convert_type(v[:, :half].astype(BF16).astype(F32), U32) >> 16
    hi = lax.bitcast_convert_type(v[:, half:].astype(BF16).astype(F32), U32) & jnp.uint32(0xFFFF0000)
    w = lo | hi
    for c in range(PACK_ROWS):
        out_ref[c] = w[:, c * LANES:(c + 1) * LANES]


def _unpack_rows(ref):
    los, his = [], []
    for c in range(PACK_ROWS):
        w = ref[c]
        los.append(lax.bitcast_convert_type(w << 16, F32))
        his.append(lax.bitcast_convert_type(w & jnp.uint32(0xFFFF0000), F32))
    return jnp.concatenate(los + his, axis=1)


def _mixer_in_kernel(x_ref, g1_ref, win_ref, gq_ref, gk_ref, hsum_ref, wgrp_ref, pscale_ref, wpu_ref,
                     q_ref, k_ref, v_ref, ga_ref, p_ref, tail_ref, *, tm, pw, sw):
    i = pl.program_id(1)

    @pl.when(i == 0)
    def _():
        tail_ref[...] = jnp.zeros_like(tail_ref)

    x = x_ref[...]
    ms = jnp.mean(x * x, axis=-1, keepdims=True)
    h = (x * lax.rsqrt(ms + EPS) * g1_ref[...]).astype(BF16)

    u = _dot(h, win_ref[:, 0:pw])
    xx = jnp.concatenate([tail_ref[...], u], axis=0)
    tail_ref[...] = u[tm - POOL_HALO:, :]
    pos = i * tm + lax.broadcasted_iota(I32, (tm, POOL_GROUP_DIM), 0)
    mixed = []
    for g, w in enumerate(POOL_WINDOWS):
        s = xx[:, g * POOL_GROUP_DIM:(g + 1) * POOL_GROUP_DIM]
        step = 1
        while step < w:
            s = s + pltpu.roll(s, step, axis=0)
            step *= 2
        count = jnp.minimum(pos + 1, w).astype(F32)
        ug = u[:, g * POOL_GROUP_DIM:(g + 1) * POOL_GROUP_DIM]
        d = s[POOL_HALO:, :] / count - ug
        mixed.append(_dot(d.astype(BF16), wgrp_ref[g]))
    pm = jnp.concatenate(mixed, axis=1) * pscale_ref[...]
    pool_out = _dot(pm.astype(BF16), wpu_ref[...])
    d_model = pool_out.shape[1]
    g_pool = _dot(h, win_ref[:, pw + 3 * sw:pw + 3 * sw + d_model])
    p_ref[...] = (jax.nn.sigmoid(g_pool) * pool_out).astype(BF16)
    g_attn = _dot(h, win_ref[:, pw + 3 * sw + d_model:pw + 3 * sw + 2 * d_model])
    ga_ref[...] = jax.nn.sigmoid(g_attn).astype(BF16)

    def head_norm(t, gain):
        ss = _dot((t * t).astype(BF16), hsum_ref[...])
        return t * lax.rsqrt(ss * (1.0 / SB_HEAD_DIM) + EPS) * gain

    q = _dot(h, win_ref[:, pw:pw + sw])
    q_ref[...] = (head_norm(q, gq_ref[...]) * (SB_HEAD_DIM ** -0.5 * LOG2_E)).astype(BF16)
    k = _dot(h, win_ref[:, pw + sw:pw + 2 * sw])
    k_ref[...] = head_norm(k, gk_ref[...]).astype(BF16)
    v_ref[...] = _dot(h, win_ref[:, pw + 2 * sw:pw + 3 * sw]).astype(BF16)


def _attn_kernel(q_ref, k_ref, v_ref, tri_ref, o_ref, *scratch, n_pairs):
    units = [(sub, p) for sub in range(ATTN_SUB) for p in range(n_pairs)]
    n_units = len(units)
    qs, acc, rr = scratch[:n_units], scratch[n_units:2 * n_units], scratch[2 * n_units:]
    bq = ATTN_BLOCK
    first_block = pl.program_id(1) * ATTN_SUB
    first_head = lax.broadcasted_iota(I32, (bq, LANES), 1) < SB_HEAD_DIM
    for u, (sub, p) in enumerate(units):
        q2 = q_ref[sub * bq:(sub + 1) * bq, p * LANES:(p + 1) * LANES]
        qs[u][:bq] = jnp.where(first_head, q2, jnp.zeros_like(q2))
        qs[u][bq:] = jnp.where(first_head, jnp.zeros_like(q2), q2)
    row = lax.broadcasted_iota(I32, (2 * bq, bq), 0)
    col = lax.broadcasted_iota(I32, (2 * bq, bq), 1)
    causal = col < (row & (bq - 1))
    contract_last = (((1,), (1,)), ((), ()))

    def softplus(z):
        return jnp.maximum(z, 0.0) + jnp.log2(1.0 + jnp.exp2(-jnp.abs(z)))

    def suffix_sums(sp):
        return _dot(sp.astype(BF16), tri_ref[...])

    def cols(ref, block, p):
        start = pl.multiple_of(jnp.maximum(block, 0) * bq, bq)
        return ref[pl.ds(start, bq), p * LANES:(p + 1) * LANES]

    def scores(u, block):
        return lax.dot_general(qs[u][...], cols(k_ref, block, units[u][1]), contract_last, preferred_element_type=F32)

    diag = [first_block + sub for sub, _ in units]
    z_d = [scores(u, diag[u]) for u in range(n_units)]
    z_n = [scores(u, diag[u] - 1) for u in range(n_units)]
    s_d = [suffix_sums(jnp.where(causal, softplus(z), 0.0)) for z in z_d]
    s_n = [suffix_sums(jnp.where(diag[u] >= 1, softplus(z_n[u]), 0.0)) for u in range(n_units)]
    for u, (_, p) in enumerate(units):
        a_d = jnp.where(causal, jnp.exp2(z_d[u] - s_d[u][:, :bq]), 0.0)
        r_d = s_d[u][:, bq:]
        a_n = jnp.where(diag[u] >= 1, jnp.exp2(z_n[u] - (r_d + s_n[u][:, :bq])), 0.0)
        acc[u][...] = (_dot(a_d.astype(BF16), cols(v_ref, diag[u], p)) + _dot(a_n.astype(BF16), cols(v_ref, diag[u] - 1, p)))
        rr[u][...] = r_d + s_n[u][:, bq:]

    def r_min():
        m = rr[0][...]
        for u in range(1, n_units):
            m = jnp.minimum(m, rr[u][...])
        return jnp.min(m)

    def cond(c):
        back, rm = c
        return jnp.logical_and(diag[-1] - back >= 0, rm < ATTN_EXIT_BITS)

    def body(c):
        back, _ = c
        blocks = [d - back for d in diag]
        zs = [scores(u, blocks[u]) for u in range(n_units)]
        ss = [suffix_sums(jnp.where(blocks[u] >= 0, softplus(zs[u]), 0.0)) for u in range(n_units)]
        for u, (_, p) in enumerate(units):
            r = rr[u][...]
            a = jnp.where(blocks[u] >= 0, jnp.exp2(zs[u] - (r + ss[u][:, :bq])), 0.0)
            acc[u][...] += _dot(a.astype(BF16), cols(v_ref, blocks[u], p))
            rr[u][...] = r + ss[u][:, bq:]
        return back + 1, r_min()

    lax.while_loop(cond, body, (2, r_min()))
    for u, (sub, p) in enumerate(units):
        o_ref[sub * bq:(sub + 1) * bq, p * LANES:(p + 1) * LANES] = (
            jnp.where(first_head, acc[u][:bq], acc[u][bq:]).astype(BF16))


def _mixer_out_kernel(x_ref, p_ref, ga_ref, sba_ref, wau_ref, wout_ref, g2_ref, wr_hi_ref, wr_lo_ref, br_ref,
                      ltri_ref, x1_ref, hp_ref, ri_ref, gate_ref, cnt_ref, logit_ref, *, tm):
    step = pl.program_id(0)

    @pl.when(step == 0)
    def _():
        cnt_ref[...] = jnp.zeros_like(cnt_ref)
        logit_ref[1] = jnp.zeros((tm, LANES), F32)

    logits = logit_ref[(step + 1) % 2]
    routed = jnp.where(step >= 1, 1.0, 0.0)
    lane = lax.broadcasted_iota(I32, logits.shape, 1).astype(F32)
    work = logits
    vals, idxs = [], []

    def topk_round():
        nonlocal work
        m = jnp.max(work, axis=-1, keepdims=True)
        ik = jnp.min(jnp.where(work == m, lane, float(LANES)), axis=-1, keepdims=True)
        vals.append(m)
        idxs.append(ik)
        work = jnp.where(lane == ik, -jnp.inf, work)

    d_model = x_ref.shape[1]
    quarter = d_model // TOP_K
    topk_round()
    attn_out = _dot(sba_ref[...], wau_ref[...])
    merged = (p_ref[...].astype(F32) + ga_ref[...].astype(F32) * attn_out).astype(BF16)
    x1_parts = []
    for c in range(TOP_K):
        if c >= 1:
            topk_round()
        cols = slice(c * quarter, (c + 1) * quarter)
        x1_parts.append(x_ref[:, cols] + _dot(merged, wout_ref[:, cols]))
    x1 = jnp.concatenate(x1_parts, axis=1)
    x1_ref[...] = x1

    es = [jnp.exp(v - vals[0]) for v in vals]
    denom = es[0] + es[1] + es[2] + es[3]
    hot = jnp.zeros(logits.shape, F32)
    for ik in idxs:
        hot = hot + jnp.where(lane == ik, routed, 0.0)
    before = _dot(ltri_ref[...], hot.astype(BF16)) + cnt_ref[0:1, :]

    ms = jnp.mean(x1 * x1, axis=-1, keepdims=True)
    h2 = x1 * lax.rsqrt(ms + EPS) * g2_ref[...]
    _pack_rows(h2, hp_ref)

    ri = jnp.zeros(logits.shape, F32)
    gt = jnp.zeros(logits.shape, F32)
    for kk in range(TOP_K):
        rank = jnp.sum(jnp.where(lane == idxs[kk], before, 0.0), axis=-1, keepdims=True)
        ri = jnp.where(lane == kk, idxs[kk], ri)
        ri = jnp.where(lane == TOP_K + kk, rank, ri)
        gt = jnp.where(lane == kk, es[kk] / denom, gt)
    ri_ref[...] = ri.T[:2 * TOP_K].astype(I32)
    gate_ref[...] = gt
    cnt_ref[...] = cnt_ref[...] + jnp.sum(hot, axis=0, keepdims=True)

    h_hi, h_lo = _split_bf16(h2)
    logit_ref[step % 2] = (_dot(h_hi, wr_hi_ref[...]) + _dot(h_hi, wr_lo_ref[...]) + _dot(h_lo, wr_hi_ref[...])
                           + br_ref[...])


def _sc_mesh():
    info = plsc.get_sparse_core_info()
    mesh = plsc.VectorSubcoreMesh(core_axis_name="c", subcore_axis_name="s")
    return mesh, info.num_cores, info.num_subcores, info.num_lanes


def _sc_scatter_rows(rows, idx, n_out):
    mesh, nc, ns, lanes = _sc_mesh()
    nw = nc * ns
    planes, m, width = rows.shape
    ch = SC_CHUNK
    n_ch = m // ch // nw
    half = planes // 2
    assert planes % 2 == 0 and n_ch * ch * nw == m and idx.shape == (m // ch * TOP_K, ch) and n_ch >= 2

    @functools.partial(
        pl.kernel, mesh=mesh, out_type=jax.ShapeDtypeStruct((planes * n_out, width), rows.dtype),
        scratch_types=([pltpu.VMEM((n_ch * TOP_K, ch), I32)] + [pltpu.VMEM((ch, width), rows.dtype)] * planes
                       + [pltpu.VMEM((TOP_K, ch), I32)] * planes
                       + [pltpu.SemaphoreType.DMA((planes,)), pltpu.SemaphoreType.DMA((planes,))]))
    def scatter_kernel(rows_hbm, idx_hbm, out_hbm, idx_v, *rest):
        bufs, ibufs, rsem, ssem = rest[:planes], rest[planes:2 * planes], rest[2 * planes], rest[2 * planes + 1]
        wid = lax.axis_index("s") * nc + lax.axis_index("c")
        pltpu.sync_copy(idx_hbm.at[pl.ds(wid * n_ch * TOP_K, n_ch * TOP_K)], idx_v)
        base = wid * n_ch * ch

        def read(j, b):
            return pltpu.make_async_copy(rows_hbm.at[b, pl.ds(pl.multiple_of(base + j * ch, ch), ch)], bufs[b], rsem.at[b])

        def scatters(b):
            return [pltpu.make_async_copy(bufs[b], out_hbm.at[ibufs[b].at[kk]], ssem.at[b]) for kk in range(TOP_K)]

        def start_scatters(j, b):
            for kk in range(TOP_K):
                for t in range(0, ch, lanes):
                    ibufs[b][kk, pl.ds(t, lanes)] = idx_v[j * TOP_K + kk, pl.ds(t, lanes)] + b * n_out
            for c in scatters(b):
                c.start()

        def wait_scatters(b):
            for c in scatters(b):
                c.wait()

        def finish(j, b):
            pj, pb = (j, b - half) if b >= half else (j - 1, b + half)
            read(pj, pb).wait()
            start_scatters(pj, pb)

        for b in range(planes):
            read(0, b).start()
        for b in range(half, planes):
            finish(0, b)

        @pl.loop(1, n_ch)
        def _(j):
            for b in range(planes):
                wait_scatters(b)
                read(j, b).start()
                finish(j, b)

        for b in range(half):
            finish(n_ch, b)
        for b in range(planes):
            wait_scatters(b)

    return scatter_kernel(rows, idx).reshape(planes, n_out, width)


def _padfill_kernel(start_ref, len_ref, xs_in_ref, xs_ref, zeros_ref, sem):
    del xs_in_ref
    zeros_ref[...] = jnp.zeros_like(zeros_ref)
    bits = [1 << b for b in reversed(range(EXPERT_BLOCK.bit_length() - 1))]

    def pieces(e):
        n = len_ref[e]
        for bit in bits:
            row0 = start_ref[e] + (n & ~(2 * bit - 1))
            copy = pltpu.make_async_copy(zeros_ref.at[:, pl.ds(0, bit)], xs_ref.at[:, pl.ds(row0, bit)], sem)
            yield (n & bit) != 0, copy

    def start(e, c):
        for on, copy in pieces(e):
            pl.when(on)(copy.start)
        return c

    def wait(e, c):
        for on, copy in pieces(e):
            pl.when(on)(copy.wait)
        return c

    lax.fori_loop(0, start_ref.shape[0], start, 0)
    lax.fori_loop(0, start_ref.shape[0], wait, 0)


def _experts_kernel(be_ref, nb_ref, xs_ref, wgu_ref, bgu_ref, wd_ref, bd_ref, ys_ref, wgu_bf_ref, wd_bf_ref):
    blk = pl.program_id(0)
    de = wd_ref.shape[1]

    @pl.when(jnp.logical_or(blk == 0, be_ref[blk] != be_ref[jnp.maximum(blk - 1, 0)]))
    def _():
        wgu_bf_ref[...] = wgu_ref[0].astype(BF16)
        wd_bf_ref[...] = wd_ref[0].astype(BF16)

    @pl.when(blk < nb_ref[0])
    def _():
        x = _unpack_rows(xs_ref).astype(BF16)
        gu = _dot(x, wgu_bf_ref[...]) + bgu_ref[0]
        glu = jnp.minimum(gu[:, :de], SWIGLU_LIMIT)
        lin = jnp.clip(gu[:, de:], -SWIGLU_LIMIT, SWIGLU_LIMIT)
        act = glu * jax.nn.sigmoid(SWIGLU_ALPHA * glu) * (lin + 1.0)
        y = _dot(act.astype(BF16), wd_bf_ref[...]) + bd_ref[0]
        _pack_rows(y, ys_ref)

    @pl.when(blk >= nb_ref[0])
    def _():
        ys_ref[...] = jnp.zeros_like(ys_ref)


def _sc_gather_rows(table, idx):
    mesh, nc, ns, lanes = _sc_mesh()
    nw = nc * ns
    planes, n_tab, width = table.shape
    n_idx_rows, ch = idx.shape
    n_ch = n_idx_rows // nw
    m = n_idx_rows * ch
    half = planes // 2
    assert planes % 2 == 0 and ch == SC_CHUNK and n_ch * nw == n_idx_rows and n_ch >= 2

    @functools.partial(
        pl.kernel, mesh=mesh, out_type=jax.ShapeDtypeStruct((planes, m, width), table.dtype),
        scratch_types=([pltpu.VMEM((n_ch, ch), I32)] + [pltpu.VMEM((ch, width), table.dtype)] * planes
                       + [pltpu.VMEM((8, ch), I32)] * planes
                       + [pltpu.SemaphoreType.DMA((planes,)), pltpu.SemaphoreType.DMA((planes,))]))
    def gather_kernel(table_hbm, idx_hbm, out_hbm, idx_v, *rest):
        bufs, ibufs, gsem, wsem = rest[:planes], rest[planes:2 * planes], rest[2 * planes], rest[2 * planes + 1]
        wid = lax.axis_index("s") * nc + lax.axis_index("c")
        pltpu.sync_copy(idx_hbm.at[pl.ds(wid * n_ch, n_ch)], idx_v)
        base = wid * n_ch * ch

        def gather(b):
            return pltpu.make_async_copy(table_hbm.at[ibufs[b].at[0]], bufs[b], gsem.at[b])

        def start_gather(j, b):
            for t in range(0, ch, lanes):
                ibufs[b][0, pl.ds(t, lanes)] = idx_v[j, pl.ds(t, lanes)] + b * n_tab
            gather(b).start()

        def write(j, b):
            return pltpu.make_async_copy(bufs[b], out_hbm.at[b, pl.ds(pl.multiple_of(base + j * ch, ch), ch)], wsem.at[b])

        def finish(j, b):
            pj, pb = (j, b - half) if b >= half else (j - 1, b + half)
            gather(pb).wait()
            write(pj, pb).start()

        for b in range(planes):
            start_gather(0, b)
        for b in range(half, planes):
            finish(0, b)

        @pl.loop(1, n_ch)
        def _(j):
            for b in range(planes):
                write(j - 1, b).wait()
                start_gather(j, b)
                finish(j, b)

        for b in range(half):
            finish(n_ch, b)
        for b in range(planes):
            write(n_ch - 1, b).wait()

    return gather_kernel(table.reshape(planes * n_tab, width), idx)


def _combine_kernel(x1_ref, gate_ref, y0_ref, y1_ref, y2_ref, y3_ref, *rest):
    o_ref = rest[-1]
    gate = gate_ref[...]
    out = x1_ref[...]
    for kk, y_ref in enumerate((y0_ref, y1_ref, y2_ref, y3_ref)):
        out = out + gate[:, kk:kk + 1] * _unpack_rows(y_ref)
    o_ref[...] = out


def _const_spec(shape):
    nd = len(shape)
    return pl.BlockSpec(shape, lambda *_: (0,) * nd)


def _layer(x, norm1_g, w_in, q_norm_g, k_norm_g, w_pool_grp, pool_scale, w_pool_up, w_attn_up, w_out, norm2_g,
           w_router, b_router, w_gate_up, b_gate_up, w_down, b_down):
    B, S, D = x.shape
    N = B * S
    pw = w_pool_up.shape[0]
    sw = w_attn_up.shape[0]
    n_exp = w_router.shape[1]
    de = w_down.shape[1]
    heads = sw // SB_HEAD_DIM
    assert pw == len(POOL_WINDOWS) * POOL_GROUP_DIM and heads % 2 == 0 and n_exp <= LANES
    assert D == 2 * PACK_ROWS * LANES and w_in.shape[1] == pw + 3 * sw + 2 * D
    tm = 512 if S % 512 == 0 else 256
    assert S % tm == 0 and S % (ATTN_SUB * ATTN_BLOCK) == 0
    xf = x.reshape(N, D)
    cparams = functools.partial(pltpu.CompilerParams, vmem_limit_bytes=VMEM_LIMIT)

    hsum = (jnp.arange(sw)[:, None] // SB_HEAD_DIM == jnp.arange(sw)[None, :] // SB_HEAD_DIM).astype(BF16)
    nt = S // tm
    tok_spec = lambda w: pl.BlockSpec((tm, w), lambda b, i: (b * nt + i, 0))
    q2, k2, v2, ga, pg = pl.pallas_call(
        functools.partial(_mixer_in_kernel, tm=tm, pw=pw, sw=sw),
        grid=(B, nt),
        in_specs=[tok_spec(D), _const_spec((1, D)), _const_spec((D, w_in.shape[1])), _const_spec((1, sw)),
                  _const_spec((1, sw)), _const_spec((sw, sw)),
                  _const_spec((len(POOL_WINDOWS), POOL_GROUP_DIM, POOL_GROUP_DIM)), _const_spec((1, pw)),
                  _const_spec((pw, D))],
        out_specs=[tok_spec(sw), tok_spec(sw), tok_spec(sw), tok_spec(D), tok_spec(D)],
        out_shape=[jax.ShapeDtypeStruct((N, sw), BF16)] * 3 + [jax.ShapeDtypeStruct((N, D), BF16)] * 2,
        scratch_shapes=[pltpu.VMEM((POOL_HALO, pw), F32)],
        compiler_params=cparams(dimension_semantics=("arbitrary", "arbitrary")),
        name="mixer_in",
    )(xf, norm1_g.reshape(1, D), w_in.astype(BF16), jnp.tile(q_norm_g, heads).reshape(1, sw),
      jnp.tile(k_norm_g, heads).reshape(1, sw), hsum, w_pool_grp.astype(BF16), pool_scale.reshape(1, pw),
      w_pool_up.astype(BF16))

    bq = ATTN_BLOCK
    rows_q = ATTN_SUB * bq
    nq = S // rows_q
    jj = jnp.arange(bq)
    tri = jnp.concatenate([(jj[:, None] >= jj[None, :]).astype(BF16), jnp.ones((bq, bq), BF16)], axis=1)
    n_pairs = heads // 2
    n_units = ATTN_SUB * n_pairs
    kv_spec = pl.BlockSpec((S, sw), lambda b, qi: (b, 0), pipeline_mode=pl.Buffered(1))
    sba = pl.pallas_call(
        functools.partial(_attn_kernel, n_pairs=n_pairs),
        grid=(B, nq),
        in_specs=[pl.BlockSpec((rows_q, sw), lambda b, qi: (b * nq + qi, 0)), kv_spec, kv_spec,
                  _const_spec((bq, 2 * bq))],
        out_specs=pl.BlockSpec((rows_q, sw), lambda b, qi: (b * nq + qi, 0)),
        out_shape=jax.ShapeDtypeStruct((N, sw), BF16),
        scratch_shapes=([pltpu.VMEM((2 * bq, LANES), BF16)] * n_units + [pltpu.VMEM((2 * bq, LANES), F32)] * (2 * n_units)),
        compiler_params=cparams(dimension_semantics=("arbitrary", "arbitrary")),
        name="sb_attn",
    )(q2, k2, v2, tri)

    wr = jnp.zeros((D, LANES), F32).at[:, :n_exp].set(w_router)
    wr_hi = wr.astype(BF16)
    wr_lo = (wr - wr_hi.astype(F32)).astype(BF16)
    br = jnp.full((1, LANES), -jnp.inf, F32).at[0, :n_exp].set(b_router)
    ltri = (jnp.arange(tm)[:, None] > jnp.arange(tm)[None, :]).astype(BF16)
    assert N % (MOE_GROUPS * tm) == 0
    ng = N // MOE_GROUPS
    steps = ng // tm
    n_assign = ng * TOP_K
    n_blocks = -(-(n_assign + n_exp * (EXPERT_BLOCK - 1)) // EXPERT_BLOCK)
    n_rows = n_blocks * EXPERT_BLOCK
    plane_spec = lambda rows, index: pl.BlockSpec((PACK_ROWS, rows, LANES), lambda i, *_: (0, index(i, *_), 0))
    row_spec = lambda w: pl.BlockSpec((tm, w), lambda i: (i, 0))
    w_attn_up_bf, w_out_bf, g2 = w_attn_up.astype(BF16), w_out.astype(BF16), norm2_g.reshape(1, D)
    bgu, bdn = b_gate_up.reshape(n_exp, 1, 2 * de), b_down.reshape(n_exp, 1, D)
    out = None
    for grp in range(MOE_GROUPS):
        grp_spec = lambda w, first=grp * steps: pl.BlockSpec((tm, w), lambda i: (first + i, 0))

        cur = lambda i: jnp.minimum(i, steps - 1)
        prev = lambda i: jnp.maximum(i - 1, 0)
        in_spec = lambda w, first=grp * steps: pl.BlockSpec((tm, w), lambda i: (first + cur(i), 0))
        x1, hpk, ri, gate, cnt = pl.pallas_call(
            functools.partial(_mixer_out_kernel, tm=tm),
            grid=(steps + 1,),
            in_specs=[in_spec(D), in_spec(D), in_spec(D), in_spec(sw), _const_spec((sw, D)), _const_spec((D, D)),
                      _const_spec((1, D)), _const_spec((D, LANES)), _const_spec((D, LANES)), _const_spec((1, LANES)),
                      _const_spec((tm, tm))],
            out_specs=[pl.BlockSpec((tm, D), lambda i: (cur(i), 0)), plane_spec(tm, cur),
                       pl.BlockSpec((2 * TOP_K, tm), lambda i: (0, prev(i))),
                       pl.BlockSpec((tm, LANES), lambda i: (prev(i), 0)), _const_spec((8, LANES))],
            out_shape=[jax.ShapeDtypeStruct((ng, D), F32), jax.ShapeDtypeStruct((PACK_ROWS, ng, LANES), U32),
                       jax.ShapeDtypeStruct((2 * TOP_K, ng), I32), jax.ShapeDtypeStruct((ng, LANES), F32),
                       jax.ShapeDtypeStruct((8, LANES), F32)],
            scratch_shapes=[pltpu.VMEM((2, tm, LANES), F32)],
            compiler_params=cparams(dimension_semantics=("arbitrary",)),
            name="mixer_out",
        )(xf, pg, ga, sba, w_attn_up_bf, w_out_bf, g2, wr_hi, wr_lo, br, ltri)

        counts = cnt[0, :n_exp].astype(I32)
        padded = (counts + EXPERT_BLOCK - 1) // EXPERT_BLOCK * EXPERT_BLOCK
        padded_end = jnp.cumsum(padded)
        start_pad = padded_end - padded
        idx = ri[:TOP_K]
        dest = ri[TOP_K:]
        for e in range(n_exp):
            dest = dest + jnp.where(idx == e, start_pad[e], 0)
        block_start = jnp.arange(n_blocks, dtype=I32) * EXPERT_BLOCK
        block_expert = jnp.minimum(jnp.sum((padded_end[None, :] <= block_start[:, None]).astype(I32), axis=1),
                                   n_exp - 1)
        n_used = (padded_end[-1] // EXPERT_BLOCK).astype(I32).reshape(1)

        dst_idx = dest.reshape(TOP_K, ng // SC_CHUNK, SC_CHUNK).transpose(1, 0, 2).reshape(-1, SC_CHUNK)
        xs = _sc_scatter_rows(hpk, dst_idx, n_rows)
        xs = pl.pallas_call(
            _padfill_kernel,
            grid_spec=pltpu.PrefetchScalarGridSpec(
                num_scalar_prefetch=2,
                grid=(1,),
                in_specs=[pl.BlockSpec(memory_space=pl.ANY)],
                out_specs=pl.BlockSpec(memory_space=pl.ANY),
                scratch_shapes=[pltpu.VMEM((PACK_ROWS, EXPERT_BLOCK // 2, LANES), U32), pltpu.SemaphoreType.DMA(())],
            ),
            out_shape=jax.ShapeDtypeStruct((PACK_ROWS, n_rows, LANES), U32),
            input_output_aliases={2: 0},
            compiler_params=cparams(dimension_semantics=("arbitrary",)),
            name="padfill",
        )(start_pad + counts, padded - counts, xs)

        ys = pl.pallas_call(
            _experts_kernel,
            grid_spec=pltpu.PrefetchScalarGridSpec(
                num_scalar_prefetch=2,
                grid=(n_blocks,),
                in_specs=[plane_spec(EXPERT_BLOCK, lambda i, be, nb: jnp.minimum(i, nb[0] - 1)),
                          pl.BlockSpec((1, D, 2 * de), lambda i, be, nb: (be[i], 0, 0)),
                          pl.BlockSpec((1, 1, 2 * de), lambda i, be, nb: (be[i], 0, 0)),
                          pl.BlockSpec((1, de, D), lambda i, be, nb: (be[i], 0, 0)),
                          pl.BlockSpec((1, 1, D), lambda i, be, nb: (be[i], 0, 0))],
                out_specs=plane_spec(EXPERT_BLOCK, lambda i, be, nb: i),
                scratch_shapes=[pltpu.VMEM((D, 2 * de), BF16), pltpu.VMEM((de, D), BF16)],
            ),
            out_shape=jax.ShapeDtypeStruct((PACK_ROWS, n_rows, LANES), U32),
            compiler_params=cparams(dimension_semantics=("arbitrary",)),
            name="experts",
        )(block_expert, n_used, xs, w_gate_up, bgu, w_down, bdn)

        y4 = _sc_gather_rows(ys, dest.reshape(-1, SC_CHUNK))

        operands = [x1, gate, y4, y4, y4, y4] + ([] if out is None else [out])
        out = pl.pallas_call(
            _combine_kernel,
            grid=(steps,),
            in_specs=([row_spec(D), row_spec(LANES)]
                      + [plane_spec(tm, lambda i, kk=kk: kk * steps + i) for kk in range(TOP_K)]
                      + ([] if out is None else [pl.BlockSpec(memory_space=pl.ANY)])),
            out_specs=grp_spec(D),
            out_shape=jax.ShapeDtypeStruct((N, D), F32),
            input_output_aliases={} if out is None else {len(operands) - 1: 0},
            compiler_params=cparams(dimension_semantics=("arbitrary",)),
            name="combine",
        )(*operands)
    return out.reshape(B, S, D)


def kernel(x, norm1_g, w_in, q_norm_g, k_norm_g, w_pool_grp, pool_scale, w_pool_up, w_attn_up, w_out, norm2_g,
           w_router, b_router, w_gate_up, b_gate_up, w_down, b_down):
    for layer in range(norm1_g.shape[0]):
        x = _layer(x, norm1_g[layer], w_in[layer], q_norm_g[layer], k_norm_g[layer], w_pool_grp[layer],
                   pool_scale[layer], w_pool_up[layer], w_attn_up[layer], w_out[layer], norm2_g[layer],
                   w_router[layer], b_router[layer], w_gate_up[layer], b_gate_up[layer], w_down[layer],
                   b_down[layer])
    return x
```

```python
import functools

import jax
import jax.numpy as jnp
from jax import lax
from jax.experimental import pallas as pl
from jax.experimental.pallas import tpu as pltpu
from jax.experimental.pallas import tpu_sc as plsc

F32 = jnp.float32
BF16 = jnp.bfloat16
U32 = jnp.uint32
I32 = jnp.int32

EPS = 1e-6
POOL_WINDOWS = (2, 4, 8, 16)
POOL_GROUP_DIM = 128
POOL_HALO = 16
SB_HEAD_DIM = 64
TOP_K = 4
SWIGLU_LIMIT = 7.0
SWIGLU_ALPHA = 1.702
EXPERT_BLOCK = 512
LANES = 128
PACK_ROWS = 4
ATTN_BLOCK = 128
ATTN_SUB = 2
ATTN_EXIT_BITS = 70.0
LOG2_E = 1.4426950408889634
VMEM_LIMIT = 56 * 1024 * 1024
SC_CHUNK = 128
MOE_GROUPS = 2


def _dot(a, b):
    return jnp.dot(a, b, preferred_element_type=F32)


def _split_bf16(x):
    hi = x.astype(BF16)
    lo = (x - hi.astype(F32)).astype(BF16)
    return hi, lo


def _pack_rows(v, out_ref):
    half = v.shape[1] // 2
    lo = lax.bitcast_convert_type(v[:, :half].astype(BF16).astype(F32), U32) >> 16
    hi = lax.bitcast_convert_type(v[:, half:].astype(BF16).astype(F32), U32) & jnp.uint32(0xFFFF0000)
    w = lo | hi
    for c in range(PACK_ROWS):
        out_ref[c] = w[:, c * LANES:(c + 1) * LANES]


def _unpack_rows(ref):
    los, his = [], []
    for c in range(PACK_ROWS):
        w = ref[c]
        los.append(lax.bitcast_convert_type(w << 16, F32))
        his.append(lax.bitcast_convert_type(w & jnp.uint32(0xFFFF0000), F32))
    return jnp.concatenate(los + his, axis=1)


def _mixer_in_kernel(x_ref, g1_ref, win_ref, gq_ref, gk_ref, hsum_ref, wgrp_ref, pscale_ref, wpu_ref,
                     q_ref, k_ref, v_ref, ga_ref, p_ref, tail_ref, *, tm, pw, sw):
    i = pl.program_id(1)

    @pl.when(i == 0)
    def _():
        tail_ref[...] = jnp.zeros_like(tail_ref)

    x = x_ref[...]
    ms = jnp.mean(x * x, axis=-1, keepdims=True)
    h = (x * lax.rsqrt(ms + EPS) * g1_ref[...]).astype(BF16)

    def head_norm(t, gain):
        ss = _dot((t * t).astype(BF16), hsum_ref[...])
        return t * lax.rsqrt(ss * (1.0 / SB_HEAD_DIM) + EPS) * gain

    def project_q():
        q = _dot(h, win_ref[:, pw:pw + sw])
        q_ref[...] = (head_norm(q, gq_ref[...]) * (SB_HEAD_DIM ** -0.5 * LOG2_E)).astype(BF16)

    def project_k():
        k = _dot(h, win_ref[:, pw + sw:pw + 2 * sw])
        k_ref[...] = head_norm(k, gk_ref[...]).astype(BF16)

    def project_v():
        v_ref[...] = _dot(h, win_ref[:, pw + 2 * sw:pw + 3 * sw]).astype(BF16)

    d_model = x.shape[1]

    def project_attn_gate():
        g_attn = _dot(h, win_ref[:, pw + 3 * sw + d_model:pw + 3 * sw + 2 * d_model])
        ga_ref[...] = jax.nn.sigmoid(g_attn).astype(BF16)

    u = _dot(h, win_ref[:, 0:pw])
    xx = jnp.concatenate([tail_ref[...], u], axis=0)
    tail_ref[...] = u[tm - POOL_HALO:, :]
    pos = i * tm + lax.broadcasted_iota(I32, (tm, POOL_GROUP_DIM), 0)
    mixed = []
    for (g, w), project in zip(enumerate(POOL_WINDOWS), (project_q, project_k, project_v, project_attn_gate)):
        project()
        s = xx[:, g * POOL_GROUP_DIM:(g + 1) * POOL_GROUP_DIM]
        step = 1
        while step < w:
            s = s + pltpu.roll(s, step, axis=0)
            step *= 2
        count = jnp.minimum(pos + 1, w).astype(F32)
        ug = u[:, g * POOL_GROUP_DIM:(g + 1) * POOL_GROUP_DIM]
        d = s[POOL_HALO:, :] / count - ug
        mixed.append(_dot(d.astype(BF16), wgrp_ref[g]))
    pm = jnp.concatenate(mixed, axis=1) * pscale_ref[...]
    pool_out = _dot(pm.astype(BF16), wpu_ref[...])
    g_pool = _dot(h, win_ref[:, pw + 3 * sw:pw + 3 * sw + d_model])
    p_ref[...] = (jax.nn.sigmoid(g_pool) * pool_out).astype(BF16)


def _attn_kernel(q_ref, k_ref, v_ref, tri_ref, o_ref, *scratch, n_pairs):
    units = [(sub, p) for sub in range(ATTN_SUB) for p in range(n_pairs)]
    n_units = len(units)
    qs, acc, rr = scratch[:n_units], scratch[n_units:2 * n_units], scratch[2 * n_units:]
    bq = ATTN_BLOCK
    first_block = pl.program_id(1) * ATTN_SUB
    first_head = lax.broadcasted_iota(I32, (bq, LANES), 1) < SB_HEAD_DIM
    for u, (sub, p) in enumerate(units):
        q2 = q_ref[sub * bq:(sub + 1) * bq, p * LANES:(p + 1) * LANES]
        qs[u][:bq] = jnp.where(first_head, q2, jnp.zeros_like(q2))
        qs[u][bq:] = jnp.where(first_head, jnp.zeros_like(q2), q2)
    row = lax.broadcasted_iota(I32, (2 * bq, bq), 0)
    col = lax.broadcasted_iota(I32, (2 * bq, bq), 1)
    causal = col < (row & (bq - 1))
    contract_last = (((1,), (1,)), ((), ()))

    def softplus(z):
        return jnp.maximum(z, 0.0) + jnp.log2(1.0 + jnp.exp2(-jnp.abs(z)))

    def suffix_sums(sp):
        return _dot(sp.astype(BF16), tri_ref[...])

    def cols(ref, block, p):
        start = pl.multiple_of(jnp.maximum(block, 0) * bq, bq)
        return ref[pl.ds(start, bq), p * LANES:(p + 1) * LANES]

    def scores(u, block):
        return lax.dot_general(qs[u][...], cols(k_ref, block, units[u][1]), contract_last, preferred_element_type=F32)

    diag = [first_block + sub for sub, _ in units]
    z_d, z_n, s_d, s_n = {}, {}, {}, {}

    def stage_scores(u):
        z_d[u] = scores(u, diag[u])
        z_n[u] = scores(u, diag[u] - 1)

    def stage_sums(u):
        s_d[u] = suffix_sums(jnp.where(causal, softplus(z_d[u]), 0.0))
        s_n[u] = suffix_sums(jnp.where(diag[u] >= 1, softplus(z_n[u]), 0.0))

    def stage_values(u):
        p = units[u][1]
        a_d = jnp.where(causal, jnp.exp2(z_d[u] - s_d[u][:, :bq]), 0.0)
        r_d = s_d[u][:, bq:]
        a_n = jnp.where(diag[u] >= 1, jnp.exp2(z_n[u] - (r_d + s_n[u][:, :bq])), 0.0)
        acc[u][...] = (_dot(a_d.astype(BF16), cols(v_ref, diag[u], p)) + _dot(a_n.astype(BF16), cols(v_ref, diag[u] - 1, p)))
        rr[u][...] = r_d + s_n[u][:, bq:]

    for t in range(n_units + 2):
        if t < n_units:
            stage_scores(t)
        if 0 <= t - 1 < n_units:
            stage_sums(t - 1)
        if 0 <= t - 2 < n_units:
            stage_values(t - 2)

    def r_min():
        m = rr[0][...]
        for u in range(1, n_units):
            m = jnp.minimum(m, rr[u][...])
        return jnp.min(m)

    def cond(c):
        back, rm = c
        return jnp.logical_and(diag[-1] - back >= 0, rm < ATTN_EXIT_BITS)

    def body(c):
        back, _ = c
        blocks = [d - back for d in diag]
        zs = [scores(u, blocks[u]) for u in range(n_units)]
        ss = [suffix_sums(jnp.where(blocks[u] >= 0, softplus(zs[u]), 0.0)) for u in range(n_units)]
        for u, (_, p) in enumerate(units):
            r = rr[u][...]
            a = jnp.where(blocks[u] >= 0, jnp.exp2(zs[u] - (r + ss[u][:, :bq])), 0.0)
            acc[u][...] += _dot(a.astype(BF16), cols(v_ref, blocks[u], p))
            rr[u][...] = r + ss[u][:, bq:]
        return back + 1, r_min()

    lax.while_loop(cond, body, (2, r_min()))
    for u, (sub, p) in enumerate(units):
        o_ref[sub * bq:(sub + 1) * bq, p * LANES:(p + 1) * LANES] = (
            jnp.where(first_head, acc[u][:bq], acc[u][bq:]).astype(BF16))


def _mixer_out_kernel(x_ref, p_ref, ga_ref, sba_ref, wau_ref, wout_ref, g2_ref, wr_hi_ref, wr_lo_ref, br_ref,
                      ltri_ref, x1_ref, hp_ref, ri_ref, gate_ref, cnt_ref, logit_ref, *, tm):
    step = pl.program_id(0)

    @pl.when(step == 0)
    def _():
        cnt_ref[...] = jnp.zeros_like(cnt_ref)
        logit_ref[1] = jnp.zeros((tm, LANES), F32)

    logits = logit_ref[(step + 1) % 2]
    routed = jnp.where(step >= 1, 1.0, 0.0)
    lane = lax.broadcasted_iota(I32, logits.shape, 1).astype(F32)
    work = logits
    vals, idxs = [], []

    def topk_round():
        nonlocal work
        m = jnp.max(work, axis=-1, keepdims=True)
        ik = jnp.min(jnp.where(work == m, lane, float(LANES)), axis=-1, keepdims=True)
        vals.append(m)
        idxs.append(ik)
        work = jnp.where(lane == ik, -jnp.inf, work)

    d_model = x_ref.shape[1]
    quarter = d_model // TOP_K
    topk_round()
    attn_out = _dot(sba_ref[...], wau_ref[...])
    merged = (p_ref[...].astype(F32) + ga_ref[...].astype(F32) * attn_out).astype(BF16)
    x1_parts = []
    for c in range(TOP_K):
        if c >= 1:
            topk_round()
        cols = slice(c * quarter, (c + 1) * quarter)
        x1_parts.append(x_ref[:, cols] + _dot(merged, wout_ref[:, cols]))
    x1 = jnp.concatenate(x1_parts, axis=1)
    x1_ref[...] = x1

    es = [jnp.exp(v - vals[0]) for v in vals]
    denom = es[0] + es[1] + es[2] + es[3]
    hot = jnp.zeros(logits.shape, F32)
    for ik in idxs:
        hot = hot + jnp.where(lane == ik, routed, 0.0)
    before = _dot(ltri_ref[...], hot.astype(BF16)) + cnt_ref[0:1, :]

    ms = jnp.mean(x1 * x1, axis=-1, keepdims=True)
    h2 = x1 * lax.rsqrt(ms + EPS) * g2_ref[...]
    _pack_rows(h2, hp_ref)

    ri = jnp.zeros(logits.shape, F32)
    gt = jnp.zeros(logits.shape, F32)
    for kk in range(TOP_K):
        rank = jnp.sum(jnp.where(lane == idxs[kk], before, 0.0), axis=-1, keepdims=True)
        ri = jnp.where(lane == kk, idxs[kk], ri)
        ri = jnp.where(lane == TOP_K + kk, rank, ri)
        gt = jnp.where(lane == kk, es[kk] / denom, gt)
    ri_ref[...] = ri.T[:2 * TOP_K].astype(I32)
    gate_ref[...] = gt
    cnt_ref[...] = cnt_ref[...] + jnp.sum(hot, axis=0, keepdims=True)

    h_hi, h_lo = _split_bf16(h2)
    logit_ref[step % 2] = (_dot(h_hi, wr_hi_ref[...]) + _dot(h_hi, wr_lo_ref[...]) + _dot(h_lo, wr_hi_ref[...])
                           + br_ref[...])


def _sc_mesh():
    info = plsc.get_sparse_core_info()
    mesh = plsc.VectorSubcoreMesh(core_axis_name="c", subcore_axis_name="s")
    return mesh, info.num_cores, info.num_subcores, info.num_lanes


def _sc_scatter_rows(rows, idx, n_out):
    mesh, nc, ns, lanes = _sc_mesh()
    nw = nc * ns
    planes, m, width = rows.shape
    ch = SC_CHUNK
    n_ch = m // ch // nw
    half = planes // 2
    assert planes % 2 == 0 and n_ch * ch * nw == m and idx.shape == (m // ch * TOP_K, ch) and n_ch >= 2

    @functools.partial(
        pl.kernel, mesh=mesh, out_type=jax.ShapeDtypeStruct((planes * n_out, width), rows.dtype),
        scratch_types=([pltpu.VMEM((n_ch * TOP_K, ch), I32)] + [pltpu.VMEM((ch, width), rows.dtype)] * planes
                       + [pltpu.VMEM((TOP_K, ch), I32)] * planes
                       + [pltpu.SemaphoreType.DMA((planes,)), pltpu.SemaphoreType.DMA((planes,))]))
    def scatter_kernel(rows_hbm, idx_hbm, out_hbm, idx_v, *rest):
        bufs, ibufs, rsem, ssem = rest[:planes], rest[planes:2 * planes], rest[2 * planes], rest[2 * planes + 1]
        wid = lax.axis_index("s") * nc + lax.axis_index("c")
        pltpu.sync_copy(idx_hbm.at[pl.ds(wid * n_ch * TOP_K, n_ch * TOP_K)], idx_v)
        base = wid * n_ch * ch

        def read(j, b):
            return pltpu.make_async_copy(rows_hbm.at[b, pl.ds(pl.multiple_of(base + j * ch, ch), ch)], bufs[b], rsem.at[b])

        def scatters(b):
            return [pltpu.make_async_copy(bufs[b], out_hbm.at[ibufs[b].at[kk]], ssem.at[b]) for kk in range(TOP_K)]

        def start_scatters(j, b):
            for kk in range(TOP_K):
                for t in range(0, ch, lanes):
                    ibufs[b][kk, pl.ds(t, lanes)] = idx_v[j * TOP_K + kk, pl.ds(t, lanes)] + b * n_out
            for c in scatters(b):
                c.start()

        def wait_scatters(b):
            for c in scatters(b):
                c.wait()

        def finish(j, b):
            pj, pb = (j, b - half) if b >= half else (j - 1, b + half)
            read(pj, pb).wait()
            start_scatters(pj, pb)

        for b in range(planes):
            read(0, b).start()
        for b in range(half, planes):
            finish(0, b)

        @pl.loop(1, n_ch)
        def _(j):
            for b in range(planes):
                wait_scatters(b)
                read(j, b).start()
                finish(j, b)

        for b in range(half):
            finish(n_ch, b)
        for b in range(planes):
            wait_scatters(b)

    return scatter_kernel(rows, idx).reshape(planes, n_out, width)


def _padfill_kernel(start_ref, len_ref, xs_in_ref, xs_ref, zeros_ref, sem):
    del xs_in_ref
    zeros_ref[...] = jnp.zeros_like(zeros_ref)
    bits = [1 << b for b in reversed(range(EXPERT_BLOCK.bit_length() - 1))]

    def pieces(e):
        n = len_ref[e]
        for bit in bits:
            row0 = start_ref[e] + (n & ~(2 * bit - 1))
            copy = pltpu.make_async_copy(zeros_ref.at[:, pl.ds(0, bit)], xs_ref.at[:, pl.ds(row0, bit)], sem)
            yield (n & bit) != 0, copy

    def start(e, c):
        for on, copy in pieces(e):
            pl.when(on)(copy.start)
        return c

    def wait(e, c):
        for on, copy in pieces(e):
            pl.when(on)(copy.wait)
        return c

    lax.fori_loop(0, start_ref.shape[0], start, 0)
    lax.fori_loop(0, start_ref.shape[0], wait, 0)


def _experts_kernel(be_ref, nb_ref, xs_ref, wgu_ref, bgu_ref, wd_ref, bd_ref, ys_ref, wgu_bf_ref, wd_bf_ref):
    blk = pl.program_id(0)
    de = wd_ref.shape[1]

    @pl.when(jnp.logical_or(blk == 0, be_ref[blk] != be_ref[jnp.maximum(blk - 1, 0)]))
    def _():
        wgu_bf_ref[...] = wgu_ref[0].astype(BF16)
        wd_bf_ref[...] = wd_ref[0].astype(BF16)

    @pl.when(blk < nb_ref[0])
    def _():
        x = _unpack_rows(xs_ref).astype(BF16)
        gu = _dot(x, wgu_bf_ref[...]) + bgu_ref[0]
        glu = jnp.minimum(gu[:, :de], SWIGLU_LIMIT)
        lin = jnp.clip(gu[:, de:], -SWIGLU_LIMIT, SWIGLU_LIMIT)
        act = glu * jax.nn.sigmoid(SWIGLU_ALPHA * glu) * (lin + 1.0)
        y = _dot(act.astype(BF16), wd_bf_ref[...]) + bd_ref[0]
        _pack_rows(y, ys_ref)

    @pl.when(blk >= nb_ref[0])
    def _():
        ys_ref[...] = jnp.zeros_like(ys_ref)


def _sc_gather_rows(table, idx):
    mesh, nc, ns, lanes = _sc_mesh()
    nw = nc * ns
    planes, n_tab, width = table.shape
    n_idx_rows, ch = idx.shape
    n_ch = n_idx_rows // nw
    m = n_idx_rows * ch
    half = planes // 2
    assert planes % 2 == 0 and ch == SC_CHUNK and n_ch * nw == n_idx_rows and n_ch >= 2

    @functools.partial(
        pl.kernel, mesh=mesh, out_type=jax.ShapeDtypeStruct((planes, m, width), table.dtype),
        scratch_types=([pltpu.VMEM((n_ch, ch), I32)] + [pltpu.VMEM((ch, width), table.dtype)] * planes
                       + [pltpu.VMEM((8, ch), I32)] * planes
                       + [pltpu.SemaphoreType.DMA((planes,)), pltpu.SemaphoreType.DMA((planes,))]))
    def gather_kernel(table_hbm, idx_hbm, out_hbm, idx_v, *rest):
        bufs, ibufs, gsem, wsem = rest[:planes], rest[planes:2 * planes], rest[2 * planes], rest[2 * planes + 1]
        wid = lax.axis_index("s") * nc + lax.axis_index("c")
        pltpu.sync_copy(idx_hbm.at[pl.ds(wid * n_ch, n_ch)], idx_v)
        base = wid * n_ch * ch

        def gather(b):
            return pltpu.make_async_copy(table_hbm.at[ibufs[b].at[0]], bufs[b], gsem.at[b])

        def start_gather(j, b):
            for t in range(0, ch, lanes):
                ibufs[b][0, pl.ds(t, lanes)] = idx_v[j, pl.ds(t, lanes)] + b * n_tab
            gather(b).start()

        def write(j, b):
            return pltpu.make_async_copy(bufs[b], out_hbm.at[b, pl.ds(pl.multiple_of(base + j * ch, ch), ch)], wsem.at[b])

        def finish(j, b):
            pj, pb = (j, b - half) if b >= half else (j - 1, b + half)
            gather(pb).wait()
            write(pj, pb).start()

        for b in range(planes):
            start_gather(0, b)
        for b in range(half, planes):
            finish(0, b)

        @pl.loop(1, n_ch)
        def _(j):
            for b in range(planes):
                write(j - 1, b).wait()
                start_gather(j, b)
                finish(j, b)

        for b in range(half):
            finish(n_ch, b)
        for b in range(planes):
            write(n_ch - 1, b).wait()

    return gather_kernel(table.reshape(planes * n_tab, width), idx)


def _combine_kernel(x1_ref, gate_ref, y0_ref, y1_ref, y2_ref, y3_ref, *rest):
    o_ref = rest[-1]
    gate = gate_ref[...]
    out = x1_ref[...]
    for kk, y_ref in enumerate((y0_ref, y1_ref, y2_ref, y3_ref)):
        out = out + gate[:, kk:kk + 1] * _unpack_rows(y_ref)
    o_ref[...] = out


def _const_spec(shape):
    nd = len(shape)
    return pl.BlockSpec(shape, lambda *_: (0,) * nd)


def _layer(x, norm1_g, w_in, q_norm_g, k_norm_g, w_pool_grp, pool_scale, w_pool_up, w_attn_up, w_out, norm2_g,
           w_router, b_router, w_gate_up, b_gate_up, w_down, b_down):
    B, S, D = x.shape
    N = B * S
    pw = w_pool_up.shape[0]
    sw = w_attn_up.shape[0]
    n_exp = w_router.shape[1]
    de = w_down.shape[1]
    heads = sw // SB_HEAD_DIM
    assert pw == len(POOL_WINDOWS) * POOL_GROUP_DIM and heads % 2 == 0 and n_exp <= LANES
    assert D == 2 * PACK_ROWS * LANES and w_in.shape[1] == pw + 3 * sw + 2 * D
    tm = 512 if S % 512 == 0 else 256
    assert S % tm == 0 and S % (ATTN_SUB * ATTN_BLOCK) == 0
    xf = x.reshape(N, D)
    cparams = functools.partial(pltpu.CompilerParams, vmem_limit_bytes=VMEM_LIMIT)

    hsum = (jnp.arange(sw)[:, None] // SB_HEAD_DIM == jnp.arange(sw)[None, :] // SB_HEAD_DIM).astype(BF16)
    nt = S // tm
    tok_spec = lambda w: pl.BlockSpec((tm, w), lambda b, i: (b * nt + i, 0))
    q2, k2, v2, ga, pg = pl.pallas_call(
        functools.partial(_mixer_in_kernel, tm=tm, pw=pw, sw=sw),
        grid=(B, nt),
        in_specs=[tok_spec(D), _const_spec((1, D)), _const_spec((D, w_in.shape[1])), _const_spec((1, sw)),
                  _const_spec((1, sw)), _const_spec((sw, sw)),
                  _const_spec((len(POOL_WINDOWS), POOL_GROUP_DIM, POOL_GROUP_DIM)), _const_spec((1, pw)),
                  _const_spec((pw, D))],
        out_specs=[tok_spec(sw), tok_spec(sw), tok_spec(sw), tok_spec(D), tok_spec(D)],
        out_shape=[jax.ShapeDtypeStruct((N, sw), BF16)] * 3 + [jax.ShapeDtypeStruct((N, D), BF16)] * 2,
        scratch_shapes=[pltpu.VMEM((POOL_HALO, pw), F32)],
        compiler_params=cparams(dimension_semantics=("arbitrary", "arbitrary")),
        name="mixer_in",
    )(xf, norm1_g.reshape(1, D), w_in.astype(BF16), jnp.tile(q_norm_g, heads).reshape(1, sw),
      jnp.tile(k_norm_g, heads).reshape(1, sw), hsum, w_pool_grp.astype(BF16), pool_scale.reshape(1, pw),
      w_pool_up.astype(BF16))

    bq = ATTN_BLOCK
    rows_q = ATTN_SUB * bq
    nq = S // rows_q
    jj = jnp.arange(bq)
    tri = jnp.concatenate([(jj[:, None] >= jj[None, :]).astype(BF16), jnp.ones((bq, bq), BF16)], axis=1)
    n_pairs = heads // 2
    n_units = ATTN_SUB * n_pairs
    kv_spec = pl.BlockSpec((S, sw), lambda b, qi: (b, 0), pipeline_mode=pl.Buffered(1))
    sba = pl.pallas_call(
        functools.partial(_attn_kernel, n_pairs=n_pairs),
        grid=(B, nq),
        in_specs=[pl.BlockSpec((rows_q, sw), lambda b, qi: (b * nq + qi, 0)), kv_spec, kv_spec,
                  _const_spec((bq, 2 * bq))],
        out_specs=pl.BlockSpec((rows_q, sw), lambda b, qi: (b * nq + qi, 0)),
        out_shape=jax.ShapeDtypeStruct((N, sw), BF16),
        scratch_shapes=([pltpu.VMEM((2 * bq, LANES), BF16)] * n_units + [pltpu.VMEM((2 * bq, LANES), F32)] * (2 * n_units)),
        compiler_params=cparams(dimension_semantics=("arbitrary", "arbitrary")),
        name="sb_attn",
    )(q2, k2, v2, tri)

    wr = jnp.zeros((D, LANES), F32).at[:, :n_exp].set(w_router)
    wr_hi = wr.astype(BF16)
    wr_lo = (wr - wr_hi.astype(F32)).astype(BF16)
    br = jnp.full((1, LANES), -jnp.inf, F32).at[0, :n_exp].set(b_router)
    ltri = (jnp.arange(tm)[:, None] > jnp.arange(tm)[None, :]).astype(BF16)
    assert N % (MOE_GROUPS * tm) == 0
    ng = N // MOE_GROUPS
    steps = ng // tm
    n_assign = ng * TOP_K
    n_blocks = -(-(n_assign + n_exp * (EXPERT_BLOCK - 1)) // EXPERT_BLOCK)
    n_rows = n_blocks * EXPERT_BLOCK
    plane_spec = lambda rows, index: pl.BlockSpec((PACK_ROWS, rows, LANES), lambda i, *_: (0, index(i, *_), 0))
    row_spec = lambda w: pl.BlockSpec((tm, w), lambda i: (i, 0))
    w_attn_up_bf, w_out_bf, g2 = w_attn_up.astype(BF16), w_out.astype(BF16), norm2_g.reshape(1, D)
    bgu, bdn = b_gate_up.reshape(n_exp, 1, 2 * de), b_down.reshape(n_exp, 1, D)
    out = None
    for grp in range(MOE_GROUPS):
        grp_spec = lambda w, first=grp * steps: pl.BlockSpec((tm, w), lambda i: (first + i, 0))

        cur = lambda i: jnp.minimum(i, steps - 1)
        prev = lambda i: jnp.maximum(i - 1, 0)
        in_spec = lambda w, first=grp * steps: pl.BlockSpec((tm, w), lambda i: (first + cur(i), 0))
        x1, hpk, ri, gate, cnt = pl.pallas_call(
            functools.partial(_mixer_out_kernel, tm=tm),
            grid=(steps + 1,),
            in_specs=[in_spec(D), in_spec(D), in_spec(D), in_spec(sw), _const_spec((sw, D)), _const_spec((D, D)),
                      _const_spec((1, D)), _const_spec((D, LANES)), _const_spec((D, LANES)), _const_spec((1, LANES)),
                      _const_spec((tm, tm))],
            out_specs=[pl.BlockSpec((tm, D), lambda i: (cur(i), 0)), plane_spec(tm, cur),
                       pl.BlockSpec((2 * TOP_K, tm), lambda i: (0, prev(i))),
                       pl.BlockSpec((tm, LANES), lambda i: (prev(i), 0)), _const_spec((8, LANES))],
            out_shape=[jax.ShapeDtypeStruct((ng, D), F32), jax.ShapeDtypeStruct((PACK_ROWS, ng, LANES), U32),
                       jax.ShapeDtypeStruct((2 * TOP_K, ng), I32), jax.ShapeDtypeStruct((ng, LANES), F32),
                       jax.ShapeDtypeStruct((8, LANES), F32)],
            scratch_shapes=[pltpu.VMEM((2, tm, LANES), F32)],
            compiler_params=cparams(dimension_semantics=("arbitrary",)),
            name="mixer_out",
        )(xf, pg, ga, sba, w_attn_up_bf, w_out_bf, g2, wr_hi, wr_lo, br, ltri)

        counts = cnt[0, :n_exp].astype(I32)
        padded = (counts + EXPERT_BLOCK - 1) // EXPERT_BLOCK * EXPERT_BLOCK
        padded_end = jnp.cumsum(padded)
        start_pad = padded_end - padded
        idx = ri[:TOP_K]
        dest = ri[TOP_K:]
        for e in range(n_exp):
            dest = dest + jnp.where(idx == e, start_pad[e], 0)
        block_start = jnp.arange(n_blocks, dtype=I32) * EXPERT_BLOCK
        block_expert = jnp.minimum(jnp.sum((padded_end[None, :] <= block_start[:, None]).astype(I32), axis=1),
                                   n_exp - 1)
        n_used = (padded_end[-1] // EXPERT_BLOCK).astype(I32).reshape(1)

        dst_idx = dest.reshape(TOP_K, ng // SC_CHUNK, SC_CHUNK).transpose(1, 0, 2).reshape(-1, SC_CHUNK)
        xs = _sc_scatter_rows(hpk, dst_idx, n_rows)
        xs = pl.pallas_call(
            _padfill_kernel,
            grid_spec=pltpu.PrefetchScalarGridSpec(
                num_scalar_prefetch=2,
                grid=(1,),
                in_specs=[pl.BlockSpec(memory_space=pl.ANY)],
                out_specs=pl.BlockSpec(memory_space=pl.ANY),
                scratch_shapes=[pltpu.VMEM((PACK_ROWS, EXPERT_BLOCK // 2, LANES), U32), pltpu.SemaphoreType.DMA(())],
            ),
            out_shape=jax.ShapeDtypeStruct((PACK_ROWS, n_rows, LANES), U32),
            input_output_aliases={2: 0},
            compiler_params=cparams(dimension_semantics=("arbitrary",)),
            name="padfill",
        )(start_pad + counts, padded - counts, xs)

        ys = pl.pallas_call(
            _experts_kernel,
            grid_spec=pltpu.PrefetchScalarGridSpec(
                num_scalar_prefetch=2,
                grid=(n_blocks,),
                in_specs=[plane_spec(EXPERT_BLOCK, lambda i, be, nb: jnp.minimum(i, nb[0] - 1)),
                          pl.BlockSpec((1, D, 2 * de), lambda i, be, nb: (be[i], 0, 0)),
                          pl.BlockSpec((1, 1, 2 * de), lambda i, be, nb: (be[i], 0, 0)),
                          pl.BlockSpec((1, de, D), lambda i, be, nb: (be[i], 0, 0)),
                          pl.BlockSpec((1, 1, D), lambda i, be, nb: (be[i], 0, 0))],
                out_specs=plane_spec(EXPERT_BLOCK, lambda i, be, nb: i),
                scratch_shapes=[pltpu.VMEM((D, 2 * de), BF16), pltpu.VMEM((de, D), BF16)],
            ),
            out_shape=jax.ShapeDtypeStruct((PACK_ROWS, n_rows, LANES), U32),
            compiler_params=cparams(dimension_semantics=("arbitrary",)),
            name="experts",
        )(block_expert, n_used, xs, w_gate_up, bgu, w_down, bdn)

        y4 = _sc_gather_rows(ys, dest.reshape(-1, SC_CHUNK))

        operands = [x1, gate, y4, y4, y4, y4] + ([] if out is None else [out])
        out = pl.pallas_call(
            _combine_kernel,
            grid=(steps,),
            in_specs=([row_spec(D), row_spec(LANES)]
                      + [plane_spec(tm, lambda i, kk=kk: kk * steps + i) for kk in range(TOP_K)]
                      + ([] if out is None else [pl.BlockSpec(memory_space=pl.ANY)])),
            out_specs=grp_spec(D),
            out_shape=jax.ShapeDtypeStruct((N, D), F32),
            input_output_aliases={} if out is None else {len(operands) - 1: 0},
            compiler_params=cparams(dimension_semantics=("arbitrary",)),
            name="combine",
        )(*operands)
    return out.reshape(B, S, D)


def kernel(x, norm1_g, w_in, q_norm_g, k_norm_g, w_pool_grp, pool_scale, w_pool_up, w_attn_up, w_out, norm2_g,
           w_router, b_router, w_gate_up, b_gate_up, w_down, b_down):
    for layer in range(norm1_g.shape[0]):
        x = _layer(x, norm1_g[layer], w_in[layer], q_norm_g[layer], k_norm_g[layer], w_pool_grp[layer],
                   pool_scale[layer], w_pool_up[layer], w_attn_up[layer], w_out[layer], norm2_g[layer],
                   w_router[layer], b_router[layer], w_gate_up[layer], b_gate_up[layer], w_down[layer],
                   b_down[layer])
    return x
```

```python
import functools

import jax
import jax.numpy as jnp
from jax import lax
from jax.experimental import pallas as pl
from jax.experimental.pallas import tpu as pltpu
from jax.experimental.pallas import tpu_sc as plsc

F32 = jnp.float32
BF16 = jnp.bfloat16
U32 = jnp.uint32
I32 = jnp.int32

EPS = 1e-6
POOL_WINDOWS = (2, 4, 8, 16)
POOL_GROUP_DIM = 128
POOL_HALO = 16
SB_HEAD_DIM = 64
TOP_K = 4
SWIGLU_LIMIT = 7.0
SWIGLU_ALPHA = 1.702
EXPERT_BLOCK = 512
LANES = 128
PACK_ROWS = 4
ATTN_BLOCK = 128
ATTN_SUB = 2
ATTN_EXIT_BITS = 70.0
LOG2_E = 1.4426950408889634
VMEM_LIMIT = 56 * 1024 * 1024
SC_CHUNK = 128
MOE_GROUPS = 2


def _dot(a, b):
    return jnp.dot(a, b, preferred_element_type=F32)


def _split_bf16(x):
    hi = x.astype(BF16)
    lo = (x - hi.astype(F32)).astype(BF16)
    return hi, lo


def _pack_rows(v, out_ref):
    half = v.shape[1] // 2
    lo = lax.bitcast_convert_type(v[:, :half].astype(BF16).astype(F32), U32) >> 16
    hi = lax.bitcast_convert_type(v[:, half:].astype(BF16).astype(F32), U32) & jnp.uint32(0xFFFF0000)
    w = lo | hi
    for c in range(PACK_ROWS):
        out_ref[c] = w[:, c * LANES:(c + 1) * LANES]


def _unpack_rows(ref):
    los, his = [], []
    for c in range(PACK_ROWS):
        w = ref[c]
        los.append(lax.bitcast_convert_type(w << 16, F32))
        his.append(lax.bitcast_convert_type(w & jnp.uint32(0xFFFF0000), F32))
    return jnp.concatenate(los + his, axis=1)


def _mixer_in_kernel(x_ref, g1_ref, win_ref, gq_ref, gk_ref, hsum_ref, wgrp_ref, pscale_ref, wpu_ref,
                     q_ref, k_ref, v_ref, ga_ref, p_ref, tail_ref, *, tm, pw, sw):
    i = pl.program_id(1)

    @pl.when(i == 0)
    def _():
        tail_ref[...] = jnp.zeros_like(tail_ref)

    x = x_ref[...]
    ms = jnp.mean(x * x, axis=-1, keepdims=True)
    h = (x * lax.rsqrt(ms + EPS) * g1_ref[...]).astype(BF16)

    def head_norm(t, gain):
        ss = _dot((t * t).astype(BF16), hsum_ref[...])
        return t * lax.rsqrt(ss * (1.0 / SB_HEAD_DIM) + EPS) * gain

    def project_q():
        q = _dot(h, win_ref[:, pw:pw + sw])
        q_ref[...] = (head_norm(q, gq_ref[...]) * (SB_HEAD_DIM ** -0.5 * LOG2_E)).astype(BF16)

    def project_k():
        k = _dot(h, win_ref[:, pw + sw:pw + 2 * sw])
        k_ref[...] = head_norm(k, gk_ref[...]).astype(BF16)

    def project_v():
        v_ref[...] = _dot(h, win_ref[:, pw + 2 * sw:pw + 3 * sw]).astype(BF16)

    d_model = x.shape[1]

    def project_attn_gate():
        g_attn = _dot(h, win_ref[:, pw + 3 * sw + d_model:pw + 3 * sw + 2 * d_model])
        ga_ref[...] = jax.nn.sigmoid(g_attn).astype(BF16)

    u = _dot(h, win_ref[:, 0:pw])
    xx = jnp.concatenate([tail_ref[...], u], axis=0)
    tail_ref[...] = u[tm - POOL_HALO:, :]
    pos = i * tm + lax.broadcasted_iota(I32, (tm, POOL_GROUP_DIM), 0)
    mixed = []
    for (g, w), project in zip(enumerate(POOL_WINDOWS), (project_q, project_k, project_v, project_attn_gate)):
        project()
        s = xx[:, g * POOL_GROUP_DIM:(g + 1) * POOL_GROUP_DIM]
        step = 1
        while step < w:
            s = s + pltpu.roll(s, step, axis=0)
            step *= 2
        count = jnp.minimum(pos + 1, w).astype(F32)
        ug = u[:, g * POOL_GROUP_DIM:(g + 1) * POOL_GROUP_DIM]
        d = s[POOL_HALO:, :] / count - ug
        mixed.append(_dot(d.astype(BF16), wgrp_ref[g]))
    pm = jnp.concatenate(mixed, axis=1) * pscale_ref[...]
    pool_out = _dot(pm.astype(BF16), wpu_ref[...])
    g_pool = _dot(h, win_ref[:, pw + 3 * sw:pw + 3 * sw + d_model])
    p_ref[...] = (jax.nn.sigmoid(g_pool) * pool_out).astype(BF16)


def _attn_kernel(q_ref, k_ref, v_ref, tri_ref, o_ref, *scratch, n_pairs):
    units = [(sub, p) for sub in range(ATTN_SUB) for p in range(n_pairs)]
    n_units = len(units)
    qs, acc, rr = scratch[:n_units], scratch[n_units:2 * n_units], scratch[2 * n_units:]
    bq = ATTN_BLOCK
    first_block = pl.program_id(1) * ATTN_SUB
    first_head = lax.broadcasted_iota(I32, (bq, LANES), 1) < SB_HEAD_DIM
    for u, (sub, p) in enumerate(units):
        q2 = q_ref[sub * bq:(sub + 1) * bq, p * LANES:(p + 1) * LANES]
        qs[u][:bq] = jnp.where(first_head, q2, jnp.zeros_like(q2))
        qs[u][bq:] = jnp.where(first_head, jnp.zeros_like(q2), q2)
    row = lax.broadcasted_iota(I32, (2 * bq, bq), 0)
    col = lax.broadcasted_iota(I32, (2 * bq, bq), 1)
    causal = col < (row & (bq - 1))
    contract_last = (((1,), (1,)), ((), ()))

    def softplus(z):
        return jnp.maximum(z, 0.0) + jnp.log2(1.0 + jnp.exp2(-jnp.abs(z)))

    def suffix_sums(sp):
        return _dot(sp.astype(BF16), tri_ref[...])

    def cols(ref, block, p):
        start = pl.multiple_of(jnp.maximum(block, 0) * bq, bq)
        return ref[pl.ds(start, bq), p * LANES:(p + 1) * LANES]

    def scores(u, block):
        return lax.dot_general(qs[u][...], cols(k_ref, block, units[u][1]), contract_last, preferred_element_type=F32)

    diag = [first_block + sub for sub, _ in units]
    z_d, z_n, s_d, s_n = {}, {}, {}, {}

    def stage_scores(u):
        z_d[u] = scores(u, diag[u])
        z_n[u] = scores(u, diag[u] - 1)

    def stage_sums(u):
        s_d[u] = suffix_sums(jnp.where(causal, softplus(z_d[u]), 0.0))
        s_n[u] = suffix_sums(jnp.where(diag[u] >= 1, softplus(z_n[u]), 0.0))

    def stage_values(u):
        p = units[u][1]
        a_d = jnp.where(causal, jnp.exp2(z_d[u] - s_d[u][:, :bq]), 0.0)
        r_d = s_d[u][:, bq:]
        a_n = jnp.where(diag[u] >= 1, jnp.exp2(z_n[u] - (r_d + s_n[u][:, :bq])), 0.0)
        acc[u][...] = (_dot(a_d.astype(BF16), cols(v_ref, diag[u], p)) + _dot(a_n.astype(BF16), cols(v_ref, diag[u] - 1, p)))
        rr[u][...] = r_d + s_n[u][:, bq:]

    for t in range(n_units + 2):
        if t < n_units:
            stage_scores(t)
        if 0 <= t - 1 < n_units:
            stage_sums(t - 1)
        if 0 <= t - 2 < n_units:
            stage_values(t - 2)

    def r_min():
        m = rr[0][...]
        for u in range(1, n_units):
            m = jnp.minimum(m, rr[u][...])
        return jnp.min(m)

    def cond(c):
        back, rm = c
        return jnp.logical_and(diag[-1] - back >= 0, rm < ATTN_EXIT_BITS)

    def body(c):
        back, _ = c
        blocks = [d - back for d in diag]
        zs = [scores(u, blocks[u]) for u in range(n_units)]
        ss = [suffix_sums(jnp.where(blocks[u] >= 0, softplus(zs[u]), 0.0)) for u in range(n_units)]
        for u, (_, p) in enumerate(units):
            r = rr[u][...]
            a = jnp.where(blocks[u] >= 0, jnp.exp2(zs[u] - (r + ss[u][:, :bq])), 0.0)
            acc[u][...] += _dot(a.astype(BF16), cols(v_ref, blocks[u], p))
            rr[u][...] = r + ss[u][:, bq:]
        return back + 1, r_min()

    lax.while_loop(cond, body, (2, r_min()))
    for u, (sub, p) in enumerate(units):
        o_ref[sub * bq:(sub + 1) * bq, p * LANES:(p + 1) * LANES] = (
            jnp.where(first_head, acc[u][:bq], acc[u][bq:]).astype(BF16))


def _mixer_out_kernel(x_ref, p_ref, ga_ref, sba_ref, wau_ref, wout_ref, g2_ref, wr_hi_ref, wr_lo_ref, br_ref,
                      ltri_ref, x1_ref, hp_ref, ri_ref, gate_ref, cnt_ref, logit_ref, *, tm):
    step = pl.program_id(0)

    @pl.when(step == 0)
    def _():
        cnt_ref[...] = jnp.zeros_like(cnt_ref)
        logit_ref[1] = jnp.zeros((tm, LANES), F32)

    logits = logit_ref[(step + 1) % 2]
    routed = jnp.where(step >= 1, 1.0, 0.0)
    lane = lax.broadcasted_iota(I32, logits.shape, 1).astype(F32)
    work = logits
    vals, idxs = [], []

    def topk_round():
        nonlocal work
        m = jnp.max(work, axis=-1, keepdims=True)
        ik = jnp.min(jnp.where(work == m, lane, float(LANES)), axis=-1, keepdims=True)
        vals.append(m)
        idxs.append(ik)
        work = jnp.where(lane == ik, -jnp.inf, work)

    d_model = x_ref.shape[1]
    quarter = d_model // TOP_K
    topk_round()
    attn_out = _dot(sba_ref[...], wau_ref[...])
    merged = (p_ref[...].astype(F32) + ga_ref[...].astype(F32) * attn_out).astype(BF16)
    x1_parts = []
    for c in range(TOP_K):
        if c >= 1:
            topk_round()
        cols = slice(c * quarter, (c + 1) * quarter)
        x1_parts.append(x_ref[:, cols] + _dot(merged, wout_ref[:, cols]))
    x1 = jnp.concatenate(x1_parts, axis=1)
    x1_ref[...] = x1

    es = [jnp.exp(v - vals[0]) for v in vals]
    denom = es[0] + es[1] + es[2] + es[3]
    hot = jnp.zeros(logits.shape, F32)
    for ik in idxs:
        hot = hot + jnp.where(lane == ik, routed, 0.0)
    before = _dot(ltri_ref[...], hot.astype(BF16)) + cnt_ref[0:1, :]

    ms = jnp.mean(x1 * x1, axis=-1, keepdims=True)
    h2 = x1 * lax.rsqrt(ms + EPS) * g2_ref[...]
    _pack_rows(h2, hp_ref)

    ri = jnp.zeros(logits.shape, F32)
    gt = jnp.zeros(logits.shape, F32)
    for kk in range(TOP_K):
        rank = jnp.sum(jnp.where(lane == idxs[kk], before, 0.0), axis=-1, keepdims=True)
        ri = jnp.where(lane == kk, idxs[kk], ri)
        ri = jnp.where(lane == TOP_K + kk, rank, ri)
        gt = jnp.where(lane == kk, es[kk] / denom, gt)
    ri_ref[...] = ri.T[:2 * TOP_K].astype(I32)
    gate_ref[...] = gt
    cnt_ref[...] = cnt_ref[...] + jnp.sum(hot, axis=0, keepdims=True)

    h_hi, h_lo = _split_bf16(h2)
    logit_ref[step % 2] = (_dot(h_hi, wr_hi_ref[...]) + _dot(h_hi, wr_lo_ref[...]) + _dot(h_lo, wr_hi_ref[...])
                           + br_ref[...])


def _dest_kernel(start_ref, ri_ref, dest_ref):
    idx = ri_ref[:TOP_K, :]
    dest = ri_ref[TOP_K:, :]
    for e in range(start_ref.shape[0]):
        dest = dest + jnp.where(idx == e, start_ref[e], 0)
    dest_ref[...] = dest


def _sc_mesh():
    info = plsc.get_sparse_core_info()
    mesh = plsc.VectorSubcoreMesh(core_axis_name="c", subcore_axis_name="s")
    return mesh, info.num_cores, info.num_subcores, info.num_lanes


def _sc_scatter_rows(rows, idx, n_out):
    mesh, nc, ns, lanes = _sc_mesh()
    nw = nc * ns
    planes, m, width = rows.shape
    ch = SC_CHUNK
    n_ch = m // ch // nw
    half = planes // 2
    assert planes % 2 == 0 and n_ch * ch * nw == m and idx.shape == (m // ch * TOP_K, ch) and n_ch >= 2
    assert n_ch % 8 == 0

    @functools.partial(
        pl.kernel, mesh=mesh, out_type=jax.ShapeDtypeStruct((planes * n_out, width), rows.dtype),
        scratch_types=([pltpu.VMEM((n_ch * TOP_K, ch), I32)] + [pltpu.VMEM((ch, width), rows.dtype)] * planes
                       + [pltpu.VMEM((TOP_K, ch), I32)] * planes
                       + [pltpu.SemaphoreType.DMA((planes,)), pltpu.SemaphoreType.DMA((planes,))]))
    def scatter_kernel(rows_hbm, idx_hbm, out_hbm, idx_v, *rest):
        bufs, ibufs, rsem, ssem = rest[:planes], rest[planes:2 * planes], rest[2 * planes], rest[2 * planes + 1]
        wid = lax.axis_index("s") * nc + lax.axis_index("c")
        for kk in range(TOP_K):
            pltpu.sync_copy(idx_hbm.at[pl.ds(kk * (m // ch) + wid * n_ch, n_ch)], idx_v.at[pl.ds(kk * n_ch, n_ch)])
        base = wid * n_ch * ch

        def read(j, b):
            return pltpu.make_async_copy(rows_hbm.at[b, pl.ds(pl.multiple_of(base + j * ch, ch), ch)], bufs[b], rsem.at[b])

        def scatters(b):
            return [pltpu.make_async_copy(bufs[b], out_hbm.at[ibufs[b].at[kk]], ssem.at[b]) for kk in range(TOP_K)]

        def start_scatters(j, b):
            for kk in range(TOP_K):
                for t in range(0, ch, lanes):
                    ibufs[b][kk, pl.ds(t, lanes)] = idx_v[kk * n_ch + j, pl.ds(t, lanes)] + b * n_out
            for c in scatters(b):
                c.start()

        def wait_scatters(b):
            for c in scatters(b):
                c.wait()

        def finish(j, b):
            pj, pb = (j, b - half) if b >= half else (j - 1, b + half)
            read(pj, pb).wait()
            start_scatters(pj, pb)

        for b in range(planes):
            read(0, b).start()
        for b in range(half, planes):
            finish(0, b)

        @pl.loop(1, n_ch)
        def _(j):
            for b in range(planes):
                wait_scatters(b)
                read(j, b).start()
                finish(j, b)

        for b in range(half):
            finish(n_ch, b)
        for b in range(planes):
            wait_scatters(b)

    return scatter_kernel(rows, idx).reshape(planes, n_out, width)


def _padfill_kernel(start_ref, len_ref, xs_in_ref, xs_ref, zeros_ref, sem):
    del xs_in_ref
    zeros_ref[...] = jnp.zeros_like(zeros_ref)
    bits = [1 << b for b in reversed(range(EXPERT_BLOCK.bit_length() - 1))]

    def pieces(e):
        n = len_ref[e]
        for bit in bits:
            row0 = start_ref[e] + (n & ~(2 * bit - 1))
            copy = pltpu.make_async_copy(zeros_ref.at[:, pl.ds(0, bit)], xs_ref.at[:, pl.ds(row0, bit)], sem)
            yield (n & bit) != 0, copy

    def start(e, c):
        for on, copy in pieces(e):
            pl.when(on)(copy.start)
        return c

    def wait(e, c):
        for on, copy in pieces(e):
            pl.when(on)(copy.wait)
        return c

    lax.fori_loop(0, start_ref.shape[0], start, 0)
    lax.fori_loop(0, start_ref.shape[0], wait, 0)


def _experts_kernel(be_ref, nb_ref, xs_ref, wgu_ref, bgu_ref, wd_ref, bd_ref, ys_ref, wgu_bf_ref, wd_bf_ref):
    blk = pl.program_id(0)
    de = wd_ref.shape[1]

    @pl.when(jnp.logical_or(blk == 0, be_ref[blk] != be_ref[jnp.maximum(blk - 1, 0)]))
    def _():
        wgu_bf_ref[...] = wgu_ref[0].astype(BF16)
        wd_bf_ref[...] = wd_ref[0].astype(BF16)

    @pl.when(blk < nb_ref[0])
    def _():
        x = _unpack_rows(xs_ref).astype(BF16)
        gu = _dot(x, wgu_bf_ref[...]) + bgu_ref[0]
        glu = jnp.minimum(gu[:, :de], SWIGLU_LIMIT)
        lin = jnp.clip(gu[:, de:], -SWIGLU_LIMIT, SWIGLU_LIMIT)
        act = glu * jax.nn.sigmoid(SWIGLU_ALPHA * glu) * (lin + 1.0)
        y = _dot(act.astype(BF16), wd_bf_ref[...]) + bd_ref[0]
        _pack_rows(y, ys_ref)

    @pl.when(blk >= nb_ref[0])
    def _():
        ys_ref[...] = jnp.zeros_like(ys_ref)


def _sc_gather_rows(table, idx):
    mesh, nc, ns, lanes = _sc_mesh()
    nw = nc * ns
    planes, n_tab, width = table.shape
    n_idx_rows, ch = idx.shape
    n_ch = n_idx_rows // nw
    m = n_idx_rows * ch
    half = planes // 2
    assert planes % 2 == 0 and ch == SC_CHUNK and n_ch * nw == n_idx_rows and n_ch >= 2

    @functools.partial(
        pl.kernel, mesh=mesh, out_type=jax.ShapeDtypeStruct((planes, m, width), table.dtype),
        scratch_types=([pltpu.VMEM((n_ch, ch), I32)] + [pltpu.VMEM((ch, width), table.dtype)] * planes
                       + [pltpu.VMEM((8, ch), I32)] * planes
                       + [pltpu.SemaphoreType.DMA((planes,)), pltpu.SemaphoreType.DMA((planes,))]))
    def gather_kernel(table_hbm, idx_hbm, out_hbm, idx_v, *rest):
        bufs, ibufs, gsem, wsem = rest[:planes], rest[planes:2 * planes], rest[2 * planes], rest[2 * planes + 1]
        wid = lax.axis_index("s") * nc + lax.axis_index("c")
        pltpu.sync_copy(idx_hbm.at[pl.ds(wid * n_ch, n_ch)], idx_v)
        base = wid * n_ch * ch

        def gather(b):
            return pltpu.make_async_copy(table_hbm.at[ibufs[b].at[0]], bufs[b], gsem.at[b])

        def start_gather(j, b):
            for t in range(0, ch, lanes):
                ibufs[b][0, pl.ds(t, lanes)] = idx_v[j, pl.ds(t, lanes)] + b * n_tab
            gather(b).start()

        def write(j, b):
            return pltpu.make_async_copy(bufs[b], out_hbm.at[b, pl.ds(pl.multiple_of(base + j * ch, ch), ch)], wsem.at[b])

        def finish(j, b):
            pj, pb = (j, b - half) if b >= half else (j - 1, b + half)
            gather(pb).wait()
            write(pj, pb).start()

        for b in range(planes):
            start_gather(0, b)
        for b in range(half, planes):
            finish(0, b)

        @pl.loop(1, n_ch)
        def _(j):
            for b in range(planes):
                write(j - 1, b).wait()
                start_gather(j, b)
                finish(j, b)

        for b in range(half):
            finish(n_ch, b)
        for b in range(planes):
            write(n_ch - 1, b).wait()

    return gather_kernel(table.reshape(planes * n_tab, width), idx)


def _combine_kernel(x1_ref, gate_ref, y0_ref, y1_ref, y2_ref, y3_ref, *rest):
    o_ref = rest[-1]
    gate = gate_ref[...]
    out = x1_ref[...]
    for kk, y_ref in enumerate((y0_ref, y1_ref, y2_ref, y3_ref)):
        out = out + gate[:, kk:kk + 1] * _unpack_rows(y_ref)
    o_ref[...] = out


def _const_spec(shape):
    nd = len(shape)
    return pl.BlockSpec(shape, lambda *_: (0,) * nd)


def _layer(x, norm1_g, w_in, q_norm_g, k_norm_g, w_pool_grp, pool_scale, w_pool_up, w_attn_up, w_out, norm2_g,
           w_router, b_router, w_gate_up, b_gate_up, w_down, b_down):
    B, S, D = x.shape
    N = B * S
    pw = w_pool_up.shape[0]
    sw = w_attn_up.shape[0]
    n_exp = w_router.shape[1]
    de = w_down.shape[1]
    heads = sw // SB_HEAD_DIM
    assert pw == len(POOL_WINDOWS) * POOL_GROUP_DIM and heads % 2 == 0 and n_exp <= LANES
    assert D == 2 * PACK_ROWS * LANES and w_in.shape[1] == pw + 3 * sw + 2 * D
    tm = 512 if S % 512 == 0 else 256
    assert S % tm == 0 and S % (ATTN_SUB * ATTN_BLOCK) == 0
    xf = x.reshape(N, D)
    cparams = functools.partial(pltpu.CompilerParams, vmem_limit_bytes=VMEM_LIMIT)

    hsum = (jnp.arange(sw)[:, None] // SB_HEAD_DIM == jnp.arange(sw)[None, :] // SB_HEAD_DIM).astype(BF16)
    nt = S // tm
    tok_spec = lambda w: pl.BlockSpec((tm, w), lambda b, i: (b * nt + i, 0))
    q2, k2, v2, ga, pg = pl.pallas_call(
        functools.partial(_mixer_in_kernel, tm=tm, pw=pw, sw=sw),
        grid=(B, nt),
        in_specs=[tok_spec(D), _const_spec((1, D)), _const_spec((D, w_in.shape[1])), _const_spec((1, sw)),
                  _const_spec((1, sw)), _const_spec((sw, sw)),
                  _const_spec((len(POOL_WINDOWS), POOL_GROUP_DIM, POOL_GROUP_DIM)), _const_spec((1, pw)),
                  _const_spec((pw, D))],
        out_specs=[tok_spec(sw), tok_spec(sw), tok_spec(sw), tok_spec(D), tok_spec(D)],
        out_shape=[jax.ShapeDtypeStruct((N, sw), BF16)] * 3 + [jax.ShapeDtypeStruct((N, D), BF16)] * 2,
        scratch_shapes=[pltpu.VMEM((POOL_HALO, pw), F32)],
        compiler_params=cparams(dimension_semantics=("arbitrary", "arbitrary")),
        name="mixer_in",
    )(xf, norm1_g.reshape(1, D), w_in.astype(BF16), jnp.tile(q_norm_g, heads).reshape(1, sw),
      jnp.tile(k_norm_g, heads).reshape(1, sw), hsum, w_pool_grp.astype(BF16), pool_scale.reshape(1, pw),
      w_pool_up.astype(BF16))

    bq = ATTN_BLOCK
    rows_q = ATTN_SUB * bq
    nq = S // rows_q
    jj = jnp.arange(bq)
    tri = jnp.concatenate([(jj[:, None] >= jj[None, :]).astype(BF16), jnp.ones((bq, bq), BF16)], axis=1)
    n_pairs = heads // 2
    n_units = ATTN_SUB * n_pairs
    kv_spec = pl.BlockSpec((S, sw), lambda b, qi: (b, 0), pipeline_mode=pl.Buffered(1))
    sba = pl.pallas_call(
        functools.partial(_attn_kernel, n_pairs=n_pairs),
        grid=(B, nq),
        in_specs=[pl.BlockSpec((rows_q, sw), lambda b, qi: (b * nq + qi, 0)), kv_spec, kv_spec,
                  _const_spec((bq, 2 * bq))],
        out_specs=pl.BlockSpec((rows_q, sw), lambda b, qi: (b * nq + qi, 0)),
        out_shape=jax.ShapeDtypeStruct((N, sw), BF16),
        scratch_shapes=([pltpu.VMEM((2 * bq, LANES), BF16)] * n_units + [pltpu.VMEM((2 * bq, LANES), F32)] * (2 * n_units)),
        compiler_params=cparams(dimension_semantics=("arbitrary", "arbitrary")),
        name="sb_attn",
    )(q2, k2, v2, tri)

    wr = jnp.zeros((D, LANES), F32).at[:, :n_exp].set(w_router)
    wr_hi = wr.astype(BF16)
    wr_lo = (wr - wr_hi.astype(F32)).astype(BF16)
    br = jnp.full((1, LANES), -jnp.inf, F32).at[0, :n_exp].set(b_router)
    ltri = (jnp.arange(tm)[:, None] > jnp.arange(tm)[None, :]).astype(BF16)
    assert N % (MOE_GROUPS * tm) == 0
    ng = N // MOE_GROUPS
    steps = ng // tm
    n_assign = ng * TOP_K
    n_blocks = -(-(n_assign + n_exp * (EXPERT_BLOCK - 1)) // EXPERT_BLOCK)
    n_rows = n_blocks * EXPERT_BLOCK
    plane_spec = lambda rows, index: pl.BlockSpec((PACK_ROWS, rows, LANES), lambda i, *_: (0, index(i, *_), 0))
    row_spec = lambda w: pl.BlockSpec((tm, w), lambda i: (i, 0))
    w_attn_up_bf, w_out_bf, g2 = w_attn_up.astype(BF16), w_out.astype(BF16), norm2_g.reshape(1, D)
    bgu, bdn = b_gate_up.reshape(n_exp, 1, 2 * de), b_down.reshape(n_exp, 1, D)
    out = None
    for grp in range(MOE_GROUPS):
        grp_spec = lambda w, first=grp * steps: pl.BlockSpec((tm, w), lambda i: (first + i, 0))

        cur = lambda i: jnp.minimum(i, steps - 1)
        prev = lambda i: jnp.maximum(i - 1, 0)
        in_spec = lambda w, first=grp * steps: pl.BlockSpec((tm, w), lambda i: (first + cur(i), 0))
        x1, hpk, ri, gate, cnt = pl.pallas_call(
            functools.partial(_mixer_out_kernel, tm=tm),
            grid=(steps + 1,),
            in_specs=[in_spec(D), in_spec(D), in_spec(D), in_spec(sw), _const_spec((sw, D)), _const_spec((D, D)),
                      _const_spec((1, D)), _const_spec((D, LANES)), _const_spec((D, LANES)), _const_spec((1, LANES)),
                      _const_spec((tm, tm))],
            out_specs=[pl.BlockSpec((tm, D), lambda i: (cur(i), 0)), plane_spec(tm, cur),
                       pl.BlockSpec((2 * TOP_K, tm), lambda i: (0, prev(i))),
                       pl.BlockSpec((tm, LANES), lambda i: (prev(i), 0)), _const_spec((8, LANES))],
            out_shape=[jax.ShapeDtypeStruct((ng, D), F32), jax.ShapeDtypeStruct((PACK_ROWS, ng, LANES), U32),
                       jax.ShapeDtypeStruct((2 * TOP_K, ng), I32), jax.ShapeDtypeStruct((ng, LANES), F32),
                       jax.ShapeDtypeStruct((8, LANES), F32)],
            scratch_shapes=[pltpu.VMEM((2, tm, LANES), F32)],
            compiler_params=cparams(dimension_semantics=("arbitrary",)),
            name="mixer_out",
        )(xf, pg, ga, sba, w_attn_up_bf, w_out_bf, g2, wr_hi, wr_lo, br, ltri)

        counts = cnt[0, :n_exp].astype(I32)
        padded = (counts + EXPERT_BLOCK - 1) // EXPERT_BLOCK * EXPERT_BLOCK
        padded_end = jnp.cumsum(padded)
        start_pad = padded_end - padded
        td = min(ng, 8192)
        dest = pl.pallas_call(
            _dest_kernel,
            grid_spec=pltpu.PrefetchScalarGridSpec(
                num_scalar_prefetch=1,
                grid=(ng // td,),
                in_specs=[pl.BlockSpec((2 * TOP_K, td), lambda i, sp: (0, i))],
                out_specs=pl.BlockSpec((TOP_K, td), lambda i, sp: (0, i)),
            ),
            out_shape=jax.ShapeDtypeStruct((TOP_K, ng), I32),
            compiler_params=cparams(dimension_semantics=("arbitrary",)),
            name="dest",
        )(start_pad, ri).reshape(-1, SC_CHUNK)
        block_start = jnp.arange(n_blocks, dtype=I32) * EXPERT_BLOCK
        block_expert = jnp.minimum(jnp.sum((padded_end[None, :] <= block_start[:, None]).astype(I32), axis=1),
                                   n_exp - 1)
        n_used = (padded_end[-1] // EXPERT_BLOCK).astype(I32).reshape(1)

        xs = _sc_scatter_rows(hpk, dest, n_rows)
        xs = pl.pallas_call(
            _padfill_kernel,
            grid_spec=pltpu.PrefetchScalarGridSpec(
                num_scalar_prefetch=2,
                grid=(1,),
                in_specs=[pl.BlockSpec(memory_space=pl.ANY)],
                out_specs=pl.BlockSpec(memory_space=pl.ANY),
                scratch_shapes=[pltpu.VMEM((PACK_ROWS, EXPERT_BLOCK // 2, LANES), U32), pltpu.SemaphoreType.DMA(())],
            ),
            out_shape=jax.ShapeDtypeStruct((PACK_ROWS, n_rows, LANES), U32),
            input_output_aliases={2: 0},
            compiler_params=cparams(dimension_semantics=("arbitrary",)),
            name="padfill",
        )(start_pad + counts, padded - counts, xs)

        ys = pl.pallas_call(
            _experts_kernel,
            grid_spec=pltpu.PrefetchScalarGridSpec(
                num_scalar_prefetch=2,
                grid=(n_blocks,),
                in_specs=[plane_spec(EXPERT_BLOCK, lambda i, be, nb: jnp.minimum(i, nb[0] - 1)),
                          pl.BlockSpec((1, D, 2 * de), lambda i, be, nb: (be[i], 0, 0)),
                          pl.BlockSpec((1, 1, 2 * de), lambda i, be, nb: (be[i], 0, 0)),
                          pl.BlockSpec((1, de, D), lambda i, be, nb: (be[i], 0, 0)),
                          pl.BlockSpec((1, 1, D), lambda i, be, nb: (be[i], 0, 0))],
                out_specs=plane_spec(EXPERT_BLOCK, lambda i, be, nb: i),
                scratch_shapes=[pltpu.VMEM((D, 2 * de), BF16), pltpu.VMEM((de, D), BF16)],
            ),
            out_shape=jax.ShapeDtypeStruct((PACK_ROWS, n_rows, LANES), U32),
            compiler_params=cparams(dimension_semantics=("arbitrary",)),
            name="experts",
        )(block_expert, n_used, xs, w_gate_up, bgu, w_down, bdn)

        y4 = _sc_gather_rows(ys, dest.reshape(-1, SC_CHUNK))

        operands = [x1, gate, y4, y4, y4, y4] + ([] if out is None else [out])
        out = pl.pallas_call(
            _combine_kernel,
            grid=(steps,),
            in_specs=([row_spec(D), row_spec(LANES)]
                      + [plane_spec(tm, lambda i, kk=kk: kk * steps + i) for kk in range(TOP_K)]
                      + ([] if out is None else [pl.BlockSpec(memory_space=pl.ANY)])),
            out_specs=grp_spec(D),
            out_shape=jax.ShapeDtypeStruct((N, D), F32),
            input_output_aliases={} if out is None else {len(operands) - 1: 0},
            compiler_params=cparams(dimension_semantics=("arbitrary",)),
            name="combine",
        )(*operands)
    return out.reshape(B, S, D)


def kernel(x, norm1_g, w_in, q_norm_g, k_norm_g, w_pool_grp, pool_scale, w_pool_up, w_attn_up, w_out, norm2_g,
           w_router, b_router, w_gate_up, b_gate_up, w_down, b_down):
    for layer in range(norm1_g.shape[0]):
        x = _layer(x, norm1_g[layer], w_in[layer], q_norm_g[layer], k_norm_g[layer], w_pool_grp[layer],
                   pool_scale[layer], w_pool_up[layer], w_attn_up[layer], w_out[layer], norm2_g[layer],
                   w_router[layer], b_router[layer], w_gate_up[layer], b_gate_up[layer], w_down[layer],
                   b_down[layer])
    return x
```

```python
import functools

import jax
import jax.numpy as jnp
from jax import lax
from jax.experimental import pallas as pl
from jax.experimental.pallas import tpu as pltpu
from jax.experimental.pallas import tpu_sc as plsc

F32 = jnp.float32
BF16 = jnp.bfloat16
U32 = jnp.uint32
I32 = jnp.int32

EPS = 1e-6
POOL_WINDOWS = (2, 4, 8, 16)
POOL_GROUP_DIM = 128
POOL_HALO = 16
SB_HEAD_DIM = 64
TOP_K = 4
SWIGLU_LIMIT = 7.0
SWIGLU_ALPHA = 1.702
EXPERT_BLOCK = 512
EXPERT_SPLIT = 4
LANES = 128
PACK_ROWS = 4
ATTN_BLOCK = 128
ATTN_SUB = 2
ATTN_EXIT_BITS = 70.0
LOG2_E = 1.4426950408889634
VMEM_LIMIT = 56 * 1024 * 1024
SC_CHUNK = 128
MOE_GROUPS = 2


def _dot(a, b):
    return jnp.dot(a, b, preferred_element_type=F32)


def _split_bf16(x):
    hi = x.astype(BF16)
    lo = (x - hi.astype(F32)).astype(BF16)
    return hi, lo


def _pack_rows(v, out_ref):
    half = v.shape[1] // 2
    lo = lax.bitcast_convert_type(v[:, :half].astype(BF16).astype(F32), U32) >> 16
    hi = lax.bitcast_convert_type(v[:, half:].astype(BF16).astype(F32), U32) & jnp.uint32(0xFFFF0000)
    w = lo | hi
    for c in range(PACK_ROWS):
        out_ref[c] = w[:, c * LANES:(c + 1) * LANES]


def _unpack_rows(ref):
    los, his = [], []
    for c in range(PACK_ROWS):
        w = ref[c]
        los.append(lax.bitcast_convert_type(w << 16, F32))
        his.append(lax.bitcast_convert_type(w & jnp.uint32(0xFFFF0000), F32))
    return jnp.concatenate(los + his, axis=1)


def _mixer_in_kernel(x_ref, g1_ref, win_ref, gq_ref, gk_ref, hsum_ref, wgrp_ref, pscale_ref, wpu_ref,
                     q_ref, k_ref, v_ref, ga_ref, p_ref, tail_ref, *, tm, pw, sw):
    i = pl.program_id(1)

    @pl.when(i == 0)
    def _():
        tail_ref[...] = jnp.zeros_like(tail_ref)

    x = x_ref[...]
    ms = jnp.mean(x * x, axis=-1, keepdims=True)
    h = (x * lax.rsqrt(ms + EPS) * g1_ref[...]).astype(BF16)

    def head_norm(t, gain):
        ss = _dot((t * t).astype(BF16), hsum_ref[...])
        return t * lax.rsqrt(ss * (1.0 / SB_HEAD_DIM) + EPS) * gain

    def project_q():
        q = _dot(h, win_ref[:, pw:pw + sw])
        q_ref[...] = (head_norm(q, gq_ref[...]) * (SB_HEAD_DIM ** -0.5 * LOG2_E)).astype(BF16)

    def project_k():
        k = _dot(h, win_ref[:, pw + sw:pw + 2 * sw])
        k_ref[...] = head_norm(k, gk_ref[...]).astype(BF16)

    def project_v():
        v_ref[...] = _dot(h, win_ref[:, pw + 2 * sw:pw + 3 * sw]).astype(BF16)

    d_model = x.shape[1]

    def project_attn_gate():
        g_attn = _dot(h, win_ref[:, pw + 3 * sw + d_model:pw + 3 * sw + 2 * d_model])
        ga_ref[...] = jax.nn.sigmoid(g_attn).astype(BF16)

    u = _dot(h, win_ref[:, 0:pw])
    xx = jnp.concatenate([tail_ref[...], u], axis=0)
    tail_ref[...] = u[tm - POOL_HALO:, :]
    pos = i * tm + lax.broadcasted_iota(I32, (tm, POOL_GROUP_DIM), 0)
    mixed = []
    for (g, w), project in zip(enumerate(POOL_WINDOWS), (project_q, project_k, project_v, project_attn_gate)):
        project()
        s = xx[:, g * POOL_GROUP_DIM:(g + 1) * POOL_GROUP_DIM]
        step = 1
        while step < w:
            s = s + pltpu.roll(s, step, axis=0)
            step *= 2
        count = jnp.minimum(pos + 1, w).astype(F32)
        ug = u[:, g * POOL_GROUP_DIM:(g + 1) * POOL_GROUP_DIM]
        d = s[POOL_HALO:, :] / count - ug
        mixed.append(_dot(d.astype(BF16), wgrp_ref[g]))
    pm = jnp.concatenate(mixed, axis=1) * pscale_ref[...]
    pool_out = _dot(pm.astype(BF16), wpu_ref[...])
    g_pool = _dot(h, win_ref[:, pw + 3 * sw:pw + 3 * sw + d_model])
    p_ref[...] = (jax.nn.sigmoid(g_pool) * pool_out).astype(BF16)


def _attn_kernel(q_ref, k_ref, v_ref, tri_ref, o_ref, *scratch, n_pairs):
    units = [(sub, p) for sub in range(ATTN_SUB) for p in range(n_pairs)]
    n_units = len(units)
    qs, acc, rr = scratch[:n_units], scratch[n_units:2 * n_units], scratch[2 * n_units:]
    bq = ATTN_BLOCK
    first_block = pl.program_id(1) * ATTN_SUB
    first_head = lax.broadcasted_iota(I32, (bq, LANES), 1) < SB_HEAD_DIM
    for u, (sub, p) in enumerate(units):
        q2 = q_ref[sub * bq:(sub + 1) * bq, p * LANES:(p + 1) * LANES]
        qs[u][:bq] = jnp.where(first_head, q2, jnp.zeros_like(q2))
        qs[u][bq:] = jnp.where(first_head, jnp.zeros_like(q2), q2)
    row = lax.broadcasted_iota(I32, (2 * bq, bq), 0)
    col = lax.broadcasted_iota(I32, (2 * bq, bq), 1)
    causal = col < (row & (bq - 1))
    contract_last = (((1,), (1,)), ((), ()))

    def softplus(z):
        return jnp.maximum(z, 0.0) + jnp.log2(1.0 + jnp.exp2(-jnp.abs(z)))

    def suffix_sums(sp):
        return _dot(sp.astype(BF16), tri_ref[...])

    def cols(ref, block, p):
        start = pl.multiple_of(jnp.maximum(block, 0) * bq, bq)
        return ref[pl.ds(start, bq), p * LANES:(p + 1) * LANES]

    def scores(u, block):
        return lax.dot_general(qs[u][...], cols(k_ref, block, units[u][1]), contract_last, preferred_element_type=F32)

    diag = [first_block + sub for sub, _ in units]
    z_d, z_n, s_d, s_n = {}, {}, {}, {}

    def stage_scores(u):
        z_d[u] = scores(u, diag[u])
        z_n[u] = scores(u, diag[u] - 1)

    def stage_sums(u):
        s_d[u] = suffix_sums(jnp.where(causal, softplus(z_d[u]), 0.0))
        s_n[u] = suffix_sums(jnp.where(diag[u] >= 1, softplus(z_n[u]), 0.0))

    def stage_values(u):
        p = units[u][1]
        a_d = jnp.where(causal, jnp.exp2(z_d[u] - s_d[u][:, :bq]), 0.0)
        r_d = s_d[u][:, bq:]
        a_n = jnp.where(diag[u] >= 1, jnp.exp2(z_n[u] - (r_d + s_n[u][:, :bq])), 0.0)
        acc[u][...] = (_dot(a_d.astype(BF16), cols(v_ref, diag[u], p)) + _dot(a_n.astype(BF16), cols(v_ref, diag[u] - 1, p)))
        rr[u][...] = r_d + s_n[u][:, bq:]

    for t in range(n_units + 2):
        if t < n_units:
            stage_scores(t)
        if 0 <= t - 1 < n_units:
            stage_sums(t - 1)
        if 0 <= t - 2 < n_units:
            stage_values(t - 2)

    def r_min():
        m = rr[0][...]
        for u in range(1, n_units):
            m = jnp.minimum(m, rr[u][...])
        return jnp.min(m)

    def cond(c):
        back, rm = c
        return jnp.logical_and(diag[-1] - back >= 0, rm < ATTN_EXIT_BITS)

    def body(c):
        back, _ = c
        blocks = [d - back for d in diag]
        zs = [scores(u, blocks[u]) for u in range(n_units)]
        ss = [suffix_sums(jnp.where(blocks[u] >= 0, softplus(zs[u]), 0.0)) for u in range(n_units)]
        for u, (_, p) in enumerate(units):
            r = rr[u][...]
            a = jnp.where(blocks[u] >= 0, jnp.exp2(zs[u] - (r + ss[u][:, :bq])), 0.0)
            acc[u][...] += _dot(a.astype(BF16), cols(v_ref, blocks[u], p))
            rr[u][...] = r + ss[u][:, bq:]
        return back + 1, r_min()

    lax.while_loop(cond, body, (2, r_min()))
    for u, (sub, p) in enumerate(units):
        o_ref[sub * bq:(sub + 1) * bq, p * LANES:(p + 1) * LANES] = (
            jnp.where(first_head, acc[u][:bq], acc[u][bq:]).astype(BF16))


def _mixer_out_kernel(x_ref, p_ref, ga_ref, sba_ref, wau_ref, wout_ref, g2_ref, wr_hi_ref, wr_lo_ref, br_ref,
                      ltri_ref, x1_ref, hp_ref, ri_ref, gate_ref, cnt_ref, logit_ref, *, tm):
    step = pl.program_id(0)

    @pl.when(step == 0)
    def _():
        cnt_ref[...] = jnp.zeros_like(cnt_ref)
        logit_ref[1] = jnp.zeros((tm, LANES), F32)

    logits = logit_ref[(step + 1) % 2]
    routed = jnp.where(step >= 1, 1.0, 0.0)
    lane = lax.broadcasted_iota(I32, logits.shape, 1).astype(F32)
    work = logits
    vals, idxs = [], []

    def topk_round():
        nonlocal work
        m = jnp.max(work, axis=-1, keepdims=True)
        ik = jnp.min(jnp.where(work == m, lane, float(LANES)), axis=-1, keepdims=True)
        vals.append(m)
        idxs.append(ik)
        work = jnp.where(lane == ik, -jnp.inf, work)

    d_model = x_ref.shape[1]
    quarter = d_model // TOP_K
    topk_round()
    attn_out = _dot(sba_ref[...], wau_ref[...])
    merged = (p_ref[...].astype(F32) + ga_ref[...].astype(F32) * attn_out).astype(BF16)
    x1_parts = []
    for c in range(TOP_K):
        if c >= 1:
            topk_round()
        cols = slice(c * quarter, (c + 1) * quarter)
        x1_parts.append(x_ref[:, cols] + _dot(merged, wout_ref[:, cols]))
    x1 = jnp.concatenate(x1_parts, axis=1)
    x1_ref[...] = x1

    es = [jnp.exp(v - vals[0]) for v in vals]
    denom = es[0] + es[1] + es[2] + es[3]
    hot = jnp.zeros(logits.shape, F32)
    for ik in idxs:
        hot = hot + jnp.where(lane == ik, routed, 0.0)
    before = _dot(ltri_ref[...], hot.astype(BF16)) + cnt_ref[0:1, :]

    ms = jnp.mean(x1 * x1, axis=-1, keepdims=True)
    h2 = x1 * lax.rsqrt(ms + EPS) * g2_ref[...]
    _pack_rows(h2, hp_ref)

    ri = jnp.zeros(logits.shape, F32)
    gt = jnp.zeros(logits.shape, F32)
    for kk in range(TOP_K):
        rank = jnp.sum(jnp.where(lane == idxs[kk], before, 0.0), axis=-1, keepdims=True)
        ri = jnp.where(lane == kk, idxs[kk], ri)
        ri = jnp.where(lane == TOP_K + kk, rank, ri)
        gt = jnp.where(lane == kk, es[kk] / denom, gt)
    ri_ref[...] = ri.T[:2 * TOP_K].astype(I32)
    gate_ref[...] = gt
    cnt_ref[...] = cnt_ref[...] + jnp.sum(hot, axis=0, keepdims=True)

    h_hi, h_lo = _split_bf16(h2)
    logit_ref[step % 2] = (_dot(h_hi, wr_hi_ref[...]) + _dot(h_hi, wr_lo_ref[...]) + _dot(h_lo, wr_hi_ref[...])
                           + br_ref[...])


def _dest_kernel(start_ref, ri_ref, dest_ref):
    idx = ri_ref[:TOP_K, :]
    dest = ri_ref[TOP_K:, :]
    for e in range(start_ref.shape[0]):
        dest = dest + jnp.where(idx == e, start_ref[e], 0)
    dest_ref[...] = dest


def _sc_mesh():
    info = plsc.get_sparse_core_info()
    mesh = plsc.VectorSubcoreMesh(core_axis_name="c", subcore_axis_name="s")
    return mesh, info.num_cores, info.num_subcores, info.num_lanes


def _sc_scatter_rows(rows, idx, n_out):
    mesh, nc, ns, lanes = _sc_mesh()
    nw = nc * ns
    planes, m, width = rows.shape
    ch = SC_CHUNK
    n_ch = m // ch // nw
    half = planes // 2
    assert planes % 2 == 0 and n_ch * ch * nw == m and idx.shape == (m // ch * TOP_K, ch) and n_ch >= 2
    assert n_ch % 8 == 0

    @functools.partial(
        pl.kernel, mesh=mesh, out_type=jax.ShapeDtypeStruct((planes * n_out, width), rows.dtype),
        scratch_types=([pltpu.VMEM((n_ch * TOP_K, ch), I32)] + [pltpu.VMEM((ch, width), rows.dtype)] * planes
                       + [pltpu.VMEM((TOP_K, ch), I32)] * planes
                       + [pltpu.SemaphoreType.DMA((planes,)), pltpu.SemaphoreType.DMA((planes,))]))
    def scatter_kernel(rows_hbm, idx_hbm, out_hbm, idx_v, *rest):
        bufs, ibufs, rsem, ssem = rest[:planes], rest[planes:2 * planes], rest[2 * planes], rest[2 * planes + 1]
        wid = lax.axis_index("s") * nc + lax.axis_index("c")
        for kk in range(TOP_K):
            pltpu.sync_copy(idx_hbm.at[pl.ds(kk * (m // ch) + wid * n_ch, n_ch)], idx_v.at[pl.ds(kk * n_ch, n_ch)])
        base = wid * n_ch * ch

        def read(j, b):
            return pltpu.make_async_copy(rows_hbm.at[b, pl.ds(pl.multiple_of(base + j * ch, ch), ch)], bufs[b], rsem.at[b])

        def scatters(b):
            return [pltpu.make_async_copy(bufs[b], out_hbm.at[ibufs[b].at[kk]], ssem.at[b]) for kk in range(TOP_K)]

        def start_scatters(j, b):
            for kk in range(TOP_K):
                for t in range(0, ch, lanes):
                    ibufs[b][kk, pl.ds(t, lanes)] = idx_v[kk * n_ch + j, pl.ds(t, lanes)] + b * n_out
            for c in scatters(b):
                c.start()

        def wait_scatters(b):
            for c in scatters(b):
                c.wait()

        def finish(j, b):
            pj, pb = (j, b - half) if b >= half else (j - 1, b + half)
            read(pj, pb).wait()
            start_scatters(pj, pb)

        for b in range(planes):
            read(0, b).start()
        for b in range(half, planes):
            finish(0, b)

        @pl.loop(1, n_ch)
        def _(j):
            for b in range(planes):
                wait_scatters(b)
                read(j, b).start()
                finish(j, b)

        for b in range(half):
            finish(n_ch, b)
        for b in range(planes):
            wait_scatters(b)

    return scatter_kernel(rows, idx).reshape(planes, n_out, width)


def _padfill_kernel(start_ref, len_ref, xs_in_ref, xs_ref, zeros_ref, sem):
    del xs_in_ref
    zeros_ref[...] = jnp.zeros_like(zeros_ref)
    bits = [1 << b for b in reversed(range(EXPERT_BLOCK.bit_length() - 1))]

    def pieces(e):
        n = len_ref[e]
        for bit in bits:
            row0 = start_ref[e] + (n & ~(2 * bit - 1))
            copy = pltpu.make_async_copy(zeros_ref.at[:, pl.ds(0, bit)], xs_ref.at[:, pl.ds(row0, bit)], sem)
            yield (n & bit) != 0, copy

    def start(e, c):
        for on, copy in pieces(e):
            pl.when(on)(copy.start)
        return c

    def wait(e, c):
        for on, copy in pieces(e):
            pl.when(on)(copy.wait)
        return c

    lax.fori_loop(0, start_ref.shape[0], start, 0)
    lax.fori_loop(0, start_ref.shape[0], wait, 0)


def _experts_kernel(be_ref, nb_ref, valid_ref, slot_ref, next_ref, xs_ref, wgu_hbm, bgu_ref, wd_hbm, bd_ref, ys_ref,
                    wgu_buf, wd_buf, wgu_bf_ref, wd_bf_ref, gu_sem, d_sem):
    del nb_ref
    blk = pl.program_id(0)
    de = wd_hbm.shape[1]
    valid = valid_ref[blk]
    expert = be_ref[blk]
    slot = slot_ref[blk]

    def weight_copies(e, s):
        return (pltpu.make_async_copy(wgu_hbm.at[e], wgu_buf.at[s], gu_sem.at[s]),
                pltpu.make_async_copy(wd_hbm.at[e], wd_buf.at[s], d_sem.at[s]))

    first_of_expert = jnp.logical_or(blk == 0, expert != be_ref[jnp.maximum(blk - 1, 0)])

    @pl.when(jnp.logical_and(valid > 0, first_of_expert))
    def _():
        @pl.when(blk == 0)
        def _():
            for c in weight_copies(expert, slot):
                c.start()

        for c in weight_copies(expert, slot):
            c.wait()
        wgu_bf_ref[...] = wgu_buf[slot].astype(BF16)
        wd_bf_ref[...] = wd_buf[slot].astype(BF16)

        @pl.when(next_ref[blk] >= 0)
        def _():
            for c in weight_copies(next_ref[blk], 1 - slot):
                c.start()

    def mlp(m):
        x = _unpack_rows(xs_ref.at[:, pl.ds(0, m)]).astype(BF16)
        gu = _dot(x, wgu_bf_ref[...]) + bgu_ref[0]
        glu = jnp.minimum(gu[:, :de], SWIGLU_LIMIT)
        lin = jnp.clip(gu[:, de:], -SWIGLU_LIMIT, SWIGLU_LIMIT)
        act = glu * jax.nn.sigmoid(SWIGLU_ALPHA * glu) * (lin + 1.0)
        y = _dot(act.astype(BF16), wd_bf_ref[...]) + bd_ref[0]
        _pack_rows(y, ys_ref.at[:, pl.ds(0, m)])
        if m < EXPERT_BLOCK:
            ys_ref[:, m:, :] = jnp.zeros((PACK_ROWS, EXPERT_BLOCK - m, LANES), U32)

    step = EXPERT_BLOCK // EXPERT_SPLIT
    for q in range(EXPERT_SPLIT):
        pl.when(jnp.logical_and(valid > q * step, valid <= (q + 1) * step))(functools.partial(mlp, (q + 1) * step))

    @pl.when(valid == 0)
    def _():
        ys_ref[...] = jnp.zeros_like(ys_ref)


def _sc_gather_rows(table, idx):
    mesh, nc, ns, lanes = _sc_mesh()
    nw = nc * ns
    planes, n_tab, width = table.shape
    n_idx_rows, ch = idx.shape
    n_ch = n_idx_rows // nw
    m = n_idx_rows * ch
    half = planes // 2
    assert planes % 2 == 0 and ch == SC_CHUNK and n_ch * nw == n_idx_rows and n_ch >= 2

    @functools.partial(
        pl.kernel, mesh=mesh, out_type=jax.ShapeDtypeStruct((planes, m, width), table.dtype),
        scratch_types=([pltpu.VMEM((n_ch, ch), I32)] + [pltpu.VMEM((ch, width), table.dtype)] * planes
                       + [pltpu.VMEM((8, ch), I32)] * planes
                       + [pltpu.SemaphoreType.DMA((planes,)), pltpu.SemaphoreType.DMA((planes,))]))
    def gather_kernel(table_hbm, idx_hbm, out_hbm, idx_v, *rest):
        bufs, ibufs, gsem, wsem = rest[:planes], rest[planes:2 * planes], rest[2 * planes], rest[2 * planes + 1]
        wid = lax.axis_index("s") * nc + lax.axis_index("c")
        pltpu.sync_copy(idx_hbm.at[pl.ds(wid * n_ch, n_ch)], idx_v)
        base = wid * n_ch * ch

        def gather(b):
            return pltpu.make_async_copy(table_hbm.at[ibufs[b].at[0]], bufs[b], gsem.at[b])

        def start_gather(j, b):
            for t in range(0, ch, lanes):
                ibufs[b][0, pl.ds(t, lanes)] = idx_v[j, pl.ds(t, lanes)] + b * n_tab
            gather(b).start()

        def write(j, b):
            return pltpu.make_async_copy(bufs[b], out_hbm.at[b, pl.ds(pl.multiple_of(base + j * ch, ch), ch)], wsem.at[b])

        def finish(j, b):
            pj, pb = (j, b - half) if b >= half else (j - 1, b + half)
            gather(pb).wait()
            write(pj, pb).start()

        for b in range(planes):
            start_gather(0, b)
        for b in range(half, planes):
            finish(0, b)

        @pl.loop(1, n_ch)
        def _(j):
            for b in range(planes):
                write(j - 1, b).wait()
                start_gather(j, b)
                finish(j, b)

        for b in range(half):
            finish(n_ch, b)
        for b in range(planes):
            write(n_ch - 1, b).wait()

    return gather_kernel(table.reshape(planes * n_tab, width), idx)


def _combine_kernel(x1_ref, gate_ref, y0_ref, y1_ref, y2_ref, y3_ref, *rest):
    o_ref = rest[-1]
    gate = gate_ref[...]
    out = x1_ref[...]
    for kk, y_ref in enumerate((y0_ref, y1_ref, y2_ref, y3_ref)):
        out = out + gate[:, kk:kk + 1] * _unpack_rows(y_ref)
    o_ref[...] = out


def _const_spec(shape):
    nd = len(shape)
    return pl.BlockSpec(shape, lambda *_: (0,) * nd)


def _layer(x, norm1_g, w_in, q_norm_g, k_norm_g, w_pool_grp, pool_scale, w_pool_up, w_attn_up, w_out, norm2_g,
           w_router, b_router, w_gate_up, b_gate_up, w_down, b_down):
    B, S, D = x.shape
    N = B * S
    pw = w_pool_up.shape[0]
    sw = w_attn_up.shape[0]
    n_exp = w_router.shape[1]
    de = w_down.shape[1]
    heads = sw // SB_HEAD_DIM
    assert pw == len(POOL_WINDOWS) * POOL_GROUP_DIM and heads % 2 == 0 and n_exp <= LANES
    assert D == 2 * PACK_ROWS * LANES and w_in.shape[1] == pw + 3 * sw + 2 * D
    tm = 512 if S % 512 == 0 else 256
    assert S % tm == 0 and S % (ATTN_SUB * ATTN_BLOCK) == 0
    xf = x.reshape(N, D)
    cparams = functools.partial(pltpu.CompilerParams, vmem_limit_bytes=VMEM_LIMIT)

    hsum = (jnp.arange(sw)[:, None] // SB_HEAD_DIM == jnp.arange(sw)[None, :] // SB_HEAD_DIM).astype(BF16)
    nt = S // tm
    tok_spec = lambda w: pl.BlockSpec((tm, w), lambda b, i: (b * nt + i, 0))
    q2, k2, v2, ga, pg = pl.pallas_call(
        functools.partial(_mixer_in_kernel, tm=tm, pw=pw, sw=sw),
        grid=(B, nt),
        in_specs=[tok_spec(D), _const_spec((1, D)), _const_spec((D, w_in.shape[1])), _const_spec((1, sw)),
                  _const_spec((1, sw)), _const_spec((sw, sw)),
                  _const_spec((len(POOL_WINDOWS), POOL_GROUP_DIM, POOL_GROUP_DIM)), _const_spec((1, pw)),
                  _const_spec((pw, D))],
        out_specs=[tok_spec(sw), tok_spec(sw), tok_spec(sw), tok_spec(D), tok_spec(D)],
        out_shape=[jax.ShapeDtypeStruct((N, sw), BF16)] * 3 + [jax.ShapeDtypeStruct((N, D), BF16)] * 2,
        scratch_shapes=[pltpu.VMEM((POOL_HALO, pw), F32)],
        compiler_params=cparams(dimension_semantics=("arbitrary", "arbitrary")),
        name="mixer_in",
    )(xf, norm1_g.reshape(1, D), w_in.astype(BF16), jnp.tile(q_norm_g, heads).reshape(1, sw),
      jnp.tile(k_norm_g, heads).reshape(1, sw), hsum, w_pool_grp.astype(BF16), pool_scale.reshape(1, pw),
      w_pool_up.astype(BF16))

    bq = ATTN_BLOCK
    rows_q = ATTN_SUB * bq
    nq = S // rows_q
    jj = jnp.arange(bq)
    tri = jnp.concatenate([(jj[:, None] >= jj[None, :]).astype(BF16), jnp.ones((bq, bq), BF16)], axis=1)
    n_pairs = heads // 2
    n_units = ATTN_SUB * n_pairs
    kv_spec = pl.BlockSpec((S, sw), lambda b, qi: (b, 0), pipeline_mode=pl.Buffered(1))
    sba = pl.pallas_call(
        functools.partial(_attn_kernel, n_pairs=n_pairs),
        grid=(B, nq),
        in_specs=[pl.BlockSpec((rows_q, sw), lambda b, qi: (b * nq + qi, 0)), kv_spec, kv_spec,
                  _const_spec((bq, 2 * bq))],
        out_specs=pl.BlockSpec((rows_q, sw), lambda b, qi: (b * nq + qi, 0)),
        out_shape=jax.ShapeDtypeStruct((N, sw), BF16),
        scratch_shapes=([pltpu.VMEM((2 * bq, LANES), BF16)] * n_units + [pltpu.VMEM((2 * bq, LANES), F32)] * (2 * n_units)),
        compiler_params=cparams(dimension_semantics=("arbitrary", "arbitrary")),
        name="sb_attn",
    )(q2, k2, v2, tri)

    wr = jnp.zeros((D, LANES), F32).at[:, :n_exp].set(w_router)
    wr_hi = wr.astype(BF16)
    wr_lo = (wr - wr_hi.astype(F32)).astype(BF16)
    br = jnp.full((1, LANES), -jnp.inf, F32).at[0, :n_exp].set(b_router)
    ltri = (jnp.arange(tm)[:, None] > jnp.arange(tm)[None, :]).astype(BF16)
    assert N % (MOE_GROUPS * tm) == 0
    ng = N // MOE_GROUPS
    steps = ng // tm
    n_assign = ng * TOP_K
    n_blocks = -(-(n_assign + n_exp * (EXPERT_BLOCK - 1)) // EXPERT_BLOCK)
    n_rows = n_blocks * EXPERT_BLOCK
    plane_spec = lambda rows, index: pl.BlockSpec((PACK_ROWS, rows, LANES), lambda i, *_: (0, index(i, *_), 0))
    row_spec = lambda w: pl.BlockSpec((tm, w), lambda i: (i, 0))
    w_attn_up_bf, w_out_bf, g2 = w_attn_up.astype(BF16), w_out.astype(BF16), norm2_g.reshape(1, D)
    bgu, bdn = b_gate_up.reshape(n_exp, 1, 2 * de), b_down.reshape(n_exp, 1, D)
    out = None
    for grp in range(MOE_GROUPS):
        grp_spec = lambda w, first=grp * steps: pl.BlockSpec((tm, w), lambda i: (first + i, 0))

        cur = lambda i: jnp.minimum(i, steps - 1)
        prev = lambda i: jnp.maximum(i - 1, 0)
        in_spec = lambda w, first=grp * steps: pl.BlockSpec((tm, w), lambda i: (first + cur(i), 0))
        x1, hpk, ri, gate, cnt = pl.pallas_call(
            functools.partial(_mixer_out_kernel, tm=tm),
            grid=(steps + 1,),
            in_specs=[in_spec(D), in_spec(D), in_spec(D), in_spec(sw), _const_spec((sw, D)), _const_spec((D, D)),
                      _const_spec((1, D)), _const_spec((D, LANES)), _const_spec((D, LANES)), _const_spec((1, LANES)),
                      _const_spec((tm, tm))],
            out_specs=[pl.BlockSpec((tm, D), lambda i: (cur(i), 0)), plane_spec(tm, cur),
                       pl.BlockSpec((2 * TOP_K, tm), lambda i: (0, prev(i))),
                       pl.BlockSpec((tm, LANES), lambda i: (prev(i), 0)), _const_spec((8, LANES))],
            out_shape=[jax.ShapeDtypeStruct((ng, D), F32), jax.ShapeDtypeStruct((PACK_ROWS, ng, LANES), U32),
                       jax.ShapeDtypeStruct((2 * TOP_K, ng), I32), jax.ShapeDtypeStruct((ng, LANES), F32),
                       jax.ShapeDtypeStruct((8, LANES), F32)],
            scratch_shapes=[pltpu.VMEM((2, tm, LANES), F32)],
            compiler_params=cparams(dimension_semantics=("arbitrary",)),
            name="mixer_out",
        )(xf, pg, ga, sba, w_attn_up_bf, w_out_bf, g2, wr_hi, wr_lo, br, ltri)

        counts = cnt[0, :n_exp].astype(I32)
        padded = (counts + EXPERT_BLOCK - 1) // EXPERT_BLOCK * EXPERT_BLOCK
        padded_end = jnp.cumsum(padded)
        start_pad = padded_end - padded
        td = min(ng, 8192)
        dest = pl.pallas_call(
            _dest_kernel,
            grid_spec=pltpu.PrefetchScalarGridSpec(
                num_scalar_prefetch=1,
                grid=(ng // td,),
                in_specs=[pl.BlockSpec((2 * TOP_K, td), lambda i, sp: (0, i))],
                out_specs=pl.BlockSpec((TOP_K, td), lambda i, sp: (0, i)),
            ),
            out_shape=jax.ShapeDtypeStruct((TOP_K, ng), I32),
            compiler_params=cparams(dimension_semantics=("arbitrary",)),
            name="dest",
        )(start_pad, ri).reshape(-1, SC_CHUNK)
        block_start = jnp.arange(n_blocks, dtype=I32) * EXPERT_BLOCK
        block_expert = jnp.minimum(jnp.sum((padded_end[None, :] <= block_start[:, None]).astype(I32), axis=1),
                                   n_exp - 1)
        n_used = (padded_end[-1] // EXPERT_BLOCK).astype(I32).reshape(1)

        xs = _sc_scatter_rows(hpk, dest, n_rows)
        xs = pl.pallas_call(
            _padfill_kernel,
            grid_spec=pltpu.PrefetchScalarGridSpec(
                num_scalar_prefetch=2,
                grid=(1,),
                in_specs=[pl.BlockSpec(memory_space=pl.ANY)],
                out_specs=pl.BlockSpec(memory_space=pl.ANY),
                scratch_shapes=[pltpu.VMEM((PACK_ROWS, EXPERT_BLOCK // 2, LANES), U32), pltpu.SemaphoreType.DMA(())],
            ),
            out_shape=jax.ShapeDtypeStruct((PACK_ROWS, n_rows, LANES), U32),
            input_output_aliases={2: 0},
            compiler_params=cparams(dimension_semantics=("arbitrary",)),
            name="padfill",
        )(start_pad + counts, padded - counts, xs)

        experts = jnp.arange(n_exp, dtype=I32)
        used = counts > 0
        block_valid = jnp.clip(jnp.sum(jnp.where(block_expert[:, None] == experts[None, :],
                                                 (start_pad + counts)[None, :], 0), axis=1) - block_start,
                               0, EXPERT_BLOCK)
        ordinal = jnp.cumsum(used.astype(I32)) - 1
        next_used = jnp.min(jnp.where(jnp.logical_and(experts[None, :] > experts[:, None], used[None, :]),
                                      experts[None, :], n_exp), axis=1)
        next_used = jnp.where(next_used == n_exp, -1, next_used)
        per_block = lambda table: jnp.sum(jnp.where(block_expert[:, None] == experts[None, :], table[None, :], 0), axis=1)
        b_spec = lambda shape: pl.BlockSpec(shape, lambda i, be, *_: (be[i], 0, 0))
        ys = pl.pallas_call(
            _experts_kernel,
            grid_spec=pltpu.PrefetchScalarGridSpec(
                num_scalar_prefetch=5,
                grid=(n_blocks,),
                in_specs=[plane_spec(EXPERT_BLOCK, lambda i, be, nb, *_: jnp.minimum(i, nb[0] - 1)),
                          pl.BlockSpec(memory_space=pl.ANY), b_spec((1, 1, 2 * de)),
                          pl.BlockSpec(memory_space=pl.ANY), b_spec((1, 1, D))],
                out_specs=plane_spec(EXPERT_BLOCK, lambda i, *_: i),
                scratch_shapes=[pltpu.VMEM((2, D, 2 * de), F32), pltpu.VMEM((2, de, D), F32),
                                pltpu.VMEM((D, 2 * de), BF16), pltpu.VMEM((de, D), BF16),
                                pltpu.SemaphoreType.DMA((2,)), pltpu.SemaphoreType.DMA((2,))],
            ),
            out_shape=jax.ShapeDtypeStruct((PACK_ROWS, n_rows, LANES), U32),
            compiler_params=cparams(dimension_semantics=("arbitrary",)),
            name="experts",
        )(block_expert, n_used, block_valid, per_block(ordinal % 2), per_block(next_used), xs, w_gate_up, bgu, w_down, bdn)

        y4 = _sc_gather_rows(ys, dest.reshape(-1, SC_CHUNK))

        operands = [x1, gate, y4, y4, y4, y4] + ([] if out is None else [out])
        out = pl.pallas_call(
            _combine_kernel,
            grid=(steps,),
            in_specs=([row_spec(D), row_spec(LANES)]
                      + [plane_spec(tm, lambda i, kk=kk: kk * steps + i) for kk in range(TOP_K)]
                      + ([] if out is None else [pl.BlockSpec(memory_space=pl.ANY)])),
            out_specs=grp_spec(D),
            out_shape=jax.ShapeDtypeStruct((N, D), F32),
            input_output_aliases={} if out is None else {len(operands) - 1: 0},
            compiler_params=cparams(dimension_semantics=("arbitrary",)),
            name="combine",
        )(*operands)
    return out.reshape(B, S, D)


def kernel(x, norm1_g, w_in, q_norm_g, k_norm_g, w_pool_grp, pool_scale, w_pool_up, w_attn_up, w_out, norm2_g,
           w_router, b_router, w_gate_up, b_gate_up, w_down, b_down):
    for layer in range(norm1_g.shape[0]):
        x = _layer(x, norm1_g[layer], w_in[layer], q_norm_g[layer], k_norm_g[layer], w_pool_grp[layer],
                   pool_scale[layer], w_pool_up[layer], w_attn_up[layer], w_out[layer], norm2_g[layer],
                   w_router[layer], b_router[layer], w_gate_up[layer], b_gate_up[layer], w_down[layer],
                   b_down[layer])
    return x
```

```python
import functools

import jax
import jax.numpy as jnp
from jax import lax
from jax.experimental import pallas as pl
from jax.experimental.pallas import tpu as pltpu
from jax.experimental.pallas import tpu_sc as plsc

F32 = jnp.float32
BF16 = jnp.bfloat16
U32 = jnp.uint32
I32 = jnp.int32

EPS = 1e-6
POOL_WINDOWS = (2, 4, 8, 16)
POOL_GROUP_DIM = 128
POOL_HALO = 16
SB_HEAD_DIM = 64
TOP_K = 4
SWIGLU_LIMIT = 7.0
SWIGLU_ALPHA = 1.702
EXPERT_BLOCK = 512
EXPERT_SPLIT = 4
LANES = 128
PACK_ROWS = 4
ATTN_BLOCK = 128
ATTN_SUB = 2
ATTN_EXIT_BITS = 70.0
LOG2_E = 1.4426950408889634
VMEM_LIMIT = 56 * 1024 * 1024
SC_CHUNK = 128
MOE_GROUPS = 4


def _dot(a, b):
    return jnp.dot(a, b, preferred_element_type=F32)


def _split_bf16(x):
    hi = x.astype(BF16)
    lo = (x - hi.astype(F32)).astype(BF16)
    return hi, lo


def _pack_rows(v, out_ref):
    half = v.shape[1] // 2
    lo = lax.bitcast_convert_type(v[:, :half].astype(BF16).astype(F32), U32) >> 16
    hi = lax.bitcast_convert_type(v[:, half:].astype(BF16).astype(F32), U32) & jnp.uint32(0xFFFF0000)
    w = lo | hi
    for c in range(PACK_ROWS):
        out_ref[c] = w[:, c * LANES:(c + 1) * LANES]


def _unpack_rows(ref):
    los, his = [], []
    for c in range(PACK_ROWS):
        w = ref[c]
        los.append(lax.bitcast_convert_type(w << 16, F32))
        his.append(lax.bitcast_convert_type(w & jnp.uint32(0xFFFF0000), F32))
    return jnp.concatenate(los + his, axis=1)


def _mixer_in_kernel(x_ref, g1_ref, win_ref, gq_ref, gk_ref, hsum_ref, wgrp_ref, pscale_ref, wpu_ref,
                     q_ref, k_ref, v_ref, ga_ref, p_ref, tail_ref, *, tm, pw, sw):
    i = pl.program_id(1)

    @pl.when(i == 0)
    def _():
        tail_ref[...] = jnp.zeros_like(tail_ref)

    x = x_ref[...]
    ms = jnp.mean(x * x, axis=-1, keepdims=True)
    h = (x * lax.rsqrt(ms + EPS) * g1_ref[...]).astype(BF16)

    def head_norm(t, gain):
        ss = _dot((t * t).astype(BF16), hsum_ref[...])
        return t * lax.rsqrt(ss * (1.0 / SB_HEAD_DIM) + EPS) * gain

    def project_q():
        q = _dot(h, win_ref[:, pw:pw + sw])
        q_ref[...] = (head_norm(q, gq_ref[...]) * (SB_HEAD_DIM ** -0.5 * LOG2_E)).astype(BF16)

    def project_k():
        k = _dot(h, win_ref[:, pw + sw:pw + 2 * sw])
        k_ref[...] = head_norm(k, gk_ref[...]).astype(BF16)

    def project_v():
        v_ref[...] = _dot(h, win_ref[:, pw + 2 * sw:pw + 3 * sw]).astype(BF16)

    d_model = x.shape[1]

    def project_attn_gate():
        g_attn = _dot(h, win_ref[:, pw + 3 * sw + d_model:pw + 3 * sw + 2 * d_model])
        ga_ref[...] = jax.nn.sigmoid(g_attn).astype(BF16)

    u = _dot(h, win_ref[:, 0:pw])
    xx = jnp.concatenate([tail_ref[...], u], axis=0)
    tail_ref[...] = u[tm - POOL_HALO:, :]
    pos = i * tm + lax.broadcasted_iota(I32, (tm, POOL_GROUP_DIM), 0)
    mixed = []
    for (g, w), project in zip(enumerate(POOL_WINDOWS), (project_q, project_k, project_v, project_attn_gate)):
        project()
        s = xx[:, g * POOL_GROUP_DIM:(g + 1) * POOL_GROUP_DIM]
        step = 1
        while step < w:
            s = s + pltpu.roll(s, step, axis=0)
            step *= 2
        count = jnp.minimum(pos + 1, w).astype(F32)
        ug = u[:, g * POOL_GROUP_DIM:(g + 1) * POOL_GROUP_DIM]
        d = s[POOL_HALO:, :] / count - ug
        mixed.append(_dot(d.astype(BF16), wgrp_ref[g]))
    pm = jnp.concatenate(mixed, axis=1) * pscale_ref[...]
    pool_out = _dot(pm.astype(BF16), wpu_ref[...])
    g_pool = _dot(h, win_ref[:, pw + 3 * sw:pw + 3 * sw + d_model])
    p_ref[...] = (jax.nn.sigmoid(g_pool) * pool_out).astype(BF16)


def _attn_kernel(q_ref, k_ref, v_ref, tri_ref, o_ref, *scratch, n_pairs):
    units = [(sub, p) for sub in range(ATTN_SUB) for p in range(n_pairs)]
    n_units = len(units)
    qs, acc, rr = scratch[:n_units], scratch[n_units:2 * n_units], scratch[2 * n_units:]
    bq = ATTN_BLOCK
    first_block = pl.program_id(1) * ATTN_SUB
    first_head = lax.broadcasted_iota(I32, (bq, LANES), 1) < SB_HEAD_DIM
    for u, (sub, p) in enumerate(units):
        q2 = q_ref[sub * bq:(sub + 1) * bq, p * LANES:(p + 1) * LANES]
        qs[u][:bq] = jnp.where(first_head, q2, jnp.zeros_like(q2))
        qs[u][bq:] = jnp.where(first_head, jnp.zeros_like(q2), q2)
    row = lax.broadcasted_iota(I32, (2 * bq, bq), 0)
    col = lax.broadcasted_iota(I32, (2 * bq, bq), 1)
    causal = col < (row & (bq - 1))
    contract_last = (((1,), (1,)), ((), ()))

    def softplus(z):
        return jnp.maximum(z, 0.0) + jnp.log2(1.0 + jnp.exp2(-jnp.abs(z)))

    def suffix_sums(sp):
        return _dot(sp.astype(BF16), tri_ref[...])

    def cols(ref, block, p):
        start = pl.multiple_of(jnp.maximum(block, 0) * bq, bq)
        return ref[pl.ds(start, bq), p * LANES:(p + 1) * LANES]

    def scores(u, block):
        return lax.dot_general(qs[u][...], cols(k_ref, block, units[u][1]), contract_last, preferred_element_type=F32)

    diag = [first_block + sub for sub, _ in units]
    z_d, z_n, s_d, s_n = {}, {}, {}, {}

    def stage_scores(u):
        z_d[u] = scores(u, diag[u])
        z_n[u] = scores(u, diag[u] - 1)

    def stage_sums(u):
        s_d[u] = suffix_sums(jnp.where(causal, softplus(z_d[u]), 0.0))
        s_n[u] = suffix_sums(jnp.where(diag[u] >= 1, softplus(z_n[u]), 0.0))

    def stage_values(u):
        p = units[u][1]
        a_d = jnp.where(causal, jnp.exp2(z_d[u] - s_d[u][:, :bq]), 0.0)
        r_d = s_d[u][:, bq:]
        a_n = jnp.where(diag[u] >= 1, jnp.exp2(z_n[u] - (r_d + s_n[u][:, :bq])), 0.0)
        acc[u][...] = (_dot(a_d.astype(BF16), cols(v_ref, diag[u], p)) + _dot(a_n.astype(BF16), cols(v_ref, diag[u] - 1, p)))
        rr[u][...] = r_d + s_n[u][:, bq:]

    for t in range(n_units + 2):
        if t < n_units:
            stage_scores(t)
        if 0 <= t - 1 < n_units:
            stage_sums(t - 1)
        if 0 <= t - 2 < n_units:
            stage_values(t - 2)

    def r_min():
        m = rr[0][...]
        for u in range(1, n_units):
            m = jnp.minimum(m, rr[u][...])
        return jnp.min(m)

    def cond(c):
        back, rm = c
        return jnp.logical_and(diag[-1] - back >= 0, rm < ATTN_EXIT_BITS)

    def body(c):
        back, _ = c
        blocks = [d - back for d in diag]
        zs = [scores(u, blocks[u]) for u in range(n_units)]
        ss = [suffix_sums(jnp.where(blocks[u] >= 0, softplus(zs[u]), 0.0)) for u in range(n_units)]
        for u, (_, p) in enumerate(units):
            r = rr[u][...]
            a = jnp.where(blocks[u] >= 0, jnp.exp2(zs[u] - (r + ss[u][:, :bq])), 0.0)
            acc[u][...] += _dot(a.astype(BF16), cols(v_ref, blocks[u], p))
            rr[u][...] = r + ss[u][:, bq:]
        return back + 1, r_min()

    lax.while_loop(cond, body, (2, r_min()))
    for u, (sub, p) in enumerate(units):
        o_ref[sub * bq:(sub + 1) * bq, p * LANES:(p + 1) * LANES] = (
            jnp.where(first_head, acc[u][:bq], acc[u][bq:]).astype(BF16))


def _mixer_out_kernel(x_ref, p_ref, ga_ref, sba_ref, wau_ref, wout_ref, g2_ref, wr_hi_ref, wr_lo_ref, br_ref,
                      ltri_ref, x1_ref, hp_ref, ri_ref, gate_ref, cnt_ref, logit_ref, *, tm):
    step = pl.program_id(0)

    @pl.when(step == 0)
    def _():
        cnt_ref[...] = jnp.zeros_like(cnt_ref)
        logit_ref[1] = jnp.zeros((tm, LANES), F32)

    logits = logit_ref[(step + 1) % 2]
    routed = jnp.where(step >= 1, 1.0, 0.0)
    lane = lax.broadcasted_iota(I32, logits.shape, 1).astype(F32)
    work = logits
    vals, idxs = [], []

    def topk_round():
        nonlocal work
        m = jnp.max(work, axis=-1, keepdims=True)
        ik = jnp.min(jnp.where(work == m, lane, float(LANES)), axis=-1, keepdims=True)
        vals.append(m)
        idxs.append(ik)
        work = jnp.where(lane == ik, -jnp.inf, work)

    d_model = x_ref.shape[1]
    quarter = d_model // TOP_K
    topk_round()
    attn_out = _dot(sba_ref[...], wau_ref[...])
    merged = (p_ref[...].astype(F32) + ga_ref[...].astype(F32) * attn_out).astype(BF16)
    x1_parts = []
    for c in range(TOP_K):
        if c >= 1:
            topk_round()
        cols = slice(c * quarter, (c + 1) * quarter)
        x1_parts.append(x_ref[:, cols] + _dot(merged, wout_ref[:, cols]))
    x1 = jnp.concatenate(x1_parts, axis=1)
    x1_ref[...] = x1

    es = [jnp.exp(v - vals[0]) for v in vals]
    denom = es[0] + es[1] + es[2] + es[3]
    hot = jnp.zeros(logits.shape, F32)
    for ik in idxs:
        hot = hot + jnp.where(lane == ik, routed, 0.0)
    before = _dot(ltri_ref[...], hot.astype(BF16)) + cnt_ref[0:1, :]

    ms = jnp.mean(x1 * x1, axis=-1, keepdims=True)
    h2 = x1 * lax.rsqrt(ms + EPS) * g2_ref[...]
    _pack_rows(h2, hp_ref)

    ri = jnp.zeros(logits.shape, F32)
    gt = jnp.zeros(logits.shape, F32)
    for kk in range(TOP_K):
        rank = jnp.sum(jnp.where(lane == idxs[kk], before, 0.0), axis=-1, keepdims=True)
        ri = jnp.where(lane == kk, idxs[kk], ri)
        ri = jnp.where(lane == TOP_K + kk, rank, ri)
        gt = jnp.where(lane == kk, es[kk] / denom, gt)
    ri_ref[...] = ri.T[:2 * TOP_K].astype(I32)
    gate_ref[...] = gt
    cnt_ref[...] = cnt_ref[...] + jnp.sum(hot, axis=0, keepdims=True)

    h_hi, h_lo = _split_bf16(h2)
    logit_ref[step % 2] = (_dot(h_hi, wr_hi_ref[...]) + _dot(h_hi, wr_lo_ref[...]) + _dot(h_lo, wr_hi_ref[...])
                           + br_ref[...])


def _dest_kernel(start_ref, ri_ref, dest_ref):
    idx = ri_ref[:TOP_K, :]
    dest = ri_ref[TOP_K:, :]
    for e in range(start_ref.shape[0]):
        dest = dest + jnp.where(idx == e, start_ref[e], 0)
    dest_ref[...] = dest


def _sc_mesh():
    info = plsc.get_sparse_core_info()
    mesh = plsc.VectorSubcoreMesh(core_axis_name="c", subcore_axis_name="s")
    return mesh, info.num_cores, info.num_subcores, info.num_lanes


def _sc_scatter_rows(rows, idx, n_out):
    mesh, nc, ns, lanes = _sc_mesh()
    nw = nc * ns
    planes, m, width = rows.shape
    ch = SC_CHUNK
    n_ch = m // ch // nw
    half = planes // 2
    assert planes % 2 == 0 and n_ch * ch * nw == m and idx.shape == (m // ch * TOP_K, ch) and n_ch >= 2
    n_cp = max(n_ch, 8)
    assert n_cp % n_ch == 0 and n_cp % 8 == 0 and (m // ch) % n_cp == 0

    @functools.partial(
        pl.kernel, mesh=mesh, out_type=jax.ShapeDtypeStruct((planes * n_out, width), rows.dtype),
        scratch_types=([pltpu.VMEM((n_cp * TOP_K, ch), I32)] + [pltpu.VMEM((ch, width), rows.dtype)] * planes
                       + [pltpu.VMEM((TOP_K, ch), I32)] * planes
                       + [pltpu.SemaphoreType.DMA((planes,)), pltpu.SemaphoreType.DMA((planes,))]))
    def scatter_kernel(rows_hbm, idx_hbm, out_hbm, idx_v, *rest):
        bufs, ibufs, rsem, ssem = rest[:planes], rest[planes:2 * planes], rest[2 * planes], rest[2 * planes + 1]
        wid = lax.axis_index("s") * nc + lax.axis_index("c")
        first_row = wid * n_ch // n_cp * n_cp
        in_group = wid * n_ch - first_row
        for kk in range(TOP_K):
            pltpu.sync_copy(idx_hbm.at[pl.ds(pl.multiple_of(kk * (m // ch) + first_row, 8), n_cp)],
                            idx_v.at[pl.ds(kk * n_cp, n_cp)])
        base = wid * n_ch * ch

        def read(j, b):
            return pltpu.make_async_copy(rows_hbm.at[b, pl.ds(pl.multiple_of(base + j * ch, ch), ch)], bufs[b], rsem.at[b])

        def scatters(b):
            return [pltpu.make_async_copy(bufs[b], out_hbm.at[ibufs[b].at[kk]], ssem.at[b]) for kk in range(TOP_K)]

        def start_scatters(j, b):
            for kk in range(TOP_K):
                for t in range(0, ch, lanes):
                    ibufs[b][kk, pl.ds(t, lanes)] = idx_v[kk * n_cp + in_group + j, pl.ds(t, lanes)] + b * n_out
            for c in scatters(b):
                c.start()

        def wait_scatters(b):
            for c in scatters(b):
                c.wait()

        def finish(j, b):
            pj, pb = (j, b - half) if b >= half else (j - 1, b + half)
            read(pj, pb).wait()
            start_scatters(pj, pb)

        for b in range(planes):
            read(0, b).start()
        for b in range(half, planes):
            finish(0, b)

        @pl.loop(1, n_ch)
        def _(j):
            for b in range(planes):
                wait_scatters(b)
                read(j, b).start()
                finish(j, b)

        for b in range(half):
            finish(n_ch, b)
        for b in range(planes):
            wait_scatters(b)

    return scatter_kernel(rows, idx).reshape(planes, n_out, width)


def _padfill_kernel(start_ref, len_ref, xs_in_ref, xs_ref, zeros_ref, sem):
    del xs_in_ref
    zeros_ref[...] = jnp.zeros_like(zeros_ref)
    bits = [1 << b for b in reversed(range(EXPERT_BLOCK.bit_length() - 1))]

    def pieces(e):
        n = len_ref[e]
        for bit in bits:
            row0 = start_ref[e] + (n & ~(2 * bit - 1))
            copy = pltpu.make_async_copy(zeros_ref.at[:, pl.ds(0, bit)], xs_ref.at[:, pl.ds(row0, bit)], sem)
            yield (n & bit) != 0, copy

    def start(e, c):
        for on, copy in pieces(e):
            pl.when(on)(copy.start)
        return c

    def wait(e, c):
        for on, copy in pieces(e):
            pl.when(on)(copy.wait)
        return c

    lax.fori_loop(0, start_ref.shape[0], start, 0)
    lax.fori_loop(0, start_ref.shape[0], wait, 0)


def _experts_kernel(be_ref, nb_ref, valid_ref, slot_ref, next_ref, xs_ref, wgu_hbm, bgu_ref, wd_hbm, bd_ref, ys_ref,
                    wgu_buf, wd_buf, wgu_bf_ref, wd_bf_ref, gu_sem, d_sem):
    del nb_ref
    blk = pl.program_id(0)
    de = wd_hbm.shape[1]
    valid = valid_ref[blk]
    expert = be_ref[blk]
    slot = slot_ref[blk]

    def weight_copies(e, s):
        return (pltpu.make_async_copy(wgu_hbm.at[e], wgu_buf.at[s], gu_sem.at[s]),
                pltpu.make_async_copy(wd_hbm.at[e], wd_buf.at[s], d_sem.at[s]))

    first_of_expert = jnp.logical_or(blk == 0, expert != be_ref[jnp.maximum(blk - 1, 0)])

    @pl.when(jnp.logical_and(valid > 0, first_of_expert))
    def _():
        @pl.when(blk == 0)
        def _():
            for c in weight_copies(expert, slot):
                c.start()

        for c in weight_copies(expert, slot):
            c.wait()
        wgu_bf_ref[...] = wgu_buf[slot].astype(BF16)
        wd_bf_ref[...] = wd_buf[slot].astype(BF16)

        @pl.when(next_ref[blk] >= 0)
        def _():
            for c in weight_copies(next_ref[blk], 1 - slot):
                c.start()

    def mlp(m):
        x = _unpack_rows(xs_ref.at[:, pl.ds(0, m)]).astype(BF16)
        gu = _dot(x, wgu_bf_ref[...]) + bgu_ref[0]
        glu = jnp.minimum(gu[:, :de], SWIGLU_LIMIT)
        lin = jnp.clip(gu[:, de:], -SWIGLU_LIMIT, SWIGLU_LIMIT)
        act = glu * jax.nn.sigmoid(SWIGLU_ALPHA * glu) * (lin + 1.0)
        y = _dot(act.astype(BF16), wd_bf_ref[...]) + bd_ref[0]
        _pack_rows(y, ys_ref.at[:, pl.ds(0, m)])
        if m < EXPERT_BLOCK:
            ys_ref[:, m:, :] = jnp.zeros((PACK_ROWS, EXPERT_BLOCK - m, LANES), U32)

    step = EXPERT_BLOCK // EXPERT_SPLIT
    for q in range(EXPERT_SPLIT):
        pl.when(jnp.logical_and(valid > q * step, valid <= (q + 1) * step))(functools.partial(mlp, (q + 1) * step))

    @pl.when(valid == 0)
    def _():
        ys_ref[...] = jnp.zeros_like(ys_ref)


def _sc_gather_rows(table, idx):
    mesh, nc, ns, lanes = _sc_mesh()
    nw = nc * ns
    planes, n_tab, width = table.shape
    n_idx_rows, ch = idx.shape
    n_ch = n_idx_rows // nw
    m = n_idx_rows * ch
    half = planes // 2
    assert planes % 2 == 0 and ch == SC_CHUNK and n_ch * nw == n_idx_rows and n_ch >= 2

    @functools.partial(
        pl.kernel, mesh=mesh, out_type=jax.ShapeDtypeStruct((planes, m, width), table.dtype),
        scratch_types=([pltpu.VMEM((n_ch, ch), I32)] + [pltpu.VMEM((ch, width), table.dtype)] * planes
                       + [pltpu.VMEM((8, ch), I32)] * planes
                       + [pltpu.SemaphoreType.DMA((planes,)), pltpu.SemaphoreType.DMA((planes,))]))
    def gather_kernel(table_hbm, idx_hbm, out_hbm, idx_v, *rest):
        bufs, ibufs, gsem, wsem = rest[:planes], rest[planes:2 * planes], rest[2 * planes], rest[2 * planes + 1]
        wid = lax.axis_index("s") * nc + lax.axis_index("c")
        pltpu.sync_copy(idx_hbm.at[pl.ds(wid * n_ch, n_ch)], idx_v)
        base = wid * n_ch * ch

        def gather(b):
            return pltpu.make_async_copy(table_hbm.at[ibufs[b].at[0]], bufs[b], gsem.at[b])

        def start_gather(j, b):
            for t in range(0, ch, lanes):
                ibufs[b][0, pl.ds(t, lanes)] = idx_v[j, pl.ds(t, lanes)] + b * n_tab
            gather(b).start()

        def write(j, b):
            return pltpu.make_async_copy(bufs[b], out_hbm.at[b, pl.ds(pl.multiple_of(base + j * ch, ch), ch)], wsem.at[b])

        def finish(j, b):
            pj, pb = (j, b - half) if b >= half else (j - 1, b + half)
            gather(pb).wait()
            write(pj, pb).start()

        for b in range(planes):
            start_gather(0, b)
        for b in range(half, planes):
            finish(0, b)

        @pl.loop(1, n_ch)
        def _(j):
            for b in range(planes):
                write(j - 1, b).wait()
                start_gather(j, b)
                finish(j, b)

        for b in range(half):
            finish(n_ch, b)
        for b in range(planes):
            write(n_ch - 1, b).wait()

    return gather_kernel(table.reshape(planes * n_tab, width), idx)


def _combine_kernel(x1_ref, gate_ref, y0_ref, y1_ref, y2_ref, y3_ref, *rest):
    o_ref = rest[-1]
    gate = gate_ref[...]
    out = x1_ref[...]
    for kk, y_ref in enumerate((y0_ref, y1_ref, y2_ref, y3_ref)):
        out = out + gate[:, kk:kk + 1] * _unpack_rows(y_ref)
    o_ref[...] = out


def _const_spec(shape):
    nd = len(shape)
    return pl.BlockSpec(shape, lambda *_: (0,) * nd)


def _layer(x, norm1_g, w_in, q_norm_g, k_norm_g, w_pool_grp, pool_scale, w_pool_up, w_attn_up, w_out, norm2_g,
           w_router, b_router, w_gate_up, b_gate_up, w_down, b_down):
    B, S, D = x.shape
    N = B * S
    pw = w_pool_up.shape[0]
    sw = w_attn_up.shape[0]
    n_exp = w_router.shape[1]
    de = w_down.shape[1]
    heads = sw // SB_HEAD_DIM
    assert pw == len(POOL_WINDOWS) * POOL_GROUP_DIM and heads % 2 == 0 and n_exp <= LANES
    assert D == 2 * PACK_ROWS * LANES and w_in.shape[1] == pw + 3 * sw + 2 * D
    tm = 512 if S % 512 == 0 else 256
    assert S % tm == 0 and S % (ATTN_SUB * ATTN_BLOCK) == 0
    xf = x.reshape(N, D)
    cparams = functools.partial(pltpu.CompilerParams, vmem_limit_bytes=VMEM_LIMIT)

    hsum = (jnp.arange(sw)[:, None] // SB_HEAD_DIM == jnp.arange(sw)[None, :] // SB_HEAD_DIM).astype(BF16)
    nt = S // tm
    tok_spec = lambda w: pl.BlockSpec((tm, w), lambda b, i: (b * nt + i, 0))
    q2, k2, v2, ga, pg = pl.pallas_call(
        functools.partial(_mixer_in_kernel, tm=tm, pw=pw, sw=sw),
        grid=(B, nt),
        in_specs=[tok_spec(D), _const_spec((1, D)), _const_spec((D, w_in.shape[1])), _const_spec((1, sw)),
                  _const_spec((1, sw)), _const_spec((sw, sw)),
                  _const_spec((len(POOL_WINDOWS), POOL_GROUP_DIM, POOL_GROUP_DIM)), _const_spec((1, pw)),
                  _const_spec((pw, D))],
        out_specs=[tok_spec(sw), tok_spec(sw), tok_spec(sw), tok_spec(D), tok_spec(D)],
        out_shape=[jax.ShapeDtypeStruct((N, sw), BF16)] * 3 + [jax.ShapeDtypeStruct((N, D), BF16)] * 2,
        scratch_shapes=[pltpu.VMEM((POOL_HALO, pw), F32)],
        compiler_params=cparams(dimension_semantics=("arbitrary", "arbitrary")),
        name="mixer_in",
    )(xf, norm1_g.reshape(1, D), w_in.astype(BF16), jnp.tile(q_norm_g, heads).reshape(1, sw),
      jnp.tile(k_norm_g, heads).reshape(1, sw), hsum, w_pool_grp.astype(BF16), pool_scale.reshape(1, pw),
      w_pool_up.astype(BF16))

    bq = ATTN_BLOCK
    rows_q = ATTN_SUB * bq
    nq = S // rows_q
    jj = jnp.arange(bq)
    tri = jnp.concatenate([(jj[:, None] >= jj[None, :]).astype(BF16), jnp.ones((bq, bq), BF16)], axis=1)
    n_pairs = heads // 2
    n_units = ATTN_SUB * n_pairs
    kv_spec = pl.BlockSpec((S, sw), lambda b, qi: (b, 0), pipeline_mode=pl.Buffered(1))
    sba = pl.pallas_call(
        functools.partial(_attn_kernel, n_pairs=n_pairs),
        grid=(B, nq),
        in_specs=[pl.BlockSpec((rows_q, sw), lambda b, qi: (b * nq + qi, 0)), kv_spec, kv_spec,
                  _const_spec((bq, 2 * bq))],
        out_specs=pl.BlockSpec((rows_q, sw), lambda b, qi: (b * nq + qi, 0)),
        out_shape=jax.ShapeDtypeStruct((N, sw), BF16),
        scratch_shapes=([pltpu.VMEM((2 * bq, LANES), BF16)] * n_units + [pltpu.VMEM((2 * bq, LANES), F32)] * (2 * n_units)),
        compiler_params=cparams(dimension_semantics=("arbitrary", "arbitrary")),
        name="sb_attn",
    )(q2, k2, v2, tri)

    wr = jnp.zeros((D, LANES), F32).at[:, :n_exp].set(w_router)
    wr_hi = wr.astype(BF16)
    wr_lo = (wr - wr_hi.astype(F32)).astype(BF16)
    br = jnp.full((1, LANES), -jnp.inf, F32).at[0, :n_exp].set(b_router)
    ltri = (jnp.arange(tm)[:, None] > jnp.arange(tm)[None, :]).astype(BF16)
    assert N % (MOE_GROUPS * tm) == 0
    ng = N // MOE_GROUPS
    steps = ng // tm
    n_assign = ng * TOP_K
    n_blocks = -(-(n_assign + n_exp * (EXPERT_BLOCK - 1)) // EXPERT_BLOCK)
    n_rows = n_blocks * EXPERT_BLOCK
    plane_spec = lambda rows, index: pl.BlockSpec((PACK_ROWS, rows, LANES), lambda i, *_: (0, index(i, *_), 0))
    row_spec = lambda w: pl.BlockSpec((tm, w), lambda i: (i, 0))
    w_attn_up_bf, w_out_bf, g2 = w_attn_up.astype(BF16), w_out.astype(BF16), norm2_g.reshape(1, D)
    bgu, bdn = b_gate_up.reshape(n_exp, 1, 2 * de), b_down.reshape(n_exp, 1, D)
    out = None
    for grp in range(MOE_GROUPS):
        grp_spec = lambda w, first=grp * steps: pl.BlockSpec((tm, w), lambda i: (first + i, 0))

        cur = lambda i: jnp.minimum(i, steps - 1)
        prev = lambda i: jnp.maximum(i - 1, 0)
        in_spec = lambda w, first=grp * steps: pl.BlockSpec((tm, w), lambda i: (first + cur(i), 0))
        x1, hpk, ri, gate, cnt = pl.pallas_call(
            functools.partial(_mixer_out_kernel, tm=tm),
            grid=(steps + 1,),
            in_specs=[in_spec(D), in_spec(D), in_spec(D), in_spec(sw), _const_spec((sw, D)), _const_spec((D, D)),
                      _const_spec((1, D)), _const_spec((D, LANES)), _const_spec((D, LANES)), _const_spec((1, LANES)),
                      _const_spec((tm, tm))],
            out_specs=[pl.BlockSpec((tm, D), lambda i: (cur(i), 0)), plane_spec(tm, cur),
                       pl.BlockSpec((2 * TOP_K, tm), lambda i: (0, prev(i))),
                       pl.BlockSpec((tm, LANES), lambda i: (prev(i), 0)), _const_spec((8, LANES))],
            out_shape=[jax.ShapeDtypeStruct((ng, D), F32), jax.ShapeDtypeStruct((PACK_ROWS, ng, LANES), U32),
                       jax.ShapeDtypeStruct((2 * TOP_K, ng), I32), jax.ShapeDtypeStruct((ng, LANES), F32),
                       jax.ShapeDtypeStruct((8, LANES), F32)],
            scratch_shapes=[pltpu.VMEM((2, tm, LANES), F32)],
            compiler_params=cparams(dimension_semantics=("arbitrary",)),
            name="mixer_out",
        )(xf, pg, ga, sba, w_attn_up_bf, w_out_bf, g2, wr_hi, wr_lo, br, ltri)

        counts = cnt[0, :n_exp].astype(I32)
        padded = (counts + EXPERT_BLOCK - 1) // EXPERT_BLOCK * EXPERT_BLOCK
        padded_end = jnp.cumsum(padded)
        start_pad = padded_end - padded
        td = min(ng, 8192)
        dest = pl.pallas_call(
            _dest_kernel,
            grid_spec=pltpu.PrefetchScalarGridSpec(
                num_scalar_prefetch=1,
                grid=(ng // td,),
                in_specs=[pl.BlockSpec((2 * TOP_K, td), lambda i, sp: (0, i))],
                out_specs=pl.BlockSpec((TOP_K, td), lambda i, sp: (0, i)),
            ),
            out_shape=jax.ShapeDtypeStruct((TOP_K, ng), I32),
            compiler_params=cparams(dimension_semantics=("arbitrary",)),
            name="dest",
        )(start_pad, ri).reshape(-1, SC_CHUNK)
        block_start = jnp.arange(n_blocks, dtype=I32) * EXPERT_BLOCK
        block_expert = jnp.minimum(jnp.sum((padded_end[None, :] <= block_start[:, None]).astype(I32), axis=1),
                                   n_exp - 1)
        n_used = (padded_end[-1] // EXPERT_BLOCK).astype(I32).reshape(1)

        xs = _sc_scatter_rows(hpk, dest, n_rows)
        xs = pl.pallas_call(
            _padfill_kernel,
            grid_spec=pltpu.PrefetchScalarGridSpec(
                num_scalar_prefetch=2,
                grid=(1,),
                in_specs=[pl.BlockSpec(memory_space=pl.ANY)],
                out_specs=pl.BlockSpec(memory_space=pl.ANY),
                scratch_shapes=[pltpu.VMEM((PACK_ROWS, EXPERT_BLOCK // 2, LANES), U32), pltpu.SemaphoreType.DMA(())],
            ),
            out_shape=jax.ShapeDtypeStruct((PACK_ROWS, n_rows, LANES), U32),
            input_output_aliases={2: 0},
            compiler_params=cparams(dimension_semantics=("arbitrary",)),
            name="padfill",
        )(start_pad + counts, padded - counts, xs)

        experts = jnp.arange(n_exp, dtype=I32)
        used = counts > 0
        block_valid = jnp.clip(jnp.sum(jnp.where(block_expert[:, None] == experts[None, :],
                                                 (start_pad + counts)[None, :], 0), axis=1) - block_start,
                               0, EXPERT_BLOCK)
        ordinal = jnp.cumsum(used.astype(I32)) - 1
        next_used = jnp.min(jnp.where(jnp.logical_and(experts[None, :] > experts[:, None], used[None, :]),
                                      experts[None, :], n_exp), axis=1)
        next_used = jnp.where(next_used == n_exp, -1, next_used)
        per_block = lambda table: jnp.sum(jnp.where(block_expert[:, None] == experts[None, :], table[None, :], 0), axis=1)
        b_spec = lambda shape: pl.BlockSpec(shape, lambda i, be, *_: (be[i], 0, 0))
        ys = pl.pallas_call(
            _experts_kernel,
            grid_spec=pltpu.PrefetchScalarGridSpec(
                num_scalar_prefetch=5,
                grid=(n_blocks,),
                in_specs=[plane_spec(EXPERT_BLOCK, lambda i, be, nb, *_: jnp.minimum(i, nb[0] - 1)),
                          pl.BlockSpec(memory_space=pl.ANY), b_spec((1, 1, 2 * de)),
                          pl.BlockSpec(memory_space=pl.ANY), b_spec((1, 1, D))],
                out_specs=plane_spec(EXPERT_BLOCK, lambda i, *_: i),
                scratch_shapes=[pltpu.VMEM((2, D, 2 * de), F32), pltpu.VMEM((2, de, D), F32),
                                pltpu.VMEM((D, 2 * de), BF16), pltpu.VMEM((de, D), BF16),
                                pltpu.SemaphoreType.DMA((2,)), pltpu.SemaphoreType.DMA((2,))],
            ),
            out_shape=jax.ShapeDtypeStruct((PACK_ROWS, n_rows, LANES), U32),
            compiler_params=cparams(dimension_semantics=("arbitrary",)),
            name="experts",
        )(block_expert, n_used, block_valid, per_block(ordinal % 2), per_block(next_used), xs, w_gate_up, bgu, w_down, bdn)

        y4 = _sc_gather_rows(ys, dest.reshape(-1, SC_CHUNK))

        operands = [x1, gate, y4, y4, y4, y4] + ([] if out is None else [out])
        out = pl.pallas_call(
            _combine_kernel,
            grid=(steps,),
            in_specs=([row_spec(D), row_spec(LANES)]
                      + [plane_spec(tm, lambda i, kk=kk: kk * steps + i) for kk in range(TOP_K)]
                      + ([] if out is None else [pl.BlockSpec(memory_space=pl.ANY)])),
            out_specs=grp_spec(D),
            out_shape=jax.ShapeDtypeStruct((N, D), F32),
            input_output_aliases={} if out is None else {len(operands) - 1: 0},
            compiler_params=cparams(dimension_semantics=("arbitrary",)),
            name="combine",
        )(*operands)
    return out.reshape(B, S, D)


def kernel(x, norm1_g, w_in, q_norm_g, k_norm_g, w_pool_grp, pool_scale, w_pool_up, w_attn_up, w_out, norm2_g,
           w_router, b_router, w_gate_up, b_gate_up, w_down, b_down):
    for layer in range(norm1_g.shape[0]):
        x = _layer(x, norm1_g[layer], w_in[layer], q_norm_g[layer], k_norm_g[layer], w_pool_grp[layer],
                   pool_scale[layer], w_pool_up[layer], w_attn_up[layer], w_out[layer], norm2_g[layer],
                   w_router[layer], b_router[layer], w_gate_up[layer], b_gate_up[layer], w_down[layer],
                   b_down[layer])
    return x
```

```python
import functools

import jax
import jax.numpy as jnp
from jax import lax
from jax.experimental import pallas as pl
from jax.experimental.pallas import tpu as pltpu
from jax.experimental.pallas import tpu_sc as plsc

F32 = jnp.float32
BF16 = jnp.bfloat16
U32 = jnp.uint32
I32 = jnp.int32

EPS = 1e-6
POOL_WINDOWS = (2, 4, 8, 16)
POOL_GROUP_DIM = 128
POOL_HALO = 16
SB_HEAD_DIM = 64
TOP_K = 4
SWIGLU_LIMIT = 7.0
SWIGLU_ALPHA = 1.702
EXPERT_BLOCK = 512
EXPERT_SPLIT = 4
LANES = 128
PACK_ROWS = 4
ATTN_BLOCK = 128
ATTN_SUB = 2
ATTN_EXIT_BITS = 70.0
LOG2_E = 1.4426950408889634
VMEM_LIMIT = 56 * 1024 * 1024
SC_CHUNK = 128
MOE_GROUPS = 2


def _dot(a, b):
    return jnp.dot(a, b, preferred_element_type=F32)


def _split_bf16(x):
    hi = x.astype(BF16)
    lo = (x - hi.astype(F32)).astype(BF16)
    return hi, lo


def _pack_rows(v, out_ref):
    half = v.shape[1] // 2
    lo = lax.bitcast_convert_type(v[:, :half].astype(BF16).astype(F32), U32) >> 16
    hi = lax.bitcast_convert_type(v[:, half:].astype(BF16).astype(F32), U32) & jnp.uint32(0xFFFF0000)
    w = lo | hi
    for c in range(PACK_ROWS):
        out_ref[c] = w[:, c * LANES:(c + 1) * LANES]


def _unpack_rows(ref):
    los, his = [], []
    for c in range(PACK_ROWS):
        w = ref[c]
        los.append(lax.bitcast_convert_type(w << 16, F32))
        his.append(lax.bitcast_convert_type(w & jnp.uint32(0xFFFF0000), F32))
    return jnp.concatenate(los + his, axis=1)


def _mixer_in_kernel(x_ref, g1_ref, win_ref, gq_ref, gk_ref, hsum_ref, wgrp_ref, pscale_ref, wpu_ref,
                     q_ref, k_ref, v_ref, ga_ref, p_ref, tail_ref, *, tm, pw, sw):
    i = pl.program_id(1)

    @pl.when(i == 0)
    def _():
        tail_ref[...] = jnp.zeros_like(tail_ref)

    x = x_ref[...]
    ms = jnp.mean(x * x, axis=-1, keepdims=True)
    h = (x * lax.rsqrt(ms + EPS) * g1_ref[...]).astype(BF16)

    def head_norm(t, gain):
        ss = _dot((t * t).astype(BF16), hsum_ref[...])
        return t * lax.rsqrt(ss * (1.0 / SB_HEAD_DIM) + EPS) * gain

    def project_q():
        q = _dot(h, win_ref[:, pw:pw + sw])
        q_ref[...] = (head_norm(q, gq_ref[...]) * (SB_HEAD_DIM ** -0.5 * LOG2_E)).astype(BF16)

    def project_k():
        k = _dot(h, win_ref[:, pw + sw:pw + 2 * sw])
        k_ref[...] = head_norm(k, gk_ref[...]).astype(BF16)

    def project_v():
        v_ref[...] = _dot(h, win_ref[:, pw + 2 * sw:pw + 3 * sw]).astype(BF16)

    d_model = x.shape[1]

    def project_attn_gate():
        g_attn = _dot(h, win_ref[:, pw + 3 * sw + d_model:pw + 3 * sw + 2 * d_model])
        ga_ref[...] = jax.nn.sigmoid(g_attn).astype(BF16)

    u = _dot(h, win_ref[:, 0:pw])
    xx = jnp.concatenate([tail_ref[...], u], axis=0)
    tail_ref[...] = u[tm - POOL_HALO:, :]
    pos = i * tm + lax.broadcasted_iota(I32, (tm, POOL_GROUP_DIM), 0)
    mixed = []
    for (g, w), project in zip(enumerate(POOL_WINDOWS), (project_q, project_k, project_v, project_attn_gate)):
        project()
        s = xx[:, g * POOL_GROUP_DIM:(g + 1) * POOL_GROUP_DIM]
        step = 1
        while step < w:
            s = s + pltpu.roll(s, step, axis=0)
            step *= 2
        count = jnp.minimum(pos + 1, w).astype(F32)
        ug = u[:, g * POOL_GROUP_DIM:(g + 1) * POOL_GROUP_DIM]
        d = s[POOL_HALO:, :] / count - ug
        mixed.append(_dot(d.astype(BF16), wgrp_ref[g]))
    pm = jnp.concatenate(mixed, axis=1) * pscale_ref[...]
    pool_out = _dot(pm.astype(BF16), wpu_ref[...])
    g_pool = _dot(h, win_ref[:, pw + 3 * sw:pw + 3 * sw + d_model])
    p_ref[...] = (jax.nn.sigmoid(g_pool) * pool_out).astype(BF16)


def _attn_kernel(q_ref, k_hbm, v_hbm, tri_ref, o_ref, kbuf, vbuf, ksem, vsem, *scratch, n_pairs):
    units = [(sub, p) for sub in range(ATTN_SUB) for p in range(n_pairs)]
    n_units = len(units)
    qs, acc, rr = scratch[:n_units], scratch[n_units:2 * n_units], scratch[2 * n_units:]
    bq = ATTN_BLOCK
    batch = pl.program_id(0)
    seq = kbuf.shape[1]

    def kv_copies(b):
        rows = pl.ds(pl.multiple_of(b * seq, seq), seq)
        return (pltpu.make_async_copy(k_hbm.at[rows], kbuf.at[b % 2], ksem.at[b % 2]),
                pltpu.make_async_copy(v_hbm.at[rows], vbuf.at[b % 2], vsem.at[b % 2]))

    @pl.when(pl.program_id(1) == 0)
    def _():
        @pl.when(batch == 0)
        def _():
            for c in kv_copies(batch):
                c.start()

        for c in kv_copies(batch):
            c.wait()

        @pl.when(batch + 1 < pl.num_programs(0))
        def _():
            for c in kv_copies(batch + 1):
                c.start()

    k_ref = kbuf.at[batch % 2]
    v_ref = vbuf.at[batch % 2]
    first_block = pl.program_id(1) * ATTN_SUB
    first_head = lax.broadcasted_iota(I32, (bq, LANES), 1) < SB_HEAD_DIM
    for u, (sub, p) in enumerate(units):
        q2 = q_ref[sub * bq:(sub + 1) * bq, p * LANES:(p + 1) * LANES]
        qs[u][:bq] = jnp.where(first_head, q2, jnp.zeros_like(q2))
        qs[u][bq:] = jnp.where(first_head, jnp.zeros_like(q2), q2)
    row = lax.broadcasted_iota(I32, (2 * bq, bq), 0)
    col = lax.broadcasted_iota(I32, (2 * bq, bq), 1)
    causal = col < (row & (bq - 1))
    contract_last = (((1,), (1,)), ((), ()))

    def softplus(z):
        return jnp.maximum(z, 0.0) + jnp.log2(1.0 + jnp.exp2(-jnp.abs(z)))

    def suffix_sums(sp):
        return _dot(sp.astype(BF16), tri_ref[...])

    def cols(ref, block, p):
        start = pl.multiple_of(jnp.maximum(block, 0) * bq, bq)
        return ref[pl.ds(start, bq), p * LANES:(p + 1) * LANES]

    def scores(u, block):
        return lax.dot_general(qs[u][...], cols(k_ref, block, units[u][1]), contract_last, preferred_element_type=F32)

    diag = [first_block + sub for sub, _ in units]
    z_d, z_n, s_d, s_n = {}, {}, {}, {}

    def stage_scores(u):
        z_d[u] = scores(u, diag[u])
        z_n[u] = scores(u, diag[u] - 1)

    def stage_sums(u):
        s_d[u] = suffix_sums(jnp.where(causal, softplus(z_d[u]), 0.0))
        s_n[u] = suffix_sums(jnp.where(diag[u] >= 1, softplus(z_n[u]), 0.0))

    def stage_values(u):
        p = units[u][1]
        a_d = jnp.where(causal, jnp.exp2(z_d[u] - s_d[u][:, :bq]), 0.0)
        r_d = s_d[u][:, bq:]
        a_n = jnp.where(diag[u] >= 1, jnp.exp2(z_n[u] - (r_d + s_n[u][:, :bq])), 0.0)
        acc[u][...] = (_dot(a_d.astype(BF16), cols(v_ref, diag[u], p)) + _dot(a_n.astype(BF16), cols(v_ref, diag[u] - 1, p)))
        rr[u][...] = r_d + s_n[u][:, bq:]

    for t in range(n_units + 2):
        if t < n_units:
            stage_scores(t)
        if 0 <= t - 1 < n_units:
            stage_sums(t - 1)
        if 0 <= t - 2 < n_units:
            stage_values(t - 2)

    def r_min():
        m = rr[0][...]
        for u in range(1, n_units):
            m = jnp.minimum(m, rr[u][...])
        return jnp.min(m)

    def cond(c):
        back, rm = c
        return jnp.logical_and(diag[-1] - back >= 0, rm < ATTN_EXIT_BITS)

    def body(c):
        back, _ = c
        blocks = [d - back for d in diag]
        zs = [scores(u, blocks[u]) for u in range(n_units)]
        ss = [suffix_sums(jnp.where(blocks[u] >= 0, softplus(zs[u]), 0.0)) for u in range(n_units)]
        for u, (_, p) in enumerate(units):
            r = rr[u][...]
            a = jnp.where(blocks[u] >= 0, jnp.exp2(zs[u] - (r + ss[u][:, :bq])), 0.0)
            acc[u][...] += _dot(a.astype(BF16), cols(v_ref, blocks[u], p))
            rr[u][...] = r + ss[u][:, bq:]
        return back + 1, r_min()

    lax.while_loop(cond, body, (2, r_min()))
    for u, (sub, p) in enumerate(units):
        o_ref[sub * bq:(sub + 1) * bq, p * LANES:(p + 1) * LANES] = (
            jnp.where(first_head, acc[u][:bq], acc[u][bq:]).astype(BF16))


def _mixer_out_kernel(x_ref, p_ref, ga_ref, sba_ref, wau_ref, wout_ref, g2_ref, wr_hi_ref, wr_lo_ref, br_ref,
                      ltri_ref, x1_ref, hp_ref, ri_ref, gate_ref, cnt_ref, logit_ref, *, tm):
    step = pl.program_id(0)

    @pl.when(step == 0)
    def _():
        cnt_ref[...] = jnp.zeros_like(cnt_ref)
        logit_ref[1] = jnp.zeros((tm, LANES), F32)

    logits = logit_ref[(step + 1) % 2]
    routed = jnp.where(step >= 1, 1.0, 0.0)
    lane = lax.broadcasted_iota(I32, logits.shape, 1).astype(F32)
    work = logits
    vals, idxs = [], []

    def topk_round():
        nonlocal work
        m = jnp.max(work, axis=-1, keepdims=True)
        ik = jnp.min(jnp.where(work == m, lane, float(LANES)), axis=-1, keepdims=True)
        vals.append(m)
        idxs.append(ik)
        work = jnp.where(lane == ik, -jnp.inf, work)

    d_model = x_ref.shape[1]
    quarter = d_model // TOP_K
    topk_round()
    attn_out = _dot(sba_ref[...], wau_ref[...])
    merged = (p_ref[...].astype(F32) + ga_ref[...].astype(F32) * attn_out).astype(BF16)
    x1_parts = []
    for c in range(TOP_K):
        if c >= 1:
            topk_round()
        cols = slice(c * quarter, (c + 1) * quarter)
        x1_parts.append(x_ref[:, cols] + _dot(merged, wout_ref[:, cols]))
    x1 = jnp.concatenate(x1_parts, axis=1)
    x1_ref[...] = x1

    es = [jnp.exp(v - vals[0]) for v in vals]
    denom = es[0] + es[1] + es[2] + es[3]
    hot = jnp.zeros(logits.shape, F32)
    for ik in idxs:
        hot = hot + jnp.where(lane == ik, routed, 0.0)
    before = _dot(ltri_ref[...], hot.astype(BF16)) + cnt_ref[0:1, :]

    ms = jnp.mean(x1 * x1, axis=-1, keepdims=True)
    h2 = x1 * lax.rsqrt(ms + EPS) * g2_ref[...]
    _pack_rows(h2, hp_ref)

    ri = jnp.zeros(logits.shape, F32)
    gt = jnp.zeros(logits.shape, F32)
    for kk in range(TOP_K):
        rank = jnp.sum(jnp.where(lane == idxs[kk], before, 0.0), axis=-1, keepdims=True)
        ri = jnp.where(lane == kk, idxs[kk], ri)
        ri = jnp.where(lane == TOP_K + kk, rank, ri)
        gt = jnp.where(lane == kk, es[kk] / denom, gt)
    ri_ref[...] = ri.T[:2 * TOP_K].astype(I32)
    gate_ref[...] = gt
    cnt_ref[...] = cnt_ref[...] + jnp.sum(hot, axis=0, keepdims=True)

    h_hi, h_lo = _split_bf16(h2)
    logit_ref[step % 2] = (_dot(h_hi, wr_hi_ref[...]) + _dot(h_hi, wr_lo_ref[...]) + _dot(h_lo, wr_hi_ref[...])
                           + br_ref[...])


def _dest_kernel(start_ref, ri_ref, dest_ref):
    idx = ri_ref[:TOP_K, :]
    dest = ri_ref[TOP_K:, :]
    for e in range(start_ref.shape[0]):
        dest = dest + jnp.where(idx == e, start_ref[e], 0)
    dest_ref[...] = dest


def _sc_mesh():
    info = plsc.get_sparse_core_info()
    mesh = plsc.VectorSubcoreMesh(core_axis_name="c", subcore_axis_name="s")
    return mesh, info.num_cores, info.num_subcores, info.num_lanes


def _sc_scatter_rows(rows, idx, n_out):
    mesh, nc, ns, lanes = _sc_mesh()
    nw = nc * ns
    planes, m, width = rows.shape
    ch = SC_CHUNK
    n_ch = m // ch // nw
    half = planes // 2
    assert planes % 2 == 0 and n_ch * ch * nw == m and idx.shape == (m // ch * TOP_K, ch) and n_ch >= 2
    assert n_ch % 8 == 0

    @functools.partial(
        pl.kernel, mesh=mesh, out_type=jax.ShapeDtypeStruct((planes * n_out, width), rows.dtype),
        scratch_types=([pltpu.VMEM((n_ch * TOP_K, ch), I32)] + [pltpu.VMEM((ch, width), rows.dtype)] * planes
                       + [pltpu.VMEM((TOP_K, ch), I32)] * planes
                       + [pltpu.SemaphoreType.DMA((planes,)), pltpu.SemaphoreType.DMA((planes,))]))
    def scatter_kernel(rows_hbm, idx_hbm, out_hbm, idx_v, *rest):
        bufs, ibufs, rsem, ssem = rest[:planes], rest[planes:2 * planes], rest[2 * planes], rest[2 * planes + 1]
        wid = lax.axis_index("s") * nc + lax.axis_index("c")
        for kk in range(TOP_K):
            pltpu.sync_copy(idx_hbm.at[pl.ds(kk * (m // ch) + wid * n_ch, n_ch)], idx_v.at[pl.ds(kk * n_ch, n_ch)])
        base = wid * n_ch * ch

        def read(j, b):
            return pltpu.make_async_copy(rows_hbm.at[b, pl.ds(pl.multiple_of(base + j * ch, ch), ch)], bufs[b], rsem.at[b])

        def scatters(b):
            return [pltpu.make_async_copy(bufs[b], out_hbm.at[ibufs[b].at[kk]], ssem.at[b]) for kk in range(TOP_K)]

        def start_scatters(j, b):
            for kk in range(TOP_K):
                for t in range(0, ch, lanes):
                    ibufs[b][kk, pl.ds(t, lanes)] = idx_v[kk * n_ch + j, pl.ds(t, lanes)] + b * n_out
            for c in scatters(b):
                c.start()

        def wait_scatters(b):
            for c in scatters(b):
                c.wait()

        def finish(j, b):
            pj, pb = (j, b - half) if b >= half else (j - 1, b + half)
            read(pj, pb).wait()
            start_scatters(pj, pb)

        for b in range(planes):
            read(0, b).start()
        for b in range(half, planes):
            finish(0, b)

        @pl.loop(1, n_ch)
        def _(j):
            for b in range(planes):
                wait_scatters(b)
                read(j, b).start()
                finish(j, b)

        for b in range(half):
            finish(n_ch, b)
        for b in range(planes):
            wait_scatters(b)

    return scatter_kernel(rows, idx).reshape(planes, n_out, width)


def _padfill_kernel(start_ref, len_ref, xs_in_ref, xs_ref, zeros_ref, sem):
    del xs_in_ref
    zeros_ref[...] = jnp.zeros_like(zeros_ref)
    bits = [1 << b for b in reversed(range(EXPERT_BLOCK.bit_length() - 1))]

    def pieces(e):
        n = len_ref[e]
        for bit in bits:
            row0 = start_ref[e] + (n & ~(2 * bit - 1))
            copy = pltpu.make_async_copy(zeros_ref.at[:, pl.ds(0, bit)], xs_ref.at[:, pl.ds(row0, bit)], sem)
            yield (n & bit) != 0, copy

    def start(e, c):
        for on, copy in pieces(e):
            pl.when(on)(copy.start)
        return c

    def wait(e, c):
        for on, copy in pieces(e):
            pl.when(on)(copy.wait)
        return c

    lax.fori_loop(0, start_ref.shape[0], start, 0)
    lax.fori_loop(0, start_ref.shape[0], wait, 0)


def _experts_kernel(be_ref, nb_ref, valid_ref, slot_ref, next_ref, xs_ref, wgu_hbm, bgu_ref, wd_hbm, bd_ref, ys_ref,
                    wgu_buf, wd_buf, wgu_bf_ref, wd_bf_ref, gu_sem, d_sem):
    del nb_ref
    blk = pl.program_id(0)
    de = wd_hbm.shape[1]
    valid = valid_ref[blk]
    expert = be_ref[blk]
    slot = slot_ref[blk]

    def weight_copies(e, s):
        return (pltpu.make_async_copy(wgu_hbm.at[e], wgu_buf.at[s], gu_sem.at[s]),
                pltpu.make_async_copy(wd_hbm.at[e], wd_buf.at[s], d_sem.at[s]))

    first_of_expert = jnp.logical_or(blk == 0, expert != be_ref[jnp.maximum(blk - 1, 0)])

    @pl.when(jnp.logical_and(valid > 0, first_of_expert))
    def _():
        @pl.when(blk == 0)
        def _():
            for c in weight_copies(expert, slot):
                c.start()

        for c in weight_copies(expert, slot):
            c.wait()
        wgu_bf_ref[...] = wgu_buf[slot].astype(BF16)
        wd_bf_ref[...] = wd_buf[slot].astype(BF16)

        @pl.when(next_ref[blk] >= 0)
        def _():
            for c in weight_copies(next_ref[blk], 1 - slot):
                c.start()

    def mlp(m):
        x = _unpack_rows(xs_ref.at[:, pl.ds(0, m)]).astype(BF16)
        gu = _dot(x, wgu_bf_ref[...]) + bgu_ref[0]
        glu = jnp.minimum(gu[:, :de], SWIGLU_LIMIT)
        lin = jnp.clip(gu[:, de:], -SWIGLU_LIMIT, SWIGLU_LIMIT)
        act = glu * jax.nn.sigmoid(SWIGLU_ALPHA * glu) * (lin + 1.0)
        y = _dot(act.astype(BF16), wd_bf_ref[...]) + bd_ref[0]
        _pack_rows(y, ys_ref.at[:, pl.ds(0, m)])
        if m < EXPERT_BLOCK:
            ys_ref[:, m:, :] = jnp.zeros((PACK_ROWS, EXPERT_BLOCK - m, LANES), U32)

    step = EXPERT_BLOCK // EXPERT_SPLIT
    for q in range(EXPERT_SPLIT):
        pl.when(jnp.logical_and(valid > q * step, valid <= (q + 1) * step))(functools.partial(mlp, (q + 1) * step))

    @pl.when(valid == 0)
    def _():
        ys_ref[...] = jnp.zeros_like(ys_ref)


def _sc_gather_rows(table, idx):
    mesh, nc, ns, lanes = _sc_mesh()
    nw = nc * ns
    planes, n_tab, width = table.shape
    n_idx_rows, ch = idx.shape
    n_ch = n_idx_rows // nw
    m = n_idx_rows * ch
    half = planes // 2
    assert planes % 2 == 0 and ch == SC_CHUNK and n_ch * nw == n_idx_rows and n_ch >= 2

    @functools.partial(
        pl.kernel, mesh=mesh, out_type=jax.ShapeDtypeStruct((planes, m, width), table.dtype),
        scratch_types=([pltpu.VMEM((n_ch, ch), I32)] + [pltpu.VMEM((ch, width), table.dtype)] * planes
                       + [pltpu.VMEM((8, ch), I32)] * planes
                       + [pltpu.SemaphoreType.DMA((planes,)), pltpu.SemaphoreType.DMA((planes,))]))
    def gather_kernel(table_hbm, idx_hbm, out_hbm, idx_v, *rest):
        bufs, ibufs, gsem, wsem = rest[:planes], rest[planes:2 * planes], rest[2 * planes], rest[2 * planes + 1]
        wid = lax.axis_index("s") * nc + lax.axis_index("c")
        pltpu.sync_copy(idx_hbm.at[pl.ds(wid * n_ch, n_ch)], idx_v)
        base = wid * n_ch * ch

        def gather(b):
            return pltpu.make_async_copy(table_hbm.at[ibufs[b].at[0]], bufs[b], gsem.at[b])

        def start_gather(j, b):
            for t in range(0, ch, lanes):
                ibufs[b][0, pl.ds(t, lanes)] = idx_v[j, pl.ds(t, lanes)] + b * n_tab
            gather(b).start()

        def write(j, b):
            return pltpu.make_async_copy(bufs[b], out_hbm.at[b, pl.ds(pl.multiple_of(base + j * ch, ch), ch)], wsem.at[b])

        def finish(j, b):
            pj, pb = (j, b - half) if b >= half else (j - 1, b + half)
            gather(pb).wait()
            write(pj, pb).start()

        for b in range(planes):
            start_gather(0, b)
        for b in range(half, planes):
            finish(0, b)

        @pl.loop(1, n_ch)
        def _(j):
            for b in range(planes):
                write(j - 1, b).wait()
                start_gather(j, b)
                finish(j, b)

        for b in range(half):
            finish(n_ch, b)
        for b in range(planes):
            write(n_ch - 1, b).wait()

    return gather_kernel(table.reshape(planes * n_tab, width), idx)


def _combine_kernel(x1_ref, gate_ref, y0_ref, y1_ref, y2_ref, y3_ref, *rest):
    o_ref = rest[-1]
    gate = gate_ref[...]
    out = x1_ref[...]
    for kk, y_ref in enumerate((y0_ref, y1_ref, y2_ref, y3_ref)):
        out = out + gate[:, kk:kk + 1] * _unpack_rows(y_ref)
    o_ref[...] = out


def _const_spec(shape):
    nd = len(shape)
    return pl.BlockSpec(shape, lambda *_: (0,) * nd)


def _layer(x, norm1_g, w_in, q_norm_g, k_norm_g, w_pool_grp, pool_scale, w_pool_up, w_attn_up, w_out, norm2_g,
           w_router, b_router, w_gate_up, b_gate_up, w_down, b_down):
    B, S, D = x.shape
    N = B * S
    pw = w_pool_up.shape[0]
    sw = w_attn_up.shape[0]
    n_exp = w_router.shape[1]
    de = w_down.shape[1]
    heads = sw // SB_HEAD_DIM
    assert pw == len(POOL_WINDOWS) * POOL_GROUP_DIM and heads % 2 == 0 and n_exp <= LANES
    assert D == 2 * PACK_ROWS * LANES and w_in.shape[1] == pw + 3 * sw + 2 * D
    tm = 512 if S % 512 == 0 else 256
    assert S % tm == 0 and S % (ATTN_SUB * ATTN_BLOCK) == 0
    xf = x.reshape(N, D)
    cparams = functools.partial(pltpu.CompilerParams, vmem_limit_bytes=VMEM_LIMIT)

    hsum = (jnp.arange(sw)[:, None] // SB_HEAD_DIM == jnp.arange(sw)[None, :] // SB_HEAD_DIM).astype(BF16)
    nt = S // tm
    tok_spec = lambda w: pl.BlockSpec((tm, w), lambda b, i: (b * nt + i, 0))
    q2, k2, v2, ga, pg = pl.pallas_call(
        functools.partial(_mixer_in_kernel, tm=tm, pw=pw, sw=sw),
        grid=(B, nt),
        in_specs=[tok_spec(D), _const_spec((1, D)), _const_spec((D, w_in.shape[1])), _const_spec((1, sw)),
                  _const_spec((1, sw)), _const_spec((sw, sw)),
                  _const_spec((len(POOL_WINDOWS), POOL_GROUP_DIM, POOL_GROUP_DIM)), _const_spec((1, pw)),
                  _const_spec((pw, D))],
        out_specs=[tok_spec(sw), tok_spec(sw), tok_spec(sw), tok_spec(D), tok_spec(D)],
        out_shape=[jax.ShapeDtypeStruct((N, sw), BF16)] * 3 + [jax.ShapeDtypeStruct((N, D), BF16)] * 2,
        scratch_shapes=[pltpu.VMEM((POOL_HALO, pw), F32)],
        compiler_params=cparams(dimension_semantics=("arbitrary", "arbitrary")),
        name="mixer_in",
    )(xf, norm1_g.reshape(1, D), w_in.astype(BF16), jnp.tile(q_norm_g, heads).reshape(1, sw),
      jnp.tile(k_norm_g, heads).reshape(1, sw), hsum, w_pool_grp.astype(BF16), pool_scale.reshape(1, pw),
      w_pool_up.astype(BF16))

    bq = ATTN_BLOCK
    rows_q = ATTN_SUB * bq
    nq = S // rows_q
    jj = jnp.arange(bq)
    tri = jnp.concatenate([(jj[:, None] >= jj[None, :]).astype(BF16), jnp.ones((bq, bq), BF16)], axis=1)
    n_pairs = heads // 2
    n_units = ATTN_SUB * n_pairs
    kv_spec = pl.BlockSpec(memory_space=pl.ANY)
    sba = pl.pallas_call(
        functools.partial(_attn_kernel, n_pairs=n_pairs),
        grid=(B, nq),
        in_specs=[pl.BlockSpec((rows_q, sw), lambda b, qi: (b * nq + qi, 0)), kv_spec, kv_spec,
                  _const_spec((bq, 2 * bq))],
        out_specs=pl.BlockSpec((rows_q, sw), lambda b, qi: (b * nq + qi, 0)),
        out_shape=jax.ShapeDtypeStruct((N, sw), BF16),
        scratch_shapes=([pltpu.VMEM((2, S, sw), BF16), pltpu.VMEM((2, S, sw), BF16),
                         pltpu.SemaphoreType.DMA((2,)), pltpu.SemaphoreType.DMA((2,))]
                        + [pltpu.VMEM((2 * bq, LANES), BF16)] * n_units + [pltpu.VMEM((2 * bq, LANES), F32)] * (2 * n_units)),
        compiler_params=cparams(dimension_semantics=("arbitrary", "arbitrary")),
        name="sb_attn",
    )(q2, k2, v2, tri)

    wr = jnp.zeros((D, LANES), F32).at[:, :n_exp].set(w_router)
    wr_hi = wr.astype(BF16)
    wr_lo = (wr - wr_hi.astype(F32)).astype(BF16)
    br = jnp.full((1, LANES), -jnp.inf, F32).at[0, :n_exp].set(b_router)
    ltri = (jnp.arange(tm)[:, None] > jnp.arange(tm)[None, :]).astype(BF16)
    assert N % (MOE_GROUPS * tm) == 0
    ng = N // MOE_GROUPS
    steps = ng // tm
    n_assign = ng * TOP_K
    n_blocks = -(-(n_assign + n_exp * (EXPERT_BLOCK - 1)) // EXPERT_BLOCK)
    n_rows = n_blocks * EXPERT_BLOCK
    plane_spec = lambda rows, index: pl.BlockSpec((PACK_ROWS, rows, LANES), lambda i, *_: (0, index(i, *_), 0))
    row_spec = lambda w: pl.BlockSpec((tm, w), lambda i: (i, 0))
    w_attn_up_bf, w_out_bf, g2 = w_attn_up.astype(BF16), w_out.astype(BF16), norm2_g.reshape(1, D)
    bgu, bdn = b_gate_up.reshape(n_exp, 1, 2 * de), b_down.reshape(n_exp, 1, D)
    out = None
    for grp in range(MOE_GROUPS):
        grp_spec = lambda w, first=grp * steps: pl.BlockSpec((tm, w), lambda i: (first + i, 0))

        cur = lambda i: jnp.minimum(i, steps - 1)
        prev = lambda i: jnp.maximum(i - 1, 0)
        in_spec = lambda w, first=grp * steps: pl.BlockSpec((tm, w), lambda i: (first + cur(i), 0))
        x1, hpk, ri, gate, cnt = pl.pallas_call(
            functools.partial(_mixer_out_kernel, tm=tm),
            grid=(steps + 1,),
            in_specs=[in_spec(D), in_spec(D), in_spec(D), in_spec(sw), _const_spec((sw, D)), _const_spec((D, D)),
                      _const_spec((1, D)), _const_spec((D, LANES)), _const_spec((D, LANES)), _const_spec((1, LANES)),
                      _const_spec((tm, tm))],
            out_specs=[pl.BlockSpec((tm, D), lambda i: (cur(i), 0)), plane_spec(tm, cur),
                       pl.BlockSpec((2 * TOP_K, tm), lambda i: (0, prev(i))),
                       pl.BlockSpec((tm, LANES), lambda i: (prev(i), 0)), _const_spec((8, LANES))],
            out_shape=[jax.ShapeDtypeStruct((ng, D), F32), jax.ShapeDtypeStruct((PACK_ROWS, ng, LANES), U32),
                       jax.ShapeDtypeStruct((2 * TOP_K, ng), I32), jax.ShapeDtypeStruct((ng, LANES), F32),
                       jax.ShapeDtypeStruct((8, LANES), F32)],
            scratch_shapes=[pltpu.VMEM((2, tm, LANES), F32)],
            compiler_params=cparams(dimension_semantics=("arbitrary",)),
            name="mixer_out",
        )(xf, pg, ga, sba, w_attn_up_bf, w_out_bf, g2, wr_hi, wr_lo, br, ltri)

        counts = cnt[0, :n_exp].astype(I32)
        padded = (counts + EXPERT_BLOCK - 1) // EXPERT_BLOCK * EXPERT_BLOCK
        padded_end = jnp.cumsum(padded)
        start_pad = padded_end - padded
        td = min(ng, 8192)
        dest = pl.pallas_call(
            _dest_kernel,
            grid_spec=pltpu.PrefetchScalarGridSpec(
                num_scalar_prefetch=1,
                grid=(ng // td,),
                in_specs=[pl.BlockSpec((2 * TOP_K, td), lambda i, sp: (0, i))],
                out_specs=pl.BlockSpec((TOP_K, td), lambda i, sp: (0, i)),
            ),
            out_shape=jax.ShapeDtypeStruct((TOP_K, ng), I32),
            compiler_params=cparams(dimension_semantics=("arbitrary",)),
            name="dest",
        )(start_pad, ri).reshape(-1, SC_CHUNK)
        block_start = jnp.arange(n_blocks, dtype=I32) * EXPERT_BLOCK
        block_expert = jnp.minimum(jnp.sum((padded_end[None, :] <= block_start[:, None]).astype(I32), axis=1),
                                   n_exp - 1)
        n_used = (padded_end[-1] // EXPERT_BLOCK).astype(I32).reshape(1)

        xs = _sc_scatter_rows(hpk, dest, n_rows)
        xs = pl.pallas_call(
            _padfill_kernel,
            grid_spec=pltpu.PrefetchScalarGridSpec(
                num_scalar_prefetch=2,
                grid=(1,),
                in_specs=[pl.BlockSpec(memory_space=pl.ANY)],
                out_specs=pl.BlockSpec(memory_space=pl.ANY),
                scratch_shapes=[pltpu.VMEM((PACK_ROWS, EXPERT_BLOCK // 2, LANES), U32), pltpu.SemaphoreType.DMA(())],
            ),
            out_shape=jax.ShapeDtypeStruct((PACK_ROWS, n_rows, LANES), U32),
            input_output_aliases={2: 0},
            compiler_params=cparams(dimension_semantics=("arbitrary",)),
            name="padfill",
        )(start_pad + counts, padded - counts, xs)

        experts = jnp.arange(n_exp, dtype=I32)
        used = counts > 0
        block_valid = jnp.clip(jnp.sum(jnp.where(block_expert[:, None] == experts[None, :],
                                                 (start_pad + counts)[None, :], 0), axis=1) - block_start,
                               0, EXPERT_BLOCK)
        ordinal = jnp.cumsum(used.astype(I32)) - 1
        next_used = jnp.min(jnp.where(jnp.logical_and(experts[None, :] > experts[:, None], used[None, :]),
                                      experts[None, :], n_exp), axis=1)
        next_used = jnp.where(next_used == n_exp, -1, next_used)
        per_block = lambda table: jnp.sum(jnp.where(block_expert[:, None] == experts[None, :], table[None, :], 0), axis=1)
        b_spec = lambda shape: pl.BlockSpec(shape, lambda i, be, *_: (be[i], 0, 0))
        ys = pl.pallas_call(
            _experts_kernel,
            grid_spec=pltpu.PrefetchScalarGridSpec(
                num_scalar_prefetch=5,
                grid=(n_blocks,),
                in_specs=[plane_spec(EXPERT_BLOCK, lambda i, be, nb, *_: jnp.minimum(i, nb[0] - 1)),
                          pl.BlockSpec(memory_space=pl.ANY), b_spec((1, 1, 2 * de)),
                          pl.BlockSpec(memory_space=pl.ANY), b_spec((1, 1, D))],
                out_specs=plane_spec(EXPERT_BLOCK, lambda i, *_: i),
                scratch_shapes=[pltpu.VMEM((2, D, 2 * de), F32), pltpu.VMEM((2, de, D), F32),
                                pltpu.VMEM((D, 2 * de), BF16), pltpu.VMEM((de, D), BF16),
                                pltpu.SemaphoreType.DMA((2,)), pltpu.SemaphoreType.DMA((2,))],
            ),
            out_shape=jax.ShapeDtypeStruct((PACK_ROWS, n_rows, LANES), U32),
            compiler_params=cparams(dimension_semantics=("arbitrary",)),
            name="experts",
        )(block_expert, n_used, block_valid, per_block(ordinal % 2), per_block(next_used), xs, w_gate_up, bgu, w_down, bdn)

        y4 = _sc_gather_rows(ys, dest.reshape(-1, SC_CHUNK))

        operands = [x1, gate, y4, y4, y4, y4] + ([] if out is None else [out])
        out = pl.pallas_call(
            _combine_kernel,
            grid=(steps,),
            in_specs=([row_spec(D), row_spec(LANES)]
                      + [plane_spec(tm, lambda i, kk=kk: kk * steps + i) for kk in range(TOP_K)]
                      + ([] if out is None else [pl.BlockSpec(memory_space=pl.ANY)])),
            out_specs=grp_spec(D),
            out_shape=jax.ShapeDtypeStruct((N, D), F32),
            input_output_aliases={} if out is None else {len(operands) - 1: 0},
            compiler_params=cparams(dimension_semantics=("arbitrary",)),
            name="combine",
        )(*operands)
    return out.reshape(B, S, D)


def kernel(x, norm1_g, w_in, q_norm_g, k_norm_g, w_pool_grp, pool_scale, w_pool_up, w_attn_up, w_out, norm2_g,
           w_router, b_router, w_gate_up, b_gate_up, w_down, b_down):
    for layer in range(norm1_g.shape[0]):
        x = _layer(x, norm1_g[layer], w_in[layer], q_norm_g[layer], k_norm_g[layer], w_pool_grp[layer],
                   pool_scale[layer], w_pool_up[layer], w_attn_up[layer], w_out[layer], norm2_g[layer],
                   w_router[layer], b_router[layer], w_gate_up[layer], b_gate_up[layer], w_down[layer],
                   b_down[layer])
    return x
```

```python
import functools

import jax
import jax.numpy as jnp
from jax import lax
from jax.experimental import pallas as pl
from jax.experimental.pallas import tpu as pltpu
from jax.experimental.pallas import tpu_sc as plsc

F32 = jnp.float32
BF16 = jnp.bfloat16
U32 = jnp.uint32
I32 = jnp.int32

EPS = 1e-6
POOL_WINDOWS = (2, 4, 8, 16)
POOL_GROUP_DIM = 128
POOL_HALO = 16
SB_HEAD_DIM = 64
TOP_K = 4
SWIGLU_LIMIT = 7.0
SWIGLU_ALPHA = 1.702
EXPERT_BLOCK = 512
EXPERT_SPLIT = 4
LANES = 128
PACK_ROWS = 4
ATTN_BLOCK = 128
ATTN_SUB = 2
ATTN_EXIT_BITS = 70.0
LOG2_E = 1.4426950408889634
VMEM_LIMIT = 56 * 1024 * 1024
SC_CHUNK = 128
MOE_GROUPS = 2


def _dot(a, b):
    return jnp.dot(a, b, preferred_element_type=F32)


def _split_bf16(x):
    hi = x.astype(BF16)
    lo = (x - hi.astype(F32)).astype(BF16)
    return hi, lo


def _pack_rows(v, out_ref):
    half = v.shape[1] // 2
    lo = lax.bitcast_convert_type(v[:, :half].astype(BF16).astype(F32), U32) >> 16
    hi = lax.bitcast_convert_type(v[:, half:].astype(BF16).astype(F32), U32) & jnp.uint32(0xFFFF0000)
    w = lo | hi
    for c in range(PACK_ROWS):
        out_ref[c] = w[:, c * LANES:(c + 1) * LANES]


def _unpack_rows(ref):
    los, his = [], []
    for c in range(PACK_ROWS):
        w = ref[c]
        los.append(lax.bitcast_convert_type(w << 16, F32))
        his.append(lax.bitcast_convert_type(w & jnp.uint32(0xFFFF0000), F32))
    return jnp.concatenate(los + his, axis=1)


def _mixer_in_kernel(x_ref, g1_ref, win_ref, gq_ref, gk_ref, hsum_ref, wgrp_ref, pscale_ref, wpu_ref,
                     q_ref, k_ref, v_ref, p_ref, tail_ref, *, tm, pw, sw):
    i = pl.program_id(1)

    @pl.when(i == 0)
    def _():
        tail_ref[...] = jnp.zeros_like(tail_ref)

    x = x_ref[...]
    ms = jnp.mean(x * x, axis=-1, keepdims=True)
    h = (x * lax.rsqrt(ms + EPS) * g1_ref[...]).astype(BF16)

    def head_norm(t, gain):
        ss = _dot((t * t).astype(BF16), hsum_ref[...])
        return t * lax.rsqrt(ss * (1.0 / SB_HEAD_DIM) + EPS) * gain

    def project_q():
        q = _dot(h, win_ref[:, pw:pw + sw])
        q_ref[...] = (head_norm(q, gq_ref[...]) * (SB_HEAD_DIM ** -0.5 * LOG2_E)).astype(BF16)

    def project_k():
        k = _dot(h, win_ref[:, pw + sw:pw + 2 * sw])
        k_ref[...] = head_norm(k, gk_ref[...]).astype(BF16)

    def project_v():
        v_ref[...] = _dot(h, win_ref[:, pw + 2 * sw:pw + 3 * sw]).astype(BF16)

    d_model = x.shape[1]
    pool_gate = []

    def project_pool_gate():
        pool_gate.append(jax.nn.sigmoid(_dot(h, win_ref[:, pw + 3 * sw:pw + 3 * sw + d_model])))

    u = _dot(h, win_ref[:, 0:pw])
    xx = jnp.concatenate([tail_ref[...], u], axis=0)
    tail_ref[...] = u[tm - POOL_HALO:, :]
    pos = i * tm + lax.broadcasted_iota(I32, (tm, POOL_GROUP_DIM), 0)
    mixed = []
    for (g, w), project in zip(enumerate(POOL_WINDOWS), (project_q, project_k, project_v, project_pool_gate)):
        project()
        s = xx[:, g * POOL_GROUP_DIM:(g + 1) * POOL_GROUP_DIM]
        step = 1
        while step < w:
            s = s + pltpu.roll(s, step, axis=0)
            step *= 2
        count = jnp.minimum(pos + 1, w).astype(F32)
        ug = u[:, g * POOL_GROUP_DIM:(g + 1) * POOL_GROUP_DIM]
        d = s[POOL_HALO:, :] / count - ug
        mixed.append(_dot(d.astype(BF16), wgrp_ref[g]))
    pm = jnp.concatenate(mixed, axis=1) * pscale_ref[...]
    pool_out = _dot(pm.astype(BF16), wpu_ref[...])
    p_ref[...] = (pool_gate[0] * pool_out).astype(BF16)


def _attn_kernel(q_ref, k_hbm, v_hbm, tri_ref, o_ref, kbuf, vbuf, ksem, vsem, *scratch, n_pairs):
    units = [(sub, p) for sub in range(ATTN_SUB) for p in range(n_pairs)]
    n_units = len(units)
    qs, acc, rr = scratch[:n_units], scratch[n_units:2 * n_units], scratch[2 * n_units:]
    bq = ATTN_BLOCK
    batch = pl.program_id(0)
    seq = kbuf.shape[1]

    def kv_copies(b):
        rows = pl.ds(pl.multiple_of(b * seq, seq), seq)
        return (pltpu.make_async_copy(k_hbm.at[rows], kbuf.at[b % 2], ksem.at[b % 2]),
                pltpu.make_async_copy(v_hbm.at[rows], vbuf.at[b % 2], vsem.at[b % 2]))

    @pl.when(pl.program_id(1) == 0)
    def _():
        @pl.when(batch == 0)
        def _():
            for c in kv_copies(batch):
                c.start()

        for c in kv_copies(batch):
            c.wait()

        @pl.when(batch + 1 < pl.num_programs(0))
        def _():
            for c in kv_copies(batch + 1):
                c.start()

    k_ref = kbuf.at[batch % 2]
    v_ref = vbuf.at[batch % 2]
    first_block = pl.program_id(1) * ATTN_SUB
    first_head = lax.broadcasted_iota(I32, (bq, LANES), 1) < SB_HEAD_DIM
    for u, (sub, p) in enumerate(units):
        q2 = q_ref[sub * bq:(sub + 1) * bq, p * LANES:(p + 1) * LANES]
        qs[u][:bq] = jnp.where(first_head, q2, jnp.zeros_like(q2))
        qs[u][bq:] = jnp.where(first_head, jnp.zeros_like(q2), q2)
    row = lax.broadcasted_iota(I32, (2 * bq, bq), 0)
    col = lax.broadcasted_iota(I32, (2 * bq, bq), 1)
    causal = col < (row & (bq - 1))
    contract_last = (((1,), (1,)), ((), ()))

    def softplus(z):
        return jnp.maximum(z, 0.0) + jnp.log2(1.0 + jnp.exp2(-jnp.abs(z)))

    def suffix_sums(sp):
        return _dot(sp.astype(BF16), tri_ref[...])

    def cols(ref, block, p):
        start = pl.multiple_of(jnp.maximum(block, 0) * bq, bq)
        return ref[pl.ds(start, bq), p * LANES:(p + 1) * LANES]

    def scores(u, block):
        return lax.dot_general(qs[u][...], cols(k_ref, block, units[u][1]), contract_last, preferred_element_type=F32)

    diag = [first_block + sub for sub, _ in units]
    z_d, z_n, s_d, s_n = {}, {}, {}, {}

    def stage_scores(u):
        z_d[u] = scores(u, diag[u])
        z_n[u] = scores(u, diag[u] - 1)

    def stage_sums(u):
        s_d[u] = suffix_sums(jnp.where(causal, softplus(z_d[u]), 0.0))
        s_n[u] = suffix_sums(jnp.where(diag[u] >= 1, softplus(z_n[u]), 0.0))

    def stage_values(u):
        p = units[u][1]
        a_d = jnp.where(causal, jnp.exp2(z_d[u] - s_d[u][:, :bq]), 0.0)
        r_d = s_d[u][:, bq:]
        a_n = jnp.where(diag[u] >= 1, jnp.exp2(z_n[u] - (r_d + s_n[u][:, :bq])), 0.0)
        acc[u][...] = (_dot(a_d.astype(BF16), cols(v_ref, diag[u], p)) + _dot(a_n.astype(BF16), cols(v_ref, diag[u] - 1, p)))
        rr[u][...] = r_d + s_n[u][:, bq:]

    for t in range(n_units + 2):
        if t < n_units:
            stage_scores(t)
        if 0 <= t - 1 < n_units:
            stage_sums(t - 1)
        if 0 <= t - 2 < n_units:
            stage_values(t - 2)

    def r_min():
        m = rr[0][...]
        for u in range(1, n_units):
            m = jnp.minimum(m, rr[u][...])
        return jnp.min(m)

    def cond(c):
        back, rm = c
        return jnp.logical_and(diag[-1] - back >= 0, rm < ATTN_EXIT_BITS)

    def body(c):
        back, _ = c
        blocks = [d - back for d in diag]
        zs = [scores(u, blocks[u]) for u in range(n_units)]
        ss = [suffix_sums(jnp.where(blocks[u] >= 0, softplus(zs[u]), 0.0)) for u in range(n_units)]
        for u, (_, p) in enumerate(units):
            r = rr[u][...]
            a = jnp.where(blocks[u] >= 0, jnp.exp2(zs[u] - (r + ss[u][:, :bq])), 0.0)
            acc[u][...] += _dot(a.astype(BF16), cols(v_ref, blocks[u], p))
            rr[u][...] = r + ss[u][:, bq:]
        return back + 1, r_min()

    lax.while_loop(cond, body, (2, r_min()))
    for u, (sub, p) in enumerate(units):
        o_ref[sub * bq:(sub + 1) * bq, p * LANES:(p + 1) * LANES] = (
            jnp.where(first_head, acc[u][:bq], acc[u][bq:]).astype(BF16))


def _mixer_out_kernel(x_ref, p_ref, sba_ref, g1_ref, wga_ref, wau_ref, wout_ref, g2_ref, wr_hi_ref, wr_lo_ref, br_ref,
                      ltri_ref, x1_ref, hp_ref, ri_ref, gate_ref, cnt_ref, logit_ref, *, tm):
    step = pl.program_id(0)

    @pl.when(step == 0)
    def _():
        cnt_ref[...] = jnp.zeros_like(cnt_ref)
        logit_ref[1] = jnp.zeros((tm, LANES), F32)

    logits = logit_ref[(step + 1) % 2]
    routed = jnp.where(step >= 1, 1.0, 0.0)
    lane = lax.broadcasted_iota(I32, logits.shape, 1).astype(F32)
    work = logits
    vals, idxs = [], []

    def topk_round():
        nonlocal work
        m = jnp.max(work, axis=-1, keepdims=True)
        ik = jnp.min(jnp.where(work == m, lane, float(LANES)), axis=-1, keepdims=True)
        vals.append(m)
        idxs.append(ik)
        work = jnp.where(lane == ik, -jnp.inf, work)

    d_model = x_ref.shape[1]
    quarter = d_model // TOP_K
    topk_round()
    x = x_ref[...]
    h1 = (x * lax.rsqrt(jnp.mean(x * x, axis=-1, keepdims=True) + EPS) * g1_ref[...]).astype(BF16)
    attn_gate = jax.nn.sigmoid(_dot(h1, wga_ref[...]))
    attn_out = _dot(sba_ref[...], wau_ref[...])
    merged = (p_ref[...].astype(F32) + attn_gate * attn_out).astype(BF16)
    x1_parts = []
    for c in range(TOP_K):
        if c >= 1:
            topk_round()
        cols = slice(c * quarter, (c + 1) * quarter)
        x1_parts.append(x[:, cols] + _dot(merged, wout_ref[:, cols]))
    x1 = jnp.concatenate(x1_parts, axis=1)
    x1_ref[...] = x1

    es = [jnp.exp(v - vals[0]) for v in vals]
    denom = es[0] + es[1] + es[2] + es[3]
    hot = jnp.zeros(logits.shape, F32)
    for ik in idxs:
        hot = hot + jnp.where(lane == ik, routed, 0.0)
    before = _dot(ltri_ref[...], hot.astype(BF16)) + cnt_ref[0:1, :]

    ms = jnp.mean(x1 * x1, axis=-1, keepdims=True)
    h2 = x1 * lax.rsqrt(ms + EPS) * g2_ref[...]
    _pack_rows(h2, hp_ref)

    ri = jnp.zeros(logits.shape, F32)
    gt = jnp.zeros(logits.shape, F32)
    for kk in range(TOP_K):
        rank = jnp.sum(jnp.where(lane == idxs[kk], before, 0.0), axis=-1, keepdims=True)
        ri = jnp.where(lane == kk, idxs[kk], ri)
        ri = jnp.where(lane == TOP_K + kk, rank, ri)
        gt = jnp.where(lane == kk, es[kk] / denom, gt)
    ri_ref[...] = ri.T[:2 * TOP_K].astype(I32)
    gate_ref[...] = gt
    cnt_ref[...] = cnt_ref[...] + jnp.sum(hot, axis=0, keepdims=True)

    h_hi, h_lo = _split_bf16(h2)
    logit_ref[step % 2] = (_dot(h_hi, wr_hi_ref[...]) + _dot(h_hi, wr_lo_ref[...]) + _dot(h_lo, wr_hi_ref[...])
                           + br_ref[...])


def _dest_kernel(start_ref, ri_ref, dest_ref):
    idx = ri_ref[:TOP_K, :]
    dest = ri_ref[TOP_K:, :]
    for e in range(start_ref.shape[0]):
        dest = dest + jnp.where(idx == e, start_ref[e], 0)
    dest_ref[...] = dest


def _sc_mesh():
    info = plsc.get_sparse_core_info()
    mesh = plsc.VectorSubcoreMesh(core_axis_name="c", subcore_axis_name="s")
    return mesh, info.num_cores, info.num_subcores, info.num_lanes


def _sc_scatter_rows(rows, idx, n_out):
    mesh, nc, ns, lanes = _sc_mesh()
    nw = nc * ns
    planes, m, width = rows.shape
    ch = SC_CHUNK
    n_ch = m // ch // nw
    half = planes // 2
    assert planes % 2 == 0 and n_ch * ch * nw == m and idx.shape == (m // ch * TOP_K, ch) and n_ch >= 2
    assert n_ch % 8 == 0

    @functools.partial(
        pl.kernel, mesh=mesh, out_type=jax.ShapeDtypeStruct((planes * n_out, width), rows.dtype),
        scratch_types=([pltpu.VMEM((n_ch * TOP_K, ch), I32)] + [pltpu.VMEM((ch, width), rows.dtype)] * planes
                       + [pltpu.VMEM((TOP_K, ch), I32)] * planes
                       + [pltpu.SemaphoreType.DMA((planes,)), pltpu.SemaphoreType.DMA((planes,))]))
    def scatter_kernel(rows_hbm, idx_hbm, out_hbm, idx_v, *rest):
        bufs, ibufs, rsem, ssem = rest[:planes], rest[planes:2 * planes], rest[2 * planes], rest[2 * planes + 1]
        wid = lax.axis_index("s") * nc + lax.axis_index("c")
        for kk in range(TOP_K):
            pltpu.sync_copy(idx_hbm.at[pl.ds(kk * (m // ch) + wid * n_ch, n_ch)], idx_v.at[pl.ds(kk * n_ch, n_ch)])
        base = wid * n_ch * ch

        def read(j, b):
            return pltpu.make_async_copy(rows_hbm.at[b, pl.ds(pl.multiple_of(base + j * ch, ch), ch)], bufs[b], rsem.at[b])

        def scatters(b):
            return [pltpu.make_async_copy(bufs[b], out_hbm.at[ibufs[b].at[kk]], ssem.at[b]) for kk in range(TOP_K)]

        def start_scatters(j, b):
            for kk in range(TOP_K):
                for t in range(0, ch, lanes):
                    ibufs[b][kk, pl.ds(t, lanes)] = idx_v[kk * n_ch + j, pl.ds(t, lanes)] + b * n_out
            for c in scatters(b):
                c.start()

        def wait_scatters(b):
            for c in scatters(b):
                c.wait()

        def finish(j, b):
            pj, pb = (j, b - half) if b >= half else (j - 1, b + half)
            read(pj, pb).wait()
            start_scatters(pj, pb)

        for b in range(planes):
            read(0, b).start()
        for b in range(half, planes):
            finish(0, b)

        @pl.loop(1, n_ch)
        def _(j):
            for b in range(planes):
                wait_scatters(b)
                read(j, b).start()
                finish(j, b)

        for b in range(half):
            finish(n_ch, b)
        for b in range(planes):
            wait_scatters(b)

    return scatter_kernel(rows, idx).reshape(planes, n_out, width)


def _padfill_kernel(start_ref, len_ref, xs_in_ref, xs_ref, zeros_ref, sem):
    del xs_in_ref
    zeros_ref[...] = jnp.zeros_like(zeros_ref)
    bits = [1 << b for b in reversed(range(EXPERT_BLOCK.bit_length() - 1))]

    def pieces(e):
        n = len_ref[e]
        for bit in bits:
            row0 = start_ref[e] + (n & ~(2 * bit - 1))
            copy = pltpu.make_async_copy(zeros_ref.at[:, pl.ds(0, bit)], xs_ref.at[:, pl.ds(row0, bit)], sem)
            yield (n & bit) != 0, copy

    def start(e, c):
        for on, copy in pieces(e):
            pl.when(on)(copy.start)
        return c

    def wait(e, c):
        for on, copy in pieces(e):
            pl.when(on)(copy.wait)
        return c

    lax.fori_loop(0, start_ref.shape[0], start, 0)
    lax.fori_loop(0, start_ref.shape[0], wait, 0)


def _experts_kernel(be_ref, nb_ref, valid_ref, slot_ref, next_ref, xs_ref, wgu_hbm, bgu_ref, wd_hbm, bd_ref, ys_ref,
                    wgu_buf, wd_buf, wgu_bf_ref, wd_bf_ref, gu_sem, d_sem):
    del nb_ref
    blk = pl.program_id(0)
    de = wd_hbm.shape[1]
    valid = valid_ref[blk]
    expert = be_ref[blk]
    slot = slot_ref[blk]

    def weight_copies(e, s):
        return (pltpu.make_async_copy(wgu_hbm.at[e], wgu_buf.at[s], gu_sem.at[s]),
                pltpu.make_async_copy(wd_hbm.at[e], wd_buf.at[s], d_sem.at[s]))

    first_of_expert = jnp.logical_or(blk == 0, expert != be_ref[jnp.maximum(blk - 1, 0)])

    @pl.when(jnp.logical_and(valid > 0, first_of_expert))
    def _():
        @pl.when(blk == 0)
        def _():
            for c in weight_copies(expert, slot):
                c.start()

        for c in weight_copies(expert, slot):
            c.wait()
        wgu_bf_ref[...] = wgu_buf[slot].astype(BF16)
        wd_bf_ref[...] = wd_buf[slot].astype(BF16)

        @pl.when(next_ref[blk] >= 0)
        def _():
            for c in weight_copies(next_ref[blk], 1 - slot):
                c.start()

    def mlp(m):
        x = _unpack_rows(xs_ref.at[:, pl.ds(0, m)]).astype(BF16)
        gu = _dot(x, wgu_bf_ref[...]) + bgu_ref[0]
        glu = jnp.minimum(gu[:, :de], SWIGLU_LIMIT)
        lin = jnp.clip(gu[:, de:], -SWIGLU_LIMIT, SWIGLU_LIMIT)
        act = glu * jax.nn.sigmoid(SWIGLU_ALPHA * glu) * (lin + 1.0)
        y = _dot(act.astype(BF16), wd_bf_ref[...]) + bd_ref[0]
        _pack_rows(y, ys_ref.at[:, pl.ds(0, m)])
        if m < EXPERT_BLOCK:
            ys_ref[:, m:, :] = jnp.zeros((PACK_ROWS, EXPERT_BLOCK - m, LANES), U32)

    step = EXPERT_BLOCK // EXPERT_SPLIT
    for q in range(EXPERT_SPLIT):
        pl.when(jnp.logical_and(valid > q * step, valid <= (q + 1) * step))(functools.partial(mlp, (q + 1) * step))

    @pl.when(valid == 0)
    def _():
        ys_ref[...] = jnp.zeros_like(ys_ref)


def _sc_gather_rows(table, idx):
    mesh, nc, ns, lanes = _sc_mesh()
    nw = nc * ns
    planes, n_tab, width = table.shape
    n_idx_rows, ch = idx.shape
    n_ch = n_idx_rows // nw
    m = n_idx_rows * ch
    half = planes // 2
    assert planes % 2 == 0 and ch == SC_CHUNK and n_ch * nw == n_idx_rows and n_ch >= 2

    @functools.partial(
        pl.kernel, mesh=mesh, out_type=jax.ShapeDtypeStruct((planes, m, width), table.dtype),
        scratch_types=([pltpu.VMEM((n_ch, ch), I32)] + [pltpu.VMEM((ch, width), table.dtype)] * planes
                       + [pltpu.VMEM((8, ch), I32)] * planes
                       + [pltpu.SemaphoreType.DMA((planes,)), pltpu.SemaphoreType.DMA((planes,))]))
    def gather_kernel(table_hbm, idx_hbm, out_hbm, idx_v, *rest):
        bufs, ibufs, gsem, wsem = rest[:planes], rest[planes:2 * planes], rest[2 * planes], rest[2 * planes + 1]
        wid = lax.axis_index("s") * nc + lax.axis_index("c")
        pltpu.sync_copy(idx_hbm.at[pl.ds(wid * n_ch, n_ch)], idx_v)
        base = wid * n_ch * ch

        def gather(b):
            return pltpu.make_async_copy(table_hbm.at[ibufs[b].at[0]], bufs[b], gsem.at[b])

        def start_gather(j, b):
            for t in range(0, ch, lanes):
                ibufs[b][0, pl.ds(t, lanes)] = idx_v[j, pl.ds(t, lanes)] + b * n_tab
            gather(b).start()

        def write(j, b):
            return pltpu.make_async_copy(bufs[b], out_hbm.at[b, pl.ds(pl.multiple_of(base + j * ch, ch), ch)], wsem.at[b])

        def finish(j, b):
            pj, pb = (j, b - half) if b >= half else (j - 1, b + half)
            gather(pb).wait()
            write(pj, pb).start()

        for b in range(planes):
            start_gather(0, b)
        for b in range(half, planes):
            finish(0, b)

        @pl.loop(1, n_ch)
        def _(j):
            for b in range(planes):
                write(j - 1, b).wait()
                start_gather(j, b)
                finish(j, b)

        for b in range(half):
            finish(n_ch, b)
        for b in range(planes):
            write(n_ch - 1, b).wait()

    return gather_kernel(table.reshape(planes * n_tab, width), idx)


def _combine_kernel(x1_ref, gate_ref, y0_ref, y1_ref, y2_ref, y3_ref, *rest):
    o_ref = rest[-1]
    gate = gate_ref[...]
    out = x1_ref[...]
    for kk, y_ref in enumerate((y0_ref, y1_ref, y2_ref, y3_ref)):
        out = out + gate[:, kk:kk + 1] * _unpack_rows(y_ref)
    o_ref[...] = out


def _const_spec(shape):
    nd = len(shape)
    return pl.BlockSpec(shape, lambda *_: (0,) * nd)


def _layer(x, norm1_g, w_in, q_norm_g, k_norm_g, w_pool_grp, pool_scale, w_pool_up, w_attn_up, w_out, norm2_g,
           w_router, b_router, w_gate_up, b_gate_up, w_down, b_down):
    B, S, D = x.shape
    N = B * S
    pw = w_pool_up.shape[0]
    sw = w_attn_up.shape[0]
    n_exp = w_router.shape[1]
    de = w_down.shape[1]
    heads = sw // SB_HEAD_DIM
    assert pw == len(POOL_WINDOWS) * POOL_GROUP_DIM and heads % 2 == 0 and n_exp <= LANES
    assert D == 2 * PACK_ROWS * LANES and w_in.shape[1] == pw + 3 * sw + 2 * D
    tm = 512 if S % 512 == 0 else 256
    assert S % tm == 0 and S % (ATTN_SUB * ATTN_BLOCK) == 0
    xf = x.reshape(N, D)
    cparams = functools.partial(pltpu.CompilerParams, vmem_limit_bytes=VMEM_LIMIT)

    hsum = (jnp.arange(sw)[:, None] // SB_HEAD_DIM == jnp.arange(sw)[None, :] // SB_HEAD_DIM).astype(BF16)
    nt = S // tm
    tok_spec = lambda w: pl.BlockSpec((tm, w), lambda b, i: (b * nt + i, 0))
    w_in_bf = w_in.astype(BF16)
    n_in = pw + 3 * sw + D
    g1 = norm1_g.reshape(1, D)
    q2, k2, v2, pg = pl.pallas_call(
        functools.partial(_mixer_in_kernel, tm=tm, pw=pw, sw=sw),
        grid=(B, nt),
        in_specs=[tok_spec(D), _const_spec((1, D)), _const_spec((D, n_in)), _const_spec((1, sw)),
                  _const_spec((1, sw)), _const_spec((sw, sw)),
                  _const_spec((len(POOL_WINDOWS), POOL_GROUP_DIM, POOL_GROUP_DIM)), _const_spec((1, pw)),
                  _const_spec((pw, D))],
        out_specs=[tok_spec(sw), tok_spec(sw), tok_spec(sw), tok_spec(D)],
        out_shape=[jax.ShapeDtypeStruct((N, sw), BF16)] * 3 + [jax.ShapeDtypeStruct((N, D), BF16)],
        scratch_shapes=[pltpu.VMEM((POOL_HALO, pw), F32)],
        compiler_params=cparams(dimension_semantics=("arbitrary", "arbitrary")),
        name="mixer_in",
    )(xf, g1, w_in_bf[:, :n_in], jnp.tile(q_norm_g, heads).reshape(1, sw),
      jnp.tile(k_norm_g, heads).reshape(1, sw), hsum, w_pool_grp.astype(BF16), pool_scale.reshape(1, pw),
      w_pool_up.astype(BF16))

    bq = ATTN_BLOCK
    rows_q = ATTN_SUB * bq
    nq = S // rows_q
    jj = jnp.arange(bq)
    tri = jnp.concatenate([(jj[:, None] >= jj[None, :]).astype(BF16), jnp.ones((bq, bq), BF16)], axis=1)
    n_pairs = heads // 2
    n_units = ATTN_SUB * n_pairs
    kv_spec = pl.BlockSpec(memory_space=pl.ANY)
    sba = pl.pallas_call(
        functools.partial(_attn_kernel, n_pairs=n_pairs),
        grid=(B, nq),
        in_specs=[pl.BlockSpec((rows_q, sw), lambda b, qi: (b * nq + qi, 0)), kv_spec, kv_spec,
                  _const_spec((bq, 2 * bq))],
        out_specs=pl.BlockSpec((rows_q, sw), lambda b, qi: (b * nq + qi, 0)),
        out_shape=jax.ShapeDtypeStruct((N, sw), BF16),
        scratch_shapes=([pltpu.VMEM((2, S, sw), BF16), pltpu.VMEM((2, S, sw), BF16),
                         pltpu.SemaphoreType.DMA((2,)), pltpu.SemaphoreType.DMA((2,))]
                        + [pltpu.VMEM((2 * bq, LANES), BF16)] * n_units + [pltpu.VMEM((2 * bq, LANES), F32)] * (2 * n_units)),
        compiler_params=cparams(dimension_semantics=("arbitrary", "arbitrary")),
        name="sb_attn",
    )(q2, k2, v2, tri)

    wr = jnp.zeros((D, LANES), F32).at[:, :n_exp].set(w_router)
    wr_hi = wr.astype(BF16)
    wr_lo = (wr - wr_hi.astype(F32)).astype(BF16)
    br = jnp.full((1, LANES), -jnp.inf, F32).at[0, :n_exp].set(b_router)
    ltri = (jnp.arange(tm)[:, None] > jnp.arange(tm)[None, :]).astype(BF16)
    assert N % (MOE_GROUPS * tm) == 0
    ng = N // MOE_GROUPS
    steps = ng // tm
    n_assign = ng * TOP_K
    n_blocks = -(-(n_assign + n_exp * (EXPERT_BLOCK - 1)) // EXPERT_BLOCK)
    n_rows = n_blocks * EXPERT_BLOCK
    plane_spec = lambda rows, index: pl.BlockSpec((PACK_ROWS, rows, LANES), lambda i, *_: (0, index(i, *_), 0))
    row_spec = lambda w: pl.BlockSpec((tm, w), lambda i: (i, 0))
    w_attn_up_bf, w_out_bf, g2 = w_attn_up.astype(BF16), w_out.astype(BF16), norm2_g.reshape(1, D)
    bgu, bdn = b_gate_up.reshape(n_exp, 1, 2 * de), b_down.reshape(n_exp, 1, D)
    out = None
    for grp in range(MOE_GROUPS):
        grp_spec = lambda w, first=grp * steps: pl.BlockSpec((tm, w), lambda i: (first + i, 0))

        cur = lambda i: jnp.minimum(i, steps - 1)
        prev = lambda i: jnp.maximum(i - 1, 0)
        in_spec = lambda w, first=grp * steps: pl.BlockSpec((tm, w), lambda i: (first + cur(i), 0))
        x1, hpk, ri, gate, cnt = pl.pallas_call(
            functools.partial(_mixer_out_kernel, tm=tm),
            grid=(steps + 1,),
            in_specs=[in_spec(D), in_spec(D), in_spec(sw), _const_spec((1, D)), _const_spec((D, D)),
                      _const_spec((sw, D)), _const_spec((D, D)),
                      _const_spec((1, D)), _const_spec((D, LANES)), _const_spec((D, LANES)), _const_spec((1, LANES)),
                      _const_spec((tm, tm))],
            out_specs=[pl.BlockSpec((tm, D), lambda i: (cur(i), 0)), plane_spec(tm, cur),
                       pl.BlockSpec((2 * TOP_K, tm), lambda i: (0, prev(i))),
                       pl.BlockSpec((tm, LANES), lambda i: (prev(i), 0)), _const_spec((8, LANES))],
            out_shape=[jax.ShapeDtypeStruct((ng, D), F32), jax.ShapeDtypeStruct((PACK_ROWS, ng, LANES), U32),
                       jax.ShapeDtypeStruct((2 * TOP_K, ng), I32), jax.ShapeDtypeStruct((ng, LANES), F32),
                       jax.ShapeDtypeStruct((8, LANES), F32)],
            scratch_shapes=[pltpu.VMEM((2, tm, LANES), F32)],
            compiler_params=cparams(dimension_semantics=("arbitrary",)),
            name="mixer_out",
        )(xf, pg, sba, g1, w_in_bf[:, n_in:], w_attn_up_bf, w_out_bf, g2, wr_hi, wr_lo, br, ltri)

        counts = cnt[0, :n_exp].astype(I32)
        padded = (counts + EXPERT_BLOCK - 1) // EXPERT_BLOCK * EXPERT_BLOCK
        padded_end = jnp.cumsum(padded)
        start_pad = padded_end - padded
        td = min(ng, 8192)
        dest = pl.pallas_call(
            _dest_kernel,
            grid_spec=pltpu.PrefetchScalarGridSpec(
                num_scalar_prefetch=1,
                grid=(ng // td,),
                in_specs=[pl.BlockSpec((2 * TOP_K, td), lambda i, sp: (0, i))],
                out_specs=pl.BlockSpec((TOP_K, td), lambda i, sp: (0, i)),
            ),
            out_shape=jax.ShapeDtypeStruct((TOP_K, ng), I32),
            compiler_params=cparams(dimension_semantics=("arbitrary",)),
            name="dest",
        )(start_pad, ri).reshape(-1, SC_CHUNK)
        block_start = jnp.arange(n_blocks, dtype=I32) * EXPERT_BLOCK
        block_expert = jnp.minimum(jnp.sum((padded_end[None, :] <= block_start[:, None]).astype(I32), axis=1),
                                   n_exp - 1)
        n_used = (padded_end[-1] // EXPERT_BLOCK).astype(I32).reshape(1)

        xs = _sc_scatter_rows(hpk, dest, n_rows)
        xs = pl.pallas_call(
            _padfill_kernel,
            grid_spec=pltpu.PrefetchScalarGridSpec(
                num_scalar_prefetch=2,
                grid=(1,),
                in_specs=[pl.BlockSpec(memory_space=pl.ANY)],
                out_specs=pl.BlockSpec(memory_space=pl.ANY),
                scratch_shapes=[pltpu.VMEM((PACK_ROWS, EXPERT_BLOCK // 2, LANES), U32), pltpu.SemaphoreType.DMA(())],
            ),
            out_shape=jax.ShapeDtypeStruct((PACK_ROWS, n_rows, LANES), U32),
            input_output_aliases={2: 0},
            compiler_params=cparams(dimension_semantics=("arbitrary",)),
            name="padfill",
        )(start_pad + counts, padded - counts, xs)

        experts = jnp.arange(n_exp, dtype=I32)
        used = counts > 0
        block_valid = jnp.clip(jnp.sum(jnp.where(block_expert[:, None] == experts[None, :],
                                                 (start_pad + counts)[None, :], 0), axis=1) - block_start,
                               0, EXPERT_BLOCK)
        ordinal = jnp.cumsum(used.astype(I32)) - 1
        next_used = jnp.min(jnp.where(jnp.logical_and(experts[None, :] > experts[:, None], used[None, :]),
                                      experts[None, :], n_exp), axis=1)
        next_used = jnp.where(next_used == n_exp, -1, next_used)
        per_block = lambda table: jnp.sum(jnp.where(block_expert[:, None] == experts[None, :], table[None, :], 0), axis=1)
        b_spec = lambda shape: pl.BlockSpec(shape, lambda i, be, *_: (be[i], 0, 0))
        ys = pl.pallas_call(
            _experts_kernel,
            grid_spec=pltpu.PrefetchScalarGridSpec(
                num_scalar_prefetch=5,
                grid=(n_blocks,),
                in_specs=[plane_spec(EXPERT_BLOCK, lambda i, be, nb, *_: jnp.minimum(i, nb[0] - 1)),
                          pl.BlockSpec(memory_space=pl.ANY), b_spec((1, 1, 2 * de)),
                          pl.BlockSpec(memory_space=pl.ANY), b_spec((1, 1, D))],
                out_specs=plane_spec(EXPERT_BLOCK, lambda i, *_: i),
                scratch_shapes=[pltpu.VMEM((2, D, 2 * de), F32), pltpu.VMEM((2, de, D), F32),
                                pltpu.VMEM((D, 2 * de), BF16), pltpu.VMEM((de, D), BF16),
                                pltpu.SemaphoreType.DMA((2,)), pltpu.SemaphoreType.DMA((2,))],
            ),
            out_shape=jax.ShapeDtypeStruct((PACK_ROWS, n_rows, LANES), U32),
            compiler_params=cparams(dimension_semantics=("arbitrary",)),
            name="experts",
        )(block_expert, n_used, block_valid, per_block(ordinal % 2), per_block(next_used), xs, w_gate_up, bgu, w_down, bdn)

        y4 = _sc_gather_rows(ys, dest.reshape(-1, SC_CHUNK))

        operands = [x1, gate, y4, y4, y4, y4] + ([] if out is None else [out])
        out = pl.pallas_call(
            _combine_kernel,
            grid=(steps,),
            in_specs=([row_spec(D), row_spec(LANES)]
                      + [plane_spec(tm, lambda i, kk=kk: kk * steps + i) for kk in range(TOP_K)]
                      + ([] if out is None else [pl.BlockSpec(memory_space=pl.ANY)])),
            out_specs=grp_spec(D),
            out_shape=jax.ShapeDtypeStruct((N, D), F32),
            input_output_aliases={} if out is None else {len(operands) - 1: 0},
            compiler_params=cparams(dimension_semantics=("arbitrary",)),
            name="combine",
        )(*operands)
    return out.reshape(B, S, D)


def kernel(x, norm1_g, w_in, q_norm_g, k_norm_g, w_pool_grp, pool_scale, w_pool_up, w_attn_up, w_out, norm2_g,
           w_router, b_router, w_gate_up, b_gate_up, w_down, b_down):
    for layer in range(norm1_g.shape[0]):
        x = _layer(x, norm1_g[layer], w_in[layer], q_norm_g[layer], k_norm_g[layer], w_pool_grp[layer],
                   pool_scale[layer], w_pool_up[layer], w_attn_up[layer], w_out[layer], norm2_g[layer],
                   w_router[layer], b_router[layer], w_gate_up[layer], b_gate_up[layer], w_down[layer],
                   b_down[layer])
    return x
```

```python
import functools

import jax
import jax.numpy as jnp
from jax import lax
from jax.experimental import pallas as pl
from jax.experimental.pallas import tpu as pltpu
from jax.experimental.pallas import tpu_sc as plsc

F32 = jnp.float32
BF16 = jnp.bfloat16
U32 = jnp.uint32
I32 = jnp.int32

EPS = 1e-6
POOL_WINDOWS = (2, 4, 8, 16)
POOL_GROUP_DIM = 128
POOL_HALO = 16
SB_HEAD_DIM = 64
TOP_K = 4
SWIGLU_LIMIT = 7.0
SWIGLU_ALPHA = 1.702
EXPERT_BLOCK = 512
EXPERT_SPLIT = 4
LANES = 128
PACK_ROWS = 4
ATTN_BLOCK = 128
ATTN_SUB = 2
ATTN_EXIT_BITS = 70.0
LOG2_E = 1.4426950408889634
VMEM_LIMIT = 56 * 1024 * 1024
SC_CHUNK = 128
MOE_GROUPS = 2


def _dot(a, b):
    return jnp.dot(a, b, preferred_element_type=F32)


def _split_bf16(x):
    hi = x.astype(BF16)
    lo = (x - hi.astype(F32)).astype(BF16)
    return hi, lo


def _pack_rows(v, out_ref):
    half = v.shape[1] // 2
    lo = lax.bitcast_convert_type(v[:, :half].astype(BF16).astype(F32), U32) >> 16
    hi = lax.bitcast_convert_type(v[:, half:].astype(BF16).astype(F32), U32) & jnp.uint32(0xFFFF0000)
    w = lo | hi
    for c in range(PACK_ROWS):
        out_ref[c] = w[:, c * LANES:(c + 1) * LANES]


def _unpack_rows(ref):
    los, his = [], []
    for c in range(PACK_ROWS):
        w = ref[c]
        los.append(lax.bitcast_convert_type(w << 16, F32))
        his.append(lax.bitcast_convert_type(w & jnp.uint32(0xFFFF0000), F32))
    return jnp.concatenate(los + his, axis=1)


def _mixer_in_kernel(x_ref, g1_ref, win_ref, gq_ref, gk_ref, hsum_ref, wgrp_ref, pscale_ref, wpu_ref,
                     q_ref, k_ref, v_ref, p_ref, tail_ref, *, tm, pw, sw):
    i = pl.program_id(1)

    @pl.when(i == 0)
    def _():
        tail_ref[...] = jnp.zeros_like(tail_ref)

    x = x_ref[...]
    ms = jnp.mean(x * x, axis=-1, keepdims=True)
    h = (x * lax.rsqrt(ms + EPS) * g1_ref[...]).astype(BF16)

    def head_norm(t, gain):
        ss = _dot((t * t).astype(BF16), hsum_ref[...])
        return t * lax.rsqrt(ss * (1.0 / SB_HEAD_DIM) + EPS) * gain

    def project_q():
        q = _dot(h, win_ref[:, pw:pw + sw])
        q_ref[...] = (head_norm(q, gq_ref[...]) * (SB_HEAD_DIM ** -0.5 * LOG2_E)).astype(BF16)

    def project_k():
        k = _dot(h, win_ref[:, pw + sw:pw + 2 * sw])
        k_ref[...] = head_norm(k, gk_ref[...]).astype(BF16)

    def project_v():
        v_ref[...] = _dot(h, win_ref[:, pw + 2 * sw:pw + 3 * sw]).astype(BF16)

    d_model = x.shape[1]
    pool_gate = []

    def project_pool_gate():
        pool_gate.append(jax.nn.sigmoid(_dot(h, win_ref[:, pw + 3 * sw:pw + 3 * sw + d_model])))

    u = _dot(h, win_ref[:, 0:pw])
    xx = jnp.concatenate([tail_ref[...], u], axis=0)
    tail_ref[...] = u[tm - POOL_HALO:, :]
    pos = i * tm + lax.broadcasted_iota(I32, (tm, POOL_GROUP_DIM), 0)
    mixed = []
    for (g, w), project in zip(enumerate(POOL_WINDOWS), (project_q, project_k, project_v, project_pool_gate)):
        project()
        s = xx[:, g * POOL_GROUP_DIM:(g + 1) * POOL_GROUP_DIM]
        step = 1
        while step < w:
            s = s + pltpu.roll(s, step, axis=0)
            step *= 2
        count = jnp.minimum(pos + 1, w).astype(F32)
        ug = u[:, g * POOL_GROUP_DIM:(g + 1) * POOL_GROUP_DIM]
        d = s[POOL_HALO:, :] / count - ug
        mixed.append(_dot(d.astype(BF16), wgrp_ref[g]))
    pm = jnp.concatenate(mixed, axis=1) * pscale_ref[...]
    pool_out = _dot(pm.astype(BF16), wpu_ref[...])
    p_ref[...] = (pool_gate[0] * pool_out).astype(BF16)


def _attn_kernel(q_ref, k_hbm, v_hbm, tri_ref, o_ref, kbuf, vbuf, ksem, vsem, *scratch, n_pairs):
    units = [(sub, p) for sub in range(ATTN_SUB) for p in range(n_pairs)]
    n_units = len(units)
    qs, acc, rr = scratch[:n_units], scratch[n_units:2 * n_units], scratch[2 * n_units:]
    bq = ATTN_BLOCK
    batch = pl.program_id(0)
    seq = kbuf.shape[1]

    def kv_copies(b):
        rows = pl.ds(pl.multiple_of(b * seq, seq), seq)
        return (pltpu.make_async_copy(k_hbm.at[rows], kbuf.at[b % 2], ksem.at[b % 2]),
                pltpu.make_async_copy(v_hbm.at[rows], vbuf.at[b % 2], vsem.at[b % 2]))

    @pl.when(pl.program_id(1) == 0)
    def _():
        @pl.when(batch == 0)
        def _():
            for c in kv_copies(batch):
                c.start()

        for c in kv_copies(batch):
            c.wait()

        @pl.when(batch + 1 < pl.num_programs(0))
        def _():
            for c in kv_copies(batch + 1):
                c.start()

    k_ref = kbuf.at[batch % 2]
    v_ref = vbuf.at[batch % 2]
    first_block = pl.program_id(1) * ATTN_SUB
    first_head = lax.broadcasted_iota(I32, (bq, LANES), 1) < SB_HEAD_DIM
    for u, (sub, p) in enumerate(units):
        q2 = q_ref[sub * bq:(sub + 1) * bq, p * LANES:(p + 1) * LANES]
        qs[u][:bq] = jnp.where(first_head, q2, jnp.zeros_like(q2))
        qs[u][bq:] = jnp.where(first_head, jnp.zeros_like(q2), q2)
    row = lax.broadcasted_iota(I32, (2 * bq, bq), 0)
    col = lax.broadcasted_iota(I32, (2 * bq, bq), 1)
    causal = col < (row & (bq - 1))
    contract_last = (((1,), (1,)), ((), ()))

    def softplus(z):
        return jnp.maximum(z, 0.0) + jnp.log2(1.0 + jnp.exp2(-jnp.abs(z)))

    def suffix_sums(sp):
        return _dot(sp.astype(BF16), tri_ref[...])

    def cols(ref, block, p):
        start = pl.multiple_of(jnp.maximum(block, 0) * bq, bq)
        return ref[pl.ds(start, bq), p * LANES:(p + 1) * LANES]

    def scores(u, block):
        return lax.dot_general(qs[u][...], cols(k_ref, block, units[u][1]), contract_last, preferred_element_type=F32)

    diag = [first_block + sub for sub, _ in units]
    z_d, z_n, s_d, s_n = {}, {}, {}, {}

    def stage_scores(u):
        z_d[u] = scores(u, diag[u])
        z_n[u] = scores(u, diag[u] - 1)

    def stage_sums(u):
        s_d[u] = suffix_sums(jnp.where(causal, softplus(z_d[u]), 0.0))
        s_n[u] = suffix_sums(jnp.where(diag[u] >= 1, softplus(z_n[u]), 0.0))

    def stage_values(u):
        p = units[u][1]
        a_d = jnp.where(causal, jnp.exp2(z_d[u] - s_d[u][:, :bq]), 0.0)
        r_d = s_d[u][:, bq:]
        a_n = jnp.where(diag[u] >= 1, jnp.exp2(z_n[u] - (r_d + s_n[u][:, :bq])), 0.0)
        acc[u][...] = (_dot(a_d.astype(BF16), cols(v_ref, diag[u], p)) + _dot(a_n.astype(BF16), cols(v_ref, diag[u] - 1, p)))
        rr[u][...] = r_d + s_n[u][:, bq:]

    for t in range(n_units + 2):
        if t < n_units:
            stage_scores(t)
        if 0 <= t - 1 < n_units:
            stage_sums(t - 1)
        if 0 <= t - 2 < n_units:
            stage_values(t - 2)

    def r_min():
        m = rr[0][...]
        for u in range(1, n_units):
            m = jnp.minimum(m, rr[u][...])
        return jnp.min(m)

    def cond(c):
        back, rm = c
        return jnp.logical_and(diag[-1] - back >= 0, rm < ATTN_EXIT_BITS)

    def body(c):
        back, _ = c
        blocks = [d - back for d in diag]
        zs = [scores(u, blocks[u]) for u in range(n_units)]
        ss = [suffix_sums(jnp.where(blocks[u] >= 0, softplus(zs[u]), 0.0)) for u in range(n_units)]
        for u, (_, p) in enumerate(units):
            r = rr[u][...]
            a = jnp.where(blocks[u] >= 0, jnp.exp2(zs[u] - (r + ss[u][:, :bq])), 0.0)
            acc[u][...] += _dot(a.astype(BF16), cols(v_ref, blocks[u], p))
            rr[u][...] = r + ss[u][:, bq:]
        return back + 1, r_min()

    lax.while_loop(cond, body, (2, r_min()))
    for u, (sub, p) in enumerate(units):
        o_ref[sub * bq:(sub + 1) * bq, p * LANES:(p + 1) * LANES] = (
            jnp.where(first_head, acc[u][:bq], acc[u][bq:]).astype(BF16))


def _mixer_out_kernel(x_ref, p_ref, sba_ref, g1_ref, wga_ref, wau_ref, wout_ref, g2_ref, wr_hi_ref, wr_lo_ref, br_ref,
                      ltri_ref, x1_ref, hp_ref, ri_ref, gate_ref, cnt_ref, logit_ref, *, tm):
    step = pl.program_id(0)

    @pl.when(step == 0)
    def _():
        cnt_ref[...] = jnp.zeros_like(cnt_ref)
        logit_ref[1] = jnp.zeros((tm, LANES), F32)

    logits = logit_ref[(step + 1) % 2]
    routed = jnp.where(step >= 1, 1.0, 0.0)
    lane = lax.broadcasted_iota(I32, logits.shape, 1).astype(F32)
    work = logits
    vals, idxs = [], []

    def topk_round():
        nonlocal work
        m = jnp.max(work, axis=-1, keepdims=True)
        ik = jnp.min(jnp.where(work == m, lane, float(LANES)), axis=-1, keepdims=True)
        vals.append(m)
        idxs.append(ik)
        work = jnp.where(lane == ik, -jnp.inf, work)

    d_model = x_ref.shape[1]
    quarter = d_model // TOP_K
    topk_round()
    x = x_ref[...]
    h1 = (x * lax.rsqrt(jnp.mean(x * x, axis=-1, keepdims=True) + EPS) * g1_ref[...]).astype(BF16)
    attn_gate = jax.nn.sigmoid(_dot(h1, wga_ref[...]))
    attn_out = _dot(sba_ref[...], wau_ref[...])
    merged = (p_ref[...].astype(F32) + attn_gate * attn_out).astype(BF16)
    x1_parts = []
    for c in range(TOP_K):
        if c >= 1:
            topk_round()
        cols = slice(c * quarter, (c + 1) * quarter)
        x1_parts.append(x[:, cols] + _dot(merged, wout_ref[:, cols]))
    x1 = jnp.concatenate(x1_parts, axis=1)
    x1_ref[...] = x1

    es = [jnp.exp(v - vals[0]) for v in vals]
    denom = es[0] + es[1] + es[2] + es[3]
    hot = jnp.zeros(logits.shape, F32)
    for ik in idxs:
        hot = hot + jnp.where(lane == ik, routed, 0.0)
    before = _dot(ltri_ref[...], hot.astype(BF16)) + cnt_ref[0:1, :]

    ms = jnp.mean(x1 * x1, axis=-1, keepdims=True)
    h2 = x1 * lax.rsqrt(ms + EPS) * g2_ref[...]
    _pack_rows(h2, hp_ref)

    ri = jnp.zeros(logits.shape, F32)
    gt = jnp.zeros(logits.shape, F32)
    for kk in range(TOP_K):
        rank = jnp.sum(jnp.where(lane == idxs[kk], before, 0.0), axis=-1, keepdims=True)
        ri = jnp.where(lane == kk, idxs[kk], ri)
        ri = jnp.where(lane == TOP_K + kk, rank, ri)
        gt = jnp.where(lane == kk, es[kk] / denom, gt)
    ri_ref[...] = ri.T[:2 * TOP_K].astype(I32)
    gate_ref[...] = gt
    cnt_ref[...] = cnt_ref[...] + jnp.sum(hot, axis=0, keepdims=True)

    h_hi, h_lo = _split_bf16(h2)
    logit_ref[step % 2] = (_dot(h_hi, wr_hi_ref[...]) + _dot(h_hi, wr_lo_ref[...]) + _dot(h_lo, wr_hi_ref[...])
                           + br_ref[...])


def _dest_kernel(start_ref, ri_ref, dest_ref):
    idx = ri_ref[:TOP_K, :]
    dest = ri_ref[TOP_K:, :]
    for e in range(start_ref.shape[0]):
        dest = dest + jnp.where(idx == e, start_ref[e], 0)
    dest_ref[...] = dest


def _sc_mesh():
    info = plsc.get_sparse_core_info()
    mesh = plsc.VectorSubcoreMesh(core_axis_name="c", subcore_axis_name="s")
    return mesh, info.num_cores, info.num_subcores, info.num_lanes


def _sc_scatter_rows(rows, idx, n_out):
    mesh, nc, ns, lanes = _sc_mesh()
    nw = nc * ns
    planes, m, width = rows.shape
    ch = SC_CHUNK
    n_ch = m // ch // nw
    half = planes // 2
    assert planes % 2 == 0 and n_ch * ch * nw == m and idx.shape == (m // ch * TOP_K, ch) and n_ch >= 2
    assert n_ch % 8 == 0

    @functools.partial(
        pl.kernel, mesh=mesh, out_type=jax.ShapeDtypeStruct((planes * n_out, width), rows.dtype),
        scratch_types=([pltpu.VMEM((n_ch * TOP_K, ch), I32)] + [pltpu.VMEM((ch, width), rows.dtype)] * planes
                       + [pltpu.VMEM((TOP_K, ch), I32)] * planes
                       + [pltpu.SemaphoreType.DMA((planes,)), pltpu.SemaphoreType.DMA((planes,))]))
    def scatter_kernel(rows_hbm, idx_hbm, out_hbm, idx_v, *rest):
        bufs, ibufs, rsem, ssem = rest[:planes], rest[planes:2 * planes], rest[2 * planes], rest[2 * planes + 1]
        wid = lax.axis_index("s") * nc + lax.axis_index("c")
        for kk in range(TOP_K):
            pltpu.sync_copy(idx_hbm.at[pl.ds(kk * (m // ch) + wid * n_ch, n_ch)], idx_v.at[pl.ds(kk * n_ch, n_ch)])
        base = wid * n_ch * ch

        def read(j, b):
            return pltpu.make_async_copy(rows_hbm.at[b, pl.ds(pl.multiple_of(base + j * ch, ch), ch)], bufs[b], rsem.at[b])

        def scatters(b):
            return [pltpu.make_async_copy(bufs[b], out_hbm.at[ibufs[b].at[kk]], ssem.at[b]) for kk in range(TOP_K)]

        def start_scatters(j, b):
            for kk in range(TOP_K):
                for t in range(0, ch, lanes):
                    ibufs[b][kk, pl.ds(t, lanes)] = idx_v[kk * n_ch + j, pl.ds(t, lanes)] + b * n_out
            for c in scatters(b):
                c.start()

        def wait_scatters(b):
            for c in scatters(b):
                c.wait()

        def finish(j, b):
            pj, pb = (j, b - half) if b >= half else (j - 1, b + half)
            read(pj, pb).wait()
            start_scatters(pj, pb)

        for b in range(planes):
            read(0, b).start()
        for b in range(half, planes):
            finish(0, b)

        @pl.loop(1, n_ch)
        def _(j):
            for b in range(planes):
                wait_scatters(b)
                read(j, b).start()
                finish(j, b)

        for b in range(half):
            finish(n_ch, b)
        for b in range(planes):
            wait_scatters(b)

    return scatter_kernel(rows, idx).reshape(planes, n_out, width)


def _padfill_kernel(start_ref, len_ref, xs_in_ref, xs_ref, zeros_ref, sem):
    del xs_in_ref
    zeros_ref[...] = jnp.zeros_like(zeros_ref)
    bits = [1 << b for b in reversed(range(EXPERT_BLOCK.bit_length() - 1))]

    def pieces(e):
        n = len_ref[e]
        for bit in bits:
            row0 = start_ref[e] + (n & ~(2 * bit - 1))
            copy = pltpu.make_async_copy(zeros_ref.at[:, pl.ds(0, bit)], xs_ref.at[:, pl.ds(row0, bit)], sem)
            yield (n & bit) != 0, copy

    def start(e, c):
        for on, copy in pieces(e):
            pl.when(on)(copy.start)
        return c

    def wait(e, c):
        for on, copy in pieces(e):
            pl.when(on)(copy.wait)
        return c

    lax.fori_loop(0, start_ref.shape[0], start, 0)
    lax.fori_loop(0, start_ref.shape[0], wait, 0)


def _experts_kernel(be_ref, nb_ref, valid_ref, slot_ref, next_ref, xs_ref, wgu_hbm, bgu_ref, wd_hbm, bd_ref, ys_ref,
                    wgu_buf, wd_buf, wgu_bf_ref, wd_bf_ref, gu_sem, d_sem):
    del nb_ref
    blk = pl.program_id(0)
    de = wd_hbm.shape[1]
    valid = valid_ref[blk]
    expert = be_ref[blk]
    slot = slot_ref[blk]

    def weight_copies(e, s):
        return (pltpu.make_async_copy(wgu_hbm.at[e], wgu_buf.at[s], gu_sem.at[s]),
                pltpu.make_async_copy(wd_hbm.at[e], wd_buf.at[s], d_sem.at[s]))

    first_of_expert = jnp.logical_or(blk == 0, expert != be_ref[jnp.maximum(blk - 1, 0)])

    @pl.when(jnp.logical_and(valid > 0, first_of_expert))
    def _():
        @pl.when(blk == 0)
        def _():
            for c in weight_copies(expert, slot):
                c.start()

        for c in weight_copies(expert, slot):
            c.wait()
        wgu_bf_ref[...] = wgu_buf[slot].astype(BF16)
        wd_bf_ref[...] = wd_buf[slot].astype(BF16)

        @pl.when(next_ref[blk] >= 0)
        def _():
            for c in weight_copies(next_ref[blk], 1 - slot):
                c.start()

    def mlp(m):
        x = _unpack_rows(xs_ref.at[:, pl.ds(0, m)]).astype(BF16)
        gu = _dot(x, wgu_bf_ref[...]) + bgu_ref[0]
        glu = jnp.minimum(gu[:, :de], SWIGLU_LIMIT)
        lin = jnp.clip(gu[:, de:], -SWIGLU_LIMIT, SWIGLU_LIMIT)
        act = glu * jax.nn.sigmoid(SWIGLU_ALPHA * glu) * (lin + 1.0)
        y = _dot(act.astype(BF16), wd_bf_ref[...]) + bd_ref[0]
        _pack_rows(y, ys_ref.at[:, pl.ds(0, m)])
        if m < EXPERT_BLOCK:
            ys_ref[:, m:, :] = jnp.zeros((PACK_ROWS, EXPERT_BLOCK - m, LANES), U32)

    step = EXPERT_BLOCK // EXPERT_SPLIT
    for q in range(EXPERT_SPLIT):
        pl.when(jnp.logical_and(valid > q * step, valid <= (q + 1) * step))(functools.partial(mlp, (q + 1) * step))

    @pl.when(valid == 0)
    def _():
        ys_ref[...] = jnp.zeros_like(ys_ref)


def _sc_gather_rows(table, idx):
    mesh, nc, ns, lanes = _sc_mesh()
    nw = nc * ns
    planes, n_tab, width = table.shape
    n_idx_rows, ch = idx.shape
    n_ch = n_idx_rows // nw
    m = n_idx_rows * ch
    half = planes // 2
    assert planes % 2 == 0 and ch == SC_CHUNK and n_ch * nw == n_idx_rows and n_ch >= 2

    @functools.partial(
        pl.kernel, mesh=mesh, out_type=jax.ShapeDtypeStruct((planes, m, width), table.dtype),
        scratch_types=([pltpu.VMEM((n_ch, ch), I32)] + [pltpu.VMEM((ch, width), table.dtype)] * planes
                       + [pltpu.VMEM((8, ch), I32)] * planes
                       + [pltpu.SemaphoreType.DMA((planes,)), pltpu.SemaphoreType.DMA((planes,))]))
    def gather_kernel(table_hbm, idx_hbm, out_hbm, idx_v, *rest):
        bufs, ibufs, gsem, wsem = rest[:planes], rest[planes:2 * planes], rest[2 * planes], rest[2 * planes + 1]
        wid = lax.axis_index("s") * nc + lax.axis_index("c")
        pltpu.sync_copy(idx_hbm.at[pl.ds(wid * n_ch, n_ch)], idx_v)
        base = wid * n_ch * ch

        def gather(b):
            return pltpu.make_async_copy(table_hbm.at[ibufs[b].at[0]], bufs[b], gsem.at[b])

        def start_gather(j, b):
            for t in range(0, ch, lanes):
                ibufs[b][0, pl.ds(t, lanes)] = idx_v[j, pl.ds(t, lanes)] + b * n_tab
            gather(b).start()

        def write(j, b):
            return pltpu.make_async_copy(bufs[b], out_hbm.at[b, pl.ds(pl.multiple_of(base + j * ch, ch), ch)], wsem.at[b])

        def finish(j, b):
            pj, pb = (j, b - half) if b >= half else (j - 1, b + half)
            gather(pb).wait()
            write(pj, pb).start()

        for b in range(planes):
            start_gather(0, b)
        for b in range(half, planes):
            finish(0, b)

        @pl.loop(1, n_ch)
        def _(j):
            for b in range(planes):
                write(j - 1, b).wait()
                start_gather(j, b)
                finish(j, b)

        for b in range(half):
            finish(n_ch, b)
        for b in range(planes):
            write(n_ch - 1, b).wait()

    return gather_kernel(table.reshape(planes * n_tab, width), idx)


def _combine_kernel(x1_ref, gate_ref, y0_ref, y1_ref, y2_ref, y3_ref, *rest):
    o_ref = rest[-1]
    gate = gate_ref[...]
    out = x1_ref[...]
    for kk, y_ref in enumerate((y0_ref, y1_ref, y2_ref, y3_ref)):
        out = out + gate[:, kk:kk + 1] * _unpack_rows(y_ref)
    o_ref[...] = out


def _const_spec(shape):
    nd = len(shape)
    return pl.BlockSpec(shape, lambda *_: (0,) * nd)


def _layer(x, norm1_g, w_in, q_norm_g, k_norm_g, w_pool_grp, pool_scale, w_pool_up, w_attn_up, w_out, norm2_g,
           w_router, b_router, w_gate_up, b_gate_up, w_down, b_down):
    B, S, D = x.shape
    N = B * S
    pw = w_pool_up.shape[0]
    sw = w_attn_up.shape[0]
    n_exp = w_router.shape[1]
    de = w_down.shape[1]
    heads = sw // SB_HEAD_DIM
    assert pw == len(POOL_WINDOWS) * POOL_GROUP_DIM and heads % 2 == 0 and n_exp <= LANES
    assert D == 2 * PACK_ROWS * LANES and w_in.shape[1] == pw + 3 * sw + 2 * D
    tm = 512 if S % 512 == 0 else 256
    assert S % tm == 0 and S % (ATTN_SUB * ATTN_BLOCK) == 0
    xf = x.reshape(N, D)
    cparams = functools.partial(pltpu.CompilerParams, vmem_limit_bytes=VMEM_LIMIT)

    hsum = (jnp.arange(sw)[:, None] // SB_HEAD_DIM == jnp.arange(sw)[None, :] // SB_HEAD_DIM).astype(BF16)
    tm_in = 2 * tm if S % (2 * tm) == 0 else tm
    nt = S // tm_in
    tok_spec = lambda w: pl.BlockSpec((tm_in, w), lambda b, i: (b * nt + i, 0))
    w_in_bf = w_in.astype(BF16)
    n_in = pw + 3 * sw + D
    g1 = norm1_g.reshape(1, D)
    q2, k2, v2, pg = pl.pallas_call(
        functools.partial(_mixer_in_kernel, tm=tm_in, pw=pw, sw=sw),
        grid=(B, nt),
        in_specs=[tok_spec(D), _const_spec((1, D)), _const_spec((D, n_in)), _const_spec((1, sw)),
                  _const_spec((1, sw)), _const_spec((sw, sw)),
                  _const_spec((len(POOL_WINDOWS), POOL_GROUP_DIM, POOL_GROUP_DIM)), _const_spec((1, pw)),
                  _const_spec((pw, D))],
        out_specs=[tok_spec(sw), tok_spec(sw), tok_spec(sw), tok_spec(D)],
        out_shape=[jax.ShapeDtypeStruct((N, sw), BF16)] * 3 + [jax.ShapeDtypeStruct((N, D), BF16)],
        scratch_shapes=[pltpu.VMEM((POOL_HALO, pw), F32)],
        compiler_params=cparams(dimension_semantics=("arbitrary", "arbitrary")),
        name="mixer_in",
    )(xf, g1, w_in_bf[:, :n_in], jnp.tile(q_norm_g, heads).reshape(1, sw),
      jnp.tile(k_norm_g, heads).reshape(1, sw), hsum, w_pool_grp.astype(BF16), pool_scale.reshape(1, pw),
      w_pool_up.astype(BF16))

    bq = ATTN_BLOCK
    rows_q = ATTN_SUB * bq
    nq = S // rows_q
    jj = jnp.arange(bq)
    tri = jnp.concatenate([(jj[:, None] >= jj[None, :]).astype(BF16), jnp.ones((bq, bq), BF16)], axis=1)
    n_pairs = heads // 2
    n_units = ATTN_SUB * n_pairs
    kv_spec = pl.BlockSpec(memory_space=pl.ANY)
    sba = pl.pallas_call(
        functools.partial(_attn_kernel, n_pairs=n_pairs),
        grid=(B, nq),
        in_specs=[pl.BlockSpec((rows_q, sw), lambda b, qi: (b * nq + qi, 0)), kv_spec, kv_spec,
                  _const_spec((bq, 2 * bq))],
        out_specs=pl.BlockSpec((rows_q, sw), lambda b, qi: (b * nq + qi, 0)),
        out_shape=jax.ShapeDtypeStruct((N, sw), BF16),
        scratch_shapes=([pltpu.VMEM((2, S, sw), BF16), pltpu.VMEM((2, S, sw), BF16),
                         pltpu.SemaphoreType.DMA((2,)), pltpu.SemaphoreType.DMA((2,))]
                        + [pltpu.VMEM((2 * bq, LANES), BF16)] * n_units + [pltpu.VMEM((2 * bq, LANES), F32)] * (2 * n_units)),
        compiler_params=cparams(dimension_semantics=("arbitrary", "arbitrary")),
        name="sb_attn",
    )(q2, k2, v2, tri)

    wr = jnp.zeros((D, LANES), F32).at[:, :n_exp].set(w_router)
    wr_hi = wr.astype(BF16)
    wr_lo = (wr - wr_hi.astype(F32)).astype(BF16)
    br = jnp.full((1, LANES), -jnp.inf, F32).at[0, :n_exp].set(b_router)
    ltri = (jnp.arange(tm)[:, None] > jnp.arange(tm)[None, :]).astype(BF16)
    assert N % (MOE_GROUPS * tm) == 0
    ng = N // MOE_GROUPS
    steps = ng // tm
    n_assign = ng * TOP_K
    n_blocks = -(-(n_assign + n_exp * (EXPERT_BLOCK - 1)) // EXPERT_BLOCK)
    n_rows = n_blocks * EXPERT_BLOCK
    plane_spec = lambda rows, index: pl.BlockSpec((PACK_ROWS, rows, LANES), lambda i, *_: (0, index(i, *_), 0))
    row_spec = lambda w: pl.BlockSpec((tm, w), lambda i: (i, 0))
    w_attn_up_bf, w_out_bf, g2 = w_attn_up.astype(BF16), w_out.astype(BF16), norm2_g.reshape(1, D)
    bgu, bdn = b_gate_up.reshape(n_exp, 1, 2 * de), b_down.reshape(n_exp, 1, D)
    out = None
    for grp in range(MOE_GROUPS):
        grp_spec = lambda w, first=grp * steps: pl.BlockSpec((tm, w), lambda i: (first + i, 0))

        cur = lambda i: jnp.minimum(i, steps - 1)
        prev = lambda i: jnp.maximum(i - 1, 0)
        in_spec = lambda w, first=grp * steps: pl.BlockSpec((tm, w), lambda i: (first + cur(i), 0))
        x1, hpk, ri, gate, cnt = pl.pallas_call(
            functools.partial(_mixer_out_kernel, tm=tm),
            grid=(steps + 1,),
            in_specs=[in_spec(D), in_spec(D), in_spec(sw), _const_spec((1, D)), _const_spec((D, D)),
                      _const_spec((sw, D)), _const_spec((D, D)),
                      _const_spec((1, D)), _const_spec((D, LANES)), _const_spec((D, LANES)), _const_spec((1, LANES)),
                      _const_spec((tm, tm))],
            out_specs=[pl.BlockSpec((tm, D), lambda i: (cur(i), 0)), plane_spec(tm, cur),
                       pl.BlockSpec((2 * TOP_K, tm), lambda i: (0, prev(i))),
                       pl.BlockSpec((tm, LANES), lambda i: (prev(i), 0)), _const_spec((8, LANES))],
            out_shape=[jax.ShapeDtypeStruct((ng, D), F32), jax.ShapeDtypeStruct((PACK_ROWS, ng, LANES), U32),
                       jax.ShapeDtypeStruct((2 * TOP_K, ng), I32), jax.ShapeDtypeStruct((ng, LANES), F32),
                       jax.ShapeDtypeStruct((8, LANES), F32)],
            scratch_shapes=[pltpu.VMEM((2, tm, LANES), F32)],
            compiler_params=cparams(dimension_semantics=("arbitrary",)),
            name="mixer_out",
        )(xf, pg, sba, g1, w_in_bf[:, n_in:], w_attn_up_bf, w_out_bf, g2, wr_hi, wr_lo, br, ltri)

        counts = cnt[0, :n_exp].astype(I32)
        padded = (counts + EXPERT_BLOCK - 1) // EXPERT_BLOCK * EXPERT_BLOCK
        padded_end = jnp.cumsum(padded)
        start_pad = padded_end - padded
        td = min(ng, 8192)
        dest = pl.pallas_call(
            _dest_kernel,
            grid_spec=pltpu.PrefetchScalarGridSpec(
                num_scalar_prefetch=1,
                grid=(ng // td,),
                in_specs=[pl.BlockSpec((2 * TOP_K, td), lambda i, sp: (0, i))],
                out_specs=pl.BlockSpec((TOP_K, td), lambda i, sp: (0, i)),
            ),
            out_shape=jax.ShapeDtypeStruct((TOP_K, ng), I32),
            compiler_params=cparams(dimension_semantics=("arbitrary",)),
            name="dest",
        )(start_pad, ri).reshape(-1, SC_CHUNK)
        block_start = jnp.arange(n_blocks, dtype=I32) * EXPERT_BLOCK
        block_expert = jnp.minimum(jnp.sum((padded_end[None, :] <= block_start[:, None]).astype(I32), axis=1),
                                   n_exp - 1)
        n_used = (padded_end[-1] // EXPERT_BLOCK).astype(I32).reshape(1)

        xs = _sc_scatter_rows(hpk, dest, n_rows)
        xs = pl.pallas_call(
            _padfill_kernel,
            grid_spec=pltpu.PrefetchScalarGridSpec(
                num_scalar_prefetch=2,
                grid=(1,),
                in_specs=[pl.BlockSpec(memory_space=pl.ANY)],
                out_specs=pl.BlockSpec(memory_space=pl.ANY),
                scratch_shapes=[pltpu.VMEM((PACK_ROWS, EXPERT_BLOCK // 2, LANES), U32), pltpu.SemaphoreType.DMA(())],
            ),
            out_shape=jax.ShapeDtypeStruct((PACK_ROWS, n_rows, LANES), U32),
            input_output_aliases={2: 0},
            compiler_params=cparams(dimension_semantics=("arbitrary",)),
            name="padfill",
        )(start_pad + counts, padded - counts, xs)

        experts = jnp.arange(n_exp, dtype=I32)
        used = counts > 0
        block_valid = jnp.clip(jnp.sum(jnp.where(block_expert[:, None] == experts[None, :],
                                                 (start_pad + counts)[None, :], 0), axis=1) - block_start,
                               0, EXPERT_BLOCK)
        ordinal = jnp.cumsum(used.astype(I32)) - 1
        next_used = jnp.min(jnp.where(jnp.logical_and(experts[None, :] > experts[:, None], used[None, :]),
                                      experts[None, :], n_exp), axis=1)
        next_used = jnp.where(next_used == n_exp, -1, next_used)
        per_block = lambda table: jnp.sum(jnp.where(block_expert[:, None] == experts[None, :], table[None, :], 0), axis=1)
        b_spec = lambda shape: pl.BlockSpec(shape, lambda i, be, *_: (be[i], 0, 0))
        ys = pl.pallas_call(
            _experts_kernel,
            grid_spec=pltpu.PrefetchScalarGridSpec(
                num_scalar_prefetch=5,
                grid=(n_blocks,),
                in_specs=[plane_spec(EXPERT_BLOCK, lambda i, be, nb, *_: jnp.minimum(i, nb[0] - 1)),
                          pl.BlockSpec(memory_space=pl.ANY), b_spec((1, 1, 2 * de)),
                          pl.BlockSpec(memory_space=pl.ANY), b_spec((1, 1, D))],
                out_specs=plane_spec(EXPERT_BLOCK, lambda i, *_: i),
                scratch_shapes=[pltpu.VMEM((2, D, 2 * de), F32), pltpu.VMEM((2, de, D), F32),
                                pltpu.VMEM((D, 2 * de), BF16), pltpu.VMEM((de, D), BF16),
                                pltpu.SemaphoreType.DMA((2,)), pltpu.SemaphoreType.DMA((2,))],
            ),
            out_shape=jax.ShapeDtypeStruct((PACK_ROWS, n_rows, LANES), U32),
            compiler_params=cparams(dimension_semantics=("arbitrary",)),
            name="experts",
        )(block_expert, n_used, block_valid, per_block(ordinal % 2), per_block(next_used), xs, w_gate_up, bgu, w_down, bdn)

        y4 = _sc_gather_rows(ys, dest.reshape(-1, SC_CHUNK))

        operands = [x1, gate, y4, y4, y4, y4] + ([] if out is None else [out])
        out = pl.pallas_call(
            _combine_kernel,
            grid=(steps,),
            in_specs=([row_spec(D), row_spec(LANES)]
                      + [plane_spec(tm, lambda i, kk=kk: kk * steps + i) for kk in range(TOP_K)]
                      + ([] if out is None else [pl.BlockSpec(memory_space=pl.ANY)])),
            out_specs=grp_spec(D),
            out_shape=jax.ShapeDtypeStruct((N, D), F32),
            input_output_aliases={} if out is None else {len(operands) - 1: 0},
            compiler_params=cparams(dimension_semantics=("arbitrary",)),
            name="combine",
        )(*operands)
    return out.reshape(B, S, D)


def kernel(x, norm1_g, w_in, q_norm_g, k_norm_g, w_pool_grp, pool_scale, w_pool_up, w_attn_up, w_out, norm2_g,
           w_router, b_router, w_gate_up, b_gate_up, w_down, b_down):
    for layer in range(norm1_g.shape[0]):
        x = _layer(x, norm1_g[layer], w_in[layer], q_norm_g[layer], k_norm_g[layer], w_pool_grp[layer],
                   pool_scale[layer], w_pool_up[layer], w_attn_up[layer], w_out[layer], norm2_g[layer],
                   w_router[layer], b_router[layer], w_gate_up[layer], b_gate_up[layer], w_down[layer],
                   b_down[layer])
    return x
```

```python
import functools

import jax
import jax.numpy as jnp
from jax import lax
from jax.experimental import pallas as pl
from jax.experimental.pallas import tpu as pltpu
from jax.experimental.pallas import tpu_sc as plsc

F32 = jnp.float32
BF16 = jnp.bfloat16
U32 = jnp.uint32
I32 = jnp.int32

EPS = 1e-6
POOL_WINDOWS = (2, 4, 8, 16)
POOL_GROUP_DIM = 128
POOL_HALO = 16
SB_HEAD_DIM = 64
TOP_K = 4
SWIGLU_LIMIT = 7.0
SWIGLU_ALPHA = 1.702
EXPERT_BLOCK = 768
EXPERT_SPLIT = 6
LANES = 128
PACK_ROWS = 4
ATTN_BLOCK = 128
ATTN_SUB = 2
ATTN_EXIT_BITS = 70.0
LOG2_E = 1.4426950408889634
VMEM_LIMIT = 56 * 1024 * 1024
SC_CHUNK = 128
MOE_GROUPS = 2


def _dot(a, b):
    return jnp.dot(a, b, preferred_element_type=F32)


def _split_bf16(x):
    hi = x.astype(BF16)
    lo = (x - hi.astype(F32)).astype(BF16)
    return hi, lo


def _pack_rows(v, out_ref):
    half = v.shape[1] // 2
    lo = lax.bitcast_convert_type(v[:, :half].astype(BF16).astype(F32), U32) >> 16
    hi = lax.bitcast_convert_type(v[:, half:].astype(BF16).astype(F32), U32) & jnp.uint32(0xFFFF0000)
    w = lo | hi
    for c in range(PACK_ROWS):
        out_ref[c] = w[:, c * LANES:(c + 1) * LANES]


def _unpack_rows(ref):
    los, his = [], []
    for c in range(PACK_ROWS):
        w = ref[c]
        los.append(lax.bitcast_convert_type(w << 16, F32))
        his.append(lax.bitcast_convert_type(w & jnp.uint32(0xFFFF0000), F32))
    return jnp.concatenate(los + his, axis=1)


def _mixer_in_kernel(x_ref, g1_ref, win_ref, gq_ref, gk_ref, hsum_ref, wgrp_ref, pscale_ref, wpu_ref,
                     q_ref, k_ref, v_ref, p_ref, tail_ref, *, tm, pw, sw):
    i = pl.program_id(1)

    @pl.when(i == 0)
    def _():
        tail_ref[...] = jnp.zeros_like(tail_ref)

    x = x_ref[...]
    ms = jnp.mean(x * x, axis=-1, keepdims=True)
    h = (x * lax.rsqrt(ms + EPS) * g1_ref[...]).astype(BF16)

    def head_norm(t, gain):
        ss = _dot((t * t).astype(BF16), hsum_ref[...])
        return t * lax.rsqrt(ss * (1.0 / SB_HEAD_DIM) + EPS) * gain

    def project_q():
        q = _dot(h, win_ref[:, pw:pw + sw])
        q_ref[...] = (head_norm(q, gq_ref[...]) * (SB_HEAD_DIM ** -0.5 * LOG2_E)).astype(BF16)

    def project_k():
        k = _dot(h, win_ref[:, pw + sw:pw + 2 * sw])
        k_ref[...] = head_norm(k, gk_ref[...]).astype(BF16)

    def project_v():
        v_ref[...] = _dot(h, win_ref[:, pw + 2 * sw:pw + 3 * sw]).astype(BF16)

    d_model = x.shape[1]
    pool_gate = []

    def project_pool_gate():
        pool_gate.append(jax.nn.sigmoid(_dot(h, win_ref[:, pw + 3 * sw:pw + 3 * sw + d_model])))

    u = _dot(h, win_ref[:, 0:pw])
    xx = jnp.concatenate([tail_ref[...], u], axis=0)
    tail_ref[...] = u[tm - POOL_HALO:, :]
    pos = i * tm + lax.broadcasted_iota(I32, (tm, POOL_GROUP_DIM), 0)
    mixed = []
    for (g, w), project in zip(enumerate(POOL_WINDOWS), (project_q, project_k, project_v, project_pool_gate)):
        project()
        s = xx[:, g * POOL_GROUP_DIM:(g + 1) * POOL_GROUP_DIM]
        step = 1
        while step < w:
            s = s + pltpu.roll(s, step, axis=0)
            step *= 2
        count = jnp.minimum(pos + 1, w).astype(F32)
        ug = u[:, g * POOL_GROUP_DIM:(g + 1) * POOL_GROUP_DIM]
        d = s[POOL_HALO:, :] / count - ug
        mixed.append(_dot(d.astype(BF16), wgrp_ref[g]))
    pm = jnp.concatenate(mixed, axis=1) * pscale_ref[...]
    pool_out = _dot(pm.astype(BF16), wpu_ref[...])
    p_ref[...] = (pool_gate[0] * pool_out).astype(BF16)


def _attn_kernel(q_ref, k_hbm, v_hbm, tri_ref, o_ref, kbuf, vbuf, ksem, vsem, *scratch, n_pairs):
    units = [(sub, p) for sub in range(ATTN_SUB) for p in range(n_pairs)]
    n_units = len(units)
    qs, acc, rr = scratch[:n_units], scratch[n_units:2 * n_units], scratch[2 * n_units:]
    bq = ATTN_BLOCK
    batch = pl.program_id(0)
    seq = kbuf.shape[1]

    def kv_copies(b):
        rows = pl.ds(pl.multiple_of(b * seq, seq), seq)
        return (pltpu.make_async_copy(k_hbm.at[rows], kbuf.at[b % 2], ksem.at[b % 2]),
                pltpu.make_async_copy(v_hbm.at[rows], vbuf.at[b % 2], vsem.at[b % 2]))

    @pl.when(pl.program_id(1) == 0)
    def _():
        @pl.when(batch == 0)
        def _():
            for c in kv_copies(batch):
                c.start()

        for c in kv_copies(batch):
            c.wait()

        @pl.when(batch + 1 < pl.num_programs(0))
        def _():
            for c in kv_copies(batch + 1):
                c.start()

    k_ref = kbuf.at[batch % 2]
    v_ref = vbuf.at[batch % 2]
    first_block = pl.program_id(1) * ATTN_SUB
    first_head = lax.broadcasted_iota(I32, (bq, LANES), 1) < SB_HEAD_DIM
    for u, (sub, p) in enumerate(units):
        q2 = q_ref[sub * bq:(sub + 1) * bq, p * LANES:(p + 1) * LANES]
        qs[u][:bq] = jnp.where(first_head, q2, jnp.zeros_like(q2))
        qs[u][bq:] = jnp.where(first_head, jnp.zeros_like(q2), q2)
    row = lax.broadcasted_iota(I32, (2 * bq, bq), 0)
    col = lax.broadcasted_iota(I32, (2 * bq, bq), 1)
    causal = col < (row & (bq - 1))
    contract_last = (((1,), (1,)), ((), ()))

    def softplus(z):
        return jnp.maximum(z, 0.0) + jnp.log2(1.0 + jnp.exp2(-jnp.abs(z)))

    def suffix_sums(sp):
        return _dot(sp.astype(BF16), tri_ref[...])

    def cols(ref, block, p):
        start = pl.multiple_of(jnp.maximum(block, 0) * bq, bq)
        return ref[pl.ds(start, bq), p * LANES:(p + 1) * LANES]

    def scores(u, block):
        return lax.dot_general(qs[u][...], cols(k_ref, block, units[u][1]), contract_last, preferred_element_type=F32)

    diag = [first_block + sub for sub, _ in units]
    z_d, z_n, s_d, s_n = {}, {}, {}, {}

    def stage_scores(u):
        z_d[u] = scores(u, diag[u])
        z_n[u] = scores(u, diag[u] - 1)

    def stage_sums(u):
        s_d[u] = suffix_sums(jnp.where(causal, softplus(z_d[u]), 0.0))
        s_n[u] = suffix_sums(jnp.where(diag[u] >= 1, softplus(z_n[u]), 0.0))

    def stage_values(u):
        p = units[u][1]
        a_d = jnp.where(causal, jnp.exp2(z_d[u] - s_d[u][:, :bq]), 0.0)
        r_d = s_d[u][:, bq:]
        a_n = jnp.where(diag[u] >= 1, jnp.exp2(z_n[u] - (r_d + s_n[u][:, :bq])), 0.0)
        acc[u][...] = (_dot(a_d.astype(BF16), cols(v_ref, diag[u], p)) + _dot(a_n.astype(BF16), cols(v_ref, diag[u] - 1, p)))
        rr[u][...] = r_d + s_n[u][:, bq:]

    for t in range(n_units + 2):
        if t < n_units:
            stage_scores(t)
        if 0 <= t - 1 < n_units:
            stage_sums(t - 1)
        if 0 <= t - 2 < n_units:
            stage_values(t - 2)

    def r_min():
        m = rr[0][...]
        for u in range(1, n_units):
            m = jnp.minimum(m, rr[u][...])
        return jnp.min(m)

    def cond(c):
        back, rm = c
        return jnp.logical_and(diag[-1] - back >= 0, rm < ATTN_EXIT_BITS)

    def body(c):
        back, _ = c
        blocks = [d - back for d in diag]
        zs = [scores(u, blocks[u]) for u in range(n_units)]
        ss = [suffix_sums(jnp.where(blocks[u] >= 0, softplus(zs[u]), 0.0)) for u in range(n_units)]
        for u, (_, p) in enumerate(units):
            r = rr[u][...]
            a = jnp.where(blocks[u] >= 0, jnp.exp2(zs[u] - (r + ss[u][:, :bq])), 0.0)
            acc[u][...] += _dot(a.astype(BF16), cols(v_ref, blocks[u], p))
            rr[u][...] = r + ss[u][:, bq:]
        return back + 1, r_min()

    lax.while_loop(cond, body, (2, r_min()))
    for u, (sub, p) in enumerate(units):
        o_ref[sub * bq:(sub + 1) * bq, p * LANES:(p + 1) * LANES] = (
            jnp.where(first_head, acc[u][:bq], acc[u][bq:]).astype(BF16))


def _mixer_out_kernel(x_ref, p_ref, sba_ref, g1_ref, wga_ref, wau_ref, wout_ref, g2_ref, wr_hi_ref, wr_lo_ref, br_ref,
                      ltri_ref, x1_ref, hp_ref, ri_ref, gate_ref, cnt_ref, logit_ref, *, tm):
    step = pl.program_id(0)

    @pl.when(step == 0)
    def _():
        cnt_ref[...] = jnp.zeros_like(cnt_ref)
        logit_ref[1] = jnp.zeros((tm, LANES), F32)

    logits = logit_ref[(step + 1) % 2]
    routed = jnp.where(step >= 1, 1.0, 0.0)
    lane = lax.broadcasted_iota(I32, logits.shape, 1).astype(F32)
    work = logits
    vals, idxs = [], []

    def topk_round():
        nonlocal work
        m = jnp.max(work, axis=-1, keepdims=True)
        ik = jnp.min(jnp.where(work == m, lane, float(LANES)), axis=-1, keepdims=True)
        vals.append(m)
        idxs.append(ik)
        work = jnp.where(lane == ik, -jnp.inf, work)

    d_model = x_ref.shape[1]
    quarter = d_model // TOP_K
    topk_round()
    x = x_ref[...]
    h1 = (x * lax.rsqrt(jnp.mean(x * x, axis=-1, keepdims=True) + EPS) * g1_ref[...]).astype(BF16)
    attn_gate = jax.nn.sigmoid(_dot(h1, wga_ref[...]))
    attn_out = _dot(sba_ref[...], wau_ref[...])
    merged = (p_ref[...].astype(F32) + attn_gate * attn_out).astype(BF16)
    x1_parts = []
    for c in range(TOP_K):
        if c >= 1:
            topk_round()
        cols = slice(c * quarter, (c + 1) * quarter)
        x1_parts.append(x[:, cols] + _dot(merged, wout_ref[:, cols]))
    x1 = jnp.concatenate(x1_parts, axis=1)
    x1_ref[...] = x1

    es = [jnp.exp(v - vals[0]) for v in vals]
    denom = es[0] + es[1] + es[2] + es[3]
    hot = jnp.zeros(logits.shape, F32)
    for ik in idxs:
        hot = hot + jnp.where(lane == ik, routed, 0.0)
    before = _dot(ltri_ref[...], hot.astype(BF16)) + cnt_ref[0:1, :]

    ms = jnp.mean(x1 * x1, axis=-1, keepdims=True)
    h2 = x1 * lax.rsqrt(ms + EPS) * g2_ref[...]
    _pack_rows(h2, hp_ref)

    ri = jnp.zeros(logits.shape, F32)
    gt = jnp.zeros(logits.shape, F32)
    for kk in range(TOP_K):
        rank = jnp.sum(jnp.where(lane == idxs[kk], before, 0.0), axis=-1, keepdims=True)
        ri = jnp.where(lane == kk, idxs[kk], ri)
        ri = jnp.where(lane == TOP_K + kk, rank, ri)
        gt = jnp.where(lane == kk, es[kk] / denom, gt)
    ri_ref[...] = ri.T[:2 * TOP_K].astype(I32)
    gate_ref[...] = gt
    cnt_ref[...] = cnt_ref[...] + jnp.sum(hot, axis=0, keepdims=True)

    h_hi, h_lo = _split_bf16(h2)
    logit_ref[step % 2] = (_dot(h_hi, wr_hi_ref[...]) + _dot(h_hi, wr_lo_ref[...]) + _dot(h_lo, wr_hi_ref[...])
                           + br_ref[...])


def _dest_kernel(start_ref, ri_ref, dest_ref):
    idx = ri_ref[:TOP_K, :]
    dest = ri_ref[TOP_K:, :]
    for e in range(start_ref.shape[0]):
        dest = dest + jnp.where(idx == e, start_ref[e], 0)
    dest_ref[...] = dest


def _sc_mesh():
    info = plsc.get_sparse_core_info()
    mesh = plsc.VectorSubcoreMesh(core_axis_name="c", subcore_axis_name="s")
    return mesh, info.num_cores, info.num_subcores, info.num_lanes


def _sc_scatter_rows(rows, idx, n_out):
    mesh, nc, ns, lanes = _sc_mesh()
    nw = nc * ns
    planes, m, width = rows.shape
    ch = SC_CHUNK
    n_ch = m // ch // nw
    half = planes // 2
    assert planes % 2 == 0 and n_ch * ch * nw == m and idx.shape == (m // ch * TOP_K, ch) and n_ch >= 2
    assert n_ch % 8 == 0

    @functools.partial(
        pl.kernel, mesh=mesh, out_type=jax.ShapeDtypeStruct((planes * n_out, width), rows.dtype),
        scratch_types=([pltpu.VMEM((n_ch * TOP_K, ch), I32)] + [pltpu.VMEM((ch, width), rows.dtype)] * planes
                       + [pltpu.VMEM((TOP_K, ch), I32)] * planes
                       + [pltpu.SemaphoreType.DMA((planes,)), pltpu.SemaphoreType.DMA((planes,))]))
    def scatter_kernel(rows_hbm, idx_hbm, out_hbm, idx_v, *rest):
        bufs, ibufs, rsem, ssem = rest[:planes], rest[planes:2 * planes], rest[2 * planes], rest[2 * planes + 1]
        wid = lax.axis_index("s") * nc + lax.axis_index("c")
        for kk in range(TOP_K):
            pltpu.sync_copy(idx_hbm.at[pl.ds(kk * (m // ch) + wid * n_ch, n_ch)], idx_v.at[pl.ds(kk * n_ch, n_ch)])
        base = wid * n_ch * ch

        def read(j, b):
            return pltpu.make_async_copy(rows_hbm.at[b, pl.ds(pl.multiple_of(base + j * ch, ch), ch)], bufs[b], rsem.at[b])

        def scatters(b):
            return [pltpu.make_async_copy(bufs[b], out_hbm.at[ibufs[b].at[kk]], ssem.at[b]) for kk in range(TOP_K)]

        def start_scatters(j, b):
            for kk in range(TOP_K):
                for t in range(0, ch, lanes):
                    ibufs[b][kk, pl.ds(t, lanes)] = idx_v[kk * n_ch + j, pl.ds(t, lanes)] + b * n_out
            for c in scatters(b):
                c.start()

        def wait_scatters(b):
            for c in scatters(b):
                c.wait()

        def finish(j, b):
            pj, pb = (j, b - half) if b >= half else (j - 1, b + half)
            read(pj, pb).wait()
            start_scatters(pj, pb)

        for b in range(planes):
            read(0, b).start()
        for b in range(half, planes):
            finish(0, b)

        @pl.loop(1, n_ch)
        def _(j):
            for b in range(planes):
                wait_scatters(b)
                read(j, b).start()
                finish(j, b)

        for b in range(half):
            finish(n_ch, b)
        for b in range(planes):
            wait_scatters(b)

    return scatter_kernel(rows, idx).reshape(planes, n_out, width)


def _padfill_kernel(start_ref, len_ref, xs_in_ref, xs_ref, zeros_ref, sem):
    del xs_in_ref
    zeros_ref[...] = jnp.zeros_like(zeros_ref)
    bits = [1 << b for b in reversed(range((EXPERT_BLOCK - 1).bit_length()))]

    def pieces(e):
        n = len_ref[e]
        for bit in bits:
            row0 = start_ref[e] + (n & ~(2 * bit - 1))
            copy = pltpu.make_async_copy(zeros_ref.at[:, pl.ds(0, bit)], xs_ref.at[:, pl.ds(row0, bit)], sem)
            yield (n & bit) != 0, copy

    def start(e, c):
        for on, copy in pieces(e):
            pl.when(on)(copy.start)
        return c

    def wait(e, c):
        for on, copy in pieces(e):
            pl.when(on)(copy.wait)
        return c

    lax.fori_loop(0, start_ref.shape[0], start, 0)
    lax.fori_loop(0, start_ref.shape[0], wait, 0)


def _experts_kernel(be_ref, nb_ref, valid_ref, slot_ref, next_ref, xs_ref, wgu_hbm, bgu_ref, wd_hbm, bd_ref, ys_ref,
                    wgu_buf, wd_buf, wgu_bf_ref, wd_bf_ref, gu_sem, d_sem):
    del nb_ref
    blk = pl.program_id(0)
    de = wd_hbm.shape[1]
    valid = valid_ref[blk]
    expert = be_ref[blk]
    slot = slot_ref[blk]

    def weight_copies(e, s):
        return (pltpu.make_async_copy(wgu_hbm.at[e], wgu_buf.at[s], gu_sem.at[s]),
                pltpu.make_async_copy(wd_hbm.at[e], wd_buf.at[s], d_sem.at[s]))

    first_of_expert = jnp.logical_or(blk == 0, expert != be_ref[jnp.maximum(blk - 1, 0)])

    @pl.when(jnp.logical_and(valid > 0, first_of_expert))
    def _():
        @pl.when(blk == 0)
        def _():
            for c in weight_copies(expert, slot):
                c.start()

        for c in weight_copies(expert, slot):
            c.wait()
        wgu_bf_ref[...] = wgu_buf[slot].astype(BF16)
        wd_bf_ref[...] = wd_buf[slot].astype(BF16)

        @pl.when(next_ref[blk] >= 0)
        def _():
            for c in weight_copies(next_ref[blk], 1 - slot):
                c.start()

    def mlp(m):
        x = _unpack_rows(xs_ref.at[:, pl.ds(0, m)]).astype(BF16)
        gu = _dot(x, wgu_bf_ref[...]) + bgu_ref[0]
        glu = jnp.minimum(gu[:, :de], SWIGLU_LIMIT)
        lin = jnp.clip(gu[:, de:], -SWIGLU_LIMIT, SWIGLU_LIMIT)
        act = glu * jax.nn.sigmoid(SWIGLU_ALPHA * glu) * (lin + 1.0)
        y = _dot(act.astype(BF16), wd_bf_ref[...]) + bd_ref[0]
        _pack_rows(y, ys_ref.at[:, pl.ds(0, m)])
        if m < EXPERT_BLOCK:
            ys_ref[:, m:, :] = jnp.zeros((PACK_ROWS, EXPERT_BLOCK - m, LANES), U32)

    step = EXPERT_BLOCK // EXPERT_SPLIT
    for q in range(EXPERT_SPLIT):
        pl.when(jnp.logical_and(valid > q * step, valid <= (q + 1) * step))(functools.partial(mlp, (q + 1) * step))

    @pl.when(valid == 0)
    def _():
        ys_ref[...] = jnp.zeros_like(ys_ref)


def _sc_gather_rows(table, idx):
    mesh, nc, ns, lanes = _sc_mesh()
    nw = nc * ns
    planes, n_tab, width = table.shape
    n_idx_rows, ch = idx.shape
    n_ch = n_idx_rows // nw
    m = n_idx_rows * ch
    half = planes // 2
    assert planes % 2 == 0 and ch == SC_CHUNK and n_ch * nw == n_idx_rows and n_ch >= 2

    @functools.partial(
        pl.kernel, mesh=mesh, out_type=jax.ShapeDtypeStruct((planes, m, width), table.dtype),
        scratch_types=([pltpu.VMEM((n_ch, ch), I32)] + [pltpu.VMEM((ch, width), table.dtype)] * planes
                       + [pltpu.VMEM((8, ch), I32)] * planes
                       + [pltpu.SemaphoreType.DMA((planes,)), pltpu.SemaphoreType.DMA((planes,))]))
    def gather_kernel(table_hbm, idx_hbm, out_hbm, idx_v, *rest):
        bufs, ibufs, gsem, wsem = rest[:planes], rest[planes:2 * planes], rest[2 * planes], rest[2 * planes + 1]
        wid = lax.axis_index("s") * nc + lax.axis_index("c")
        pltpu.sync_copy(idx_hbm.at[pl.ds(wid * n_ch, n_ch)], idx_v)
        base = wid * n_ch * ch

        def gather(b):
            return pltpu.make_async_copy(table_hbm.at[ibufs[b].at[0]], bufs[b], gsem.at[b])

        def start_gather(j, b):
            for t in range(0, ch, lanes):
                ibufs[b][0, pl.ds(t, lanes)] = idx_v[j, pl.ds(t, lanes)] + b * n_tab
            gather(b).start()

        def write(j, b):
            return pltpu.make_async_copy(bufs[b], out_hbm.at[b, pl.ds(pl.multiple_of(base + j * ch, ch), ch)], wsem.at[b])

        def finish(j, b):
            pj, pb = (j, b - half) if b >= half else (j - 1, b + half)
            gather(pb).wait()
            write(pj, pb).start()

        for b in range(planes):
            start_gather(0, b)
        for b in range(half, planes):
            finish(0, b)

        @pl.loop(1, n_ch)
        def _(j):
            for b in range(planes):
                write(j - 1, b).wait()
                start_gather(j, b)
                finish(j, b)

        for b in range(half):
            finish(n_ch, b)
        for b in range(planes):
            write(n_ch - 1, b).wait()

    return gather_kernel(table.reshape(planes * n_tab, width), idx)


def _combine_kernel(x1_ref, gate_ref, y0_ref, y1_ref, y2_ref, y3_ref, *rest):
    o_ref = rest[-1]
    gate = gate_ref[...]
    out = x1_ref[...]
    for kk, y_ref in enumerate((y0_ref, y1_ref, y2_ref, y3_ref)):
        out = out + gate[:, kk:kk + 1] * _unpack_rows(y_ref)
    o_ref[...] = out


def _const_spec(shape):
    nd = len(shape)
    return pl.BlockSpec(shape, lambda *_: (0,) * nd)


def _layer(x, norm1_g, w_in, q_norm_g, k_norm_g, w_pool_grp, pool_scale, w_pool_up, w_attn_up, w_out, norm2_g,
           w_router, b_router, w_gate_up, b_gate_up, w_down, b_down):
    B, S, D = x.shape
    N = B * S
    pw = w_pool_up.shape[0]
    sw = w_attn_up.shape[0]
    n_exp = w_router.shape[1]
    de = w_down.shape[1]
    heads = sw // SB_HEAD_DIM
    assert pw == len(POOL_WINDOWS) * POOL_GROUP_DIM and heads % 2 == 0 and n_exp <= LANES
    assert D == 2 * PACK_ROWS * LANES and w_in.shape[1] == pw + 3 * sw + 2 * D
    tm = 512 if S % 512 == 0 else 256
    assert S % tm == 0 and S % (ATTN_SUB * ATTN_BLOCK) == 0
    xf = x.reshape(N, D)
    cparams = functools.partial(pltpu.CompilerParams, vmem_limit_bytes=VMEM_LIMIT)

    hsum = (jnp.arange(sw)[:, None] // SB_HEAD_DIM == jnp.arange(sw)[None, :] // SB_HEAD_DIM).astype(BF16)
    tm_in = 2 * tm if S % (2 * tm) == 0 else tm
    nt = S // tm_in
    tok_spec = lambda w: pl.BlockSpec((tm_in, w), lambda b, i: (b * nt + i, 0))
    w_in_bf = w_in.astype(BF16)
    n_in = pw + 3 * sw + D
    g1 = norm1_g.reshape(1, D)
    q2, k2, v2, pg = pl.pallas_call(
        functools.partial(_mixer_in_kernel, tm=tm_in, pw=pw, sw=sw),
        grid=(B, nt),
        in_specs=[tok_spec(D), _const_spec((1, D)), _const_spec((D, n_in)), _const_spec((1, sw)),
                  _const_spec((1, sw)), _const_spec((sw, sw)),
                  _const_spec((len(POOL_WINDOWS), POOL_GROUP_DIM, POOL_GROUP_DIM)), _const_spec((1, pw)),
                  _const_spec((pw, D))],
        out_specs=[tok_spec(sw), tok_spec(sw), tok_spec(sw), tok_spec(D)],
        out_shape=[jax.ShapeDtypeStruct((N, sw), BF16)] * 3 + [jax.ShapeDtypeStruct((N, D), BF16)],
        scratch_shapes=[pltpu.VMEM((POOL_HALO, pw), F32)],
        compiler_params=cparams(dimension_semantics=("arbitrary", "arbitrary")),
        name="mixer_in",
    )(xf, g1, w_in_bf[:, :n_in], jnp.tile(q_norm_g, heads).reshape(1, sw),
      jnp.tile(k_norm_g, heads).reshape(1, sw), hsum, w_pool_grp.astype(BF16), pool_scale.reshape(1, pw),
      w_pool_up.astype(BF16))

    bq = ATTN_BLOCK
    rows_q = ATTN_SUB * bq
    nq = S // rows_q
    jj = jnp.arange(bq)
    tri = jnp.concatenate([(jj[:, None] >= jj[None, :]).astype(BF16), jnp.ones((bq, bq), BF16)], axis=1)
    n_pairs = heads // 2
    n_units = ATTN_SUB * n_pairs
    kv_spec = pl.BlockSpec(memory_space=pl.ANY)
    sba = pl.pallas_call(
        functools.partial(_attn_kernel, n_pairs=n_pairs),
        grid=(B, nq),
        in_specs=[pl.BlockSpec((rows_q, sw), lambda b, qi: (b * nq + qi, 0)), kv_spec, kv_spec,
                  _const_spec((bq, 2 * bq))],
        out_specs=pl.BlockSpec((rows_q, sw), lambda b, qi: (b * nq + qi, 0)),
        out_shape=jax.ShapeDtypeStruct((N, sw), BF16),
        scratch_shapes=([pltpu.VMEM((2, S, sw), BF16), pltpu.VMEM((2, S, sw), BF16),
                         pltpu.SemaphoreType.DMA((2,)), pltpu.SemaphoreType.DMA((2,))]
                        + [pltpu.VMEM((2 * bq, LANES), BF16)] * n_units + [pltpu.VMEM((2 * bq, LANES), F32)] * (2 * n_units)),
        compiler_params=cparams(dimension_semantics=("arbitrary", "arbitrary")),
        name="sb_attn",
    )(q2, k2, v2, tri)

    wr = jnp.zeros((D, LANES), F32).at[:, :n_exp].set(w_router)
    wr_hi = wr.astype(BF16)
    wr_lo = (wr - wr_hi.astype(F32)).astype(BF16)
    br = jnp.full((1, LANES), -jnp.inf, F32).at[0, :n_exp].set(b_router)
    ltri = (jnp.arange(tm)[:, None] > jnp.arange(tm)[None, :]).astype(BF16)
    assert N % (MOE_GROUPS * tm) == 0
    ng = N // MOE_GROUPS
    steps = ng // tm
    n_assign = ng * TOP_K
    n_blocks = -(-(n_assign + n_exp * (EXPERT_BLOCK - 1)) // EXPERT_BLOCK)
    n_rows = n_blocks * EXPERT_BLOCK
    plane_spec = lambda rows, index: pl.BlockSpec((PACK_ROWS, rows, LANES), lambda i, *_: (0, index(i, *_), 0))
    row_spec = lambda w: pl.BlockSpec((tm, w), lambda i: (i, 0))
    w_attn_up_bf, w_out_bf, g2 = w_attn_up.astype(BF16), w_out.astype(BF16), norm2_g.reshape(1, D)
    bgu, bdn = b_gate_up.reshape(n_exp, 1, 2 * de), b_down.reshape(n_exp, 1, D)
    out = None
    for grp in range(MOE_GROUPS):
        grp_spec = lambda w, first=grp * steps: pl.BlockSpec((tm, w), lambda i: (first + i, 0))

        cur = lambda i: jnp.minimum(i, steps - 1)
        prev = lambda i: jnp.maximum(i - 1, 0)
        in_spec = lambda w, first=grp * steps: pl.BlockSpec((tm, w), lambda i: (first + cur(i), 0))
        x1, hpk, ri, gate, cnt = pl.pallas_call(
            functools.partial(_mixer_out_kernel, tm=tm),
            grid=(steps + 1,),
            in_specs=[in_spec(D), in_spec(D), in_spec(sw), _const_spec((1, D)), _const_spec((D, D)),
                      _const_spec((sw, D)), _const_spec((D, D)),
                      _const_spec((1, D)), _const_spec((D, LANES)), _const_spec((D, LANES)), _const_spec((1, LANES)),
                      _const_spec((tm, tm))],
            out_specs=[pl.BlockSpec((tm, D), lambda i: (cur(i), 0)), plane_spec(tm, cur),
                       pl.BlockSpec((2 * TOP_K, tm), lambda i: (0, prev(i))),
                       pl.BlockSpec((tm, LANES), lambda i: (prev(i), 0)), _const_spec((8, LANES))],
            out_shape=[jax.ShapeDtypeStruct((ng, D), F32), jax.ShapeDtypeStruct((PACK_ROWS, ng, LANES), U32),
                       jax.ShapeDtypeStruct((2 * TOP_K, ng), I32), jax.ShapeDtypeStruct((ng, LANES), F32),
                       jax.ShapeDtypeStruct((8, LANES), F32)],
            scratch_shapes=[pltpu.VMEM((2, tm, LANES), F32)],
            compiler_params=cparams(dimension_semantics=("arbitrary",)),
            name="mixer_out",
        )(xf, pg, sba, g1, w_in_bf[:, n_in:], w_attn_up_bf, w_out_bf, g2, wr_hi, wr_lo, br, ltri)

        counts = cnt[0, :n_exp].astype(I32)
        padded = (counts + EXPERT_BLOCK - 1) // EXPERT_BLOCK * EXPERT_BLOCK
        padded_end = jnp.cumsum(padded)
        start_pad = padded_end - padded
        td = min(ng, 8192)
        dest = pl.pallas_call(
            _dest_kernel,
            grid_spec=pltpu.PrefetchScalarGridSpec(
                num_scalar_prefetch=1,
                grid=(ng // td,),
                in_specs=[pl.BlockSpec((2 * TOP_K, td), lambda i, sp: (0, i))],
                out_specs=pl.BlockSpec((TOP_K, td), lambda i, sp: (0, i)),
            ),
            out_shape=jax.ShapeDtypeStruct((TOP_K, ng), I32),
            compiler_params=cparams(dimension_semantics=("arbitrary",)),
            name="dest",
        )(start_pad, ri).reshape(-1, SC_CHUNK)
        block_start = jnp.arange(n_blocks, dtype=I32) * EXPERT_BLOCK
        block_expert = jnp.minimum(jnp.sum((padded_end[None, :] <= block_start[:, None]).astype(I32), axis=1),
                                   n_exp - 1)
        n_used = (padded_end[-1] // EXPERT_BLOCK).astype(I32).reshape(1)

        xs = _sc_scatter_rows(hpk, dest, n_rows)
        xs = pl.pallas_call(
            _padfill_kernel,
            grid_spec=pltpu.PrefetchScalarGridSpec(
                num_scalar_prefetch=2,
                grid=(1,),
                in_specs=[pl.BlockSpec(memory_space=pl.ANY)],
                out_specs=pl.BlockSpec(memory_space=pl.ANY),
                scratch_shapes=[pltpu.VMEM((PACK_ROWS, 1 << ((EXPERT_BLOCK - 1).bit_length() - 1), LANES), U32),
                                pltpu.SemaphoreType.DMA(())],
            ),
            out_shape=jax.ShapeDtypeStruct((PACK_ROWS, n_rows, LANES), U32),
            input_output_aliases={2: 0},
            compiler_params=cparams(dimension_semantics=("arbitrary",)),
            name="padfill",
        )(start_pad + counts, padded - counts, xs)

        experts = jnp.arange(n_exp, dtype=I32)
        used = counts > 0
        block_valid = jnp.clip(jnp.sum(jnp.where(block_expert[:, None] == experts[None, :],
                                                 (start_pad + counts)[None, :], 0), axis=1) - block_start,
                               0, EXPERT_BLOCK)
        ordinal = jnp.cumsum(used.astype(I32)) - 1
        next_used = jnp.min(jnp.where(jnp.logical_and(experts[None, :] > experts[:, None], used[None, :]),
                                      experts[None, :], n_exp), axis=1)
        next_used = jnp.where(next_used == n_exp, -1, next_used)
        per_block = lambda table: jnp.sum(jnp.where(block_expert[:, None] == experts[None, :], table[None, :], 0), axis=1)
        b_spec = lambda shape: pl.BlockSpec(shape, lambda i, be, *_: (be[i], 0, 0))
        ys = pl.pallas_call(
            _experts_kernel,
            grid_spec=pltpu.PrefetchScalarGridSpec(
                num_scalar_prefetch=5,
                grid=(n_blocks,),
                in_specs=[plane_spec(EXPERT_BLOCK, lambda i, be, nb, *_: jnp.minimum(i, nb[0] - 1)),
                          pl.BlockSpec(memory_space=pl.ANY), b_spec((1, 1, 2 * de)),
                          pl.BlockSpec(memory_space=pl.ANY), b_spec((1, 1, D))],
                out_specs=plane_spec(EXPERT_BLOCK, lambda i, *_: i),
                scratch_shapes=[pltpu.VMEM((2, D, 2 * de), F32), pltpu.VMEM((2, de, D), F32),
                                pltpu.VMEM((D, 2 * de), BF16), pltpu.VMEM((de, D), BF16),
                                pltpu.SemaphoreType.DMA((2,)), pltpu.SemaphoreType.DMA((2,))],
            ),
            out_shape=jax.ShapeDtypeStruct((PACK_ROWS, n_rows, LANES), U32),
            compiler_params=cparams(dimension_semantics=("arbitrary",)),
            name="experts",
        )(block_expert, n_used, block_valid, per_block(ordinal % 2), per_block(next_used), xs, w_gate_up, bgu, w_down, bdn)

        y4 = _sc_gather_rows(ys, dest.reshape(-1, SC_CHUNK))

        operands = [x1, gate, y4, y4, y4, y4] + ([] if out is None else [out])
        out = pl.pallas_call(
            _combine_kernel,
            grid=(steps,),
            in_specs=([row_spec(D), row_spec(LANES)]
                      + [plane_spec(tm, lambda i, kk=kk: kk * steps + i) for kk in range(TOP_K)]
                      + ([] if out is None else [pl.BlockSpec(memory_space=pl.ANY)])),
            out_specs=grp_spec(D),
            out_shape=jax.ShapeDtypeStruct((N, D), F32),
            input_output_aliases={} if out is None else {len(operands) - 1: 0},
            compiler_params=cparams(dimension_semantics=("arbitrary",)),
            name="combine",
        )(*operands)
    return out.reshape(B, S, D)


def kernel(x, norm1_g, w_in, q_norm_g, k_norm_g, w_pool_grp, pool_scale, w_pool_up, w_attn_up, w_out, norm2_g,
           w_router, b_router, w_gate_up, b_gate_up, w_down, b_down):
    for layer in range(norm1_g.shape[0]):
        x = _layer(x, norm1_g[layer], w_in[layer], q_norm_g[layer], k_norm_g[layer], w_pool_grp[layer],
                   pool_scale[layer], w_pool_up[layer], w_attn_up[layer], w_out[layer], norm2_g[layer],
                   w_router[layer], b_router[layer], w_gate_up[layer], b_gate_up[layer], w_down[layer],
                   b_down[layer])
    return x
```

```python
import functools

import jax
import jax.numpy as jnp
from jax import lax
from jax.experimental import pallas as pl
from jax.experimental.pallas import tpu as pltpu
from jax.experimental.pallas import tpu_sc as plsc

F32 = jnp.float32
BF16 = jnp.bfloat16
U32 = jnp.uint32
I32 = jnp.int32

EPS = 1e-6
POOL_WINDOWS = (2, 4, 8, 16)
POOL_GROUP_DIM = 128
POOL_HALO = 16
SB_HEAD_DIM = 64
TOP_K = 4
SWIGLU_LIMIT = 7.0
SWIGLU_ALPHA = 1.702
EXPERT_BLOCK = 1024
EXPERT_SPLIT = 8
LANES = 128
PACK_ROWS = 4
ATTN_BLOCK = 128
ATTN_SUB = 2
ATTN_EXIT_BITS = 70.0
LOG2_E = 1.4426950408889634
VMEM_LIMIT = 56 * 1024 * 1024
SC_CHUNK = 128
MOE_GROUPS = 2


def _dot(a, b):
    return jnp.dot(a, b, preferred_element_type=F32)


def _split_bf16(x):
    hi = x.astype(BF16)
    lo = (x - hi.astype(F32)).astype(BF16)
    return hi, lo


def _pack_rows(v, out_ref):
    half = v.shape[1] // 2
    lo = lax.bitcast_convert_type(v[:, :half].astype(BF16).astype(F32), U32) >> 16
    hi = lax.bitcast_convert_type(v[:, half:].astype(BF16).astype(F32), U32) & jnp.uint32(0xFFFF0000)
    w = lo | hi
    for c in range(PACK_ROWS):
        out_ref[c] = w[:, c * LANES:(c + 1) * LANES]


def _unpack_rows(ref):
    los, his = [], []
    for c in range(PACK_ROWS):
        w = ref[c]
        los.append(lax.bitcast_convert_type(w << 16, F32))
        his.append(lax.bitcast_convert_type(w & jnp.uint32(0xFFFF0000), F32))
    return jnp.concatenate(los + his, axis=1)


def _mixer_in_kernel(x_ref, g1_ref, win_ref, gq_ref, gk_ref, hsum_ref, wgrp_ref, pscale_ref, wpu_ref,
                     q_ref, k_ref, v_ref, p_ref, tail_ref, *, tm, pw, sw):
    i = pl.program_id(1)

    @pl.when(i == 0)
    def _():
        tail_ref[...] = jnp.zeros_like(tail_ref)

    x = x_ref[...]
    ms = jnp.mean(x * x, axis=-1, keepdims=True)
    h = (x * lax.rsqrt(ms + EPS) * g1_ref[...]).astype(BF16)

    def head_norm(t, gain):
        ss = _dot((t * t).astype(BF16), hsum_ref[...])
        return t * lax.rsqrt(ss * (1.0 / SB_HEAD_DIM) + EPS) * gain

    def project_q():
        q = _dot(h, win_ref[:, pw:pw + sw])
        q_ref[...] = (head_norm(q, gq_ref[...]) * (SB_HEAD_DIM ** -0.5 * LOG2_E)).astype(BF16)

    def project_k():
        k = _dot(h, win_ref[:, pw + sw:pw + 2 * sw])
        k_ref[...] = head_norm(k, gk_ref[...]).astype(BF16)

    def project_v():
        v_ref[...] = _dot(h, win_ref[:, pw + 2 * sw:pw + 3 * sw]).astype(BF16)

    d_model = x.shape[1]
    pool_gate = []

    def project_pool_gate():
        pool_gate.append(jax.nn.sigmoid(_dot(h, win_ref[:, pw + 3 * sw:pw + 3 * sw + d_model])))

    u = _dot(h, win_ref[:, 0:pw])
    xx = jnp.concatenate([tail_ref[...], u], axis=0)
    tail_ref[...] = u[tm - POOL_HALO:, :]
    pos = i * tm + lax.broadcasted_iota(I32, (tm, POOL_GROUP_DIM), 0)
    mixed = []
    for (g, w), project in zip(enumerate(POOL_WINDOWS), (project_q, project_k, project_v, project_pool_gate)):
        project()
        s = xx[:, g * POOL_GROUP_DIM:(g + 1) * POOL_GROUP_DIM]
        step = 1
        while step < w:
            s = s + pltpu.roll(s, step, axis=0)
            step *= 2
        count = jnp.minimum(pos + 1, w).astype(F32)
        ug = u[:, g * POOL_GROUP_DIM:(g + 1) * POOL_GROUP_DIM]
        d = s[POOL_HALO:, :] / count - ug
        mixed.append(_dot(d.astype(BF16), wgrp_ref[g]))
    pm = jnp.concatenate(mixed, axis=1) * pscale_ref[...]
    pool_out = _dot(pm.astype(BF16), wpu_ref[...])
    p_ref[...] = (pool_gate[0] * pool_out).astype(BF16)


def _attn_kernel(q_ref, k_hbm, v_hbm, tri_ref, o_ref, kbuf, vbuf, ksem, vsem, *scratch, n_pairs):
    units = [(sub, p) for sub in range(ATTN_SUB) for p in range(n_pairs)]
    n_units = len(units)
    qs, acc, rr = scratch[:n_units], scratch[n_units:2 * n_units], scratch[2 * n_units:]
    bq = ATTN_BLOCK
    batch = pl.program_id(0)
    seq = kbuf.shape[1]

    def kv_copies(b):
        rows = pl.ds(pl.multiple_of(b * seq, seq), seq)
        return (pltpu.make_async_copy(k_hbm.at[rows], kbuf.at[b % 2], ksem.at[b % 2]),
                pltpu.make_async_copy(v_hbm.at[rows], vbuf.at[b % 2], vsem.at[b % 2]))

    @pl.when(pl.program_id(1) == 0)
    def _():
        @pl.when(batch == 0)
        def _():
            for c in kv_copies(batch):
                c.start()

        for c in kv_copies(batch):
            c.wait()

        @pl.when(batch + 1 < pl.num_programs(0))
        def _():
            for c in kv_copies(batch + 1):
                c.start()

    k_ref = kbuf.at[batch % 2]
    v_ref = vbuf.at[batch % 2]
    first_block = pl.program_id(1) * ATTN_SUB
    first_head = lax.broadcasted_iota(I32, (bq, LANES), 1) < SB_HEAD_DIM
    for u, (sub, p) in enumerate(units):
        q2 = q_ref[sub * bq:(sub + 1) * bq, p * LANES:(p + 1) * LANES]
        qs[u][:bq] = jnp.where(first_head, q2, jnp.zeros_like(q2))
        qs[u][bq:] = jnp.where(first_head, jnp.zeros_like(q2), q2)
    row = lax.broadcasted_iota(I32, (2 * bq, bq), 0)
    col = lax.broadcasted_iota(I32, (2 * bq, bq), 1)
    causal = col < (row & (bq - 1))
    contract_last = (((1,), (1,)), ((), ()))

    def softplus(z):
        return jnp.maximum(z, 0.0) + jnp.log2(1.0 + jnp.exp2(-jnp.abs(z)))

    def suffix_sums(sp):
        return _dot(sp.astype(BF16), tri_ref[...])

    def cols(ref, block, p):
        start = pl.multiple_of(jnp.maximum(block, 0) * bq, bq)
        return ref[pl.ds(start, bq), p * LANES:(p + 1) * LANES]

    def scores(u, block):
        return lax.dot_general(qs[u][...], cols(k_ref, block, units[u][1]), contract_last, preferred_element_type=F32)

    diag = [first_block + sub for sub, _ in units]
    z_d, z_n, s_d, s_n = {}, {}, {}, {}

    def stage_scores(u):
        z_d[u] = scores(u, diag[u])
        z_n[u] = scores(u, diag[u] - 1)

    def stage_sums(u):
        s_d[u] = suffix_sums(jnp.where(causal, softplus(z_d[u]), 0.0))
        s_n[u] = suffix_sums(jnp.where(diag[u] >= 1, softplus(z_n[u]), 0.0))

    def stage_values(u):
        p = units[u][1]
        a_d = jnp.where(causal, jnp.exp2(z_d[u] - s_d[u][:, :bq]), 0.0)
        r_d = s_d[u][:, bq:]
        a_n = jnp.where(diag[u] >= 1, jnp.exp2(z_n[u] - (r_d + s_n[u][:, :bq])), 0.0)
        acc[u][...] = (_dot(a_d.astype(BF16), cols(v_ref, diag[u], p)) + _dot(a_n.astype(BF16), cols(v_ref, diag[u] - 1, p)))
        rr[u][...] = r_d + s_n[u][:, bq:]

    for t in range(n_units + 2):
        if t < n_units:
            stage_scores(t)
        if 0 <= t - 1 < n_units:
            stage_sums(t - 1)
        if 0 <= t - 2 < n_units:
            stage_values(t - 2)

    def r_min():
        m = rr[0][...]
        for u in range(1, n_units):
            m = jnp.minimum(m, rr[u][...])
        return jnp.min(m)

    def cond(c):
        back, rm = c
        return jnp.logical_and(diag[-1] - back >= 0, rm < ATTN_EXIT_BITS)

    def body(c):
        back, _ = c
        blocks = [d - back for d in diag]
        zs = [scores(u, blocks[u]) for u in range(n_units)]
        ss = [suffix_sums(jnp.where(blocks[u] >= 0, softplus(zs[u]), 0.0)) for u in range(n_units)]
        for u, (_, p) in enumerate(units):
            r = rr[u][...]
            a = jnp.where(blocks[u] >= 0, jnp.exp2(zs[u] - (r + ss[u][:, :bq])), 0.0)
            acc[u][...] += _dot(a.astype(BF16), cols(v_ref, blocks[u], p))
            rr[u][...] = r + ss[u][:, bq:]
        return back + 1, r_min()

    lax.while_loop(cond, body, (2, r_min()))
    for u, (sub, p) in enumerate(units):
        o_ref[sub * bq:(sub + 1) * bq, p * LANES:(p + 1) * LANES] = (
            jnp.where(first_head, acc[u][:bq], acc[u][bq:]).astype(BF16))


def _mixer_out_kernel(x_ref, p_ref, sba_ref, g1_ref, wga_ref, wau_ref, wout_ref, g2_ref, wr_hi_ref, wr_lo_ref, br_ref,
                      ltri_ref, x1_ref, hp_ref, ri_ref, gate_ref, cnt_ref, logit_ref, *, tm):
    step = pl.program_id(0)

    @pl.when(step == 0)
    def _():
        cnt_ref[...] = jnp.zeros_like(cnt_ref)
        logit_ref[1] = jnp.zeros((tm, LANES), F32)

    logits = logit_ref[(step + 1) % 2]
    routed = jnp.where(step >= 1, 1.0, 0.0)
    lane = lax.broadcasted_iota(I32, logits.shape, 1).astype(F32)
    work = logits
    vals, idxs = [], []

    def topk_round():
        nonlocal work
        m = jnp.max(work, axis=-1, keepdims=True)
        ik = jnp.min(jnp.where(work == m, lane, float(LANES)), axis=-1, keepdims=True)
        vals.append(m)
        idxs.append(ik)
        work = jnp.where(lane == ik, -jnp.inf, work)

    d_model = x_ref.shape[1]
    quarter = d_model // TOP_K
    topk_round()
    x = x_ref[...]
    h1 = (x * lax.rsqrt(jnp.mean(x * x, axis=-1, keepdims=True) + EPS) * g1_ref[...]).astype(BF16)
    attn_gate = jax.nn.sigmoid(_dot(h1, wga_ref[...]))
    attn_out = _dot(sba_ref[...], wau_ref[...])
    merged = (p_ref[...].astype(F32) + attn_gate * attn_out).astype(BF16)
    x1_parts = []
    for c in range(TOP_K):
        if c >= 1:
            topk_round()
        cols = slice(c * quarter, (c + 1) * quarter)
        x1_parts.append(x[:, cols] + _dot(merged, wout_ref[:, cols]))
    x1 = jnp.concatenate(x1_parts, axis=1)
    x1_ref[...] = x1

    es = [jnp.exp(v - vals[0]) for v in vals]
    denom = es[0] + es[1] + es[2] + es[3]
    hot = jnp.zeros(logits.shape, F32)
    for ik in idxs:
        hot = hot + jnp.where(lane == ik, routed, 0.0)
    before = _dot(ltri_ref[...], hot.astype(BF16)) + cnt_ref[0:1, :]

    ms = jnp.mean(x1 * x1, axis=-1, keepdims=True)
    h2 = x1 * lax.rsqrt(ms + EPS) * g2_ref[...]
    _pack_rows(h2, hp_ref)

    ri = jnp.zeros(logits.shape, F32)
    gt = jnp.zeros(logits.shape, F32)
    for kk in range(TOP_K):
        rank = jnp.sum(jnp.where(lane == idxs[kk], before, 0.0), axis=-1, keepdims=True)
        ri = jnp.where(lane == kk, idxs[kk], ri)
        ri = jnp.where(lane == TOP_K + kk, rank, ri)
        gt = jnp.where(lane == kk, es[kk] / denom, gt)
    ri_ref[...] = ri.T[:2 * TOP_K].astype(I32)
    gate_ref[...] = gt
    cnt_ref[...] = cnt_ref[...] + jnp.sum(hot, axis=0, keepdims=True)

    h_hi, h_lo = _split_bf16(h2)
    logit_ref[step % 2] = (_dot(h_hi, wr_hi_ref[...]) + _dot(h_hi, wr_lo_ref[...]) + _dot(h_lo, wr_hi_ref[...])
                           + br_ref[...])


def _dest_kernel(start_ref, ri_ref, dest_ref):
    idx = ri_ref[:TOP_K, :]
    dest = ri_ref[TOP_K:, :]
    for e in range(start_ref.shape[0]):
        dest = dest + jnp.where(idx == e, start_ref[e], 0)
    dest_ref[...] = dest


def _sc_mesh():
    info = plsc.get_sparse_core_info()
    mesh = plsc.VectorSubcoreMesh(core_axis_name="c", subcore_axis_name="s")
    return mesh, info.num_cores, info.num_subcores, info.num_lanes


def _sc_scatter_rows(rows, idx, n_out):
    mesh, nc, ns, lanes = _sc_mesh()
    nw = nc * ns
    planes, m, width = rows.shape
    ch = SC_CHUNK
    n_ch = m // ch // nw
    half = planes // 2
    assert planes % 2 == 0 and n_ch * ch * nw == m and idx.shape == (m // ch * TOP_K, ch) and n_ch >= 2
    assert n_ch % 8 == 0

    @functools.partial(
        pl.kernel, mesh=mesh, out_type=jax.ShapeDtypeStruct((planes * n_out, width), rows.dtype),
        scratch_types=([pltpu.VMEM((n_ch * TOP_K, ch), I32)] + [pltpu.VMEM((ch, width), rows.dtype)] * planes
                       + [pltpu.VMEM((TOP_K, ch), I32)] * planes
                       + [pltpu.SemaphoreType.DMA((planes,)), pltpu.SemaphoreType.DMA((planes,))]))
    def scatter_kernel(rows_hbm, idx_hbm, out_hbm, idx_v, *rest):
        bufs, ibufs, rsem, ssem = rest[:planes], rest[planes:2 * planes], rest[2 * planes], rest[2 * planes + 1]
        wid = lax.axis_index("s") * nc + lax.axis_index("c")
        for kk in range(TOP_K):
            pltpu.sync_copy(idx_hbm.at[pl.ds(kk * (m // ch) + wid * n_ch, n_ch)], idx_v.at[pl.ds(kk * n_ch, n_ch)])
        base = wid * n_ch * ch

        def read(j, b):
            return pltpu.make_async_copy(rows_hbm.at[b, pl.ds(pl.multiple_of(base + j * ch, ch), ch)], bufs[b], rsem.at[b])

        def scatters(b):
            return [pltpu.make_async_copy(bufs[b], out_hbm.at[ibufs[b].at[kk]], ssem.at[b]) for kk in range(TOP_K)]

        def start_scatters(j, b):
            for kk in range(TOP_K):
                for t in range(0, ch, lanes):
                    ibufs[b][kk, pl.ds(t, lanes)] = idx_v[kk * n_ch + j, pl.ds(t, lanes)] + b * n_out
            for c in scatters(b):
                c.start()

        def wait_scatters(b):
            for c in scatters(b):
                c.wait()

        def finish(j, b):
            pj, pb = (j, b - half) if b >= half else (j - 1, b + half)
            read(pj, pb).wait()
            start_scatters(pj, pb)

        for b in range(planes):
            read(0, b).start()
        for b in range(half, planes):
            finish(0, b)

        @pl.loop(1, n_ch)
        def _(j):
            for b in range(planes):
                wait_scatters(b)
                read(j, b).start()
                finish(j, b)

        for b in range(half):
            finish(n_ch, b)
        for b in range(planes):
            wait_scatters(b)

    return scatter_kernel(rows, idx).reshape(planes, n_out, width)


def _padfill_kernel(start_ref, len_ref, xs_in_ref, xs_ref, zeros_ref, sem):
    del xs_in_ref
    zeros_ref[...] = jnp.zeros_like(zeros_ref)
    bits = [1 << b for b in reversed(range((EXPERT_BLOCK - 1).bit_length()))]

    def pieces(e):
        n = len_ref[e]
        for bit in bits:
            row0 = start_ref[e] + (n & ~(2 * bit - 1))
            copy = pltpu.make_async_copy(zeros_ref.at[:, pl.ds(0, bit)], xs_ref.at[:, pl.ds(row0, bit)], sem)
            yield (n & bit) != 0, copy

    def start(e, c):
        for on, copy in pieces(e):
            pl.when(on)(copy.start)
        return c

    def wait(e, c):
        for on, copy in pieces(e):
            pl.when(on)(copy.wait)
        return c

    lax.fori_loop(0, start_ref.shape[0], start, 0)
    lax.fori_loop(0, start_ref.shape[0], wait, 0)


def _experts_kernel(be_ref, nb_ref, valid_ref, slot_ref, next_ref, xs_ref, wgu_hbm, bgu_ref, wd_hbm, bd_ref, ys_ref,
                    wgu_buf, wd_buf, wgu_bf_ref, wd_bf_ref, gu_sem, d_sem):
    del nb_ref
    blk = pl.program_id(0)
    de = wd_hbm.shape[1]
    valid = valid_ref[blk]
    expert = be_ref[blk]
    slot = slot_ref[blk]

    def weight_copies(e, s):
        return (pltpu.make_async_copy(wgu_hbm.at[e], wgu_buf.at[s], gu_sem.at[s]),
                pltpu.make_async_copy(wd_hbm.at[e], wd_buf.at[s], d_sem.at[s]))

    first_of_expert = jnp.logical_or(blk == 0, expert != be_ref[jnp.maximum(blk - 1, 0)])

    @pl.when(jnp.logical_and(valid > 0, first_of_expert))
    def _():
        @pl.when(blk == 0)
        def _():
            for c in weight_copies(expert, slot):
                c.start()

        for c in weight_copies(expert, slot):
            c.wait()
        wgu_bf_ref[...] = wgu_buf[slot].astype(BF16)
        wd_bf_ref[...] = wd_buf[slot].astype(BF16)

        @pl.when(next_ref[blk] >= 0)
        def _():
            for c in weight_copies(next_ref[blk], 1 - slot):
                c.start()

    def mlp(m):
        x = _unpack_rows(xs_ref.at[:, pl.ds(0, m)]).astype(BF16)
        gu = _dot(x, wgu_bf_ref[...]) + bgu_ref[0]
        glu = jnp.minimum(gu[:, :de], SWIGLU_LIMIT)
        lin = jnp.clip(gu[:, de:], -SWIGLU_LIMIT, SWIGLU_LIMIT)
        act = glu * jax.nn.sigmoid(SWIGLU_ALPHA * glu) * (lin + 1.0)
        y = _dot(act.astype(BF16), wd_bf_ref[...]) + bd_ref[0]
        _pack_rows(y, ys_ref.at[:, pl.ds(0, m)])
        if m < EXPERT_BLOCK:
            ys_ref[:, m:, :] = jnp.zeros((PACK_ROWS, EXPERT_BLOCK - m, LANES), U32)

    step = EXPERT_BLOCK // EXPERT_SPLIT
    for q in range(EXPERT_SPLIT):
        pl.when(jnp.logical_and(valid > q * step, valid <= (q + 1) * step))(functools.partial(mlp, (q + 1) * step))

    @pl.when(valid == 0)
    def _():
        ys_ref[...] = jnp.zeros_like(ys_ref)


def _sc_gather_rows(table, idx):
    mesh, nc, ns, lanes = _sc_mesh()
    nw = nc * ns
    planes, n_tab, width = table.shape
    n_idx_rows, ch = idx.shape
    n_ch = n_idx_rows // nw
    m = n_idx_rows * ch
    half = planes // 2
    assert planes % 2 == 0 and ch == SC_CHUNK and n_ch * nw == n_idx_rows and n_ch >= 2

    @functools.partial(
        pl.kernel, mesh=mesh, out_type=jax.ShapeDtypeStruct((planes, m, width), table.dtype),
        scratch_types=([pltpu.VMEM((n_ch, ch), I32)] + [pltpu.VMEM((ch, width), table.dtype)] * planes
                       + [pltpu.VMEM((8, ch), I32)] * planes
                       + [pltpu.SemaphoreType.DMA((planes,)), pltpu.SemaphoreType.DMA((planes,))]))
    def gather_kernel(table_hbm, idx_hbm, out_hbm, idx_v, *rest):
        bufs, ibufs, gsem, wsem = rest[:planes], rest[planes:2 * planes], rest[2 * planes], rest[2 * planes + 1]
        wid = lax.axis_index("s") * nc + lax.axis_index("c")
        pltpu.sync_copy(idx_hbm.at[pl.ds(wid * n_ch, n_ch)], idx_v)
        base = wid * n_ch * ch

        def gather(b):
            return pltpu.make_async_copy(table_hbm.at[ibufs[b].at[0]], bufs[b], gsem.at[b])

        def start_gather(j, b):
            for t in range(0, ch, lanes):
                ibufs[b][0, pl.ds(t, lanes)] = idx_v[j, pl.ds(t, lanes)] + b * n_tab
            gather(b).start()

        def write(j, b):
            return pltpu.make_async_copy(bufs[b], out_hbm.at[b, pl.ds(pl.multiple_of(base + j * ch, ch), ch)], wsem.at[b])

        def finish(j, b):
            pj, pb = (j, b - half) if b >= half else (j - 1, b + half)
            gather(pb).wait()
            write(pj, pb).start()

        for b in range(planes):
            start_gather(0, b)
        for b in range(half, planes):
            finish(0, b)

        @pl.loop(1, n_ch)
        def _(j):
            for b in range(planes):
                write(j - 1, b).wait()
                start_gather(j, b)
                finish(j, b)

        for b in range(half):
            finish(n_ch, b)
        for b in range(planes):
            write(n_ch - 1, b).wait()

    return gather_kernel(table.reshape(planes * n_tab, width), idx)


def _combine_kernel(x1_ref, gate_ref, y0_ref, y1_ref, y2_ref, y3_ref, *rest):
    o_ref = rest[-1]
    gate = gate_ref[...]
    out = x1_ref[...]
    for kk, y_ref in enumerate((y0_ref, y1_ref, y2_ref, y3_ref)):
        out = out + gate[:, kk:kk + 1] * _unpack_rows(y_ref)
    o_ref[...] = out


def _const_spec(shape):
    nd = len(shape)
    return pl.BlockSpec(shape, lambda *_: (0,) * nd)


def _layer(x, norm1_g, w_in, q_norm_g, k_norm_g, w_pool_grp, pool_scale, w_pool_up, w_attn_up, w_out, norm2_g,
           w_router, b_router, w_gate_up, b_gate_up, w_down, b_down):
    B, S, D = x.shape
    N = B * S
    pw = w_pool_up.shape[0]
    sw = w_attn_up.shape[0]
    n_exp = w_router.shape[1]
    de = w_down.shape[1]
    heads = sw // SB_HEAD_DIM
    assert pw == len(POOL_WINDOWS) * POOL_GROUP_DIM and heads % 2 == 0 and n_exp <= LANES
    assert D == 2 * PACK_ROWS * LANES and w_in.shape[1] == pw + 3 * sw + 2 * D
    tm = 512 if S % 512 == 0 else 256
    assert S % tm == 0 and S % (ATTN_SUB * ATTN_BLOCK) == 0
    xf = x.reshape(N, D)
    cparams = functools.partial(pltpu.CompilerParams, vmem_limit_bytes=VMEM_LIMIT)

    hsum = (jnp.arange(sw)[:, None] // SB_HEAD_DIM == jnp.arange(sw)[None, :] // SB_HEAD_DIM).astype(BF16)
    tm_in = 2 * tm if S % (2 * tm) == 0 else tm
    nt = S // tm_in
    tok_spec = lambda w: pl.BlockSpec((tm_in, w), lambda b, i: (b * nt + i, 0))
    w_in_bf = w_in.astype(BF16)
    n_in = pw + 3 * sw + D
    g1 = norm1_g.reshape(1, D)
    q2, k2, v2, pg = pl.pallas_call(
        functools.partial(_mixer_in_kernel, tm=tm_in, pw=pw, sw=sw),
        grid=(B, nt),
        in_specs=[tok_spec(D), _const_spec((1, D)), _const_spec((D, n_in)), _const_spec((1, sw)),
                  _const_spec((1, sw)), _const_spec((sw, sw)),
                  _const_spec((len(POOL_WINDOWS), POOL_GROUP_DIM, POOL_GROUP_DIM)), _const_spec((1, pw)),
                  _const_spec((pw, D))],
        out_specs=[tok_spec(sw), tok_spec(sw), tok_spec(sw), tok_spec(D)],
        out_shape=[jax.ShapeDtypeStruct((N, sw), BF16)] * 3 + [jax.ShapeDtypeStruct((N, D), BF16)],
        scratch_shapes=[pltpu.VMEM((POOL_HALO, pw), F32)],
        compiler_params=cparams(dimension_semantics=("arbitrary", "arbitrary")),
        name="mixer_in",
    )(xf, g1, w_in_bf[:, :n_in], jnp.tile(q_norm_g, heads).reshape(1, sw),
      jnp.tile(k_norm_g, heads).reshape(1, sw), hsum, w_pool_grp.astype(BF16), pool_scale.reshape(1, pw),
      w_pool_up.astype(BF16))

    bq = ATTN_BLOCK
    rows_q = ATTN_SUB * bq
    nq = S // rows_q
    jj = jnp.arange(bq)
    tri = jnp.concatenate([(jj[:, None] >= jj[None, :]).astype(BF16), jnp.ones((bq, bq), BF16)], axis=1)
    n_pairs = heads // 2
    n_units = ATTN_SUB * n_pairs
    kv_spec = pl.BlockSpec(memory_space=pl.ANY)
    sba = pl.pallas_call(
        functools.partial(_attn_kernel, n_pairs=n_pairs),
        grid=(B, nq),
        in_specs=[pl.BlockSpec((rows_q, sw), lambda b, qi: (b * nq + qi, 0)), kv_spec, kv_spec,
                  _const_spec((bq, 2 * bq))],
        out_specs=pl.BlockSpec((rows_q, sw), lambda b, qi: (b * nq + qi, 0)),
        out_shape=jax.ShapeDtypeStruct((N, sw), BF16),
        scratch_shapes=([pltpu.VMEM((2, S, sw), BF16), pltpu.VMEM((2, S, sw), BF16),
                         pltpu.SemaphoreType.DMA((2,)), pltpu.SemaphoreType.DMA((2,))]
                        + [pltpu.VMEM((2 * bq, LANES), BF16)] * n_units + [pltpu.VMEM((2 * bq, LANES), F32)] * (2 * n_units)),
        compiler_params=cparams(dimension_semantics=("arbitrary", "arbitrary")),
        name="sb_attn",
    )(q2, k2, v2, tri)

    wr = jnp.zeros((D, LANES), F32).at[:, :n_exp].set(w_router)
    wr_hi = wr.astype(BF16)
    wr_lo = (wr - wr_hi.astype(F32)).astype(BF16)
    br = jnp.full((1, LANES), -jnp.inf, F32).at[0, :n_exp].set(b_router)
    ltri = (jnp.arange(tm)[:, None] > jnp.arange(tm)[None, :]).astype(BF16)
    assert N % (MOE_GROUPS * tm) == 0
    ng = N // MOE_GROUPS
    steps = ng // tm
    n_assign = ng * TOP_K
    n_blocks = -(-(n_assign + n_exp * (EXPERT_BLOCK - 1)) // EXPERT_BLOCK)
    n_rows = n_blocks * EXPERT_BLOCK
    plane_spec = lambda rows, index: pl.BlockSpec((PACK_ROWS, rows, LANES), lambda i, *_: (0, index(i, *_), 0))
    row_spec = lambda w: pl.BlockSpec((tm, w), lambda i: (i, 0))
    w_attn_up_bf, w_out_bf, g2 = w_attn_up.astype(BF16), w_out.astype(BF16), norm2_g.reshape(1, D)
    bgu, bdn = b_gate_up.reshape(n_exp, 1, 2 * de), b_down.reshape(n_exp, 1, D)
    out = None
    for grp in range(MOE_GROUPS):
        grp_spec = lambda w, first=grp * steps: pl.BlockSpec((tm, w), lambda i: (first + i, 0))

        cur = lambda i: jnp.minimum(i, steps - 1)
        prev = lambda i: jnp.maximum(i - 1, 0)
        in_spec = lambda w, first=grp * steps: pl.BlockSpec((tm, w), lambda i: (first + cur(i), 0))
        x1, hpk, ri, gate, cnt = pl.pallas_call(
            functools.partial(_mixer_out_kernel, tm=tm),
            grid=(steps + 1,),
            in_specs=[in_spec(D), in_spec(D), in_spec(sw), _const_spec((1, D)), _const_spec((D, D)),
                      _const_spec((sw, D)), _const_spec((D, D)),
                      _const_spec((1, D)), _const_spec((D, LANES)), _const_spec((D, LANES)), _const_spec((1, LANES)),
                      _const_spec((tm, tm))],
            out_specs=[pl.BlockSpec((tm, D), lambda i: (cur(i), 0)), plane_spec(tm, cur),
                       pl.BlockSpec((2 * TOP_K, tm), lambda i: (0, prev(i))),
                       pl.BlockSpec((tm, LANES), lambda i: (prev(i), 0)), _const_spec((8, LANES))],
            out_shape=[jax.ShapeDtypeStruct((ng, D), F32), jax.ShapeDtypeStruct((PACK_ROWS, ng, LANES), U32),
                       jax.ShapeDtypeStruct((2 * TOP_K, ng), I32), jax.ShapeDtypeStruct((ng, LANES), F32),
                       jax.ShapeDtypeStruct((8, LANES), F32)],
            scratch_shapes=[pltpu.VMEM((2, tm, LANES), F32)],
            compiler_params=cparams(dimension_semantics=("arbitrary",)),
            name="mixer_out",
        )(xf, pg, sba, g1, w_in_bf[:, n_in:], w_attn_up_bf, w_out_bf, g2, wr_hi, wr_lo, br, ltri)

        counts = cnt[0, :n_exp].astype(I32)
        padded = (counts + EXPERT_BLOCK - 1) // EXPERT_BLOCK * EXPERT_BLOCK
        padded_end = jnp.cumsum(padded)
        start_pad = padded_end - padded
        td = min(ng, 8192)
        dest = pl.pallas_call(
            _dest_kernel,
            grid_spec=pltpu.PrefetchScalarGridSpec(
                num_scalar_prefetch=1,
                grid=(ng // td,),
                in_specs=[pl.BlockSpec((2 * TOP_K, td), lambda i, sp: (0, i))],
                out_specs=pl.BlockSpec((TOP_K, td), lambda i, sp: (0, i)),
            ),
            out_shape=jax.ShapeDtypeStruct((TOP_K, ng), I32),
            compiler_params=cparams(dimension_semantics=("arbitrary",)),
            name="dest",
        )(start_pad, ri).reshape(-1, SC_CHUNK)
        block_start = jnp.arange(n_blocks, dtype=I32) * EXPERT_BLOCK
        block_expert = jnp.minimum(jnp.sum((padded_end[None, :] <= block_start[:, None]).astype(I32), axis=1),
                                   n_exp - 1)
        n_used = (padded_end[-1] // EXPERT_BLOCK).astype(I32).reshape(1)

        xs = _sc_scatter_rows(hpk, dest, n_rows)
        xs = pl.pallas_call(
            _padfill_kernel,
            grid_spec=pltpu.PrefetchScalarGridSpec(
                num_scalar_prefetch=2,
                grid=(1,),
                in_specs=[pl.BlockSpec(memory_space=pl.ANY)],
                out_specs=pl.BlockSpec(memory_space=pl.ANY),
                scratch_shapes=[pltpu.VMEM((PACK_ROWS, 1 << ((EXPERT_BLOCK - 1).bit_length() - 1), LANES), U32),
                                pltpu.SemaphoreType.DMA(())],
            ),
            out_shape=jax.ShapeDtypeStruct((PACK_ROWS, n_rows, LANES), U32),
            input_output_aliases={2: 0},
            compiler_params=cparams(dimension_semantics=("arbitrary",)),
            name="padfill",
        )(start_pad + counts, padded - counts, xs)

        experts = jnp.arange(n_exp, dtype=I32)
        used = counts > 0
        block_valid = jnp.clip(jnp.sum(jnp.where(block_expert[:, None] == experts[None, :],
                                                 (start_pad + counts)[None, :], 0), axis=1) - block_start,
                               0, EXPERT_BLOCK)
        ordinal = jnp.cumsum(used.astype(I32)) - 1
        next_used = jnp.min(jnp.where(jnp.logical_and(experts[None, :] > experts[:, None], used[None, :]),
                                      experts[None, :], n_exp), axis=1)
        next_used = jnp.where(next_used == n_exp, -1, next_used)
        per_block = lambda table: jnp.sum(jnp.where(block_expert[:, None] == experts[None, :], table[None, :], 0), axis=1)
        b_spec = lambda shape: pl.BlockSpec(shape, lambda i, be, *_: (be[i], 0, 0))
        ys = pl.pallas_call(
            _experts_kernel,
            grid_spec=pltpu.PrefetchScalarGridSpec(
                num_scalar_prefetch=5,
                grid=(n_blocks,),
                in_specs=[plane_spec(EXPERT_BLOCK, lambda i, be, nb, *_: jnp.minimum(i, nb[0] - 1)),
                          pl.BlockSpec(memory_space=pl.ANY), b_spec((1, 1, 2 * de)),
                          pl.BlockSpec(memory_space=pl.ANY), b_spec((1, 1, D))],
                out_specs=plane_spec(EXPERT_BLOCK, lambda i, *_: i),
                scratch_shapes=[pltpu.VMEM((2, D, 2 * de), F32), pltpu.VMEM((2, de, D), F32),
                                pltpu.VMEM((D, 2 * de), BF16), pltpu.VMEM((de, D), BF16),
                                pltpu.SemaphoreType.DMA((2,)), pltpu.SemaphoreType.DMA((2,))],
            ),
            out_shape=jax.ShapeDtypeStruct((PACK_ROWS, n_rows, LANES), U32),
            compiler_params=cparams(dimension_semantics=("arbitrary",)),
            name="experts",
        )(block_expert, n_used, block_valid, per_block(ordinal % 2), per_block(next_used), xs, w_gate_up, bgu, w_down, bdn)

        y4 = _sc_gather_rows(ys, dest.reshape(-1, SC_CHUNK))

        operands = [x1, gate, y4, y4, y4, y4] + ([] if out is None else [out])
        out = pl.pallas_call(
            _combine_kernel,
            grid=(steps,),
            in_specs=([row_spec(D), row_spec(LANES)]
                      + [plane_spec(tm, lambda i, kk=kk: kk * steps + i) for kk in range(TOP_K)]
                      + ([] if out is None else [pl.BlockSpec(memory_space=pl.ANY)])),
            out_specs=grp_spec(D),
            out_shape=jax.ShapeDtypeStruct((N, D), F32),
            input_output_aliases={} if out is None else {len(operands) - 1: 0},
            compiler_params=cparams(dimension_semantics=("arbitrary",)),
            name="combine",
        )(*operands)
    return out.reshape(B, S, D)


def kernel(x, norm1_g, w_in, q_norm_g, k_norm_g, w_pool_grp, pool_scale, w_pool_up, w_attn_up, w_out, norm2_g,
           w_router, b_router, w_gate_up, b_gate_up, w_down, b_down):
    for layer in range(norm1_g.shape[0]):
        x = _layer(x, norm1_g[layer], w_in[layer], q_norm_g[layer], k_norm_g[layer], w_pool_grp[layer],
                   pool_scale[layer], w_pool_up[layer], w_attn_up[layer], w_out[layer], norm2_g[layer],
                   w_router[layer], b_router[layer], w_gate_up[layer], b_gate_up[layer], w_down[layer],
                   b_down[layer])
    return x
```

```python
import functools

import jax
import jax.numpy as jnp
from jax import lax
from jax.experimental import pallas as pl
from jax.experimental.pallas import tpu as pltpu
from jax.experimental.pallas import tpu_sc as plsc

F32 = jnp.float32
BF16 = jnp.bfloat16
U32 = jnp.uint32
I32 = jnp.int32

EPS = 1e-6
POOL_WINDOWS = (2, 4, 8, 16)
POOL_GROUP_DIM = 128
POOL_HALO = 16
SB_HEAD_DIM = 64
TOP_K = 4
SWIGLU_LIMIT = 7.0
SWIGLU_ALPHA = 1.702
EXPERT_BLOCK = 1024
EXPERT_CAST_COLS = 256
EXPERT_SPLIT = 8
LANES = 128
PACK_ROWS = 4
ATTN_BLOCK = 128
ATTN_SUB = 2
ATTN_EXIT_BITS = 70.0
LOG2_E = 1.4426950408889634
VMEM_LIMIT = 56 * 1024 * 1024
SC_CHUNK = 128
MOE_GROUPS = 2


def _dot(a, b):
    return jnp.dot(a, b, preferred_element_type=F32)


def _split_bf16(x):
    hi = x.astype(BF16)
    lo = (x - hi.astype(F32)).astype(BF16)
    return hi, lo


def _pack_rows(v, out_ref):
    half = v.shape[1] // 2
    lo = lax.bitcast_convert_type(v[:, :half].astype(BF16).astype(F32), U32) >> 16
    hi = lax.bitcast_convert_type(v[:, half:].astype(BF16).astype(F32), U32) & jnp.uint32(0xFFFF0000)
    w = lo | hi
    for c in range(PACK_ROWS):
        out_ref[c] = w[:, c * LANES:(c + 1) * LANES]


def _unpack_rows(ref):
    los, his = [], []
    for c in range(PACK_ROWS):
        w = ref[c]
        los.append(lax.bitcast_convert_type(w << 16, F32))
        his.append(lax.bitcast_convert_type(w & jnp.uint32(0xFFFF0000), F32))
    return jnp.concatenate(los + his, axis=1)


def _mixer_in_kernel(x_ref, g1_ref, win_ref, gq_ref, gk_ref, hsum_ref, wgrp_ref, pscale_ref, wpu_ref,
                     q_ref, k_ref, v_ref, p_ref, tail_ref, *, tm, pw, sw):
    i = pl.program_id(1)

    @pl.when(i == 0)
    def _():
        tail_ref[...] = jnp.zeros_like(tail_ref)

    x = x_ref[...]
    ms = jnp.mean(x * x, axis=-1, keepdims=True)
    h = (x * lax.rsqrt(ms + EPS) * g1_ref[...]).astype(BF16)

    def head_norm(t, gain):
        ss = _dot((t * t).astype(BF16), hsum_ref[...])
        return t * lax.rsqrt(ss * (1.0 / SB_HEAD_DIM) + EPS) * gain

    def project_q():
        q = _dot(h, win_ref[:, pw:pw + sw])
        q_ref[...] = (head_norm(q, gq_ref[...]) * (SB_HEAD_DIM ** -0.5 * LOG2_E)).astype(BF16)

    def project_k():
        k = _dot(h, win_ref[:, pw + sw:pw + 2 * sw])
        k_ref[...] = head_norm(k, gk_ref[...]).astype(BF16)

    def project_v():
        v_ref[...] = _dot(h, win_ref[:, pw + 2 * sw:pw + 3 * sw]).astype(BF16)

    d_model = x.shape[1]
    pool_gate = []

    def project_pool_gate():
        pool_gate.append(jax.nn.sigmoid(_dot(h, win_ref[:, pw + 3 * sw:pw + 3 * sw + d_model])))

    u = _dot(h, win_ref[:, 0:pw])
    xx = jnp.concatenate([tail_ref[...], u], axis=0)
    tail_ref[...] = u[tm - POOL_HALO:, :]
    pos = i * tm + lax.broadcasted_iota(I32, (tm, POOL_GROUP_DIM), 0)
    mixed = []
    for (g, w), project in zip(enumerate(POOL_WINDOWS), (project_q, project_k, project_v, project_pool_gate)):
        project()
        s = xx[:, g * POOL_GROUP_DIM:(g + 1) * POOL_GROUP_DIM]
        step = 1
        while step < w:
            s = s + pltpu.roll(s, step, axis=0)
            step *= 2
        count = jnp.minimum(pos + 1, w).astype(F32)
        ug = u[:, g * POOL_GROUP_DIM:(g + 1) * POOL_GROUP_DIM]
        d = s[POOL_HALO:, :] / count - ug
        mixed.append(_dot(d.astype(BF16), wgrp_ref[g]))
    pm = jnp.concatenate(mixed, axis=1) * pscale_ref[...]
    pool_out = _dot(pm.astype(BF16), wpu_ref[...])
    p_ref[...] = (pool_gate[0] * pool_out).astype(BF16)


def _attn_kernel(q_ref, k_hbm, v_hbm, tri_ref, o_ref, kbuf, vbuf, ksem, vsem, *scratch, n_pairs):
    units = [(sub, p) for sub in range(ATTN_SUB) for p in range(n_pairs)]
    n_units = len(units)
    qs, acc, rr = scratch[:n_units], scratch[n_units:2 * n_units], scratch[2 * n_units:]
    bq = ATTN_BLOCK
    batch = pl.program_id(0)
    seq = kbuf.shape[1]

    def kv_copies(b):
        rows = pl.ds(pl.multiple_of(b * seq, seq), seq)
        return (pltpu.make_async_copy(k_hbm.at[rows], kbuf.at[b % 2], ksem.at[b % 2]),
                pltpu.make_async_copy(v_hbm.at[rows], vbuf.at[b % 2], vsem.at[b % 2]))

    @pl.when(pl.program_id(1) == 0)
    def _():
        @pl.when(batch == 0)
        def _():
            for c in kv_copies(batch):
                c.start()

        for c in kv_copies(batch):
            c.wait()

        @pl.when(batch + 1 < pl.num_programs(0))
        def _():
            for c in kv_copies(batch + 1):
                c.start()

    k_ref = kbuf.at[batch % 2]
    v_ref = vbuf.at[batch % 2]
    first_block = pl.program_id(1) * ATTN_SUB
    first_head = lax.broadcasted_iota(I32, (bq, LANES), 1) < SB_HEAD_DIM
    for u, (sub, p) in enumerate(units):
        q2 = q_ref[sub * bq:(sub + 1) * bq, p * LANES:(p + 1) * LANES]
        qs[u][:bq] = jnp.where(first_head, q2, jnp.zeros_like(q2))
        qs[u][bq:] = jnp.where(first_head, jnp.zeros_like(q2), q2)
    row = lax.broadcasted_iota(I32, (2 * bq, bq), 0)
    col = lax.broadcasted_iota(I32, (2 * bq, bq), 1)
    causal = col < (row & (bq - 1))
    contract_last = (((1,), (1,)), ((), ()))

    def softplus(z):
        return jnp.maximum(z, 0.0) + jnp.log2(1.0 + jnp.exp2(-jnp.abs(z)))

    def suffix_sums(sp):
        return _dot(sp.astype(BF16), tri_ref[...])

    def cols(ref, block, p):
        start = pl.multiple_of(jnp.maximum(block, 0) * bq, bq)
        return ref[pl.ds(start, bq), p * LANES:(p + 1) * LANES]

    def scores(u, block):
        return lax.dot_general(qs[u][...], cols(k_ref, block, units[u][1]), contract_last, preferred_element_type=F32)

    diag = [first_block + sub for sub, _ in units]
    z_d, z_n, s_d, s_n = {}, {}, {}, {}

    def stage_scores(u):
        z_d[u] = scores(u, diag[u])
        z_n[u] = scores(u, diag[u] - 1)

    def stage_sums(u):
        s_d[u] = suffix_sums(jnp.where(causal, softplus(z_d[u]), 0.0))
        s_n[u] = suffix_sums(jnp.where(diag[u] >= 1, softplus(z_n[u]), 0.0))

    def stage_values(u):
        p = units[u][1]
        a_d = jnp.where(causal, jnp.exp2(z_d[u] - s_d[u][:, :bq]), 0.0)
        r_d = s_d[u][:, bq:]
        a_n = jnp.where(diag[u] >= 1, jnp.exp2(z_n[u] - (r_d + s_n[u][:, :bq])), 0.0)
        acc[u][...] = (_dot(a_d.astype(BF16), cols(v_ref, diag[u], p)) + _dot(a_n.astype(BF16), cols(v_ref, diag[u] - 1, p)))
        rr[u][...] = r_d + s_n[u][:, bq:]

    for t in range(n_units + 2):
        if t < n_units:
            stage_scores(t)
        if 0 <= t - 1 < n_units:
            stage_sums(t - 1)
        if 0 <= t - 2 < n_units:
            stage_values(t - 2)

    def r_min():
        m = rr[0][...]
        for u in range(1, n_units):
            m = jnp.minimum(m, rr[u][...])
        return jnp.min(m)

    def cond(c):
        back, rm = c
        return jnp.logical_and(diag[-1] - back >= 0, rm < ATTN_EXIT_BITS)

    def body(c):
        back, _ = c
        blocks = [d - back for d in diag]
        zs = [scores(u, blocks[u]) for u in range(n_units)]
        ss = [suffix_sums(jnp.where(blocks[u] >= 0, softplus(zs[u]), 0.0)) for u in range(n_units)]
        for u, (_, p) in enumerate(units):
            r = rr[u][...]
            a = jnp.where(blocks[u] >= 0, jnp.exp2(zs[u] - (r + ss[u][:, :bq])), 0.0)
            acc[u][...] += _dot(a.astype(BF16), cols(v_ref, blocks[u], p))
            rr[u][...] = r + ss[u][:, bq:]
        return back + 1, r_min()

    lax.while_loop(cond, body, (2, r_min()))
    for u, (sub, p) in enumerate(units):
        o_ref[sub * bq:(sub + 1) * bq, p * LANES:(p + 1) * LANES] = (
            jnp.where(first_head, acc[u][:bq], acc[u][bq:]).astype(BF16))


def _mixer_out_kernel(x_ref, p_ref, sba_ref, g1_ref, wga_ref, wau_ref, wout_ref, g2_ref, wr_hi_ref, wr_lo_ref, br_ref,
                      ltri_ref, x1_ref, hp_ref, ri_ref, gate_ref, cnt_ref, logit_ref, *, tm):
    step = pl.program_id(0)

    @pl.when(step == 0)
    def _():
        cnt_ref[...] = jnp.zeros_like(cnt_ref)
        logit_ref[1] = jnp.zeros((tm, LANES), F32)

    logits = logit_ref[(step + 1) % 2]
    routed = jnp.where(step >= 1, 1.0, 0.0)
    lane = lax.broadcasted_iota(I32, logits.shape, 1).astype(F32)
    work = logits
    vals, idxs = [], []

    def topk_round():
        nonlocal work
        m = jnp.max(work, axis=-1, keepdims=True)
        ik = jnp.min(jnp.where(work == m, lane, float(LANES)), axis=-1, keepdims=True)
        vals.append(m)
        idxs.append(ik)
        work = jnp.where(lane == ik, -jnp.inf, work)

    d_model = x_ref.shape[1]
    quarter = d_model // TOP_K
    topk_round()
    x = x_ref[...]
    h1 = (x * lax.rsqrt(jnp.mean(x * x, axis=-1, keepdims=True) + EPS) * g1_ref[...]).astype(BF16)
    attn_gate = jax.nn.sigmoid(_dot(h1, wga_ref[...]))
    attn_out = _dot(sba_ref[...], wau_ref[...])
    merged = (p_ref[...].astype(F32) + attn_gate * attn_out).astype(BF16)
    x1_parts = []
    for c in range(TOP_K):
        if c >= 1:
            topk_round()
        cols = slice(c * quarter, (c + 1) * quarter)
        x1_parts.append(x[:, cols] + _dot(merged, wout_ref[:, cols]))
    x1 = jnp.concatenate(x1_parts, axis=1)
    x1_ref[...] = x1

    es = [jnp.exp(v - vals[0]) for v in vals]
    denom = es[0] + es[1] + es[2] + es[3]
    hot = jnp.zeros(logits.shape, F32)
    for ik in idxs:
        hot = hot + jnp.where(lane == ik, routed, 0.0)
    before = _dot(ltri_ref[...], hot.astype(BF16)) + cnt_ref[0:1, :]

    ms = jnp.mean(x1 * x1, axis=-1, keepdims=True)
    h2 = x1 * lax.rsqrt(ms + EPS) * g2_ref[...]
    _pack_rows(h2, hp_ref)

    ri = jnp.zeros(logits.shape, F32)
    gt = jnp.zeros(logits.shape, F32)
    for kk in range(TOP_K):
        rank = jnp.sum(jnp.where(lane == idxs[kk], before, 0.0), axis=-1, keepdims=True)
        ri = jnp.where(lane == kk, idxs[kk], ri)
        ri = jnp.where(lane == TOP_K + kk, rank, ri)
        gt = jnp.where(lane == kk, es[kk] / denom, gt)
    ri_ref[...] = ri.T[:2 * TOP_K].astype(I32)
    gate_ref[...] = gt
    cnt_ref[...] = cnt_ref[...] + jnp.sum(hot, axis=0, keepdims=True)

    h_hi, h_lo = _split_bf16(h2)
    logit_ref[step % 2] = (_dot(h_hi, wr_hi_ref[...]) + _dot(h_hi, wr_lo_ref[...]) + _dot(h_lo, wr_hi_ref[...])
                           + br_ref[...])


def _dest_kernel(start_ref, ri_ref, dest_ref):
    idx = ri_ref[:TOP_K, :]
    dest = ri_ref[TOP_K:, :]
    for e in range(start_ref.shape[0]):
        dest = dest + jnp.where(idx == e, start_ref[e], 0)
    dest_ref[...] = dest


def _sc_mesh():
    info = plsc.get_sparse_core_info()
    mesh = plsc.VectorSubcoreMesh(core_axis_name="c", subcore_axis_name="s")
    return mesh, info.num_cores, info.num_subcores, info.num_lanes


def _sc_scatter_rows(rows, idx, n_out):
    mesh, nc, ns, lanes = _sc_mesh()
    nw = nc * ns
    planes, m, width = rows.shape
    ch = SC_CHUNK
    n_ch = m // ch // nw
    half = planes // 2
    assert planes % 2 == 0 and n_ch * ch * nw == m and idx.shape == (m // ch * TOP_K, ch) and n_ch >= 2
    assert n_ch % 8 == 0

    @functools.partial(
        pl.kernel, mesh=mesh, out_type=jax.ShapeDtypeStruct((planes * n_out, width), rows.dtype),
        scratch_types=([pltpu.VMEM((n_ch * TOP_K, ch), I32)] + [pltpu.VMEM((ch, width), rows.dtype)] * planes
                       + [pltpu.VMEM((TOP_K, ch), I32)] * planes
                       + [pltpu.SemaphoreType.DMA((planes,)), pltpu.SemaphoreType.DMA((planes,))]))
    def scatter_kernel(rows_hbm, idx_hbm, out_hbm, idx_v, *rest):
        bufs, ibufs, rsem, ssem = rest[:planes], rest[planes:2 * planes], rest[2 * planes], rest[2 * planes + 1]
        wid = lax.axis_index("s") * nc + lax.axis_index("c")
        for kk in range(TOP_K):
            pltpu.sync_copy(idx_hbm.at[pl.ds(kk * (m // ch) + wid * n_ch, n_ch)], idx_v.at[pl.ds(kk * n_ch, n_ch)])
        base = wid * n_ch * ch

        def read(j, b):
            return pltpu.make_async_copy(rows_hbm.at[b, pl.ds(pl.multiple_of(base + j * ch, ch), ch)], bufs[b], rsem.at[b])

        def scatters(b):
            return [pltpu.make_async_copy(bufs[b], out_hbm.at[ibufs[b].at[kk]], ssem.at[b]) for kk in range(TOP_K)]

        def start_scatters(j, b):
            for kk in range(TOP_K):
                for t in range(0, ch, lanes):
                    ibufs[b][kk, pl.ds(t, lanes)] = idx_v[kk * n_ch + j, pl.ds(t, lanes)] + b * n_out
            for c in scatters(b):
                c.start()

        def wait_scatters(b):
            for c in scatters(b):
                c.wait()

        def finish(j, b):
            pj, pb = (j, b - half) if b >= half else (j - 1, b + half)
            read(pj, pb).wait()
            start_scatters(pj, pb)

        for b in range(planes):
            read(0, b).start()
        for b in range(half, planes):
            finish(0, b)

        @pl.loop(1, n_ch)
        def _(j):
            for b in range(planes):
                wait_scatters(b)
                read(j, b).start()
                finish(j, b)

        for b in range(half):
            finish(n_ch, b)
        for b in range(planes):
            wait_scatters(b)

    return scatter_kernel(rows, idx).reshape(planes, n_out, width)


def _padfill_kernel(start_ref, len_ref, xs_in_ref, xs_ref, zeros_ref, sem):
    del xs_in_ref
    zeros_ref[...] = jnp.zeros_like(zeros_ref)
    bits = [1 << b for b in reversed(range((EXPERT_BLOCK - 1).bit_length()))]

    def pieces(e):
        n = len_ref[e]
        for bit in bits:
            row0 = start_ref[e] + (n & ~(2 * bit - 1))
            copy = pltpu.make_async_copy(zeros_ref.at[:, pl.ds(0, bit)], xs_ref.at[:, pl.ds(row0, bit)], sem)
            yield (n & bit) != 0, copy

    def start(e, c):
        for on, copy in pieces(e):
            pl.when(on)(copy.start)
        return c

    def wait(e, c):
        for on, copy in pieces(e):
            pl.when(on)(copy.wait)
        return c

    lax.fori_loop(0, start_ref.shape[0], start, 0)
    lax.fori_loop(0, start_ref.shape[0], wait, 0)


def _experts_kernel(be_ref, nb_ref, valid_ref, slot_ref, next_ref, xs_ref, wgu_hbm, bgu_ref, wd_hbm, bd_ref, ys_ref,
                    wgu_buf, wd_buf, wgu_bf_ref, wd_bf_ref, gu_sem, d_sem):
    del nb_ref
    blk = pl.program_id(0)
    de = wd_hbm.shape[1]
    valid = valid_ref[blk]
    expert = be_ref[blk]
    slot = slot_ref[blk]
    following = next_ref[blk]
    prev1 = be_ref[jnp.maximum(blk - 1, 0)]
    prev2 = be_ref[jnp.maximum(blk - 2, 0)]

    def weight_copies(e, s):
        return (pltpu.make_async_copy(wgu_hbm.at[e], wgu_buf.at[s], gu_sem.at[s]),
                pltpu.make_async_copy(wd_hbm.at[e], wd_buf.at[s], d_sem.at[s]))

    def cast_rows(s, piece, pieces):
        for buf, bf_ref in ((wgu_buf, wgu_bf_ref), (wd_buf, wd_bf_ref)):
            rows = buf.shape[1] // pieces
            bf_ref[s, pl.ds(piece * rows, rows), :] = buf[s, pl.ds(piece * rows, rows), :].astype(BF16)

    first_of_expert = jnp.logical_or(blk == 0, expert != prev1)
    second_of_expert = jnp.logical_and(jnp.logical_and(blk >= 1, expert == prev1),
                                       jnp.logical_or(blk == 1, prev1 != prev2))
    cast_ahead = jnp.logical_and(jnp.logical_and(second_of_expert, following >= 0), valid > 0)
    cast_in_mlp = jnp.logical_and(cast_ahead, valid == EXPERT_BLOCK)
    cast_before = jnp.logical_and(blk >= 2, prev1 == prev2)

    @pl.when(jnp.logical_and(valid > 0, first_of_expert))
    def _():
        @pl.when(blk == 0)
        def _():
            for c in weight_copies(expert, slot):
                c.start()

        @pl.when(jnp.logical_not(cast_before))
        def _():
            for c in weight_copies(expert, slot):
                c.wait()
            cast_rows(slot, 0, 1)

        @pl.when(following >= 0)
        def _():
            for c in weight_copies(following, 1 - slot):
                c.start()

    @pl.when(jnp.logical_and(cast_ahead, valid < EXPERT_BLOCK))
    def _():
        for c in weight_copies(following, 1 - slot):
            c.wait()
        cast_rows(1 - slot, 0, 1)

    def activation(gu_gate, gu_lin):
        glu = jnp.minimum(gu_gate, SWIGLU_LIMIT)
        lin = jnp.clip(gu_lin, -SWIGLU_LIMIT, SWIGLU_LIMIT)
        return (glu * jax.nn.sigmoid(SWIGLU_ALPHA * glu) * (lin + 1.0)).astype(BF16)

    def mlp(m):
        x = _unpack_rows(xs_ref.at[:, pl.ds(0, m)]).astype(BF16)
        gu = _dot(x, wgu_bf_ref[slot]) + bgu_ref[0]
        y = _dot(activation(gu[:, :de], gu[:, de:]), wd_bf_ref[slot]) + bd_ref[0]
        _pack_rows(y, ys_ref.at[:, pl.ds(0, m)])
        if m < EXPERT_BLOCK:
            ys_ref[:, m:, :] = jnp.zeros((PACK_ROWS, EXPERT_BLOCK - m, LANES), U32)

    def mlp_and_cast():
        for c in weight_copies(following, 1 - slot):
            c.wait()
        x = _unpack_rows(xs_ref).astype(BF16)
        bgu = bgu_ref[0]
        pieces = 2 * de // EXPERT_CAST_COLS
        gu = []
        for j in range(pieces):
            cols = slice(j * EXPERT_CAST_COLS, (j + 1) * EXPERT_CAST_COLS)
            gu.append(_dot(x, wgu_bf_ref[slot, :, cols]) + bgu[:, cols])
            cast_rows(1 - slot, j, pieces)
        act = jnp.concatenate([activation(g, l) for g, l in zip(gu[:pieces // 2], gu[pieces // 2:])], axis=1)
        y = _dot(act, wd_bf_ref[slot]) + bd_ref[0]
        _pack_rows(y, ys_ref)

    pl.when(cast_in_mlp)(mlp_and_cast)
    step = EXPERT_BLOCK // EXPERT_SPLIT
    for q in range(EXPERT_SPLIT):
        here = jnp.logical_and(valid > q * step, valid <= (q + 1) * step)
        if q == EXPERT_SPLIT - 1:
            here = jnp.logical_and(here, jnp.logical_not(cast_in_mlp))
        pl.when(here)(functools.partial(mlp, (q + 1) * step))

    @pl.when(valid == 0)
    def _():
        ys_ref[...] = jnp.zeros_like(ys_ref)


def _sc_gather_rows(table, idx):
    mesh, nc, ns, lanes = _sc_mesh()
    nw = nc * ns
    planes, n_tab, width = table.shape
    n_idx_rows, ch = idx.shape
    n_ch = n_idx_rows // nw
    m = n_idx_rows * ch
    half = planes // 2
    assert planes % 2 == 0 and ch == SC_CHUNK and n_ch * nw == n_idx_rows and n_ch >= 2

    @functools.partial(
        pl.kernel, mesh=mesh, out_type=jax.ShapeDtypeStruct((planes, m, width), table.dtype),
        scratch_types=([pltpu.VMEM((n_ch, ch), I32)] + [pltpu.VMEM((ch, width), table.dtype)] * planes
                       + [pltpu.VMEM((8, ch), I32)] * planes
                       + [pltpu.SemaphoreType.DMA((planes,)), pltpu.SemaphoreType.DMA((planes,))]))
    def gather_kernel(table_hbm, idx_hbm, out_hbm, idx_v, *rest):
        bufs, ibufs, gsem, wsem = rest[:planes], rest[planes:2 * planes], rest[2 * planes], rest[2 * planes + 1]
        wid = lax.axis_index("s") * nc + lax.axis_index("c")
        pltpu.sync_copy(idx_hbm.at[pl.ds(wid * n_ch, n_ch)], idx_v)
        base = wid * n_ch * ch

        def gather(b):
            return pltpu.make_async_copy(table_hbm.at[ibufs[b].at[0]], bufs[b], gsem.at[b])

        def start_gather(j, b):
            for t in range(0, ch, lanes):
                ibufs[b][0, pl.ds(t, lanes)] = idx_v[j, pl.ds(t, lanes)] + b * n_tab
            gather(b).start()

        def write(j, b):
            return pltpu.make_async_copy(bufs[b], out_hbm.at[b, pl.ds(pl.multiple_of(base + j * ch, ch), ch)], wsem.at[b])

        def finish(j, b):
            pj, pb = (j, b - half) if b >= half else (j - 1, b + half)
            gather(pb).wait()
            write(pj, pb).start()

        for b in range(planes):
            start_gather(0, b)
        for b in range(half, planes):
            finish(0, b)

        @pl.loop(1, n_ch)
        def _(j):
            for b in range(planes):
                write(j - 1, b).wait()
                start_gather(j, b)
                finish(j, b)

        for b in range(half):
            finish(n_ch, b)
        for b in range(planes):
            write(n_ch - 1, b).wait()

    return gather_kernel(table.reshape(planes * n_tab, width), idx)


def _combine_kernel(x1_ref, gate_ref, y0_ref, y1_ref, y2_ref, y3_ref, *rest):
    o_ref = rest[-1]
    gate = gate_ref[...]
    out = x1_ref[...]
    for kk, y_ref in enumerate((y0_ref, y1_ref, y2_ref, y3_ref)):
        out = out + gate[:, kk:kk + 1] * _unpack_rows(y_ref)
    o_ref[...] = out


def _const_spec(shape):
    nd = len(shape)
    return pl.BlockSpec(shape, lambda *_: (0,) * nd)


def _layer(x, norm1_g, w_in, q_norm_g, k_norm_g, w_pool_grp, pool_scale, w_pool_up, w_attn_up, w_out, norm2_g,
           w_router, b_router, w_gate_up, b_gate_up, w_down, b_down):
    B, S, D = x.shape
    N = B * S
    pw = w_pool_up.shape[0]
    sw = w_attn_up.shape[0]
    n_exp = w_router.shape[1]
    de = w_down.shape[1]
    heads = sw // SB_HEAD_DIM
    assert pw == len(POOL_WINDOWS) * POOL_GROUP_DIM and heads % 2 == 0 and n_exp <= LANES
    assert D == 2 * PACK_ROWS * LANES and w_in.shape[1] == pw + 3 * sw + 2 * D
    tm = 512 if S % 512 == 0 else 256
    assert S % tm == 0 and S % (ATTN_SUB * ATTN_BLOCK) == 0
    xf = x.reshape(N, D)
    cparams = functools.partial(pltpu.CompilerParams, vmem_limit_bytes=VMEM_LIMIT)

    hsum = (jnp.arange(sw)[:, None] // SB_HEAD_DIM == jnp.arange(sw)[None, :] // SB_HEAD_DIM).astype(BF16)
    tm_in = 2 * tm if S % (2 * tm) == 0 else tm
    nt = S // tm_in
    tok_spec = lambda w: pl.BlockSpec((tm_in, w), lambda b, i: (b * nt + i, 0))
    w_in_bf = w_in.astype(BF16)
    n_in = pw + 3 * sw + D
    g1 = norm1_g.reshape(1, D)
    q2, k2, v2, pg = pl.pallas_call(
        functools.partial(_mixer_in_kernel, tm=tm_in, pw=pw, sw=sw),
        grid=(B, nt),
        in_specs=[tok_spec(D), _const_spec((1, D)), _const_spec((D, n_in)), _const_spec((1, sw)),
                  _const_spec((1, sw)), _const_spec((sw, sw)),
                  _const_spec((len(POOL_WINDOWS), POOL_GROUP_DIM, POOL_GROUP_DIM)), _const_spec((1, pw)),
                  _const_spec((pw, D))],
        out_specs=[tok_spec(sw), tok_spec(sw), tok_spec(sw), tok_spec(D)],
        out_shape=[jax.ShapeDtypeStruct((N, sw), BF16)] * 3 + [jax.ShapeDtypeStruct((N, D), BF16)],
        scratch_shapes=[pltpu.VMEM((POOL_HALO, pw), F32)],
        compiler_params=cparams(dimension_semantics=("arbitrary", "arbitrary")),
        name="mixer_in",
    )(xf, g1, w_in_bf[:, :n_in], jnp.tile(q_norm_g, heads).reshape(1, sw),
      jnp.tile(k_norm_g, heads).reshape(1, sw), hsum, w_pool_grp.astype(BF16), pool_scale.reshape(1, pw),
      w_pool_up.astype(BF16))

    bq = ATTN_BLOCK
    rows_q = ATTN_SUB * bq
    nq = S // rows_q
    jj = jnp.arange(bq)
    tri = jnp.concatenate([(jj[:, None] >= jj[None, :]).astype(BF16), jnp.ones((bq, bq), BF16)], axis=1)
    n_pairs = heads // 2
    n_units = ATTN_SUB * n_pairs
    kv_spec = pl.BlockSpec(memory_space=pl.ANY)
    sba = pl.pallas_call(
        functools.partial(_attn_kernel, n_pairs=n_pairs),
        grid=(B, nq),
        in_specs=[pl.BlockSpec((rows_q, sw), lambda b, qi: (b * nq + qi, 0)), kv_spec, kv_spec,
                  _const_spec((bq, 2 * bq))],
        out_specs=pl.BlockSpec((rows_q, sw), lambda b, qi: (b * nq + qi, 0)),
        out_shape=jax.ShapeDtypeStruct((N, sw), BF16),
        scratch_shapes=([pltpu.VMEM((2, S, sw), BF16), pltpu.VMEM((2, S, sw), BF16),
                         pltpu.SemaphoreType.DMA((2,)), pltpu.SemaphoreType.DMA((2,))]
                        + [pltpu.VMEM((2 * bq, LANES), BF16)] * n_units + [pltpu.VMEM((2 * bq, LANES), F32)] * (2 * n_units)),
        compiler_params=cparams(dimension_semantics=("arbitrary", "arbitrary")),
        name="sb_attn",
    )(q2, k2, v2, tri)

    wr = jnp.zeros((D, LANES), F32).at[:, :n_exp].set(w_router)
    wr_hi = wr.astype(BF16)
    wr_lo = (wr - wr_hi.astype(F32)).astype(BF16)
    br = jnp.full((1, LANES), -jnp.inf, F32).at[0, :n_exp].set(b_router)
    ltri = (jnp.arange(tm)[:, None] > jnp.arange(tm)[None, :]).astype(BF16)
    assert N % (MOE_GROUPS * tm) == 0
    ng = N // MOE_GROUPS
    steps = ng // tm
    n_assign = ng * TOP_K
    n_blocks = -(-(n_assign + n_exp * (EXPERT_BLOCK - 1)) // EXPERT_BLOCK)
    n_rows = n_blocks * EXPERT_BLOCK
    plane_spec = lambda rows, index: pl.BlockSpec((PACK_ROWS, rows, LANES), lambda i, *_: (0, index(i, *_), 0))
    row_spec = lambda w: pl.BlockSpec((tm, w), lambda i: (i, 0))
    w_attn_up_bf, w_out_bf, g2 = w_attn_up.astype(BF16), w_out.astype(BF16), norm2_g.reshape(1, D)
    bgu, bdn = b_gate_up.reshape(n_exp, 1, 2 * de), b_down.reshape(n_exp, 1, D)
    out = None
    for grp in range(MOE_GROUPS):
        grp_spec = lambda w, first=grp * steps: pl.BlockSpec((tm, w), lambda i: (first + i, 0))

        cur = lambda i: jnp.minimum(i, steps - 1)
        prev = lambda i: jnp.maximum(i - 1, 0)
        in_spec = lambda w, first=grp * steps: pl.BlockSpec((tm, w), lambda i: (first + cur(i), 0))
        x1, hpk, ri, gate, cnt = pl.pallas_call(
            functools.partial(_mixer_out_kernel, tm=tm),
            grid=(steps + 1,),
            in_specs=[in_spec(D), in_spec(D), in_spec(sw), _const_spec((1, D)), _const_spec((D, D)),
                      _const_spec((sw, D)), _const_spec((D, D)),
                      _const_spec((1, D)), _const_spec((D, LANES)), _const_spec((D, LANES)), _const_spec((1, LANES)),
                      _const_spec((tm, tm))],
            out_specs=[pl.BlockSpec((tm, D), lambda i: (cur(i), 0)), plane_spec(tm, cur),
                       pl.BlockSpec((2 * TOP_K, tm), lambda i: (0, prev(i))),
                       pl.BlockSpec((tm, LANES), lambda i: (prev(i), 0)), _const_spec((8, LANES))],
            out_shape=[jax.ShapeDtypeStruct((ng, D), F32), jax.ShapeDtypeStruct((PACK_ROWS, ng, LANES), U32),
                       jax.ShapeDtypeStruct((2 * TOP_K, ng), I32), jax.ShapeDtypeStruct((ng, LANES), F32),
                       jax.ShapeDtypeStruct((8, LANES), F32)],
            scratch_shapes=[pltpu.VMEM((2, tm, LANES), F32)],
            compiler_params=cparams(dimension_semantics=("arbitrary",)),
            name="mixer_out",
        )(xf, pg, sba, g1, w_in_bf[:, n_in:], w_attn_up_bf, w_out_bf, g2, wr_hi, wr_lo, br, ltri)

        counts = cnt[0, :n_exp].astype(I32)
        padded = (counts + EXPERT_BLOCK - 1) // EXPERT_BLOCK * EXPERT_BLOCK
        padded_end = jnp.cumsum(padded)
        start_pad = padded_end - padded
        td = min(ng, 8192)
        dest = pl.pallas_call(
            _dest_kernel,
            grid_spec=pltpu.PrefetchScalarGridSpec(
                num_scalar_prefetch=1,
                grid=(ng // td,),
                in_specs=[pl.BlockSpec((2 * TOP_K, td), lambda i, sp: (0, i))],
                out_specs=pl.BlockSpec((TOP_K, td), lambda i, sp: (0, i)),
            ),
            out_shape=jax.ShapeDtypeStruct((TOP_K, ng), I32),
            compiler_params=cparams(dimension_semantics=("arbitrary",)),
            name="dest",
        )(start_pad, ri).reshape(-1, SC_CHUNK)
        block_start = jnp.arange(n_blocks, dtype=I32) * EXPERT_BLOCK
        block_expert = jnp.minimum(jnp.sum((padded_end[None, :] <= block_start[:, None]).astype(I32), axis=1),
                                   n_exp - 1)
        n_used = (padded_end[-1] // EXPERT_BLOCK).astype(I32).reshape(1)

        xs = _sc_scatter_rows(hpk, dest, n_rows)
        xs = pl.pallas_call(
            _padfill_kernel,
            grid_spec=pltpu.PrefetchScalarGridSpec(
                num_scalar_prefetch=2,
                grid=(1,),
                in_specs=[pl.BlockSpec(memory_space=pl.ANY)],
                out_specs=pl.BlockSpec(memory_space=pl.ANY),
                scratch_shapes=[pltpu.VMEM((PACK_ROWS, 1 << ((EXPERT_BLOCK - 1).bit_length() - 1), LANES), U32),
                                pltpu.SemaphoreType.DMA(())],
            ),
            out_shape=jax.ShapeDtypeStruct((PACK_ROWS, n_rows, LANES), U32),
            input_output_aliases={2: 0},
            compiler_params=cparams(dimension_semantics=("arbitrary",)),
            name="padfill",
        )(start_pad + counts, padded - counts, xs)

        experts = jnp.arange(n_exp, dtype=I32)
        used = counts > 0
        block_valid = jnp.clip(jnp.sum(jnp.where(block_expert[:, None] == experts[None, :],
                                                 (start_pad + counts)[None, :], 0), axis=1) - block_start,
                               0, EXPERT_BLOCK)
        ordinal = jnp.cumsum(used.astype(I32)) - 1
        next_used = jnp.min(jnp.where(jnp.logical_and(experts[None, :] > experts[:, None], used[None, :]),
                                      experts[None, :], n_exp), axis=1)
        next_used = jnp.where(next_used == n_exp, -1, next_used)
        per_block = lambda table: jnp.sum(jnp.where(block_expert[:, None] == experts[None, :], table[None, :], 0), axis=1)
        b_spec = lambda shape: pl.BlockSpec(shape, lambda i, be, *_: (be[i], 0, 0))
        ys = pl.pallas_call(
            _experts_kernel,
            grid_spec=pltpu.PrefetchScalarGridSpec(
                num_scalar_prefetch=5,
                grid=(n_blocks,),
                in_specs=[plane_spec(EXPERT_BLOCK, lambda i, be, nb, *_: jnp.minimum(i, nb[0] - 1)),
                          pl.BlockSpec(memory_space=pl.ANY), b_spec((1, 1, 2 * de)),
                          pl.BlockSpec(memory_space=pl.ANY), b_spec((1, 1, D))],
                out_specs=plane_spec(EXPERT_BLOCK, lambda i, *_: i),
                scratch_shapes=[pltpu.VMEM((2, D, 2 * de), F32), pltpu.VMEM((2, de, D), F32),
                                pltpu.VMEM((2, D, 2 * de), BF16), pltpu.VMEM((2, de, D), BF16),
                                pltpu.SemaphoreType.DMA((2,)), pltpu.SemaphoreType.DMA((2,))],
            ),
            out_shape=jax.ShapeDtypeStruct((PACK_ROWS, n_rows, LANES), U32),
            compiler_params=cparams(dimension_semantics=("arbitrary",)),
            name="experts",
        )(block_expert, n_used, block_valid, per_block(ordinal % 2), per_block(next_used), xs, w_gate_up, bgu, w_down, bdn)

        y4 = _sc_gather_rows(ys, dest.reshape(-1, SC_CHUNK))

        operands = [x1, gate, y4, y4, y4, y4] + ([] if out is None else [out])
        out = pl.pallas_call(
            _combine_kernel,
            grid=(steps,),
            in_specs=([row_spec(D), row_spec(LANES)]
                      + [plane_spec(tm, lambda i, kk=kk: kk * steps + i) for kk in range(TOP_K)]
                      + ([] if out is None else [pl.BlockSpec(memory_space=pl.ANY)])),
            out_specs=grp_spec(D),
            out_shape=jax.ShapeDtypeStruct((N, D), F32),
            input_output_aliases={} if out is None else {len(operands) - 1: 0},
            compiler_params=cparams(dimension_semantics=("arbitrary",)),
            name="combine",
        )(*operands)
    return out.reshape(B, S, D)


def kernel(x, norm1_g, w_in, q_norm_g, k_norm_g, w_pool_grp, pool_scale, w_pool_up, w_attn_up, w_out, norm2_g,
           w_router, b_router, w_gate_up, b_gate_up, w_down, b_down):
    for layer in range(norm1_g.shape[0]):
        x = _layer(x, norm1_g[layer], w_in[layer], q_norm_g[layer], k_norm_g[layer], w_pool_grp[layer],
                   pool_scale[layer], w_pool_up[layer], w_attn_up[layer], w_out[layer], norm2_g[layer],
                   w_router[layer], b_router[layer], w_gate_up[layer], b_gate_up[layer], w_down[layer],
                   b_down[layer])
    return x
```

```python
import functools

import jax
import jax.numpy as jnp
from jax import lax
from jax.experimental import pallas as pl
from jax.experimental.pallas import tpu as pltpu
from jax.experimental.pallas import tpu_sc as plsc

F32 = jnp.float32
BF16 = jnp.bfloat16
U32 = jnp.uint32
I32 = jnp.int32

EPS = 1e-6
POOL_WINDOWS = (2, 4, 8, 16)
POOL_GROUP_DIM = 128
POOL_HALO = 16
SB_HEAD_DIM = 64
TOP_K = 4
SWIGLU_LIMIT = 7.0
SWIGLU_ALPHA = 1.702
EXPERT_BLOCK = 1024
EXPERT_CAST_COLS = 256
EXPERT_SPLIT = 8
LANES = 128
PACK_ROWS = 4
ATTN_BLOCK = 128
ATTN_SUB = 2
ATTN_EXIT_BITS = 70.0
LOG2_E = 1.4426950408889634
VMEM_LIMIT = 56 * 1024 * 1024
SC_CHUNK = 128
MOE_GROUPS = 2


def _dot(a, b):
    return jnp.dot(a, b, preferred_element_type=F32)


def _split_bf16(x):
    hi = x.astype(BF16)
    lo = (x - hi.astype(F32)).astype(BF16)
    return hi, lo


def _pack_rows(v, out_ref):
    half = v.shape[1] // 2
    lo = lax.bitcast_convert_type(v[:, :half].astype(BF16).astype(F32), U32) >> 16
    hi = lax.bitcast_convert_type(v[:, half:].astype(BF16).astype(F32), U32) & jnp.uint32(0xFFFF0000)
    w = lo | hi
    for c in range(PACK_ROWS):
        out_ref[c] = w[:, c * LANES:(c + 1) * LANES]


def _unpack_rows(ref):
    los, his = [], []
    for c in range(PACK_ROWS):
        w = ref[c]
        los.append(lax.bitcast_convert_type(w << 16, F32))
        his.append(lax.bitcast_convert_type(w & jnp.uint32(0xFFFF0000), F32))
    return jnp.concatenate(los + his, axis=1)


def _mixer_in_kernel(x_ref, g1_ref, win_ref, gq_ref, gk_ref, hsum_ref, wgrp_ref, pscale_ref, wpu_ref,
                     q_ref, k_ref, v_ref, p_ref, tail_ref, *, tm, pw, sw):
    i = pl.program_id(1)

    @pl.when(i == 0)
    def _():
        tail_ref[...] = jnp.zeros_like(tail_ref)

    x = x_ref[...]
    ms = jnp.mean(x * x, axis=-1, keepdims=True)
    h = (x * lax.rsqrt(ms + EPS) * g1_ref[...]).astype(BF16)

    def head_norm(t, gain):
        ss = _dot((t * t).astype(BF16), hsum_ref[...])
        return t * lax.rsqrt(ss * (1.0 / SB_HEAD_DIM) + EPS) * gain

    def project_q():
        q = _dot(h, win_ref[:, pw:pw + sw])
        q_ref[...] = (head_norm(q, gq_ref[...]) * (SB_HEAD_DIM ** -0.5 * LOG2_E)).astype(BF16)

    def project_k():
        k = _dot(h, win_ref[:, pw + sw:pw + 2 * sw])
        k_ref[...] = head_norm(k, gk_ref[...]).astype(BF16)

    def project_v():
        v_ref[...] = _dot(h, win_ref[:, pw + 2 * sw:pw + 3 * sw]).astype(BF16)

    d_model = x.shape[1]
    pool_gate = []

    def project_pool_gate():
        pool_gate.append(jax.nn.sigmoid(_dot(h, win_ref[:, pw + 3 * sw:pw + 3 * sw + d_model])))

    u = _dot(h, win_ref[:, 0:pw])
    xx = jnp.concatenate([tail_ref[...], u], axis=0)
    tail_ref[...] = u[tm - POOL_HALO:, :]
    pos = i * tm + lax.broadcasted_iota(I32, (tm, POOL_GROUP_DIM), 0)
    mixed = []
    for (g, w), project in zip(enumerate(POOL_WINDOWS), (project_q, project_k, project_v, project_pool_gate)):
        project()
        s = xx[:, g * POOL_GROUP_DIM:(g + 1) * POOL_GROUP_DIM]
        step = 1
        while step < w:
            s = s + pltpu.roll(s, step, axis=0)
            step *= 2
        count = jnp.minimum(pos + 1, w).astype(F32)
        ug = u[:, g * POOL_GROUP_DIM:(g + 1) * POOL_GROUP_DIM]
        d = s[POOL_HALO:, :] / count - ug
        mixed.append(_dot(d.astype(BF16), wgrp_ref[g]))
    pm = jnp.concatenate(mixed, axis=1) * pscale_ref[...]
    pool_out = _dot(pm.astype(BF16), wpu_ref[...])
    p_ref[...] = (pool_gate[0] * pool_out).astype(BF16)


def _attn_kernel(q_ref, k_hbm, v_hbm, tri_ref, o_ref, kbuf, vbuf, ksem, vsem, *scratch, n_pairs):
    units = [(sub, p) for sub in range(ATTN_SUB) for p in range(n_pairs)]
    n_units = len(units)
    qs, acc, rr = scratch[:n_units], scratch[n_units:2 * n_units], scratch[2 * n_units:]
    bq = ATTN_BLOCK
    batch = pl.program_id(0)
    seq = kbuf.shape[1]

    def kv_copies(b):
        rows = pl.ds(pl.multiple_of(b * seq, seq), seq)
        return (pltpu.make_async_copy(k_hbm.at[rows], kbuf.at[b % 2], ksem.at[b % 2]),
                pltpu.make_async_copy(v_hbm.at[rows], vbuf.at[b % 2], vsem.at[b % 2]))

    @pl.when(pl.program_id(1) == 0)
    def _():
        @pl.when(batch == 0)
        def _():
            for c in kv_copies(batch):
                c.start()

        for c in kv_copies(batch):
            c.wait()

        @pl.when(batch + 1 < pl.num_programs(0))
        def _():
            for c in kv_copies(batch + 1):
                c.start()

    k_ref = kbuf.at[batch % 2]
    v_ref = vbuf.at[batch % 2]
    first_block = pl.program_id(1) * ATTN_SUB
    first_head = lax.broadcasted_iota(I32, (bq, LANES), 1) < SB_HEAD_DIM
    for u, (sub, p) in enumerate(units):
        q2 = q_ref[sub * bq:(sub + 1) * bq, p * LANES:(p + 1) * LANES]
        qs[u][:bq] = jnp.where(first_head, q2, jnp.zeros_like(q2))
        qs[u][bq:] = jnp.where(first_head, jnp.zeros_like(q2), q2)
    row = lax.broadcasted_iota(I32, (2 * bq, bq), 0)
    col = lax.broadcasted_iota(I32, (2 * bq, bq), 1)
    causal = col < (row & (bq - 1))
    contract_last = (((1,), (1,)), ((), ()))

    def softplus(z):
        return jnp.maximum(z, 0.0) + jnp.log2(1.0 + jnp.exp2(-jnp.abs(z)))

    def suffix_sums(sp):
        return _dot(sp.astype(BF16), tri_ref[...])

    def cols(ref, block, p):
        start = pl.multiple_of(jnp.maximum(block, 0) * bq, bq)
        return ref[pl.ds(start, bq), p * LANES:(p + 1) * LANES]

    def scores(u, block):
        return lax.dot_general(qs[u][...], cols(k_ref, block, units[u][1]), contract_last, preferred_element_type=F32)

    diag = [first_block + sub for sub, _ in units]
    z_d, z_n, s_d, s_n = {}, {}, {}, {}

    def stage_scores(u):
        z_d[u] = scores(u, diag[u])
        z_n[u] = scores(u, diag[u] - 1)

    def stage_sums(u):
        s_d[u] = suffix_sums(jnp.where(causal, softplus(z_d[u]), 0.0))
        s_n[u] = suffix_sums(jnp.where(diag[u] >= 1, softplus(z_n[u]), 0.0))

    def stage_values(u):
        p = units[u][1]
        a_d = jnp.where(causal, jnp.exp2(z_d[u] - s_d[u][:, :bq]), 0.0)
        r_d = s_d[u][:, bq:]
        a_n = jnp.where(diag[u] >= 1, jnp.exp2(z_n[u] - (r_d + s_n[u][:, :bq])), 0.0)
        acc[u][...] = (_dot(a_d.astype(BF16), cols(v_ref, diag[u], p)) + _dot(a_n.astype(BF16), cols(v_ref, diag[u] - 1, p)))
        rr[u][...] = r_d + s_n[u][:, bq:]

    for t in range(n_units + 2):
        if t < n_units:
            stage_scores(t)
        if 0 <= t - 1 < n_units:
            stage_sums(t - 1)
        if 0 <= t - 2 < n_units:
            stage_values(t - 2)

    def r_min():
        m = rr[0][...]
        for u in range(1, n_units):
            m = jnp.minimum(m, rr[u][...])
        return jnp.min(m)

    def cond(c):
        back, rm = c
        return jnp.logical_and(diag[-1] - back >= 0, rm < ATTN_EXIT_BITS)

    def body(c):
        back, _ = c
        blocks = [d - back for d in diag]
        zs = [scores(u, blocks[u]) for u in range(n_units)]
        ss = [suffix_sums(jnp.where(blocks[u] >= 0, softplus(zs[u]), 0.0)) for u in range(n_units)]
        for u, (_, p) in enumerate(units):
            r = rr[u][...]
            a = jnp.where(blocks[u] >= 0, jnp.exp2(zs[u] - (r + ss[u][:, :bq])), 0.0)
            acc[u][...] += _dot(a.astype(BF16), cols(v_ref, blocks[u], p))
            rr[u][...] = r + ss[u][:, bq:]
        return back + 1, r_min()

    lax.while_loop(cond, body, (2, r_min()))
    for u, (sub, p) in enumerate(units):
        o_ref[sub * bq:(sub + 1) * bq, p * LANES:(p + 1) * LANES] = (
            jnp.where(first_head, acc[u][:bq], acc[u][bq:]).astype(BF16))


def _mixer_out_kernel(x_ref, p_ref, sba_ref, g1_ref, wga_ref, wau_ref, wout_ref, g2_ref, wr_hi_ref, wr_lo_ref, br_ref,
                      ltri_ref, x1_ref, hp_ref, ri_ref, gate_ref, cnt_ref, logit_ref, *, tm):
    step = pl.program_id(0)

    @pl.when(step == 0)
    def _():
        cnt_ref[...] = jnp.zeros_like(cnt_ref)
        logit_ref[1] = jnp.zeros((tm, LANES), F32)

    logits = logit_ref[(step + 1) % 2]
    routed = jnp.where(step >= 1, 1.0, 0.0)
    lane = lax.broadcasted_iota(I32, logits.shape, 1).astype(F32)
    work = logits
    vals, idxs = [], []

    def topk_round():
        nonlocal work
        m = jnp.max(work, axis=-1, keepdims=True)
        ik = jnp.min(jnp.where(work == m, lane, float(LANES)), axis=-1, keepdims=True)
        vals.append(m)
        idxs.append(ik)
        work = jnp.where(lane == ik, -jnp.inf, work)

    d_model = x_ref.shape[1]
    quarter = d_model // TOP_K
    topk_round()
    x = x_ref[...]
    h1 = (x * lax.rsqrt(jnp.mean(x * x, axis=-1, keepdims=True) + EPS) * g1_ref[...]).astype(BF16)
    attn_gate = jax.nn.sigmoid(_dot(h1, wga_ref[...]))
    attn_out = _dot(sba_ref[...], wau_ref[...])
    merged = (p_ref[...].astype(F32) + attn_gate * attn_out).astype(BF16)
    x1_parts = []
    for c in range(TOP_K):
        if c >= 1:
            topk_round()
        cols = slice(c * quarter, (c + 1) * quarter)
        x1_parts.append(x[:, cols] + _dot(merged, wout_ref[:, cols]))
    x1 = jnp.concatenate(x1_parts, axis=1)
    x1_ref[...] = x1

    es = [jnp.exp(v - vals[0]) for v in vals]
    denom = es[0] + es[1] + es[2] + es[3]
    hot = jnp.zeros(logits.shape, F32)
    for ik in idxs:
        hot = hot + jnp.where(lane == ik, routed, 0.0)
    before = _dot(ltri_ref[...], hot.astype(BF16)) + cnt_ref[0:1, :]

    ms = jnp.mean(x1 * x1, axis=-1, keepdims=True)
    h2 = x1 * lax.rsqrt(ms + EPS) * g2_ref[...]
    _pack_rows(h2, hp_ref)

    ri = jnp.zeros(logits.shape, F32)
    gt = jnp.zeros(logits.shape, F32)
    for kk in range(TOP_K):
        rank = jnp.sum(jnp.where(lane == idxs[kk], before, 0.0), axis=-1, keepdims=True)
        ri = jnp.where(lane == kk, idxs[kk], ri)
        ri = jnp.where(lane == TOP_K + kk, rank, ri)
        gt = jnp.where(lane == kk, es[kk] / denom, gt)
    ri_ref[...] = ri.T[:2 * TOP_K].astype(I32)
    gate_ref[...] = gt
    cnt_ref[...] = cnt_ref[...] + jnp.sum(hot, axis=0, keepdims=True)

    h_hi, h_lo = _split_bf16(h2)
    logit_ref[step % 2] = (_dot(h_hi, wr_hi_ref[...]) + _dot(h_hi, wr_lo_ref[...]) + _dot(h_lo, wr_hi_ref[...])
                           + br_ref[...])


def _dest_kernel(start_ref, ri_ref, dest_ref):
    idx = ri_ref[:TOP_K, :]
    dest = ri_ref[TOP_K:, :]
    for e in range(start_ref.shape[0]):
        dest = dest + jnp.where(idx == e, start_ref[e], 0)
    dest_ref[...] = dest


def _sc_mesh():
    info = plsc.get_sparse_core_info()
    mesh = plsc.VectorSubcoreMesh(core_axis_name="c", subcore_axis_name="s")
    return mesh, info.num_cores, info.num_subcores, info.num_lanes


def _sc_scatter_rows(rows, idx, n_out):
    mesh, nc, ns, lanes = _sc_mesh()
    nw = nc * ns
    planes, m, width = rows.shape
    ch = SC_CHUNK
    n_ch = m // ch // nw
    half = planes // 2
    assert planes % 2 == 0 and n_ch * ch * nw == m and idx.shape == (m // ch * TOP_K, ch) and n_ch >= 2
    assert n_ch % 8 == 0

    @functools.partial(
        pl.kernel, mesh=mesh, out_type=jax.ShapeDtypeStruct((planes * n_out, width), rows.dtype),
        scratch_types=([pltpu.VMEM((n_ch * TOP_K, ch), I32)] + [pltpu.VMEM((ch, width), rows.dtype)] * planes
                       + [pltpu.VMEM((TOP_K, ch), I32)] * planes
                       + [pltpu.SemaphoreType.DMA((planes,)), pltpu.SemaphoreType.DMA((planes,))]))
    def scatter_kernel(rows_hbm, idx_hbm, out_hbm, idx_v, *rest):
        bufs, ibufs, rsem, ssem = rest[:planes], rest[planes:2 * planes], rest[2 * planes], rest[2 * planes + 1]
        wid = lax.axis_index("s") * nc + lax.axis_index("c")
        for kk in range(TOP_K):
            pltpu.sync_copy(idx_hbm.at[pl.ds(kk * (m // ch) + wid * n_ch, n_ch)], idx_v.at[pl.ds(kk * n_ch, n_ch)])
        base = wid * n_ch * ch

        def read(j, b):
            return pltpu.make_async_copy(rows_hbm.at[b, pl.ds(pl.multiple_of(base + j * ch, ch), ch)], bufs[b], rsem.at[b])

        def scatters(b):
            return [pltpu.make_async_copy(bufs[b], out_hbm.at[ibufs[b].at[kk]], ssem.at[b]) for kk in range(TOP_K)]

        def start_scatters(j, b):
            for kk in range(TOP_K):
                for t in range(0, ch, lanes):
                    ibufs[b][kk, pl.ds(t, lanes)] = idx_v[kk * n_ch + j, pl.ds(t, lanes)] + b * n_out
            for c in scatters(b):
                c.start()

        def wait_scatters(b):
            for c in scatters(b):
                c.wait()

        def finish(j, b):
            pj, pb = (j, b - half) if b >= half else (j - 1, b + half)
            read(pj, pb).wait()
            start_scatters(pj, pb)

        for b in range(planes):
            read(0, b).start()
        for b in range(half, planes):
            finish(0, b)

        @pl.loop(1, n_ch)
        def _(j):
            for b in range(planes):
                wait_scatters(b)
                read(j, b).start()
                finish(j, b)

        for b in range(half):
            finish(n_ch, b)
        for b in range(planes):
            wait_scatters(b)

    return scatter_kernel(rows, idx).reshape(planes, n_out, width)


def _padfill_kernel(start_ref, len_ref, xs_in_ref, xs_ref, zeros_ref, sem):
    del xs_in_ref
    zeros_ref[...] = jnp.zeros_like(zeros_ref)
    bits = [1 << b for b in reversed(range((EXPERT_BLOCK - 1).bit_length()))]

    def pieces(e):
        n = len_ref[e]
        for bit in bits:
            row0 = start_ref[e] + (n & ~(2 * bit - 1))
            copy = pltpu.make_async_copy(zeros_ref.at[:, pl.ds(0, bit)], xs_ref.at[:, pl.ds(row0, bit)], sem)
            yield (n & bit) != 0, copy

    def start(e, c):
        for on, copy in pieces(e):
            pl.when(on)(copy.start)
        return c

    def wait(e, c):
        for on, copy in pieces(e):
            pl.when(on)(copy.wait)
        return c

    lax.fori_loop(0, start_ref.shape[0], start, 0)
    lax.fori_loop(0, start_ref.shape[0], wait, 0)


def _experts_kernel(be_ref, nb_ref, valid_ref, slot_ref, next_ref, xs_ref, wgu_hbm, bgu_ref, wd_hbm, bd_ref, ys_ref,
                    wgu_buf, wd_buf, wgu_bf_ref, wd_bf_ref, gu_sem, d_sem):
    del nb_ref
    blk = pl.program_id(0)
    de = wd_hbm.shape[1]
    valid = valid_ref[blk]
    expert = be_ref[blk]
    slot = slot_ref[blk]
    following = next_ref[blk]
    prev1 = be_ref[jnp.maximum(blk - 1, 0)]
    prev2 = be_ref[jnp.maximum(blk - 2, 0)]

    def weight_copies(e, s):
        return (pltpu.make_async_copy(wgu_hbm.at[e], wgu_buf.at[s], gu_sem.at[s]),
                pltpu.make_async_copy(wd_hbm.at[e], wd_buf.at[s], d_sem.at[s]))

    def cast_rows(s, piece, pieces):
        for buf, bf_ref in ((wgu_buf, wgu_bf_ref), (wd_buf, wd_bf_ref)):
            rows = buf.shape[1] // pieces
            bf_ref[s, pl.ds(piece * rows, rows), :] = buf[s, pl.ds(piece * rows, rows), :].astype(BF16)

    first_of_expert = jnp.logical_or(blk == 0, expert != prev1)
    after = jnp.minimum(blk + 1, pl.num_programs(0) - 1)
    last_full = jnp.logical_and(valid == EXPERT_BLOCK,
                                jnp.logical_or(be_ref[after] != expert, valid_ref[after] < EXPERT_BLOCK))
    cast_in_mlp = jnp.logical_and(last_full, following >= 0)
    cast_before = jnp.logical_or(jnp.logical_and(blk >= 1, valid_ref[jnp.maximum(blk - 1, 0)] == EXPERT_BLOCK),
                                 jnp.logical_and(blk >= 2, prev1 == prev2))

    @pl.when(jnp.logical_and(valid > 0, first_of_expert))
    def _():
        @pl.when(blk == 0)
        def _():
            for c in weight_copies(expert, slot):
                c.start()

        @pl.when(jnp.logical_not(cast_before))
        def _():
            for c in weight_copies(expert, slot):
                c.wait()
            cast_rows(slot, 0, 1)

        @pl.when(following >= 0)
        def _():
            for c in weight_copies(following, 1 - slot):
                c.start()

    def activation(gu_gate, gu_lin):
        glu = jnp.minimum(gu_gate, SWIGLU_LIMIT)
        lin = jnp.clip(gu_lin, -SWIGLU_LIMIT, SWIGLU_LIMIT)
        return (glu * jax.nn.sigmoid(SWIGLU_ALPHA * glu) * (lin + 1.0)).astype(BF16)

    def mlp(m):
        x = _unpack_rows(xs_ref.at[:, pl.ds(0, m)]).astype(BF16)
        gu = _dot(x, wgu_bf_ref[slot]) + bgu_ref[0]
        y = _dot(activation(gu[:, :de], gu[:, de:]), wd_bf_ref[slot]) + bd_ref[0]
        _pack_rows(y, ys_ref.at[:, pl.ds(0, m)])
        if m < EXPERT_BLOCK:
            ys_ref[:, m:, :] = jnp.zeros((PACK_ROWS, EXPERT_BLOCK - m, LANES), U32)

    def mlp_and_cast():
        for c in weight_copies(following, 1 - slot):
            c.wait()
        x = _unpack_rows(xs_ref).astype(BF16)
        bgu = bgu_ref[0]
        pieces = 2 * de // EXPERT_CAST_COLS
        gu = []
        for j in range(pieces):
            cols = slice(j * EXPERT_CAST_COLS, (j + 1) * EXPERT_CAST_COLS)
            gu.append(_dot(x, wgu_bf_ref[slot, :, cols]) + bgu[:, cols])
            cast_rows(1 - slot, j, pieces)
        act = jnp.concatenate([activation(g, l) for g, l in zip(gu[:pieces // 2], gu[pieces // 2:])], axis=1)
        y = _dot(act, wd_bf_ref[slot]) + bd_ref[0]
        _pack_rows(y, ys_ref)

    pl.when(cast_in_mlp)(mlp_and_cast)
    step = EXPERT_BLOCK // EXPERT_SPLIT
    for q in range(EXPERT_SPLIT):
        here = jnp.logical_and(valid > q * step, valid <= (q + 1) * step)
        if q == EXPERT_SPLIT - 1:
            here = jnp.logical_and(here, jnp.logical_not(cast_in_mlp))
        pl.when(here)(functools.partial(mlp, (q + 1) * step))

    @pl.when(valid == 0)
    def _():
        ys_ref[...] = jnp.zeros_like(ys_ref)


def _sc_gather_rows(table, idx):
    mesh, nc, ns, lanes = _sc_mesh()
    nw = nc * ns
    planes, n_tab, width = table.shape
    n_idx_rows, ch = idx.shape
    n_ch = n_idx_rows // nw
    m = n_idx_rows * ch
    half = planes // 2
    assert planes % 2 == 0 and ch == SC_CHUNK and n_ch * nw == n_idx_rows and n_ch >= 2

    @functools.partial(
        pl.kernel, mesh=mesh, out_type=jax.ShapeDtypeStruct((planes, m, width), table.dtype),
        scratch_types=([pltpu.VMEM((n_ch, ch), I32)] + [pltpu.VMEM((ch, width), table.dtype)] * planes
                       + [pltpu.VMEM((8, ch), I32)] * planes
                       + [pltpu.SemaphoreType.DMA((planes,)), pltpu.SemaphoreType.DMA((planes,))]))
    def gather_kernel(table_hbm, idx_hbm, out_hbm, idx_v, *rest):
        bufs, ibufs, gsem, wsem = rest[:planes], rest[planes:2 * planes], rest[2 * planes], rest[2 * planes + 1]
        wid = lax.axis_index("s") * nc + lax.axis_index("c")
        pltpu.sync_copy(idx_hbm.at[pl.ds(wid * n_ch, n_ch)], idx_v)
        base = wid * n_ch * ch

        def gather(b):
            return pltpu.make_async_copy(table_hbm.at[ibufs[b].at[0]], bufs[b], gsem.at[b])

        def start_gather(j, b):
            for t in range(0, ch, lanes):
                ibufs[b][0, pl.ds(t, lanes)] = idx_v[j, pl.ds(t, lanes)] + b * n_tab
            gather(b).start()

        def write(j, b):
            return pltpu.make_async_copy(bufs[b], out_hbm.at[b, pl.ds(pl.multiple_of(base + j * ch, ch), ch)], wsem.at[b])

        def finish(j, b):
            pj, pb = (j, b - half) if b >= half else (j - 1, b + half)
            gather(pb).wait()
            write(pj, pb).start()

        for b in range(planes):
            start_gather(0, b)
        for b in range(half, planes):
            finish(0, b)

        @pl.loop(1, n_ch)
        def _(j):
            for b in range(planes):
                write(j - 1, b).wait()
                start_gather(j, b)
                finish(j, b)

        for b in range(half):
            finish(n_ch, b)
        for b in range(planes):
            write(n_ch - 1, b).wait()

    return gather_kernel(table.reshape(planes * n_tab, width), idx)


def _combine_kernel(x1_ref, gate_ref, y0_ref, y1_ref, y2_ref, y3_ref, *rest):
    o_ref = rest[-1]
    gate = gate_ref[...]
    out = x1_ref[...]
    for kk, y_ref in enumerate((y0_ref, y1_ref, y2_ref, y3_ref)):
        out = out + gate[:, kk:kk + 1] * _unpack_rows(y_ref)
    o_ref[...] = out


def _const_spec(shape):
    nd = len(shape)
    return pl.BlockSpec(shape, lambda *_: (0,) * nd)


def _layer(x, norm1_g, w_in, q_norm_g, k_norm_g, w_pool_grp, pool_scale, w_pool_up, w_attn_up, w_out, norm2_g,
           w_router, b_router, w_gate_up, b_gate_up, w_down, b_down):
    B, S, D = x.shape
    N = B * S
    pw = w_pool_up.shape[0]
    sw = w_attn_up.shape[0]
    n_exp = w_router.shape[1]
    de = w_down.shape[1]
    heads = sw // SB_HEAD_DIM
    assert pw == len(POOL_WINDOWS) * POOL_GROUP_DIM and heads % 2 == 0 and n_exp <= LANES
    assert D == 2 * PACK_ROWS * LANES and w_in.shape[1] == pw + 3 * sw + 2 * D
    tm = 512 if S % 512 == 0 else 256
    assert S % tm == 0 and S % (ATTN_SUB * ATTN_BLOCK) == 0
    xf = x.reshape(N, D)
    cparams = functools.partial(pltpu.CompilerParams, vmem_limit_bytes=VMEM_LIMIT)

    hsum = (jnp.arange(sw)[:, None] // SB_HEAD_DIM == jnp.arange(sw)[None, :] // SB_HEAD_DIM).astype(BF16)
    tm_in = 2 * tm if S % (2 * tm) == 0 else tm
    nt = S // tm_in
    tok_spec = lambda w: pl.BlockSpec((tm_in, w), lambda b, i: (b * nt + i, 0))
    w_in_bf = w_in.astype(BF16)
    n_in = pw + 3 * sw + D
    g1 = norm1_g.reshape(1, D)
    q2, k2, v2, pg = pl.pallas_call(
        functools.partial(_mixer_in_kernel, tm=tm_in, pw=pw, sw=sw),
        grid=(B, nt),
        in_specs=[tok_spec(D), _const_spec((1, D)), _const_spec((D, n_in)), _const_spec((1, sw)),
                  _const_spec((1, sw)), _const_spec((sw, sw)),
                  _const_spec((len(POOL_WINDOWS), POOL_GROUP_DIM, POOL_GROUP_DIM)), _const_spec((1, pw)),
                  _const_spec((pw, D))],
        out_specs=[tok_spec(sw), tok_spec(sw), tok_spec(sw), tok_spec(D)],
        out_shape=[jax.ShapeDtypeStruct((N, sw), BF16)] * 3 + [jax.ShapeDtypeStruct((N, D), BF16)],
        scratch_shapes=[pltpu.VMEM((POOL_HALO, pw), F32)],
        compiler_params=cparams(dimension_semantics=("arbitrary", "arbitrary")),
        name="mixer_in",
    )(xf, g1, w_in_bf[:, :n_in], jnp.tile(q_norm_g, heads).reshape(1, sw),
      jnp.tile(k_norm_g, heads).reshape(1, sw), hsum, w_pool_grp.astype(BF16), pool_scale.reshape(1, pw),
      w_pool_up.astype(BF16))

    bq = ATTN_BLOCK
    rows_q = ATTN_SUB * bq
    nq = S // rows_q
    jj = jnp.arange(bq)
    tri = jnp.concatenate([(jj[:, None] >= jj[None, :]).astype(BF16), jnp.ones((bq, bq), BF16)], axis=1)
    n_pairs = heads // 2
    n_units = ATTN_SUB * n_pairs
    kv_spec = pl.BlockSpec(memory_space=pl.ANY)
    sba = pl.pallas_call(
        functools.partial(_attn_kernel, n_pairs=n_pairs),
        grid=(B, nq),
        in_specs=[pl.BlockSpec((rows_q, sw), lambda b, qi: (b * nq + qi, 0)), kv_spec, kv_spec,
                  _const_spec((bq, 2 * bq))],
        out_specs=pl.BlockSpec((rows_q, sw), lambda b, qi: (b * nq + qi, 0)),
        out_shape=jax.ShapeDtypeStruct((N, sw), BF16),
        scratch_shapes=([pltpu.VMEM((2, S, sw), BF16), pltpu.VMEM((2, S, sw), BF16),
                         pltpu.SemaphoreType.DMA((2,)), pltpu.SemaphoreType.DMA((2,))]
                        + [pltpu.VMEM((2 * bq, LANES), BF16)] * n_units + [pltpu.VMEM((2 * bq, LANES), F32)] * (2 * n_units)),
        compiler_params=cparams(dimension_semantics=("arbitrary", "arbitrary")),
        name="sb_attn",
    )(q2, k2, v2, tri)

    wr = jnp.zeros((D, LANES), F32).at[:, :n_exp].set(w_router)
    wr_hi = wr.astype(BF16)
    wr_lo = (wr - wr_hi.astype(F32)).astype(BF16)
    br = jnp.full((1, LANES), -jnp.inf, F32).at[0, :n_exp].set(b_router)
    ltri = (jnp.arange(tm)[:, None] > jnp.arange(tm)[None, :]).astype(BF16)
    assert N % (MOE_GROUPS * tm) == 0
    ng = N // MOE_GROUPS
    steps = ng // tm
    n_assign = ng * TOP_K
    n_blocks = -(-(n_assign + n_exp * (EXPERT_BLOCK - 1)) // EXPERT_BLOCK)
    n_rows = n_blocks * EXPERT_BLOCK
    plane_spec = lambda rows, index: pl.BlockSpec((PACK_ROWS, rows, LANES), lambda i, *_: (0, index(i, *_), 0))
    row_spec = lambda w: pl.BlockSpec((tm, w), lambda i: (i, 0))
    w_attn_up_bf, w_out_bf, g2 = w_attn_up.astype(BF16), w_out.astype(BF16), norm2_g.reshape(1, D)
    bgu, bdn = b_gate_up.reshape(n_exp, 1, 2 * de), b_down.reshape(n_exp, 1, D)
    out = None
    for grp in range(MOE_GROUPS):
        grp_spec = lambda w, first=grp * steps: pl.BlockSpec((tm, w), lambda i: (first + i, 0))

        cur = lambda i: jnp.minimum(i, steps - 1)
        prev = lambda i: jnp.maximum(i - 1, 0)
        in_spec = lambda w, first=grp * steps: pl.BlockSpec((tm, w), lambda i: (first + cur(i), 0))
        x1, hpk, ri, gate, cnt = pl.pallas_call(
            functools.partial(_mixer_out_kernel, tm=tm),
            grid=(steps + 1,),
            in_specs=[in_spec(D), in_spec(D), in_spec(sw), _const_spec((1, D)), _const_spec((D, D)),
                      _const_spec((sw, D)), _const_spec((D, D)),
                      _const_spec((1, D)), _const_spec((D, LANES)), _const_spec((D, LANES)), _const_spec((1, LANES)),
                      _const_spec((tm, tm))],
            out_specs=[pl.BlockSpec((tm, D), lambda i: (cur(i), 0)), plane_spec(tm, cur),
                       pl.BlockSpec((2 * TOP_K, tm), lambda i: (0, prev(i))),
                       pl.BlockSpec((tm, LANES), lambda i: (prev(i), 0)), _const_spec((8, LANES))],
            out_shape=[jax.ShapeDtypeStruct((ng, D), F32), jax.ShapeDtypeStruct((PACK_ROWS, ng, LANES), U32),
                       jax.ShapeDtypeStruct((2 * TOP_K, ng), I32), jax.ShapeDtypeStruct((ng, LANES), F32),
                       jax.ShapeDtypeStruct((8, LANES), F32)],
            scratch_shapes=[pltpu.VMEM((2, tm, LANES), F32)],
            compiler_params=cparams(dimension_semantics=("arbitrary",)),
            name="mixer_out",
        )(xf, pg, sba, g1, w_in_bf[:, n_in:], w_attn_up_bf, w_out_bf, g2, wr_hi, wr_lo, br, ltri)

        counts = cnt[0, :n_exp].astype(I32)
        padded = (counts + EXPERT_BLOCK - 1) // EXPERT_BLOCK * EXPERT_BLOCK
        padded_end = jnp.cumsum(padded)
        start_pad = padded_end - padded
        td = min(ng, 8192)
        dest = pl.pallas_call(
            _dest_kernel,
            grid_spec=pltpu.PrefetchScalarGridSpec(
                num_scalar_prefetch=1,
                grid=(ng // td,),
                in_specs=[pl.BlockSpec((2 * TOP_K, td), lambda i, sp: (0, i))],
                out_specs=pl.BlockSpec((TOP_K, td), lambda i, sp: (0, i)),
            ),
            out_shape=jax.ShapeDtypeStruct((TOP_K, ng), I32),
            compiler_params=cparams(dimension_semantics=("arbitrary",)),
            name="dest",
        )(start_pad, ri).reshape(-1, SC_CHUNK)
        block_start = jnp.arange(n_blocks, dtype=I32) * EXPERT_BLOCK
        block_expert = jnp.minimum(jnp.sum((padded_end[None, :] <= block_start[:, None]).astype(I32), axis=1),
                                   n_exp - 1)
        n_used = (padded_end[-1] // EXPERT_BLOCK).astype(I32).reshape(1)

        xs = _sc_scatter_rows(hpk, dest, n_rows)
        xs = pl.pallas_call(
            _padfill_kernel,
            grid_spec=pltpu.PrefetchScalarGridSpec(
                num_scalar_prefetch=2,
                grid=(1,),
                in_specs=[pl.BlockSpec(memory_space=pl.ANY)],
                out_specs=pl.BlockSpec(memory_space=pl.ANY),
                scratch_shapes=[pltpu.VMEM((PACK_ROWS, 1 << ((EXPERT_BLOCK - 1).bit_length() - 1), LANES), U32),
                                pltpu.SemaphoreType.DMA(())],
            ),
            out_shape=jax.ShapeDtypeStruct((PACK_ROWS, n_rows, LANES), U32),
            input_output_aliases={2: 0},
            compiler_params=cparams(dimension_semantics=("arbitrary",)),
            name="padfill",
        )(start_pad + counts, padded - counts, xs)

        experts = jnp.arange(n_exp, dtype=I32)
        used = counts > 0
        block_valid = jnp.clip(jnp.sum(jnp.where(block_expert[:, None] == experts[None, :],
                                                 (start_pad + counts)[None, :], 0), axis=1) - block_start,
                               0, EXPERT_BLOCK)
        ordinal = jnp.cumsum(used.astype(I32)) - 1
        next_used = jnp.min(jnp.where(jnp.logical_and(experts[None, :] > experts[:, None], used[None, :]),
                                      experts[None, :], n_exp), axis=1)
        next_used = jnp.where(next_used == n_exp, -1, next_used)
        per_block = lambda table: jnp.sum(jnp.where(block_expert[:, None] == experts[None, :], table[None, :], 0), axis=1)
        b_spec = lambda shape: pl.BlockSpec(shape, lambda i, be, *_: (be[i], 0, 0))
        ys = pl.pallas_call(
            _experts_kernel,
            grid_spec=pltpu.PrefetchScalarGridSpec(
                num_scalar_prefetch=5,
                grid=(n_blocks,),
                in_specs=[plane_spec(EXPERT_BLOCK, lambda i, be, nb, *_: jnp.minimum(i, nb[0] - 1)),
                          pl.BlockSpec(memory_space=pl.ANY), b_spec((1, 1, 2 * de)),
                          pl.BlockSpec(memory_space=pl.ANY), b_spec((1, 1, D))],
                out_specs=plane_spec(EXPERT_BLOCK, lambda i, *_: i),
                scratch_shapes=[pltpu.VMEM((2, D, 2 * de), F32), pltpu.VMEM((2, de, D), F32),
                                pltpu.VMEM((2, D, 2 * de), BF16), pltpu.VMEM((2, de, D), BF16),
                                pltpu.SemaphoreType.DMA((2,)), pltpu.SemaphoreType.DMA((2,))],
            ),
            out_shape=jax.ShapeDtypeStruct((PACK_ROWS, n_rows, LANES), U32),
            compiler_params=cparams(dimension_semantics=("arbitrary",)),
            name="experts",
        )(block_expert, n_used, block_valid, per_block(ordinal % 2), per_block(next_used), xs, w_gate_up, bgu, w_down, bdn)

        y4 = _sc_gather_rows(ys, dest.reshape(-1, SC_CHUNK))

        operands = [x1, gate, y4, y4, y4, y4] + ([] if out is None else [out])
        out = pl.pallas_call(
            _combine_kernel,
            grid=(steps,),
            in_specs=([row_spec(D), row_spec(LANES)]
                      + [plane_spec(tm, lambda i, kk=kk: kk * steps + i) for kk in range(TOP_K)]
                      + ([] if out is None else [pl.BlockSpec(memory_space=pl.ANY)])),
            out_specs=grp_spec(D),
            out_shape=jax.ShapeDtypeStruct((N, D), F32),
            input_output_aliases={} if out is None else {len(operands) - 1: 0},
            compiler_params=cparams(dimension_semantics=("arbitrary",)),
            name="combine",
        )(*operands)
    return out.reshape(B, S, D)


def kernel(x, norm1_g, w_in, q_norm_g, k_norm_g, w_pool_grp, pool_scale, w_pool_up, w_attn_up, w_out, norm2_g,
           w_router, b_router, w_gate_up, b_gate_up, w_down, b_down):
    for layer in range(norm1_g.shape[0]):
        x = _layer(x, norm1_g[layer], w_in[layer], q_norm_g[layer], k_norm_g[layer], w_pool_grp[layer],
                   pool_scale[layer], w_pool_up[layer], w_attn_up[layer], w_out[layer], norm2_g[layer],
                   w_router[layer], b_router[layer], w_gate_up[layer], b_gate_up[layer], w_down[layer],
                   b_down[layer])
    return x
```

```python
import functools

import jax
import jax.numpy as jnp
from jax import lax
from jax.experimental import pallas as pl
from jax.experimental.pallas import tpu as pltpu
from jax.experimental.pallas import tpu_sc as plsc

F32 = jnp.float32
BF16 = jnp.bfloat16
U32 = jnp.uint32
I32 = jnp.int32

EPS = 1e-6
POOL_WINDOWS = (2, 4, 8, 16)
POOL_GROUP_DIM = 128
POOL_HALO = 16
SB_HEAD_DIM = 64
TOP_K = 4
SWIGLU_LIMIT = 7.0
SWIGLU_ALPHA = 1.702
EXPERT_BLOCK = 1024
EXPERT_CAST_COLS = 256
EXPERT_SPLIT = 4
LANES = 128
PACK_ROWS = 4
ATTN_BLOCK = 128
ATTN_SUB = 2
ATTN_EXIT_BITS = 70.0
LOG2_E = 1.4426950408889634
VMEM_LIMIT = 56 * 1024 * 1024
SC_CHUNK = 128
MOE_GROUPS = 2


def _dot(a, b):
    return jnp.dot(a, b, preferred_element_type=F32)


def _split_bf16(x):
    hi = x.astype(BF16)
    lo = (x - hi.astype(F32)).astype(BF16)
    return hi, lo


def _pack_rows(v, out_ref):
    half = v.shape[1] // 2
    lo = lax.bitcast_convert_type(v[:, :half].astype(BF16).astype(F32), U32) >> 16
    hi = lax.bitcast_convert_type(v[:, half:].astype(BF16).astype(F32), U32) & jnp.uint32(0xFFFF0000)
    w = lo | hi
    for c in range(PACK_ROWS):
        out_ref[c] = w[:, c * LANES:(c + 1) * LANES]


def _unpack_rows(ref):
    los, his = [], []
    for c in range(PACK_ROWS):
        w = ref[c]
        los.append(lax.bitcast_convert_type(w << 16, F32))
        his.append(lax.bitcast_convert_type(w & jnp.uint32(0xFFFF0000), F32))
    return jnp.concatenate(los + his, axis=1)


def _mixer_in_kernel(x_ref, g1_ref, win_ref, gq_ref, gk_ref, hsum_ref, wgrp_ref, pscale_ref, wpu_ref,
                     q_ref, k_ref, v_ref, p_ref, tail_ref, *, tm, pw, sw):
    i = pl.program_id(1)

    @pl.when(i == 0)
    def _():
        tail_ref[...] = jnp.zeros_like(tail_ref)

    x = x_ref[...]
    ms = jnp.mean(x * x, axis=-1, keepdims=True)
    h = (x * lax.rsqrt(ms + EPS) * g1_ref[...]).astype(BF16)

    def head_norm(t, gain):
        ss = _dot((t * t).astype(BF16), hsum_ref[...])
        return t * lax.rsqrt(ss * (1.0 / SB_HEAD_DIM) + EPS) * gain

    def project_q():
        q = _dot(h, win_ref[:, pw:pw + sw])
        q_ref[...] = (head_norm(q, gq_ref[...]) * (SB_HEAD_DIM ** -0.5 * LOG2_E)).astype(BF16)

    def project_k():
        k = _dot(h, win_ref[:, pw + sw:pw + 2 * sw])
        k_ref[...] = head_norm(k, gk_ref[...]).astype(BF16)

    def project_v():
        v_ref[...] = _dot(h, win_ref[:, pw + 2 * sw:pw + 3 * sw]).astype(BF16)

    d_model = x.shape[1]
    pool_gate = []

    def project_pool_gate():
        pool_gate.append(jax.nn.sigmoid(_dot(h, win_ref[:, pw + 3 * sw:pw + 3 * sw + d_model])))

    u = _dot(h, win_ref[:, 0:pw])
    xx = jnp.concatenate([tail_ref[...], u], axis=0)
    tail_ref[...] = u[tm - POOL_HALO:, :]
    pos = i * tm + lax.broadcasted_iota(I32, (tm, POOL_GROUP_DIM), 0)
    mixed = []
    for (g, w), project in zip(enumerate(POOL_WINDOWS), (project_q, project_k, project_v, project_pool_gate)):
        project()
        s = xx[:, g * POOL_GROUP_DIM:(g + 1) * POOL_GROUP_DIM]
        step = 1
        while step < w:
            s = s + pltpu.roll(s, step, axis=0)
            step *= 2
        count = jnp.minimum(pos + 1, w).astype(F32)
        ug = u[:, g * POOL_GROUP_DIM:(g + 1) * POOL_GROUP_DIM]
        d = s[POOL_HALO:, :] / count - ug
        mixed.append(_dot(d.astype(BF16), wgrp_ref[g]))
    pm = jnp.concatenate(mixed, axis=1) * pscale_ref[...]
    pool_out = _dot(pm.astype(BF16), wpu_ref[...])
    p_ref[...] = (pool_gate[0] * pool_out).astype(BF16)


def _attn_kernel(q_ref, k_hbm, v_hbm, tri_ref, o_ref, kbuf, vbuf, ksem, vsem, *scratch, n_pairs):
    units = [(sub, p) for sub in range(ATTN_SUB) for p in range(n_pairs)]
    n_units = len(units)
    qs, acc, rr = scratch[:n_units], scratch[n_units:2 * n_units], scratch[2 * n_units:]
    bq = ATTN_BLOCK
    batch = pl.program_id(0)
    seq = kbuf.shape[1]

    def kv_copies(b):
        rows = pl.ds(pl.multiple_of(b * seq, seq), seq)
        return (pltpu.make_async_copy(k_hbm.at[rows], kbuf.at[b % 2], ksem.at[b % 2]),
                pltpu.make_async_copy(v_hbm.at[rows], vbuf.at[b % 2], vsem.at[b % 2]))

    @pl.when(pl.program_id(1) == 0)
    def _():
        @pl.when(batch == 0)
        def _():
            for c in kv_copies(batch):
                c.start()

        for c in kv_copies(batch):
            c.wait()

        @pl.when(batch + 1 < pl.num_programs(0))
        def _():
            for c in kv_copies(batch + 1):
                c.start()

    k_ref = kbuf.at[batch % 2]
    v_ref = vbuf.at[batch % 2]
    first_block = pl.program_id(1) * ATTN_SUB
    first_head = lax.broadcasted_iota(I32, (bq, LANES), 1) < SB_HEAD_DIM
    for u, (sub, p) in enumerate(units):
        q2 = q_ref[sub * bq:(sub + 1) * bq, p * LANES:(p + 1) * LANES]
        qs[u][:bq] = jnp.where(first_head, q2, jnp.zeros_like(q2))
        qs[u][bq:] = jnp.where(first_head, jnp.zeros_like(q2), q2)
    row = lax.broadcasted_iota(I32, (2 * bq, bq), 0)
    col = lax.broadcasted_iota(I32, (2 * bq, bq), 1)
    causal = col < (row & (bq - 1))
    contract_last = (((1,), (1,)), ((), ()))

    def softplus(z):
        return jnp.maximum(z, 0.0) + jnp.log2(1.0 + jnp.exp2(-jnp.abs(z)))

    def suffix_sums(sp):
        return _dot(sp.astype(BF16), tri_ref[...])

    def cols(ref, block, p):
        start = pl.multiple_of(jnp.maximum(block, 0) * bq, bq)
        return ref[pl.ds(start, bq), p * LANES:(p + 1) * LANES]

    def scores(u, block):
        return lax.dot_general(qs[u][...], cols(k_ref, block, units[u][1]), contract_last, preferred_element_type=F32)

    diag = [first_block + sub for sub, _ in units]
    z_d, z_n, s_d, s_n = {}, {}, {}, {}

    def stage_scores(u):
        z_d[u] = scores(u, diag[u])
        z_n[u] = scores(u, diag[u] - 1)

    def stage_sums(u):
        s_d[u] = suffix_sums(jnp.where(causal, softplus(z_d[u]), 0.0))
        s_n[u] = suffix_sums(jnp.where(diag[u] >= 1, softplus(z_n[u]), 0.0))

    def stage_values(u):
        p = units[u][1]
        a_d = jnp.where(causal, jnp.exp2(z_d[u] - s_d[u][:, :bq]), 0.0)
        r_d = s_d[u][:, bq:]
        a_n = jnp.where(diag[u] >= 1, jnp.exp2(z_n[u] - (r_d + s_n[u][:, :bq])), 0.0)
        acc[u][...] = (_dot(a_d.astype(BF16), cols(v_ref, diag[u], p)) + _dot(a_n.astype(BF16), cols(v_ref, diag[u] - 1, p)))
        rr[u][...] = r_d + s_n[u][:, bq:]

    for t in range(n_units + 2):
        if t < n_units:
            stage_scores(t)
        if 0 <= t - 1 < n_units:
            stage_sums(t - 1)
        if 0 <= t - 2 < n_units:
            stage_values(t - 2)

    def r_min():
        m = rr[0][...]
        for u in range(1, n_units):
            m = jnp.minimum(m, rr[u][...])
        return jnp.min(m)

    def cond(c):
        back, rm = c
        return jnp.logical_and(diag[-1] - back >= 0, rm < ATTN_EXIT_BITS)

    def body(c):
        back, _ = c
        blocks = [d - back for d in diag]
        zs = [scores(u, blocks[u]) for u in range(n_units)]
        ss = [suffix_sums(jnp.where(blocks[u] >= 0, softplus(zs[u]), 0.0)) for u in range(n_units)]
        for u, (_, p) in enumerate(units):
            r = rr[u][...]
            a = jnp.where(blocks[u] >= 0, jnp.exp2(zs[u] - (r + ss[u][:, :bq])), 0.0)
            acc[u][...] += _dot(a.astype(BF16), cols(v_ref, blocks[u], p))
            rr[u][...] = r + ss[u][:, bq:]
        return back + 1, r_min()

    lax.while_loop(cond, body, (2, r_min()))
    for u, (sub, p) in enumerate(units):
        o_ref[sub * bq:(sub + 1) * bq, p * LANES:(p + 1) * LANES] = (
            jnp.where(first_head, acc[u][:bq], acc[u][bq:]).astype(BF16))


def _mixer_out_kernel(x_ref, p_ref, sba_ref, g1_ref, wga_ref, wau_ref, wout_ref, g2_ref, wr_hi_ref, wr_lo_ref, br_ref,
                      ltri_ref, x1_ref, hp_ref, ri_ref, gate_ref, cnt_ref, logit_ref, *, tm):
    step = pl.program_id(0)

    @pl.when(step == 0)
    def _():
        cnt_ref[...] = jnp.zeros_like(cnt_ref)
        logit_ref[1] = jnp.zeros((tm, LANES), F32)

    logits = logit_ref[(step + 1) % 2]
    routed = jnp.where(step >= 1, 1.0, 0.0)
    lane = lax.broadcasted_iota(I32, logits.shape, 1).astype(F32)
    work = logits
    vals, idxs = [], []

    def topk_round():
        nonlocal work
        m = jnp.max(work, axis=-1, keepdims=True)
        ik = jnp.min(jnp.where(work == m, lane, float(LANES)), axis=-1, keepdims=True)
        vals.append(m)
        idxs.append(ik)
        work = jnp.where(lane == ik, -jnp.inf, work)

    d_model = x_ref.shape[1]
    quarter = d_model // TOP_K
    topk_round()
    x = x_ref[...]
    h1 = (x * lax.rsqrt(jnp.mean(x * x, axis=-1, keepdims=True) + EPS) * g1_ref[...]).astype(BF16)
    attn_gate = jax.nn.sigmoid(_dot(h1, wga_ref[...]))
    attn_out = _dot(sba_ref[...], wau_ref[...])
    merged = (p_ref[...].astype(F32) + attn_gate * attn_out).astype(BF16)
    x1_parts = []
    for c in range(TOP_K):
        if c >= 1:
            topk_round()
        cols = slice(c * quarter, (c + 1) * quarter)
        x1_parts.append(x[:, cols] + _dot(merged, wout_ref[:, cols]))
    x1 = jnp.concatenate(x1_parts, axis=1)
    x1_ref[...] = x1

    es = [jnp.exp(v - vals[0]) for v in vals]
    denom = es[0] + es[1] + es[2] + es[3]
    hot = jnp.zeros(logits.shape, F32)
    for ik in idxs:
        hot = hot + jnp.where(lane == ik, routed, 0.0)
    before = _dot(ltri_ref[...], hot.astype(BF16)) + cnt_ref[0:1, :]

    ms = jnp.mean(x1 * x1, axis=-1, keepdims=True)
    h2 = x1 * lax.rsqrt(ms + EPS) * g2_ref[...]
    _pack_rows(h2, hp_ref)

    ri = jnp.zeros(logits.shape, F32)
    gt = jnp.zeros(logits.shape, F32)
    for kk in range(TOP_K):
        rank = jnp.sum(jnp.where(lane == idxs[kk], before, 0.0), axis=-1, keepdims=True)
        ri = jnp.where(lane == kk, idxs[kk], ri)
        ri = jnp.where(lane == TOP_K + kk, rank, ri)
        gt = jnp.where(lane == kk, es[kk] / denom, gt)
    ri_ref[...] = ri.T[:2 * TOP_K].astype(I32)
    gate_ref[...] = gt
    cnt_ref[...] = cnt_ref[...] + jnp.sum(hot, axis=0, keepdims=True)

    h_hi, h_lo = _split_bf16(h2)
    logit_ref[step % 2] = (_dot(h_hi, wr_hi_ref[...]) + _dot(h_hi, wr_lo_ref[...]) + _dot(h_lo, wr_hi_ref[...])
                           + br_ref[...])


def _dest_kernel(start_ref, ri_ref, dest_ref):
    idx = ri_ref[:TOP_K, :]
    dest = ri_ref[TOP_K:, :]
    for e in range(start_ref.shape[0]):
        dest = dest + jnp.where(idx == e, start_ref[e], 0)
    dest_ref[...] = dest


def _sc_mesh():
    info = plsc.get_sparse_core_info()
    mesh = plsc.VectorSubcoreMesh(core_axis_name="c", subcore_axis_name="s")
    return mesh, info.num_cores, info.num_subcores, info.num_lanes


def _sc_scatter_rows(rows, idx, n_out):
    mesh, nc, ns, lanes = _sc_mesh()
    nw = nc * ns
    planes, m, width = rows.shape
    ch = SC_CHUNK
    n_ch = m // ch // nw
    half = planes // 2
    assert planes % 2 == 0 and n_ch * ch * nw == m and idx.shape == (m // ch * TOP_K, ch) and n_ch >= 2
    assert n_ch % 8 == 0

    @functools.partial(
        pl.kernel, mesh=mesh, out_type=jax.ShapeDtypeStruct((planes * n_out, width), rows.dtype),
        scratch_types=([pltpu.VMEM((n_ch * TOP_K, ch), I32)] + [pltpu.VMEM((ch, width), rows.dtype)] * planes
                       + [pltpu.VMEM((TOP_K, ch), I32)] * planes
                       + [pltpu.SemaphoreType.DMA((planes,)), pltpu.SemaphoreType.DMA((planes,))]))
    def scatter_kernel(rows_hbm, idx_hbm, out_hbm, idx_v, *rest):
        bufs, ibufs, rsem, ssem = rest[:planes], rest[planes:2 * planes], rest[2 * planes], rest[2 * planes + 1]
        wid = lax.axis_index("s") * nc + lax.axis_index("c")
        for kk in range(TOP_K):
            pltpu.sync_copy(idx_hbm.at[pl.ds(kk * (m // ch) + wid * n_ch, n_ch)], idx_v.at[pl.ds(kk * n_ch, n_ch)])
        base = wid * n_ch * ch

        def read(j, b):
            return pltpu.make_async_copy(rows_hbm.at[b, pl.ds(pl.multiple_of(base + j * ch, ch), ch)], bufs[b], rsem.at[b])

        def scatters(b):
            return [pltpu.make_async_copy(bufs[b], out_hbm.at[ibufs[b].at[kk]], ssem.at[b]) for kk in range(TOP_K)]

        def start_scatters(j, b):
            for kk in range(TOP_K):
                for t in range(0, ch, lanes):
                    ibufs[b][kk, pl.ds(t, lanes)] = idx_v[kk * n_ch + j, pl.ds(t, lanes)] + b * n_out
            for c in scatters(b):
                c.start()

        def wait_scatters(b):
            for c in scatters(b):
                c.wait()

        def finish(j, b):
            pj, pb = (j, b - half) if b >= half else (j - 1, b + half)
            read(pj, pb).wait()
            start_scatters(pj, pb)

        for b in range(planes):
            read(0, b).start()
        for b in range(half, planes):
            finish(0, b)

        @pl.loop(1, n_ch)
        def _(j):
            for b in range(planes):
                wait_scatters(b)
                read(j, b).start()
                finish(j, b)

        for b in range(half):
            finish(n_ch, b)
        for b in range(planes):
            wait_scatters(b)

    return scatter_kernel(rows, idx).reshape(planes, n_out, width)


def _padfill_kernel(start_ref, len_ref, xs_in_ref, xs_ref, zeros_ref, sem):
    del xs_in_ref
    zeros_ref[...] = jnp.zeros_like(zeros_ref)
    bits = [1 << b for b in reversed(range((EXPERT_BLOCK - 1).bit_length()))]

    def pieces(e):
        n = len_ref[e]
        for bit in bits:
            row0 = start_ref[e] + (n & ~(2 * bit - 1))
            copy = pltpu.make_async_copy(zeros_ref.at[:, pl.ds(0, bit)], xs_ref.at[:, pl.ds(row0, bit)], sem)
            yield (n & bit) != 0, copy

    def start(e, c):
        for on, copy in pieces(e):
            pl.when(on)(copy.start)
        return c

    def wait(e, c):
        for on, copy in pieces(e):
            pl.when(on)(copy.wait)
        return c

    lax.fori_loop(0, start_ref.shape[0], start, 0)
    lax.fori_loop(0, start_ref.shape[0], wait, 0)


def _experts_kernel(be_ref, nb_ref, valid_ref, slot_ref, next_ref, xs_ref, wgu_hbm, bgu_ref, wd_hbm, bd_ref, ys_ref,
                    wgu_buf, wd_buf, wgu_bf_ref, wd_bf_ref, gu_sem, d_sem):
    del nb_ref
    blk = pl.program_id(0)
    de = wd_hbm.shape[1]
    valid = valid_ref[blk]
    expert = be_ref[blk]
    slot = slot_ref[blk]
    following = next_ref[blk]
    prev1 = be_ref[jnp.maximum(blk - 1, 0)]
    prev2 = be_ref[jnp.maximum(blk - 2, 0)]

    def weight_copies(e, s):
        return (pltpu.make_async_copy(wgu_hbm.at[e], wgu_buf.at[s], gu_sem.at[s]),
                pltpu.make_async_copy(wd_hbm.at[e], wd_buf.at[s], d_sem.at[s]))

    def cast_rows(s, piece, pieces):
        for buf, bf_ref in ((wgu_buf, wgu_bf_ref), (wd_buf, wd_bf_ref)):
            rows = buf.shape[1] // pieces
            bf_ref[s, pl.ds(piece * rows, rows), :] = buf[s, pl.ds(piece * rows, rows), :].astype(BF16)

    first_of_expert = jnp.logical_or(blk == 0, expert != prev1)
    after = jnp.minimum(blk + 1, pl.num_programs(0) - 1)
    last_full = jnp.logical_and(valid == EXPERT_BLOCK,
                                jnp.logical_or(be_ref[after] != expert, valid_ref[after] < EXPERT_BLOCK))
    cast_in_mlp = jnp.logical_and(last_full, following >= 0)
    cast_before = jnp.logical_or(jnp.logical_and(blk >= 1, valid_ref[jnp.maximum(blk - 1, 0)] == EXPERT_BLOCK),
                                 jnp.logical_and(blk >= 2, prev1 == prev2))

    @pl.when(jnp.logical_and(valid > 0, first_of_expert))
    def _():
        @pl.when(blk == 0)
        def _():
            for c in weight_copies(expert, slot):
                c.start()

        @pl.when(jnp.logical_not(cast_before))
        def _():
            for c in weight_copies(expert, slot):
                c.wait()
            cast_rows(slot, 0, 1)

        @pl.when(following >= 0)
        def _():
            for c in weight_copies(following, 1 - slot):
                c.start()

    def activation(gu_gate, gu_lin):
        glu = jnp.minimum(gu_gate, SWIGLU_LIMIT)
        lin = jnp.clip(gu_lin, -SWIGLU_LIMIT, SWIGLU_LIMIT)
        return (glu * jax.nn.sigmoid(SWIGLU_ALPHA * glu) * (lin + 1.0)).astype(BF16)

    def mlp(m):
        x = _unpack_rows(xs_ref.at[:, pl.ds(0, m)]).astype(BF16)
        gu = _dot(x, wgu_bf_ref[slot]) + bgu_ref[0]
        y = _dot(activation(gu[:, :de], gu[:, de:]), wd_bf_ref[slot]) + bd_ref[0]
        _pack_rows(y, ys_ref.at[:, pl.ds(0, m)])
        if m < EXPERT_BLOCK:
            ys_ref[:, m:, :] = jnp.zeros((PACK_ROWS, EXPERT_BLOCK - m, LANES), U32)

    def mlp_and_cast():
        for c in weight_copies(following, 1 - slot):
            c.wait()
        x = _unpack_rows(xs_ref).astype(BF16)
        bgu = bgu_ref[0]
        pieces = 2 * de // EXPERT_CAST_COLS
        gu = []
        for j in range(pieces):
            cols = slice(j * EXPERT_CAST_COLS, (j + 1) * EXPERT_CAST_COLS)
            gu.append(_dot(x, wgu_bf_ref[slot, :, cols]) + bgu[:, cols])
            cast_rows(1 - slot, j, pieces)
        act = jnp.concatenate([activation(g, l) for g, l in zip(gu[:pieces // 2], gu[pieces // 2:])], axis=1)
        y = _dot(act, wd_bf_ref[slot]) + bd_ref[0]
        _pack_rows(y, ys_ref)

    pl.when(cast_in_mlp)(mlp_and_cast)
    step = EXPERT_BLOCK // EXPERT_SPLIT
    for q in range(EXPERT_SPLIT):
        here = jnp.logical_and(valid > q * step, valid <= (q + 1) * step)
        if q == EXPERT_SPLIT - 1:
            here = jnp.logical_and(here, jnp.logical_not(cast_in_mlp))
        pl.when(here)(functools.partial(mlp, (q + 1) * step))

    @pl.when(valid == 0)
    def _():
        ys_ref[...] = jnp.zeros_like(ys_ref)


def _sc_gather_rows(table, idx):
    mesh, nc, ns, lanes = _sc_mesh()
    nw = nc * ns
    planes, n_tab, width = table.shape
    n_idx_rows, ch = idx.shape
    n_ch = n_idx_rows // nw
    m = n_idx_rows * ch
    half = planes // 2
    assert planes % 2 == 0 and ch == SC_CHUNK and n_ch * nw == n_idx_rows and n_ch >= 2

    @functools.partial(
        pl.kernel, mesh=mesh, out_type=jax.ShapeDtypeStruct((planes, m, width), table.dtype),
        scratch_types=([pltpu.VMEM((n_ch, ch), I32)] + [pltpu.VMEM((ch, width), table.dtype)] * planes
                       + [pltpu.VMEM((8, ch), I32)] * planes
                       + [pltpu.SemaphoreType.DMA((planes,)), pltpu.SemaphoreType.DMA((planes,))]))
    def gather_kernel(table_hbm, idx_hbm, out_hbm, idx_v, *rest):
        bufs, ibufs, gsem, wsem = rest[:planes], rest[planes:2 * planes], rest[2 * planes], rest[2 * planes + 1]
        wid = lax.axis_index("s") * nc + lax.axis_index("c")
        pltpu.sync_copy(idx_hbm.at[pl.ds(wid * n_ch, n_ch)], idx_v)
        base = wid * n_ch * ch

        def gather(b):
            return pltpu.make_async_copy(table_hbm.at[ibufs[b].at[0]], bufs[b], gsem.at[b])

        def start_gather(j, b):
            for t in range(0, ch, lanes):
                ibufs[b][0, pl.ds(t, lanes)] = idx_v[j, pl.ds(t, lanes)] + b * n_tab
            gather(b).start()

        def write(j, b):
            return pltpu.make_async_copy(bufs[b], out_hbm.at[b, pl.ds(pl.multiple_of(base + j * ch, ch), ch)], wsem.at[b])

        def finish(j, b):
            pj, pb = (j, b - half) if b >= half else (j - 1, b + half)
            gather(pb).wait()
            write(pj, pb).start()

        for b in range(planes):
            start_gather(0, b)
        for b in range(half, planes):
            finish(0, b)

        @pl.loop(1, n_ch)
        def _(j):
            for b in range(planes):
                write(j - 1, b).wait()
                start_gather(j, b)
                finish(j, b)

        for b in range(half):
            finish(n_ch, b)
        for b in range(planes):
            write(n_ch - 1, b).wait()

    return gather_kernel(table.reshape(planes * n_tab, width), idx)


def _combine_kernel(x1_ref, gate_ref, y0_ref, y1_ref, y2_ref, y3_ref, *rest):
    o_ref = rest[-1]
    gate = gate_ref[...]
    out = x1_ref[...]
    for kk, y_ref in enumerate((y0_ref, y1_ref, y2_ref, y3_ref)):
        out = out + gate[:, kk:kk + 1] * _unpack_rows(y_ref)
    o_ref[...] = out


def _const_spec(shape):
    nd = len(shape)
    return pl.BlockSpec(shape, lambda *_: (0,) * nd)


def _layer(x, norm1_g, w_in, q_norm_g, k_norm_g, w_pool_grp, pool_scale, w_pool_up, w_attn_up, w_out, norm2_g,
           w_router, b_router, w_gate_up, b_gate_up, w_down, b_down):
    B, S, D = x.shape
    N = B * S
    pw = w_pool_up.shape[0]
    sw = w_attn_up.shape[0]
    n_exp = w_router.shape[1]
    de = w_down.shape[1]
    heads = sw // SB_HEAD_DIM
    assert pw == len(POOL_WINDOWS) * POOL_GROUP_DIM and heads % 2 == 0 and n_exp <= LANES
    assert D == 2 * PACK_ROWS * LANES and w_in.shape[1] == pw + 3 * sw + 2 * D
    tm = 512 if S % 512 == 0 else 256
    assert S % tm == 0 and S % (ATTN_SUB * ATTN_BLOCK) == 0
    xf = x.reshape(N, D)
    cparams = functools.partial(pltpu.CompilerParams, vmem_limit_bytes=VMEM_LIMIT)

    hsum = (jnp.arange(sw)[:, None] // SB_HEAD_DIM == jnp.arange(sw)[None, :] // SB_HEAD_DIM).astype(BF16)
    tm_in = 2 * tm if S % (2 * tm) == 0 else tm
    nt = S // tm_in
    tok_spec = lambda w: pl.BlockSpec((tm_in, w), lambda b, i: (b * nt + i, 0))
    w_in_bf = w_in.astype(BF16)
    n_in = pw + 3 * sw + D
    g1 = norm1_g.reshape(1, D)
    q2, k2, v2, pg = pl.pallas_call(
        functools.partial(_mixer_in_kernel, tm=tm_in, pw=pw, sw=sw),
        grid=(B, nt),
        in_specs=[tok_spec(D), _const_spec((1, D)), _const_spec((D, n_in)), _const_spec((1, sw)),
                  _const_spec((1, sw)), _const_spec((sw, sw)),
                  _const_spec((len(POOL_WINDOWS), POOL_GROUP_DIM, POOL_GROUP_DIM)), _const_spec((1, pw)),
                  _const_spec((pw, D))],
        out_specs=[tok_spec(sw), tok_spec(sw), tok_spec(sw), tok_spec(D)],
        out_shape=[jax.ShapeDtypeStruct((N, sw), BF16)] * 3 + [jax.ShapeDtypeStruct((N, D), BF16)],
        scratch_shapes=[pltpu.VMEM((POOL_HALO, pw), F32)],
        compiler_params=cparams(dimension_semantics=("arbitrary", "arbitrary")),
        name="mixer_in",
    )(xf, g1, w_in_bf[:, :n_in], jnp.tile(q_norm_g, heads).reshape(1, sw),
      jnp.tile(k_norm_g, heads).reshape(1, sw), hsum, w_pool_grp.astype(BF16), pool_scale.reshape(1, pw),
      w_pool_up.astype(BF16))

    bq = ATTN_BLOCK
    rows_q = ATTN_SUB * bq
    nq = S // rows_q
    jj = jnp.arange(bq)
    tri = jnp.concatenate([(jj[:, None] >= jj[None, :]).astype(BF16), jnp.ones((bq, bq), BF16)], axis=1)
    n_pairs = heads // 2
    n_units = ATTN_SUB * n_pairs
    kv_spec = pl.BlockSpec(memory_space=pl.ANY)
    sba = pl.pallas_call(
        functools.partial(_attn_kernel, n_pairs=n_pairs),
        grid=(B, nq),
        in_specs=[pl.BlockSpec((rows_q, sw), lambda b, qi: (b * nq + qi, 0)), kv_spec, kv_spec,
                  _const_spec((bq, 2 * bq))],
        out_specs=pl.BlockSpec((rows_q, sw), lambda b, qi: (b * nq + qi, 0)),
        out_shape=jax.ShapeDtypeStruct((N, sw), BF16),
        scratch_shapes=([pltpu.VMEM((2, S, sw), BF16), pltpu.VMEM((2, S, sw), BF16),
                         pltpu.SemaphoreType.DMA((2,)), pltpu.SemaphoreType.DMA((2,))]
                        + [pltpu.VMEM((2 * bq, LANES), BF16)] * n_units + [pltpu.VMEM((2 * bq, LANES), F32)] * (2 * n_units)),
        compiler_params=cparams(dimension_semantics=("arbitrary", "arbitrary")),
        name="sb_attn",
    )(q2, k2, v2, tri)

    wr = jnp.zeros((D, LANES), F32).at[:, :n_exp].set(w_router)
    wr_hi = wr.astype(BF16)
    wr_lo = (wr - wr_hi.astype(F32)).astype(BF16)
    br = jnp.full((1, LANES), -jnp.inf, F32).at[0, :n_exp].set(b_router)
    ltri = (jnp.arange(tm)[:, None] > jnp.arange(tm)[None, :]).astype(BF16)
    assert N % (MOE_GROUPS * tm) == 0
    ng = N // MOE_GROUPS
    steps = ng // tm
    n_assign = ng * TOP_K
    n_blocks = -(-(n_assign + n_exp * (EXPERT_BLOCK - 1)) // EXPERT_BLOCK)
    n_rows = n_blocks * EXPERT_BLOCK
    plane_spec = lambda rows, index: pl.BlockSpec((PACK_ROWS, rows, LANES), lambda i, *_: (0, index(i, *_), 0))
    row_spec = lambda w: pl.BlockSpec((tm, w), lambda i: (i, 0))
    w_attn_up_bf, w_out_bf, g2 = w_attn_up.astype(BF16), w_out.astype(BF16), norm2_g.reshape(1, D)
    bgu, bdn = b_gate_up.reshape(n_exp, 1, 2 * de), b_down.reshape(n_exp, 1, D)
    out = None
    for grp in range(MOE_GROUPS):
        grp_spec = lambda w, first=grp * steps: pl.BlockSpec((tm, w), lambda i: (first + i, 0))

        cur = lambda i: jnp.minimum(i, steps - 1)
        prev = lambda i: jnp.maximum(i - 1, 0)
        in_spec = lambda w, first=grp * steps: pl.BlockSpec((tm, w), lambda i: (first + cur(i), 0))
        x1, hpk, ri, gate, cnt = pl.pallas_call(
            functools.partial(_mixer_out_kernel, tm=tm),
            grid=(steps + 1,),
            in_specs=[in_spec(D), in_spec(D), in_spec(sw), _const_spec((1, D)), _const_spec((D, D)),
                      _const_spec((sw, D)), _const_spec((D, D)),
                      _const_spec((1, D)), _const_spec((D, LANES)), _const_spec((D, LANES)), _const_spec((1, LANES)),
                      _const_spec((tm, tm))],
            out_specs=[pl.BlockSpec((tm, D), lambda i: (cur(i), 0)), plane_spec(tm, cur),
                       pl.BlockSpec((2 * TOP_K, tm), lambda i: (0, prev(i))),
                       pl.BlockSpec((tm, LANES), lambda i: (prev(i), 0)), _const_spec((8, LANES))],
            out_shape=[jax.ShapeDtypeStruct((ng, D), F32), jax.ShapeDtypeStruct((PACK_ROWS, ng, LANES), U32),
                       jax.ShapeDtypeStruct((2 * TOP_K, ng), I32), jax.ShapeDtypeStruct((ng, LANES), F32),
                       jax.ShapeDtypeStruct((8, LANES), F32)],
            scratch_shapes=[pltpu.VMEM((2, tm, LANES), F32)],
            compiler_params=cparams(dimension_semantics=("arbitrary",)),
            name="mixer_out",
        )(xf, pg, sba, g1, w_in_bf[:, n_in:], w_attn_up_bf, w_out_bf, g2, wr_hi, wr_lo, br, ltri)

        counts = cnt[0, :n_exp].astype(I32)
        padded = (counts + EXPERT_BLOCK - 1) // EXPERT_BLOCK * EXPERT_BLOCK
        padded_end = jnp.cumsum(padded)
        start_pad = padded_end - padded
        td = min(ng, 8192)
        dest = pl.pallas_call(
            _dest_kernel,
            grid_spec=pltpu.PrefetchScalarGridSpec(
                num_scalar_prefetch=1,
                grid=(ng // td,),
                in_specs=[pl.BlockSpec((2 * TOP_K, td), lambda i, sp: (0, i))],
                out_specs=pl.BlockSpec((TOP_K, td), lambda i, sp: (0, i)),
            ),
            out_shape=jax.ShapeDtypeStruct((TOP_K, ng), I32),
            compiler_params=cparams(dimension_semantics=("arbitrary",)),
            name="dest",
        )(start_pad, ri).reshape(-1, SC_CHUNK)
        block_start = jnp.arange(n_blocks, dtype=I32) * EXPERT_BLOCK
        block_expert = jnp.minimum(jnp.sum((padded_end[None, :] <= block_start[:, None]).astype(I32), axis=1),
                                   n_exp - 1)
        n_used = (padded_end[-1] // EXPERT_BLOCK).astype(I32).reshape(1)

        xs = _sc_scatter_rows(hpk, dest, n_rows)
        xs = pl.pallas_call(
            _padfill_kernel,
            grid_spec=pltpu.PrefetchScalarGridSpec(
                num_scalar_prefetch=2,
                grid=(1,),
                in_specs=[pl.BlockSpec(memory_space=pl.ANY)],
                out_specs=pl.BlockSpec(memory_space=pl.ANY),
                scratch_shapes=[pltpu.VMEM((PACK_ROWS, 1 << ((EXPERT_BLOCK - 1).bit_length() - 1), LANES), U32),
                                pltpu.SemaphoreType.DMA(())],
            ),
            out_shape=jax.ShapeDtypeStruct((PACK_ROWS, n_rows, LANES), U32),
            input_output_aliases={2: 0},
            compiler_params=cparams(dimension_semantics=("arbitrary",)),
            name="padfill",
        )(start_pad + counts, padded - counts, xs)

        experts = jnp.arange(n_exp, dtype=I32)
        used = counts > 0
        block_valid = jnp.clip(jnp.sum(jnp.where(block_expert[:, None] == experts[None, :],
                                                 (start_pad + counts)[None, :], 0), axis=1) - block_start,
                               0, EXPERT_BLOCK)
        ordinal = jnp.cumsum(used.astype(I32)) - 1
        next_used = jnp.min(jnp.where(jnp.logical_and(experts[None, :] > experts[:, None], used[None, :]),
                                      experts[None, :], n_exp), axis=1)
        next_used = jnp.where(next_used == n_exp, -1, next_used)
        per_block = lambda table: jnp.sum(jnp.where(block_expert[:, None] == experts[None, :], table[None, :], 0), axis=1)
        b_spec = lambda shape: pl.BlockSpec(shape, lambda i, be, *_: (be[i], 0, 0))
        ys = pl.pallas_call(
            _experts_kernel,
            grid_spec=pltpu.PrefetchScalarGridSpec(
                num_scalar_prefetch=5,
                grid=(n_blocks,),
                in_specs=[plane_spec(EXPERT_BLOCK, lambda i, be, nb, *_: jnp.minimum(i, nb[0] - 1)),
                          pl.BlockSpec(memory_space=pl.ANY), b_spec((1, 1, 2 * de)),
                          pl.BlockSpec(memory_space=pl.ANY), b_spec((1, 1, D))],
                out_specs=plane_spec(EXPERT_BLOCK, lambda i, *_: i),
                scratch_shapes=[pltpu.VMEM((2, D, 2 * de), F32), pltpu.VMEM((2, de, D), F32),
                                pltpu.VMEM((2, D, 2 * de), BF16), pltpu.VMEM((2, de, D), BF16),
                                pltpu.SemaphoreType.DMA((2,)), pltpu.SemaphoreType.DMA((2,))],
            ),
            out_shape=jax.ShapeDtypeStruct((PACK_ROWS, n_rows, LANES), U32),
            compiler_params=cparams(dimension_semantics=("arbitrary",)),
            name="experts",
        )(block_expert, n_used, block_valid, per_block(ordinal % 2), per_block(next_used), xs, w_gate_up, bgu, w_down, bdn)

        y4 = _sc_gather_rows(ys, dest.reshape(-1, SC_CHUNK))

        operands = [x1, gate, y4, y4, y4, y4] + ([] if out is None else [out])
        out = pl.pallas_call(
            _combine_kernel,
            grid=(steps,),
            in_specs=([row_spec(D), row_spec(LANES)]
                      + [plane_spec(tm, lambda i, kk=kk: kk * steps + i) for kk in range(TOP_K)]
                      + ([] if out is None else [pl.BlockSpec(memory_space=pl.ANY)])),
            out_specs=grp_spec(D),
            out_shape=jax.ShapeDtypeStruct((N, D), F32),
            input_output_aliases={} if out is None else {len(operands) - 1: 0},
            compiler_params=cparams(dimension_semantics=("arbitrary",)),
            name="combine",
        )(*operands)
    return out.reshape(B, S, D)


def kernel(x, norm1_g, w_in, q_norm_g, k_norm_g, w_pool_grp, pool_scale, w_pool_up, w_attn_up, w_out, norm2_g,
           w_router, b_router, w_gate_up, b_gate_up, w_down, b_down):
    for layer in range(norm1_g.shape[0]):
        x = _layer(x, norm1_g[layer], w_in[layer], q_norm_g[layer], k_norm_g[layer], w_pool_grp[layer],
                   pool_scale[layer], w_pool_up[layer], w_attn_up[layer], w_out[layer], norm2_g[layer],
                   w_router[layer], b_router[layer], w_gate_up[layer], b_gate_up[layer], w_down[layer],
                   b_down[layer])
    return x
```

```python
import functools

import jax
import jax.numpy as jnp
from jax import lax
from jax.experimental import pallas as pl
from jax.experimental.pallas import tpu as pltpu
from jax.experimental.pallas import tpu_sc as plsc

F32 = jnp.float32
BF16 = jnp.bfloat16
U32 = jnp.uint32
I32 = jnp.int32

EPS = 1e-6
POOL_WINDOWS = (2, 4, 8, 16)
POOL_GROUP_DIM = 128
POOL_HALO = 16
SB_HEAD_DIM = 64
TOP_K = 4
SWIGLU_LIMIT = 7.0
SWIGLU_ALPHA = 1.702
EXPERT_BLOCK = 1024
EXPERT_SPLIT = 8
LANES = 128
PACK_ROWS = 4
ATTN_BLOCK = 128
ATTN_SUB = 2
ATTN_EXIT_BITS = 70.0
LOG2_E = 1.4426950408889634
VMEM_LIMIT = 56 * 1024 * 1024
SC_CHUNK = 128
MOE_GROUPS = 2


def _dot(a, b):
    return jnp.dot(a, b, preferred_element_type=F32)


def _split_bf16(x):
    hi = x.astype(BF16)
    lo = (x - hi.astype(F32)).astype(BF16)
    return hi, lo


def _pack_rows(v, out_ref):
    half = v.shape[1] // 2
    lo = lax.bitcast_convert_type(v[:, :half].astype(BF16).astype(F32), U32) >> 16
    hi = lax.bitcast_convert_type(v[:, half:].astype(BF16).astype(F32), U32) & jnp.uint32(0xFFFF0000)
    w = lo | hi
    for c in range(PACK_ROWS):
        out_ref[c] = w[:, c * LANES:(c + 1) * LANES]


def _unpack_rows(ref):
    los, his = [], []
    for c in range(PACK_ROWS):
        w = ref[c]
        los.append(lax.bitcast_convert_type(w << 16, F32))
        his.append(lax.bitcast_convert_type(w & jnp.uint32(0xFFFF0000), F32))
    return jnp.concatenate(los + his, axis=1)


def _mixer_in_kernel(x_ref, g1_ref, win_ref, gq_ref, gk_ref, hsum_ref, wgrp_ref, pscale_ref, wpu_ref,
                     q_ref, k_ref, v_ref, p_ref, tail_ref, *, tm, pw, sw):
    i = pl.program_id(1)

    @pl.when(i == 0)
    def _():
        tail_ref[...] = jnp.zeros_like(tail_ref)

    x = x_ref[...]
    ms = jnp.mean(x * x, axis=-1, keepdims=True)
    h = (x * lax.rsqrt(ms + EPS) * g1_ref[...]).astype(BF16)

    def head_norm(t, gain):
        ss = _dot((t * t).astype(BF16), hsum_ref[...])
        return t * lax.rsqrt(ss * (1.0 / SB_HEAD_DIM) + EPS) * gain

    def project_q():
        q = _dot(h, win_ref[:, pw:pw + sw])
        q_ref[...] = (head_norm(q, gq_ref[...]) * (SB_HEAD_DIM ** -0.5 * LOG2_E)).astype(BF16)

    def project_k():
        k = _dot(h, win_ref[:, pw + sw:pw + 2 * sw])
        k_ref[...] = head_norm(k, gk_ref[...]).astype(BF16)

    def project_v():
        v_ref[...] = _dot(h, win_ref[:, pw + 2 * sw:pw + 3 * sw]).astype(BF16)

    d_model = x.shape[1]
    pool_gate = []

    def project_pool_gate():
        pool_gate.append(jax.nn.sigmoid(_dot(h, win_ref[:, pw + 3 * sw:pw + 3 * sw + d_model])))

    u = _dot(h, win_ref[:, 0:pw])
    xx = jnp.concatenate([tail_ref[...], u], axis=0)
    tail_ref[...] = u[tm - POOL_HALO:, :]
    pos = i * tm + lax.broadcasted_iota(I32, (tm, POOL_GROUP_DIM), 0)
    mixed = []
    for (g, w), project in zip(enumerate(POOL_WINDOWS), (project_q, project_k, project_v, project_pool_gate)):
        project()
        s = xx[:, g * POOL_GROUP_DIM:(g + 1) * POOL_GROUP_DIM]
        step = 1
        while step < w:
            s = s + pltpu.roll(s, step, axis=0)
            step *= 2
        count = jnp.minimum(pos + 1, w).astype(F32)
        ug = u[:, g * POOL_GROUP_DIM:(g + 1) * POOL_GROUP_DIM]
        d = s[POOL_HALO:, :] / count - ug
        mixed.append(_dot(d.astype(BF16), wgrp_ref[g]))
    pm = jnp.concatenate(mixed, axis=1) * pscale_ref[...]
    pool_out = _dot(pm.astype(BF16), wpu_ref[...])
    p_ref[...] = (pool_gate[0] * pool_out).astype(BF16)


def _attn_kernel(q_ref, k_hbm, v_hbm, tri_ref, o_ref, kbuf, vbuf, ksem, vsem, *scratch, n_pairs):
    units = [(sub, p) for sub in range(ATTN_SUB) for p in range(n_pairs)]
    n_units = len(units)
    qs, acc, rr = scratch[:n_units], scratch[n_units:2 * n_units], scratch[2 * n_units:]
    bq = ATTN_BLOCK
    batch = pl.program_id(0)
    seq = kbuf.shape[1]

    def kv_copies(b):
        rows = pl.ds(pl.multiple_of(b * seq, seq), seq)
        return (pltpu.make_async_copy(k_hbm.at[rows], kbuf.at[b % 2], ksem.at[b % 2]),
                pltpu.make_async_copy(v_hbm.at[rows], vbuf.at[b % 2], vsem.at[b % 2]))

    @pl.when(pl.program_id(1) == 0)
    def _():
        @pl.when(batch == 0)
        def _():
            for c in kv_copies(batch):
                c.start()

        for c in kv_copies(batch):
            c.wait()

        @pl.when(batch + 1 < pl.num_programs(0))
        def _():
            for c in kv_copies(batch + 1):
                c.start()

    k_ref = kbuf.at[batch % 2]
    v_ref = vbuf.at[batch % 2]
    first_block = pl.program_id(1) * ATTN_SUB
    first_head = lax.broadcasted_iota(I32, (bq, LANES), 1) < SB_HEAD_DIM
    for u, (sub, p) in enumerate(units):
        q2 = q_ref[sub * bq:(sub + 1) * bq, p * LANES:(p + 1) * LANES]
        qs[u][:bq] = jnp.where(first_head, q2, jnp.zeros_like(q2))
        qs[u][bq:] = jnp.where(first_head, jnp.zeros_like(q2), q2)
    row = lax.broadcasted_iota(I32, (2 * bq, bq), 0)
    col = lax.broadcasted_iota(I32, (2 * bq, bq), 1)
    causal = col < (row & (bq - 1))
    contract_last = (((1,), (1,)), ((), ()))

    def softplus(z):
        return jnp.maximum(z, 0.0) + jnp.log2(1.0 + jnp.exp2(-jnp.abs(z)))

    def suffix_sums(sp):
        return _dot(sp.astype(BF16), tri_ref[...])

    def cols(ref, block, p):
        start = pl.multiple_of(jnp.maximum(block, 0) * bq, bq)
        return ref[pl.ds(start, bq), p * LANES:(p + 1) * LANES]

    def scores(u, block):
        return lax.dot_general(qs[u][...], cols(k_ref, block, units[u][1]), contract_last, preferred_element_type=F32)

    diag = [first_block + sub for sub, _ in units]
    z_d, z_n, s_d, s_n = {}, {}, {}, {}

    def stage_scores(u):
        z_d[u] = scores(u, diag[u])
        z_n[u] = scores(u, diag[u] - 1)

    def stage_sums(u):
        s_d[u] = suffix_sums(jnp.where(causal, softplus(z_d[u]), 0.0))
        s_n[u] = suffix_sums(jnp.where(diag[u] >= 1, softplus(z_n[u]), 0.0))

    def stage_values(u):
        p = units[u][1]
        a_d = jnp.where(causal, jnp.exp2(z_d[u] - s_d[u][:, :bq]), 0.0)
        r_d = s_d[u][:, bq:]
        a_n = jnp.where(diag[u] >= 1, jnp.exp2(z_n[u] - (r_d + s_n[u][:, :bq])), 0.0)
        acc[u][...] = (_dot(a_d.astype(BF16), cols(v_ref, diag[u], p)) + _dot(a_n.astype(BF16), cols(v_ref, diag[u] - 1, p)))
        rr[u][...] = r_d + s_n[u][:, bq:]

    for t in range(n_units + 2):
        if t < n_units:
            stage_scores(t)
        if 0 <= t - 1 < n_units:
            stage_sums(t - 1)
        if 0 <= t - 2 < n_units:
            stage_values(t - 2)

    def r_min():
        m = rr[0][...]
        for u in range(1, n_units):
            m = jnp.minimum(m, rr[u][...])
        return jnp.min(m)

    def cond(c):
        back, rm = c
        return jnp.logical_and(diag[-1] - back >= 0, rm < ATTN_EXIT_BITS)

    def body(c):
        back, _ = c
        blocks = [d - back for d in diag]
        zs = [scores(u, blocks[u]) for u in range(n_units)]
        ss = [suffix_sums(jnp.where(blocks[u] >= 0, softplus(zs[u]), 0.0)) for u in range(n_units)]
        for u, (_, p) in enumerate(units):
            r = rr[u][...]
            a = jnp.where(blocks[u] >= 0, jnp.exp2(zs[u] - (r + ss[u][:, :bq])), 0.0)
            acc[u][...] += _dot(a.astype(BF16), cols(v_ref, blocks[u], p))
            rr[u][...] = r + ss[u][:, bq:]
        return back + 1, r_min()

    lax.while_loop(cond, body, (2, r_min()))
    for u, (sub, p) in enumerate(units):
        o_ref[sub * bq:(sub + 1) * bq, p * LANES:(p + 1) * LANES] = (
            jnp.where(first_head, acc[u][:bq], acc[u][bq:]).astype(BF16))


def _mixer_out_kernel(x_ref, p_ref, sba_ref, g1_ref, wga_ref, wau_ref, wout_ref, g2_ref, wr_hi_ref, wr_lo_ref, br_ref,
                      ltri_ref, x1_ref, hp_ref, ri_ref, gate_ref, cnt_ref, logit_ref, *, tm):
    step = pl.program_id(0)

    @pl.when(step == 0)
    def _():
        cnt_ref[...] = jnp.zeros_like(cnt_ref)
        logit_ref[1] = jnp.zeros((tm, LANES), F32)

    logits = logit_ref[(step + 1) % 2]
    routed = jnp.where(step >= 1, 1.0, 0.0)
    lane = lax.broadcasted_iota(I32, logits.shape, 1).astype(F32)
    work = logits
    vals, idxs = [], []

    def topk_round():
        nonlocal work
        m = jnp.max(work, axis=-1, keepdims=True)
        ik = jnp.min(jnp.where(work == m, lane, float(LANES)), axis=-1, keepdims=True)
        vals.append(m)
        idxs.append(ik)
        work = jnp.where(lane == ik, -jnp.inf, work)

    d_model = x_ref.shape[1]
    quarter = d_model // TOP_K
    topk_round()
    x = x_ref[...]
    h1 = (x * lax.rsqrt(jnp.mean(x * x, axis=-1, keepdims=True) + EPS) * g1_ref[...]).astype(BF16)
    attn_gate = jax.nn.sigmoid(_dot(h1, wga_ref[...]))
    attn_out = _dot(sba_ref[...], wau_ref[...])
    merged = (p_ref[...].astype(F32) + attn_gate * attn_out).astype(BF16)
    x1_parts = []
    for c in range(TOP_K):
        if c >= 1:
            topk_round()
        cols = slice(c * quarter, (c + 1) * quarter)
        x1_parts.append(x[:, cols] + _dot(merged, wout_ref[:, cols]))
    x1 = jnp.concatenate(x1_parts, axis=1)
    x1_ref[...] = x1

    es = [jnp.exp(v - vals[0]) for v in vals]
    denom = es[0] + es[1] + es[2] + es[3]
    hot = jnp.zeros(logits.shape, F32)
    for ik in idxs:
        hot = hot + jnp.where(lane == ik, routed, 0.0)
    before = _dot(ltri_ref[...], hot.astype(BF16)) + cnt_ref[0:1, :]

    ms = jnp.mean(x1 * x1, axis=-1, keepdims=True)
    h2 = x1 * lax.rsqrt(ms + EPS) * g2_ref[...]
    _pack_rows(h2, hp_ref)

    ri = jnp.zeros(logits.shape, F32)
    gt = jnp.zeros(logits.shape, F32)
    for kk in range(TOP_K):
        rank = jnp.sum(jnp.where(lane == idxs[kk], before, 0.0), axis=-1, keepdims=True)
        ri = jnp.where(lane == kk, idxs[kk], ri)
        ri = jnp.where(lane == TOP_K + kk, rank, ri)
        gt = jnp.where(lane == kk, es[kk] / denom, gt)
    ri_ref[...] = ri.T[:2 * TOP_K].astype(I32)
    gate_ref[...] = gt
    cnt_ref[...] = cnt_ref[...] + jnp.sum(hot, axis=0, keepdims=True)

    h_hi, h_lo = _split_bf16(h2)
    logit_ref[step % 2] = (_dot(h_hi, wr_hi_ref[...]) + _dot(h_hi, wr_lo_ref[...]) + _dot(h_lo, wr_hi_ref[...])
                           + br_ref[...])


def _dest_kernel(start_ref, ri_ref, dest_ref):
    idx = ri_ref[:TOP_K, :]
    dest = ri_ref[TOP_K:, :]
    for e in range(start_ref.shape[0]):
        dest = dest + jnp.where(idx == e, start_ref[e], 0)
    dest_ref[...] = dest


def _sc_mesh():
    info = plsc.get_sparse_core_info()
    mesh = plsc.VectorSubcoreMesh(core_axis_name="c", subcore_axis_name="s")
    return mesh, info.num_cores, info.num_subcores, info.num_lanes


def _sc_scatter_rows(rows, idx, n_out):
    mesh, nc, ns, lanes = _sc_mesh()
    nw = nc * ns
    planes, m, width = rows.shape
    ch = SC_CHUNK
    n_ch = m // ch // nw
    half = planes // 2
    assert planes % 2 == 0 and n_ch * ch * nw == m and idx.shape == (m // ch * TOP_K, ch) and n_ch >= 2
    assert n_ch % 8 == 0

    @functools.partial(
        pl.kernel, mesh=mesh, out_type=jax.ShapeDtypeStruct((planes * n_out, width), rows.dtype),
        scratch_types=([pltpu.VMEM((n_ch * TOP_K, ch), I32)] + [pltpu.VMEM((ch, width), rows.dtype)] * planes
                       + [pltpu.VMEM((TOP_K, ch), I32)] * planes
                       + [pltpu.SemaphoreType.DMA((planes,)), pltpu.SemaphoreType.DMA((planes,))]))
    def scatter_kernel(rows_hbm, idx_hbm, out_hbm, idx_v, *rest):
        bufs, ibufs, rsem, ssem = rest[:planes], rest[planes:2 * planes], rest[2 * planes], rest[2 * planes + 1]
        wid = lax.axis_index("s") * nc + lax.axis_index("c")
        for kk in range(TOP_K):
            pltpu.sync_copy(idx_hbm.at[pl.ds(kk * (m // ch) + wid * n_ch, n_ch)], idx_v.at[pl.ds(kk * n_ch, n_ch)])
        base = wid * n_ch * ch

        def read(j, b):
            return pltpu.make_async_copy(rows_hbm.at[b, pl.ds(pl.multiple_of(base + j * ch, ch), ch)], bufs[b], rsem.at[b])

        def scatters(b):
            return [pltpu.make_async_copy(bufs[b], out_hbm.at[ibufs[b].at[kk]], ssem.at[b]) for kk in range(TOP_K)]

        def start_scatters(j, b):
            for kk in range(TOP_K):
                for t in range(0, ch, lanes):
                    ibufs[b][kk, pl.ds(t, lanes)] = idx_v[kk * n_ch + j, pl.ds(t, lanes)] + b * n_out
            for c in scatters(b):
                c.start()

        def wait_scatters(b):
            for c in scatters(b):
                c.wait()

        def finish(j, b):
            pj, pb = (j, b - half) if b >= half else (j - 1, b + half)
            read(pj, pb).wait()
            start_scatters(pj, pb)

        for b in range(planes):
            read(0, b).start()
        for b in range(half, planes):
            finish(0, b)

        @pl.loop(1, n_ch)
        def _(j):
            for b in range(planes):
                wait_scatters(b)
                read(j, b).start()
                finish(j, b)

        for b in range(half):
            finish(n_ch, b)
        for b in range(planes):
            wait_scatters(b)

    return scatter_kernel(rows, idx).reshape(planes, n_out, width)


def _padfill_kernel(start_ref, len_ref, xs_in_ref, xs_ref, zeros_ref, sem):
    del xs_in_ref
    zeros_ref[...] = jnp.zeros_like(zeros_ref)
    bits = [1 << b for b in reversed(range((EXPERT_BLOCK - 1).bit_length()))]

    def pieces(e):
        n = len_ref[e]
        for bit in bits:
            row0 = start_ref[e] + (n & ~(2 * bit - 1))
            copy = pltpu.make_async_copy(zeros_ref.at[:, pl.ds(0, bit)], xs_ref.at[:, pl.ds(row0, bit)], sem)
            yield (n & bit) != 0, copy

    def start(e, c):
        for on, copy in pieces(e):
            pl.when(on)(copy.start)
        return c

    def wait(e, c):
        for on, copy in pieces(e):
            pl.when(on)(copy.wait)
        return c

    lax.fori_loop(0, start_ref.shape[0], start, 0)
    lax.fori_loop(0, start_ref.shape[0], wait, 0)


def _experts_kernel(be_ref, nb_ref, valid_ref, slot_ref, next_ref, xs_ref, wgu_hbm, bgu_ref, wd_hbm, bd_ref, ys_ref,
                    wgu_buf, wd_buf, wgu_bf_ref, wd_bf_ref, gu_sem, d_sem):
    del nb_ref
    blk = pl.program_id(0)
    de = wd_hbm.shape[1]
    valid = valid_ref[blk]
    expert = be_ref[blk]
    slot = slot_ref[blk]

    def weight_copies(e, s):
        return (pltpu.make_async_copy(wgu_hbm.at[e], wgu_buf.at[s], gu_sem.at[s]),
                pltpu.make_async_copy(wd_hbm.at[e], wd_buf.at[s], d_sem.at[s]))

    first_of_expert = jnp.logical_or(blk == 0, expert != be_ref[jnp.maximum(blk - 1, 0)])

    @pl.when(jnp.logical_and(valid > 0, first_of_expert))
    def _():
        @pl.when(blk == 0)
        def _():
            for c in weight_copies(expert, slot):
                c.start()

        for c in weight_copies(expert, slot):
            c.wait()
        wgu_bf_ref[...] = wgu_buf[slot].astype(BF16)
        wd_bf_ref[...] = wd_buf[slot].astype(BF16)

        @pl.when(next_ref[blk] >= 0)
        def _():
            for c in weight_copies(next_ref[blk], 1 - slot):
                c.start()

    def mlp(m):
        x = _unpack_rows(xs_ref.at[:, pl.ds(0, m)]).astype(BF16)
        gu = _dot(x, wgu_bf_ref[...]) + bgu_ref[0]
        glu = jnp.minimum(gu[:, :de], SWIGLU_LIMIT)
        lin = jnp.clip(gu[:, de:], -SWIGLU_LIMIT, SWIGLU_LIMIT)
        act = glu * jax.nn.sigmoid(SWIGLU_ALPHA * glu) * (lin + 1.0)
        y = _dot(act.astype(BF16), wd_bf_ref[...]) + bd_ref[0]
        _pack_rows(y, ys_ref.at[:, pl.ds(0, m)])
        if m < EXPERT_BLOCK:
            ys_ref[:, m:, :] = jnp.zeros((PACK_ROWS, EXPERT_BLOCK - m, LANES), U32)

    step = EXPERT_BLOCK // EXPERT_SPLIT
    for q in range(EXPERT_SPLIT):
        pl.when(jnp.logical_and(valid > q * step, valid <= (q + 1) * step))(functools.partial(mlp, (q + 1) * step))

    @pl.when(valid == 0)
    def _():
        ys_ref[...] = jnp.zeros_like(ys_ref)


def _sc_gather_rows(table, idx):
    mesh, nc, ns, lanes = _sc_mesh()
    nw = nc * ns
    planes, n_tab, width = table.shape
    n_idx_rows, ch = idx.shape
    n_ch = n_idx_rows // nw
    m = n_idx_rows * ch
    half = planes // 2
    assert planes % 2 == 0 and ch == SC_CHUNK and n_ch * nw == n_idx_rows and n_ch >= 2

    @functools.partial(
        pl.kernel, mesh=mesh, out_type=jax.ShapeDtypeStruct((planes, m, width), table.dtype),
        scratch_types=([pltpu.VMEM((n_ch, ch), I32)] + [pltpu.VMEM((ch, width), table.dtype)] * planes
                       + [pltpu.VMEM((8, ch), I32)] * planes
                       + [pltpu.SemaphoreType.DMA((planes,)), pltpu.SemaphoreType.DMA((planes,))]))
    def gather_kernel(table_hbm, idx_hbm, out_hbm, idx_v, *rest):
        bufs, ibufs, gsem, wsem = rest[:planes], rest[planes:2 * planes], rest[2 * planes], rest[2 * planes + 1]
        wid = lax.axis_index("s") * nc + lax.axis_index("c")
        pltpu.sync_copy(idx_hbm.at[pl.ds(wid * n_ch, n_ch)], idx_v)
        base = wid * n_ch * ch

        def gather(b):
            return pltpu.make_async_copy(table_hbm.at[ibufs[b].at[0]], bufs[b], gsem.at[b])

        def start_gather(j, b):
            for t in range(0, ch, lanes):
                ibufs[b][0, pl.ds(t, lanes)] = idx_v[j, pl.ds(t, lanes)] + b * n_tab
            gather(b).start()

        def write(j, b):
            return pltpu.make_async_copy(bufs[b], out_hbm.at[b, pl.ds(pl.multiple_of(base + j * ch, ch), ch)], wsem.at[b])

        def finish(j, b):
            pj, pb = (j, b - half) if b >= half else (j - 1, b + half)
            gather(pb).wait()
            write(pj, pb).start()

        for b in range(planes):
            start_gather(0, b)
        for b in range(half, planes):
            finish(0, b)

        @pl.loop(1, n_ch)
        def _(j):
            for b in range(planes):
                write(j - 1, b).wait()
                start_gather(j, b)
                finish(j, b)

        for b in range(half):
            finish(n_ch, b)
        for b in range(planes):
            write(n_ch - 1, b).wait()

    return gather_kernel(table.reshape(planes * n_tab, width), idx)


def _combine_kernel(x1_ref, gate_ref, y0_ref, y1_ref, y2_ref, y3_ref, *rest):
    o_ref = rest[-1]
    gate = gate_ref[...]
    out = x1_ref[...]
    for kk, y_ref in enumerate((y0_ref, y1_ref, y2_ref, y3_ref)):
        out = out + gate[:, kk:kk + 1] * _unpack_rows(y_ref)
    o_ref[...] = out


def _const_spec(shape):
    nd = len(shape)
    return pl.BlockSpec(shape, lambda *_: (0,) * nd)


def _layer(x, norm1_g, w_in, q_norm_g, k_norm_g, w_pool_grp, pool_scale, w_pool_up, w_attn_up, w_out, norm2_g,
           w_router, b_router, w_gate_up, b_gate_up, w_down, b_down):
    B, S, D = x.shape
    N = B * S
    pw = w_pool_up.shape[0]
    sw = w_attn_up.shape[0]
    n_exp = w_router.shape[1]
    de = w_down.shape[1]
    heads = sw // SB_HEAD_DIM
    assert pw == len(POOL_WINDOWS) * POOL_GROUP_DIM and heads % 2 == 0 and n_exp <= LANES
    assert D == 2 * PACK_ROWS * LANES and w_in.shape[1] == pw + 3 * sw + 2 * D
    tm = 512 if S % 512 == 0 else 256
    assert S % tm == 0 and S % (ATTN_SUB * ATTN_BLOCK) == 0
    xf = x.reshape(N, D)
    cparams = functools.partial(pltpu.CompilerParams, vmem_limit_bytes=VMEM_LIMIT)

    hsum = (jnp.arange(sw)[:, None] // SB_HEAD_DIM == jnp.arange(sw)[None, :] // SB_HEAD_DIM).astype(BF16)
    tm_in = 2 * tm if S % (2 * tm) == 0 else tm
    nt = S // tm_in
    tok_spec = lambda w: pl.BlockSpec((tm_in, w), lambda b, i: (b * nt + i, 0))
    w_in_bf = w_in.astype(BF16)
    n_in = pw + 3 * sw + D
    g1 = norm1_g.reshape(1, D)
    q2, k2, v2, pg = pl.pallas_call(
        functools.partial(_mixer_in_kernel, tm=tm_in, pw=pw, sw=sw),
        grid=(B, nt),
        in_specs=[tok_spec(D), _const_spec((1, D)), _const_spec((D, n_in)), _const_spec((1, sw)),
                  _const_spec((1, sw)), _const_spec((sw, sw)),
                  _const_spec((len(POOL_WINDOWS), POOL_GROUP_DIM, POOL_GROUP_DIM)), _const_spec((1, pw)),
                  _const_spec((pw, D))],
        out_specs=[tok_spec(sw), tok_spec(sw), tok_spec(sw), tok_spec(D)],
        out_shape=[jax.ShapeDtypeStruct((N, sw), BF16)] * 3 + [jax.ShapeDtypeStruct((N, D), BF16)],
        scratch_shapes=[pltpu.VMEM((POOL_HALO, pw), F32)],
        compiler_params=cparams(dimension_semantics=("arbitrary", "arbitrary")),
        name="mixer_in",
    )(xf, g1, w_in_bf[:, :n_in], jnp.tile(q_norm_g, heads).reshape(1, sw),
      jnp.tile(k_norm_g, heads).reshape(1, sw), hsum, w_pool_grp.astype(BF16), pool_scale.reshape(1, pw),
      w_pool_up.astype(BF16))

    bq = ATTN_BLOCK
    rows_q = ATTN_SUB * bq
    nq = S // rows_q
    jj = jnp.arange(bq)
    tri = jnp.concatenate([(jj[:, None] >= jj[None, :]).astype(BF16), jnp.ones((bq, bq), BF16)], axis=1)
    n_pairs = heads // 2
    n_units = ATTN_SUB * n_pairs
    kv_spec = pl.BlockSpec(memory_space=pl.ANY)
    sba = pl.pallas_call(
        functools.partial(_attn_kernel, n_pairs=n_pairs),
        grid=(B, nq),
        in_specs=[pl.BlockSpec((rows_q, sw), lambda b, qi: (b * nq + qi, 0)), kv_spec, kv_spec,
                  _const_spec((bq, 2 * bq))],
        out_specs=pl.BlockSpec((rows_q, sw), lambda b, qi: (b * nq + qi, 0)),
        out_shape=jax.ShapeDtypeStruct((N, sw), BF16),
        scratch_shapes=([pltpu.VMEM((2, S, sw), BF16), pltpu.VMEM((2, S, sw), BF16),
                         pltpu.SemaphoreType.DMA((2,)), pltpu.SemaphoreType.DMA((2,))]
                        + [pltpu.VMEM((2 * bq, LANES), BF16)] * n_units + [pltpu.VMEM((2 * bq, LANES), F32)] * (2 * n_units)),
        compiler_params=cparams(dimension_semantics=("arbitrary", "arbitrary")),
        name="sb_attn",
    )(q2, k2, v2, tri)

    wr = jnp.zeros((D, LANES), F32).at[:, :n_exp].set(w_router)
    wr_hi = wr.astype(BF16)
    wr_lo = (wr - wr_hi.astype(F32)).astype(BF16)
    br = jnp.full((1, LANES), -jnp.inf, F32).at[0, :n_exp].set(b_router)
    ltri = (jnp.arange(tm)[:, None] > jnp.arange(tm)[None, :]).astype(BF16)
    assert N % (MOE_GROUPS * tm) == 0
    ng = N // MOE_GROUPS
    steps = ng // tm
    n_assign = ng * TOP_K
    n_blocks = -(-(n_assign + n_exp * (EXPERT_BLOCK - 1)) // EXPERT_BLOCK)
    n_rows = n_blocks * EXPERT_BLOCK
    plane_spec = lambda rows, index: pl.BlockSpec((PACK_ROWS, rows, LANES), lambda i, *_: (0, index(i, *_), 0))
    row_spec = lambda w: pl.BlockSpec((tm, w), lambda i: (i, 0))
    w_attn_up_bf, w_out_bf, g2 = w_attn_up.astype(BF16), w_out.astype(BF16), norm2_g.reshape(1, D)
    bgu, bdn = b_gate_up.reshape(n_exp, 1, 2 * de), b_down.reshape(n_exp, 1, D)
    out = None
    for grp in range(MOE_GROUPS):
        grp_spec = lambda w, first=grp * steps: pl.BlockSpec((tm, w), lambda i: (first + i, 0))

        cur = lambda i: jnp.minimum(i, steps - 1)
        prev = lambda i: jnp.maximum(i - 1, 0)
        in_spec = lambda w, first=grp * steps: pl.BlockSpec((tm, w), lambda i: (first + cur(i), 0))
        x1, hpk, ri, gate, cnt = pl.pallas_call(
            functools.partial(_mixer_out_kernel, tm=tm),
            grid=(steps + 1,),
            in_specs=[in_spec(D), in_spec(D), in_spec(sw), _const_spec((1, D)), _const_spec((D, D)),
                      _const_spec((sw, D)), _const_spec((D, D)),
                      _const_spec((1, D)), _const_spec((D, LANES)), _const_spec((D, LANES)), _const_spec((1, LANES)),
                      _const_spec((tm, tm))],
            out_specs=[pl.BlockSpec((tm, D), lambda i: (cur(i), 0)), plane_spec(tm, cur),
                       pl.BlockSpec((2 * TOP_K, tm), lambda i: (0, prev(i))),
                       pl.BlockSpec((tm, LANES), lambda i: (prev(i), 0)), _const_spec((8, LANES))],
            out_shape=[jax.ShapeDtypeStruct((ng, D), F32), jax.ShapeDtypeStruct((PACK_ROWS, ng, LANES), U32),
                       jax.ShapeDtypeStruct((2 * TOP_K, ng), I32), jax.ShapeDtypeStruct((ng, LANES), F32),
                       jax.ShapeDtypeStruct((8, LANES), F32)],
            scratch_shapes=[pltpu.VMEM((2, tm, LANES), F32)],
            compiler_params=cparams(dimension_semantics=("arbitrary",)),
            name="mixer_out",
        )(xf, pg, sba, g1, w_in_bf[:, n_in:], w_attn_up_bf, w_out_bf, g2, wr_hi, wr_lo, br, ltri)

        counts = cnt[0, :n_exp].astype(I32)
        padded = (counts + EXPERT_BLOCK - 1) // EXPERT_BLOCK * EXPERT_BLOCK
        padded_end = jnp.cumsum(padded)
        start_pad = padded_end - padded
        td = min(ng, 8192)
        dest = pl.pallas_call(
            _dest_kernel,
            grid_spec=pltpu.PrefetchScalarGridSpec(
                num_scalar_prefetch=1,
                grid=(ng // td,),
                in_specs=[pl.BlockSpec((2 * TOP_K, td), lambda i, sp: (0, i))],
                out_specs=pl.BlockSpec((TOP_K, td), lambda i, sp: (0, i)),
            ),
            out_shape=jax.ShapeDtypeStruct((TOP_K, ng), I32),
            compiler_params=cparams(dimension_semantics=("arbitrary",)),
            name="dest",
        )(start_pad, ri).reshape(-1, SC_CHUNK)
        block_start = jnp.arange(n_blocks, dtype=I32) * EXPERT_BLOCK
        block_expert = jnp.minimum(jnp.sum((padded_end[None, :] <= block_start[:, None]).astype(I32), axis=1),
                                   n_exp - 1)
        n_used = (padded_end[-1] // EXPERT_BLOCK).astype(I32).reshape(1)

        xs = _sc_scatter_rows(hpk, dest, n_rows)
        xs = pl.pallas_call(
            _padfill_kernel,
            grid_spec=pltpu.PrefetchScalarGridSpec(
                num_scalar_prefetch=2,
                grid=(1,),
                in_specs=[pl.BlockSpec(memory_space=pl.ANY)],
                out_specs=pl.BlockSpec(memory_space=pl.ANY),
                scratch_shapes=[pltpu.VMEM((PACK_ROWS, 1 << ((EXPERT_BLOCK - 1).bit_length() - 1), LANES), U32),
                                pltpu.SemaphoreType.DMA(())],
            ),
            out_shape=jax.ShapeDtypeStruct((PACK_ROWS, n_rows, LANES), U32),
            input_output_aliases={2: 0},
            compiler_params=cparams(dimension_semantics=("arbitrary",)),
            name="padfill",
        )(start_pad + counts, padded - counts, xs)

        experts = jnp.arange(n_exp, dtype=I32)
        used = counts > 0
        block_valid = jnp.clip(jnp.sum(jnp.where(block_expert[:, None] == experts[None, :],
                                                 (start_pad + counts)[None, :], 0), axis=1) - block_start,
                               0, EXPERT_BLOCK)
        ordinal = jnp.cumsum(used.astype(I32)) - 1
        next_used = jnp.min(jnp.where(jnp.logical_and(experts[None, :] > experts[:, None], used[None, :]),
                                      experts[None, :], n_exp), axis=1)
        next_used = jnp.where(next_used == n_exp, -1, next_used)
        per_block = lambda table: jnp.sum(jnp.where(block_expert[:, None] == experts[None, :], table[None, :], 0), axis=1)
        b_spec = lambda shape: pl.BlockSpec(shape, lambda i, be, *_: (be[i], 0, 0))
        ys = pl.pallas_call(
            _experts_kernel,
            grid_spec=pltpu.PrefetchScalarGridSpec(
                num_scalar_prefetch=5,
                grid=(n_blocks,),
                in_specs=[plane_spec(EXPERT_BLOCK, lambda i, be, nb, *_: jnp.minimum(i, nb[0] - 1)),
                          pl.BlockSpec(memory_space=pl.ANY), b_spec((1, 1, 2 * de)),
                          pl.BlockSpec(memory_space=pl.ANY), b_spec((1, 1, D))],
                out_specs=plane_spec(EXPERT_BLOCK, lambda i, *_: i),
                scratch_shapes=[pltpu.VMEM((2, D, 2 * de), F32), pltpu.VMEM((2, de, D), F32),
                                pltpu.VMEM((D, 2 * de), BF16), pltpu.VMEM((de, D), BF16),
                                pltpu.SemaphoreType.DMA((2,)), pltpu.SemaphoreType.DMA((2,))],
            ),
            out_shape=jax.ShapeDtypeStruct((PACK_ROWS, n_rows, LANES), U32),
            compiler_params=cparams(dimension_semantics=("arbitrary",)),
            name="experts",
        )(block_expert, n_used, block_valid, per_block(ordinal % 2), per_block(next_used), xs, w_gate_up, bgu, w_down, bdn)

        y4 = _sc_gather_rows(ys, dest.reshape(-1, SC_CHUNK))

        operands = [x1, gate, y4, y4, y4, y4] + ([] if out is None else [out])
        out = pl.pallas_call(
            _combine_kernel,
            grid=(steps,),
            in_specs=([row_spec(D), row_spec(LANES)]
                      + [plane_spec(tm, lambda i, kk=kk: kk * steps + i) for kk in range(TOP_K)]
                      + ([] if out is None else [pl.BlockSpec(memory_space=pl.ANY)])),
            out_specs=grp_spec(D),
            out_shape=jax.ShapeDtypeStruct((N, D), F32),
            input_output_aliases={} if out is None else {len(operands) - 1: 0},
            compiler_params=cparams(dimension_semantics=("arbitrary",)),
            name="combine",
        )(*operands)
    return out.reshape(B, S, D)


def kernel(x, norm1_g, w_in, q_norm_g, k_norm_g, w_pool_grp, pool_scale, w_pool_up, w_attn_up, w_out, norm2_g,
           w_router, b_router, w_gate_up, b_gate_up, w_down, b_down):
    for layer in range(norm1_g.shape[0]):
        x = _layer(x, norm1_g[layer], w_in[layer], q_norm_g[layer], k_norm_g[layer], w_pool_grp[layer],
                   pool_scale[layer], w_pool_up[layer], w_attn_up[layer], w_out[layer], norm2_g[layer],
                   w_router[layer], b_router[layer], w_gate_up[layer], b_gate_up[layer], w_down[layer],
                   b_down[layer])
    return x
```

```python
import functools

import jax
import jax.numpy as jnp
from jax import lax
from jax.experimental import pallas as pl
from jax.experimental.pallas import tpu as pltpu
from jax.experimental.pallas import tpu_sc as plsc

F32 = jnp.float32
BF16 = jnp.bfloat16
U32 = jnp.uint32
I32 = jnp.int32

EPS = 1e-6
POOL_WINDOWS = (2, 4, 8, 16)
POOL_GROUP_DIM = 128
POOL_HALO = 16
SB_HEAD_DIM = 64
TOP_K = 4
SWIGLU_LIMIT = 7.0
SWIGLU_ALPHA = 1.702
EXPERT_BLOCK = 1024
EXPERT_SPLIT = 8
EXPERT_STEP = EXPERT_BLOCK // EXPERT_SPLIT
LANES = 128
PACK_ROWS = 4
ATTN_BLOCK = 128
ATTN_SUB = 2
ATTN_EXIT_BITS = 70.0
LOG2_E = 1.4426950408889634
VMEM_LIMIT = 56 * 1024 * 1024
SC_CHUNK = 128
MOE_GROUPS = 2


def _dot(a, b):
    return jnp.dot(a, b, preferred_element_type=F32)


def _split_bf16(x):
    hi = x.astype(BF16)
    lo = (x - hi.astype(F32)).astype(BF16)
    return hi, lo


def _pack_rows(v, out_ref):
    half = v.shape[1] // 2
    lo = lax.bitcast_convert_type(v[:, :half].astype(BF16).astype(F32), U32) >> 16
    hi = lax.bitcast_convert_type(v[:, half:].astype(BF16).astype(F32), U32) & jnp.uint32(0xFFFF0000)
    w = lo | hi
    for c in range(PACK_ROWS):
        out_ref[c] = w[:, c * LANES:(c + 1) * LANES]


def _unpack_rows(ref):
    los, his = [], []
    for c in range(PACK_ROWS):
        w = ref[c]
        los.append(lax.bitcast_convert_type(w << 16, F32))
        his.append(lax.bitcast_convert_type(w & jnp.uint32(0xFFFF0000), F32))
    return jnp.concatenate(los + his, axis=1)


def _mixer_in_kernel(x_ref, g1_ref, win_ref, gq_ref, gk_ref, hsum_ref, wgrp_ref, pscale_ref, wpu_ref,
                     q_ref, k_ref, v_ref, p_ref, tail_ref, *, tm, pw, sw):
    i = pl.program_id(1)

    @pl.when(i == 0)
    def _():
        tail_ref[...] = jnp.zeros_like(tail_ref)

    x = x_ref[...]
    ms = jnp.mean(x * x, axis=-1, keepdims=True)
    h = (x * lax.rsqrt(ms + EPS) * g1_ref[...]).astype(BF16)

    def head_norm(t, gain):
        ss = _dot((t * t).astype(BF16), hsum_ref[...])
        return t * lax.rsqrt(ss * (1.0 / SB_HEAD_DIM) + EPS) * gain

    def project_q():
        q = _dot(h, win_ref[:, pw:pw + sw])
        q_ref[...] = (head_norm(q, gq_ref[...]) * (SB_HEAD_DIM ** -0.5 * LOG2_E)).astype(BF16)

    def project_k():
        k = _dot(h, win_ref[:, pw + sw:pw + 2 * sw])
        k_ref[...] = head_norm(k, gk_ref[...]).astype(BF16)

    def project_v():
        v_ref[...] = _dot(h, win_ref[:, pw + 2 * sw:pw + 3 * sw]).astype(BF16)

    d_model = x.shape[1]
    pool_gate = []

    def project_pool_gate():
        pool_gate.append(jax.nn.sigmoid(_dot(h, win_ref[:, pw + 3 * sw:pw + 3 * sw + d_model])))

    u = _dot(h, win_ref[:, 0:pw])
    xx = jnp.concatenate([tail_ref[...], u], axis=0)
    tail_ref[...] = u[tm - POOL_HALO:, :]
    pos = i * tm + lax.broadcasted_iota(I32, (tm, POOL_GROUP_DIM), 0)
    mixed = []
    for (g, w), project in zip(enumerate(POOL_WINDOWS), (project_q, project_k, project_v, project_pool_gate)):
        project()
        s = xx[:, g * POOL_GROUP_DIM:(g + 1) * POOL_GROUP_DIM]
        step = 1
        while step < w:
            s = s + pltpu.roll(s, step, axis=0)
            step *= 2
        count = jnp.minimum(pos + 1, w).astype(F32)
        ug = u[:, g * POOL_GROUP_DIM:(g + 1) * POOL_GROUP_DIM]
        d = s[POOL_HALO:, :] / count - ug
        mixed.append(_dot(d.astype(BF16), wgrp_ref[g]))
    pm = jnp.concatenate(mixed, axis=1) * pscale_ref[...]
    pool_out = _dot(pm.astype(BF16), wpu_ref[...])
    p_ref[...] = (pool_gate[0] * pool_out).astype(BF16)


def _attn_kernel(q_ref, k_hbm, v_hbm, tri_ref, o_ref, kbuf, vbuf, ksem, vsem, *scratch, n_pairs):
    units = [(sub, p) for sub in range(ATTN_SUB) for p in range(n_pairs)]
    n_units = len(units)
    qs, acc, rr = scratch[:n_units], scratch[n_units:2 * n_units], scratch[2 * n_units:]
    bq = ATTN_BLOCK
    batch = pl.program_id(0)
    seq = kbuf.shape[1]

    def kv_copies(b):
        rows = pl.ds(pl.multiple_of(b * seq, seq), seq)
        return (pltpu.make_async_copy(k_hbm.at[rows], kbuf.at[b % 2], ksem.at[b % 2]),
                pltpu.make_async_copy(v_hbm.at[rows], vbuf.at[b % 2], vsem.at[b % 2]))

    @pl.when(pl.program_id(1) == 0)
    def _():
        @pl.when(batch == 0)
        def _():
            for c in kv_copies(batch):
                c.start()

        for c in kv_copies(batch):
            c.wait()

        @pl.when(batch + 1 < pl.num_programs(0))
        def _():
            for c in kv_copies(batch + 1):
                c.start()

    k_ref = kbuf.at[batch % 2]
    v_ref = vbuf.at[batch % 2]
    first_block = pl.program_id(1) * ATTN_SUB
    first_head = lax.broadcasted_iota(I32, (bq, LANES), 1) < SB_HEAD_DIM
    for u, (sub, p) in enumerate(units):
        q2 = q_ref[sub * bq:(sub + 1) * bq, p * LANES:(p + 1) * LANES]
        qs[u][:bq] = jnp.where(first_head, q2, jnp.zeros_like(q2))
        qs[u][bq:] = jnp.where(first_head, jnp.zeros_like(q2), q2)
    row = lax.broadcasted_iota(I32, (2 * bq, bq), 0)
    col = lax.broadcasted_iota(I32, (2 * bq, bq), 1)
    causal = col < (row & (bq - 1))
    contract_last = (((1,), (1,)), ((), ()))

    def softplus(z):
        return jnp.maximum(z, 0.0) + jnp.log2(1.0 + jnp.exp2(-jnp.abs(z)))

    def suffix_sums(sp):
        return _dot(sp.astype(BF16), tri_ref[...])

    def cols(ref, block, p):
        start = pl.multiple_of(jnp.maximum(block, 0) * bq, bq)
        return ref[pl.ds(start, bq), p * LANES:(p + 1) * LANES]

    def scores(u, block):
        return lax.dot_general(qs[u][...], cols(k_ref, block, units[u][1]), contract_last, preferred_element_type=F32)

    diag = [first_block + sub for sub, _ in units]
    z_d, z_n, s_d, s_n = {}, {}, {}, {}

    def stage_scores(u):
        z_d[u] = scores(u, diag[u])
        z_n[u] = scores(u, diag[u] - 1)

    def stage_sums(u):
        s_d[u] = suffix_sums(jnp.where(causal, softplus(z_d[u]), 0.0))
        s_n[u] = suffix_sums(jnp.where(diag[u] >= 1, softplus(z_n[u]), 0.0))

    def stage_values(u):
        p = units[u][1]
        a_d = jnp.where(causal, jnp.exp2(z_d[u] - s_d[u][:, :bq]), 0.0)
        r_d = s_d[u][:, bq:]
        a_n = jnp.where(diag[u] >= 1, jnp.exp2(z_n[u] - (r_d + s_n[u][:, :bq])), 0.0)
        acc[u][...] = (_dot(a_d.astype(BF16), cols(v_ref, diag[u], p)) + _dot(a_n.astype(BF16), cols(v_ref, diag[u] - 1, p)))
        rr[u][...] = r_d + s_n[u][:, bq:]

    for t in range(n_units + 2):
        if t < n_units:
            stage_scores(t)
        if 0 <= t - 1 < n_units:
            stage_sums(t - 1)
        if 0 <= t - 2 < n_units:
            stage_values(t - 2)

    def r_min():
        m = rr[0][...]
        for u in range(1, n_units):
            m = jnp.minimum(m, rr[u][...])
        return jnp.min(m)

    def cond(c):
        back, rm = c
        return jnp.logical_and(diag[-1] - back >= 0, rm < ATTN_EXIT_BITS)

    def body(c):
        back, _ = c
        blocks = [d - back for d in diag]
        zs = [scores(u, blocks[u]) for u in range(n_units)]
        ss = [suffix_sums(jnp.where(blocks[u] >= 0, softplus(zs[u]), 0.0)) for u in range(n_units)]
        for u, (_, p) in enumerate(units):
            r = rr[u][...]
            a = jnp.where(blocks[u] >= 0, jnp.exp2(zs[u] - (r + ss[u][:, :bq])), 0.0)
            acc[u][...] += _dot(a.astype(BF16), cols(v_ref, blocks[u], p))
            rr[u][...] = r + ss[u][:, bq:]
        return back + 1, r_min()

    lax.while_loop(cond, body, (2, r_min()))
    for u, (sub, p) in enumerate(units):
        o_ref[sub * bq:(sub + 1) * bq, p * LANES:(p + 1) * LANES] = (
            jnp.where(first_head, acc[u][:bq], acc[u][bq:]).astype(BF16))


def _mixer_out_kernel(x_ref, p_ref, sba_ref, g1_ref, wga_ref, wau_ref, wout_ref, g2_ref, wr_hi_ref, wr_lo_ref, br_ref,
                      ltri_ref, x1_ref, hp_ref, ri_ref, gate_ref, cnt_ref, logit_ref, *, tm):
    step = pl.program_id(0)

    @pl.when(step == 0)
    def _():
        cnt_ref[...] = jnp.zeros_like(cnt_ref)
        logit_ref[1] = jnp.zeros((tm, LANES), F32)

    logits = logit_ref[(step + 1) % 2]
    routed = jnp.where(step >= 1, 1.0, 0.0)
    lane = lax.broadcasted_iota(I32, logits.shape, 1).astype(F32)
    work = logits
    vals, idxs = [], []

    def topk_round():
        nonlocal work
        m = jnp.max(work, axis=-1, keepdims=True)
        ik = jnp.min(jnp.where(work == m, lane, float(LANES)), axis=-1, keepdims=True)
        vals.append(m)
        idxs.append(ik)
        work = jnp.where(lane == ik, -jnp.inf, work)

    d_model = x_ref.shape[1]
    quarter = d_model // TOP_K
    topk_round()
    x = x_ref[...]
    h1 = (x * lax.rsqrt(jnp.mean(x * x, axis=-1, keepdims=True) + EPS) * g1_ref[...]).astype(BF16)
    attn_gate = jax.nn.sigmoid(_dot(h1, wga_ref[...]))
    attn_out = _dot(sba_ref[...], wau_ref[...])
    merged = (p_ref[...].astype(F32) + attn_gate * attn_out).astype(BF16)
    x1_parts = []
    for c in range(TOP_K):
        if c >= 1:
            topk_round()
        cols = slice(c * quarter, (c + 1) * quarter)
        x1_parts.append(x[:, cols] + _dot(merged, wout_ref[:, cols]))
    x1 = jnp.concatenate(x1_parts, axis=1)
    x1_ref[...] = x1

    es = [jnp.exp(v - vals[0]) for v in vals]
    denom = es[0] + es[1] + es[2] + es[3]
    hot = jnp.zeros(logits.shape, F32)
    for ik in idxs:
        hot = hot + jnp.where(lane == ik, routed, 0.0)
    before = _dot(ltri_ref[...], hot.astype(BF16)) + cnt_ref[0:1, :]

    ms = jnp.mean(x1 * x1, axis=-1, keepdims=True)
    h2 = x1 * lax.rsqrt(ms + EPS) * g2_ref[...]
    _pack_rows(h2, hp_ref)

    ri = jnp.zeros(logits.shape, F32)
    gt = jnp.zeros(logits.shape, F32)
    for kk in range(TOP_K):
        rank = jnp.sum(jnp.where(lane == idxs[kk], before, 0.0), axis=-1, keepdims=True)
        ri = jnp.where(lane == kk, idxs[kk], ri)
        ri = jnp.where(lane == TOP_K + kk, rank, ri)
        gt = jnp.where(lane == kk, es[kk] / denom, gt)
    ri_ref[...] = ri.T[:2 * TOP_K].astype(I32)
    gate_ref[...] = gt
    cnt_ref[...] = cnt_ref[...] + jnp.sum(hot, axis=0, keepdims=True)

    h_hi, h_lo = _split_bf16(h2)
    logit_ref[step % 2] = (_dot(h_hi, wr_hi_ref[...]) + _dot(h_hi, wr_lo_ref[...]) + _dot(h_lo, wr_hi_ref[...])
                           + br_ref[...])


def _dest_kernel(start_ref, ri_ref, dest_ref):
    idx = ri_ref[:TOP_K, :]
    dest = ri_ref[TOP_K:, :]
    for e in range(start_ref.shape[0]):
        dest = dest + jnp.where(idx == e, start_ref[e], 0)
    dest_ref[...] = dest


def _sc_mesh():
    info = plsc.get_sparse_core_info()
    mesh = plsc.VectorSubcoreMesh(core_axis_name="c", subcore_axis_name="s")
    return mesh, info.num_cores, info.num_subcores, info.num_lanes


def _sc_scatter_rows(rows, idx, n_out):
    mesh, nc, ns, lanes = _sc_mesh()
    nw = nc * ns
    planes, m, width = rows.shape
    ch = SC_CHUNK
    n_ch = m // ch // nw
    half = planes // 2
    assert planes % 2 == 0 and n_ch * ch * nw == m and idx.shape == (m // ch * TOP_K, ch) and n_ch >= 2
    assert n_ch % 8 == 0

    @functools.partial(
        pl.kernel, mesh=mesh, out_type=jax.ShapeDtypeStruct((planes * n_out, width), rows.dtype),
        scratch_types=([pltpu.VMEM((n_ch * TOP_K, ch), I32)] + [pltpu.VMEM((ch, width), rows.dtype)] * planes
                       + [pltpu.VMEM((TOP_K, ch), I32)] * planes
                       + [pltpu.SemaphoreType.DMA((planes,)), pltpu.SemaphoreType.DMA((planes,))]))
    def scatter_kernel(rows_hbm, idx_hbm, out_hbm, idx_v, *rest):
        bufs, ibufs, rsem, ssem = rest[:planes], rest[planes:2 * planes], rest[2 * planes], rest[2 * planes + 1]
        wid = lax.axis_index("s") * nc + lax.axis_index("c")
        for kk in range(TOP_K):
            pltpu.sync_copy(idx_hbm.at[pl.ds(kk * (m // ch) + wid * n_ch, n_ch)], idx_v.at[pl.ds(kk * n_ch, n_ch)])
        base = wid * n_ch * ch

        def read(j, b):
            return pltpu.make_async_copy(rows_hbm.at[b, pl.ds(pl.multiple_of(base + j * ch, ch), ch)], bufs[b], rsem.at[b])

        def scatters(b):
            return [pltpu.make_async_copy(bufs[b], out_hbm.at[ibufs[b].at[kk]], ssem.at[b]) for kk in range(TOP_K)]

        def start_scatters(j, b):
            for kk in range(TOP_K):
                for t in range(0, ch, lanes):
                    ibufs[b][kk, pl.ds(t, lanes)] = idx_v[kk * n_ch + j, pl.ds(t, lanes)] + b * n_out
            for c in scatters(b):
                c.start()

        def wait_scatters(b):
            for c in scatters(b):
                c.wait()

        def finish(j, b):
            pj, pb = (j, b - half) if b >= half else (j - 1, b + half)
            read(pj, pb).wait()
            start_scatters(pj, pb)

        for b in range(planes):
            read(0, b).start()
        for b in range(half, planes):
            finish(0, b)

        @pl.loop(1, n_ch)
        def _(j):
            for b in range(planes):
                wait_scatters(b)
                read(j, b).start()
                finish(j, b)

        for b in range(half):
            finish(n_ch, b)
        for b in range(planes):
            wait_scatters(b)

    return scatter_kernel(rows, idx).reshape(planes, n_out, width)


def _padfill_kernel(start_ref, len_ref, xs_in_ref, xs_ref, zeros_ref, sem):
    del xs_in_ref
    zeros_ref[...] = jnp.zeros_like(zeros_ref)
    bits = [1 << b for b in reversed(range((EXPERT_STEP - 1).bit_length()))]

    def pieces(e):
        n = len_ref[e]
        for bit in bits:
            row0 = start_ref[e] + (n & ~(2 * bit - 1))
            copy = pltpu.make_async_copy(zeros_ref.at[:, pl.ds(0, bit)], xs_ref.at[:, pl.ds(row0, bit)], sem)
            yield (n & bit) != 0, copy

    def start(e, c):
        for on, copy in pieces(e):
            pl.when(on)(copy.start)
        return c

    def wait(e, c):
        for on, copy in pieces(e):
            pl.when(on)(copy.wait)
        return c

    lax.fori_loop(0, start_ref.shape[0], start, 0)
    lax.fori_loop(0, start_ref.shape[0], wait, 0)


def _experts_kernel(be_ref, nb_ref, valid_ref, slot_ref, next_ref, xs_ref, wgu_hbm, bgu_ref, wd_hbm, bd_ref, ys_ref,
                    wgu_buf, wd_buf, wgu_bf_ref, wd_bf_ref, gu_sem, d_sem):
    del nb_ref
    blk = pl.program_id(0)
    de = wd_hbm.shape[1]
    valid = valid_ref[blk]
    expert = be_ref[blk]
    slot = slot_ref[blk]

    def weight_copies(e, s):
        return (pltpu.make_async_copy(wgu_hbm.at[e], wgu_buf.at[s], gu_sem.at[s]),
                pltpu.make_async_copy(wd_hbm.at[e], wd_buf.at[s], d_sem.at[s]))

    first_of_expert = jnp.logical_or(blk == 0, expert != be_ref[jnp.maximum(blk - 1, 0)])

    @pl.when(jnp.logical_and(valid > 0, first_of_expert))
    def _():
        @pl.when(blk == 0)
        def _():
            for c in weight_copies(expert, slot):
                c.start()

        for c in weight_copies(expert, slot):
            c.wait()
        wgu_bf_ref[...] = wgu_buf[slot].astype(BF16)
        wd_bf_ref[...] = wd_buf[slot].astype(BF16)

        @pl.when(next_ref[blk] >= 0)
        def _():
            for c in weight_copies(next_ref[blk], 1 - slot):
                c.start()

    def mlp(m):
        x = _unpack_rows(xs_ref.at[:, pl.ds(0, m)]).astype(BF16)
        gu = _dot(x, wgu_bf_ref[...]) + bgu_ref[0]
        glu = jnp.minimum(gu[:, :de], SWIGLU_LIMIT)
        lin = jnp.clip(gu[:, de:], -SWIGLU_LIMIT, SWIGLU_LIMIT)
        act = glu * jax.nn.sigmoid(SWIGLU_ALPHA * glu) * (lin + 1.0)
        y = _dot(act.astype(BF16), wd_bf_ref[...]) + bd_ref[0]
        _pack_rows(y, ys_ref.at[:, pl.ds(0, m)])
        if m < EXPERT_BLOCK:
            ys_ref[:, m:, :] = jnp.zeros((PACK_ROWS, EXPERT_BLOCK - m, LANES), U32)

    for q in range(EXPERT_SPLIT):
        pl.when(jnp.logical_and(valid > q * EXPERT_STEP, valid <= (q + 1) * EXPERT_STEP))(
            functools.partial(mlp, (q + 1) * EXPERT_STEP))

    @pl.when(valid == 0)
    def _():
        ys_ref[...] = jnp.zeros_like(ys_ref)


def _sc_gather_rows(table, idx):
    mesh, nc, ns, lanes = _sc_mesh()
    nw = nc * ns
    planes, n_tab, width = table.shape
    n_idx_rows, ch = idx.shape
    n_ch = n_idx_rows // nw
    m = n_idx_rows * ch
    half = planes // 2
    assert planes % 2 == 0 and ch == SC_CHUNK and n_ch * nw == n_idx_rows and n_ch >= 2

    @functools.partial(
        pl.kernel, mesh=mesh, out_type=jax.ShapeDtypeStruct((planes, m, width), table.dtype),
        scratch_types=([pltpu.VMEM((n_ch, ch), I32)] + [pltpu.VMEM((ch, width), table.dtype)] * planes
                       + [pltpu.VMEM((8, ch), I32)] * planes
                       + [pltpu.SemaphoreType.DMA((planes,)), pltpu.SemaphoreType.DMA((planes,))]))
    def gather_kernel(table_hbm, idx_hbm, out_hbm, idx_v, *rest):
        bufs, ibufs, gsem, wsem = rest[:planes], rest[planes:2 * planes], rest[2 * planes], rest[2 * planes + 1]
        wid = lax.axis_index("s") * nc + lax.axis_index("c")
        pltpu.sync_copy(idx_hbm.at[pl.ds(wid * n_ch, n_ch)], idx_v)
        base = wid * n_ch * ch

        def gather(b):
            return pltpu.make_async_copy(table_hbm.at[ibufs[b].at[0]], bufs[b], gsem.at[b])

        def start_gather(j, b):
            for t in range(0, ch, lanes):
                ibufs[b][0, pl.ds(t, lanes)] = idx_v[j, pl.ds(t, lanes)] + b * n_tab
            gather(b).start()

        def write(j, b):
            return pltpu.make_async_copy(bufs[b], out_hbm.at[b, pl.ds(pl.multiple_of(base + j * ch, ch), ch)], wsem.at[b])

        def finish(j, b):
            pj, pb = (j, b - half) if b >= half else (j - 1, b + half)
            gather(pb).wait()
            write(pj, pb).start()

        for b in range(planes):
            start_gather(0, b)
        for b in range(half, planes):
            finish(0, b)

        @pl.loop(1, n_ch)
        def _(j):
            for b in range(planes):
                write(j - 1, b).wait()
                start_gather(j, b)
                finish(j, b)

        for b in range(half):
            finish(n_ch, b)
        for b in range(planes):
            write(n_ch - 1, b).wait()

    return gather_kernel(table.reshape(planes * n_tab, width), idx)


def _combine_kernel(x1_ref, gate_ref, y0_ref, y1_ref, y2_ref, y3_ref, *rest):
    o_ref = rest[-1]
    gate = gate_ref[...]
    out = x1_ref[...]
    for kk, y_ref in enumerate((y0_ref, y1_ref, y2_ref, y3_ref)):
        out = out + gate[:, kk:kk + 1] * _unpack_rows(y_ref)
    o_ref[...] = out


def _const_spec(shape):
    nd = len(shape)
    return pl.BlockSpec(shape, lambda *_: (0,) * nd)


def _layer(x, norm1_g, w_in, q_norm_g, k_norm_g, w_pool_grp, pool_scale, w_pool_up, w_attn_up, w_out, norm2_g,
           w_router, b_router, w_gate_up, b_gate_up, w_down, b_down):
    B, S, D = x.shape
    N = B * S
    pw = w_pool_up.shape[0]
    sw = w_attn_up.shape[0]
    n_exp = w_router.shape[1]
    de = w_down.shape[1]
    heads = sw // SB_HEAD_DIM
    assert pw == len(POOL_WINDOWS) * POOL_GROUP_DIM and heads % 2 == 0 and n_exp <= LANES
    assert D == 2 * PACK_ROWS * LANES and w_in.shape[1] == pw + 3 * sw + 2 * D
    tm = 512 if S % 512 == 0 else 256
    assert S % tm == 0 and S % (ATTN_SUB * ATTN_BLOCK) == 0
    xf = x.reshape(N, D)
    cparams = functools.partial(pltpu.CompilerParams, vmem_limit_bytes=VMEM_LIMIT)

    hsum = (jnp.arange(sw)[:, None] // SB_HEAD_DIM == jnp.arange(sw)[None, :] // SB_HEAD_DIM).astype(BF16)
    tm_in = 2 * tm if S % (2 * tm) == 0 else tm
    nt = S // tm_in
    tok_spec = lambda w: pl.BlockSpec((tm_in, w), lambda b, i: (b * nt + i, 0))
    w_in_bf = w_in.astype(BF16)
    n_in = pw + 3 * sw + D
    g1 = norm1_g.reshape(1, D)
    q2, k2, v2, pg = pl.pallas_call(
        functools.partial(_mixer_in_kernel, tm=tm_in, pw=pw, sw=sw),
        grid=(B, nt),
        in_specs=[tok_spec(D), _const_spec((1, D)), _const_spec((D, n_in)), _const_spec((1, sw)),
                  _const_spec((1, sw)), _const_spec((sw, sw)),
                  _const_spec((len(POOL_WINDOWS), POOL_GROUP_DIM, POOL_GROUP_DIM)), _const_spec((1, pw)),
                  _const_spec((pw, D))],
        out_specs=[tok_spec(sw), tok_spec(sw), tok_spec(sw), tok_spec(D)],
        out_shape=[jax.ShapeDtypeStruct((N, sw), BF16)] * 3 + [jax.ShapeDtypeStruct((N, D), BF16)],
        scratch_shapes=[pltpu.VMEM((POOL_HALO, pw), F32)],
        compiler_params=cparams(dimension_semantics=("arbitrary", "arbitrary")),
        name="mixer_in",
    )(xf, g1, w_in_bf[:, :n_in], jnp.tile(q_norm_g, heads).reshape(1, sw),
      jnp.tile(k_norm_g, heads).reshape(1, sw), hsum, w_pool_grp.astype(BF16), pool_scale.reshape(1, pw),
      w_pool_up.astype(BF16))

    bq = ATTN_BLOCK
    rows_q = ATTN_SUB * bq
    nq = S // rows_q
    jj = jnp.arange(bq)
    tri = jnp.concatenate([(jj[:, None] >= jj[None, :]).astype(BF16), jnp.ones((bq, bq), BF16)], axis=1)
    n_pairs = heads // 2
    n_units = ATTN_SUB * n_pairs
    kv_spec = pl.BlockSpec(memory_space=pl.ANY)
    sba = pl.pallas_call(
        functools.partial(_attn_kernel, n_pairs=n_pairs),
        grid=(B, nq),
        in_specs=[pl.BlockSpec((rows_q, sw), lambda b, qi: (b * nq + qi, 0)), kv_spec, kv_spec,
                  _const_spec((bq, 2 * bq))],
        out_specs=pl.BlockSpec((rows_q, sw), lambda b, qi: (b * nq + qi, 0)),
        out_shape=jax.ShapeDtypeStruct((N, sw), BF16),
        scratch_shapes=([pltpu.VMEM((2, S, sw), BF16), pltpu.VMEM((2, S, sw), BF16),
                         pltpu.SemaphoreType.DMA((2,)), pltpu.SemaphoreType.DMA((2,))]
                        + [pltpu.VMEM((2 * bq, LANES), BF16)] * n_units + [pltpu.VMEM((2 * bq, LANES), F32)] * (2 * n_units)),
        compiler_params=cparams(dimension_semantics=("arbitrary", "arbitrary")),
        name="sb_attn",
    )(q2, k2, v2, tri)

    wr = jnp.zeros((D, LANES), F32).at[:, :n_exp].set(w_router)
    wr_hi = wr.astype(BF16)
    wr_lo = (wr - wr_hi.astype(F32)).astype(BF16)
    br = jnp.full((1, LANES), -jnp.inf, F32).at[0, :n_exp].set(b_router)
    ltri = (jnp.arange(tm)[:, None] > jnp.arange(tm)[None, :]).astype(BF16)
    assert N % (MOE_GROUPS * tm) == 0
    ng = N // MOE_GROUPS
    steps = ng // tm
    n_assign = ng * TOP_K
    n_blocks = -(-(n_assign + n_exp * (EXPERT_BLOCK - 1)) // EXPERT_BLOCK)
    n_rows = n_blocks * EXPERT_BLOCK
    plane_spec = lambda rows, index: pl.BlockSpec((PACK_ROWS, rows, LANES), lambda i, *_: (0, index(i, *_), 0))
    row_spec = lambda w: pl.BlockSpec((tm, w), lambda i: (i, 0))
    w_attn_up_bf, w_out_bf, g2 = w_attn_up.astype(BF16), w_out.astype(BF16), norm2_g.reshape(1, D)
    bgu, bdn = b_gate_up.reshape(n_exp, 1, 2 * de), b_down.reshape(n_exp, 1, D)
    out = None
    for grp in range(MOE_GROUPS):
        grp_spec = lambda w, first=grp * steps: pl.BlockSpec((tm, w), lambda i: (first + i, 0))

        cur = lambda i: jnp.minimum(i, steps - 1)
        prev = lambda i: jnp.maximum(i - 1, 0)
        in_spec = lambda w, first=grp * steps: pl.BlockSpec((tm, w), lambda i: (first + cur(i), 0))
        x1, hpk, ri, gate, cnt = pl.pallas_call(
            functools.partial(_mixer_out_kernel, tm=tm),
            grid=(steps + 1,),
            in_specs=[in_spec(D), in_spec(D), in_spec(sw), _const_spec((1, D)), _const_spec((D, D)),
                      _const_spec((sw, D)), _const_spec((D, D)),
                      _const_spec((1, D)), _const_spec((D, LANES)), _const_spec((D, LANES)), _const_spec((1, LANES)),
                      _const_spec((tm, tm))],
            out_specs=[pl.BlockSpec((tm, D), lambda i: (cur(i), 0)), plane_spec(tm, cur),
                       pl.BlockSpec((2 * TOP_K, tm), lambda i: (0, prev(i))),
                       pl.BlockSpec((tm, LANES), lambda i: (prev(i), 0)), _const_spec((8, LANES))],
            out_shape=[jax.ShapeDtypeStruct((ng, D), F32), jax.ShapeDtypeStruct((PACK_ROWS, ng, LANES), U32),
                       jax.ShapeDtypeStruct((2 * TOP_K, ng), I32), jax.ShapeDtypeStruct((ng, LANES), F32),
                       jax.ShapeDtypeStruct((8, LANES), F32)],
            scratch_shapes=[pltpu.VMEM((2, tm, LANES), F32)],
            compiler_params=cparams(dimension_semantics=("arbitrary",)),
            name="mixer_out",
        )(xf, pg, sba, g1, w_in_bf[:, n_in:], w_attn_up_bf, w_out_bf, g2, wr_hi, wr_lo, br, ltri)

        counts = cnt[0, :n_exp].astype(I32)
        padded = (counts + EXPERT_BLOCK - 1) // EXPERT_BLOCK * EXPERT_BLOCK
        padded_end = jnp.cumsum(padded)
        start_pad = padded_end - padded
        td = min(ng, 8192)
        dest = pl.pallas_call(
            _dest_kernel,
            grid_spec=pltpu.PrefetchScalarGridSpec(
                num_scalar_prefetch=1,
                grid=(ng // td,),
                in_specs=[pl.BlockSpec((2 * TOP_K, td), lambda i, sp: (0, i))],
                out_specs=pl.BlockSpec((TOP_K, td), lambda i, sp: (0, i)),
            ),
            out_shape=jax.ShapeDtypeStruct((TOP_K, ng), I32),
            compiler_params=cparams(dimension_semantics=("arbitrary",)),
            name="dest",
        )(start_pad, ri).reshape(-1, SC_CHUNK)
        block_start = jnp.arange(n_blocks, dtype=I32) * EXPERT_BLOCK
        block_expert = jnp.minimum(jnp.sum((padded_end[None, :] <= block_start[:, None]).astype(I32), axis=1),
                                   n_exp - 1)
        n_used = (padded_end[-1] // EXPERT_BLOCK).astype(I32).reshape(1)

        xs = _sc_scatter_rows(hpk, dest, n_rows)
        xs = pl.pallas_call(
            _padfill_kernel,
            grid_spec=pltpu.PrefetchScalarGridSpec(
                num_scalar_prefetch=2,
                grid=(1,),
                in_specs=[pl.BlockSpec(memory_space=pl.ANY)],
                out_specs=pl.BlockSpec(memory_space=pl.ANY),
                scratch_shapes=[pltpu.VMEM((PACK_ROWS, 1 << ((EXPERT_STEP - 1).bit_length() - 1), LANES), U32),
                                pltpu.SemaphoreType.DMA(())],
            ),
            out_shape=jax.ShapeDtypeStruct((PACK_ROWS, n_rows, LANES), U32),
            input_output_aliases={2: 0},
            compiler_params=cparams(dimension_semantics=("arbitrary",)),
            name="padfill",
        )(start_pad + counts, (-counts) % EXPERT_STEP, xs)

        experts = jnp.arange(n_exp, dtype=I32)
        used = counts > 0
        block_valid = jnp.clip(jnp.sum(jnp.where(block_expert[:, None] == experts[None, :],
                                                 (start_pad + counts)[None, :], 0), axis=1) - block_start,
                               0, EXPERT_BLOCK)
        ordinal = jnp.cumsum(used.astype(I32)) - 1
        next_used = jnp.min(jnp.where(jnp.logical_and(experts[None, :] > experts[:, None], used[None, :]),
                                      experts[None, :], n_exp), axis=1)
        next_used = jnp.where(next_used == n_exp, -1, next_used)
        per_block = lambda table: jnp.sum(jnp.where(block_expert[:, None] == experts[None, :], table[None, :], 0), axis=1)
        b_spec = lambda shape: pl.BlockSpec(shape, lambda i, be, *_: (be[i], 0, 0))
        ys = pl.pallas_call(
            _experts_kernel,
            grid_spec=pltpu.PrefetchScalarGridSpec(
                num_scalar_prefetch=5,
                grid=(n_blocks,),
                in_specs=[plane_spec(EXPERT_BLOCK, lambda i, be, nb, *_: jnp.minimum(i, nb[0] - 1)),
                          pl.BlockSpec(memory_space=pl.ANY), b_spec((1, 1, 2 * de)),
                          pl.BlockSpec(memory_space=pl.ANY), b_spec((1, 1, D))],
                out_specs=plane_spec(EXPERT_BLOCK, lambda i, *_: i),
                scratch_shapes=[pltpu.VMEM((2, D, 2 * de), F32), pltpu.VMEM((2, de, D), F32),
                                pltpu.VMEM((D, 2 * de), BF16), pltpu.VMEM((de, D), BF16),
                                pltpu.SemaphoreType.DMA((2,)), pltpu.SemaphoreType.DMA((2,))],
            ),
            out_shape=jax.ShapeDtypeStruct((PACK_ROWS, n_rows, LANES), U32),
            compiler_params=cparams(dimension_semantics=("arbitrary",)),
            name="experts",
        )(block_expert, n_used, block_valid, per_block(ordinal % 2), per_block(next_used), xs, w_gate_up, bgu, w_down, bdn)

        y4 = _sc_gather_rows(ys, dest.reshape(-1, SC_CHUNK))

        operands = [x1, gate, y4, y4, y4, y4] + ([] if out is None else [out])
        out = pl.pallas_call(
            _combine_kernel,
            grid=(steps,),
            in_specs=([row_spec(D), row_spec(LANES)]
                      + [plane_spec(tm, lambda i, kk=kk: kk * steps + i) for kk in range(TOP_K)]
                      + ([] if out is None else [pl.BlockSpec(memory_space=pl.ANY)])),
            out_specs=grp_spec(D),
            out_shape=jax.ShapeDtypeStruct((N, D), F32),
            input_output_aliases={} if out is None else {len(operands) - 1: 0},
            compiler_params=cparams(dimension_semantics=("arbitrary",)),
            name="combine",
        )(*operands)
    return out.reshape(B, S, D)


def kernel(x, norm1_g, w_in, q_norm_g, k_norm_g, w_pool_grp, pool_scale, w_pool_up, w_attn_up, w_out, norm2_g,
           w_router, b_router, w_gate_up, b_gate_up, w_down, b_down):
    for layer in range(norm1_g.shape[0]):
        x = _layer(x, norm1_g[layer], w_in[layer], q_norm_g[layer], k_norm_g[layer], w_pool_grp[layer],
                   pool_scale[layer], w_pool_up[layer], w_attn_up[layer], w_out[layer], norm2_g[layer],
                   w_router[layer], b_router[layer], w_gate_up[layer], b_gate_up[layer], w_down[layer],
                   b_down[layer])
    return x
```

```python
import functools

import jax
import jax.numpy as jnp
from jax import lax
from jax.experimental import pallas as pl
from jax.experimental.pallas import tpu as pltpu
from jax.experimental.pallas import tpu_sc as plsc

F32 = jnp.float32
BF16 = jnp.bfloat16
U32 = jnp.uint32
I32 = jnp.int32

EPS = 1e-6
POOL_WINDOWS = (2, 4, 8, 16)
POOL_GROUP_DIM = 128
POOL_HALO = 16
SB_HEAD_DIM = 64
TOP_K = 4
SWIGLU_LIMIT = 7.0
SWIGLU_ALPHA = 1.702
EXPERT_BLOCK = 1024
EXPERT_SPLIT = 8
EXPERT_STEP = EXPERT_BLOCK // EXPERT_SPLIT
LANES = 128
PACK_ROWS = 4
ATTN_BLOCK = 128
ATTN_SUB = 2
ATTN_EXIT_BITS = 70.0
LOG2_E = 1.4426950408889634
VMEM_LIMIT = 56 * 1024 * 1024
SC_CHUNK = 128
MOE_GROUPS = 2


def _dot(a, b):
    return jnp.dot(a, b, preferred_element_type=F32)


def _split_bf16(x):
    hi = x.astype(BF16)
    lo = (x - hi.astype(F32)).astype(BF16)
    return hi, lo


def _pack_rows(v, out_ref):
    half = v.shape[1] // 2
    lo = lax.bitcast_convert_type(v[:, :half].astype(BF16).astype(F32), U32) >> 16
    hi = lax.bitcast_convert_type(v[:, half:].astype(BF16).astype(F32), U32) & jnp.uint32(0xFFFF0000)
    w = lo | hi
    for c in range(PACK_ROWS):
        out_ref[c] = w[:, c * LANES:(c + 1) * LANES]


def _unpack_rows(ref):
    los, his = [], []
    for c in range(PACK_ROWS):
        w = ref[c]
        los.append(lax.bitcast_convert_type(w << 16, F32))
        his.append(lax.bitcast_convert_type(w & jnp.uint32(0xFFFF0000), F32))
    return jnp.concatenate(los + his, axis=1)


def _mixer_in_kernel(x_ref, g1_ref, win_ref, gq_ref, gk_ref, hsum_ref, wgrp_ref, pscale_ref, wpu_ref,
                     q_ref, k_ref, v_ref, p_ref, tail_ref, *, tm, pw, sw):
    i = pl.program_id(1)

    @pl.when(i == 0)
    def _():
        tail_ref[...] = jnp.zeros_like(tail_ref)

    x = x_ref[...]
    ms = jnp.mean(x * x, axis=-1, keepdims=True)
    h = (x * lax.rsqrt(ms + EPS) * g1_ref[...]).astype(BF16)

    def head_norm(t, gain):
        ss = _dot((t * t).astype(BF16), hsum_ref[...])
        return t * lax.rsqrt(ss * (1.0 / SB_HEAD_DIM) + EPS) * gain

    def project_q():
        q = _dot(h, win_ref[:, pw:pw + sw])
        q_ref[...] = (head_norm(q, gq_ref[...]) * (SB_HEAD_DIM ** -0.5 * LOG2_E)).astype(BF16)

    def project_k():
        k = _dot(h, win_ref[:, pw + sw:pw + 2 * sw])
        k_ref[...] = head_norm(k, gk_ref[...]).astype(BF16)

    def project_v():
        v_ref[...] = _dot(h, win_ref[:, pw + 2 * sw:pw + 3 * sw]).astype(BF16)

    d_model = x.shape[1]
    pool_gate = []

    def project_pool_gate():
        pool_gate.append(jax.nn.sigmoid(_dot(h, win_ref[:, pw + 3 * sw:pw + 3 * sw + d_model])))

    u = _dot(h, win_ref[:, 0:pw])
    xx = jnp.concatenate([tail_ref[...], u], axis=0)
    tail_ref[...] = u[tm - POOL_HALO:, :]
    pos = i * tm + lax.broadcasted_iota(I32, (tm, POOL_GROUP_DIM), 0)
    mixed = []
    for (g, w), project in zip(enumerate(POOL_WINDOWS), (project_q, project_k, project_v, project_pool_gate)):
        project()
        s = xx[:, g * POOL_GROUP_DIM:(g + 1) * POOL_GROUP_DIM]
        step = 1
        while step < w:
            s = s + pltpu.roll(s, step, axis=0)
            step *= 2
        count = jnp.minimum(pos + 1, w).astype(F32)
        ug = u[:, g * POOL_GROUP_DIM:(g + 1) * POOL_GROUP_DIM]
        d = s[POOL_HALO:, :] / count - ug
        mixed.append(_dot(d.astype(BF16), wgrp_ref[g]))
    pm = jnp.concatenate(mixed, axis=1) * pscale_ref[...]
    pool_out = _dot(pm.astype(BF16), wpu_ref[...])
    p_ref[...] = (pool_gate[0] * pool_out).astype(BF16)


def _attn_kernel(q_ref, k_hbm, v_hbm, tri_ref, o_ref, kbuf, vbuf, ksem, vsem, *scratch, n_pairs):
    units = [(sub, p) for sub in range(ATTN_SUB) for p in range(n_pairs)]
    n_units = len(units)
    qs, acc, rr = scratch[:n_units], scratch[n_units:2 * n_units], scratch[2 * n_units:]
    bq = ATTN_BLOCK
    batch = pl.program_id(0)
    seq = kbuf.shape[1]

    def kv_copies(b):
        rows = pl.ds(pl.multiple_of(b * seq, seq), seq)
        return (pltpu.make_async_copy(k_hbm.at[rows], kbuf.at[b % 2], ksem.at[b % 2]),
                pltpu.make_async_copy(v_hbm.at[rows], vbuf.at[b % 2], vsem.at[b % 2]))

    @pl.when(pl.program_id(1) == 0)
    def _():
        @pl.when(batch == 0)
        def _():
            for c in kv_copies(batch):
                c.start()

        for c in kv_copies(batch):
            c.wait()

        @pl.when(batch + 1 < pl.num_programs(0))
        def _():
            for c in kv_copies(batch + 1):
                c.start()

    k_ref = kbuf.at[batch % 2]
    v_ref = vbuf.at[batch % 2]
    first_block = pl.program_id(1) * ATTN_SUB
    first_head = lax.broadcasted_iota(I32, (bq, LANES), 1) < SB_HEAD_DIM
    for u, (sub, p) in enumerate(units):
        q2 = q_ref[sub * bq:(sub + 1) * bq, p * LANES:(p + 1) * LANES]
        qs[u][:bq] = jnp.where(first_head, q2, jnp.zeros_like(q2))
        qs[u][bq:] = jnp.where(first_head, jnp.zeros_like(q2), q2)
    row = lax.broadcasted_iota(I32, (2 * bq, bq), 0)
    col = lax.broadcasted_iota(I32, (2 * bq, bq), 1)
    causal = col < (row & (bq - 1))
    contract_last = (((1,), (1,)), ((), ()))

    def softplus(z):
        return jnp.maximum(z, 0.0) + jnp.log2(1.0 + jnp.exp2(-jnp.abs(z)))

    def suffix_sums(sp):
        return _dot(sp.astype(BF16), tri_ref[...])

    def cols(ref, block, p):
        start = pl.multiple_of(jnp.maximum(block, 0) * bq, bq)
        return ref[pl.ds(start, bq), p * LANES:(p + 1) * LANES]

    def scores(u, block):
        return lax.dot_general(qs[u][...], cols(k_ref, block, units[u][1]), contract_last, preferred_element_type=F32)

    diag = [first_block + sub for sub, _ in units]
    z_d, z_n, s_d, s_n = {}, {}, {}, {}

    def stage_scores(u):
        z_d[u] = scores(u, diag[u])
        z_n[u] = scores(u, diag[u] - 1)

    def stage_sums(u):
        s_d[u] = suffix_sums(jnp.where(causal, softplus(z_d[u]), 0.0))
        s_n[u] = suffix_sums(jnp.where(diag[u] >= 1, softplus(z_n[u]), 0.0))

    def stage_values(u):
        p = units[u][1]
        a_d = jnp.where(causal, jnp.exp2(z_d[u] - s_d[u][:, :bq]), 0.0)
        r_d = s_d[u][:, bq:]
        a_n = jnp.where(diag[u] >= 1, jnp.exp2(z_n[u] - (r_d + s_n[u][:, :bq])), 0.0)
        acc[u][...] = (_dot(a_d.astype(BF16), cols(v_ref, diag[u], p)) + _dot(a_n.astype(BF16), cols(v_ref, diag[u] - 1, p)))
        rr[u][...] = r_d + s_n[u][:, bq:]

    for t in range(n_units + 2):
        if t < n_units:
            stage_scores(t)
        if 0 <= t - 1 < n_units:
            stage_sums(t - 1)
        if 0 <= t - 2 < n_units:
            stage_values(t - 2)

    def r_min():
        m = rr[0][...]
        for u in range(1, n_units):
            m = jnp.minimum(m, rr[u][...])
        return jnp.min(m)

    def cond(c):
        back, rm = c
        return jnp.logical_and(diag[-1] - back >= 0, rm < ATTN_EXIT_BITS)

    def body(c):
        back, _ = c
        blocks = [d - back for d in diag]
        zs = [scores(u, blocks[u]) for u in range(n_units)]
        ss = [suffix_sums(jnp.where(blocks[u] >= 0, softplus(zs[u]), 0.0)) for u in range(n_units)]
        for u, (_, p) in enumerate(units):
            r = rr[u][...]
            a = jnp.where(blocks[u] >= 0, jnp.exp2(zs[u] - (r + ss[u][:, :bq])), 0.0)
            acc[u][...] += _dot(a.astype(BF16), cols(v_ref, blocks[u], p))
            rr[u][...] = r + ss[u][:, bq:]
        return back + 1, r_min()

    lax.while_loop(cond, body, (2, r_min()))
    for u, (sub, p) in enumerate(units):
        o_ref[sub * bq:(sub + 1) * bq, p * LANES:(p + 1) * LANES] = (
            jnp.where(first_head, acc[u][:bq], acc[u][bq:]).astype(BF16))


def _mixer_out_kernel(x_ref, p_ref, sba_ref, g1_ref, wga_ref, wau_ref, wout_ref, g2_ref, wr_hi_ref, wr_lo_ref, br_ref,
                      ltri_ref, x1_ref, hp_ref, ri_ref, gate_ref, cnt_ref, logit_ref, *, tm):
    step = pl.program_id(0)

    @pl.when(step == 0)
    def _():
        cnt_ref[...] = jnp.zeros_like(cnt_ref)
        logit_ref[1] = jnp.zeros((tm, LANES), F32)

    logits = logit_ref[(step + 1) % 2]
    routed = jnp.where(step >= 1, 1.0, 0.0)
    lane = lax.broadcasted_iota(I32, logits.shape, 1).astype(F32)
    work = logits
    vals, idxs = [], []

    def topk_round():
        nonlocal work
        m = jnp.max(work, axis=-1, keepdims=True)
        ik = jnp.min(jnp.where(work == m, lane, float(LANES)), axis=-1, keepdims=True)
        vals.append(m)
        idxs.append(ik)
        work = jnp.where(lane == ik, -jnp.inf, work)

    d_model = x_ref.shape[1]
    quarter = d_model // TOP_K
    topk_round()
    x = x_ref[...]
    h1 = (x * lax.rsqrt(jnp.mean(x * x, axis=-1, keepdims=True) + EPS) * g1_ref[...]).astype(BF16)
    attn_gate = jax.nn.sigmoid(_dot(h1, wga_ref[...]))
    attn_out = _dot(sba_ref[...], wau_ref[...])
    merged = (p_ref[...].astype(F32) + attn_gate * attn_out).astype(BF16)
    x1_parts = []
    for c in range(TOP_K):
        if c >= 1:
            topk_round()
        cols = slice(c * quarter, (c + 1) * quarter)
        x1_parts.append(x[:, cols] + _dot(merged, wout_ref[:, cols]))
    x1 = jnp.concatenate(x1_parts, axis=1)
    x1_ref[...] = x1

    es = [jnp.exp(v - vals[0]) for v in vals]
    denom = es[0] + es[1] + es[2] + es[3]
    hot = jnp.zeros(logits.shape, F32)
    for ik in idxs:
        hot = hot + jnp.where(lane == ik, routed, 0.0)
    before = _dot(ltri_ref[...], hot.astype(BF16)) + cnt_ref[0:1, :]

    ms = jnp.mean(x1 * x1, axis=-1, keepdims=True)
    h2 = x1 * lax.rsqrt(ms + EPS) * g2_ref[...]
    _pack_rows(h2, hp_ref)

    ri = jnp.zeros(logits.shape, F32)
    gt = jnp.zeros(logits.shape, F32)
    for kk in range(TOP_K):
        rank = jnp.sum(jnp.where(lane == idxs[kk], before, 0.0), axis=-1, keepdims=True)
        ri = jnp.where(lane == kk, idxs[kk], ri)
        ri = jnp.where(lane == TOP_K + kk, rank, ri)
        gt = jnp.where(lane == kk, es[kk] / denom, gt)
    ri_ref[...] = ri.T[:2 * TOP_K].astype(I32)
    gate_ref[...] = gt
    cnt_ref[...] = cnt_ref[...] + jnp.sum(hot, axis=0, keepdims=True)

    h_hi, h_lo = _split_bf16(h2)
    logit_ref[step % 2] = (_dot(h_hi, wr_hi_ref[...]) + _dot(h_hi, wr_lo_ref[...]) + _dot(h_lo, wr_hi_ref[...])
                           + br_ref[...])


def _dest_kernel(start_ref, ri_ref, dest_ref):
    idx = ri_ref[:TOP_K, :]
    dest = ri_ref[TOP_K:, :]
    for e in range(start_ref.shape[0]):
        dest = dest + jnp.where(idx == e, start_ref[e], 0)
    dest_ref[...] = dest


def _sc_mesh():
    info = plsc.get_sparse_core_info()
    mesh = plsc.VectorSubcoreMesh(core_axis_name="c", subcore_axis_name="s")
    return mesh, info.num_cores, info.num_subcores, info.num_lanes


def _sc_scatter_rows(rows, idx, n_out):
    mesh, nc, ns, lanes = _sc_mesh()
    nw = nc * ns
    planes, m, width = rows.shape
    ch = SC_CHUNK
    n_ch = m // ch // nw
    half = planes // 2
    assert planes % 2 == 0 and n_ch * ch * nw == m and idx.shape == (m // ch * TOP_K, ch) and n_ch >= 2
    assert n_ch % 8 == 0

    @functools.partial(
        pl.kernel, mesh=mesh, out_type=jax.ShapeDtypeStruct((planes * n_out, width), rows.dtype),
        scratch_types=([pltpu.VMEM((n_ch * TOP_K, ch), I32)] + [pltpu.VMEM((ch, width), rows.dtype)] * planes
                       + [pltpu.VMEM((TOP_K, ch), I32)] * planes
                       + [pltpu.SemaphoreType.DMA((planes,)), pltpu.SemaphoreType.DMA((planes,))]))
    def scatter_kernel(rows_hbm, idx_hbm, out_hbm, idx_v, *rest):
        bufs, ibufs, rsem, ssem = rest[:planes], rest[planes:2 * planes], rest[2 * planes], rest[2 * planes + 1]
        wid = lax.axis_index("s") * nc + lax.axis_index("c")
        for kk in range(TOP_K):
            pltpu.sync_copy(idx_hbm.at[pl.ds(kk * (m // ch) + wid * n_ch, n_ch)], idx_v.at[pl.ds(kk * n_ch, n_ch)])
        base = wid * n_ch * ch

        def read(j, b):
            return pltpu.make_async_copy(rows_hbm.at[b, pl.ds(pl.multiple_of(base + j * ch, ch), ch)], bufs[b], rsem.at[b])

        def scatters(b):
            return [pltpu.make_async_copy(bufs[b], out_hbm.at[ibufs[b].at[kk]], ssem.at[b]) for kk in range(TOP_K)]

        def start_scatters(j, b):
            for kk in range(TOP_K):
                for t in range(0, ch, lanes):
                    ibufs[b][kk, pl.ds(t, lanes)] = idx_v[kk * n_ch + j, pl.ds(t, lanes)] + b * n_out
            for c in scatters(b):
                c.start()

        def wait_scatters(b):
            for c in scatters(b):
                c.wait()

        def finish(j, b):
            pj, pb = (j, b - half) if b >= half else (j - 1, b + half)
            read(pj, pb).wait()
            start_scatters(pj, pb)

        for b in range(planes):
            read(0, b).start()
        for b in range(half, planes):
            finish(0, b)

        @pl.loop(1, n_ch)
        def _(j):
            for b in range(planes):
                wait_scatters(b)
                read(j, b).start()
                finish(j, b)

        for b in range(half):
            finish(n_ch, b)
        for b in range(planes):
            wait_scatters(b)

    return scatter_kernel(rows, idx).reshape(planes, n_out, width)


def _padfill_kernel(start_ref, len_ref, xs_in_ref, xs_ref, zeros_ref, sem):
    del xs_in_ref
    zeros_ref[...] = jnp.zeros_like(zeros_ref)
    bits = [1 << b for b in reversed(range((EXPERT_STEP - 1).bit_length()))]

    def pieces(e):
        n = len_ref[e]
        for bit in bits:
            row0 = start_ref[e] + (n & ~(2 * bit - 1))
            copy = pltpu.make_async_copy(zeros_ref.at[:, pl.ds(0, bit)], xs_ref.at[:, pl.ds(row0, bit)], sem)
            yield (n & bit) != 0, copy

    def start(e, c):
        for on, copy in pieces(e):
            pl.when(on)(copy.start)
        return c

    def wait(e, c):
        for on, copy in pieces(e):
            pl.when(on)(copy.wait)
        return c

    lax.fori_loop(0, start_ref.shape[0], start, 0)
    lax.fori_loop(0, start_ref.shape[0], wait, 0)


def _experts_kernel(be_ref, nb_ref, valid_ref, slot_ref, next_ref, xs_ref, wgu_hbm, bgu_ref, wd_hbm, bd_ref, ys_ref,
                    wgu_buf, wd_buf, wgu_bf_ref, wd_bf_ref, gu_sem, d_sem):
    del nb_ref
    blk = pl.program_id(0)
    de = wd_hbm.shape[1]
    valid = valid_ref[blk]
    expert = be_ref[blk]
    slot = slot_ref[blk]

    def weight_copies(e, s):
        return (pltpu.make_async_copy(wgu_hbm.at[e], wgu_buf.at[s], gu_sem.at[s]),
                pltpu.make_async_copy(wd_hbm.at[e], wd_buf.at[s], d_sem.at[s]))

    first_of_expert = jnp.logical_or(blk == 0, expert != be_ref[jnp.maximum(blk - 1, 0)])

    @pl.when(jnp.logical_and(valid > 0, first_of_expert))
    def _():
        @pl.when(blk == 0)
        def _():
            for c in weight_copies(expert, slot):
                c.start()

        for c in weight_copies(expert, slot):
            c.wait()
        wgu_bf_ref[...] = wgu_buf[slot].astype(BF16)
        wd_bf_ref[...] = wd_buf[slot].astype(BF16)

        @pl.when(next_ref[blk] >= 0)
        def _():
            for c in weight_copies(next_ref[blk], 1 - slot):
                c.start()

    def mlp(m):
        x = _unpack_rows(xs_ref.at[:, pl.ds(0, m)]).astype(BF16)
        gu = _dot(x, wgu_bf_ref[...]) + bgu_ref[0]
        glu = jnp.minimum(gu[:, :de], SWIGLU_LIMIT)
        lin = jnp.clip(gu[:, de:], -SWIGLU_LIMIT, SWIGLU_LIMIT)
        act = glu * jax.nn.sigmoid(SWIGLU_ALPHA * glu) * (lin + 1.0)
        y = _dot(act.astype(BF16), wd_bf_ref[...]) + bd_ref[0]
        _pack_rows(y, ys_ref.at[:, pl.ds(0, m)])
        if m < EXPERT_BLOCK:
            ys_ref[:, m:, :] = jnp.zeros((PACK_ROWS, EXPERT_BLOCK - m, LANES), U32)

    for q in range(EXPERT_SPLIT):
        pl.when(jnp.logical_and(valid > q * EXPERT_STEP, valid <= (q + 1) * EXPERT_STEP))(
            functools.partial(mlp, (q + 1) * EXPERT_STEP))


def _sc_gather_rows(table, idx):
    mesh, nc, ns, lanes = _sc_mesh()
    nw = nc * ns
    planes, n_tab, width = table.shape
    n_idx_rows, ch = idx.shape
    n_ch = n_idx_rows // nw
    m = n_idx_rows * ch
    half = planes // 2
    assert planes % 2 == 0 and ch == SC_CHUNK and n_ch * nw == n_idx_rows and n_ch >= 2

    @functools.partial(
        pl.kernel, mesh=mesh, out_type=jax.ShapeDtypeStruct((planes, m, width), table.dtype),
        scratch_types=([pltpu.VMEM((n_ch, ch), I32)] + [pltpu.VMEM((ch, width), table.dtype)] * planes
                       + [pltpu.VMEM((8, ch), I32)] * planes
                       + [pltpu.SemaphoreType.DMA((planes,)), pltpu.SemaphoreType.DMA((planes,))]))
    def gather_kernel(table_hbm, idx_hbm, out_hbm, idx_v, *rest):
        bufs, ibufs, gsem, wsem = rest[:planes], rest[planes:2 * planes], rest[2 * planes], rest[2 * planes + 1]
        wid = lax.axis_index("s") * nc + lax.axis_index("c")
        pltpu.sync_copy(idx_hbm.at[pl.ds(wid * n_ch, n_ch)], idx_v)
        base = wid * n_ch * ch

        def gather(b):
            return pltpu.make_async_copy(table_hbm.at[ibufs[b].at[0]], bufs[b], gsem.at[b])

        def start_gather(j, b):
            for t in range(0, ch, lanes):
                ibufs[b][0, pl.ds(t, lanes)] = idx_v[j, pl.ds(t, lanes)] + b * n_tab
            gather(b).start()

        def write(j, b):
            return pltpu.make_async_copy(bufs[b], out_hbm.at[b, pl.ds(pl.multiple_of(base + j * ch, ch), ch)], wsem.at[b])

        def finish(j, b):
            pj, pb = (j, b - half) if b >= half else (j - 1, b + half)
            gather(pb).wait()
            write(pj, pb).start()

        for b in range(planes):
            start_gather(0, b)
        for b in range(half, planes):
            finish(0, b)

        @pl.loop(1, n_ch)
        def _(j):
            for b in range(planes):
                write(j - 1, b).wait()
                start_gather(j, b)
                finish(j, b)

        for b in range(half):
            finish(n_ch, b)
        for b in range(planes):
            write(n_ch - 1, b).wait()

    return gather_kernel(table.reshape(planes * n_tab, width), idx)


def _combine_kernel(x1_ref, gate_ref, y0_ref, y1_ref, y2_ref, y3_ref, *rest):
    o_ref = rest[-1]
    gate = gate_ref[...]
    out = x1_ref[...]
    for kk, y_ref in enumerate((y0_ref, y1_ref, y2_ref, y3_ref)):
        out = out + gate[:, kk:kk + 1] * _unpack_rows(y_ref)
    o_ref[...] = out


def _const_spec(shape):
    nd = len(shape)
    return pl.BlockSpec(shape, lambda *_: (0,) * nd)


def _layer(x, norm1_g, w_in, q_norm_g, k_norm_g, w_pool_grp, pool_scale, w_pool_up, w_attn_up, w_out, norm2_g,
           w_router, b_router, w_gate_up, b_gate_up, w_down, b_down):
    B, S, D = x.shape
    N = B * S
    pw = w_pool_up.shape[0]
    sw = w_attn_up.shape[0]
    n_exp = w_router.shape[1]
    de = w_down.shape[1]
    heads = sw // SB_HEAD_DIM
    assert pw == len(POOL_WINDOWS) * POOL_GROUP_DIM and heads % 2 == 0 and n_exp <= LANES
    assert D == 2 * PACK_ROWS * LANES and w_in.shape[1] == pw + 3 * sw + 2 * D
    tm = 512 if S % 512 == 0 else 256
    assert S % tm == 0 and S % (ATTN_SUB * ATTN_BLOCK) == 0
    xf = x.reshape(N, D)
    cparams = functools.partial(pltpu.CompilerParams, vmem_limit_bytes=VMEM_LIMIT)

    hsum = (jnp.arange(sw)[:, None] // SB_HEAD_DIM == jnp.arange(sw)[None, :] // SB_HEAD_DIM).astype(BF16)
    tm_in = 2 * tm if S % (2 * tm) == 0 else tm
    nt = S // tm_in
    tok_spec = lambda w: pl.BlockSpec((tm_in, w), lambda b, i: (b * nt + i, 0))
    w_in_bf = w_in.astype(BF16)
    n_in = pw + 3 * sw + D
    g1 = norm1_g.reshape(1, D)
    q2, k2, v2, pg = pl.pallas_call(
        functools.partial(_mixer_in_kernel, tm=tm_in, pw=pw, sw=sw),
        grid=(B, nt),
        in_specs=[tok_spec(D), _const_spec((1, D)), _const_spec((D, n_in)), _const_spec((1, sw)),
                  _const_spec((1, sw)), _const_spec((sw, sw)),
                  _const_spec((len(POOL_WINDOWS), POOL_GROUP_DIM, POOL_GROUP_DIM)), _const_spec((1, pw)),
                  _const_spec((pw, D))],
        out_specs=[tok_spec(sw), tok_spec(sw), tok_spec(sw), tok_spec(D)],
        out_shape=[jax.ShapeDtypeStruct((N, sw), BF16)] * 3 + [jax.ShapeDtypeStruct((N, D), BF16)],
        scratch_shapes=[pltpu.VMEM((POOL_HALO, pw), F32)],
        compiler_params=cparams(dimension_semantics=("arbitrary", "arbitrary")),
        name="mixer_in",
    )(xf, g1, w_in_bf[:, :n_in], jnp.tile(q_norm_g, heads).reshape(1, sw),
      jnp.tile(k_norm_g, heads).reshape(1, sw), hsum, w_pool_grp.astype(BF16), pool_scale.reshape(1, pw),
      w_pool_up.astype(BF16))

    bq = ATTN_BLOCK
    rows_q = ATTN_SUB * bq
    nq = S // rows_q
    jj = jnp.arange(bq)
    tri = jnp.concatenate([(jj[:, None] >= jj[None, :]).astype(BF16), jnp.ones((bq, bq), BF16)], axis=1)
    n_pairs = heads // 2
    n_units = ATTN_SUB * n_pairs
    kv_spec = pl.BlockSpec(memory_space=pl.ANY)
    sba = pl.pallas_call(
        functools.partial(_attn_kernel, n_pairs=n_pairs),
        grid=(B, nq),
        in_specs=[pl.BlockSpec((rows_q, sw), lambda b, qi: (b * nq + qi, 0)), kv_spec, kv_spec,
                  _const_spec((bq, 2 * bq))],
        out_specs=pl.BlockSpec((rows_q, sw), lambda b, qi: (b * nq + qi, 0)),
        out_shape=jax.ShapeDtypeStruct((N, sw), BF16),
        scratch_shapes=([pltpu.VMEM((2, S, sw), BF16), pltpu.VMEM((2, S, sw), BF16),
                         pltpu.SemaphoreType.DMA((2,)), pltpu.SemaphoreType.DMA((2,))]
                        + [pltpu.VMEM((2 * bq, LANES), BF16)] * n_units + [pltpu.VMEM((2 * bq, LANES), F32)] * (2 * n_units)),
        compiler_params=cparams(dimension_semantics=("arbitrary", "arbitrary")),
        name="sb_attn",
    )(q2, k2, v2, tri)

    wr = jnp.zeros((D, LANES), F32).at[:, :n_exp].set(w_router)
    wr_hi = wr.astype(BF16)
    wr_lo = (wr - wr_hi.astype(F32)).astype(BF16)
    br = jnp.full((1, LANES), -jnp.inf, F32).at[0, :n_exp].set(b_router)
    ltri = (jnp.arange(tm)[:, None] > jnp.arange(tm)[None, :]).astype(BF16)
    assert N % (MOE_GROUPS * tm) == 0
    ng = N // MOE_GROUPS
    steps = ng // tm
    n_assign = ng * TOP_K
    n_blocks = -(-(n_assign + n_exp * (EXPERT_BLOCK - 1)) // EXPERT_BLOCK)
    n_rows = n_blocks * EXPERT_BLOCK
    plane_spec = lambda rows, index: pl.BlockSpec((PACK_ROWS, rows, LANES), lambda i, *_: (0, index(i, *_), 0))
    row_spec = lambda w: pl.BlockSpec((tm, w), lambda i: (i, 0))
    w_attn_up_bf, w_out_bf, g2 = w_attn_up.astype(BF16), w_out.astype(BF16), norm2_g.reshape(1, D)
    bgu, bdn = b_gate_up.reshape(n_exp, 1, 2 * de), b_down.reshape(n_exp, 1, D)
    out = None
    for grp in range(MOE_GROUPS):
        grp_spec = lambda w, first=grp * steps: pl.BlockSpec((tm, w), lambda i: (first + i, 0))

        cur = lambda i: jnp.minimum(i, steps - 1)
        prev = lambda i: jnp.maximum(i - 1, 0)
        in_spec = lambda w, first=grp * steps: pl.BlockSpec((tm, w), lambda i: (first + cur(i), 0))
        x1, hpk, ri, gate, cnt = pl.pallas_call(
            functools.partial(_mixer_out_kernel, tm=tm),
            grid=(steps + 1,),
            in_specs=[in_spec(D), in_spec(D), in_spec(sw), _const_spec((1, D)), _const_spec((D, D)),
                      _const_spec((sw, D)), _const_spec((D, D)),
                      _const_spec((1, D)), _const_spec((D, LANES)), _const_spec((D, LANES)), _const_spec((1, LANES)),
                      _const_spec((tm, tm))],
            out_specs=[pl.BlockSpec((tm, D), lambda i: (cur(i), 0)), plane_spec(tm, cur),
                       pl.BlockSpec((2 * TOP_K, tm), lambda i: (0, prev(i))),
                       pl.BlockSpec((tm, LANES), lambda i: (prev(i), 0)), _const_spec((8, LANES))],
            out_shape=[jax.ShapeDtypeStruct((ng, D), F32), jax.ShapeDtypeStruct((PACK_ROWS, ng, LANES), U32),
                       jax.ShapeDtypeStruct((2 * TOP_K, ng), I32), jax.ShapeDtypeStruct((ng, LANES), F32),
                       jax.ShapeDtypeStruct((8, LANES), F32)],
            scratch_shapes=[pltpu.VMEM((2, tm, LANES), F32)],
            compiler_params=cparams(dimension_semantics=("arbitrary",)),
            name="mixer_out",
        )(xf, pg, sba, g1, w_in_bf[:, n_in:], w_attn_up_bf, w_out_bf, g2, wr_hi, wr_lo, br, ltri)

        counts = cnt[0, :n_exp].astype(I32)
        padded = (counts + EXPERT_BLOCK - 1) // EXPERT_BLOCK * EXPERT_BLOCK
        padded_end = jnp.cumsum(padded)
        start_pad = padded_end - padded
        td = min(ng, 8192)
        dest = pl.pallas_call(
            _dest_kernel,
            grid_spec=pltpu.PrefetchScalarGridSpec(
                num_scalar_prefetch=1,
                grid=(ng // td,),
                in_specs=[pl.BlockSpec((2 * TOP_K, td), lambda i, sp: (0, i))],
                out_specs=pl.BlockSpec((TOP_K, td), lambda i, sp: (0, i)),
            ),
            out_shape=jax.ShapeDtypeStruct((TOP_K, ng), I32),
            compiler_params=cparams(dimension_semantics=("arbitrary",)),
            name="dest",
        )(start_pad, ri).reshape(-1, SC_CHUNK)
        block_start = jnp.arange(n_blocks, dtype=I32) * EXPERT_BLOCK
        block_expert = jnp.minimum(jnp.sum((padded_end[None, :] <= block_start[:, None]).astype(I32), axis=1),
                                   n_exp - 1)
        n_used = (padded_end[-1] // EXPERT_BLOCK).astype(I32).reshape(1)

        xs = _sc_scatter_rows(hpk, dest, n_rows)
        xs = pl.pallas_call(
            _padfill_kernel,
            grid_spec=pltpu.PrefetchScalarGridSpec(
                num_scalar_prefetch=2,
                grid=(1,),
                in_specs=[pl.BlockSpec(memory_space=pl.ANY)],
                out_specs=pl.BlockSpec(memory_space=pl.ANY),
                scratch_shapes=[pltpu.VMEM((PACK_ROWS, 1 << ((EXPERT_STEP - 1).bit_length() - 1), LANES), U32),
                                pltpu.SemaphoreType.DMA(())],
            ),
            out_shape=jax.ShapeDtypeStruct((PACK_ROWS, n_rows, LANES), U32),
            input_output_aliases={2: 0},
            compiler_params=cparams(dimension_semantics=("arbitrary",)),
            name="padfill",
        )(start_pad + counts, (-counts) % EXPERT_STEP, xs)

        experts = jnp.arange(n_exp, dtype=I32)
        used = counts > 0
        block_valid = jnp.clip(jnp.sum(jnp.where(block_expert[:, None] == experts[None, :],
                                                 (start_pad + counts)[None, :], 0), axis=1) - block_start,
                               0, EXPERT_BLOCK)
        ordinal = jnp.cumsum(used.astype(I32)) - 1
        next_used = jnp.min(jnp.where(jnp.logical_and(experts[None, :] > experts[:, None], used[None, :]),
                                      experts[None, :], n_exp), axis=1)
        next_used = jnp.where(next_used == n_exp, -1, next_used)
        per_block = lambda table: jnp.sum(jnp.where(block_expert[:, None] == experts[None, :], table[None, :], 0), axis=1)
        b_spec = lambda shape: pl.BlockSpec(shape, lambda i, be, *_: (be[i], 0, 0))
        ys = pl.pallas_call(
            _experts_kernel,
            grid_spec=pltpu.PrefetchScalarGridSpec(
                num_scalar_prefetch=5,
                grid=(n_blocks,),
                in_specs=[plane_spec(EXPERT_BLOCK, lambda i, be, nb, *_: jnp.minimum(i, nb[0] - 1)),
                          pl.BlockSpec(memory_space=pl.ANY), b_spec((1, 1, 2 * de)),
                          pl.BlockSpec(memory_space=pl.ANY), b_spec((1, 1, D))],
                out_specs=plane_spec(EXPERT_BLOCK, lambda i, be, nb, *_: jnp.minimum(i, nb[0] - 1)),
                scratch_shapes=[pltpu.VMEM((2, D, 2 * de), F32), pltpu.VMEM((2, de, D), F32),
                                pltpu.VMEM((D, 2 * de), BF16), pltpu.VMEM((de, D), BF16),
                                pltpu.SemaphoreType.DMA((2,)), pltpu.SemaphoreType.DMA((2,))],
            ),
            out_shape=jax.ShapeDtypeStruct((PACK_ROWS, n_rows, LANES), U32),
            compiler_params=cparams(dimension_semantics=("arbitrary",)),
            name="experts",
        )(block_expert, n_used, block_valid, per_block(ordinal % 2), per_block(next_used), xs, w_gate_up, bgu, w_down, bdn)

        y4 = _sc_gather_rows(ys, dest.reshape(-1, SC_CHUNK))

        operands = [x1, gate, y4, y4, y4, y4] + ([] if out is None else [out])
        out = pl.pallas_call(
            _combine_kernel,
            grid=(steps,),
            in_specs=([row_spec(D), row_spec(LANES)]
                      + [plane_spec(tm, lambda i, kk=kk: kk * steps + i) for kk in range(TOP_K)]
                      + ([] if out is None else [pl.BlockSpec(memory_space=pl.ANY)])),
            out_specs=grp_spec(D),
            out_shape=jax.ShapeDtypeStruct((N, D), F32),
            input_output_aliases={} if out is None else {len(operands) - 1: 0},
            compiler_params=cparams(dimension_semantics=("arbitrary",)),
            name="combine",
        )(*operands)
    return out.reshape(B, S, D)


def kernel(x, norm1_g, w_in, q_norm_g, k_norm_g, w_pool_grp, pool_scale, w_pool_up, w_attn_up, w_out, norm2_g,
           w_router, b_router, w_gate_up, b_gate_up, w_down, b_down):
    for layer in range(norm1_g.shape[0]):
        x = _layer(x, norm1_g[layer], w_in[layer], q_norm_g[layer], k_norm_g[layer], w_pool_grp[layer],
                   pool_scale[layer], w_pool_up[layer], w_attn_up[layer], w_out[layer], norm2_g[layer],
                   w_router[layer], b_router[layer], w_gate_up[layer], b_gate_up[layer], w_down[layer],
                   b_down[layer])
    return x
```

```python
import functools

import jax
import jax.numpy as jnp
from jax import lax
from jax.experimental import pallas as pl
from jax.experimental.pallas import tpu as pltpu
from jax.experimental.pallas import tpu_sc as plsc

F32 = jnp.float32
BF16 = jnp.bfloat16
U32 = jnp.uint32
I32 = jnp.int32

EPS = 1e-6
POOL_WINDOWS = (2, 4, 8, 16)
POOL_GROUP_DIM = 128
POOL_HALO = 16
SB_HEAD_DIM = 64
TOP_K = 4
SWIGLU_LIMIT = 7.0
SWIGLU_ALPHA = 1.702
EXPERT_BLOCK = 1024
EXPERT_SPLIT = 8
EXPERT_STEP = EXPERT_BLOCK // EXPERT_SPLIT
LANES = 128
PACK_ROWS = 4
ATTN_BLOCK = 128
ATTN_SUB = 2
ATTN_EXIT_BITS = 70.0
LOG2_E = 1.4426950408889634
VMEM_LIMIT = 56 * 1024 * 1024
SC_CHUNK = 128
COMBINE_BUFFERS = 3
MOE_GROUPS = 2


def _dot(a, b):
    return jnp.dot(a, b, preferred_element_type=F32)


def _split_bf16(x):
    hi = x.astype(BF16)
    lo = (x - hi.astype(F32)).astype(BF16)
    return hi, lo


def _pack_rows(v, out_ref):
    half = v.shape[1] // 2
    lo = lax.bitcast_convert_type(v[:, :half].astype(BF16).astype(F32), U32) >> 16
    hi = lax.bitcast_convert_type(v[:, half:].astype(BF16).astype(F32), U32) & jnp.uint32(0xFFFF0000)
    w = lo | hi
    for c in range(PACK_ROWS):
        out_ref[c] = w[:, c * LANES:(c + 1) * LANES]


def _unpack_rows(ref):
    los, his = [], []
    for c in range(PACK_ROWS):
        w = ref[c]
        los.append(lax.bitcast_convert_type(w << 16, F32))
        his.append(lax.bitcast_convert_type(w & jnp.uint32(0xFFFF0000), F32))
    return jnp.concatenate(los + his, axis=1)


def _mixer_in_kernel(x_ref, g1_ref, win_ref, gq_ref, gk_ref, hsum_ref, wgrp_ref, pscale_ref, wpu_ref,
                     q_ref, k_ref, v_ref, p_ref, tail_ref, *, tm, pw, sw):
    i = pl.program_id(1)

    @pl.when(i == 0)
    def _():
        tail_ref[...] = jnp.zeros_like(tail_ref)

    x = x_ref[...]
    ms = jnp.mean(x * x, axis=-1, keepdims=True)
    h = (x * lax.rsqrt(ms + EPS) * g1_ref[...]).astype(BF16)

    def head_norm(t, gain):
        ss = _dot((t * t).astype(BF16), hsum_ref[...])
        return t * lax.rsqrt(ss * (1.0 / SB_HEAD_DIM) + EPS) * gain

    def project_q():
        q = _dot(h, win_ref[:, pw:pw + sw])
        q_ref[...] = (head_norm(q, gq_ref[...]) * (SB_HEAD_DIM ** -0.5 * LOG2_E)).astype(BF16)

    def project_k():
        k = _dot(h, win_ref[:, pw + sw:pw + 2 * sw])
        k_ref[...] = head_norm(k, gk_ref[...]).astype(BF16)

    def project_v():
        v_ref[...] = _dot(h, win_ref[:, pw + 2 * sw:pw + 3 * sw]).astype(BF16)

    d_model = x.shape[1]
    pool_gate = []

    def project_pool_gate():
        pool_gate.append(jax.nn.sigmoid(_dot(h, win_ref[:, pw + 3 * sw:pw + 3 * sw + d_model])))

    u = _dot(h, win_ref[:, 0:pw])
    xx = jnp.concatenate([tail_ref[...], u], axis=0)
    tail_ref[...] = u[tm - POOL_HALO:, :]
    pos = i * tm + lax.broadcasted_iota(I32, (tm, POOL_GROUP_DIM), 0)
    mixed = []
    for (g, w), project in zip(enumerate(POOL_WINDOWS), (project_q, project_k, project_v, project_pool_gate)):
        project()
        s = xx[:, g * POOL_GROUP_DIM:(g + 1) * POOL_GROUP_DIM]
        step = 1
        while step < w:
            s = s + pltpu.roll(s, step, axis=0)
            step *= 2
        count = jnp.minimum(pos + 1, w).astype(F32)
        ug = u[:, g * POOL_GROUP_DIM:(g + 1) * POOL_GROUP_DIM]
        d = s[POOL_HALO:, :] / count - ug
        mixed.append(_dot(d.astype(BF16), wgrp_ref[g]))
    pm = jnp.concatenate(mixed, axis=1) * pscale_ref[...]
    pool_out = _dot(pm.astype(BF16), wpu_ref[...])
    p_ref[...] = (pool_gate[0] * pool_out).astype(BF16)


def _attn_kernel(q_ref, k_hbm, v_hbm, tri_ref, o_ref, kbuf, vbuf, ksem, vsem, *scratch, n_pairs):
    units = [(sub, p) for sub in range(ATTN_SUB) for p in range(n_pairs)]
    n_units = len(units)
    qs, acc, rr = scratch[:n_units], scratch[n_units:2 * n_units], scratch[2 * n_units:]
    bq = ATTN_BLOCK
    batch = pl.program_id(0)
    seq = kbuf.shape[1]

    def kv_copies(b):
        rows = pl.ds(pl.multiple_of(b * seq, seq), seq)
        return (pltpu.make_async_copy(k_hbm.at[rows], kbuf.at[b % 2], ksem.at[b % 2]),
                pltpu.make_async_copy(v_hbm.at[rows], vbuf.at[b % 2], vsem.at[b % 2]))

    @pl.when(pl.program_id(1) == 0)
    def _():
        @pl.when(batch == 0)
        def _():
            for c in kv_copies(batch):
                c.start()

        for c in kv_copies(batch):
            c.wait()

        @pl.when(batch + 1 < pl.num_programs(0))
        def _():
            for c in kv_copies(batch + 1):
                c.start()

    k_ref = kbuf.at[batch % 2]
    v_ref = vbuf.at[batch % 2]
    first_block = pl.program_id(1) * ATTN_SUB
    first_head = lax.broadcasted_iota(I32, (bq, LANES), 1) < SB_HEAD_DIM
    for u, (sub, p) in enumerate(units):
        q2 = q_ref[sub * bq:(sub + 1) * bq, p * LANES:(p + 1) * LANES]
        qs[u][:bq] = jnp.where(first_head, q2, jnp.zeros_like(q2))
        qs[u][bq:] = jnp.where(first_head, jnp.zeros_like(q2), q2)
    row = lax.broadcasted_iota(I32, (2 * bq, bq), 0)
    col = lax.broadcasted_iota(I32, (2 * bq, bq), 1)
    causal = col < (row & (bq - 1))
    contract_last = (((1,), (1,)), ((), ()))

    def softplus(z):
        return jnp.maximum(z, 0.0) + jnp.log2(1.0 + jnp.exp2(-jnp.abs(z)))

    def suffix_sums(sp):
        return _dot(sp.astype(BF16), tri_ref[...])

    def cols(ref, block, p):
        start = pl.multiple_of(jnp.maximum(block, 0) * bq, bq)
        return ref[pl.ds(start, bq), p * LANES:(p + 1) * LANES]

    def scores(u, block):
        return lax.dot_general(qs[u][...], cols(k_ref, block, units[u][1]), contract_last, preferred_element_type=F32)

    diag = [first_block + sub for sub, _ in units]
    z_d, z_n, s_d, s_n = {}, {}, {}, {}

    def stage_scores(u):
        z_d[u] = scores(u, diag[u])
        z_n[u] = scores(u, diag[u] - 1)

    def stage_sums(u):
        s_d[u] = suffix_sums(jnp.where(causal, softplus(z_d[u]), 0.0))
        s_n[u] = suffix_sums(jnp.where(diag[u] >= 1, softplus(z_n[u]), 0.0))

    def stage_values(u):
        p = units[u][1]
        a_d = jnp.where(causal, jnp.exp2(z_d[u] - s_d[u][:, :bq]), 0.0)
        r_d = s_d[u][:, bq:]
        a_n = jnp.where(diag[u] >= 1, jnp.exp2(z_n[u] - (r_d + s_n[u][:, :bq])), 0.0)
        acc[u][...] = (_dot(a_d.astype(BF16), cols(v_ref, diag[u], p)) + _dot(a_n.astype(BF16), cols(v_ref, diag[u] - 1, p)))
        rr[u][...] = r_d + s_n[u][:, bq:]

    for t in range(n_units + 2):
        if t < n_units:
            stage_scores(t)
        if 0 <= t - 1 < n_units:
            stage_sums(t - 1)
        if 0 <= t - 2 < n_units:
            stage_values(t - 2)

    def r_min():
        m = rr[0][...]
        for u in range(1, n_units):
            m = jnp.minimum(m, rr[u][...])
        return jnp.min(m)

    def cond(c):
        back, rm = c
        return jnp.logical_and(diag[-1] - back >= 0, rm < ATTN_EXIT_BITS)

    def body(c):
        back, _ = c
        blocks = [d - back for d in diag]
        zs = [scores(u, blocks[u]) for u in range(n_units)]
        ss = [suffix_sums(jnp.where(blocks[u] >= 0, softplus(zs[u]), 0.0)) for u in range(n_units)]
        for u, (_, p) in enumerate(units):
            r = rr[u][...]
            a = jnp.where(blocks[u] >= 0, jnp.exp2(zs[u] - (r + ss[u][:, :bq])), 0.0)
            acc[u][...] += _dot(a.astype(BF16), cols(v_ref, blocks[u], p))
            rr[u][...] = r + ss[u][:, bq:]
        return back + 1, r_min()

    lax.while_loop(cond, body, (2, r_min()))
    for u, (sub, p) in enumerate(units):
        o_ref[sub * bq:(sub + 1) * bq, p * LANES:(p + 1) * LANES] = (
            jnp.where(first_head, acc[u][:bq], acc[u][bq:]).astype(BF16))


def _mixer_out_kernel(x_ref, p_ref, sba_ref, g1_ref, wga_ref, wau_ref, wout_ref, g2_ref, wr_hi_ref, wr_lo_ref, br_ref,
                      ltri_ref, x1_ref, hp_ref, ri_ref, gate_ref, cnt_ref, logit_ref, *, tm):
    step = pl.program_id(0)

    @pl.when(step == 0)
    def _():
        cnt_ref[...] = jnp.zeros_like(cnt_ref)
        logit_ref[1] = jnp.zeros((tm, LANES), F32)

    logits = logit_ref[(step + 1) % 2]
    routed = jnp.where(step >= 1, 1.0, 0.0)
    lane = lax.broadcasted_iota(I32, logits.shape, 1).astype(F32)
    work = logits
    vals, idxs = [], []

    def topk_round():
        nonlocal work
        m = jnp.max(work, axis=-1, keepdims=True)
        ik = jnp.min(jnp.where(work == m, lane, float(LANES)), axis=-1, keepdims=True)
        vals.append(m)
        idxs.append(ik)
        work = jnp.where(lane == ik, -jnp.inf, work)

    d_model = x_ref.shape[1]
    quarter = d_model // TOP_K
    topk_round()
    x = x_ref[...]
    h1 = (x * lax.rsqrt(jnp.mean(x * x, axis=-1, keepdims=True) + EPS) * g1_ref[...]).astype(BF16)
    attn_gate = jax.nn.sigmoid(_dot(h1, wga_ref[...]))
    attn_out = _dot(sba_ref[...], wau_ref[...])
    merged = (p_ref[...].astype(F32) + attn_gate * attn_out).astype(BF16)
    x1_parts = []
    for c in range(TOP_K):
        if c >= 1:
            topk_round()
        cols = slice(c * quarter, (c + 1) * quarter)
        x1_parts.append(x[:, cols] + _dot(merged, wout_ref[:, cols]))
    x1 = jnp.concatenate(x1_parts, axis=1)
    x1_ref[...] = x1

    es = [jnp.exp(v - vals[0]) for v in vals]
    denom = es[0] + es[1] + es[2] + es[3]
    hot = jnp.zeros(logits.shape, F32)
    for ik in idxs:
        hot = hot + jnp.where(lane == ik, routed, 0.0)
    before = _dot(ltri_ref[...], hot.astype(BF16)) + cnt_ref[0:1, :]

    ms = jnp.mean(x1 * x1, axis=-1, keepdims=True)
    h2 = x1 * lax.rsqrt(ms + EPS) * g2_ref[...]
    _pack_rows(h2, hp_ref)

    ri = jnp.zeros(logits.shape, F32)
    gt = jnp.zeros(logits.shape, F32)
    for kk in range(TOP_K):
        rank = jnp.sum(jnp.where(lane == idxs[kk], before, 0.0), axis=-1, keepdims=True)
        ri = jnp.where(lane == kk, idxs[kk], ri)
        ri = jnp.where(lane == TOP_K + kk, rank, ri)
        gt = jnp.where(lane == kk, es[kk] / denom, gt)
    ri_ref[...] = ri.T[:2 * TOP_K].astype(I32)
    gate_ref[...] = gt
    cnt_ref[...] = cnt_ref[...] + jnp.sum(hot, axis=0, keepdims=True)

    h_hi, h_lo = _split_bf16(h2)
    logit_ref[step % 2] = (_dot(h_hi, wr_hi_ref[...]) + _dot(h_hi, wr_lo_ref[...]) + _dot(h_lo, wr_hi_ref[...])
                           + br_ref[...])


def _dest_kernel(start_ref, ri_ref, dest_ref):
    idx = ri_ref[:TOP_K, :]
    dest = ri_ref[TOP_K:, :]
    for e in range(start_ref.shape[0]):
        dest = dest + jnp.where(idx == e, start_ref[e], 0)
    dest_ref[...] = dest


def _sc_mesh():
    info = plsc.get_sparse_core_info()
    mesh = plsc.VectorSubcoreMesh(core_axis_name="c", subcore_axis_name="s")
    return mesh, info.num_cores, info.num_subcores, info.num_lanes


def _sc_scatter_rows(rows, idx, n_out):
    mesh, nc, ns, lanes = _sc_mesh()
    nw = nc * ns
    planes, m, width = rows.shape
    ch = SC_CHUNK
    n_ch = m // ch // nw
    half = planes // 2
    assert planes % 2 == 0 and n_ch * ch * nw == m and idx.shape == (m // ch * TOP_K, ch) and n_ch >= 2
    assert n_ch % 8 == 0

    @functools.partial(
        pl.kernel, mesh=mesh, out_type=jax.ShapeDtypeStruct((planes * n_out, width), rows.dtype),
        scratch_types=([pltpu.VMEM((n_ch * TOP_K, ch), I32)] + [pltpu.VMEM((ch, width), rows.dtype)] * planes
                       + [pltpu.VMEM((TOP_K, ch), I32)] * planes
                       + [pltpu.SemaphoreType.DMA((planes,)), pltpu.SemaphoreType.DMA((planes,))]))
    def scatter_kernel(rows_hbm, idx_hbm, out_hbm, idx_v, *rest):
        bufs, ibufs, rsem, ssem = rest[:planes], rest[planes:2 * planes], rest[2 * planes], rest[2 * planes + 1]
        wid = lax.axis_index("s") * nc + lax.axis_index("c")
        for kk in range(TOP_K):
            pltpu.sync_copy(idx_hbm.at[pl.ds(kk * (m // ch) + wid * n_ch, n_ch)], idx_v.at[pl.ds(kk * n_ch, n_ch)])
        base = wid * n_ch * ch

        def read(j, b):
            return pltpu.make_async_copy(rows_hbm.at[b, pl.ds(pl.multiple_of(base + j * ch, ch), ch)], bufs[b], rsem.at[b])

        def scatters(b):
            return [pltpu.make_async_copy(bufs[b], out_hbm.at[ibufs[b].at[kk]], ssem.at[b]) for kk in range(TOP_K)]

        def start_scatters(j, b):
            for kk in range(TOP_K):
                for t in range(0, ch, lanes):
                    ibufs[b][kk, pl.ds(t, lanes)] = idx_v[kk * n_ch + j, pl.ds(t, lanes)] + b * n_out
            for c in scatters(b):
                c.start()

        def wait_scatters(b):
            for c in scatters(b):
                c.wait()

        def finish(j, b):
            pj, pb = (j, b - half) if b >= half else (j - 1, b + half)
            read(pj, pb).wait()
            start_scatters(pj, pb)

        for b in range(planes):
            read(0, b).start()
        for b in range(half, planes):
            finish(0, b)

        @pl.loop(1, n_ch)
        def _(j):
            for b in range(planes):
                wait_scatters(b)
                read(j, b).start()
                finish(j, b)

        for b in range(half):
            finish(n_ch, b)
        for b in range(planes):
            wait_scatters(b)

    return scatter_kernel(rows, idx).reshape(planes, n_out, width)


def _padfill_kernel(start_ref, len_ref, xs_in_ref, xs_ref, zeros_ref, sem):
    del xs_in_ref
    zeros_ref[...] = jnp.zeros_like(zeros_ref)
    bits = [1 << b for b in reversed(range((EXPERT_STEP - 1).bit_length()))]

    def pieces(e):
        n = len_ref[e]
        for bit in bits:
            row0 = start_ref[e] + (n & ~(2 * bit - 1))
            copy = pltpu.make_async_copy(zeros_ref.at[:, pl.ds(0, bit)], xs_ref.at[:, pl.ds(row0, bit)], sem)
            yield (n & bit) != 0, copy

    def start(e, c):
        for on, copy in pieces(e):
            pl.when(on)(copy.start)
        return c

    def wait(e, c):
        for on, copy in pieces(e):
            pl.when(on)(copy.wait)
        return c

    lax.fori_loop(0, start_ref.shape[0], start, 0)
    lax.fori_loop(0, start_ref.shape[0], wait, 0)


def _experts_kernel(be_ref, nb_ref, valid_ref, slot_ref, next_ref, xs_ref, wgu_hbm, bgu_ref, wd_hbm, bd_ref, ys_ref,
                    wgu_buf, wd_buf, wgu_bf_ref, wd_bf_ref, gu_sem, d_sem):
    del nb_ref
    blk = pl.program_id(0)
    de = wd_hbm.shape[1]
    valid = valid_ref[blk]
    expert = be_ref[blk]
    slot = slot_ref[blk]

    def weight_copies(e, s):
        return (pltpu.make_async_copy(wgu_hbm.at[e], wgu_buf.at[s], gu_sem.at[s]),
                pltpu.make_async_copy(wd_hbm.at[e], wd_buf.at[s], d_sem.at[s]))

    first_of_expert = jnp.logical_or(blk == 0, expert != be_ref[jnp.maximum(blk - 1, 0)])

    @pl.when(jnp.logical_and(valid > 0, first_of_expert))
    def _():
        @pl.when(blk == 0)
        def _():
            for c in weight_copies(expert, slot):
                c.start()

        for c in weight_copies(expert, slot):
            c.wait()
        wgu_bf_ref[...] = wgu_buf[slot].astype(BF16)
        wd_bf_ref[...] = wd_buf[slot].astype(BF16)

        @pl.when(next_ref[blk] >= 0)
        def _():
            for c in weight_copies(next_ref[blk], 1 - slot):
                c.start()

    def mlp(m):
        x = _unpack_rows(xs_ref.at[:, pl.ds(0, m)]).astype(BF16)
        gu = _dot(x, wgu_bf_ref[...]) + bgu_ref[0]
        glu = jnp.minimum(gu[:, :de], SWIGLU_LIMIT)
        lin = jnp.clip(gu[:, de:], -SWIGLU_LIMIT, SWIGLU_LIMIT)
        act = glu * jax.nn.sigmoid(SWIGLU_ALPHA * glu) * (lin + 1.0)
        y = _dot(act.astype(BF16), wd_bf_ref[...]) + bd_ref[0]
        _pack_rows(y, ys_ref.at[:, pl.ds(0, m)])
        if m < EXPERT_BLOCK:
            ys_ref[:, m:, :] = jnp.zeros((PACK_ROWS, EXPERT_BLOCK - m, LANES), U32)

    for q in range(EXPERT_SPLIT):
        pl.when(jnp.logical_and(valid > q * EXPERT_STEP, valid <= (q + 1) * EXPERT_STEP))(
            functools.partial(mlp, (q + 1) * EXPERT_STEP))


def _sc_gather_rows(table, idx):
    mesh, nc, ns, lanes = _sc_mesh()
    nw = nc * ns
    planes, n_tab, width = table.shape
    n_idx_rows, ch = idx.shape
    n_ch = n_idx_rows // nw
    m = n_idx_rows * ch
    half = planes // 2
    assert planes % 2 == 0 and ch == SC_CHUNK and n_ch * nw == n_idx_rows and n_ch >= 2

    @functools.partial(
        pl.kernel, mesh=mesh, out_type=jax.ShapeDtypeStruct((planes, m, width), table.dtype),
        scratch_types=([pltpu.VMEM((n_ch, ch), I32)] + [pltpu.VMEM((ch, width), table.dtype)] * planes
                       + [pltpu.VMEM((8, ch), I32)] * planes
                       + [pltpu.SemaphoreType.DMA((planes,)), pltpu.SemaphoreType.DMA((planes,))]))
    def gather_kernel(table_hbm, idx_hbm, out_hbm, idx_v, *rest):
        bufs, ibufs, gsem, wsem = rest[:planes], rest[planes:2 * planes], rest[2 * planes], rest[2 * planes + 1]
        wid = lax.axis_index("s") * nc + lax.axis_index("c")
        pltpu.sync_copy(idx_hbm.at[pl.ds(wid * n_ch, n_ch)], idx_v)
        base = wid * n_ch * ch

        def gather(b):
            return pltpu.make_async_copy(table_hbm.at[ibufs[b].at[0]], bufs[b], gsem.at[b])

        def start_gather(j, b):
            for t in range(0, ch, lanes):
                ibufs[b][0, pl.ds(t, lanes)] = idx_v[j, pl.ds(t, lanes)] + b * n_tab
            gather(b).start()

        def write(j, b):
            return pltpu.make_async_copy(bufs[b], out_hbm.at[b, pl.ds(pl.multiple_of(base + j * ch, ch), ch)], wsem.at[b])

        def finish(j, b):
            pj, pb = (j, b - half) if b >= half else (j - 1, b + half)
            gather(pb).wait()
            write(pj, pb).start()

        for b in range(planes):
            start_gather(0, b)
        for b in range(half, planes):
            finish(0, b)

        @pl.loop(1, n_ch)
        def _(j):
            for b in range(planes):
                write(j - 1, b).wait()
                start_gather(j, b)
                finish(j, b)

        for b in range(half):
            finish(n_ch, b)
        for b in range(planes):
            write(n_ch - 1, b).wait()

    return gather_kernel(table.reshape(planes * n_tab, width), idx)


def _combine_kernel(x1_ref, gate_ref, y0_ref, y1_ref, y2_ref, y3_ref, *rest):
    o_ref = rest[-1]
    gate = gate_ref[...]
    out = x1_ref[...]
    for kk, y_ref in enumerate((y0_ref, y1_ref, y2_ref, y3_ref)):
        out = out + gate[:, kk:kk + 1] * _unpack_rows(y_ref)
    o_ref[...] = out


def _combine_stream_kernel(x1_hbm, gate_hbm, y4_hbm, *rest, steps, first, tm):
    deep = pl.Buffered(COMBINE_BUFFERS)
    width = x1_hbm.shape[1]
    in_specs = ([pl.BlockSpec((tm, width), lambda i: (i, 0), pipeline_mode=deep),
                 pl.BlockSpec((tm, LANES), lambda i: (i, 0), pipeline_mode=deep)]
                + [pl.BlockSpec((PACK_ROWS, tm, LANES), lambda i, kk=kk: (0, kk * steps + i, 0), pipeline_mode=deep)
                   for kk in range(TOP_K)])
    out_specs = [pl.BlockSpec((tm, width), lambda i: (first + i, 0))]
    pltpu.emit_pipeline(_combine_kernel, grid=(steps,), in_specs=in_specs, out_specs=out_specs)(
        x1_hbm, gate_hbm, y4_hbm, y4_hbm, y4_hbm, y4_hbm, rest[-1])


def _const_spec(shape):
    nd = len(shape)
    return pl.BlockSpec(shape, lambda *_: (0,) * nd)


def _layer(x, norm1_g, w_in, q_norm_g, k_norm_g, w_pool_grp, pool_scale, w_pool_up, w_attn_up, w_out, norm2_g,
           w_router, b_router, w_gate_up, b_gate_up, w_down, b_down):
    B, S, D = x.shape
    N = B * S
    pw = w_pool_up.shape[0]
    sw = w_attn_up.shape[0]
    n_exp = w_router.shape[1]
    de = w_down.shape[1]
    heads = sw // SB_HEAD_DIM
    assert pw == len(POOL_WINDOWS) * POOL_GROUP_DIM and heads % 2 == 0 and n_exp <= LANES
    assert D == 2 * PACK_ROWS * LANES and w_in.shape[1] == pw + 3 * sw + 2 * D
    tm = 512 if S % 512 == 0 else 256
    assert S % tm == 0 and S % (ATTN_SUB * ATTN_BLOCK) == 0
    xf = x.reshape(N, D)
    cparams = functools.partial(pltpu.CompilerParams, vmem_limit_bytes=VMEM_LIMIT)

    hsum = (jnp.arange(sw)[:, None] // SB_HEAD_DIM == jnp.arange(sw)[None, :] // SB_HEAD_DIM).astype(BF16)
    tm_in = 2 * tm if S % (2 * tm) == 0 else tm
    nt = S // tm_in
    tok_spec = lambda w: pl.BlockSpec((tm_in, w), lambda b, i: (b * nt + i, 0))
    w_in_bf = w_in.astype(BF16)
    n_in = pw + 3 * sw + D
    g1 = norm1_g.reshape(1, D)
    q2, k2, v2, pg = pl.pallas_call(
        functools.partial(_mixer_in_kernel, tm=tm_in, pw=pw, sw=sw),
        grid=(B, nt),
        in_specs=[tok_spec(D), _const_spec((1, D)), _const_spec((D, n_in)), _const_spec((1, sw)),
                  _const_spec((1, sw)), _const_spec((sw, sw)),
                  _const_spec((len(POOL_WINDOWS), POOL_GROUP_DIM, POOL_GROUP_DIM)), _const_spec((1, pw)),
                  _const_spec((pw, D))],
        out_specs=[tok_spec(sw), tok_spec(sw), tok_spec(sw), tok_spec(D)],
        out_shape=[jax.ShapeDtypeStruct((N, sw), BF16)] * 3 + [jax.ShapeDtypeStruct((N, D), BF16)],
        scratch_shapes=[pltpu.VMEM((POOL_HALO, pw), F32)],
        compiler_params=cparams(dimension_semantics=("arbitrary", "arbitrary")),
        name="mixer_in",
    )(xf, g1, w_in_bf[:, :n_in], jnp.tile(q_norm_g, heads).reshape(1, sw),
      jnp.tile(k_norm_g, heads).reshape(1, sw), hsum, w_pool_grp.astype(BF16), pool_scale.reshape(1, pw),
      w_pool_up.astype(BF16))

    bq = ATTN_BLOCK
    rows_q = ATTN_SUB * bq
    nq = S // rows_q
    jj = jnp.arange(bq)
    tri = jnp.concatenate([(jj[:, None] >= jj[None, :]).astype(BF16), jnp.ones((bq, bq), BF16)], axis=1)
    n_pairs = heads // 2
    n_units = ATTN_SUB * n_pairs
    kv_spec = pl.BlockSpec(memory_space=pl.ANY)
    sba = pl.pallas_call(
        functools.partial(_attn_kernel, n_pairs=n_pairs),
        grid=(B, nq),
        in_specs=[pl.BlockSpec((rows_q, sw), lambda b, qi: (b * nq + qi, 0)), kv_spec, kv_spec,
                  _const_spec((bq, 2 * bq))],
        out_specs=pl.BlockSpec((rows_q, sw), lambda b, qi: (b * nq + qi, 0)),
        out_shape=jax.ShapeDtypeStruct((N, sw), BF16),
        scratch_shapes=([pltpu.VMEM((2, S, sw), BF16), pltpu.VMEM((2, S, sw), BF16),
                         pltpu.SemaphoreType.DMA((2,)), pltpu.SemaphoreType.DMA((2,))]
                        + [pltpu.VMEM((2 * bq, LANES), BF16)] * n_units + [pltpu.VMEM((2 * bq, LANES), F32)] * (2 * n_units)),
        compiler_params=cparams(dimension_semantics=("arbitrary", "arbitrary")),
        name="sb_attn",
    )(q2, k2, v2, tri)

    wr = jnp.zeros((D, LANES), F32).at[:, :n_exp].set(w_router)
    wr_hi = wr.astype(BF16)
    wr_lo = (wr - wr_hi.astype(F32)).astype(BF16)
    br = jnp.full((1, LANES), -jnp.inf, F32).at[0, :n_exp].set(b_router)
    ltri = (jnp.arange(tm)[:, None] > jnp.arange(tm)[None, :]).astype(BF16)
    assert N % (MOE_GROUPS * tm) == 0
    ng = N // MOE_GROUPS
    steps = ng // tm
    n_assign = ng * TOP_K
    n_blocks = -(-(n_assign + n_exp * (EXPERT_BLOCK - 1)) // EXPERT_BLOCK)
    n_rows = n_blocks * EXPERT_BLOCK
    plane_spec = lambda rows, index: pl.BlockSpec((PACK_ROWS, rows, LANES), lambda i, *_: (0, index(i, *_), 0))
    row_spec = lambda w: pl.BlockSpec((tm, w), lambda i: (i, 0))
    w_attn_up_bf, w_out_bf, g2 = w_attn_up.astype(BF16), w_out.astype(BF16), norm2_g.reshape(1, D)
    bgu, bdn = b_gate_up.reshape(n_exp, 1, 2 * de), b_down.reshape(n_exp, 1, D)
    out = None
    for grp in range(MOE_GROUPS):
        grp_spec = lambda w, first=grp * steps: pl.BlockSpec((tm, w), lambda i: (first + i, 0))

        cur = lambda i: jnp.minimum(i, steps - 1)
        prev = lambda i: jnp.maximum(i - 1, 0)
        in_spec = lambda w, first=grp * steps: pl.BlockSpec((tm, w), lambda i: (first + cur(i), 0))
        x1, hpk, ri, gate, cnt = pl.pallas_call(
            functools.partial(_mixer_out_kernel, tm=tm),
            grid=(steps + 1,),
            in_specs=[in_spec(D), in_spec(D), in_spec(sw), _const_spec((1, D)), _const_spec((D, D)),
                      _const_spec((sw, D)), _const_spec((D, D)),
                      _const_spec((1, D)), _const_spec((D, LANES)), _const_spec((D, LANES)), _const_spec((1, LANES)),
                      _const_spec((tm, tm))],
            out_specs=[pl.BlockSpec((tm, D), lambda i: (cur(i), 0)), plane_spec(tm, cur),
                       pl.BlockSpec((2 * TOP_K, tm), lambda i: (0, prev(i))),
                       pl.BlockSpec((tm, LANES), lambda i: (prev(i), 0)), _const_spec((8, LANES))],
            out_shape=[jax.ShapeDtypeStruct((ng, D), F32), jax.ShapeDtypeStruct((PACK_ROWS, ng, LANES), U32),
                       jax.ShapeDtypeStruct((2 * TOP_K, ng), I32), jax.ShapeDtypeStruct((ng, LANES), F32),
                       jax.ShapeDtypeStruct((8, LANES), F32)],
            scratch_shapes=[pltpu.VMEM((2, tm, LANES), F32)],
            compiler_params=cparams(dimension_semantics=("arbitrary",)),
            name="mixer_out",
        )(xf, pg, sba, g1, w_in_bf[:, n_in:], w_attn_up_bf, w_out_bf, g2, wr_hi, wr_lo, br, ltri)

        counts = cnt[0, :n_exp].astype(I32)
        padded = (counts + EXPERT_BLOCK - 1) // EXPERT_BLOCK * EXPERT_BLOCK
        padded_end = jnp.cumsum(padded)
        start_pad = padded_end - padded
        td = min(ng, 8192)
        dest = pl.pallas_call(
            _dest_kernel,
            grid_spec=pltpu.PrefetchScalarGridSpec(
                num_scalar_prefetch=1,
                grid=(ng // td,),
                in_specs=[pl.BlockSpec((2 * TOP_K, td), lambda i, sp: (0, i))],
                out_specs=pl.BlockSpec((TOP_K, td), lambda i, sp: (0, i)),
            ),
            out_shape=jax.ShapeDtypeStruct((TOP_K, ng), I32),
            compiler_params=cparams(dimension_semantics=("arbitrary",)),
            name="dest",
        )(start_pad, ri).reshape(-1, SC_CHUNK)
        block_start = jnp.arange(n_blocks, dtype=I32) * EXPERT_BLOCK
        block_expert = jnp.minimum(jnp.sum((padded_end[None, :] <= block_start[:, None]).astype(I32), axis=1),
                                   n_exp - 1)
        n_used = (padded_end[-1] // EXPERT_BLOCK).astype(I32).reshape(1)

        xs = _sc_scatter_rows(hpk, dest, n_rows)
        xs = pl.pallas_call(
            _padfill_kernel,
            grid_spec=pltpu.PrefetchScalarGridSpec(
                num_scalar_prefetch=2,
                grid=(1,),
                in_specs=[pl.BlockSpec(memory_space=pl.ANY)],
                out_specs=pl.BlockSpec(memory_space=pl.ANY),
                scratch_shapes=[pltpu.VMEM((PACK_ROWS, 1 << ((EXPERT_STEP - 1).bit_length() - 1), LANES), U32),
                                pltpu.SemaphoreType.DMA(())],
            ),
            out_shape=jax.ShapeDtypeStruct((PACK_ROWS, n_rows, LANES), U32),
            input_output_aliases={2: 0},
            compiler_params=cparams(dimension_semantics=("arbitrary",)),
            name="padfill",
        )(start_pad + counts, (-counts) % EXPERT_STEP, xs)

        experts = jnp.arange(n_exp, dtype=I32)
        used = counts > 0
        block_valid = jnp.clip(jnp.sum(jnp.where(block_expert[:, None] == experts[None, :],
                                                 (start_pad + counts)[None, :], 0), axis=1) - block_start,
                               0, EXPERT_BLOCK)
        ordinal = jnp.cumsum(used.astype(I32)) - 1
        next_used = jnp.min(jnp.where(jnp.logical_and(experts[None, :] > experts[:, None], used[None, :]),
                                      experts[None, :], n_exp), axis=1)
        next_used = jnp.where(next_used == n_exp, -1, next_used)
        per_block = lambda table: jnp.sum(jnp.where(block_expert[:, None] == experts[None, :], table[None, :], 0), axis=1)
        b_spec = lambda shape: pl.BlockSpec(shape, lambda i, be, *_: (be[i], 0, 0))
        ys = pl.pallas_call(
            _experts_kernel,
            grid_spec=pltpu.PrefetchScalarGridSpec(
                num_scalar_prefetch=5,
                grid=(n_blocks,),
                in_specs=[plane_spec(EXPERT_BLOCK, lambda i, be, nb, *_: jnp.minimum(i, nb[0] - 1)),
                          pl.BlockSpec(memory_space=pl.ANY), b_spec((1, 1, 2 * de)),
                          pl.BlockSpec(memory_space=pl.ANY), b_spec((1, 1, D))],
                out_specs=plane_spec(EXPERT_BLOCK, lambda i, be, nb, *_: jnp.minimum(i, nb[0] - 1)),
                scratch_shapes=[pltpu.VMEM((2, D, 2 * de), F32), pltpu.VMEM((2, de, D), F32),
                                pltpu.VMEM((D, 2 * de), BF16), pltpu.VMEM((de, D), BF16),
                                pltpu.SemaphoreType.DMA((2,)), pltpu.SemaphoreType.DMA((2,))],
            ),
            out_shape=jax.ShapeDtypeStruct((PACK_ROWS, n_rows, LANES), U32),
            compiler_params=cparams(dimension_semantics=("arbitrary",)),
            name="experts",
        )(block_expert, n_used, block_valid, per_block(ordinal % 2), per_block(next_used), xs, w_gate_up, bgu, w_down, bdn)

        y4 = _sc_gather_rows(ys, dest.reshape(-1, SC_CHUNK))

        operands = [x1, gate, y4] + ([] if out is None else [out])
        out = pl.pallas_call(
            functools.partial(_combine_stream_kernel, steps=steps, first=grp * steps, tm=tm),
            in_specs=[pl.BlockSpec(memory_space=pl.ANY)] * len(operands),
            out_specs=pl.BlockSpec(memory_space=pl.ANY),
            out_shape=jax.ShapeDtypeStruct((N, D), F32),
            input_output_aliases={} if out is None else {len(operands) - 1: 0},
            compiler_params=cparams(),
            name="combine",
        )(*operands)
    return out.reshape(B, S, D)


def kernel(x, norm1_g, w_in, q_norm_g, k_norm_g, w_pool_grp, pool_scale, w_pool_up, w_attn_up, w_out, norm2_g,
           w_router, b_router, w_gate_up, b_gate_up, w_down, b_down):
    for layer in range(norm1_g.shape[0]):
        x = _layer(x, norm1_g[layer], w_in[layer], q_norm_g[layer], k_norm_g[layer], w_pool_grp[layer],
                   pool_scale[layer], w_pool_up[layer], w_attn_up[layer], w_out[layer], norm2_g[layer],
                   w_router[layer], b_router[layer], w_gate_up[layer], b_gate_up[layer], w_down[layer],
                   b_down[layer])
    return x
```
